```python
import math
import functools
import jax
import jax.numpy as jnp
from jax import lax
import numpy as np

D_MODEL = 1024
BATCH = 8
SEQ = 4096
DEPTH = 4

GRID_W = 64
CTX_LEN = 256
POS_BASE = 10000.0
EPS = 1e-6

GLA_HEADS = 4
GLA_DK = 48
GLA_DV = 96
GLA_RANK = 16
GLA_GATE_NORM = 16.0
GLA_CHUNK = 64

S5_CH = 256
S5_GROUP = 16
S5_GROUPS = S5_CH // S5_GROUP
S5_STATE = 64
S5_DT_MIN = 1e-3
S5_DT_MAX = 1e-1

ML_HEADS = 4
ML_DH = 96
ML_CHUNK = 64
ML_CONV = 3

D_GLA = GLA_HEADS * GLA_DV
D_ML = ML_HEADS * ML_DH
D_MIX = D_GLA + S5_CH + D_ML

D_FF = 2816
N_EXPERTS = 8
TOP_K = 2
D_FF_EXPERT = 2816
MOE_BLOCK = 256

PROJ_SPLITS = (GLA_HEADS * GLA_DK, GLA_HEADS * GLA_DK, D_GLA, GLA_RANK, D_GLA,
               S5_CH,
               D_ML, D_ML, D_ML, D_ML, 4 * ML_HEADS)
D_PROJ = sum(PROJ_SPLITS)

kernel_name = 'hybrid_gla_s5_mlstm_moe_prefix_dit'


def rmsnorm(x, g):
    xf = x.astype(jnp.float32)
    y = xf * lax.rsqrt(jnp.mean(xf * xf, axis=-1, keepdims=True) + EPS)
    return (y * g.astype(jnp.float32)).astype(x.dtype)


def modulation(cond, w, b):
    m = (jax.nn.silu(cond) @ w + b)[..., None, :]
    return jnp.split(m, 6, axis=-1)


def modulate(h, shift, scale):
    return h * (1.0 + scale) + shift


def pos_embed_2d(n_tokens, dtype):
    rows = n_tokens // GRID_W
    row, col = jnp.meshgrid(jnp.arange(rows, dtype=jnp.float32),
                            jnp.arange(GRID_W, dtype=jnp.float32), indexing='ij')
    n_freq = D_MODEL // 4
    omega = jnp.exp(-math.log(POS_BASE) * jnp.arange(n_freq, dtype=jnp.float32) / n_freq)

    def axis_embed(p):
        ang = p.reshape(-1, 1) * omega
        return jnp.concatenate([jnp.sin(ang), jnp.cos(ang)], axis=-1)

    return jnp.concatenate([axis_embed(row), axis_embed(col)], axis=-1).astype(dtype)


def split_heads(t, n):
    b, l, _ = t.shape
    return t.reshape(b, l, n, -1).transpose(0, 2, 1, 3)


def merge_heads(t):
    b, n, l, d = t.shape
    return t.transpose(0, 2, 1, 3).reshape(b, l, n * d)


def to_chunks(t, size):
    b, h, l = t.shape[:3]
    t = t.reshape((b, h, l // size, size) + t.shape[3:])
    return jnp.moveaxis(t, 2, 0)


def from_chunks(t):
    t = jnp.moveaxis(t, 0, 2)
    return t.reshape((t.shape[0], t.shape[1], -1) + t.shape[4:])


def _flip_if(t, rev, axis):
    return jnp.flip(t, axis) if rev else t


def run_bidirectional(scan_fns, ctx_dirs, lat_dirs, init_state, axis):
    out_c, out_l = [], []
    for d, scan_fn in enumerate(scan_fns):
        rev = d == 1
        o_c, s_ctx = scan_fn(*[_flip_if(t, rev, axis) for t in ctx_dirs[d]], init_state)
        o_l, _ = scan_fn(*[_flip_if(t, rev, axis) for t in lat_dirs[d]], s_ctx)
        out_c.append(_flip_if(o_c, rev, axis))
        out_l.append(_flip_if(o_l, rev, axis))
    return out_c[0] + out_c[1], out_l[0] + out_l[1]


def gla_scan(q, k, v, log_a, s0):
    qc, kc, vc, ac = (to_chunks(t, GLA_CHUNK) for t in (q, k, v, log_a))
    b = jnp.cumsum(ac, axis=-2)
    b_last = b[..., -1:, :]
    q_in = qc * jnp.exp(b)
    k_in = kc * jnp.exp(-b)
    k_out = kc * jnp.exp(b_last - b)
    lower = jnp.tril(jnp.ones((GLA_CHUNK, GLA_CHUNK), bool))
    att = jnp.where(lower, jnp.einsum('nbhtd,nbhsd->nbhts', q_in, k_in), 0.0)
    o_intra = jnp.einsum('nbhts,nbhsv->nbhtv', att, vc)

    def step(s, xs):
        qi, ki, vi, di = xs
        o = jnp.einsum('bhtd,bhdv->bhtv', qi, s)
        s = s * di[..., None] + jnp.einsum('bhsd,bhsv->bhdv', ki, vi)
        return s, o

    s_fin, o_inter = lax.scan(step, s0, (q_in, k_out, vc, jnp.exp(b_last[..., 0, :])))
    return from_chunks(o_intra + o_inter), s_fin


def gla_mixer(p_c, p_l, wa2, ba, gain):
    def dirs(p):
        q, k, v, lr, _ = p
        qh = split_heads(q * GLA_DK ** -0.5, GLA_HEADS)
        kh, vh = split_heads(k, GLA_HEADS), split_heads(v, GLA_HEADS)
        return [(qh, kh, vh,
                 split_heads(jax.nn.log_sigmoid(lr @ wa2[d] + ba[d]) / GLA_GATE_NORM, GLA_HEADS))
                for d in (0, 1)]

    bsz = p_c[0].shape[0]
    s0 = jnp.zeros((bsz, GLA_HEADS, GLA_DK, GLA_DV), jnp.float32)
    o_c, o_l = run_bidirectional((gla_scan, gla_scan), dirs(p_c), dirs(p_l), s0, axis=2)

    def out(o, g):
        return merge_heads(rmsnorm(o, gain)) * jax.nn.silu(g)

    return out(o_c, p_c[4]), out(o_l, p_l[4])


def s5_discretise(lam_re, lam_im, log_dt, b_re, b_im):
    dt = jnp.exp(log_dt)[:, None]
    mag = jnp.exp(lam_re * dt)
    abar_re, abar_im = mag * jnp.cos(lam_im * dt), mag * jnp.sin(lam_im * dt)
    den = lam_re * lam_re + lam_im * lam_im
    pr, pi = abar_re - 1.0, abar_im
    coef_re = (pr * lam_re + pi * lam_im) / den
    coef_im = (pi * lam_re - pr * lam_im) / den
    bbar_re = coef_re[..., None] * b_re - coef_im[..., None] * b_im
    bbar_im = coef_re[..., None] * b_im + coef_im[..., None] * b_re
    return abar_re, abar_im, bbar_re, bbar_im


def _complex_affine_combine(e1, e2):
    a1r, a1i, b1r, b1i = e1
    a2r, a2i, b2r, b2i = e2
    return (a1r * a2r - a1i * a2i, a1r * a2i + a1i * a2r,
            a2r * b1r - a2i * b1i + b2r, a2r * b1i + a2i * b1r + b2i)


def s5_scan(u, state, params):
    abar_re, abar_im, bbar_re, bbar_im, c_re, c_im = params
    bu_re = jnp.einsum('blgp,gnp->blgn', u, bbar_re)
    bu_im = jnp.einsum('blgp,gnp->blgn', u, bbar_im)
    s0_re, s0_im = state
    bu_re = bu_re.at[:, 0].add(abar_re * s0_re - abar_im * s0_im)
    bu_im = bu_im.at[:, 0].add(abar_re * s0_im + abar_im * s0_re)
    a_re = jnp.broadcast_to(abar_re, bu_re.shape)
    a_im = jnp.broadcast_to(abar_im, bu_im.shape)
    _, _, s_re, s_im = lax.associative_scan(_complex_affine_combine, (a_re, a_im, bu_re, bu_im), axis=1)
    y = jnp.einsum('blgn,gpn->blgp', s_re, c_re) - jnp.einsum('blgn,gpn->blgp', s_im, c_im)
    return y, (s_re[:, -1], s_im[:, -1])


def s5_mixer(u_c, u_l, lam_re, lam_im, log_dt, b_re, b_im, c_re, c_im, d_skip, glu_w, glu_b):
    f32 = lambda t: t.astype(jnp.float32)
    grp = lambda u: u.reshape(u.shape[0], u.shape[1], S5_GROUPS, S5_GROUP)
    scans = [functools.partial(
        s5_scan,
        params=s5_discretise(f32(lam_re[d]), f32(lam_im[d]), f32(log_dt[d]), f32(b_re), f32(b_im))
        + (f32(c_re), f32(c_im))) for d in (0, 1)]
    bsz = u_c.shape[0]
    zero = jnp.zeros((bsz, S5_GROUPS, S5_STATE), jnp.float32)
    y_c, y_l = run_bidirectional(scans, [(grp(u_c),)] * 2, [(grp(u_l),)] * 2, (zero, zero), axis=1)

    def glu(y, u):
        z = jax.nn.gelu(y.reshape(u.shape) + f32(d_skip) * u)
        return z * jax.nn.sigmoid(z @ f32(glu_w) + f32(glu_b))

    return glu(y_c, u_c), glu(y_l, u_l)


def mlstm_scan(q, k, v, log_i, log_f, state):
    qc, kc, vc, ic, fc = (to_chunks(t, ML_CHUNK) for t in (q, k, v, log_i, log_f))
    lower = jnp.tril(jnp.ones((ML_CHUNK, ML_CHUNK), bool))

    def step(carry, xs):
        s, n, m = carry
        qi, ki, vi, li, lf = xs
        fcum = jnp.cumsum(lf, axis=-1)
        d_log = jnp.where(lower, fcum[..., :, None] - fcum[..., None, :] + li[..., None, :], -jnp.inf)
        prev_log = fcum + m[..., None]
        m_t = jnp.maximum(prev_log, d_log.max(-1))
        w = jnp.exp(d_log - m_t[..., None])
        w_prev = jnp.exp(prev_log - m_t)
        qk = jnp.einsum('bhtd,bhsd->bhts', qi, ki) * w
        num = w_prev[..., None] * jnp.einsum('bhtd,bhdv->bhtv', qi, s) + jnp.einsum('bhts,bhsv->bhtv', qk, vi)
        den = w_prev * jnp.einsum('bhtd,bhd->bht', qi, n) + qk.sum(-1)
        h = num / jnp.maximum(jnp.abs(den), jnp.exp(-m_t))[..., None]
        m_new = m_t[..., -1]
        w_s = jnp.exp(fcum[..., -1:] - fcum + li - m_new[..., None])
        g = jnp.exp(fcum[..., -1] + m - m_new)
        s = g[..., None, None] * s + jnp.einsum('bhs,bhsd,bhsv->bhdv', w_s, ki, vi)
        n = g[..., None] * n + jnp.einsum('bhs,bhsd->bhd', w_s, ki)
        return (s, n, m_new), h

    state, h = lax.scan(step, state, (qc, kc, vc, ic, fc))
    return from_chunks(h), state


def short_conv(x, w, b):
    y = lax.conv_general_dilated(x, w.astype(x.dtype)[:, None, :], window_strides=(1,), padding='SAME',
                                 dimension_numbers=('NWC', 'WIO', 'NWC'),
                                 feature_group_count=x.shape[-1])
    return jax.nn.silu(y + b)


def mlstm_mixer(p_c, p_l, conv_w, conv_b, gate_b, gain):
    def dirs(p):
        q, k, v, _, gates = p
        qk = short_conv(jnp.concatenate([q, k], axis=-1), conv_w, conv_b)
        q, k = jnp.split(qk, 2, axis=-1)
        bsz, l, _ = q.shape
        g = (gates.reshape(bsz, l, 4, ML_HEADS) + gate_b).transpose(2, 0, 3, 1)
        qh = split_heads(q, ML_HEADS)
        kh = split_heads(k, ML_HEADS) * ML_DH ** -0.5
        vh = split_heads(v, ML_HEADS)
        return [(qh, kh, vh, g[0], jax.nn.log_sigmoid(g[1])),
                (qh, kh, vh, g[2], jax.nn.log_sigmoid(g[3]))]

    bsz = p_c[0].shape[0]
    init = (jnp.zeros((bsz, ML_HEADS, ML_DH, ML_DH), jnp.float32),
            jnp.zeros((bsz, ML_HEADS, ML_DH), jnp.float32),
            jnp.zeros((bsz, ML_HEADS), jnp.float32))
    h_c, h_l = run_bidirectional((mlstm_scan, mlstm_scan), dirs(p_c), dirs(p_l), init, axis=2)

    def out(h, o):
        return merge_heads(rmsnorm(h, gain)) * jax.nn.sigmoid(o)

    return out(h_c, p_c[3]), out(h_l, p_l[3])


def token_mixers(a_c, a_l, w_in, w_out, gla_wa2, gla_ba, gla_norm,
                 s5_lam_re, s5_lam_im, s5_log_dt, s5_b_re, s5_b_im, s5_c_re, s5_c_im, s5_d,
                 s5_glu_w, s5_glu_b, ml_conv_w, ml_conv_b, ml_gate_b, ml_norm):
    cuts = np.cumsum(PROJ_SPLITS)[:-1]
    p_c = jnp.split((a_c @ w_in).astype(jnp.float32), cuts, axis=-1)
    p_l = jnp.split((a_l @ w_in).astype(jnp.float32), cuts, axis=-1)
    gla = gla_mixer(p_c[0:5], p_l[0:5], gla_wa2, gla_ba, gla_norm)
    s5 = s5_mixer(p_c[5], p_l[5], s5_lam_re, s5_lam_im, s5_log_dt, s5_b_re, s5_b_im,
                  s5_c_re, s5_c_im, s5_d, s5_glu_w, s5_glu_b)
    ml = mlstm_mixer(p_c[6:11], p_l[6:11], ml_conv_w, ml_conv_b, ml_gate_b, ml_norm)
    out_c = jnp.concatenate([gla[0], s5[0], ml[0]], axis=-1).astype(a_c.dtype) @ w_out
    out_l = jnp.concatenate([gla[1], s5[1], ml[1]], axis=-1).astype(a_l.dtype) @ w_out
    return out_c, out_l


def swiglu(x, w1, w3, w2):
    return (jax.nn.silu(x @ w1) * (x @ w3)) @ w2


def moe_ffn(x, w_router, w1, w3, w2):
    n_tok = x.shape[0]
    logits = (x @ w_router).astype(jnp.float32)
    top_logit, top_idx = lax.top_k(logits, TOP_K)
    top_w = jax.nn.softmax(top_logit, axis=-1)
    flat_e = top_idx.reshape(-1)
    n_assign = n_tok * TOP_K
    order = jnp.argsort(flat_e)
    sorted_e = flat_e[order]
    counts = jnp.bincount(flat_e, length=N_EXPERTS)
    padded = (counts + MOE_BLOCK - 1) // MOE_BLOCK * MOE_BLOCK
    start = jnp.cumsum(counts) - counts
    pend = jnp.cumsum(padded)
    pstart = pend - padded
    dest = pstart[sorted_e] + jnp.arange(n_assign) - start[sorted_e]
    n_blocks = -(-n_assign // MOE_BLOCK) + N_EXPERTS
    n_rows = n_blocks * MOE_BLOCK
    row_token = jnp.zeros((n_rows,), jnp.int32).at[dest].set((order // TOP_K).astype(jnp.int32))
    row_w = jnp.zeros((n_rows,), jnp.float32).at[dest].set(top_w.reshape(-1)[order])
    block_expert = jnp.minimum(
        jnp.searchsorted(pend, jnp.arange(n_blocks) * MOE_BLOCK, side='right'), N_EXPERTS - 1)
    xb = x[row_token].reshape(n_blocks, MOE_BLOCK, -1)

    def expert_block(args):
        xi, e = args
        return swiglu(xi, w1[e], w3[e], w2[e])

    yb = lax.map(expert_block, (xb, block_expert))
    y_rows = (yb.reshape(n_rows, -1).astype(jnp.float32) * row_w[:, None]).astype(x.dtype)
    return jnp.zeros_like(x).at[row_token].add(y_rows)


def channel_mixer(i, t, ffn_w1, ffn_w3, ffn_w2, moe_router, moe_w1, moe_w3, moe_w2):
    j = i // 2
    if i % 2 == 0:
        return swiglu(t, ffn_w1[j], ffn_w3[j], ffn_w2[j])
    return moe_ffn(t, moe_router[j], moe_w1[j], moe_w3[j], moe_w2[j])


def setup_inputs(seed: int = 0) -> dict:
    key = jax.random.key(seed)
    keys = iter(jax.random.split(key, 48))

    def normal(shape, scale):
        return scale * jax.random.normal(next(keys), shape, jnp.float32)

    d = D_MODEL
    n_dense, n_moe = (DEPTH + 1) // 2, DEPTH // 2
    g_, n_, p_ = S5_GROUPS, S5_STATE, S5_GROUP
    ml_gate_b = normal((DEPTH, 4, ML_HEADS), 0.1)
    ml_gate_b = ml_gate_b.at[:, 1::2].add(jnp.linspace(3.0, 6.0, ML_HEADS))
    return {
        'x': normal((BATCH, SEQ, d), 1.0),
        'c': normal((BATCH, d), 1.0),
        'ctx': normal((BATCH, CTX_LEN, d), 1.0),
        'c_ctx': normal((d,), 1.0),
        'w_ada': normal((DEPTH, d, 6 * d), 0.5 * d ** -0.5),
        'b_ada': normal((DEPTH, 6 * d), 0.02),
        'norm1': 1.0 + normal((DEPTH, d), 0.01),
        'norm2': 1.0 + normal((DEPTH, d), 0.01),
        'w_in': normal((DEPTH, d, D_PROJ), d ** -0.5),
        'w_out': normal((DEPTH, D_MIX, d), D_MIX ** -0.5),
        'gla_wa2': normal((DEPTH, 2, GLA_RANK, GLA_HEADS * GLA_DK), GLA_RANK ** -0.5),
        'gla_ba': normal((DEPTH, 2, GLA_HEADS * GLA_DK), 0.1),
        'gla_norm': 1.0 + normal((DEPTH, GLA_DV), 0.01),
        's5_lam_re': -0.5 + normal((DEPTH, 2, g_, n_), 0.01),
        's5_lam_im': jnp.pi * jnp.arange(n_, dtype=jnp.float32) + normal((DEPTH, 2, g_, n_), 0.01),
        's5_log_dt': jax.random.uniform(next(keys), (DEPTH, 2, g_), jnp.float32,
                                        math.log(S5_DT_MIN), math.log(S5_DT_MAX)),
        's5_b_re': normal((DEPTH, g_, n_, p_), (2 * p_) ** -0.5),
        's5_b_im': normal((DEPTH, g_, n_, p_), (2 * p_) ** -0.5),
        's5_c_re': normal((DEPTH, g_, p_, n_), n_ ** -0.5),
        's5_c_im': normal((DEPTH, g_, p_, n_), n_ ** -0.5),
        's5_d': normal((DEPTH, S5_CH), 1.0),
        's5_glu_w': normal((DEPTH, S5_CH, S5_CH), S5_CH ** -0.5),
        's5_glu_b': normal((DEPTH, S5_CH), 0.02),
        'ml_conv_w': normal((DEPTH, ML_CONV, 2 * D_ML), ML_CONV ** -0.5),
        'ml_conv_b': normal((DEPTH, 2 * D_ML), 0.02),
        'ml_gate_b': ml_gate_b,
        'ml_norm': 1.0 + normal((DEPTH, ML_DH), 0.01),
        'ffn_w1': normal((n_dense, d, D_FF), d ** -0.5),
        'ffn_w3': normal((n_dense, d, D_FF), d ** -0.5),
        'ffn_w2': normal((n_dense, D_FF, d), D_FF ** -0.5),
        'moe_router': normal((n_moe, d, N_EXPERTS), d ** -0.5),
        'moe_w1': normal((n_moe, N_EXPERTS, d, D_FF_EXPERT), d ** -0.5),
        'moe_w3': normal((n_moe, N_EXPERTS, d, D_FF_EXPERT), d ** -0.5),
        'moe_w2': normal((n_moe, N_EXPERTS, D_FF_EXPERT, d), D_FF_EXPERT ** -0.5),
        'norm_f': 1.0 + normal((d,), 0.01),
    }


def reference(x, c, ctx, c_ctx, w_ada, b_ada, norm1, norm2, w_in, w_out,
              gla_wa2, gla_ba, gla_norm,
              s5_lam_re, s5_lam_im, s5_log_dt, s5_b_re, s5_b_im, s5_c_re, s5_c_im, s5_d,
              s5_glu_w, s5_glu_b,
              ml_conv_w, ml_conv_b, ml_gate_b, ml_norm,
              ffn_w1, ffn_w3, ffn_w2, moe_router, moe_w1, moe_w3, moe_w2, norm_f):
    bsz, n_lat, d = x.shape
    h_lat = x + pos_embed_2d(n_lat, x.dtype)[None]
    h_ctx = ctx
    for i in range(DEPTH):
        last = i == DEPTH - 1
        sh1_l, sc1_l, g1_l, sh2_l, sc2_l, g2_l = modulation(c, w_ada[i], b_ada[i])
        sh1_c, sc1_c, g1_c, sh2_c, sc2_c, g2_c = modulation(c_ctx, w_ada[i], b_ada[i])
        m_c, m_l = token_mixers(
            modulate(rmsnorm(h_ctx, norm1[i]), sh1_c, sc1_c),
            modulate(rmsnorm(h_lat, norm1[i]), sh1_l, sc1_l),
            w_in[i], w_out[i], gla_wa2[i], gla_ba[i], gla_norm[i],
            s5_lam_re[i], s5_lam_im[i], s5_log_dt[i], s5_b_re[i], s5_b_im[i], s5_c_re[i], s5_c_im[i],
            s5_d[i], s5_glu_w[i], s5_glu_b[i],
            ml_conv_w[i], ml_conv_b[i], ml_gate_b[i], ml_norm[i])
        h_lat = h_lat + g1_l * m_l
        f_l = modulate(rmsnorm(h_lat, norm2[i]), sh2_l, sc2_l).reshape(-1, d)
        if last:
            y_l = channel_mixer(i, f_l, ffn_w1, ffn_w3, ffn_w2, moe_router, moe_w1, moe_w3, moe_w2)
        else:
            h_ctx = h_ctx + g1_c * m_c
            f_c = modulate(rmsnorm(h_ctx, norm2[i]), sh2_c, sc2_c).reshape(-1, d)
            y = channel_mixer(i, jnp.concatenate([f_c, f_l], axis=0),
                              ffn_w1, ffn_w3, ffn_w2, moe_router, moe_w1, moe_w3, moe_w2)
            n_c = f_c.shape[0]
            y_l = y[n_c:]
            h_ctx = h_ctx + g2_c * y[:n_c].reshape(h_ctx.shape)
        h_lat = h_lat + g2_l * y_l.reshape(h_lat.shape)
    return rmsnorm(h_lat, norm_f)
```

```python
import functools
import math

import numpy as np
import jax
import jax.numpy as jnp
from jax import lax
from jax.experimental import pallas as pl
from jax.experimental.pallas import tpu as pltpu

F32 = jnp.float32
BF16 = jnp.bfloat16
HIGHEST = lax.Precision.HIGHEST

GRID_W = 64
POS_BASE = 10000.0
EPS = 1e-6
GLA_HEADS, GLA_DK, GLA_DV, GLA_RANK, GLA_GATE_NORM = 4, 48, 96, 16, 16.0
S5_GROUP, S5_STATE = 16, 64
ML_HEADS, ML_DH, ML_CONV = 4, 96, 3
N_EXPERTS, TOP_K = 8, 2

LANE = 128
CHUNK = 64
SEG = 256
NEG = -1e30
VMEM_LIMIT = 56 * 1024 * 1024

HP = LANE * GLA_HEADS
NG = 4 * HP + LANE
NM = 4 * HP + LANE


def _cparams(sem):
    return pltpu.CompilerParams(dimension_semantics=sem, vmem_limit_bytes=VMEM_LIMIT)


def _dot(a, b, **kw):
    return jnp.dot(a, b, preferred_element_type=F32, **kw)


def _dot_nt(a, b, **kw):
    return lax.dot_general(a, b, (((1,), (1,)), ((), ())), preferred_element_type=F32, **kw)


def _dot_tn(a, b, **kw):
    return lax.dot_general(a, b, (((0,), (0,)), ((), ())), preferred_element_type=F32, **kw)


def _log_sigmoid(x):
    return jnp.minimum(x, 0.0) - jnp.log1p(jnp.exp(-jnp.abs(x)))


def _silu(x):
    return x * jax.nn.sigmoid(x)


def _gelu_tanh(x):
    return 0.5 * x * (1.0 + jnp.tanh(math.sqrt(2.0 / math.pi) * (x + 0.044715 * (x * x * x))))


def _embed_kernel(nc_blocks, ctx_ref, x_ref, pos_ref, o_ref):
    i = pl.program_id(0)

    @pl.when(i < nc_blocks)
    def _():
        o_ref[...] = ctx_ref[...]

    @pl.when(i >= nc_blocks)
    def _():
        o_ref[...] = x_ref[...] + pos_ref[...]


def _embed(ctx2, x2, pos, n_lat):
    rc, d = ctx2.shape
    rl = x2.shape[0]
    ncb, nlb, npb = rc // SEG, rl // SEG, n_lat // SEG
    return pl.pallas_call(
        functools.partial(_embed_kernel, ncb),
        out_shape=jax.ShapeDtypeStruct((rc + rl, d), F32),
        grid=(ncb + nlb,),
        in_specs=[pl.BlockSpec((SEG, d), lambda i: (jnp.minimum(i, ncb - 1), 0)),
                  pl.BlockSpec((SEG, d), lambda i: (jnp.maximum(i - ncb, 0), 0)),
                  pl.BlockSpec((SEG, d), lambda i: (jnp.maximum(i - ncb, 0) % npb, 0))],
        out_specs=pl.BlockSpec((SEG, d), lambda i: (i, 0)),
        compiler_params=_cparams(("arbitrary",)),
        name="embed",
    )(ctx2, x2, pos)


def _mod_kernel(c_ref, w_ref, b_ref, o_ref):
    s = _silu(c_ref[...])
    o_ref[0] = _dot(s, w_ref[0], precision=HIGHEST) + b_ref[0]


def _modulation(cond, w_ada, b_ada):
    depth, d, n6 = w_ada.shape
    tn = n6 // 4
    rows = cond.shape[0]
    return pl.pallas_call(
        _mod_kernel,
        out_shape=jax.ShapeDtypeStruct((depth, rows, n6), F32),
        grid=(depth, n6 // tn),
        in_specs=[pl.BlockSpec((rows, d), lambda l, j: (0, 0)),
                  pl.BlockSpec((1, d, tn), lambda l, j: (l, 0, j)),
                  pl.BlockSpec((1, 1, tn), lambda l, j: (l, 0, j))],
        out_specs=pl.BlockSpec((1, rows, tn), lambda l, j: (l, 0, j)),
        compiler_params=_cparams(("arbitrary", "arbitrary")),
        name="modulation",
    )(cond, w_ada, b_ada.reshape(depth, 1, n6))


def _proj_kernel(h_ref, mod_ref, g_ref, w_ref, wgt_ref, pg_ref, pu_ref, pm_ref, gt_ref):
    x = h_ref[...]
    y = x * lax.rsqrt(jnp.mean(x * x, axis=-1, keepdims=True) + EPS) * g_ref[...]
    m = mod_ref[0]
    ab = (y * (1.0 + m[1:2]) + m[0:1]).astype(BF16)
    pg_ref[...] = _dot(ab, w_ref[:, 0:NG])
    pu_ref[...] = _dot(ab, w_ref[:, NG:NG + 2 * LANE])
    pm_ref[...] = _dot(ab, w_ref[:, NG + 2 * LANE:])
    gt_ref[...] = _dot_nt(wgt_ref[...], ab)


def _mod_row(i, tm, rc, n_lat, nb):
    r = i * tm
    return jnp.where(r < rc, nb, (r - rc) // n_lat)


def _proj(h, modtab, g, w, wgt, *, layer, rc, n_lat, nb):
    r, d = h.shape
    tm = SEG
    sel = functools.partial(_mod_row, tm=tm, rc=rc, n_lat=n_lat, nb=nb)
    nw = w.shape[1]
    return pl.pallas_call(
        _proj_kernel,
        out_shape=(jax.ShapeDtypeStruct((r, NG), F32), jax.ShapeDtypeStruct((r, 2 * LANE), F32),
                   jax.ShapeDtypeStruct((r, NM), F32), jax.ShapeDtypeStruct((16, r), F32)),
        grid=(r // tm,),
        in_specs=[pl.BlockSpec((tm, d), lambda i: (i, 0)),
                  pl.BlockSpec((None, 1, 8, d), lambda i: (layer, sel(i), 0, 0)),
                  pl.BlockSpec((1, d), lambda i: (0, 0)),
                  pl.BlockSpec((d, nw), lambda i: (0, 0)),
                  pl.BlockSpec((16, d), lambda i: (0, 0))],
        out_specs=(pl.BlockSpec((tm, NG), lambda i: (i, 0)),
                   pl.BlockSpec((tm, 2 * LANE), lambda i: (i, 0)),
                   pl.BlockSpec((tm, NM), lambda i: (i, 0)),
                   pl.BlockSpec((16, tm), lambda i: (0, i))),
        compiler_params=_cparams(("arbitrary",)),
        name="proj",
    )(h, modtab, g, w, wgt)


def _conv_kernel(seg_blocks_c, nc_blocks, seg_blocks_l, x_ref, prev_ref, next_ref, w_ref, b_ref, post_ref, o_ref):
    i = pl.program_id(0)
    x = x_ref[...]
    tm = x.shape[0]
    j = jnp.where(i < nc_blocks, i % seg_blocks_c, (i - nc_blocks) % seg_blocks_l)
    nseg = jnp.where(i < nc_blocks, seg_blocks_c, seg_blocks_l)
    first = (j == 0).astype(F32)
    last = (j == nseg - 1).astype(F32)
    row = lax.broadcasted_iota(jnp.int32, x.shape, 0)
    xp = jnp.where(row == 0, prev_ref[7:8, :] * (1.0 - first), pltpu.roll(x, 1, axis=0))
    xn = jnp.where(row == tm - 1, next_ref[0:1, :] * (1.0 - last), pltpu.roll(x, tm - 1, axis=0))
    y = w_ref[0:1] * xp + w_ref[1:2] * x + w_ref[2:3] * xn + b_ref[...]
    o_ref[...] = _silu(y) * post_ref[...]


def _conv(pm, w, b, post, *, rc, lc, n_lat):
    r = pm.shape[0]
    tm = SEG
    wq = 2 * HP
    t8 = tm // 8
    nblk = r // tm
    return pl.pallas_call(
        functools.partial(_conv_kernel, lc // tm, rc // tm, n_lat // tm),
        out_shape=jax.ShapeDtypeStruct((r, wq), F32),
        grid=(nblk,),
        in_specs=[pl.BlockSpec((tm, wq), lambda i: (i, 0)),
                  pl.BlockSpec((8, wq), lambda i: (jnp.maximum(i * t8 - 1, 0), 0)),
                  pl.BlockSpec((8, wq), lambda i: (jnp.minimum((i + 1) * t8, r // 8 - 1), 0)),
                  pl.BlockSpec((8, wq), lambda i: (0, 0)),
                  pl.BlockSpec((1, wq), lambda i: (0, 0)),
                  pl.BlockSpec((1, wq), lambda i: (0, 0))],
        out_specs=pl.BlockSpec((tm, wq), lambda i: (i, 0)),
        compiler_params=_cparams(("arbitrary",)),
        name="conv",
    )(pm, pm, pm, w, b, post)


def _scan_block(d, s, b, ncb, ntb, nb):
    rev = jnp.where(s < ncb, ncb - 1 - s, ntb - 1 - (s - ncb))
    blk = jnp.where(d == 0, s, rev)
    nlb = ntb - ncb
    return jnp.where(blk < ncb, b * ncb + blk, nb * ncb + b * nlb + (blk - ncb))


def _tri(d):
    r = lax.broadcasted_iota(jnp.int32, (CHUNK, CHUNK), 0)
    c = lax.broadcasted_iota(jnp.int32, (CHUNK, CHUNK), 1)
    return jnp.where(d == 0, r - c, c - r) >= 0


def _gla_kernel(p_ref, wa_ref, ba_ref, o_ref, st_ref):
    d, s, b = pl.program_id(0), pl.program_id(1), pl.program_id(2)

    @pl.when(s == 0)
    def _():
        st_ref[b] = jnp.zeros(st_ref.shape[1:], F32)

    valid = _tri(d)
    tri = valid.astype(F32)
    n_sub = p_ref.shape[0] // CHUNK

    def chunk(j, carry):
        c = jnp.where(d == 0, j, n_sub - 1 - j)
        r0 = pl.multiple_of(c * CHUNK, CHUNK)
        q = p_ref[pl.ds(r0, CHUNK), 0:HP]
        k = p_ref[pl.ds(r0, CHUNK), HP:2 * HP]
        v = p_ref[pl.ds(r0, CHUNK), 2 * HP:3 * HP].astype(BF16)
        lr = p_ref[pl.ds(r0, CHUNK), 4 * HP:4 * HP + LANE]
        la = _log_sigmoid(_dot(lr, wa_ref[0]) + ba_ref[0]) * (1.0 / GLA_GATE_NORM)
        bc = _dot(tri, la, precision=HIGHEST)
        b_last = jnp.sum(la, axis=0, keepdims=True)
        e_last = jnp.exp(b_last)
        q_in = (q * (GLA_DK ** -0.5) * jnp.exp(bc)).astype(BF16)
        k_in = k * jnp.exp(-bc)
        k_out = (k_in * e_last).astype(BF16)
        k_in = k_in.astype(BF16)
        for h in range(GLA_HEADS):
            sl = slice(h * LANE, (h + 1) * LANE)
            st = st_ref[b, h]
            att = jnp.where(valid, _dot_nt(q_in[:, sl], k_in[:, sl]), 0.0).astype(BF16)
            o = _dot(att, v[:, sl]) + _dot_nt(q_in[:, sl], st.astype(BF16))
            o_ref[0, pl.ds(r0, CHUNK), sl] = o
            st_ref[b, h] = st * e_last[:, sl] + _dot_tn(v[:, sl], k_out[:, sl])
        return carry

    lax.fori_loop(0, n_sub, chunk, 0)


def _gla(pg, wa, ba, *, nb, ncb, ntb):
    r = pg.shape[0]
    blk = functools.partial(_scan_block, ncb=ncb, ntb=ntb, nb=nb)
    return pl.pallas_call(
        _gla_kernel,
        out_shape=jax.ShapeDtypeStruct((2, r, HP), F32),
        grid=(2, ntb, nb),
        in_specs=[pl.BlockSpec((SEG, NG), lambda d, s, b: (blk(d, s, b), 0)),
                  pl.BlockSpec((1, LANE, HP), lambda d, s, b: (d, 0, 0)),
                  pl.BlockSpec((1, 1, HP), lambda d, s, b: (d, 0, 0))],
        out_specs=pl.BlockSpec((1, SEG, HP), lambda d, s, b: (d, blk(d, s, b), 0)),
        scratch_shapes=[pltpu.VMEM((nb, GLA_HEADS, LANE, LANE), F32)],
        compiler_params=_cparams(("arbitrary", "arbitrary", "arbitrary")),
        name="gla_scan",
    )(pg, wa, ba)


def _mlstm_kernel(qk_ref, v_ref, g_ref, gt_ref, gbr_ref, gbc_ref, o_ref, st_ref, m_ref):
    d, s, b = pl.program_id(0), pl.program_id(1), pl.program_id(2)

    @pl.when(s == 0)
    def _():
        st_ref[b] = jnp.zeros(st_ref.shape[1:], F32)
        m_ref[b] = jnp.zeros(m_ref.shape[1:], F32)

    valid = _tri(d)
    tri = valid.astype(F32)
    n_sub = qk_ref.shape[0] // CHUNK
    fwd = d == 0
    lane = lax.broadcasted_iota(jnp.int32, (CHUNK, LANE), 1)

    def chunk(j, carry):
        c = jnp.where(fwd, j, n_sub - 1 - j)
        r0 = pl.multiple_of(c * CHUNK, CHUNK)
        gc = g_ref[pl.ds(r0, CHUNK), :] + gbr_ref[...]
        gr = gt_ref[c] + gbc_ref[...]
        fcum_c = _dot(tri, _log_sigmoid(gc), precision=HIGHEST)
        fcum_r = _dot_nt(_log_sigmoid(gr), tri, precision=HIGHEST)
        for h in range(ML_HEADS):
            sl = slice(h * LANE, (h + 1) * LANE)
            ci0, cf0, ci1, cf1 = h, ML_HEADS + h, 2 * ML_HEADS + h, 3 * ML_HEADS + h
            fc = jnp.where(fwd, fcum_c[:, cf0:cf0 + 1], fcum_c[:, cf1:cf1 + 1])
            fr = jnp.where(fwd, fcum_r[cf0:cf0 + 1, :], fcum_r[cf1:cf1 + 1, :])
            lic = jnp.where(fwd, gc[:, ci0:ci0 + 1], gc[:, ci1:ci1 + 1])
            lir = jnp.where(fwd, gr[ci0:ci0 + 1, :], gr[ci1:ci1 + 1, :])
            m_prev = m_ref[b, h:h + 1, 0:1]
            q = qk_ref[pl.ds(r0, CHUNK), sl].astype(BF16)
            k = qk_ref[pl.ds(r0, CHUNK), HP + h * LANE:HP + (h + 1) * LANE]
            v = jnp.where(lane == ML_DH, 1.0, v_ref[pl.ds(r0, CHUNK), sl]).astype(BF16)
            st = st_ref[b, h]
            d_log = jnp.where(valid, fc - fr + lir, NEG)
            prev_log = fc + m_prev
            m_t = jnp.maximum(prev_log, jnp.max(d_log, axis=-1, keepdims=True))
            w = jnp.exp(d_log - m_t)
            w_prev = jnp.exp(prev_log - m_t)
            qk = (_dot_nt(q, k.astype(BF16)) * w).astype(BF16)
            num = w_prev * _dot_nt(q, st.astype(BF16)) + _dot(qk, v)
            den = num[:, ML_DH:ML_DH + 1]
            hh = num / jnp.maximum(jnp.abs(den), jnp.exp(-m_t))
            o_ref[0, pl.ds(r0, CHUNK), sl] = jnp.where(lane < ML_DH, hh, 0.0)
            f_tot = jnp.where(fwd, fc[CHUNK - 1:CHUNK], fc[0:1])
            m_new = jnp.where(fwd, m_t[CHUNK - 1:CHUNK], m_t[0:1])
            w_s = jnp.exp(f_tot - fc + lic - m_new)
            gdec = jnp.exp(f_tot + m_prev - m_new)
            st_ref[b, h] = gdec * st + _dot_tn(v, (k * w_s).astype(BF16))
            m_ref[b, h:h + 1, :] = jnp.broadcast_to(m_new, (1, LANE))
        return carry

    lax.fori_loop(0, n_sub, chunk, 0)


def _mlstm(qk, pm, gt, gbr, gbc, *, nb, ncb, ntb):
    r = qk.shape[0]
    blk = functools.partial(_scan_block, ncb=ncb, ntb=ntb, nb=nb)
    return pl.pallas_call(
        _mlstm_kernel,
        out_shape=jax.ShapeDtypeStruct((2, r, HP), F32),
        grid=(2, ntb, nb),
        in_specs=[pl.BlockSpec((SEG, 2 * HP), lambda d, s, b: (blk(d, s, b), 0)),
                  pl.BlockSpec((SEG, HP), lambda d, s, b: (blk(d, s, b), 2)),
                  pl.BlockSpec((SEG, LANE), lambda d, s, b: (blk(d, s, b), 4 * GLA_HEADS)),
                  pl.BlockSpec((SEG // CHUNK, 16, CHUNK), lambda d, s, b: (blk(d, s, b), 0, 0)),
                  pl.BlockSpec((1, LANE), lambda d, s, b: (0, 0)),
                  pl.BlockSpec((16, CHUNK), lambda d, s, b: (0, 0))],
        out_specs=pl.BlockSpec((1, SEG, HP), lambda d, s, b: (d, blk(d, s, b), 0)),
        scratch_shapes=[pltpu.VMEM((nb, ML_HEADS, LANE, LANE), F32), pltpu.VMEM((nb, 8, LANE), F32)],
        compiler_params=_cparams(("arbitrary", "arbitrary", "arbitrary")),
        name="mlstm_scan",
    )(qk, pm, pm, gt, gbr, gbc)


def _s5_kernel(nb, u_ref, bre_ref, bim_ref, are_ref, aim_ref, cre_ref, cim_ref, o_ref, xr_ref, xi_ref, st_ref):
    d, s = pl.program_id(0), pl.program_id(1)

    @pl.when(s == 0)
    def _():
        st_ref[...] = jnp.zeros(st_ref.shape, F32)

    u = u_ref[...].astype(BF16)
    xr_ref[...] = _dot(u, bre_ref[0])
    xi_ref[...] = _dot(u, bim_ref[0])
    ar, ai = are_ref[0], aim_ref[0]
    n_t = u_ref.shape[0] // nb

    def step(j, carry):
        sr, si = carry
        t = jnp.where(d == 0, j, n_t - 1 - j)
        r0 = pl.multiple_of(t * nb, nb)
        nr = ar * sr - ai * si + xr_ref[pl.ds(r0, nb), :]
        ni = ar * si + ai * sr + xi_ref[pl.ds(r0, nb), :]
        xr_ref[pl.ds(r0, nb), :] = nr
        xi_ref[pl.ds(r0, nb), :] = ni
        return nr, ni

    sr, si = lax.fori_loop(0, n_t, step, (st_ref[0], st_ref[1]), unroll=4)
    st_ref[0] = sr
    st_ref[1] = si
    o_ref[0] = _dot(xr_ref[...].astype(BF16), cre_ref[...]) - _dot(xi_ref[...].astype(BF16), cim_ref[...])


def _s5(ut, bre, bim, are, aim, cre, cim, *, nb, nc, nt):
    rows, ch = ut.shape
    tr = CHUNK * nb
    ns = bre.shape[-1]

    def blk(d, s):
        rev = jnp.where(s < nc, nc - 1 - s, nt - 1 - (s - nc))
        return jnp.where(d == 0, s, rev)

    return pl.pallas_call(
        functools.partial(_s5_kernel, nb),
        out_shape=jax.ShapeDtypeStruct((2, rows, ch), F32),
        grid=(2, nt),
        in_specs=[pl.BlockSpec((tr, ch), lambda d, s: (blk(d, s), 0)),
                  pl.BlockSpec((1, ch, ns), lambda d, s: (d, 0, 0)),
                  pl.BlockSpec((1, ch, ns), lambda d, s: (d, 0, 0)),
                  pl.BlockSpec((1, nb, ns), lambda d, s: (d, 0, 0)),
                  pl.BlockSpec((1, nb, ns), lambda d, s: (d, 0, 0)),
                  pl.BlockSpec((ns, ch), lambda d, s: (0, 0)),
                  pl.BlockSpec((ns, ch), lambda d, s: (0, 0))],
        out_specs=pl.BlockSpec((1, tr, ch), lambda d, s: (d, blk(d, s), 0)),
        scratch_shapes=[pltpu.VMEM((tr, ns), F32), pltpu.VMEM((tr, ns), F32), pltpu.VMEM((2, nb, ns), F32)],
        compiler_params=_cparams(("arbitrary", "arbitrary")),
        name="s5_scan",
    )(ut, bre, bim, are, aim, cre, cim)


def _head_norm(o, gain, dim):
    parts = []
    for h in range(o.shape[1] // LANE):
        seg = o[:, h * LANE:(h + 1) * LANE]
        ms = jnp.sum(seg * seg, axis=-1, keepdims=True) * (1.0 / dim)
        parts.append(seg * lax.rsqrt(ms + EPS))
    return jnp.concatenate(parts, axis=1) * gain


def _mix_kernel(with_router, og_ref, gg_ref, ys_ref, u_ref, om_ref, mo_ref, h_ref, mod_ref, gn_ref, mn_ref,
                sd_ref, gw_ref, gb_ref, wo_ref, n2_ref, *rest):
    if with_router:
        wr_ref, ho_ref, f_ref, rt_ref = rest
    else:
        ho_ref, f_ref = rest
    gla = _head_norm(og_ref[0] + og_ref[1], gn_ref[...], GLA_DV) * _silu(gg_ref[...])
    u = u_ref[...]
    z = _gelu_tanh(ys_ref[0] + ys_ref[1] + sd_ref[...] * u)
    s5 = z * jax.nn.sigmoid(_dot(z.astype(BF16), gw_ref[...]) + gb_ref[...])
    ml = _head_norm(om_ref[0] + om_ref[1], mn_ref[...], ML_DH) * jax.nn.sigmoid(mo_ref[...])
    mix = (_dot(gla.astype(BF16), wo_ref[0:HP]) + _dot(s5.astype(BF16), wo_ref[HP:HP + 2 * LANE])
           + _dot(ml.astype(BF16), wo_ref[HP + 2 * LANE:]))
    m = mod_ref[0]
    hn = h_ref[...] + m[2:3] * mix
    ho_ref[...] = hn
    y = hn * lax.rsqrt(jnp.mean(hn * hn, axis=-1, keepdims=True) + EPS) * n2_ref[...]
    f = y * (1.0 + m[4:5]) + m[3:4]
    f_ref[...] = f
    if with_router:
        logits = _dot(f, wr_ref[...], precision=HIGHEST)
        lane = lax.broadcasted_iota(jnp.int32, logits.shape, 1)
        l0 = jnp.where(lane < N_EXPERTS, logits, NEG)
        m1 = jnp.max(l0, axis=-1, keepdims=True)
        i1 = jnp.min(jnp.where(l0 == m1, lane, LANE), axis=-1, keepdims=True)
        l1 = jnp.where(lane == i1, NEG, l0)
        m2 = jnp.max(l1, axis=-1, keepdims=True)
        i2 = jnp.min(jnp.where(l1 == m2, lane, LANE), axis=-1, keepdims=True)
        e = jnp.exp(m2 - m1)
        w1 = 1.0 / (1.0 + e)
        w2 = e / (1.0 + e)
        rt_ref[...] = jnp.where(lane == 0, i1.astype(F32),
                                jnp.where(lane == 1, i2.astype(F32),
                                          jnp.where(lane == 2, w1, jnp.where(lane == 3, w2, 0.0))))


def _mix(og, pg, ys, pu, om, pm, h, modtab, gn, mn, sd, gw, gb, wo, n2, wr, *, layer, rc, n_lat, nb):
    r, d = h.shape
    tm = SEG
    sel = functools.partial(_mod_row, tm=tm, rc=rc, n_lat=n_lat, nb=nb)
    full = lambda a: pl.BlockSpec(a.shape, lambda i: (0,) * a.ndim)
    with_router = wr is not None
    in_specs = [pl.BlockSpec((2, tm, HP), lambda i: (0, i, 0)),
                pl.BlockSpec((tm, HP), lambda i: (i, 3)),
                pl.BlockSpec((2, tm, 2 * LANE), lambda i: (0, i, 0)),
                pl.BlockSpec((tm, 2 * LANE), lambda i: (i, 0)),
                pl.BlockSpec((2, tm, HP), lambda i: (0, i, 0)),
                pl.BlockSpec((tm, HP), lambda i: (i, 3)),
                pl.BlockSpec((tm, d), lambda i: (i, 0)),
                pl.BlockSpec((None, 1, 8, d), lambda i: (layer, sel(i), 0, 0)),
                full(gn), full(mn), full(sd), full(gw), full(gb), full(wo), full(n2)]
    args = [og, pg, ys, pu, om, pm, h, modtab, gn, mn, sd, gw, gb, wo, n2]
    out_shape = [jax.ShapeDtypeStruct((r, d), F32), jax.ShapeDtypeStruct((r, d), F32)]
    out_specs = [pl.BlockSpec((tm, d), lambda i: (i, 0)), pl.BlockSpec((tm, d), lambda i: (i, 0))]
    if with_router:
        in_specs.append(full(wr))
        args.append(wr)
        out_shape.append(jax.ShapeDtypeStruct((r, LANE), F32))
        out_specs.append(pl.BlockSpec((tm, LANE), lambda i: (i, 0)))
    return pl.pallas_call(
        functools.partial(_mix_kernel, with_router),
        out_shape=tuple(out_shape),
        grid=(r // tm,),
        in_specs=in_specs,
        out_specs=tuple(out_specs),
        compiler_params=_cparams(("arbitrary",)),
        name="mix_out",
    )(*args)


FF_TILE = 256


def _swiglu_acc(xb, w1_ref, w3_ref, w2_ref, lead=()):
    dff = w1_ref.shape[-1]
    acc = None
    for j in range(dff // FF_TILE):
        sl = slice(j * FF_TILE, (j + 1) * FF_TILE)
        h1 = _dot(xb, w1_ref[lead + (slice(None), sl)])
        h3 = _dot(xb, w3_ref[lead + (slice(None), sl)])
        a = (_silu(h1) * h3).astype(BF16)
        t = _dot(a, w2_ref[lead + (sl, slice(None))])
        acc = t if acc is None else acc + t
    return acc


def _ffn_kernel(f_ref, h_ref, mod_ref, w1_ref, w3_ref, w2_ref, o_ref):
    y = _swiglu_acc(f_ref[...].astype(BF16), w1_ref, w3_ref, w2_ref)
    o_ref[...] = h_ref[...] + mod_ref[0][5:6] * y


def _ffn(f, h, modtab, w1, w3, w2, *, layer, rc, n_lat, nb):
    r, d = h.shape
    tm = SEG
    sel = functools.partial(_mod_row, tm=tm, rc=rc, n_lat=n_lat, nb=nb)
    full = lambda a: pl.BlockSpec(a.shape, lambda i: (0,) * a.ndim)
    return pl.pallas_call(
        _ffn_kernel,
        out_shape=jax.ShapeDtypeStruct((r, d), F32),
        grid=(r // tm,),
        in_specs=[pl.BlockSpec((tm, d), lambda i: (i, 0)),
                  pl.BlockSpec((tm, d), lambda i: (i, 0)),
                  pl.BlockSpec((None, 1, 8, d), lambda i: (layer, sel(i), 0, 0)),
                  full(w1), full(w3), full(w2)],
        out_specs=pl.BlockSpec((tm, d), lambda i: (i, 0)),
        compiler_params=_cparams(("arbitrary",)),
        name="ffn",
    )(f, h, modtab, w1, w3, w2)


MOE_TM = 256


def _moe_kernel(be_ref, x_ref, rw_ref, w1_ref, w3_ref, w2_ref, o_ref):
    y = _swiglu_acc(x_ref[...].astype(BF16), w1_ref, w3_ref, w2_ref, lead=(0,))
    o_ref[...] = y * rw_ref[:, 0:1]


def _moe_experts(block_expert, xg, rw, w1, w3, w2):
    n_rows, d = xg.shape
    dff = w1.shape[-1]
    tm = MOE_TM
    return pl.pallas_call(
        _moe_kernel,
        out_shape=jax.ShapeDtypeStruct((n_rows, d), F32),
        grid_spec=pltpu.PrefetchScalarGridSpec(
            num_scalar_prefetch=1,
            grid=(n_rows // tm,),
            in_specs=[pl.BlockSpec((tm, d), lambda i, be: (i, 0)),
                      pl.BlockSpec((tm, LANE), lambda i, be: (i, 0)),
                      pl.BlockSpec((1, d, dff), lambda i, be: (be[i], 0, 0)),
                      pl.BlockSpec((1, d, dff), lambda i, be: (be[i], 0, 0)),
                      pl.BlockSpec((1, dff, d), lambda i, be: (be[i], 0, 0))],
            out_specs=pl.BlockSpec((tm, d), lambda i, be: (i, 0))),
        compiler_params=_cparams(("arbitrary",)),
        name="moe_experts",
    )(block_expert, xg, rw, w1, w3, w2)


def _resid_kernel(final, h_ref, y0_ref, y1_ref, mod_ref, *rest):
    hn = h_ref[...] + mod_ref[0][5:6] * (y0_ref[...] + y1_ref[...])
    if final:
        nf_ref, o_ref = rest
        o_ref[...] = hn * lax.rsqrt(jnp.mean(hn * hn, axis=-1, keepdims=True) + EPS) * nf_ref[...]
    else:
        (o_ref,) = rest
        o_ref[...] = hn


def _moe_resid(h, y0, y1, modtab, nf, *, layer, rc, n_lat, nb, final):
    r, d = h.shape
    tm = SEG
    off = rc // tm if final else 0
    sel = functools.partial(_mod_row, tm=tm, rc=rc, n_lat=n_lat, nb=nb)
    row = lambda i: (i + off, 0)
    in_specs = [pl.BlockSpec((tm, d), row), pl.BlockSpec((tm, d), row), pl.BlockSpec((tm, d), row),
                pl.BlockSpec((None, 1, 8, d), lambda i: (layer, sel(i + off), 0, 0))]
    args = [h, y0, y1, modtab]
    if final:
        in_specs.append(pl.BlockSpec((1, d), lambda i: (0, 0)))
        args.append(nf)
    n_out = r - off * tm
    return pl.pallas_call(
        functools.partial(_resid_kernel, final),
        out_shape=jax.ShapeDtypeStruct((n_out, d), F32),
        grid=(n_out // tm,),
        in_specs=in_specs,
        out_specs=pl.BlockSpec((tm, d), lambda i: (i, 0)),
        compiler_params=_cparams(("arbitrary",)),
        name="moe_resid",
    )(*args)


def _final_norm_kernel(h_ref, nf_ref, o_ref):
    hn = h_ref[...]
    o_ref[...] = hn * lax.rsqrt(jnp.mean(hn * hn, axis=-1, keepdims=True) + EPS) * nf_ref[...]


def _final_norm(h, nf, rc):
    r, d = h.shape
    tm = SEG
    off = rc // tm
    return pl.pallas_call(
        _final_norm_kernel,
        out_shape=jax.ShapeDtypeStruct((r - rc, d), F32),
        grid=((r - rc) // tm,),
        in_specs=[pl.BlockSpec((tm, d), lambda i: (i + off, 0)), pl.BlockSpec((1, d), lambda i: (0, 0))],
        out_specs=pl.BlockSpec((tm, d), lambda i: (i, 0)),
        compiler_params=_cparams(("arbitrary",)),
        name="final_norm",
    )(h, nf)


def _pad_heads(w, heads, dim):
    lead = w.shape[:-1]
    w = w.reshape(lead + (heads, dim))
    w = jnp.pad(w, [(0, 0)] * len(lead) + [(0, 0), (0, LANE - dim)])
    return w.reshape(lead + (heads * LANE,))


def _pad_last(w, to):
    return jnp.pad(w, [(0, 0)] * (w.ndim - 1) + [(0, to - w.shape[-1])])


def _pos_embed(n_tokens, d):
    rows = n_tokens // GRID_W
    row, col = jnp.meshgrid(jnp.arange(rows, dtype=F32), jnp.arange(GRID_W, dtype=F32), indexing='ij')
    n_freq = d // 4
    omega = jnp.exp(-math.log(POS_BASE) * jnp.arange(n_freq, dtype=F32) / n_freq)

    def axis_embed(p):
        ang = p.reshape(-1, 1) * omega
        return jnp.concatenate([jnp.sin(ang), jnp.cos(ang)], axis=-1)

    return jnp.concatenate([axis_embed(row), axis_embed(col)], axis=-1)


def _s5_discretise(lam_re, lam_im, log_dt, b_re, b_im):
    dt = jnp.exp(log_dt)[:, None]
    mag = jnp.exp(lam_re * dt)
    abar_re, abar_im = mag * jnp.cos(lam_im * dt), mag * jnp.sin(lam_im * dt)
    den = lam_re * lam_re + lam_im * lam_im
    pr, pi = abar_re - 1.0, abar_im
    coef_re = (pr * lam_re + pi * lam_im) / den
    coef_im = (pi * lam_re - pr * lam_im) / den
    bbar_re = coef_re[..., None] * b_re - coef_im[..., None] * b_im
    bbar_im = coef_re[..., None] * b_im + coef_im[..., None] * b_re
    return abar_re, abar_im, bbar_re, bbar_im


def _block_diag(m):
    g, a, b = m.shape
    eye = jnp.eye(g, dtype=m.dtype)
    return (eye[:, None, :, None] * m[:, :, None, :]).reshape(g * a, g * b)


def _to_time_major(a, nb, rc, lc, n_lat):
    lead = a.shape[:-2]
    ch = a.shape[-1]
    c = jnp.swapaxes(a[..., :rc, :].reshape(lead + (nb, lc, ch)), -3, -2)
    l = jnp.swapaxes(a[..., rc:, :].reshape(lead + (nb, n_lat, ch)), -3, -2)
    return jnp.concatenate([c, l], axis=-3).reshape(lead + ((lc + n_lat) * nb, ch))


def _from_time_major(a, nb, rc, lc, n_lat):
    lead = a.shape[:-2]
    ch = a.shape[-1]
    a = a.reshape(lead + (lc + n_lat, nb, ch))
    c = jnp.swapaxes(a[..., :lc, :, :], -3, -2).reshape(lead + (rc, ch))
    l = jnp.swapaxes(a[..., lc:, :, :], -3, -2).reshape(lead + (nb * n_lat, ch))
    return jnp.concatenate([c, l], axis=-2)


def _route_plan(route, tm):
    n_tok = route.shape[0]
    idx = route[:, 0:TOP_K].astype(jnp.int32)
    flat_e = idx.reshape(-1)
    onehot = (flat_e[:, None] == jnp.arange(N_EXPERTS)[None, :]).astype(jnp.int32)
    csum = jnp.cumsum(onehot, axis=0)
    counts = csum[-1]
    rank = jnp.take_along_axis(csum, flat_e[:, None], axis=1)[:, 0] - 1
    padded = (counts + tm - 1) // tm * tm
    pend = jnp.cumsum(padded)
    pstart = pend - padded
    dest = pstart[flat_e] + rank
    n_assign = n_tok * TOP_K
    n_blocks = -(-n_assign // tm) + N_EXPERTS
    n_rows = n_blocks * tm
    tok = jnp.arange(n_assign, dtype=jnp.int32) // TOP_K
    row_token = jnp.zeros((n_rows,), jnp.int32).at[dest].set(tok)
    row_w = jnp.zeros((n_rows,), F32).at[dest].set(route[:, TOP_K:2 * TOP_K].reshape(-1))
    block_expert = jnp.minimum(
        jnp.searchsorted(pend, jnp.arange(n_blocks) * tm, side='right'), N_EXPERTS - 1).astype(jnp.int32)
    return row_token, row_w, block_expert, dest.reshape(n_tok, TOP_K)


def kernel(x, c, ctx, c_ctx, w_ada, b_ada, norm1, norm2, w_in, w_out, gla_wa2, gla_ba, gla_norm, s5_lam_re, s5_lam_im, s5_log_dt, s5_b_re, s5_b_im, s5_c_re, s5_c_im, s5_d, s5_glu_w, s5_glu_b, ml_conv_w, ml_conv_b, ml_gate_b, ml_norm, ffn_w1, ffn_w3, ffn_w2, moe_router, moe_w1, moe_w3, moe_w2, norm_f):
    nb, n_lat, d = x.shape
    lc = ctx.shape[1]
    depth = w_ada.shape[0]
    rc = nb * lc
    assert lc % SEG == 0 and n_lat % SEG == 0 and nb == 8
    ncb, ntb = lc // SEG, (lc + n_lat) // SEG
    dims = dict(rc=rc, n_lat=n_lat, nb=nb)

    h = _embed(ctx.reshape(rc, d), x.reshape(nb * n_lat, d), _pos_embed(n_lat, d), n_lat)

    cond = jnp.zeros((16, d), F32).at[:nb].set(c).at[nb].set(c_ctx)
    mod = _modulation(cond, w_ada, b_ada)
    modtab = jnp.pad(mod.reshape(depth, 16, 6, d), ((0, 0), (0, 0), (0, 2), (0, 0)))

    dk, dv, dh = GLA_HEADS * GLA_DK, GLA_HEADS * GLA_DV, ML_HEADS * ML_DH
    s5c = s5_d.shape[-1]
    cuts = np.cumsum([dk, dk, dv, GLA_RANK, dv, s5c, dh, dh, dh, dh, 4 * ML_HEADS])
    n_groups = s5c // S5_GROUP

    for i in range(depth):
        wi = w_in[i]
        parts = jnp.split(wi, cuts[:-1], axis=-1)
        gq, gk, gv, glr, gg, su, mq, mk, mv, mo, mg = parts
        w_all = jnp.concatenate([
            _pad_heads(gq, GLA_HEADS, GLA_DK), _pad_heads(gk, GLA_HEADS, GLA_DK),
            _pad_heads(gv, GLA_HEADS, GLA_DV), _pad_heads(gg, GLA_HEADS, GLA_DV), _pad_last(glr, LANE),
            su,
            _pad_heads(mq, ML_HEADS, ML_DH), _pad_heads(mk, ML_HEADS, ML_DH),
            _pad_heads(mv, ML_HEADS, ML_DH), _pad_heads(mo, ML_HEADS, ML_DH), _pad_last(mg, LANE)],
            axis=-1).astype(BF16)
        wgt = mg.T.astype(BF16)
        pg, pu, pm, gt = _proj(h, modtab, norm1[i][None], w_all, wgt, layer=i, **dims)

        wa = jnp.pad(_pad_heads(gla_wa2[i], GLA_HEADS, GLA_DK), ((0, 0), (0, LANE - GLA_RANK), (0, 0)))
        ba = _pad_heads(gla_ba[i], GLA_HEADS, GLA_DK)[:, None, :]
        og = _gla(pg, wa, ba, nb=nb, ncb=ncb, ntb=ntb)

        bres, bims, ares, aims = [], [], [], []
        for dr in (0, 1):
            a_re, a_im, b_re, b_im = _s5_discretise(s5_lam_re[i, dr], s5_lam_im[i, dr], s5_log_dt[i, dr],
                                                    s5_b_re[i], s5_b_im[i])
            bres.append(_block_diag(jnp.swapaxes(b_re, 1, 2)))
            bims.append(_block_diag(jnp.swapaxes(b_im, 1, 2)))
            ares.append(jnp.broadcast_to(a_re.reshape(1, -1), (nb, a_re.size)))
            aims.append(jnp.broadcast_to(a_im.reshape(1, -1), (nb, a_im.size)))
        cre = _block_diag(jnp.swapaxes(s5_c_re[i], 1, 2)).astype(BF16)
        cim = _block_diag(jnp.swapaxes(s5_c_im[i], 1, 2)).astype(BF16)
        ut = _to_time_major(pu, nb, rc, lc, n_lat)
        yt = _s5(ut, jnp.stack(bres).astype(BF16), jnp.stack(bims).astype(BF16), jnp.stack(ares), jnp.stack(aims),
                 cre, cim, nb=nb, nc=lc // CHUNK, nt=(lc + n_lat) // CHUNK)
        ys = _from_time_major(yt, nb, rc, lc, n_lat)

        cw = jnp.concatenate([_pad_heads(ml_conv_w[i][:, :dh], ML_HEADS, ML_DH),
                              _pad_heads(ml_conv_w[i][:, dh:], ML_HEADS, ML_DH)], axis=-1)
        cw = jnp.pad(cw, ((0, 8 - ML_CONV), (0, 0)))
        cb = jnp.concatenate([_pad_heads(ml_conv_b[i][:dh], ML_HEADS, ML_DH),
                              _pad_heads(ml_conv_b[i][dh:], ML_HEADS, ML_DH)])[None]
        post = jnp.concatenate([jnp.ones((HP,), F32), jnp.full((HP,), ML_DH ** -0.5, F32)])[None]
        qk = _conv(pm, cw, cb, post, rc=rc, lc=lc, n_lat=n_lat)
        gb = ml_gate_b[i].reshape(-1)
        gbr = _pad_last(gb, LANE)[None]
        gbc = jnp.broadcast_to(gb[:, None], (16, CHUNK))
        gt3 = gt.reshape(16, -1, CHUNK).swapaxes(0, 1)
        om = _mlstm(qk, pm, gt3, gbr, gbc, nb=nb, ncb=ncb, ntb=ntb)

        wo = w_out[i]
        wo_p = jnp.concatenate([
            jnp.pad(wo[:dv].reshape(GLA_HEADS, GLA_DV, d), ((0, 0), (0, LANE - GLA_DV), (0, 0))).reshape(HP, d),
            wo[dv:dv + s5c],
            jnp.pad(wo[dv + s5c:].reshape(ML_HEADS, ML_DH, d), ((0, 0), (0, LANE - ML_DH), (0, 0))).reshape(HP, d)],
            axis=0).astype(BF16)
        gn = jnp.tile(_pad_last(gla_norm[i], LANE), GLA_HEADS)[None]
        mn = jnp.tile(_pad_last(ml_norm[i], LANE), ML_HEADS)[None]
        is_moe = i % 2 == 1
        j = i // 2
        wr = _pad_last(moe_router[j], LANE) if is_moe else None
        outs = _mix(og, pg, ys, pu, om, pm, h, modtab, gn, mn, s5_d[i][None], s5_glu_w[i].astype(BF16),
                    s5_glu_b[i][None], wo_p, norm2[i][None], wr, layer=i, **dims)
        last = i == depth - 1
        if not is_moe:
            h, f = outs
            h = _ffn(f, h, modtab, ffn_w1[j].astype(BF16), ffn_w3[j].astype(BF16), ffn_w2[j].astype(BF16),
                     layer=i, **dims)
            if last:
                h = _final_norm(h, norm_f[None], rc)
        else:
            h, f, route = outs
            if last:
                f_r, route_r, off = f[rc:], route[rc:], rc
            else:
                f_r, route_r, off = f, route, 0
            row_token, row_w, block_expert, dest = _route_plan(route_r, MOE_TM)
            xg = jnp.take(f_r, row_token, axis=0)
            rw = jnp.broadcast_to(row_w[:, None], (row_w.shape[0], LANE))
            yg = _moe_experts(block_expert, xg, rw, moe_w1[j].astype(BF16), moe_w3[j].astype(BF16),
                              moe_w2[j].astype(BF16))
            y0 = jnp.take(yg, dest[:, 0], axis=0)
            y1 = jnp.take(yg, dest[:, 1], axis=0)
            if last:
                pad = jnp.zeros((off, d), F32)
                y0, y1 = jnp.concatenate([pad, y0]), jnp.concatenate([pad, y1])
            h = _moe_resid(h, y0, y1, modtab, norm_f[None], layer=i, final=last, **dims)
    return h.reshape(nb, n_lat, d)
```

```python
import functools
import math

import numpy as np
import jax
import jax.numpy as jnp
from jax import lax
from jax.experimental import pallas as pl
from jax.experimental.pallas import tpu as pltpu

F32 = jnp.float32
BF16 = jnp.bfloat16
HIGHEST = lax.Precision.HIGHEST

GRID_W = 64
POS_BASE = 10000.0
EPS = 1e-6
GLA_HEADS, GLA_DK, GLA_DV, GLA_RANK, GLA_GATE_NORM = 4, 48, 96, 16, 16.0
S5_GROUP, S5_STATE = 16, 64
ML_HEADS, ML_DH, ML_CONV = 4, 96, 3
N_EXPERTS, TOP_K = 8, 2

LANE = 128
CHUNK = 64
SEG = 256
N_SUB = SEG // CHUNK
SCAN_BATCHES = 2
NEG = -1e30
VMEM_LIMIT = 56 * 1024 * 1024

HP = LANE * GLA_HEADS
NG = 4 * HP + LANE
NM = 4 * HP + LANE


def _cparams(sem):
    return pltpu.CompilerParams(dimension_semantics=sem, vmem_limit_bytes=VMEM_LIMIT)


def _dot(a, b, **kw):
    return jnp.dot(a, b, preferred_element_type=F32, **kw)


def _dot_nt(a, b, **kw):
    return lax.dot_general(a, b, (((1,), (1,)), ((), ())), preferred_element_type=F32, **kw)


def _dot_tn(a, b, **kw):
    return lax.dot_general(a, b, (((0,), (0,)), ((), ())), preferred_element_type=F32, **kw)


def _log_sigmoid(x):
    return jnp.minimum(x, 0.0) - jnp.log1p(jnp.exp(-jnp.abs(x)))


def _silu(x):
    return x * jax.nn.sigmoid(x)


def _gelu_tanh(x):
    return 0.5 * x * (1.0 + jnp.tanh(math.sqrt(2.0 / math.pi) * (x + 0.044715 * (x * x * x))))


def _rmsnorm(x, g):
    return x * lax.rsqrt(jnp.mean(x * x, axis=-1, keepdims=True) + EPS) * g


class _Rows:
    def __init__(self, nb, ncb, ntb):
        self.nb, self.ncb, self.ntb, self.nlb = nb, ncb, ntb, ntb - ncb

    def n_blocks(self, lat_only):
        return self.nb * (self.nlb if lat_only else self.ntb)

    def src(self, lat_only):
        if lat_only:
            return lambda i: (i // self.nlb) * self.ntb + self.ncb + i % self.nlb
        return lambda i: i

    def sel(self, lat_only):
        if lat_only:
            return lambda i: i // self.nlb
        return lambda i: jnp.where(i % self.ntb < self.ncb, self.nb, i // self.ntb)


def _embed_kernel(ncb, ntb, ctx_ref, x_ref, pos_ref, o_ref):
    j = pl.program_id(0) % ntb

    @pl.when(j < ncb)
    def _():
        o_ref[...] = ctx_ref[...]

    @pl.when(j >= ncb)
    def _():
        o_ref[...] = x_ref[...] + pos_ref[...]


def _embed(ctx2, x2, pos, rows):
    d = ctx2.shape[1]
    ncb, ntb, nlb = rows.ncb, rows.ntb, rows.nlb
    return pl.pallas_call(
        functools.partial(_embed_kernel, ncb, ntb),
        out_shape=jax.ShapeDtypeStruct((rows.nb * ntb * SEG, d), F32),
        grid=(rows.nb * ntb,),
        in_specs=[pl.BlockSpec((SEG, d), lambda i: ((i // ntb) * ncb + jnp.minimum(i % ntb, ncb - 1), 0)),
                  pl.BlockSpec((SEG, d), lambda i: ((i // ntb) * nlb + jnp.maximum(i % ntb - ncb, 0), 0)),
                  pl.BlockSpec((SEG, d), lambda i: (jnp.maximum(i % ntb - ncb, 0), 0))],
        out_specs=pl.BlockSpec((SEG, d), lambda i: (i, 0)),
        compiler_params=_cparams(("arbitrary",)),
        name="embed",
    )(ctx2, x2, pos)


def _mod_kernel(c_ref, w_ref, b_ref, o_ref):
    s = _silu(c_ref[...])
    o_ref[0] = _dot(s, w_ref[0], precision=HIGHEST) + b_ref[0]


def _modulation(cond, w_ada, b_ada):
    depth, d, n6 = w_ada.shape
    tn = n6 // 4
    n_rows = cond.shape[0]
    return pl.pallas_call(
        _mod_kernel,
        out_shape=jax.ShapeDtypeStruct((depth, n_rows, n6), F32),
        grid=(depth, n6 // tn),
        in_specs=[pl.BlockSpec((n_rows, d), lambda l, j: (0, 0)),
                  pl.BlockSpec((1, d, tn), lambda l, j: (l, 0, j)),
                  pl.BlockSpec((1, 1, tn), lambda l, j: (l, 0, j))],
        out_specs=pl.BlockSpec((1, n_rows, tn), lambda l, j: (l, 0, j)),
        compiler_params=_cparams(("arbitrary", "arbitrary")),
        name="modulation",
    )(cond, w_ada, b_ada.reshape(depth, 1, n6))


def _proj_kernel(h_ref, mod_ref, g_ref, w_ref, wgt_ref, pg_ref, pu_ref, pm_ref, gt_ref):
    m = mod_ref[0]
    ab = (_rmsnorm(h_ref[...], g_ref[...]) * (1.0 + m[1:2]) + m[0:1]).astype(BF16)
    pg_ref[...] = _dot(ab, w_ref[:, 0:NG])
    pu_ref[...] = _dot(ab, w_ref[:, NG:NG + 2 * LANE])
    pm_ref[...] = _dot(ab, w_ref[:, NG + 2 * LANE:])
    gt_ref[...] = _dot_nt(wgt_ref[...], ab)


def _proj(h, modtab, g, w, wgt, *, layer, rows):
    r, d = h.shape
    tm = SEG
    sel = rows.sel(False)
    nw = w.shape[1]
    return pl.pallas_call(
        _proj_kernel,
        out_shape=(jax.ShapeDtypeStruct((r, NG), F32), jax.ShapeDtypeStruct((r, 2 * LANE), F32),
                   jax.ShapeDtypeStruct((r, NM), F32), jax.ShapeDtypeStruct((16, r), F32)),
        grid=(r // tm,),
        in_specs=[pl.BlockSpec((tm, d), lambda i: (i, 0)),
                  pl.BlockSpec((None, 1, 8, d), lambda i: (layer, sel(i), 0, 0)),
                  pl.BlockSpec((1, d), lambda i: (0, 0)),
                  pl.BlockSpec((d, nw), lambda i: (0, 0)),
                  pl.BlockSpec((16, d), lambda i: (0, 0))],
        out_specs=(pl.BlockSpec((tm, NG), lambda i: (i, 0)),
                   pl.BlockSpec((tm, 2 * LANE), lambda i: (i, 0)),
                   pl.BlockSpec((tm, NM), lambda i: (i, 0)),
                   pl.BlockSpec((16, tm), lambda i: (0, i))),
        compiler_params=_cparams(("arbitrary",)),
        name="proj",
    )(h, modtab, g, w, wgt)


def _conv_kernel(ncb, ntb, x_ref, prev_ref, next_ref, w_ref, b_ref, post_ref, o_ref):
    j = pl.program_id(0) % ntb
    x = x_ref[...]
    tm = x.shape[0]
    first = jnp.logical_or(j == 0, j == ncb).astype(F32)
    last = jnp.logical_or(j == ncb - 1, j == ntb - 1).astype(F32)
    row = lax.broadcasted_iota(jnp.int32, x.shape, 0)
    xp = jnp.where(row == 0, prev_ref[7:8, :] * (1.0 - first), pltpu.roll(x, 1, axis=0))
    xn = jnp.where(row == tm - 1, next_ref[0:1, :] * (1.0 - last), pltpu.roll(x, tm - 1, axis=0))
    y = w_ref[0:1] * xp + w_ref[1:2] * x + w_ref[2:3] * xn + b_ref[...]
    o_ref[...] = _silu(y) * post_ref[...]


def _conv(pm, w, b, post, *, rows):
    r = pm.shape[0]
    tm = SEG
    wq = 2 * HP
    t8 = tm // 8
    return pl.pallas_call(
        functools.partial(_conv_kernel, rows.ncb, rows.ntb),
        out_shape=jax.ShapeDtypeStruct((r, wq), F32),
        grid=(r // tm,),
        in_specs=[pl.BlockSpec((tm, wq), lambda i: (i, 0)),
                  pl.BlockSpec((8, wq), lambda i: (jnp.maximum(i * t8 - 1, 0), 0)),
                  pl.BlockSpec((8, wq), lambda i: (jnp.minimum((i + 1) * t8, r // 8 - 1), 0)),
                  pl.BlockSpec((8, wq), lambda i: (0, 0)),
                  pl.BlockSpec((1, wq), lambda i: (0, 0)),
                  pl.BlockSpec((1, wq), lambda i: (0, 0))],
        out_specs=pl.BlockSpec((tm, wq), lambda i: (i, 0)),
        compiler_params=_cparams(("arbitrary",)),
        name="conv",
    )(pm, pm, pm, w, b, post)


def _scan_pos(d, s, ncb, ntb):
    rev = jnp.where(s < ncb, ncb - 1 - s, ntb - 1 - (s - ncb))
    return jnp.where(d == 0, s, rev)


def _tri(d):
    r = lax.broadcasted_iota(jnp.int32, (CHUNK, CHUNK), 0)
    c = lax.broadcasted_iota(jnp.int32, (CHUNK, CHUNK), 1)
    return jnp.where(d == 0, r - c, c - r) >= 0


def _chunk_rows(d):
    return [pl.multiple_of(jnp.where(d == 0, j, N_SUB - 1 - j) * CHUNK, CHUNK) for j in range(N_SUB)]


def _gla_kernel(p_ref, wa_ref, ba_ref, o_ref, st_ref):
    d, s, g = pl.program_id(0), pl.program_id(1), pl.program_id(2)
    nbb = p_ref.shape[0]
    b0 = g * nbb

    @pl.when(s == 0)
    def _():
        st_ref[pl.ds(b0, nbb)] = jnp.zeros((nbb,) + st_ref.shape[1:], F32)

    valid = _tri(d)
    tri = valid.astype(F32)
    r0s = _chunk_rows(d)
    wa, ba = wa_ref[0], ba_ref[0]
    inst = [(bb, j) for bb in range(nbb) for j in range(N_SUB)]
    heads = [slice(h * LANE, (h + 1) * LANE) for h in range(GLA_HEADS)]

    la = {}
    for bb, j in inst:
        lr = p_ref[bb, pl.ds(r0s[j], CHUNK), 4 * HP:4 * HP + LANE]
        la[bb, j] = _log_sigmoid(_dot(lr, wa) + ba) * (1.0 / GLA_GATE_NORM)
    bc, e_last = {}, {}
    for i in inst:
        bc[i] = _dot(tri, la[i], precision=HIGHEST)
        e_last[i] = jnp.exp(jnp.sum(la[i], axis=0, keepdims=True))
    q_in, k_in, k_out, v = {}, {}, {}, {}
    for bb, j in inst:
        i = (bb, j)
        rs = pl.ds(r0s[j], CHUNK)
        q_in[i] = (p_ref[bb, rs, 0:HP] * (GLA_DK ** -0.5) * jnp.exp(bc[i])).astype(BF16)
        kd = p_ref[bb, rs, HP:2 * HP] * jnp.exp(-bc[i])
        k_out[i] = (kd * e_last[i]).astype(BF16)
        k_in[i] = kd.astype(BF16)
        v[i] = p_ref[bb, rs, 2 * HP:3 * HP].astype(BF16)
    att = {}
    for i in inst:
        for h, sl in enumerate(heads):
            att[i, h] = jnp.where(valid, _dot_nt(q_in[i][:, sl], k_in[i][:, sl]), 0.0).astype(BF16)
    o_intra, ds = {}, {}
    for i in inst:
        for h, sl in enumerate(heads):
            o_intra[i, h] = _dot(att[i, h], v[i][:, sl])
            ds[i, h] = _dot_tn(v[i][:, sl], k_out[i][:, sl])
    s_in = {}
    for bb in range(nbb):
        for h, sl in enumerate(heads):
            st = st_ref[b0 + bb, h]
            for j in range(N_SUB):
                s_in[(bb, j), h] = st.astype(BF16)
                st = st * e_last[bb, j][:, sl] + ds[(bb, j), h]
            st_ref[b0 + bb, h] = st
    for bb, j in inst:
        for h, sl in enumerate(heads):
            o = o_intra[(bb, j), h] + _dot_nt(q_in[bb, j][:, sl], s_in[(bb, j), h])
            o_ref[0, bb, pl.ds(r0s[j], CHUNK), sl] = o


def _gla(pg3, wa, ba, *, rows):
    nb, l, _ = pg3.shape
    nbb = SCAN_BATCHES
    pos = functools.partial(_scan_pos, ncb=rows.ncb, ntb=rows.ntb)
    return pl.pallas_call(
        _gla_kernel,
        out_shape=jax.ShapeDtypeStruct((2, nb, l, HP), F32),
        grid=(2, rows.ntb, nb // nbb),
        in_specs=[pl.BlockSpec((nbb, SEG, NG), lambda d, s, g: (g, pos(d, s), 0)),
                  pl.BlockSpec((1, LANE, HP), lambda d, s, g: (d, 0, 0)),
                  pl.BlockSpec((1, 1, HP), lambda d, s, g: (d, 0, 0))],
        out_specs=pl.BlockSpec((1, nbb, SEG, HP), lambda d, s, g: (d, g, pos(d, s), 0)),
        scratch_shapes=[pltpu.VMEM((nb, GLA_HEADS, LANE, LANE), F32)],
        compiler_params=_cparams(("arbitrary", "arbitrary", "arbitrary")),
        name="gla_scan",
    )(pg3, wa, ba)


def _mlstm_kernel(qk_ref, v_ref, g_ref, gt_ref, gbr_ref, gbc_ref, o_ref, st_ref, m_ref):
    d, s, g = pl.program_id(0), pl.program_id(1), pl.program_id(2)
    nbb = qk_ref.shape[0]
    b0 = g * nbb

    @pl.when(s == 0)
    def _():
        st_ref[pl.ds(b0, nbb)] = jnp.zeros((nbb,) + st_ref.shape[1:], F32)
        m_ref[pl.ds(b0, nbb)] = jnp.zeros((nbb,) + m_ref.shape[1:], F32)

    fwd = d == 0
    valid = _tri(d)
    tri = valid.astype(F32)
    r0s = _chunk_rows(d)
    cidx = [jnp.where(fwd, j, N_SUB - 1 - j) for j in range(N_SUB)]
    lane = lax.broadcasted_iota(jnp.int32, (CHUNK, LANE), 1)
    inst = [(bb, j) for bb in range(nbb) for j in range(N_SUB)]
    hinst = [(bb, j, h) for bb, j in inst for h in range(ML_HEADS)]
    gbr, gbc = gbr_ref[...], gbc_ref[...]

    def pick(a, h, kind, axis):
        i0, i1 = kind * ML_HEADS + h, (2 + kind) * ML_HEADS + h
        if axis == 1:
            return jnp.where(fwd, a[:, i0:i0 + 1], a[:, i1:i1 + 1])
        return jnp.where(fwd, a[i0:i0 + 1, :], a[i1:i1 + 1, :])

    gc, gr, fcum_c, fcum_r = {}, {}, {}, {}
    for bb, j in inst:
        i = (bb, j)
        gc[i] = g_ref[bb, pl.ds(r0s[j], CHUNK), :] + gbr
        gr[i] = gt_ref[bb, cidx[j]] + gbc
    for i in inst:
        fcum_c[i] = _dot(tri, _log_sigmoid(gc[i]), precision=HIGHEST)
        fcum_r[i] = _dot_nt(_log_sigmoid(gr[i]), tri, precision=HIGHEST)
    fc, lic, d_log, rmax = {}, {}, {}, {}
    for bb, j, h in hinst:
        i = (bb, j)
        fc[bb, j, h] = pick(fcum_c[i], h, 1, 1)
        lic[bb, j, h] = pick(gc[i], h, 0, 1)
        dl = jnp.where(valid, fc[bb, j, h] - pick(fcum_r[i], h, 1, 0) + pick(gr[i], h, 0, 0), NEG)
        d_log[bb, j, h] = dl
        rmax[bb, j, h] = jnp.max(dl, axis=-1, keepdims=True)
    m_prev, m_t, m_new, f_tot = {}, {}, {}, {}
    for bb in range(nbb):
        for h in range(ML_HEADS):
            m = m_ref[b0 + bb, h:h + 1, 0:1]
            for j in range(N_SUB):
                i = (bb, j, h)
                m_prev[i] = m
                m_t[i] = jnp.maximum(fc[i] + m, rmax[i])
                m = jnp.where(fwd, m_t[i][CHUNK - 1:CHUNK], m_t[i][0:1])
                m_new[i] = m
                f_tot[i] = jnp.where(fwd, fc[i][CHUNK - 1:CHUNK], fc[i][0:1])
            m_ref[b0 + bb, h:h + 1, :] = jnp.broadcast_to(m, (1, LANE))
    q, v, qk, w_prev, gdec, ds = {}, {}, {}, {}, {}, {}
    for bb, j, h in hinst:
        i = (bb, j, h)
        rs = pl.ds(r0s[j], CHUNK)
        sl = slice(h * LANE, (h + 1) * LANE)
        q[i] = qk_ref[bb, rs, sl].astype(BF16)
        k = qk_ref[bb, rs, HP + h * LANE:HP + (h + 1) * LANE]
        v[i] = jnp.where(lane == ML_DH, 1.0, v_ref[bb, rs, sl]).astype(BF16)
        w = jnp.exp(d_log[i] - m_t[i])
        w_prev[i] = jnp.exp(fc[i] + m_prev[i] - m_t[i])
        qk[i] = (_dot_nt(q[i], k.astype(BF16)) * w).astype(BF16)
        w_s = jnp.exp(f_tot[i] - fc[i] + lic[i] - m_new[i])
        gdec[i] = jnp.exp(f_tot[i] + m_prev[i] - m_new[i])
        ds[i] = _dot_tn(v[i], (k * w_s).astype(BF16))
    s_in = {}
    for bb in range(nbb):
        for h in range(ML_HEADS):
            st = st_ref[b0 + bb, h]
            for j in range(N_SUB):
                i = (bb, j, h)
                s_in[i] = st.astype(BF16)
                st = gdec[i] * st + ds[i]
            st_ref[b0 + bb, h] = st
    for bb, j, h in hinst:
        i = (bb, j, h)
        num = w_prev[i] * _dot_nt(q[i], s_in[i]) + _dot(qk[i], v[i])
        den = num[:, ML_DH:ML_DH + 1]
        hh = num / jnp.maximum(jnp.abs(den), jnp.exp(-m_t[i]))
        o_ref[0, bb, pl.ds(r0s[j], CHUNK), h * LANE:(h + 1) * LANE] = jnp.where(lane < ML_DH, hh, 0.0)


def _mlstm(qk3, pm3, gt4, gbr, gbc, *, rows):
    nb, l, _ = qk3.shape
    nbb = SCAN_BATCHES
    pos = functools.partial(_scan_pos, ncb=rows.ncb, ntb=rows.ntb)
    return pl.pallas_call(
        _mlstm_kernel,
        out_shape=jax.ShapeDtypeStruct((2, nb, l, HP), F32),
        grid=(2, rows.ntb, nb // nbb),
        in_specs=[pl.BlockSpec((nbb, SEG, 2 * HP), lambda d, s, g: (g, pos(d, s), 0)),
                  pl.BlockSpec((nbb, SEG, HP), lambda d, s, g: (g, pos(d, s), 2)),
                  pl.BlockSpec((nbb, SEG, LANE), lambda d, s, g: (g, pos(d, s), 4 * GLA_HEADS)),
                  pl.BlockSpec((nbb, N_SUB, 16, CHUNK), lambda d, s, g: (g, pos(d, s), 0, 0)),
                  pl.BlockSpec((1, LANE), lambda d, s, g: (0, 0)),
                  pl.BlockSpec((16, CHUNK), lambda d, s, g: (0, 0))],
        out_specs=pl.BlockSpec((1, nbb, SEG, HP), lambda d, s, g: (d, g, pos(d, s), 0)),
        scratch_shapes=[pltpu.VMEM((nb, ML_HEADS, LANE, LANE), F32), pltpu.VMEM((nb, 8, LANE), F32)],
        compiler_params=_cparams(("arbitrary", "arbitrary", "arbitrary")),
        name="mlstm_scan",
    )(qk3, pm3, pm3, gt4, gbr, gbc)


def _s5_kernel(nb, u_ref, bre_ref, bim_ref, are_ref, aim_ref, cre_ref, cim_ref, o_ref, xr_ref, xi_ref, st_ref):
    d, s = pl.program_id(0), pl.program_id(1)

    @pl.when(s == 0)
    def _():
        st_ref[...] = jnp.zeros(st_ref.shape, F32)

    u = u_ref[...].astype(BF16)
    xr_ref[...] = _dot(u, bre_ref[0])
    xi_ref[...] = _dot(u, bim_ref[0])
    ar, ai = are_ref[0], aim_ref[0]
    n_t = u_ref.shape[0] // nb

    def step(j, carry):
        sr, si = carry
        t = jnp.where(d == 0, j, n_t - 1 - j)
        r0 = pl.multiple_of(t * nb, nb)
        nr = ar * sr - ai * si + xr_ref[pl.ds(r0, nb), :]
        ni = ar * si + ai * sr + xi_ref[pl.ds(r0, nb), :]
        xr_ref[pl.ds(r0, nb), :] = nr
        xi_ref[pl.ds(r0, nb), :] = ni
        return nr, ni

    sr, si = lax.fori_loop(0, n_t, step, (st_ref[0], st_ref[1]), unroll=4)
    st_ref[0] = sr
    st_ref[1] = si
    o_ref[0] = _dot(xr_ref[...].astype(BF16), cre_ref[...]) - _dot(xi_ref[...].astype(BF16), cim_ref[...])


def _s5(ut, bre, bim, are, aim, cre, cim, *, nb, nc, nt):
    n_rows, ch = ut.shape
    tr = CHUNK * nb
    ns = bre.shape[-1]
    pos = functools.partial(_scan_pos, ncb=nc, ntb=nt)
    return pl.pallas_call(
        functools.partial(_s5_kernel, nb),
        out_shape=jax.ShapeDtypeStruct((2, n_rows, ch), F32),
        grid=(2, nt),
        in_specs=[pl.BlockSpec((tr, ch), lambda d, s: (pos(d, s), 0)),
                  pl.BlockSpec((1, ch, ns), lambda d, s: (d, 0, 0)),
                  pl.BlockSpec((1, ch, ns), lambda d, s: (d, 0, 0)),
                  pl.BlockSpec((1, nb, ns), lambda d, s: (d, 0, 0)),
                  pl.BlockSpec((1, nb, ns), lambda d, s: (d, 0, 0)),
                  pl.BlockSpec((ns, ch), lambda d, s: (0, 0)),
                  pl.BlockSpec((ns, ch), lambda d, s: (0, 0))],
        out_specs=pl.BlockSpec((1, tr, ch), lambda d, s: (d, pos(d, s), 0)),
        scratch_shapes=[pltpu.VMEM((tr, ns), F32), pltpu.VMEM((tr, ns), F32), pltpu.VMEM((2, nb, ns), F32)],
        compiler_params=_cparams(("arbitrary", "arbitrary")),
        name="s5_scan",
    )(ut, bre, bim, are, aim, cre, cim)


def _head_norm(o, gain, dim):
    parts = []
    for h in range(o.shape[1] // LANE):
        seg = o[:, h * LANE:(h + 1) * LANE]
        ms = jnp.sum(seg * seg, axis=-1, keepdims=True) * (1.0 / dim)
        parts.append(seg * lax.rsqrt(ms + EPS))
    return jnp.concatenate(parts, axis=1) * gain


def _mix_kernel(with_router, og_ref, gg_ref, ys_ref, u_ref, om_ref, mo_ref, h_ref, mod_ref, gn_ref, mn_ref,
                sd_ref, gw_ref, gb_ref, wo_ref, n2_ref, *rest):
    if with_router:
        wr_ref, ho_ref, f_ref, rt_ref = rest
    else:
        ho_ref, f_ref = rest
    gla = _head_norm(og_ref[0] + og_ref[1], gn_ref[...], GLA_DV) * _silu(gg_ref[...])
    z = _gelu_tanh(ys_ref[0] + ys_ref[1] + sd_ref[...] * u_ref[...])
    s5 = z * jax.nn.sigmoid(_dot(z.astype(BF16), gw_ref[...]) + gb_ref[...])
    ml = _head_norm(om_ref[0] + om_ref[1], mn_ref[...], ML_DH) * jax.nn.sigmoid(mo_ref[...])
    mix = (_dot(gla.astype(BF16), wo_ref[0:HP]) + _dot(s5.astype(BF16), wo_ref[HP:HP + 2 * LANE])
           + _dot(ml.astype(BF16), wo_ref[HP + 2 * LANE:]))
    m = mod_ref[0]
    hn = h_ref[...] + m[2:3] * mix
    ho_ref[...] = hn
    f = _rmsnorm(hn, n2_ref[...]) * (1.0 + m[4:5]) + m[3:4]
    f_ref[...] = f
    if with_router:
        logits = _dot(f, wr_ref[...], precision=HIGHEST)
        lane = lax.broadcasted_iota(jnp.int32, logits.shape, 1)
        l0 = jnp.where(lane < N_EXPERTS, logits, NEG)
        m1 = jnp.max(l0, axis=-1, keepdims=True)
        i1 = jnp.min(jnp.where(l0 == m1, lane, LANE), axis=-1, keepdims=True)
        l1 = jnp.where(lane == i1, NEG, l0)
        m2 = jnp.max(l1, axis=-1, keepdims=True)
        i2 = jnp.min(jnp.where(l1 == m2, lane, LANE), axis=-1, keepdims=True)
        e = jnp.exp(m2 - m1)
        w1 = 1.0 / (1.0 + e)
        w2 = e / (1.0 + e)
        rt_ref[...] = jnp.where(lane == 0, i1.astype(F32),
                                jnp.where(lane == 1, i2.astype(F32),
                                          jnp.where(lane == 2, w1, jnp.where(lane == 3, w2, 0.0))))


def _mix(og, pg, ys, pu, om, pm, h, modtab, gn, mn, sd, gw, gb, wo, n2, wr, *, layer, rows, lat_only):
    d = h.shape[1]
    tm = SEG
    n = rows.n_blocks(lat_only)
    src, sel = rows.src(lat_only), rows.sel(lat_only)
    full = lambda a: pl.BlockSpec(a.shape, lambda i: (0,) * a.ndim)
    with_router = wr is not None
    in_specs = [pl.BlockSpec((2, tm, HP), lambda i: (0, src(i), 0)),
                pl.BlockSpec((tm, HP), lambda i: (src(i), 3)),
                pl.BlockSpec((2, tm, 2 * LANE), lambda i: (0, src(i), 0)),
                pl.BlockSpec((tm, 2 * LANE), lambda i: (src(i), 0)),
                pl.BlockSpec((2, tm, HP), lambda i: (0, src(i), 0)),
                pl.BlockSpec((tm, HP), lambda i: (src(i), 3)),
                pl.BlockSpec((tm, d), lambda i: (src(i), 0)),
                pl.BlockSpec((None, 1, 8, d), lambda i: (layer, sel(i), 0, 0)),
                full(gn), full(mn), full(sd), full(gw), full(gb), full(wo), full(n2)]
    args = [og, pg, ys, pu, om, pm, h, modtab, gn, mn, sd, gw, gb, wo, n2]
    out_shape = [jax.ShapeDtypeStruct((n * tm, d), F32), jax.ShapeDtypeStruct((n * tm, d), F32)]
    out_specs = [pl.BlockSpec((tm, d), lambda i: (i, 0)), pl.BlockSpec((tm, d), lambda i: (i, 0))]
    if with_router:
        in_specs.append(full(wr))
        args.append(wr)
        out_shape.append(jax.ShapeDtypeStruct((n * tm, LANE), F32))
        out_specs.append(pl.BlockSpec((tm, LANE), lambda i: (i, 0)))
    return pl.pallas_call(
        functools.partial(_mix_kernel, with_router),
        out_shape=tuple(out_shape),
        grid=(n,),
        in_specs=in_specs,
        out_specs=tuple(out_specs),
        compiler_params=_cparams(("arbitrary",)),
        name="mix_out",
    )(*args)


FF_TILE = 256


def _swiglu_acc(xb, w1_ref, w3_ref, w2_ref, lead=()):
    dff = w1_ref.shape[-1]
    acc = None
    for j in range(dff // FF_TILE):
        sl = slice(j * FF_TILE, (j + 1) * FF_TILE)
        h1 = _dot(xb, w1_ref[lead + (slice(None), sl)])
        h3 = _dot(xb, w3_ref[lead + (slice(None), sl)])
        a = (_silu(h1) * h3).astype(BF16)
        t = _dot(a, w2_ref[lead + (sl, slice(None))])
        acc = t if acc is None else acc + t
    return acc


def _ffn_kernel(final, f_ref, h_ref, mod_ref, w1_ref, w3_ref, w2_ref, *rest):
    y = _swiglu_acc(f_ref[...].astype(BF16), w1_ref, w3_ref, w2_ref)
    hn = h_ref[...] + mod_ref[0][5:6] * y
    if final:
        nf_ref, o_ref = rest
        o_ref[...] = _rmsnorm(hn, nf_ref[...])
    else:
        (o_ref,) = rest
        o_ref[...] = hn


def _ffn(f, h, modtab, w1, w3, w2, nf, *, layer, rows, lat_only):
    r, d = h.shape
    tm = SEG
    sel = rows.sel(lat_only)
    full = lambda a: pl.BlockSpec(a.shape, lambda i: (0,) * a.ndim)
    in_specs = [pl.BlockSpec((tm, d), lambda i: (i, 0)),
                pl.BlockSpec((tm, d), lambda i: (i, 0)),
                pl.BlockSpec((None, 1, 8, d), lambda i: (layer, sel(i), 0, 0)),
                full(w1), full(w3), full(w2)]
    args = [f, h, modtab, w1, w3, w2]
    if lat_only:
        in_specs.append(full(nf))
        args.append(nf)
    return pl.pallas_call(
        functools.partial(_ffn_kernel, lat_only),
        out_shape=jax.ShapeDtypeStruct((r, d), F32),
        grid=(r // tm,),
        in_specs=in_specs,
        out_specs=pl.BlockSpec((tm, d), lambda i: (i, 0)),
        compiler_params=_cparams(("arbitrary",)),
        name="ffn",
    )(*args)


MOE_TM = 256


def _moe_kernel(be_ref, x_ref, rw_ref, w1_ref, w3_ref, w2_ref, o_ref):
    y = _swiglu_acc(x_ref[...].astype(BF16), w1_ref, w3_ref, w2_ref, lead=(0,))
    o_ref[...] = y * rw_ref[:, 0:1]


def _moe_experts(block_expert, xg, rw, w1, w3, w2):
    n_rows, d = xg.shape
    dff = w1.shape[-1]
    tm = MOE_TM
    return pl.pallas_call(
        _moe_kernel,
        out_shape=jax.ShapeDtypeStruct((n_rows, d), F32),
        grid_spec=pltpu.PrefetchScalarGridSpec(
            num_scalar_prefetch=1,
            grid=(n_rows // tm,),
            in_specs=[pl.BlockSpec((tm, d), lambda i, be: (i, 0)),
                      pl.BlockSpec((tm, LANE), lambda i, be: (i, 0)),
                      pl.BlockSpec((1, d, dff), lambda i, be: (be[i], 0, 0)),
                      pl.BlockSpec((1, d, dff), lambda i, be: (be[i], 0, 0)),
                      pl.BlockSpec((1, dff, d), lambda i, be: (be[i], 0, 0))],
            out_specs=pl.BlockSpec((tm, d), lambda i, be: (i, 0))),
        compiler_params=_cparams(("arbitrary",)),
        name="moe_experts",
    )(block_expert, xg, rw, w1, w3, w2)


def _resid_kernel(final, h_ref, y0_ref, y1_ref, mod_ref, *rest):
    hn = h_ref[...] + mod_ref[0][5:6] * (y0_ref[...] + y1_ref[...])
    if final:
        nf_ref, o_ref = rest
        o_ref[...] = _rmsnorm(hn, nf_ref[...])
    else:
        (o_ref,) = rest
        o_ref[...] = hn


def _moe_resid(h, y0, y1, modtab, nf, *, layer, rows, lat_only):
    r, d = h.shape
    tm = SEG
    sel = rows.sel(lat_only)
    row = lambda i: (i, 0)
    in_specs = [pl.BlockSpec((tm, d), row), pl.BlockSpec((tm, d), row), pl.BlockSpec((tm, d), row),
                pl.BlockSpec((None, 1, 8, d), lambda i: (layer, sel(i), 0, 0))]
    args = [h, y0, y1, modtab]
    if lat_only:
        in_specs.append(pl.BlockSpec((1, d), lambda i: (0, 0)))
        args.append(nf)
    return pl.pallas_call(
        functools.partial(_resid_kernel, lat_only),
        out_shape=jax.ShapeDtypeStruct((r, d), F32),
        grid=(r // tm,),
        in_specs=in_specs,
        out_specs=pl.BlockSpec((tm, d), row),
        compiler_params=_cparams(("arbitrary",)),
        name="moe_resid",
    )(*args)


def _pad_heads(w, heads, dim):
    lead = w.shape[:-1]
    w = w.reshape(lead + (heads, dim))
    w = jnp.pad(w, [(0, 0)] * len(lead) + [(0, 0), (0, LANE - dim)])
    return w.reshape(lead + (heads * LANE,))


def _pad_last(w, to):
    return jnp.pad(w, [(0, 0)] * (w.ndim - 1) + [(0, to - w.shape[-1])])


def _pos_embed(n_tokens, d):
    n_grid_rows = n_tokens // GRID_W
    row, col = jnp.meshgrid(jnp.arange(n_grid_rows, dtype=F32), jnp.arange(GRID_W, dtype=F32), indexing='ij')
    n_freq = d // 4
    omega = jnp.exp(-math.log(POS_BASE) * jnp.arange(n_freq, dtype=F32) / n_freq)

    def axis_embed(p):
        ang = p.reshape(-1, 1) * omega
        return jnp.concatenate([jnp.sin(ang), jnp.cos(ang)], axis=-1)

    return jnp.concatenate([axis_embed(row), axis_embed(col)], axis=-1)


def _s5_discretise(lam_re, lam_im, log_dt, b_re, b_im):
    dt = jnp.exp(log_dt)[:, None]
    mag = jnp.exp(lam_re * dt)
    abar_re, abar_im = mag * jnp.cos(lam_im * dt), mag * jnp.sin(lam_im * dt)
    den = lam_re * lam_re + lam_im * lam_im
    pr, pi = abar_re - 1.0, abar_im
    coef_re = (pr * lam_re + pi * lam_im) / den
    coef_im = (pi * lam_re - pr * lam_im) / den
    bbar_re = coef_re[..., None] * b_re - coef_im[..., None] * b_im
    bbar_im = coef_re[..., None] * b_im + coef_im[..., None] * b_re
    return abar_re, abar_im, bbar_re, bbar_im


def _block_diag(m):
    g, a, b = m.shape
    eye = jnp.eye(g, dtype=m.dtype)
    return (eye[:, None, :, None] * m[:, :, None, :]).reshape(g * a, g * b)


def _route_plan(route, tm):
    n_tok = route.shape[0]
    n_assign = n_tok * TOP_K
    flat_e = route[:, 0:TOP_K].astype(jnp.int32).T.reshape(-1)
    flat_w = route[:, TOP_K:2 * TOP_K].T.reshape(-1)
    onehot = (jnp.arange(N_EXPERTS, dtype=jnp.int32)[:, None] == flat_e[None, :]).astype(jnp.int32)
    csum = jnp.cumsum(onehot, axis=1)
    counts = csum[:, -1]
    padded = (counts + tm - 1) // tm * tm
    pend = jnp.cumsum(padded)
    pstart = pend - padded
    dest = jnp.sum(onehot * (csum - 1 + pstart[:, None]), axis=0)
    n_blocks = -(-n_assign // tm) + N_EXPERTS
    n_rows = n_blocks * tm
    row_assign = jnp.full((n_rows,), -1, jnp.int32).at[dest].set(jnp.arange(n_assign, dtype=jnp.int32))
    live = row_assign >= 0
    ra = jnp.maximum(row_assign, 0)
    row_token = ra % n_tok
    row_w = jnp.where(live, flat_w[ra], 0.0)
    block_expert = jnp.minimum(
        jnp.searchsorted(pend, jnp.arange(n_blocks) * tm, side='right'), N_EXPERTS - 1).astype(jnp.int32)
    return row_token, row_w, block_expert, dest.reshape(TOP_K, n_tok)


def kernel(x, c, ctx, c_ctx, w_ada, b_ada, norm1, norm2, w_in, w_out, gla_wa2, gla_ba, gla_norm, s5_lam_re, s5_lam_im, s5_log_dt, s5_b_re, s5_b_im, s5_c_re, s5_c_im, s5_d, s5_glu_w, s5_glu_b, ml_conv_w, ml_conv_b, ml_gate_b, ml_norm, ffn_w1, ffn_w3, ffn_w2, moe_router, moe_w1, moe_w3, moe_w2, norm_f):
    nb, n_lat, d = x.shape
    lc = ctx.shape[1]
    depth = w_ada.shape[0]
    l = lc + n_lat
    assert lc % SEG == 0 and n_lat % SEG == 0 and nb == 8 and nb % SCAN_BATCHES == 0
    rows = _Rows(nb, lc // SEG, l // SEG)

    h = _embed(ctx.reshape(nb * lc, d), x.reshape(nb * n_lat, d), _pos_embed(n_lat, d), rows)

    cond = jnp.zeros((16, d), F32).at[:nb].set(c).at[nb].set(c_ctx)
    mod = _modulation(cond, w_ada, b_ada)
    modtab = jnp.pad(mod.reshape(depth, 16, 6, d), ((0, 0), (0, 0), (0, 2), (0, 0)))

    dk, dv, dh = GLA_HEADS * GLA_DK, GLA_HEADS * GLA_DV, ML_HEADS * ML_DH
    s5c = s5_d.shape[-1]
    cuts = np.cumsum([dk, dk, dv, GLA_RANK, dv, s5c, dh, dh, dh, dh, 4 * ML_HEADS])

    for i in range(depth):
        last = i == depth - 1
        gq, gk, gv, glr, gg, su, mq, mk, mv, mo, mg = jnp.split(w_in[i], cuts[:-1], axis=-1)
        w_all = jnp.concatenate([
            _pad_heads(gq, GLA_HEADS, GLA_DK), _pad_heads(gk, GLA_HEADS, GLA_DK),
            _pad_heads(gv, GLA_HEADS, GLA_DV), _pad_heads(gg, GLA_HEADS, GLA_DV), _pad_last(glr, LANE),
            su,
            _pad_heads(mq, ML_HEADS, ML_DH), _pad_heads(mk, ML_HEADS, ML_DH),
            _pad_heads(mv, ML_HEADS, ML_DH), _pad_heads(mo, ML_HEADS, ML_DH), _pad_last(mg, LANE)],
            axis=-1).astype(BF16)
        wgt = mg.T.astype(BF16)
        pg, pu, pm, gt = _proj(h, modtab, norm1[i][None], w_all, wgt, layer=i, rows=rows)

        wa = jnp.pad(_pad_heads(gla_wa2[i], GLA_HEADS, GLA_DK), ((0, 0), (0, LANE - GLA_RANK), (0, 0)))
        ba = _pad_heads(gla_ba[i], GLA_HEADS, GLA_DK)[:, None, :]
        og = _gla(pg.reshape(nb, l, NG), wa, ba, rows=rows).reshape(2, nb * l, HP)

        bres, bims, ares, aims = [], [], [], []
        for dr in (0, 1):
            a_re, a_im, b_re, b_im = _s5_discretise(s5_lam_re[i, dr], s5_lam_im[i, dr], s5_log_dt[i, dr],
                                                    s5_b_re[i], s5_b_im[i])
            bres.append(_block_diag(jnp.swapaxes(b_re, 1, 2)))
            bims.append(_block_diag(jnp.swapaxes(b_im, 1, 2)))
            ares.append(jnp.broadcast_to(a_re.reshape(1, -1), (nb, a_re.size)))
            aims.append(jnp.broadcast_to(a_im.reshape(1, -1), (nb, a_im.size)))
        cre = _block_diag(jnp.swapaxes(s5_c_re[i], 1, 2)).astype(BF16)
        cim = _block_diag(jnp.swapaxes(s5_c_im[i], 1, 2)).astype(BF16)
        ut = pu.reshape(nb, l, s5c).swapaxes(0, 1).reshape(l * nb, s5c)
        yt = _s5(ut, jnp.stack(bres).astype(BF16), jnp.stack(bims).astype(BF16), jnp.stack(ares), jnp.stack(aims),
                 cre, cim, nb=nb, nc=lc // CHUNK, nt=l // CHUNK)
        ys = yt.reshape(2, l, nb, s5c).swapaxes(1, 2).reshape(2, nb * l, s5c)

        cw = jnp.concatenate([_pad_heads(ml_conv_w[i][:, :dh], ML_HEADS, ML_DH),
                              _pad_heads(ml_conv_w[i][:, dh:], ML_HEADS, ML_DH)], axis=-1)
        cw = jnp.pad(cw, ((0, 8 - ML_CONV), (0, 0)))
        cb = jnp.concatenate([_pad_heads(ml_conv_b[i][:dh], ML_HEADS, ML_DH),
                              _pad_heads(ml_conv_b[i][dh:], ML_HEADS, ML_DH)])[None]
        post = jnp.concatenate([jnp.ones((HP,), F32), jnp.full((HP,), ML_DH ** -0.5, F32)])[None]
        qk = _conv(pm, cw, cb, post, rows=rows)
        gb = ml_gate_b[i].reshape(-1)
        gbr = _pad_last(gb, LANE)[None]
        gbc = jnp.broadcast_to(gb[:, None], (16, CHUNK))
        gt4 = gt.reshape(16, nb, l // CHUNK, CHUNK).transpose(1, 2, 0, 3)
        om = _mlstm(qk.reshape(nb, l, 2 * HP), pm.reshape(nb, l, NM), gt4, gbr, gbc,
                    rows=rows).reshape(2, nb * l, HP)

        wo = w_out[i]
        wo_p = jnp.concatenate([
            jnp.pad(wo[:dv].reshape(GLA_HEADS, GLA_DV, d), ((0, 0), (0, LANE - GLA_DV), (0, 0))).reshape(HP, d),
            wo[dv:dv + s5c],
            jnp.pad(wo[dv + s5c:].reshape(ML_HEADS, ML_DH, d), ((0, 0), (0, LANE - ML_DH), (0, 0))).reshape(HP, d)],
            axis=0).astype(BF16)
        gn = jnp.tile(_pad_last(gla_norm[i], LANE), GLA_HEADS)[None]
        mn = jnp.tile(_pad_last(ml_norm[i], LANE), ML_HEADS)[None]
        is_moe = i % 2 == 1
        j = i // 2
        wr = _pad_last(moe_router[j], LANE) if is_moe else None
        outs = _mix(og, pg, ys, pu, om, pm, h, modtab, gn, mn, s5_d[i][None], s5_glu_w[i].astype(BF16),
                    s5_glu_b[i][None], wo_p, norm2[i][None], wr, layer=i, rows=rows, lat_only=last)
        if not is_moe:
            h, f = outs
            h = _ffn(f, h, modtab, ffn_w1[j].astype(BF16), ffn_w3[j].astype(BF16), ffn_w2[j].astype(BF16),
                     norm_f[None], layer=i, rows=rows, lat_only=last)
        else:
            h, f, route = outs
            row_token, row_w, block_expert, dest = _route_plan(route, MOE_TM)
            xg = jnp.take(f, row_token, axis=0)
            rw = jnp.broadcast_to(row_w[:, None], (row_w.shape[0], LANE))
            yg = _moe_experts(block_expert, xg, rw, moe_w1[j].astype(BF16), moe_w3[j].astype(BF16),
                              moe_w2[j].astype(BF16))
            y0 = jnp.take(yg, dest[0], axis=0)
            y1 = jnp.take(yg, dest[1], axis=0)
            h = _moe_resid(h, y0, y1, modtab, norm_f[None], layer=i, rows=rows, lat_only=last)
    return h.reshape(nb, n_lat, d)
```

```python
import functools
import math

import numpy as np
import jax
import jax.numpy as jnp
from jax import lax
from jax.experimental import pallas as pl
from jax.experimental.pallas import tpu as pltpu

F32 = jnp.float32
BF16 = jnp.bfloat16
HIGHEST = lax.Precision.HIGHEST

GRID_W = 64
POS_BASE = 10000.0
EPS = 1e-6
GLA_HEADS, GLA_DK, GLA_DV, GLA_RANK, GLA_GATE_NORM = 4, 48, 96, 16, 16.0
S5_GROUP, S5_STATE = 16, 64
ML_HEADS, ML_DH, ML_CONV = 4, 96, 3
N_EXPERTS, TOP_K = 8, 2

LANE = 128
CHUNK = 64
SEG = 256
N_SUB = SEG // CHUNK
SCAN_BATCHES = 2
NEG = -1e30
VMEM_LIMIT = 56 * 1024 * 1024

HP = LANE * GLA_HEADS
NG = 4 * HP + LANE
NM = 4 * HP + 2 * LANE


def _cparams(sem):
    return pltpu.CompilerParams(dimension_semantics=sem, vmem_limit_bytes=VMEM_LIMIT)


def _dot(a, b, **kw):
    return jnp.dot(a, b, preferred_element_type=F32, **kw)


def _dot_nt(a, b, **kw):
    return lax.dot_general(a, b, (((1,), (1,)), ((), ())), preferred_element_type=F32, **kw)


def _dot_tn(a, b, **kw):
    return lax.dot_general(a, b, (((0,), (0,)), ((), ())), preferred_element_type=F32, **kw)


def _log_sigmoid(x):
    return jnp.minimum(x, 0.0) - jnp.log1p(jnp.exp(-jnp.abs(x)))


def _silu(x):
    return x * jax.nn.sigmoid(x)


def _gelu_tanh(x):
    return 0.5 * x * (1.0 + jnp.tanh(math.sqrt(2.0 / math.pi) * (x + 0.044715 * (x * x * x))))


def _rmsnorm(x, g):
    return x * lax.rsqrt(jnp.mean(x * x, axis=-1, keepdims=True) + EPS) * g


class _Rows:
    def __init__(self, nb, ncb, ntb):
        self.nb, self.ncb, self.ntb, self.nlb = nb, ncb, ntb, ntb - ncb

    def n_blocks(self, lat_only):
        return self.nb * (self.nlb if lat_only else self.ntb)

    def src(self, lat_only):
        if lat_only:
            return lambda i: (i // self.nlb) * self.ntb + self.ncb + i % self.nlb
        return lambda i: i

    def sel(self, lat_only):
        if lat_only:
            return lambda i: i // self.nlb
        return lambda i: jnp.where(i % self.ntb < self.ncb, self.nb, i // self.ntb)


def _embed_kernel(ncb, ntb, ctx_ref, x_ref, pos_ref, o_ref):
    j = pl.program_id(0) % ntb

    @pl.when(j < ncb)
    def _():
        o_ref[...] = ctx_ref[...]

    @pl.when(j >= ncb)
    def _():
        o_ref[...] = x_ref[...] + pos_ref[...]


def _embed(ctx2, x2, pos, rows):
    d = ctx2.shape[1]
    ncb, ntb, nlb = rows.ncb, rows.ntb, rows.nlb
    return pl.pallas_call(
        functools.partial(_embed_kernel, ncb, ntb),
        out_shape=jax.ShapeDtypeStruct((rows.nb * ntb * SEG, d), F32),
        grid=(rows.nb * ntb,),
        in_specs=[pl.BlockSpec((SEG, d), lambda i: ((i // ntb) * ncb + jnp.minimum(i % ntb, ncb - 1), 0)),
                  pl.BlockSpec((SEG, d), lambda i: ((i // ntb) * nlb + jnp.maximum(i % ntb - ncb, 0), 0)),
                  pl.BlockSpec((SEG, d), lambda i: (jnp.maximum(i % ntb - ncb, 0), 0))],
        out_specs=pl.BlockSpec((SEG, d), lambda i: (i, 0)),
        compiler_params=_cparams(("arbitrary",)),
        name="embed",
    )(ctx2, x2, pos)


def _mod_kernel(c_ref, w_ref, b_ref, o_ref):
    s = _silu(c_ref[...])
    o_ref[0] = _dot(s, w_ref[0], precision=HIGHEST) + b_ref[0]


def _modulation(cond, w_ada, b_ada):
    depth, d, n6 = w_ada.shape
    tn = n6 // 4
    n_rows = cond.shape[0]
    return pl.pallas_call(
        _mod_kernel,
        out_shape=jax.ShapeDtypeStruct((depth, n_rows, n6), F32),
        grid=(depth, n6 // tn),
        in_specs=[pl.BlockSpec((n_rows, d), lambda l, j: (0, 0)),
                  pl.BlockSpec((1, d, tn), lambda l, j: (l, 0, j)),
                  pl.BlockSpec((1, 1, tn), lambda l, j: (l, 0, j))],
        out_specs=pl.BlockSpec((1, n_rows, tn), lambda l, j: (l, 0, j)),
        compiler_params=_cparams(("arbitrary", "arbitrary")),
        name="modulation",
    )(cond, w_ada, b_ada.reshape(depth, 1, n6))


def _proj_kernel(h_ref, mod_ref, g_ref, w_ref, wgt_ref, pg_ref, pu_ref, pm_ref, gt_ref):
    m = mod_ref[0]
    ab = (_rmsnorm(h_ref[...], g_ref[...]) * (1.0 + m[1:2]) + m[0:1]).astype(BF16)
    pg_ref[...] = _dot(ab, w_ref[:, 0:NG])
    pu_ref[...] = _dot(ab, w_ref[:, NG:NG + 2 * LANE])
    pm_ref[...] = _dot(ab, w_ref[:, NG + 2 * LANE:])
    gt_ref[...] = _dot_nt(wgt_ref[...], ab)


def _proj(h, modtab, g, w, wgt, *, layer, rows):
    r, d = h.shape
    tm = SEG
    sel = rows.sel(False)
    nw = w.shape[1]
    return pl.pallas_call(
        _proj_kernel,
        out_shape=(jax.ShapeDtypeStruct((r, NG), F32), jax.ShapeDtypeStruct((r, 2 * LANE), F32),
                   jax.ShapeDtypeStruct((r, NM), F32), jax.ShapeDtypeStruct((16, r), F32)),
        grid=(r // tm,),
        in_specs=[pl.BlockSpec((tm, d), lambda i: (i, 0)),
                  pl.BlockSpec((None, 1, 8, d), lambda i: (layer, sel(i), 0, 0)),
                  pl.BlockSpec((1, d), lambda i: (0, 0)),
                  pl.BlockSpec((d, nw), lambda i: (0, 0)),
                  pl.BlockSpec((16, d), lambda i: (0, 0))],
        out_specs=(pl.BlockSpec((tm, NG), lambda i: (i, 0)),
                   pl.BlockSpec((tm, 2 * LANE), lambda i: (i, 0)),
                   pl.BlockSpec((tm, NM), lambda i: (i, 0)),
                   pl.BlockSpec((16, tm), lambda i: (0, i))),
        compiler_params=_cparams(("arbitrary",)),
        name="proj",
    )(h, modtab, g, w, wgt)


def _conv_kernel(ncb, ntb, x_ref, prev_ref, next_ref, w_ref, b_ref, post_ref, o_ref):
    j = pl.program_id(0) % ntb
    x = x_ref[...]
    tm = x.shape[0]
    first = jnp.logical_or(j == 0, j == ncb).astype(F32)
    last = jnp.logical_or(j == ncb - 1, j == ntb - 1).astype(F32)
    row = lax.broadcasted_iota(jnp.int32, x.shape, 0)
    xp = jnp.where(row == 0, prev_ref[7:8, :] * (1.0 - first), pltpu.roll(x, 1, axis=0))
    xn = jnp.where(row == tm - 1, next_ref[0:1, :] * (1.0 - last), pltpu.roll(x, tm - 1, axis=0))
    y = w_ref[0:1] * xp + w_ref[1:2] * x + w_ref[2:3] * xn + b_ref[...]
    o_ref[...] = _silu(y) * post_ref[...]


def _conv(pm, w, b, post, *, rows):
    r = pm.shape[0]
    tm = SEG
    wq = 2 * HP
    t8 = tm // 8
    return pl.pallas_call(
        functools.partial(_conv_kernel, rows.ncb, rows.ntb),
        out_shape=jax.ShapeDtypeStruct((r, wq), F32),
        grid=(r // tm,),
        in_specs=[pl.BlockSpec((tm, wq), lambda i: (i, 0)),
                  pl.BlockSpec((8, wq), lambda i: (jnp.maximum(i * t8 - 1, 0), 0)),
                  pl.BlockSpec((8, wq), lambda i: (jnp.minimum((i + 1) * t8, r // 8 - 1), 0)),
                  pl.BlockSpec((8, wq), lambda i: (0, 0)),
                  pl.BlockSpec((1, wq), lambda i: (0, 0)),
                  pl.BlockSpec((1, wq), lambda i: (0, 0))],
        out_specs=pl.BlockSpec((tm, wq), lambda i: (i, 0)),
        compiler_params=_cparams(("arbitrary",)),
        name="conv",
    )(pm, pm, pm, w, b, post)


def _scan_pos(d, s, ncb, ntb):
    rev = jnp.where(s < ncb, ncb - 1 - s, ntb - 1 - (s - ncb))
    return jnp.where(d == 0, s, rev)


def _scan_pos_static(rev, s, ncb, ntb):
    if not rev:
        return s
    return jnp.where(s < ncb, ncb - 1 - s, ntb - 1 - (s - ncb))


def _tri(rev):
    r = lax.broadcasted_iota(jnp.int32, (CHUNK, CHUNK), 0)
    c = lax.broadcasted_iota(jnp.int32, (CHUNK, CHUNK), 1)
    return (r <= c) if rev else (r >= c)


def _chunk_rows(rev):
    return [(N_SUB - 1 - j if rev else j) * CHUNK for j in range(N_SUB)]


def _gla_kernel(rev, p_ref, wa_ref, ba_ref, o_ref, st_ref):
    s, g = pl.program_id(0), pl.program_id(1)
    nbb = p_ref.shape[0]
    b0 = g * nbb

    @pl.when(s == 0)
    def _():
        st_ref[pl.ds(b0, nbb)] = jnp.zeros((nbb,) + st_ref.shape[1:], F32)

    valid = _tri(rev)
    tri = valid.astype(F32)
    r0s = _chunk_rows(rev)
    wa, ba = wa_ref[...], ba_ref[...]
    inst = [(bb, j) for bb in range(nbb) for j in range(N_SUB)]
    heads = [slice(h * LANE, (h + 1) * LANE) for h in range(GLA_HEADS)]

    la = {}
    for bb, j in inst:
        lr = p_ref[bb, pl.ds(r0s[j], CHUNK), 4 * HP:4 * HP + LANE]
        la[bb, j] = _log_sigmoid(_dot(lr, wa) + ba) * (1.0 / GLA_GATE_NORM)
    bc, e_last = {}, {}
    for i in inst:
        bc[i] = _dot_exact01(tri, la[i], lhs_is_01=True)
        e_last[i] = jnp.exp(jnp.sum(la[i], axis=0, keepdims=True))
    q_in, k_in, k_out, v = {}, {}, {}, {}
    for bb, j in inst:
        i = (bb, j)
        rs = pl.ds(r0s[j], CHUNK)
        q_in[i] = (p_ref[bb, rs, 0:HP] * (GLA_DK ** -0.5) * jnp.exp(bc[i])).astype(BF16)
        kd = p_ref[bb, rs, HP:2 * HP] * jnp.exp(-bc[i])
        k_out[i] = (kd * e_last[i]).astype(BF16)
        k_in[i] = kd.astype(BF16)
        v[i] = p_ref[bb, rs, 2 * HP:3 * HP].astype(BF16)
    att = {}
    for i in inst:
        for h, sl in enumerate(heads):
            att[i, h] = jnp.where(valid, _dot_nt(q_in[i][:, sl], k_in[i][:, sl]), 0.0).astype(BF16)
    o_intra, ds = {}, {}
    for i in inst:
        for h, sl in enumerate(heads):
            o_intra[i, h] = _dot(att[i, h], v[i][:, sl])
            ds[i, h] = _dot_tn(v[i][:, sl], k_out[i][:, sl])
    s_in = {}
    for bb in range(nbb):
        for h, sl in enumerate(heads):
            st = st_ref[b0 + bb, h]
            for j in range(N_SUB):
                s_in[(bb, j), h] = st.astype(BF16)
                st = st * e_last[bb, j][:, sl] + ds[(bb, j), h]
            st_ref[b0 + bb, h] = st
    for bb, j in inst:
        for h, sl in enumerate(heads):
            o = o_intra[(bb, j), h] + _dot_nt(q_in[bb, j][:, sl], s_in[(bb, j), h])
            o_ref[bb, pl.ds(r0s[j], CHUNK), sl] = o


def _gla(pg3, wa, ba, *, rows, rev):
    nb, l, _ = pg3.shape
    nbb = SCAN_BATCHES
    pos = functools.partial(_scan_pos_static, rev, ncb=rows.ncb, ntb=rows.ntb)
    return pl.pallas_call(
        functools.partial(_gla_kernel, rev),
        out_shape=jax.ShapeDtypeStruct((nb, l, HP), F32),
        grid=(rows.ntb, nb // nbb),
        in_specs=[pl.BlockSpec((nbb, SEG, NG), lambda s, g: (g, pos(s), 0)),
                  pl.BlockSpec((LANE, HP), lambda s, g: (0, 0)),
                  pl.BlockSpec((1, HP), lambda s, g: (0, 0))],
        out_specs=pl.BlockSpec((nbb, SEG, HP), lambda s, g: (g, pos(s), 0)),
        scratch_shapes=[pltpu.VMEM((nb, GLA_HEADS, LANE, LANE), F32)],
        compiler_params=_cparams(("arbitrary", "arbitrary")),
        name="gla_scan_bwd" if rev else "gla_scan_fwd",
    )(pg3, wa, ba)


def _mlstm_kernel_old(qk_ref, v_ref, g_ref, gt_ref, gbr_ref, gbc_ref, o_ref, st_ref, m_ref):
    d, s, g = pl.program_id(0), pl.program_id(1), pl.program_id(2)
    nbb = qk_ref.shape[0]
    b0 = g * nbb

    @pl.when(s == 0)
    def _():
        st_ref[pl.ds(b0, nbb)] = jnp.zeros((nbb,) + st_ref.shape[1:], F32)
        m_ref[pl.ds(b0, nbb)] = jnp.zeros((nbb,) + m_ref.shape[1:], F32)

    fwd = d == 0
    valid = _tri(d)
    tri = valid.astype(F32)
    r0s = _chunk_rows(d)
    cidx = [jnp.where(fwd, j, N_SUB - 1 - j) for j in range(N_SUB)]
    lane = lax.broadcasted_iota(jnp.int32, (CHUNK, LANE), 1)
    inst = [(bb, j) for bb in range(nbb) for j in range(N_SUB)]
    hinst = [(bb, j, h) for bb, j in inst for h in range(ML_HEADS)]
    gbr, gbc = gbr_ref[...], gbc_ref[...]

    def pick(a, h, kind, axis):
        i0, i1 = kind * ML_HEADS + h, (2 + kind) * ML_HEADS + h
        if axis == 1:
            return jnp.where(fwd, a[:, i0:i0 + 1], a[:, i1:i1 + 1])
        return jnp.where(fwd, a[i0:i0 + 1, :], a[i1:i1 + 1, :])

    gc, gr, fcum_c, fcum_r = {}, {}, {}, {}
    for bb, j in inst:
        i = (bb, j)
        gc[i] = g_ref[bb, pl.ds(r0s[j], CHUNK), :] + gbr
        gr[i] = gt_ref[bb, cidx[j]] + gbc
    for i in inst:
        fcum_c[i] = _dot(tri, _log_sigmoid(gc[i]), precision=HIGHEST)
        fcum_r[i] = _dot_nt(_log_sigmoid(gr[i]), tri, precision=HIGHEST)
    fc, lic, d_log, rmax = {}, {}, {}, {}
    for bb, j, h in hinst:
        i = (bb, j)
        fc[bb, j, h] = pick(fcum_c[i], h, 1, 1)
        lic[bb, j, h] = pick(gc[i], h, 0, 1)
        dl = jnp.where(valid, fc[bb, j, h] - pick(fcum_r[i], h, 1, 0) + pick(gr[i], h, 0, 0), NEG)
        d_log[bb, j, h] = dl
        rmax[bb, j, h] = jnp.max(dl, axis=-1, keepdims=True)
    m_prev, m_t, m_new, f_tot = {}, {}, {}, {}
    for bb in range(nbb):
        for h in range(ML_HEADS):
            m = m_ref[b0 + bb, h:h + 1, 0:1]
            for j in range(N_SUB):
                i = (bb, j, h)
                m_prev[i] = m
                m_t[i] = jnp.maximum(fc[i] + m, rmax[i])
                m = jnp.where(fwd, m_t[i][CHUNK - 1:CHUNK], m_t[i][0:1])
                m_new[i] = m
                f_tot[i] = jnp.where(fwd, fc[i][CHUNK - 1:CHUNK], fc[i][0:1])
            m_ref[b0 + bb, h:h + 1, :] = jnp.broadcast_to(m, (1, LANE))
    q, v, qk, w_prev, gdec, ds = {}, {}, {}, {}, {}, {}
    for bb, j, h in hinst:
        i = (bb, j, h)
        rs = pl.ds(r0s[j], CHUNK)
        sl = slice(h * LANE, (h + 1) * LANE)
        q[i] = qk_ref[bb, rs, sl].astype(BF16)
        k = qk_ref[bb, rs, HP + h * LANE:HP + (h + 1) * LANE]
        v[i] = jnp.where(lane == ML_DH, 1.0, v_ref[bb, rs, sl]).astype(BF16)
        w = jnp.exp(d_log[i] - m_t[i])
        w_prev[i] = jnp.exp(fc[i] + m_prev[i] - m_t[i])
        qk[i] = (_dot_nt(q[i], k.astype(BF16)) * w).astype(BF16)
        w_s = jnp.exp(f_tot[i] - fc[i] + lic[i] - m_new[i])
        gdec[i] = jnp.exp(f_tot[i] + m_prev[i] - m_new[i])
        ds[i] = _dot_tn(v[i], (k * w_s).astype(BF16))
    s_in = {}
    for bb in range(nbb):
        for h in range(ML_HEADS):
            st = st_ref[b0 + bb, h]
            for j in range(N_SUB):
                i = (bb, j, h)
                s_in[i] = st.astype(BF16)
                st = gdec[i] * st + ds[i]
            st_ref[b0 + bb, h] = st
    for bb, j, h in hinst:
        i = (bb, j, h)
        num = w_prev[i] * _dot_nt(q[i], s_in[i]) + _dot(qk[i], v[i])
        den = num[:, ML_DH:ML_DH + 1]
        hh = num / jnp.maximum(jnp.abs(den), jnp.exp(-m_t[i]))
        o_ref[0, bb, pl.ds(r0s[j], CHUNK), h * LANE:(h + 1) * LANE] = jnp.where(lane < ML_DH, hh, 0.0)


def _mlstm_old(qk3, pm3, gt4, gbr, gbc, *, rows):
    nb, l, _ = qk3.shape
    nbb = SCAN_BATCHES
    pos = functools.partial(_scan_pos, ncb=rows.ncb, ntb=rows.ntb)
    return pl.pallas_call(
        _mlstm_kernel,
        out_shape=jax.ShapeDtypeStruct((2, nb, l, HP), F32),
        grid=(2, rows.ntb, nb // nbb),
        in_specs=[pl.BlockSpec((nbb, SEG, 2 * HP), lambda d, s, g: (g, pos(d, s), 0)),
                  pl.BlockSpec((nbb, SEG, HP), lambda d, s, g: (g, pos(d, s), 2)),
                  pl.BlockSpec((nbb, SEG, LANE), lambda d, s, g: (g, pos(d, s), 4 * GLA_HEADS)),
                  pl.BlockSpec((nbb, N_SUB, 16, CHUNK), lambda d, s, g: (g, pos(d, s), 0, 0)),
                  pl.BlockSpec((1, LANE), lambda d, s, g: (0, 0)),
                  pl.BlockSpec((16, CHUNK), lambda d, s, g: (0, 0))],
        out_specs=pl.BlockSpec((1, nbb, SEG, HP), lambda d, s, g: (d, g, pos(d, s), 0)),
        scratch_shapes=[pltpu.VMEM((nb, ML_HEADS, LANE, LANE), F32), pltpu.VMEM((nb, 8, LANE), F32)],
        compiler_params=_cparams(("arbitrary", "arbitrary", "arbitrary")),
        name="mlstm_scan",
    )(qk3, pm3, pm3, gt4, gbr, gbc)


def _split_bf16(x, n):
    parts, r = [], x
    for _ in range(n):
        p = r.astype(BF16)
        parts.append(p)
        r = r - p.astype(F32)
    return parts


def _dot_exact01(a, b, lhs_is_01):
    if lhs_is_01:
        a = a.astype(BF16)
        terms = [_dot(a, p) for p in _split_bf16(b, 3)]
    else:
        b = b.astype(BF16)
        terms = [_dot(p, b) for p in _split_bf16(a, 3)]
    return terms[0] + terms[1] + terms[2]


def _cummax_rows(a, rev):
    n = a.shape[0]
    row = lax.broadcasted_iota(jnp.int32, a.shape, 0)
    k = 1
    while k < n:
        if rev:
            sh = jnp.where(row < n - k, pltpu.roll(a, n - k, axis=0), NEG)
        else:
            sh = jnp.where(row >= k, pltpu.roll(a, k, axis=0), NEG)
        a = jnp.maximum(a, sh)
        k *= 2
    return a


ML_GL = ML_HEADS


def _mlstm_kernel(rev, qk_ref, v_ref, g_ref, gt_ref, gbr_ref, gbt_ref, o_ref, st_ref, m_ref):
    s, g = pl.program_id(0), pl.program_id(1)
    nbb = qk_ref.shape[0]
    b0 = g * nbb

    @pl.when(s == 0)
    def _():
        st_ref[pl.ds(b0, nbb)] = jnp.zeros((nbb,) + st_ref.shape[1:], F32)
        m_ref[pl.ds(b0, nbb)] = jnp.zeros((nbb,) + m_ref.shape[1:], F32)

    valid = _tri(rev)
    tri = valid.astype(F32)
    r0s = _chunk_rows(rev)
    cs = [r // CHUNK for r in r0s]
    last = 0 if rev else CHUNK - 1
    inst = [(bb, j) for bb in range(nbb) for j in range(N_SUB)]
    heads = [slice(h * LANE, (h + 1) * LANE) for h in range(ML_HEADS)]

    r_sel = lax.broadcasted_iota(jnp.int32, (LANE, HP), 0)
    c_sel = lax.broadcasted_iota(jnp.int32, (LANE, HP), 1)
    sel_h = (r_sel == ML_GL + c_sel // LANE).astype(BF16)
    r_t = lax.broadcasted_iota(jnp.int32, (HP, HP), 0)
    c_t = lax.broadcasted_iota(jnp.int32, (HP, HP), 1)
    same = jnp.logical_and(r_t // LANE == c_t // LANE, jnp.logical_and(r_t % LANE < CHUNK, c_t % LANE < CHUNK))
    before = (r_t % LANE >= c_t % LANE) if rev else (r_t % LANE <= c_t % LANE)
    tri_b = jnp.logical_and(same, before).astype(BF16)
    r_v = lax.broadcasted_iota(jnp.int32, (CHUNK, HP), 0)
    c_v = lax.broadcasted_iota(jnp.int32, (CHUNK, HP), 1) % LANE
    valid4 = jnp.logical_and(c_v < CHUNK, (r_v <= c_v) if rev else (r_v >= c_v))
    lane4 = lax.broadcasted_iota(jnp.int32, (CHUNK, HP), 1) % LANE
    lane_c = lax.broadcasted_iota(jnp.int32, (CHUNK, LANE), 1)
    lane1 = lax.broadcasted_iota(jnp.int32, (1, LANE), 1)
    head_lane = jnp.logical_and(lane1 >= ML_GL, lane1 < ML_GL + ML_HEADS)
    gbr, gbt = gbr_ref[...], gbt_ref[...]

    gcs, fcm, cmx, a_row, grt = {}, {}, {}, {}, {}
    for bb, j in inst:
        gc = g_ref[bb, pl.ds(r0s[j], CHUNK), :] + gbr
        gcs[bb, j] = pltpu.roll(gc, ML_GL, axis=1)
        fcm[bb, j] = _dot_exact01(tri, _log_sigmoid(gc), lhs_is_01=True)
        grt[bb, j] = gt_ref[bb, cs[j]] + gbt
    row_id = lax.broadcasted_iota(jnp.int32, (len(inst), HP), 0)
    lfr = jnp.zeros((len(inst), HP), F32)
    for n, i in enumerate(inst):
        lfr = jnp.where(row_id == n, _log_sigmoid(grt[i][1:2]), lfr)
    fcr = _dot_exact01(lfr, tri_b, lhs_is_01=False)
    for n, i in enumerate(inst):
        a_row[i] = grt[i][0:1] - fcr[n:n + 1]
        cmx[i] = _cummax_rows(gcs[i] - fcm[i], rev)
    bx = {}
    e_neg, gd = {}, {}
    for bb in range(nbb):
        m_prev = m_ref[b0 + bb, 0:1, :]
        for j in range(N_SUB):
            i = (bb, j)
            m_t = fcm[i] + jnp.maximum(m_prev, cmx[i])
            m_new = m_t[last:last + 1]
            f_tot = fcm[i][last:last + 1]
            u = fcm[i] - m_t
            w_prev = jnp.exp(u + m_prev)
            w_s = jnp.exp(f_tot - fcm[i] + gcs[i] - m_new)
            gdec = jnp.broadcast_to(jnp.exp(f_tot + m_prev - m_new), (16, LANE))
            e_neg[i] = jnp.exp(-m_t)
            keep = lambda a: jnp.where(head_lane, a, 0.0)
            bx[i] = jnp.concatenate(_split_bf16(keep(u), 3) + _split_bf16(keep(w_prev), 2)
                                    + _split_bf16(keep(w_s), 2) + _split_bf16(keep(gdec), 2), axis=0)
            m_prev = m_new
        m_ref[b0 + bb] = jnp.broadcast_to(m_prev, (8, LANE))
    ub, wpb, wsb, gdb = {}, {}, {}, {}
    for i in inst:
        y = _dot(bx[i], sel_h)
        c = CHUNK
        ub[i] = y[0:c] + y[c:2 * c] + y[2 * c:3 * c]
        wpb[i] = y[3 * c:4 * c] + y[4 * c:5 * c]
        wsb[i] = y[5 * c:6 * c] + y[6 * c:7 * c]
        gdb[i] = y[7 * c:7 * c + 1] + y[7 * c + 16:7 * c + 17]
    q, v, qkw, ds = {}, {}, {}, {}
    for bb, j in inst:
        i = (bb, j)
        rs = pl.ds(r0s[j], CHUNK)
        w = jnp.where(valid4, jnp.exp(ub[i] + a_row[i]), 0.0)
        q[i] = qk_ref[bb, rs, 0:HP].astype(BF16)
        k = qk_ref[bb, rs, HP:2 * HP]
        kb = k.astype(BF16)
        kw = (k * wsb[i]).astype(BF16)
        v[i] = jnp.where(lane4 == ML_DH, 1.0, v_ref[bb, rs, :]).astype(BF16)
        for h, sl in enumerate(heads):
            sc = _dot_nt(q[i][:, sl], kb[:, sl])
            qkw[i, h] = (sc * w[:, h * LANE:h * LANE + CHUNK]).astype(BF16)
            ds[i, h] = _dot_tn(v[i][:, sl], kw[:, sl])
    s_in = {}
    for bb in range(nbb):
        for h, sl in enumerate(heads):
            st = st_ref[b0 + bb, h]
            for j in range(N_SUB):
                i = (bb, j)
                s_in[i, h] = st.astype(BF16)
                st = gdb[i][:, sl] * st + ds[i, h]
            st_ref[b0 + bb, h] = st
    num = {}
    for i in inst:
        parts = [_dot_nt(q[i][:, sl], s_in[i, h]) for h, sl in enumerate(heads)]
        intra = [_dot(qkw[i, h], v[i][:, sl]) for h, sl in enumerate(heads)]
        num[i] = wpb[i] * jnp.concatenate(parts, axis=1) + jnp.concatenate(intra, axis=1)
    for bb, j in inst:
        i = (bb, j)
        den = jnp.zeros((CHUNK, LANE), F32)
        for h, sl in enumerate(heads):
            dh = jnp.sum(jnp.where(lane_c == ML_DH, num[i][:, sl], 0.0), axis=-1, keepdims=True)
            den = jnp.where(lane_c == ML_GL + h, jnp.broadcast_to(dh, (CHUNK, LANE)), den)
        r1, r2 = _split_bf16(jnp.where(head_lane, 1.0 / jnp.maximum(jnp.abs(den), e_neg[i]), 0.0), 2)
        rb = _dot(r1, sel_h) + _dot(r2, sel_h)
        o_ref[bb, pl.ds(r0s[j], CHUNK), :] = jnp.where(lane4 < ML_DH, num[i] * rb, 0.0)


def _mlstm(qk3, pm3, gtl, gbr, gbt, *, rows, rev):
    nb, l, _ = qk3.shape
    nbb = SCAN_BATCHES
    dr = 1 if rev else 0
    pos = functools.partial(_scan_pos_static, rev, ncb=rows.ncb, ntb=rows.ntb)
    return pl.pallas_call(
        functools.partial(_mlstm_kernel, rev),
        out_shape=jax.ShapeDtypeStruct((nb, l, HP), F32),
        grid=(rows.ntb, nb // nbb),
        in_specs=[pl.BlockSpec((nbb, SEG, 2 * HP), lambda s, g: (g, pos(s), 0)),
                  pl.BlockSpec((nbb, SEG, HP), lambda s, g: (g, pos(s), 2)),
                  pl.BlockSpec((nbb, SEG, LANE), lambda s, g: (g, pos(s), 4 * ML_HEADS + dr)),
                  pl.BlockSpec((nbb, N_SUB, None, 2, HP), lambda s, g: (g, pos(s), dr, 0, 0)),
                  pl.BlockSpec((1, LANE), lambda s, g: (0, 0)),
                  pl.BlockSpec((2, HP), lambda s, g: (0, 0))],
        out_specs=pl.BlockSpec((nbb, SEG, HP), lambda s, g: (g, pos(s), 0)),
        scratch_shapes=[pltpu.VMEM((nb, ML_HEADS, LANE, LANE), F32), pltpu.VMEM((nb, 8, LANE), F32)],
        compiler_params=_cparams(("arbitrary", "arbitrary")),
        name="mlstm_scan_bwd" if rev else "mlstm_scan_fwd",
    )(qk3, pm3, pm3, gtl, gbr, gbt)


def _s5_kernel(nb, u_ref, bre_ref, bim_ref, are_ref, aim_ref, cre_ref, cim_ref, o_ref, xr_ref, xi_ref, st_ref):
    d, s = pl.program_id(0), pl.program_id(1)

    @pl.when(s == 0)
    def _():
        st_ref[...] = jnp.zeros(st_ref.shape, F32)

    u = u_ref[...].astype(BF16)
    xr_ref[...] = _dot(u, bre_ref[0])
    xi_ref[...] = _dot(u, bim_ref[0])
    ar, ai = are_ref[0], aim_ref[0]
    n_t = u_ref.shape[0] // nb

    def step(j, carry):
        sr, si = carry
        t = jnp.where(d == 0, j, n_t - 1 - j)
        r0 = pl.multiple_of(t * nb, nb)
        nr = ar * sr - ai * si + xr_ref[pl.ds(r0, nb), :]
        ni = ar * si + ai * sr + xi_ref[pl.ds(r0, nb), :]
        xr_ref[pl.ds(r0, nb), :] = nr
        xi_ref[pl.ds(r0, nb), :] = ni
        return nr, ni

    sr, si = lax.fori_loop(0, n_t, step, (st_ref[0], st_ref[1]), unroll=4)
    st_ref[0] = sr
    st_ref[1] = si
    o_ref[0] = _dot(xr_ref[...].astype(BF16), cre_ref[...]) - _dot(xi_ref[...].astype(BF16), cim_ref[...])


def _s5(ut, bre, bim, are, aim, cre, cim, *, nb, nc, nt):
    n_rows, ch = ut.shape
    tr = CHUNK * nb
    ns = bre.shape[-1]
    pos = functools.partial(_scan_pos, ncb=nc, ntb=nt)
    return pl.pallas_call(
        functools.partial(_s5_kernel, nb),
        out_shape=jax.ShapeDtypeStruct((2, n_rows, ch), F32),
        grid=(2, nt),
        in_specs=[pl.BlockSpec((tr, ch), lambda d, s: (pos(d, s), 0)),
                  pl.BlockSpec((1, ch, ns), lambda d, s: (d, 0, 0)),
                  pl.BlockSpec((1, ch, ns), lambda d, s: (d, 0, 0)),
                  pl.BlockSpec((1, nb, ns), lambda d, s: (d, 0, 0)),
                  pl.BlockSpec((1, nb, ns), lambda d, s: (d, 0, 0)),
                  pl.BlockSpec((ns, ch), lambda d, s: (0, 0)),
                  pl.BlockSpec((ns, ch), lambda d, s: (0, 0))],
        out_specs=pl.BlockSpec((1, tr, ch), lambda d, s: (d, pos(d, s), 0)),
        scratch_shapes=[pltpu.VMEM((tr, ns), F32), pltpu.VMEM((tr, ns), F32), pltpu.VMEM((2, nb, ns), F32)],
        compiler_params=_cparams(("arbitrary", "arbitrary")),
        name="s5_scan",
    )(ut, bre, bim, are, aim, cre, cim)


def _head_norm(o, gain, dim):
    parts = []
    for h in range(o.shape[1] // LANE):
        seg = o[:, h * LANE:(h + 1) * LANE]
        ms = jnp.sum(seg * seg, axis=-1, keepdims=True) * (1.0 / dim)
        parts.append(seg * lax.rsqrt(ms + EPS))
    return jnp.concatenate(parts, axis=1) * gain


def _mix_kernel(with_router, ogf_ref, ogb_ref, gg_ref, ys_ref, u_ref, omf_ref, omb_ref, mo_ref, h_ref, mod_ref,
                gn_ref, mn_ref, sd_ref, gw_ref, gb_ref, wo_ref, n2_ref, *rest):
    if with_router:
        wr_ref, ho_ref, f_ref, rt_ref = rest
    else:
        ho_ref, f_ref = rest
    gla = _head_norm(ogf_ref[...] + ogb_ref[...], gn_ref[...], GLA_DV) * _silu(gg_ref[...])
    z = _gelu_tanh(ys_ref[0] + ys_ref[1] + sd_ref[...] * u_ref[...])
    s5 = z * jax.nn.sigmoid(_dot(z.astype(BF16), gw_ref[...]) + gb_ref[...])
    ml = _head_norm(omf_ref[...] + omb_ref[...], mn_ref[...], ML_DH) * jax.nn.sigmoid(mo_ref[...])
    mix = (_dot(gla.astype(BF16), wo_ref[0:HP]) + _dot(s5.astype(BF16), wo_ref[HP:HP + 2 * LANE])
           + _dot(ml.astype(BF16), wo_ref[HP + 2 * LANE:]))
    m = mod_ref[0]
    hn = h_ref[...] + m[2:3] * mix
    ho_ref[...] = hn
    f = _rmsnorm(hn, n2_ref[...]) * (1.0 + m[4:5]) + m[3:4]
    f_ref[...] = f
    if with_router:
        logits = _dot(f, wr_ref[...], precision=HIGHEST)
        lane = lax.broadcasted_iota(jnp.int32, logits.shape, 1)
        l0 = jnp.where(lane < N_EXPERTS, logits, NEG)
        m1 = jnp.max(l0, axis=-1, keepdims=True)
        i1 = jnp.min(jnp.where(l0 == m1, lane, LANE), axis=-1, keepdims=True)
        l1 = jnp.where(lane == i1, NEG, l0)
        m2 = jnp.max(l1, axis=-1, keepdims=True)
        i2 = jnp.min(jnp.where(l1 == m2, lane, LANE), axis=-1, keepdims=True)
        e = jnp.exp(m2 - m1)
        w1 = 1.0 / (1.0 + e)
        w2 = e / (1.0 + e)
        rt_ref[...] = jnp.where(lane == 0, i1.astype(F32),
                                jnp.where(lane == 1, i2.astype(F32),
                                          jnp.where(lane == 2, w1, jnp.where(lane == 3, w2, 0.0))))


def _mix(ogf, ogb, pg, ys, pu, omf, omb, pm, h, modtab, gn, mn, sd, gw, gb, wo, n2, wr, *, layer, rows, lat_only):
    d = h.shape[1]
    tm = SEG
    n = rows.n_blocks(lat_only)
    src, sel = rows.src(lat_only), rows.sel(lat_only)
    full = lambda a: pl.BlockSpec(a.shape, lambda i: (0,) * a.ndim)
    with_router = wr is not None
    in_specs = [pl.BlockSpec((tm, HP), lambda i: (src(i), 0)),
                pl.BlockSpec((tm, HP), lambda i: (src(i), 0)),
                pl.BlockSpec((tm, HP), lambda i: (src(i), 3)),
                pl.BlockSpec((2, tm, 2 * LANE), lambda i: (0, src(i), 0)),
                pl.BlockSpec((tm, 2 * LANE), lambda i: (src(i), 0)),
                pl.BlockSpec((tm, HP), lambda i: (src(i), 0)),
                pl.BlockSpec((tm, HP), lambda i: (src(i), 0)),
                pl.BlockSpec((tm, HP), lambda i: (src(i), 3)),
                pl.BlockSpec((tm, d), lambda i: (src(i), 0)),
                pl.BlockSpec((None, 1, 8, d), lambda i: (layer, sel(i), 0, 0)),
                full(gn), full(mn), full(sd), full(gw), full(gb), full(wo), full(n2)]
    args = [ogf, ogb, pg, ys, pu, omf, omb, pm, h, modtab, gn, mn, sd, gw, gb, wo, n2]
    out_shape = [jax.ShapeDtypeStruct((n * tm, d), F32), jax.ShapeDtypeStruct((n * tm, d), F32)]
    out_specs = [pl.BlockSpec((tm, d), lambda i: (i, 0)), pl.BlockSpec((tm, d), lambda i: (i, 0))]
    if with_router:
        in_specs.append(full(wr))
        args.append(wr)
        out_shape.append(jax.ShapeDtypeStruct((n * tm, LANE), F32))
        out_specs.append(pl.BlockSpec((tm, LANE), lambda i: (i, 0)))
    return pl.pallas_call(
        functools.partial(_mix_kernel, with_router),
        out_shape=tuple(out_shape),
        grid=(n,),
        in_specs=in_specs,
        out_specs=tuple(out_specs),
        compiler_params=_cparams(("arbitrary",)),
        name="mix_out",
    )(*args)


FF_TILE = 256


def _swiglu_acc(xb, w1_ref, w3_ref, w2_ref, lead=()):
    dff = w1_ref.shape[-1]
    acc = None
    for j in range(dff // FF_TILE):
        sl = slice(j * FF_TILE, (j + 1) * FF_TILE)
        h1 = _dot(xb, w1_ref[lead + (slice(None), sl)])
        h3 = _dot(xb, w3_ref[lead + (slice(None), sl)])
        a = (_silu(h1) * h3).astype(BF16)
        t = _dot(a, w2_ref[lead + (sl, slice(None))])
        acc = t if acc is None else acc + t
    return acc


def _ffn_kernel(final, f_ref, h_ref, mod_ref, w1_ref, w3_ref, w2_ref, *rest):
    y = _swiglu_acc(f_ref[...].astype(BF16), w1_ref, w3_ref, w2_ref)
    hn = h_ref[...] + mod_ref[0][5:6] * y
    if final:
        nf_ref, o_ref = rest
        o_ref[...] = _rmsnorm(hn, nf_ref[...])
    else:
        (o_ref,) = rest
        o_ref[...] = hn


def _ffn(f, h, modtab, w1, w3, w2, nf, *, layer, rows, lat_only):
    r, d = h.shape
    tm = SEG
    sel = rows.sel(lat_only)
    full = lambda a: pl.BlockSpec(a.shape, lambda i: (0,) * a.ndim)
    in_specs = [pl.BlockSpec((tm, d), lambda i: (i, 0)),
                pl.BlockSpec((tm, d), lambda i: (i, 0)),
                pl.BlockSpec((None, 1, 8, d), lambda i: (layer, sel(i), 0, 0)),
                full(w1), full(w3), full(w2)]
    args = [f, h, modtab, w1, w3, w2]
    if lat_only:
        in_specs.append(full(nf))
        args.append(nf)
    return pl.pallas_call(
        functools.partial(_ffn_kernel, lat_only),
        out_shape=jax.ShapeDtypeStruct((r, d), F32),
        grid=(r // tm,),
        in_specs=in_specs,
        out_specs=pl.BlockSpec((tm, d), lambda i: (i, 0)),
        compiler_params=_cparams(("arbitrary",)),
        name="ffn",
    )(*args)


MOE_TM = 256


def _moe_kernel(be_ref, x_ref, rw_ref, w1_ref, w3_ref, w2_ref, o_ref):
    y = _swiglu_acc(x_ref[...].astype(BF16), w1_ref, w3_ref, w2_ref, lead=(0,))
    o_ref[...] = y * rw_ref[:, 0:1]


def _moe_experts(block_expert, xg, rw, w1, w3, w2):
    n_rows, d = xg.shape
    dff = w1.shape[-1]
    tm = MOE_TM
    return pl.pallas_call(
        _moe_kernel,
        out_shape=jax.ShapeDtypeStruct((n_rows, d), F32),
        grid_spec=pltpu.PrefetchScalarGridSpec(
            num_scalar_prefetch=1,
            grid=(n_rows // tm,),
            in_specs=[pl.BlockSpec((tm, d), lambda i, be: (i, 0)),
                      pl.BlockSpec((tm, LANE), lambda i, be: (i, 0)),
                      pl.BlockSpec((1, d, dff), lambda i, be: (be[i], 0, 0)),
                      pl.BlockSpec((1, d, dff), lambda i, be: (be[i], 0, 0)),
                      pl.BlockSpec((1, dff, d), lambda i, be: (be[i], 0, 0))],
            out_specs=pl.BlockSpec((tm, d), lambda i, be: (i, 0))),
        compiler_params=_cparams(("arbitrary",)),
        name="moe_experts",
    )(block_expert, xg, rw, w1, w3, w2)


def _resid_kernel(final, h_ref, y0_ref, y1_ref, mod_ref, *rest):
    hn = h_ref[...] + mod_ref[0][5:6] * (y0_ref[...] + y1_ref[...])
    if final:
        nf_ref, o_ref = rest
        o_ref[...] = _rmsnorm(hn, nf_ref[...])
    else:
        (o_ref,) = rest
        o_ref[...] = hn


def _moe_resid(h, y0, y1, modtab, nf, *, layer, rows, lat_only):
    r, d = h.shape
    tm = SEG
    sel = rows.sel(lat_only)
    row = lambda i: (i, 0)
    in_specs = [pl.BlockSpec((tm, d), row), pl.BlockSpec((tm, d), row), pl.BlockSpec((tm, d), row),
                pl.BlockSpec((None, 1, 8, d), lambda i: (layer, sel(i), 0, 0))]
    args = [h, y0, y1, modtab]
    if lat_only:
        in_specs.append(pl.BlockSpec((1, d), lambda i: (0, 0)))
        args.append(nf)
    return pl.pallas_call(
        functools.partial(_resid_kernel, lat_only),
        out_shape=jax.ShapeDtypeStruct((r, d), F32),
        grid=(r // tm,),
        in_specs=in_specs,
        out_specs=pl.BlockSpec((tm, d), row),
        compiler_params=_cparams(("arbitrary",)),
        name="moe_resid",
    )(*args)


def _pad_heads(w, heads, dim):
    lead = w.shape[:-1]
    w = w.reshape(lead + (heads, dim))
    w = jnp.pad(w, [(0, 0)] * len(lead) + [(0, 0), (0, LANE - dim)])
    return w.reshape(lead + (heads * LANE,))


def _pad_last(w, to):
    return jnp.pad(w, [(0, 0)] * (w.ndim - 1) + [(0, to - w.shape[-1])])


def _pos_embed(n_tokens, d):
    n_grid_rows = n_tokens // GRID_W
    row, col = jnp.meshgrid(jnp.arange(n_grid_rows, dtype=F32), jnp.arange(GRID_W, dtype=F32), indexing='ij')
    n_freq = d // 4
    omega = jnp.exp(-math.log(POS_BASE) * jnp.arange(n_freq, dtype=F32) / n_freq)

    def axis_embed(p):
        ang = p.reshape(-1, 1) * omega
        return jnp.concatenate([jnp.sin(ang), jnp.cos(ang)], axis=-1)

    return jnp.concatenate([axis_embed(row), axis_embed(col)], axis=-1)


def _s5_discretise(lam_re, lam_im, log_dt, b_re, b_im):
    dt = jnp.exp(log_dt)[:, None]
    mag = jnp.exp(lam_re * dt)
    abar_re, abar_im = mag * jnp.cos(lam_im * dt), mag * jnp.sin(lam_im * dt)
    den = lam_re * lam_re + lam_im * lam_im
    pr, pi = abar_re - 1.0, abar_im
    coef_re = (pr * lam_re + pi * lam_im) / den
    coef_im = (pi * lam_re - pr * lam_im) / den
    bbar_re = coef_re[..., None] * b_re - coef_im[..., None] * b_im
    bbar_im = coef_re[..., None] * b_im + coef_im[..., None] * b_re
    return abar_re, abar_im, bbar_re, bbar_im


def _block_diag(m):
    g, a, b = m.shape
    eye = jnp.eye(g, dtype=m.dtype)
    return (eye[:, None, :, None] * m[:, :, None, :]).reshape(g * a, g * b)


def _route_plan(route, tm):
    n_tok = route.shape[0]
    n_assign = n_tok * TOP_K
    flat_e = route[:, 0:TOP_K].astype(jnp.int32).T.reshape(-1)
    flat_w = route[:, TOP_K:2 * TOP_K].T.reshape(-1)
    onehot = (jnp.arange(N_EXPERTS, dtype=jnp.int32)[:, None] == flat_e[None, :]).astype(jnp.int32)
    csum = jnp.cumsum(onehot, axis=1)
    counts = csum[:, -1]
    padded = (counts + tm - 1) // tm * tm
    pend = jnp.cumsum(padded)
    pstart = pend - padded
    dest = jnp.sum(onehot * (csum - 1 + pstart[:, None]), axis=0)
    n_blocks = -(-n_assign // tm) + N_EXPERTS
    n_rows = n_blocks * tm
    row_assign = jnp.full((n_rows,), -1, jnp.int32).at[dest].set(jnp.arange(n_assign, dtype=jnp.int32))
    live = row_assign >= 0
    ra = jnp.maximum(row_assign, 0)
    row_token = ra % n_tok
    row_w = jnp.where(live, flat_w[ra], 0.0)
    block_expert = jnp.minimum(
        jnp.searchsorted(pend, jnp.arange(n_blocks) * tm, side='right'), N_EXPERTS - 1).astype(jnp.int32)
    return row_token, row_w, block_expert, dest.reshape(TOP_K, n_tok)


def kernel(x, c, ctx, c_ctx, w_ada, b_ada, norm1, norm2, w_in, w_out, gla_wa2, gla_ba, gla_norm, s5_lam_re, s5_lam_im, s5_log_dt, s5_b_re, s5_b_im, s5_c_re, s5_c_im, s5_d, s5_glu_w, s5_glu_b, ml_conv_w, ml_conv_b, ml_gate_b, ml_norm, ffn_w1, ffn_w3, ffn_w2, moe_router, moe_w1, moe_w3, moe_w2, norm_f):
    nb, n_lat, d = x.shape
    lc = ctx.shape[1]
    depth = w_ada.shape[0]
    l = lc + n_lat
    assert lc % SEG == 0 and n_lat % SEG == 0 and nb == 8 and nb % SCAN_BATCHES == 0
    rows = _Rows(nb, lc // SEG, l // SEG)

    h = _embed(ctx.reshape(nb * lc, d), x.reshape(nb * n_lat, d), _pos_embed(n_lat, d), rows)

    cond = jnp.zeros((16, d), F32).at[:nb].set(c).at[nb].set(c_ctx)
    mod = _modulation(cond, w_ada, b_ada)
    modtab = jnp.pad(mod.reshape(depth, 16, 6, d), ((0, 0), (0, 0), (0, 2), (0, 0)))

    dk, dv, dh = GLA_HEADS * GLA_DK, GLA_HEADS * GLA_DV, ML_HEADS * ML_DH
    s5c = s5_d.shape[-1]
    cuts = np.cumsum([dk, dk, dv, GLA_RANK, dv, s5c, dh, dh, dh, dh, 4 * ML_HEADS])

    for i in range(depth):
        last = i == depth - 1
        gq, gk, gv, glr, gg, su, mq, mk, mv, mo, mg = jnp.split(w_in[i], cuts[:-1], axis=-1)
        w_all = jnp.concatenate([
            _pad_heads(gq, GLA_HEADS, GLA_DK), _pad_heads(gk, GLA_HEADS, GLA_DK),
            _pad_heads(gv, GLA_HEADS, GLA_DV), _pad_heads(gg, GLA_HEADS, GLA_DV), _pad_last(glr, LANE),
            su,
            _pad_heads(mq, ML_HEADS, ML_DH), _pad_heads(mk, ML_HEADS, ML_DH),
            _pad_heads(mv, ML_HEADS, ML_DH), _pad_heads(mo, ML_HEADS, ML_DH),
            _pad_last(mg[:, :2 * ML_HEADS], LANE), _pad_last(mg[:, 2 * ML_HEADS:], LANE)],
            axis=-1).astype(BF16)
        wgt = mg.T.astype(BF16)
        pg, pu, pm, gt = _proj(h, modtab, norm1[i][None], w_all, wgt, layer=i, rows=rows)

        wa = jnp.pad(_pad_heads(gla_wa2[i], GLA_HEADS, GLA_DK), ((0, 0), (0, LANE - GLA_RANK), (0, 0)))
        ba = _pad_heads(gla_ba[i], GLA_HEADS, GLA_DK)[:, None, :]
        pg3 = pg.reshape(nb, l, NG)
        ogf = _gla(pg3, wa[0], ba[0], rows=rows, rev=False).reshape(nb * l, HP)
        ogb = _gla(pg3, wa[1], ba[1], rows=rows, rev=True).reshape(nb * l, HP)

        bres, bims, ares, aims = [], [], [], []
        for dr in (0, 1):
            a_re, a_im, b_re, b_im = _s5_discretise(s5_lam_re[i, dr], s5_lam_im[i, dr], s5_log_dt[i, dr],
                                                    s5_b_re[i], s5_b_im[i])
            bres.append(_block_diag(jnp.swapaxes(b_re, 1, 2)))
            bims.append(_block_diag(jnp.swapaxes(b_im, 1, 2)))
            ares.append(jnp.broadcast_to(a_re.reshape(1, -1), (nb, a_re.size)))
            aims.append(jnp.broadcast_to(a_im.reshape(1, -1), (nb, a_im.size)))
        cre = _block_diag(jnp.swapaxes(s5_c_re[i], 1, 2)).astype(BF16)
        cim = _block_diag(jnp.swapaxes(s5_c_im[i], 1, 2)).astype(BF16)
        ut = pu.reshape(nb, l, s5c).swapaxes(0, 1).reshape(l * nb, s5c)
        yt = _s5(ut, jnp.stack(bres).astype(BF16), jnp.stack(bims).astype(BF16), jnp.stack(ares), jnp.stack(aims),
                 cre, cim, nb=nb, nc=lc // CHUNK, nt=l // CHUNK)
        ys = yt.reshape(2, l, nb, s5c).swapaxes(1, 2).reshape(2, nb * l, s5c)

        cw = jnp.concatenate([_pad_heads(ml_conv_w[i][:, :dh], ML_HEADS, ML_DH),
                              _pad_heads(ml_conv_w[i][:, dh:], ML_HEADS, ML_DH)], axis=-1)
        cw = jnp.pad(cw, ((0, 8 - ML_CONV), (0, 0)))
        cb = jnp.concatenate([_pad_heads(ml_conv_b[i][:dh], ML_HEADS, ML_DH),
                              _pad_heads(ml_conv_b[i][dh:], ML_HEADS, ML_DH)])[None]
        post = jnp.concatenate([jnp.ones((HP,), F32), jnp.full((HP,), ML_DH ** -0.5, F32)])[None]
        qk = _conv(pm, cw, cb, post, rows=rows)
        gb = ml_gate_b[i].reshape(2, 2, ML_HEADS)
        gbr = _pad_last(gb.reshape(2, 1, 2 * ML_HEADS), LANE)
        gbt = _pad_last(jnp.broadcast_to(gb[..., None], (2, 2, ML_HEADS, CHUNK)), LANE).reshape(2, 2, HP)
        gtl = _pad_last(gt.reshape(2, 2, ML_HEADS, nb, l // CHUNK, CHUNK).transpose(3, 4, 0, 1, 2, 5),
                        LANE).reshape(nb, l // CHUNK, 2, 2, HP)
        qk3, pm3 = qk.reshape(nb, l, 2 * HP), pm.reshape(nb, l, NM)
        omf = _mlstm(qk3, pm3, gtl, gbr[0], gbt[0], rows=rows, rev=False).reshape(nb * l, HP)
        omb = _mlstm(qk3, pm3, gtl, gbr[1], gbt[1], rows=rows, rev=True).reshape(nb * l, HP)

        wo = w_out[i]
        wo_p = jnp.concatenate([
            jnp.pad(wo[:dv].reshape(GLA_HEADS, GLA_DV, d), ((0, 0), (0, LANE - GLA_DV), (0, 0))).reshape(HP, d),
            wo[dv:dv + s5c],
            jnp.pad(wo[dv + s5c:].reshape(ML_HEADS, ML_DH, d), ((0, 0), (0, LANE - ML_DH), (0, 0))).reshape(HP, d)],
            axis=0).astype(BF16)
        gn = jnp.tile(_pad_last(gla_norm[i], LANE), GLA_HEADS)[None]
        mn = jnp.tile(_pad_last(ml_norm[i], LANE), ML_HEADS)[None]
        is_moe = i % 2 == 1
        j = i // 2
        wr = _pad_last(moe_router[j], LANE) if is_moe else None
        outs = _mix(ogf, ogb, pg, ys, pu, omf, omb, pm, h, modtab, gn, mn, s5_d[i][None], s5_glu_w[i].astype(BF16),
                    s5_glu_b[i][None], wo_p, norm2[i][None], wr, layer=i, rows=rows, lat_only=last)
        if not is_moe:
            h, f = outs
            h = _ffn(f, h, modtab, ffn_w1[j].astype(BF16), ffn_w3[j].astype(BF16), ffn_w2[j].astype(BF16),
                     norm_f[None], layer=i, rows=rows, lat_only=last)
        else:
            h, f, route = outs
            row_token, row_w, block_expert, dest = _route_plan(route, MOE_TM)
            take = lambda a, idx: a.at[idx].get(mode='promise_in_bounds')
            xg = take(f, row_token)
            rw = jnp.broadcast_to(row_w[:, None], (row_w.shape[0], LANE))
            yg = _moe_experts(block_expert, xg, rw, moe_w1[j].astype(BF16), moe_w3[j].astype(BF16),
                              moe_w2[j].astype(BF16))
            y0 = take(yg, dest[0])
            y1 = take(yg, dest[1])
            h = _moe_resid(h, y0, y1, modtab, norm_f[None], layer=i, rows=rows, lat_only=last)
    return h.reshape(nb, n_lat, d)
```

```python
import functools
import math

import numpy as np
import jax
import jax.numpy as jnp
from jax import lax
from jax.experimental import pallas as pl
from jax.experimental.pallas import tpu as pltpu

F32 = jnp.float32
BF16 = jnp.bfloat16
HIGHEST = lax.Precision.HIGHEST

GRID_W = 64
POS_BASE = 10000.0
EPS = 1e-6
GLA_HEADS, GLA_DK, GLA_DV, GLA_RANK, GLA_GATE_NORM = 4, 48, 96, 16, 16.0
S5_GROUP, S5_STATE = 16, 64
ML_HEADS, ML_DH, ML_CONV = 4, 96, 3
N_EXPERTS, TOP_K = 8, 2

LANE = 128
CHUNK = 64
SEG = 256
N_SUB = SEG // CHUNK
SCAN_BATCHES = 2
NEG = -1e30
VMEM_LIMIT = 56 * 1024 * 1024

HP = LANE * GLA_HEADS
NG = 4 * HP + LANE
NM = 4 * HP + 2 * LANE


def _cparams(sem):
    return pltpu.CompilerParams(dimension_semantics=sem, vmem_limit_bytes=VMEM_LIMIT)


def _dot(a, b, **kw):
    return jnp.dot(a, b, preferred_element_type=F32, **kw)


def _dot_nt(a, b, **kw):
    return lax.dot_general(a, b, (((1,), (1,)), ((), ())), preferred_element_type=F32, **kw)


def _dot_tn(a, b, **kw):
    return lax.dot_general(a, b, (((0,), (0,)), ((), ())), preferred_element_type=F32, **kw)


def _log_sigmoid(x):
    return jnp.minimum(x, 0.0) - jnp.log1p(jnp.exp(-jnp.abs(x)))


def _silu(x):
    return x * jax.nn.sigmoid(x)


def _gelu_tanh(x):
    return 0.5 * x * (1.0 + jnp.tanh(math.sqrt(2.0 / math.pi) * (x + 0.044715 * (x * x * x))))


def _rmsnorm(x, g):
    return x * lax.rsqrt(jnp.mean(x * x, axis=-1, keepdims=True) + EPS) * g


class _Rows:
    def __init__(self, nb, ncb, ntb):
        self.nb, self.ncb, self.ntb, self.nlb = nb, ncb, ntb, ntb - ncb

    def n_blocks(self, lat_only):
        return self.nb * (self.nlb if lat_only else self.ntb)

    def src(self, lat_only):
        if lat_only:
            return lambda i: (i // self.nlb) * self.ntb + self.ncb + i % self.nlb
        return lambda i: i

    def sel(self, lat_only):
        if lat_only:
            return lambda i: i // self.nlb
        return lambda i: jnp.where(i % self.ntb < self.ncb, self.nb, i // self.ntb)


def _embed_kernel(ncb, ntb, ctx_ref, x_ref, pos_ref, o_ref):
    j = pl.program_id(0) % ntb

    @pl.when(j < ncb)
    def _():
        o_ref[...] = ctx_ref[...]

    @pl.when(j >= ncb)
    def _():
        o_ref[...] = x_ref[...] + pos_ref[...]


def _embed(ctx2, x2, pos, rows):
    d = ctx2.shape[1]
    ncb, ntb, nlb = rows.ncb, rows.ntb, rows.nlb
    return pl.pallas_call(
        functools.partial(_embed_kernel, ncb, ntb),
        out_shape=jax.ShapeDtypeStruct((rows.nb * ntb * SEG, d), F32),
        grid=(rows.nb * ntb,),
        in_specs=[pl.BlockSpec((SEG, d), lambda i: ((i // ntb) * ncb + jnp.minimum(i % ntb, ncb - 1), 0)),
                  pl.BlockSpec((SEG, d), lambda i: ((i // ntb) * nlb + jnp.maximum(i % ntb - ncb, 0), 0)),
                  pl.BlockSpec((SEG, d), lambda i: (jnp.maximum(i % ntb - ncb, 0), 0))],
        out_specs=pl.BlockSpec((SEG, d), lambda i: (i, 0)),
        compiler_params=_cparams(("arbitrary",)),
        name="embed",
    )(ctx2, x2, pos)


def _mod_kernel(c_ref, w_ref, b_ref, o_ref):
    s = _silu(c_ref[...])
    o_ref[0] = _dot(s, w_ref[0], precision=HIGHEST) + b_ref[0]


def _modulation(cond, w_ada, b_ada):
    depth, d, n6 = w_ada.shape
    tn = n6 // 4
    n_rows = cond.shape[0]
    return pl.pallas_call(
        _mod_kernel,
        out_shape=jax.ShapeDtypeStruct((depth, n_rows, n6), F32),
        grid=(depth, n6 // tn),
        in_specs=[pl.BlockSpec((n_rows, d), lambda l, j: (0, 0)),
                  pl.BlockSpec((1, d, tn), lambda l, j: (l, 0, j)),
                  pl.BlockSpec((1, 1, tn), lambda l, j: (l, 0, j))],
        out_specs=pl.BlockSpec((1, n_rows, tn), lambda l, j: (l, 0, j)),
        compiler_params=_cparams(("arbitrary", "arbitrary")),
        name="modulation",
    )(cond, w_ada, b_ada.reshape(depth, 1, n6))


def _proj_kernel(h_ref, mod_ref, g_ref, w_ref, wgt_ref, pg_ref, pu_ref, pm_ref, gt_ref):
    m = mod_ref[0]
    ab = (_rmsnorm(h_ref[...], g_ref[...]) * (1.0 + m[1:2]) + m[0:1]).astype(BF16)
    pg_ref[...] = _dot(ab, w_ref[:, 0:NG])
    pu_ref[...] = _dot(ab, w_ref[:, NG:NG + 2 * LANE])
    pm_ref[...] = _dot(ab, w_ref[:, NG + 2 * LANE:])
    gt_ref[...] = _dot_nt(wgt_ref[...], ab)


def _proj(h, modtab, g, w, wgt, *, layer, rows):
    r, d = h.shape
    tm = SEG
    sel = rows.sel(False)
    nw = w.shape[1]
    return pl.pallas_call(
        _proj_kernel,
        out_shape=(jax.ShapeDtypeStruct((r, NG), F32), jax.ShapeDtypeStruct((r, 2 * LANE), F32),
                   jax.ShapeDtypeStruct((r, NM), F32), jax.ShapeDtypeStruct((16, r), F32)),
        grid=(r // tm,),
        in_specs=[pl.BlockSpec((tm, d), lambda i: (i, 0)),
                  pl.BlockSpec((None, 1, 8, d), lambda i: (layer, sel(i), 0, 0)),
                  pl.BlockSpec((1, d), lambda i: (0, 0)),
                  pl.BlockSpec((d, nw), lambda i: (0, 0)),
                  pl.BlockSpec((16, d), lambda i: (0, 0))],
        out_specs=(pl.BlockSpec((tm, NG), lambda i: (i, 0)),
                   pl.BlockSpec((tm, 2 * LANE), lambda i: (i, 0)),
                   pl.BlockSpec((tm, NM), lambda i: (i, 0)),
                   pl.BlockSpec((16, tm), lambda i: (0, i))),
        compiler_params=_cparams(("arbitrary",)),
        name="proj",
    )(h, modtab, g, w, wgt)


def _conv_kernel(ncb, ntb, x_ref, prev_ref, next_ref, w_ref, b_ref, post_ref, o_ref):
    j = pl.program_id(0) % ntb
    x = x_ref[...]
    tm = x.shape[0]
    first = jnp.logical_or(j == 0, j == ncb).astype(F32)
    last = jnp.logical_or(j == ncb - 1, j == ntb - 1).astype(F32)
    row = lax.broadcasted_iota(jnp.int32, x.shape, 0)
    xp = jnp.where(row == 0, prev_ref[7:8, :] * (1.0 - first), pltpu.roll(x, 1, axis=0))
    xn = jnp.where(row == tm - 1, next_ref[0:1, :] * (1.0 - last), pltpu.roll(x, tm - 1, axis=0))
    y = w_ref[0:1] * xp + w_ref[1:2] * x + w_ref[2:3] * xn + b_ref[...]
    o_ref[...] = _silu(y) * post_ref[...]


def _conv(pm, w, b, post, *, rows):
    r = pm.shape[0]
    tm = SEG
    wq = 2 * HP
    t8 = tm // 8
    return pl.pallas_call(
        functools.partial(_conv_kernel, rows.ncb, rows.ntb),
        out_shape=jax.ShapeDtypeStruct((r, wq), F32),
        grid=(r // tm,),
        in_specs=[pl.BlockSpec((tm, wq), lambda i: (i, 0)),
                  pl.BlockSpec((8, wq), lambda i: (jnp.maximum(i * t8 - 1, 0), 0)),
                  pl.BlockSpec((8, wq), lambda i: (jnp.minimum((i + 1) * t8, r // 8 - 1), 0)),
                  pl.BlockSpec((8, wq), lambda i: (0, 0)),
                  pl.BlockSpec((1, wq), lambda i: (0, 0)),
                  pl.BlockSpec((1, wq), lambda i: (0, 0))],
        out_specs=pl.BlockSpec((tm, wq), lambda i: (i, 0)),
        compiler_params=_cparams(("arbitrary",)),
        name="conv",
    )(pm, pm, pm, w, b, post)


def _scan_pos(d, s, ncb, ntb):
    rev = jnp.where(s < ncb, ncb - 1 - s, ntb - 1 - (s - ncb))
    return jnp.where(d == 0, s, rev)


def _scan_pos_static(rev, s, ncb, ntb):
    if not rev:
        return s
    return jnp.where(s < ncb, ncb - 1 - s, ntb - 1 - (s - ncb))


def _tri(rev):
    r = lax.broadcasted_iota(jnp.int32, (CHUNK, CHUNK), 0)
    c = lax.broadcasted_iota(jnp.int32, (CHUNK, CHUNK), 1)
    return (r <= c) if rev else (r >= c)


def _chunk_rows(rev):
    return [(N_SUB - 1 - j if rev else j) * CHUNK for j in range(N_SUB)]


def _gla_kernel(rev, p_ref, wa_ref, ba_ref, o_ref, st_ref):
    s, g = pl.program_id(0), pl.program_id(1)
    nbb = p_ref.shape[0]
    b0 = g * nbb

    @pl.when(s == 0)
    def _():
        st_ref[pl.ds(b0, nbb)] = jnp.zeros((nbb,) + st_ref.shape[1:], F32)

    valid = _tri(rev)
    tri = valid.astype(F32)
    r0s = _chunk_rows(rev)
    wa, ba = wa_ref[...], ba_ref[...]
    inst = [(bb, j) for bb in range(nbb) for j in range(N_SUB)]
    heads = [slice(h * LANE, (h + 1) * LANE) for h in range(GLA_HEADS)]

    la = {}
    for bb, j in inst:
        lr = p_ref[bb, pl.ds(r0s[j], CHUNK), 4 * HP:4 * HP + LANE]
        la[bb, j] = _log_sigmoid(_dot(lr, wa) + ba) * (1.0 / GLA_GATE_NORM)
    bc, e_last = {}, {}
    for i in inst:
        bc[i] = _dot_exact01(tri, la[i], lhs_is_01=True)
        e_last[i] = jnp.exp(jnp.sum(la[i], axis=0, keepdims=True))
    q_in, k_in, k_out, v = {}, {}, {}, {}
    for bb, j in inst:
        i = (bb, j)
        rs = pl.ds(r0s[j], CHUNK)
        q_in[i] = (p_ref[bb, rs, 0:HP] * (GLA_DK ** -0.5) * jnp.exp(bc[i])).astype(BF16)
        kd = p_ref[bb, rs, HP:2 * HP] * jnp.exp(-bc[i])
        k_out[i] = (kd * e_last[i]).astype(BF16)
        k_in[i] = kd.astype(BF16)
        v[i] = p_ref[bb, rs, 2 * HP:3 * HP].astype(BF16)
    att = {}
    for i in inst:
        for h, sl in enumerate(heads):
            att[i, h] = jnp.where(valid, _dot_nt(q_in[i][:, sl], k_in[i][:, sl]), 0.0).astype(BF16)
    o_intra, ds = {}, {}
    for i in inst:
        for h, sl in enumerate(heads):
            o_intra[i, h] = _dot(att[i, h], v[i][:, sl])
            ds[i, h] = _dot_tn(v[i][:, sl], k_out[i][:, sl])
    s_in = {}
    for bb in range(nbb):
        for h, sl in enumerate(heads):
            st = st_ref[b0 + bb, h]
            for j in range(N_SUB):
                s_in[(bb, j), h] = st.astype(BF16)
                st = st * e_last[bb, j][:, sl] + ds[(bb, j), h]
            st_ref[b0 + bb, h] = st
    for bb, j in inst:
        for h, sl in enumerate(heads):
            o = o_intra[(bb, j), h] + _dot_nt(q_in[bb, j][:, sl], s_in[(bb, j), h])
            o_ref[bb, pl.ds(r0s[j], CHUNK), sl] = o


def _gla(pg3, wa, ba, *, rows, rev):
    nb, l, _ = pg3.shape
    nbb = SCAN_BATCHES
    pos = functools.partial(_scan_pos_static, rev, ncb=rows.ncb, ntb=rows.ntb)
    return pl.pallas_call(
        functools.partial(_gla_kernel, rev),
        out_shape=jax.ShapeDtypeStruct((nb, l, HP), F32),
        grid=(rows.ntb, nb // nbb),
        in_specs=[pl.BlockSpec((nbb, SEG, NG), lambda s, g: (g, pos(s), 0)),
                  pl.BlockSpec((LANE, HP), lambda s, g: (0, 0)),
                  pl.BlockSpec((1, HP), lambda s, g: (0, 0))],
        out_specs=pl.BlockSpec((nbb, SEG, HP), lambda s, g: (g, pos(s), 0)),
        scratch_shapes=[pltpu.VMEM((nb, GLA_HEADS, LANE, LANE), F32)],
        compiler_params=_cparams(("arbitrary", "arbitrary")),
        name="gla_scan_bwd" if rev else "gla_scan_fwd",
    )(pg3, wa, ba)


def _mlstm_kernel_old(qk_ref, v_ref, g_ref, gt_ref, gbr_ref, gbc_ref, o_ref, st_ref, m_ref):
    d, s, g = pl.program_id(0), pl.program_id(1), pl.program_id(2)
    nbb = qk_ref.shape[0]
    b0 = g * nbb

    @pl.when(s == 0)
    def _():
        st_ref[pl.ds(b0, nbb)] = jnp.zeros((nbb,) + st_ref.shape[1:], F32)
        m_ref[pl.ds(b0, nbb)] = jnp.zeros((nbb,) + m_ref.shape[1:], F32)

    fwd = d == 0
    valid = _tri(d)
    tri = valid.astype(F32)
    r0s = _chunk_rows(d)
    cidx = [jnp.where(fwd, j, N_SUB - 1 - j) for j in range(N_SUB)]
    lane = lax.broadcasted_iota(jnp.int32, (CHUNK, LANE), 1)
    inst = [(bb, j) for bb in range(nbb) for j in range(N_SUB)]
    hinst = [(bb, j, h) for bb, j in inst for h in range(ML_HEADS)]
    gbr, gbc = gbr_ref[...], gbc_ref[...]

    def pick(a, h, kind, axis):
        i0, i1 = kind * ML_HEADS + h, (2 + kind) * ML_HEADS + h
        if axis == 1:
            return jnp.where(fwd, a[:, i0:i0 + 1], a[:, i1:i1 + 1])
        return jnp.where(fwd, a[i0:i0 + 1, :], a[i1:i1 + 1, :])

    gc, gr, fcum_c, fcum_r = {}, {}, {}, {}
    for bb, j in inst:
        i = (bb, j)
        gc[i] = g_ref[bb, pl.ds(r0s[j], CHUNK), :] + gbr
        gr[i] = gt_ref[bb, cidx[j]] + gbc
    for i in inst:
        fcum_c[i] = _dot(tri, _log_sigmoid(gc[i]), precision=HIGHEST)
        fcum_r[i] = _dot_nt(_log_sigmoid(gr[i]), tri, precision=HIGHEST)
    fc, lic, d_log, rmax = {}, {}, {}, {}
    for bb, j, h in hinst:
        i = (bb, j)
        fc[bb, j, h] = pick(fcum_c[i], h, 1, 1)
        lic[bb, j, h] = pick(gc[i], h, 0, 1)
        dl = jnp.where(valid, fc[bb, j, h] - pick(fcum_r[i], h, 1, 0) + pick(gr[i], h, 0, 0), NEG)
        d_log[bb, j, h] = dl
        rmax[bb, j, h] = jnp.max(dl, axis=-1, keepdims=True)
    m_prev, m_t, m_new, f_tot = {}, {}, {}, {}
    for bb in range(nbb):
        for h in range(ML_HEADS):
            m = m_ref[b0 + bb, h:h + 1, 0:1]
            for j in range(N_SUB):
                i = (bb, j, h)
                m_prev[i] = m
                m_t[i] = jnp.maximum(fc[i] + m, rmax[i])
                m = jnp.where(fwd, m_t[i][CHUNK - 1:CHUNK], m_t[i][0:1])
                m_new[i] = m
                f_tot[i] = jnp.where(fwd, fc[i][CHUNK - 1:CHUNK], fc[i][0:1])
            m_ref[b0 + bb, h:h + 1, :] = jnp.broadcast_to(m, (1, LANE))
    q, v, qk, w_prev, gdec, ds = {}, {}, {}, {}, {}, {}
    for bb, j, h in hinst:
        i = (bb, j, h)
        rs = pl.ds(r0s[j], CHUNK)
        sl = slice(h * LANE, (h + 1) * LANE)
        q[i] = qk_ref[bb, rs, sl].astype(BF16)
        k = qk_ref[bb, rs, HP + h * LANE:HP + (h + 1) * LANE]
        v[i] = jnp.where(lane == ML_DH, 1.0, v_ref[bb, rs, sl]).astype(BF16)
        w = jnp.exp(d_log[i] - m_t[i])
        w_prev[i] = jnp.exp(fc[i] + m_prev[i] - m_t[i])
        qk[i] = (_dot_nt(q[i], k.astype(BF16)) * w).astype(BF16)
        w_s = jnp.exp(f_tot[i] - fc[i] + lic[i] - m_new[i])
        gdec[i] = jnp.exp(f_tot[i] + m_prev[i] - m_new[i])
        ds[i] = _dot_tn(v[i], (k * w_s).astype(BF16))
    s_in = {}
    for bb in range(nbb):
        for h in range(ML_HEADS):
            st = st_ref[b0 + bb, h]
            for j in range(N_SUB):
                i = (bb, j, h)
                s_in[i] = st.astype(BF16)
                st = gdec[i] * st + ds[i]
            st_ref[b0 + bb, h] = st
    for bb, j, h in hinst:
        i = (bb, j, h)
        num = w_prev[i] * _dot_nt(q[i], s_in[i]) + _dot(qk[i], v[i])
        den = num[:, ML_DH:ML_DH + 1]
        hh = num / jnp.maximum(jnp.abs(den), jnp.exp(-m_t[i]))
        o_ref[0, bb, pl.ds(r0s[j], CHUNK), h * LANE:(h + 1) * LANE] = jnp.where(lane < ML_DH, hh, 0.0)


def _mlstm_old(qk3, pm3, gt4, gbr, gbc, *, rows):
    nb, l, _ = qk3.shape
    nbb = SCAN_BATCHES
    pos = functools.partial(_scan_pos, ncb=rows.ncb, ntb=rows.ntb)
    return pl.pallas_call(
        _mlstm_kernel,
        out_shape=jax.ShapeDtypeStruct((2, nb, l, HP), F32),
        grid=(2, rows.ntb, nb // nbb),
        in_specs=[pl.BlockSpec((nbb, SEG, 2 * HP), lambda d, s, g: (g, pos(d, s), 0)),
                  pl.BlockSpec((nbb, SEG, HP), lambda d, s, g: (g, pos(d, s), 2)),
                  pl.BlockSpec((nbb, SEG, LANE), lambda d, s, g: (g, pos(d, s), 4 * GLA_HEADS)),
                  pl.BlockSpec((nbb, N_SUB, 16, CHUNK), lambda d, s, g: (g, pos(d, s), 0, 0)),
                  pl.BlockSpec((1, LANE), lambda d, s, g: (0, 0)),
                  pl.BlockSpec((16, CHUNK), lambda d, s, g: (0, 0))],
        out_specs=pl.BlockSpec((1, nbb, SEG, HP), lambda d, s, g: (d, g, pos(d, s), 0)),
        scratch_shapes=[pltpu.VMEM((nb, ML_HEADS, LANE, LANE), F32), pltpu.VMEM((nb, 8, LANE), F32)],
        compiler_params=_cparams(("arbitrary", "arbitrary", "arbitrary")),
        name="mlstm_scan",
    )(qk3, pm3, pm3, gt4, gbr, gbc)


def _split_bf16(x, n):
    parts, r = [], x
    for _ in range(n):
        p = r.astype(BF16)
        parts.append(p)
        r = r - p.astype(F32)
    return parts


def _dot_exact01(a, b, lhs_is_01):
    if lhs_is_01:
        a = a.astype(BF16)
        terms = [_dot(a, p) for p in _split_bf16(b, 3)]
    else:
        b = b.astype(BF16)
        terms = [_dot(p, b) for p in _split_bf16(a, 3)]
    return terms[0] + terms[1] + terms[2]


def _cummax_rows(a, rev):
    n = a.shape[0]
    row = lax.broadcasted_iota(jnp.int32, a.shape, 0)
    k = 1
    while k < n:
        if rev:
            sh = jnp.where(row < n - k, pltpu.roll(a, n - k, axis=0), NEG)
        else:
            sh = jnp.where(row >= k, pltpu.roll(a, k, axis=0), NEG)
        a = jnp.maximum(a, sh)
        k *= 2
    return a


ML_GL = ML_HEADS


def _mlstm_kernel(rev, qk_ref, v_ref, g_ref, gt_ref, gbr_ref, gbt_ref, o_ref, st_ref, m_ref):
    s, g = pl.program_id(0), pl.program_id(1)
    nbb = qk_ref.shape[0]
    b0 = g * nbb

    @pl.when(s == 0)
    def _():
        st_ref[pl.ds(b0, nbb)] = jnp.zeros((nbb,) + st_ref.shape[1:], F32)
        m_ref[pl.ds(b0, nbb)] = jnp.zeros((nbb,) + m_ref.shape[1:], F32)

    valid = _tri(rev)
    tri = valid.astype(F32)
    r0s = _chunk_rows(rev)
    cs = [r // CHUNK for r in r0s]
    last = 0 if rev else CHUNK - 1
    inst = [(bb, j) for bb in range(nbb) for j in range(N_SUB)]
    heads = [slice(h * LANE, (h + 1) * LANE) for h in range(ML_HEADS)]

    r_sel = lax.broadcasted_iota(jnp.int32, (LANE, HP), 0)
    c_sel = lax.broadcasted_iota(jnp.int32, (LANE, HP), 1)
    sel_h = (r_sel == ML_GL + c_sel // LANE).astype(BF16)
    r_t = lax.broadcasted_iota(jnp.int32, (HP, HP), 0)
    c_t = lax.broadcasted_iota(jnp.int32, (HP, HP), 1)
    same = jnp.logical_and(r_t // LANE == c_t // LANE, jnp.logical_and(r_t % LANE < CHUNK, c_t % LANE < CHUNK))
    before = (r_t % LANE >= c_t % LANE) if rev else (r_t % LANE <= c_t % LANE)
    tri_b = jnp.logical_and(same, before).astype(BF16)
    r_v = lax.broadcasted_iota(jnp.int32, (CHUNK, HP), 0)
    c_v = lax.broadcasted_iota(jnp.int32, (CHUNK, HP), 1) % LANE
    valid4 = jnp.logical_and(c_v < CHUNK, (r_v <= c_v) if rev else (r_v >= c_v))
    lane4 = lax.broadcasted_iota(jnp.int32, (CHUNK, HP), 1) % LANE
    lane_c = lax.broadcasted_iota(jnp.int32, (CHUNK, LANE), 1)
    lane1 = lax.broadcasted_iota(jnp.int32, (1, LANE), 1)
    head_lane = jnp.logical_and(lane1 >= ML_GL, lane1 < ML_GL + ML_HEADS)
    gbr, gbt = gbr_ref[...], gbt_ref[...]

    gcs, fcm, cmx, a_row, grt = {}, {}, {}, {}, {}
    for bb, j in inst:
        gc = g_ref[bb, pl.ds(r0s[j], CHUNK), :] + gbr
        gcs[bb, j] = pltpu.roll(gc, ML_GL, axis=1)
        fcm[bb, j] = _dot_exact01(tri, _log_sigmoid(gc), lhs_is_01=True)
        grt[bb, j] = gt_ref[bb, cs[j]] + gbt
    row_id = lax.broadcasted_iota(jnp.int32, (len(inst), HP), 0)
    lfr = jnp.zeros((len(inst), HP), F32)
    for n, i in enumerate(inst):
        lfr = jnp.where(row_id == n, _log_sigmoid(grt[i][1:2]), lfr)
    fcr = _dot_exact01(lfr, tri_b, lhs_is_01=False)
    for n, i in enumerate(inst):
        a_row[i] = grt[i][0:1] - fcr[n:n + 1]
        cmx[i] = _cummax_rows(gcs[i] - fcm[i], rev)
    bx = {}
    e_neg, gd = {}, {}
    for bb in range(nbb):
        m_prev = m_ref[b0 + bb, 0:1, :]
        for j in range(N_SUB):
            i = (bb, j)
            m_t = fcm[i] + jnp.maximum(m_prev, cmx[i])
            m_new = m_t[last:last + 1]
            f_tot = fcm[i][last:last + 1]
            u = fcm[i] - m_t
            w_prev = jnp.exp(u + m_prev)
            w_s = jnp.exp(f_tot - fcm[i] + gcs[i] - m_new)
            gdec = jnp.broadcast_to(jnp.exp(f_tot + m_prev - m_new), (16, LANE))
            e_neg[i] = jnp.exp(-m_t)
            keep = lambda a: jnp.where(head_lane, a, 0.0)
            bx[i] = jnp.concatenate(_split_bf16(keep(u), 3) + _split_bf16(keep(w_prev), 2)
                                    + _split_bf16(keep(w_s), 2) + _split_bf16(keep(gdec), 2), axis=0)
            m_prev = m_new
        m_ref[b0 + bb] = jnp.broadcast_to(m_prev, (8, LANE))
    ub, wpb, wsb, gdb = {}, {}, {}, {}
    for i in inst:
        y = _dot(bx[i], sel_h)
        c = CHUNK
        ub[i] = y[0:c] + y[c:2 * c] + y[2 * c:3 * c]
        wpb[i] = y[3 * c:4 * c] + y[4 * c:5 * c]
        wsb[i] = y[5 * c:6 * c] + y[6 * c:7 * c]
        gdb[i] = y[7 * c:7 * c + 1] + y[7 * c + 16:7 * c + 17]
    q, v, qkw, ds = {}, {}, {}, {}
    for bb, j in inst:
        i = (bb, j)
        rs = pl.ds(r0s[j], CHUNK)
        w = jnp.where(valid4, jnp.exp(ub[i] + a_row[i]), 0.0)
        q[i] = qk_ref[bb, rs, 0:HP].astype(BF16)
        k = qk_ref[bb, rs, HP:2 * HP]
        kb = k.astype(BF16)
        kw = (k * wsb[i]).astype(BF16)
        v[i] = jnp.where(lane4 == ML_DH, 1.0, v_ref[bb, rs, :]).astype(BF16)
        for h, sl in enumerate(heads):
            sc = _dot_nt(q[i][:, sl], kb[:, sl])
            qkw[i, h] = (sc * w[:, h * LANE:h * LANE + CHUNK]).astype(BF16)
            ds[i, h] = _dot_tn(v[i][:, sl], kw[:, sl])
    s_in = {}
    for bb in range(nbb):
        for h, sl in enumerate(heads):
            st = st_ref[b0 + bb, h]
            for j in range(N_SUB):
                i = (bb, j)
                s_in[i, h] = st.astype(BF16)
                st = gdb[i][:, sl] * st + ds[i, h]
            st_ref[b0 + bb, h] = st
    num = {}
    for i in inst:
        parts = [_dot_nt(q[i][:, sl], s_in[i, h]) for h, sl in enumerate(heads)]
        intra = [_dot(qkw[i, h], v[i][:, sl]) for h, sl in enumerate(heads)]
        num[i] = wpb[i] * jnp.concatenate(parts, axis=1) + jnp.concatenate(intra, axis=1)
    for bb, j in inst:
        i = (bb, j)
        den = jnp.zeros((CHUNK, LANE), F32)
        for h, sl in enumerate(heads):
            dh = jnp.sum(jnp.where(lane_c == ML_DH, num[i][:, sl], 0.0), axis=-1, keepdims=True)
            den = jnp.where(lane_c == ML_GL + h, jnp.broadcast_to(dh, (CHUNK, LANE)), den)
        r1, r2 = _split_bf16(jnp.where(head_lane, 1.0 / jnp.maximum(jnp.abs(den), e_neg[i]), 0.0), 2)
        rb = _dot(r1, sel_h) + _dot(r2, sel_h)
        o_ref[bb, pl.ds(r0s[j], CHUNK), :] = jnp.where(lane4 < ML_DH, num[i] * rb, 0.0)


def _mlstm(qk3, pm3, gtl, gbr, gbt, *, rows, rev):
    nb, l, _ = qk3.shape
    nbb = SCAN_BATCHES
    dr = 1 if rev else 0
    pos = functools.partial(_scan_pos_static, rev, ncb=rows.ncb, ntb=rows.ntb)
    return pl.pallas_call(
        functools.partial(_mlstm_kernel, rev),
        out_shape=jax.ShapeDtypeStruct((nb, l, HP), F32),
        grid=(rows.ntb, nb // nbb),
        in_specs=[pl.BlockSpec((nbb, SEG, 2 * HP), lambda s, g: (g, pos(s), 0)),
                  pl.BlockSpec((nbb, SEG, HP), lambda s, g: (g, pos(s), 2)),
                  pl.BlockSpec((nbb, SEG, LANE), lambda s, g: (g, pos(s), 4 * ML_HEADS + dr)),
                  pl.BlockSpec((nbb, N_SUB, None, 2, HP), lambda s, g: (g, pos(s), dr, 0, 0)),
                  pl.BlockSpec((1, LANE), lambda s, g: (0, 0)),
                  pl.BlockSpec((2, HP), lambda s, g: (0, 0))],
        out_specs=pl.BlockSpec((nbb, SEG, HP), lambda s, g: (g, pos(s), 0)),
        scratch_shapes=[pltpu.VMEM((nb, ML_HEADS, LANE, LANE), F32), pltpu.VMEM((nb, 8, LANE), F32)],
        compiler_params=_cparams(("arbitrary", "arbitrary")),
        name="mlstm_scan_bwd" if rev else "mlstm_scan_fwd",
    )(qk3, pm3, pm3, gtl, gbr, gbt)


def _s5_kernel(nb, u_ref, bre_ref, bim_ref, are_ref, aim_ref, cre_ref, cim_ref, o_ref, xr_ref, xi_ref, st_ref):
    d, s = pl.program_id(0), pl.program_id(1)

    @pl.when(s == 0)
    def _():
        st_ref[...] = jnp.zeros(st_ref.shape, F32)

    u = u_ref[...].astype(BF16)
    xr_ref[...] = _dot(u, bre_ref[0])
    xi_ref[...] = _dot(u, bim_ref[0])
    ar, ai = are_ref[0], aim_ref[0]
    n_t = u_ref.shape[0] // nb

    def step(j, carry):
        sr, si = carry
        t = jnp.where(d == 0, j, n_t - 1 - j)
        r0 = pl.multiple_of(t * nb, nb)
        nr = ar * sr - ai * si + xr_ref[pl.ds(r0, nb), :]
        ni = ar * si + ai * sr + xi_ref[pl.ds(r0, nb), :]
        xr_ref[pl.ds(r0, nb), :] = nr
        xi_ref[pl.ds(r0, nb), :] = ni
        return nr, ni

    sr, si = lax.fori_loop(0, n_t, step, (st_ref[0], st_ref[1]), unroll=4)
    st_ref[0] = sr
    st_ref[1] = si
    o_ref[0] = _dot(xr_ref[...].astype(BF16), cre_ref[...]) - _dot(xi_ref[...].astype(BF16), cim_ref[...])


def _s5(ut, bre, bim, are, aim, cre, cim, *, nb, nc, nt):
    n_rows, ch = ut.shape
    tr = CHUNK * nb
    ns = bre.shape[-1]
    pos = functools.partial(_scan_pos, ncb=nc, ntb=nt)
    return pl.pallas_call(
        functools.partial(_s5_kernel, nb),
        out_shape=jax.ShapeDtypeStruct((2, n_rows, ch), F32),
        grid=(2, nt),
        in_specs=[pl.BlockSpec((tr, ch), lambda d, s: (pos(d, s), 0)),
                  pl.BlockSpec((1, ch, ns), lambda d, s: (d, 0, 0)),
                  pl.BlockSpec((1, ch, ns), lambda d, s: (d, 0, 0)),
                  pl.BlockSpec((1, nb, ns), lambda d, s: (d, 0, 0)),
                  pl.BlockSpec((1, nb, ns), lambda d, s: (d, 0, 0)),
                  pl.BlockSpec((ns, ch), lambda d, s: (0, 0)),
                  pl.BlockSpec((ns, ch), lambda d, s: (0, 0))],
        out_specs=pl.BlockSpec((1, tr, ch), lambda d, s: (d, pos(d, s), 0)),
        scratch_shapes=[pltpu.VMEM((tr, ns), F32), pltpu.VMEM((tr, ns), F32), pltpu.VMEM((2, nb, ns), F32)],
        compiler_params=_cparams(("arbitrary", "arbitrary")),
        name="s5_scan",
    )(ut, bre, bim, are, aim, cre, cim)


def _head_norm(o, gain, dim):
    parts = []
    for h in range(o.shape[1] // LANE):
        seg = o[:, h * LANE:(h + 1) * LANE]
        ms = jnp.sum(seg * seg, axis=-1, keepdims=True) * (1.0 / dim)
        parts.append(seg * lax.rsqrt(ms + EPS))
    return jnp.concatenate(parts, axis=1) * gain


def _mix_kernel(with_router, ogf_ref, ogb_ref, gg_ref, ys_ref, u_ref, omf_ref, omb_ref, mo_ref, h_ref, mod_ref,
                gn_ref, mn_ref, sd_ref, gw_ref, gb_ref, wo_ref, n2_ref, *rest):
    if with_router:
        wr_ref, ho_ref, f_ref, rt_ref = rest
    else:
        ho_ref, f_ref = rest
    gla = _head_norm(ogf_ref[...] + ogb_ref[...], gn_ref[...], GLA_DV) * _silu(gg_ref[...])
    z = _gelu_tanh(ys_ref[0] + ys_ref[1] + sd_ref[...] * u_ref[...])
    s5 = z * jax.nn.sigmoid(_dot(z.astype(BF16), gw_ref[...]) + gb_ref[...])
    ml = _head_norm(omf_ref[...] + omb_ref[...], mn_ref[...], ML_DH) * jax.nn.sigmoid(mo_ref[...])
    mix = (_dot(gla.astype(BF16), wo_ref[0:HP]) + _dot(s5.astype(BF16), wo_ref[HP:HP + 2 * LANE])
           + _dot(ml.astype(BF16), wo_ref[HP + 2 * LANE:]))
    m = mod_ref[0]
    hn = h_ref[...] + m[2:3] * mix
    ho_ref[...] = hn
    f = _rmsnorm(hn, n2_ref[...]) * (1.0 + m[4:5]) + m[3:4]
    f_ref[...] = f.astype(BF16)
    if with_router:
        logits = _dot(f, wr_ref[...], precision=HIGHEST)
        lane = lax.broadcasted_iota(jnp.int32, logits.shape, 1)
        l0 = jnp.where(lane < N_EXPERTS, logits, NEG)
        m1 = jnp.max(l0, axis=-1, keepdims=True)
        i1 = jnp.min(jnp.where(l0 == m1, lane, LANE), axis=-1, keepdims=True)
        l1 = jnp.where(lane == i1, NEG, l0)
        m2 = jnp.max(l1, axis=-1, keepdims=True)
        i2 = jnp.min(jnp.where(l1 == m2, lane, LANE), axis=-1, keepdims=True)
        e = jnp.exp(m2 - m1)
        w1 = 1.0 / (1.0 + e)
        w2 = e / (1.0 + e)
        rt_ref[...] = jnp.where(lane == 0, i1.astype(F32),
                                jnp.where(lane == 1, i2.astype(F32),
                                          jnp.where(lane == 2, w1, jnp.where(lane == 3, w2, 0.0))))


def _mix(ogf, ogb, pg, ys, pu, omf, omb, pm, h, modtab, gn, mn, sd, gw, gb, wo, n2, wr, *, layer, rows, lat_only):
    d = h.shape[1]
    tm = SEG
    n = rows.n_blocks(lat_only)
    src, sel = rows.src(lat_only), rows.sel(lat_only)
    full = lambda a: pl.BlockSpec(a.shape, lambda i: (0,) * a.ndim)
    with_router = wr is not None
    in_specs = [pl.BlockSpec((tm, HP), lambda i: (src(i), 0)),
                pl.BlockSpec((tm, HP), lambda i: (src(i), 0)),
                pl.BlockSpec((tm, HP), lambda i: (src(i), 3)),
                pl.BlockSpec((2, tm, 2 * LANE), lambda i: (0, src(i), 0)),
                pl.BlockSpec((tm, 2 * LANE), lambda i: (src(i), 0)),
                pl.BlockSpec((tm, HP), lambda i: (src(i), 0)),
                pl.BlockSpec((tm, HP), lambda i: (src(i), 0)),
                pl.BlockSpec((tm, HP), lambda i: (src(i), 3)),
                pl.BlockSpec((tm, d), lambda i: (src(i), 0)),
                pl.BlockSpec((None, 1, 8, d), lambda i: (layer, sel(i), 0, 0)),
                full(gn), full(mn), full(sd), full(gw), full(gb), full(wo), full(n2)]
    args = [ogf, ogb, pg, ys, pu, omf, omb, pm, h, modtab, gn, mn, sd, gw, gb, wo, n2]
    out_shape = [jax.ShapeDtypeStruct((n * tm, d), F32), jax.ShapeDtypeStruct((n * tm, d), BF16)]
    out_specs = [pl.BlockSpec((tm, d), lambda i: (i, 0)), pl.BlockSpec((tm, d), lambda i: (i, 0))]
    if with_router:
        in_specs.append(full(wr))
        args.append(wr)
        out_shape.append(jax.ShapeDtypeStruct((n * tm, LANE), F32))
        out_specs.append(pl.BlockSpec((tm, LANE), lambda i: (i, 0)))
    return pl.pallas_call(
        functools.partial(_mix_kernel, with_router),
        out_shape=tuple(out_shape),
        grid=(n,),
        in_specs=in_specs,
        out_specs=tuple(out_specs),
        compiler_params=_cparams(("arbitrary",)),
        name="mix_out",
    )(*args)


FF_TILE = 256


def _swiglu(xb, w1_ref, w3_ref, w2_ref, a_ref, lead=()):
    dff = w1_ref.shape[-1]
    for j in range(dff // FF_TILE):
        sl = slice(j * FF_TILE, (j + 1) * FF_TILE)
        h1 = _dot(xb, w1_ref[lead + (slice(None), sl)])
        h3 = _dot(xb, w3_ref[lead + (slice(None), sl)])
        a_ref[:, sl] = (_silu(h1) * h3).astype(BF16)
    return _dot(a_ref[...], w2_ref[lead + (slice(None), slice(None))])


FFN_TM = 1024


def _ffn_kernel(final, sel, f_ref, h_ref, mod_ref, w1_ref, w3_ref, w2_ref, *rest):
    if final:
        nf_ref, o_ref, a_ref = rest
    else:
        o_ref, a_ref = rest
    y = _swiglu(f_ref[...], w1_ref, w3_ref, w2_ref, a_ref)
    n_seg = f_ref.shape[0] // SEG
    for q in range(n_seg):
        rs = slice(q * SEG, (q + 1) * SEG)
        gate = mod_ref[sel(pl.program_id(0) * n_seg + q)][5:6]
        hn = h_ref[rs, :] + gate * y[rs]
        o_ref[rs, :] = _rmsnorm(hn, nf_ref[...]) if final else hn


def _ffn(f, h, modtab, w1, w3, w2, nf, *, layer, rows, lat_only):
    r, d = h.shape
    tm = FFN_TM
    assert r % tm == 0
    dff = w1.shape[-1]
    full = lambda a: pl.BlockSpec(a.shape, lambda i: (0,) * a.ndim)
    resident = lambda a: pl.BlockSpec(a.shape, lambda i: (0,) * a.ndim, pipeline_mode=pl.Buffered(1))
    in_specs = [pl.BlockSpec((tm, d), lambda i: (i, 0)),
                pl.BlockSpec((tm, d), lambda i: (i, 0)),
                pl.BlockSpec((None,) + modtab.shape[1:], lambda i: (layer, 0, 0, 0)),
                resident(w1), resident(w3), resident(w2)]
    args = [f, h, modtab, w1, w3, w2]
    if lat_only:
        in_specs.append(full(nf))
        args.append(nf)
    return pl.pallas_call(
        functools.partial(_ffn_kernel, lat_only, rows.sel(lat_only)),
        out_shape=jax.ShapeDtypeStruct((r, d), F32),
        grid=(r // tm,),
        in_specs=in_specs,
        out_specs=pl.BlockSpec((tm, d), lambda i: (i, 0)),
        scratch_shapes=[pltpu.VMEM((tm, dff), BF16)],
        compiler_params=_cparams(("arbitrary",)),
        name="ffn",
    )(*args)


MOE_TM = 512


def _moe_kernel(be_ref, live_ref, x_ref, rw_ref, w1_ref, w3_ref, w2_ref, o_ref, a_ref):
    i = pl.program_id(0)

    @pl.when(live_ref[i] > 0)
    def _():
        y = _swiglu(x_ref[...], w1_ref, w3_ref, w2_ref, a_ref, lead=(0,))
        o_ref[...] = y * rw_ref[:, 0:1]

    @pl.when(live_ref[i] == 0)
    def _():
        o_ref[...] = jnp.zeros(o_ref.shape, F32)


def _moe_experts(block_expert, block_live, xg, rw, w1, w3, w2):
    n_rows, d = xg.shape
    dff = w1.shape[-1]
    tm = MOE_TM
    return pl.pallas_call(
        _moe_kernel,
        out_shape=jax.ShapeDtypeStruct((n_rows, d), F32),
        grid_spec=pltpu.PrefetchScalarGridSpec(
            num_scalar_prefetch=2,
            grid=(n_rows // tm,),
            in_specs=[pl.BlockSpec((tm, d), lambda i, be, lv: (i, 0)),
                      pl.BlockSpec((tm, LANE), lambda i, be, lv: (i, 0)),
                      pl.BlockSpec((1, d, dff), lambda i, be, lv: (be[i], 0, 0)),
                      pl.BlockSpec((1, d, dff), lambda i, be, lv: (be[i], 0, 0)),
                      pl.BlockSpec((1, dff, d), lambda i, be, lv: (be[i], 0, 0))],
            out_specs=pl.BlockSpec((tm, d), lambda i, be, lv: (i, 0)),
            scratch_shapes=[pltpu.VMEM((tm, dff), BF16)]),
        compiler_params=_cparams(("arbitrary",)),
        name="moe_experts",
    )(block_expert, block_live, xg, rw, w1, w3, w2)


def _resid_kernel(final, h_ref, y0_ref, y1_ref, mod_ref, *rest):
    hn = h_ref[...] + mod_ref[0][5:6] * (y0_ref[...] + y1_ref[...])
    if final:
        nf_ref, o_ref = rest
        o_ref[...] = _rmsnorm(hn, nf_ref[...])
    else:
        (o_ref,) = rest
        o_ref[...] = hn


def _moe_resid(h, y0, y1, modtab, nf, *, layer, rows, lat_only):
    r, d = h.shape
    tm = SEG
    sel = rows.sel(lat_only)
    row = lambda i: (i, 0)
    in_specs = [pl.BlockSpec((tm, d), row), pl.BlockSpec((tm, d), row), pl.BlockSpec((tm, d), row),
                pl.BlockSpec((None, 1, 8, d), lambda i: (layer, sel(i), 0, 0))]
    args = [h, y0, y1, modtab]
    if lat_only:
        in_specs.append(pl.BlockSpec((1, d), lambda i: (0, 0)))
        args.append(nf)
    return pl.pallas_call(
        functools.partial(_resid_kernel, lat_only),
        out_shape=jax.ShapeDtypeStruct((r, d), F32),
        grid=(r // tm,),
        in_specs=in_specs,
        out_specs=pl.BlockSpec((tm, d), row),
        compiler_params=_cparams(("arbitrary",)),
        name="moe_resid",
    )(*args)


def _pad_heads(w, heads, dim):
    lead = w.shape[:-1]
    w = w.reshape(lead + (heads, dim))
    w = jnp.pad(w, [(0, 0)] * len(lead) + [(0, 0), (0, LANE - dim)])
    return w.reshape(lead + (heads * LANE,))


def _pad_last(w, to):
    return jnp.pad(w, [(0, 0)] * (w.ndim - 1) + [(0, to - w.shape[-1])])


def _pos_embed(n_tokens, d):
    n_grid_rows = n_tokens // GRID_W
    row, col = jnp.meshgrid(jnp.arange(n_grid_rows, dtype=F32), jnp.arange(GRID_W, dtype=F32), indexing='ij')
    n_freq = d // 4
    omega = jnp.exp(-math.log(POS_BASE) * jnp.arange(n_freq, dtype=F32) / n_freq)

    def axis_embed(p):
        ang = p.reshape(-1, 1) * omega
        return jnp.concatenate([jnp.sin(ang), jnp.cos(ang)], axis=-1)

    return jnp.concatenate([axis_embed(row), axis_embed(col)], axis=-1)


def _s5_discretise(lam_re, lam_im, log_dt, b_re, b_im):
    dt = jnp.exp(log_dt)[:, None]
    mag = jnp.exp(lam_re * dt)
    abar_re, abar_im = mag * jnp.cos(lam_im * dt), mag * jnp.sin(lam_im * dt)
    den = lam_re * lam_re + lam_im * lam_im
    pr, pi = abar_re - 1.0, abar_im
    coef_re = (pr * lam_re + pi * lam_im) / den
    coef_im = (pi * lam_re - pr * lam_im) / den
    bbar_re = coef_re[..., None] * b_re - coef_im[..., None] * b_im
    bbar_im = coef_re[..., None] * b_im + coef_im[..., None] * b_re
    return abar_re, abar_im, bbar_re, bbar_im


def _block_diag(m):
    g, a, b = m.shape
    eye = jnp.eye(g, dtype=m.dtype)
    return (eye[:, None, :, None] * m[:, :, None, :]).reshape(g * a, g * b)


def _route_plan(route, tm):
    n_tok = route.shape[0]
    n_assign = n_tok * TOP_K
    flat_e = route[:, 0:TOP_K].astype(jnp.int32).T.reshape(-1)
    flat_w = route[:, TOP_K:2 * TOP_K].T.reshape(-1)
    onehot = (jnp.arange(N_EXPERTS, dtype=jnp.int32)[:, None] == flat_e[None, :]).astype(jnp.int32)
    csum = jnp.cumsum(onehot, axis=1)
    counts = csum[:, -1]
    padded = (counts + tm - 1) // tm * tm
    pend = jnp.cumsum(padded)
    pstart = pend - padded
    dest = jnp.sum(onehot * (csum - 1 + pstart[:, None]), axis=0)
    n_blocks = -(-n_assign // tm) + N_EXPERTS
    n_rows = n_blocks * tm
    row_assign = jnp.full((n_rows,), -1, jnp.int32).at[dest].set(jnp.arange(n_assign, dtype=jnp.int32))
    live = row_assign >= 0
    ra = jnp.maximum(row_assign, 0)
    row_token = ra % n_tok
    row_w = jnp.where(live, flat_w[ra], 0.0)
    block_start = jnp.arange(n_blocks, dtype=jnp.int32) * tm
    block_expert = jnp.minimum(jnp.searchsorted(pend, block_start, side='right'), N_EXPERTS - 1).astype(jnp.int32)
    block_live = (block_start < (pstart + counts)[block_expert]).astype(jnp.int32)
    return row_token, row_w, block_expert, block_live, dest.reshape(TOP_K, n_tok)


def kernel(x, c, ctx, c_ctx, w_ada, b_ada, norm1, norm2, w_in, w_out, gla_wa2, gla_ba, gla_norm, s5_lam_re, s5_lam_im, s5_log_dt, s5_b_re, s5_b_im, s5_c_re, s5_c_im, s5_d, s5_glu_w, s5_glu_b, ml_conv_w, ml_conv_b, ml_gate_b, ml_norm, ffn_w1, ffn_w3, ffn_w2, moe_router, moe_w1, moe_w3, moe_w2, norm_f):
    nb, n_lat, d = x.shape
    lc = ctx.shape[1]
    depth = w_ada.shape[0]
    l = lc + n_lat
    assert lc % SEG == 0 and n_lat % SEG == 0 and nb == 8 and nb % SCAN_BATCHES == 0
    rows = _Rows(nb, lc // SEG, l // SEG)

    h = _embed(ctx.reshape(nb * lc, d), x.reshape(nb * n_lat, d), _pos_embed(n_lat, d), rows)

    cond = jnp.zeros((16, d), F32).at[:nb].set(c).at[nb].set(c_ctx)
    mod = _modulation(cond, w_ada, b_ada)
    modtab = jnp.pad(mod.reshape(depth, 16, 6, d), ((0, 0), (0, 0), (0, 2), (0, 0)))

    dk, dv, dh = GLA_HEADS * GLA_DK, GLA_HEADS * GLA_DV, ML_HEADS * ML_DH
    s5c = s5_d.shape[-1]
    cuts = np.cumsum([dk, dk, dv, GLA_RANK, dv, s5c, dh, dh, dh, dh, 4 * ML_HEADS])

    for i in range(depth):
        last = i == depth - 1
        gq, gk, gv, glr, gg, su, mq, mk, mv, mo, mg = jnp.split(w_in[i], cuts[:-1], axis=-1)
        w_all = jnp.concatenate([
            _pad_heads(gq, GLA_HEADS, GLA_DK), _pad_heads(gk, GLA_HEADS, GLA_DK),
            _pad_heads(gv, GLA_HEADS, GLA_DV), _pad_heads(gg, GLA_HEADS, GLA_DV), _pad_last(glr, LANE),
            su,
            _pad_heads(mq, ML_HEADS, ML_DH), _pad_heads(mk, ML_HEADS, ML_DH),
            _pad_heads(mv, ML_HEADS, ML_DH), _pad_heads(mo, ML_HEADS, ML_DH),
            _pad_last(mg[:, :2 * ML_HEADS], LANE), _pad_last(mg[:, 2 * ML_HEADS:], LANE)],
            axis=-1).astype(BF16)
        wgt = mg.T.astype(BF16)
        pg, pu, pm, gt = _proj(h, modtab, norm1[i][None], w_all, wgt, layer=i, rows=rows)

        wa = jnp.pad(_pad_heads(gla_wa2[i], GLA_HEADS, GLA_DK), ((0, 0), (0, LANE - GLA_RANK), (0, 0)))
        ba = _pad_heads(gla_ba[i], GLA_HEADS, GLA_DK)[:, None, :]
        pg3 = pg.reshape(nb, l, NG)
        ogf = _gla(pg3, wa[0], ba[0], rows=rows, rev=False).reshape(nb * l, HP)
        ogb = _gla(pg3, wa[1], ba[1], rows=rows, rev=True).reshape(nb * l, HP)

        bres, bims, ares, aims = [], [], [], []
        for dr in (0, 1):
            a_re, a_im, b_re, b_im = _s5_discretise(s5_lam_re[i, dr], s5_lam_im[i, dr], s5_log_dt[i, dr],
                                                    s5_b_re[i], s5_b_im[i])
            bres.append(_block_diag(jnp.swapaxes(b_re, 1, 2)))
            bims.append(_block_diag(jnp.swapaxes(b_im, 1, 2)))
            ares.append(jnp.broadcast_to(a_re.reshape(1, -1), (nb, a_re.size)))
            aims.append(jnp.broadcast_to(a_im.reshape(1, -1), (nb, a_im.size)))
        cre = _block_diag(jnp.swapaxes(s5_c_re[i], 1, 2)).astype(BF16)
        cim = _block_diag(jnp.swapaxes(s5_c_im[i], 1, 2)).astype(BF16)
        ut = pu.reshape(nb, l, s5c).swapaxes(0, 1).reshape(l * nb, s5c)
        yt = _s5(ut, jnp.stack(bres).astype(BF16), jnp.stack(bims).astype(BF16), jnp.stack(ares), jnp.stack(aims),
                 cre, cim, nb=nb, nc=lc // CHUNK, nt=l // CHUNK)
        ys = yt.reshape(2, l, nb, s5c).swapaxes(1, 2).reshape(2, nb * l, s5c)

        cw = jnp.concatenate([_pad_heads(ml_conv_w[i][:, :dh], ML_HEADS, ML_DH),
                              _pad_heads(ml_conv_w[i][:, dh:], ML_HEADS, ML_DH)], axis=-1)
        cw = jnp.pad(cw, ((0, 8 - ML_CONV), (0, 0)))
        cb = jnp.concatenate([_pad_heads(ml_conv_b[i][:dh], ML_HEADS, ML_DH),
                              _pad_heads(ml_conv_b[i][dh:], ML_HEADS, ML_DH)])[None]
        post = jnp.concatenate([jnp.ones((HP,), F32), jnp.full((HP,), ML_DH ** -0.5, F32)])[None]
        qk = _conv(pm, cw, cb, post, rows=rows)
        gb = ml_gate_b[i].reshape(2, 2, ML_HEADS)
        gbr = _pad_last(gb.reshape(2, 1, 2 * ML_HEADS), LANE)
        gbt = _pad_last(jnp.broadcast_to(gb[..., None], (2, 2, ML_HEADS, CHUNK)), LANE).reshape(2, 2, HP)
        gtl = _pad_last(gt.reshape(2, 2, ML_HEADS, nb, l // CHUNK, CHUNK).transpose(3, 4, 0, 1, 2, 5),
                        LANE).reshape(nb, l // CHUNK, 2, 2, HP)
        qk3, pm3 = qk.reshape(nb, l, 2 * HP), pm.reshape(nb, l, NM)
        omf = _mlstm(qk3, pm3, gtl, gbr[0], gbt[0], rows=rows, rev=False).reshape(nb * l, HP)
        omb = _mlstm(qk3, pm3, gtl, gbr[1], gbt[1], rows=rows, rev=True).reshape(nb * l, HP)

        wo = w_out[i]
        wo_p = jnp.concatenate([
            jnp.pad(wo[:dv].reshape(GLA_HEADS, GLA_DV, d), ((0, 0), (0, LANE - GLA_DV), (0, 0))).reshape(HP, d),
            wo[dv:dv + s5c],
            jnp.pad(wo[dv + s5c:].reshape(ML_HEADS, ML_DH, d), ((0, 0), (0, LANE - ML_DH), (0, 0))).reshape(HP, d)],
            axis=0).astype(BF16)
        gn = jnp.tile(_pad_last(gla_norm[i], LANE), GLA_HEADS)[None]
        mn = jnp.tile(_pad_last(ml_norm[i], LANE), ML_HEADS)[None]
        is_moe = i % 2 == 1
        j = i // 2
        wr = _pad_last(moe_router[j], LANE) if is_moe else None
        outs = _mix(ogf, ogb, pg, ys, pu, omf, omb, pm, h, modtab, gn, mn, s5_d[i][None], s5_glu_w[i].astype(BF16),
                    s5_glu_b[i][None], wo_p, norm2[i][None], wr, layer=i, rows=rows, lat_only=last)
        if not is_moe:
            h, f = outs
            h = _ffn(f, h, modtab, ffn_w1[j].astype(BF16), ffn_w3[j].astype(BF16), ffn_w2[j].astype(BF16),
                     norm_f[None], layer=i, rows=rows, lat_only=last)
        else:
            h, f, route = outs
            row_token, row_w, block_expert, block_live, dest = _route_plan(route, MOE_TM)
            take = lambda a, idx: a.at[idx].get(mode='promise_in_bounds')
            xg = take(f, row_token)
            rw = jnp.broadcast_to(row_w[:, None], (row_w.shape[0], LANE))
            yg = _moe_experts(block_expert, block_live, xg, rw, moe_w1[j].astype(BF16), moe_w3[j].astype(BF16),
                              moe_w2[j].astype(BF16))
            y0 = take(yg, dest[0])
            y1 = take(yg, dest[1])
            h = _moe_resid(h, y0, y1, modtab, norm_f[None], layer=i, rows=rows, lat_only=last)
    return h.reshape(nb, n_lat, d)
```

```python
import functools
import math

import numpy as np
import jax
import jax.numpy as jnp
from jax import lax
from jax.experimental import pallas as pl
from jax.experimental.pallas import tpu as pltpu
from jax.experimental.pallas import tpu_sc as plsc

F32 = jnp.float32
BF16 = jnp.bfloat16
HIGHEST = lax.Precision.HIGHEST

GRID_W = 64
POS_BASE = 10000.0
EPS = 1e-6
GLA_HEADS, GLA_DK, GLA_DV, GLA_RANK, GLA_GATE_NORM = 4, 48, 96, 16, 16.0
S5_GROUP, S5_STATE = 16, 64
ML_HEADS, ML_DH, ML_CONV = 4, 96, 3
N_EXPERTS, TOP_K = 8, 2

LANE = 128
CHUNK = 64
SEG = 256
N_SUB = SEG // CHUNK
SCAN_BATCHES = 2
NEG = -1e30
VMEM_LIMIT = 56 * 1024 * 1024

HP = LANE * GLA_HEADS
NG = 4 * HP + LANE
NM = 2 * HP + 2 * LANE


def _cparams(sem):
    return pltpu.CompilerParams(dimension_semantics=sem, vmem_limit_bytes=VMEM_LIMIT)


def _dot(a, b, **kw):
    return jnp.dot(a, b, preferred_element_type=F32, **kw)


def _dot_nt(a, b, **kw):
    return lax.dot_general(a, b, (((1,), (1,)), ((), ())), preferred_element_type=F32, **kw)


def _dot_tn(a, b, **kw):
    return lax.dot_general(a, b, (((0,), (0,)), ((), ())), preferred_element_type=F32, **kw)


def _log_sigmoid(x):
    return jnp.minimum(x, 0.0) - jnp.log1p(jnp.exp(-jnp.abs(x)))


def _silu(x):
    return x * jax.nn.sigmoid(x)


def _gelu_tanh(x):
    return 0.5 * x * (1.0 + jnp.tanh(math.sqrt(2.0 / math.pi) * (x + 0.044715 * (x * x * x))))


def _rmsnorm(x, g):
    return x * lax.rsqrt(jnp.mean(x * x, axis=-1, keepdims=True) + EPS) * g


class _Rows:
    def __init__(self, nb, ncb, ntb):
        self.nb, self.ncb, self.ntb, self.nlb = nb, ncb, ntb, ntb - ncb

    def n_blocks(self, lat_only):
        return self.nb * (self.nlb if lat_only else self.ntb)

    def src(self, lat_only):
        if lat_only:
            return lambda i: (i // self.nlb) * self.ntb + self.ncb + i % self.nlb
        return lambda i: i

    def sel(self, lat_only):
        if lat_only:
            return lambda i: i // self.nlb
        return lambda i: jnp.where(i % self.ntb < self.ncb, self.nb, i // self.ntb)


def _embed_kernel(ncb, ntb, ctx_ref, x_ref, pos_ref, o_ref):
    j = pl.program_id(0) % ntb

    @pl.when(j < ncb)
    def _():
        o_ref[...] = ctx_ref[...]

    @pl.when(j >= ncb)
    def _():
        o_ref[...] = x_ref[...] + pos_ref[...]


def _embed(ctx2, x2, pos, rows):
    d = ctx2.shape[1]
    ncb, ntb, nlb = rows.ncb, rows.ntb, rows.nlb
    return pl.pallas_call(
        functools.partial(_embed_kernel, ncb, ntb),
        out_shape=jax.ShapeDtypeStruct((rows.nb * ntb * SEG, d), F32),
        grid=(rows.nb * ntb,),
        in_specs=[pl.BlockSpec((SEG, d), lambda i: ((i // ntb) * ncb + jnp.minimum(i % ntb, ncb - 1), 0)),
                  pl.BlockSpec((SEG, d), lambda i: ((i // ntb) * nlb + jnp.maximum(i % ntb - ncb, 0), 0)),
                  pl.BlockSpec((SEG, d), lambda i: (jnp.maximum(i % ntb - ncb, 0), 0))],
        out_specs=pl.BlockSpec((SEG, d), lambda i: (i, 0)),
        compiler_params=_cparams(("arbitrary",)),
        name="embed",
    )(ctx2, x2, pos)


def _mod_kernel(c_ref, w_ref, b_ref, o_ref):
    s = _silu(c_ref[...])
    o_ref[0] = _dot(s, w_ref[0], precision=HIGHEST) + b_ref[0]


def _modulation(cond, w_ada, b_ada):
    depth, d, n6 = w_ada.shape
    tn = n6 // 4
    n_rows = cond.shape[0]
    return pl.pallas_call(
        _mod_kernel,
        out_shape=jax.ShapeDtypeStruct((depth, n_rows, n6), F32),
        grid=(depth, n6 // tn),
        in_specs=[pl.BlockSpec((n_rows, d), lambda l, j: (0, 0)),
                  pl.BlockSpec((1, d, tn), lambda l, j: (l, 0, j)),
                  pl.BlockSpec((1, 1, tn), lambda l, j: (l, 0, j))],
        out_specs=pl.BlockSpec((1, n_rows, tn), lambda l, j: (l, 0, j)),
        compiler_params=_cparams(("arbitrary", "arbitrary")),
        name="modulation",
    )(cond, w_ada, b_ada.reshape(depth, 1, n6))


def _proj_kernel(ncb, ntb, h_ref, hp_ref, hn_ref, mod_ref, g_ref, w_ref, wgt_ref, cw_ref, cb_ref, post_ref,
                 pg_ref, pu_ref, pm_ref, qk_ref, gt_ref):
    j = pl.program_id(0) % ntb
    m = mod_ref[0]
    act = lambda x: (_rmsnorm(x, g_ref[...]) * (1.0 + m[1:2]) + m[0:1]).astype(BF16)
    ab = act(h_ref[...])
    c1, c2 = NG + 2 * LANE, NG + 2 * LANE + NM
    pg_ref[...] = _dot(ab, w_ref[:, 0:NG])
    pu_ref[...] = _dot(ab, w_ref[:, NG:c1])
    pm_ref[...] = _dot(ab, w_ref[:, c1:c2])
    gt_ref[...] = _dot_nt(wgt_ref[...], ab)
    x = _dot(ab, w_ref[:, c2:])
    halo = _dot(act(jnp.concatenate([hp_ref[...], hn_ref[...]], axis=0)), w_ref[:, c2:])
    tm = x.shape[0]
    first = jnp.logical_or(j == 0, j == ncb).astype(F32)
    last = jnp.logical_or(j == ncb - 1, j == ntb - 1).astype(F32)
    row = lax.broadcasted_iota(jnp.int32, x.shape, 0)
    xp = jnp.where(row == 0, halo[7:8, :] * (1.0 - first), pltpu.roll(x, 1, axis=0))
    xn = jnp.where(row == tm - 1, halo[8:9, :] * (1.0 - last), pltpu.roll(x, tm - 1, axis=0))
    y = cw_ref[0:1] * xp + cw_ref[1:2] * x + cw_ref[2:3] * xn + cb_ref[...]
    qk_ref[...] = _silu(y) * post_ref[...]


def _proj(h, modtab, g, w, wgt, cw, cb, post, *, layer, rows):
    r, d = h.shape
    tm = SEG
    t8 = tm // 8
    sel = rows.sel(False)
    full = lambda a: pl.BlockSpec(a.shape, lambda i: (0,) * a.ndim)
    return pl.pallas_call(
        functools.partial(_proj_kernel, rows.ncb, rows.ntb),
        out_shape=(jax.ShapeDtypeStruct((r, NG), F32), jax.ShapeDtypeStruct((r, 2 * LANE), F32),
                   jax.ShapeDtypeStruct((r, NM), F32), jax.ShapeDtypeStruct((r, 2 * HP), F32),
                   jax.ShapeDtypeStruct((16, r), F32)),
        grid=(r // tm,),
        in_specs=[pl.BlockSpec((tm, d), lambda i: (i, 0)),
                  pl.BlockSpec((8, d), lambda i: (jnp.maximum(i * t8 - 1, 0), 0)),
                  pl.BlockSpec((8, d), lambda i: (jnp.minimum((i + 1) * t8, r // 8 - 1), 0)),
                  pl.BlockSpec((None, 1, 8, d), lambda i: (layer, sel(i), 0, 0)),
                  full(g), full(w), full(wgt), full(cw), full(cb), full(post)],
        out_specs=(pl.BlockSpec((tm, NG), lambda i: (i, 0)),
                   pl.BlockSpec((tm, 2 * LANE), lambda i: (i, 0)),
                   pl.BlockSpec((tm, NM), lambda i: (i, 0)),
                   pl.BlockSpec((tm, 2 * HP), lambda i: (i, 0)),
                   pl.BlockSpec((16, tm), lambda i: (0, i))),
        compiler_params=_cparams(("arbitrary",)),
        name="proj",
    )(h, h, h, modtab, g, w, wgt, cw, cb, post)


def _conv_kernel(ncb, ntb, x_ref, prev_ref, next_ref, w_ref, b_ref, post_ref, o_ref):
    j = pl.program_id(0) % ntb
    x = x_ref[...]
    tm = x.shape[0]
    first = jnp.logical_or(j == 0, j == ncb).astype(F32)
    last = jnp.logical_or(j == ncb - 1, j == ntb - 1).astype(F32)
    row = lax.broadcasted_iota(jnp.int32, x.shape, 0)
    xp = jnp.where(row == 0, prev_ref[7:8, :] * (1.0 - first), pltpu.roll(x, 1, axis=0))
    xn = jnp.where(row == tm - 1, next_ref[0:1, :] * (1.0 - last), pltpu.roll(x, tm - 1, axis=0))
    y = w_ref[0:1] * xp + w_ref[1:2] * x + w_ref[2:3] * xn + b_ref[...]
    o_ref[...] = _silu(y) * post_ref[...]


def _conv(pm, w, b, post, *, rows):
    r = pm.shape[0]
    tm = SEG
    wq = 2 * HP
    t8 = tm // 8
    return pl.pallas_call(
        functools.partial(_conv_kernel, rows.ncb, rows.ntb),
        out_shape=jax.ShapeDtypeStruct((r, wq), F32),
        grid=(r // tm,),
        in_specs=[pl.BlockSpec((tm, wq), lambda i: (i, 0)),
                  pl.BlockSpec((8, wq), lambda i: (jnp.maximum(i * t8 - 1, 0), 0)),
                  pl.BlockSpec((8, wq), lambda i: (jnp.minimum((i + 1) * t8, r // 8 - 1), 0)),
                  pl.BlockSpec((8, wq), lambda i: (0, 0)),
                  pl.BlockSpec((1, wq), lambda i: (0, 0)),
                  pl.BlockSpec((1, wq), lambda i: (0, 0))],
        out_specs=pl.BlockSpec((tm, wq), lambda i: (i, 0)),
        compiler_params=_cparams(("arbitrary",)),
        name="conv",
    )(pm, pm, pm, w, b, post)


def _scan_pos(d, s, ncb, ntb):
    rev = jnp.where(s < ncb, ncb - 1 - s, ntb - 1 - (s - ncb))
    return jnp.where(d == 0, s, rev)


def _scan_pos_static(rev, s, ncb, ntb):
    if not rev:
        return s
    return jnp.where(s < ncb, ncb - 1 - s, ntb - 1 - (s - ncb))


def _tri(rev):
    r = lax.broadcasted_iota(jnp.int32, (CHUNK, CHUNK), 0)
    c = lax.broadcasted_iota(jnp.int32, (CHUNK, CHUNK), 1)
    return (r <= c) if rev else (r >= c)


def _chunk_rows(rev):
    return [(N_SUB - 1 - j if rev else j) * CHUNK for j in range(N_SUB)]


def _gla_kernel(rev, p_ref, wa_ref, ba_ref, o_ref, st_ref):
    s, g = pl.program_id(0), pl.program_id(1)
    nbb = p_ref.shape[0]
    b0 = g * nbb

    @pl.when(s == 0)
    def _():
        st_ref[pl.ds(b0, nbb)] = jnp.zeros((nbb,) + st_ref.shape[1:], F32)

    valid = _tri(rev)
    tri = valid.astype(F32)
    r0s = _chunk_rows(rev)
    wa, ba = wa_ref[...], ba_ref[...]
    inst = [(bb, j) for bb in range(nbb) for j in range(N_SUB)]
    heads = [slice(h * LANE, (h + 1) * LANE) for h in range(GLA_HEADS)]

    la = {}
    for bb, j in inst:
        lr = p_ref[bb, pl.ds(r0s[j], CHUNK), 4 * HP:4 * HP + LANE]
        la[bb, j] = _log_sigmoid(_dot(lr, wa) + ba) * (1.0 / GLA_GATE_NORM)
    bc, e_last = {}, {}
    for i in inst:
        bc[i] = _dot_exact01(tri, la[i], lhs_is_01=True)
        e_last[i] = jnp.exp(jnp.sum(la[i], axis=0, keepdims=True))
    q_in, k_in, k_out, v = {}, {}, {}, {}
    for bb, j in inst:
        i = (bb, j)
        rs = pl.ds(r0s[j], CHUNK)
        q_in[i] = (p_ref[bb, rs, 0:HP] * (GLA_DK ** -0.5) * jnp.exp(bc[i])).astype(BF16)
        kd = p_ref[bb, rs, HP:2 * HP] * jnp.exp(-bc[i])
        k_out[i] = (kd * e_last[i]).astype(BF16)
        k_in[i] = kd.astype(BF16)
        v[i] = p_ref[bb, rs, 2 * HP:3 * HP].astype(BF16)
    att = {}
    for i in inst:
        for h, sl in enumerate(heads):
            att[i, h] = jnp.where(valid, _dot_nt(q_in[i][:, sl], k_in[i][:, sl]), 0.0).astype(BF16)
    o_intra, ds = {}, {}
    for i in inst:
        for h, sl in enumerate(heads):
            o_intra[i, h] = _dot(att[i, h], v[i][:, sl])
            ds[i, h] = _dot_tn(v[i][:, sl], k_out[i][:, sl])
    s_in = {}
    for bb in range(nbb):
        for h, sl in enumerate(heads):
            st = st_ref[b0 + bb, h]
            for j in range(N_SUB):
                s_in[(bb, j), h] = st.astype(BF16)
                st = st * e_last[bb, j][:, sl] + ds[(bb, j), h]
            st_ref[b0 + bb, h] = st
    for bb, j in inst:
        for h, sl in enumerate(heads):
            o = o_intra[(bb, j), h] + _dot_nt(q_in[bb, j][:, sl], s_in[(bb, j), h])
            o_ref[bb, pl.ds(r0s[j], CHUNK), sl] = o


def _gla(pg3, wa, ba, *, rows, rev):
    nb, l, _ = pg3.shape
    nbb = SCAN_BATCHES
    pos = functools.partial(_scan_pos_static, rev, ncb=rows.ncb, ntb=rows.ntb)
    return pl.pallas_call(
        functools.partial(_gla_kernel, rev),
        out_shape=jax.ShapeDtypeStruct((nb, l, HP), F32),
        grid=(rows.ntb, nb // nbb),
        in_specs=[pl.BlockSpec((nbb, SEG, NG), lambda s, g: (g, pos(s), 0)),
                  pl.BlockSpec((LANE, HP), lambda s, g: (0, 0)),
                  pl.BlockSpec((1, HP), lambda s, g: (0, 0))],
        out_specs=pl.BlockSpec((nbb, SEG, HP), lambda s, g: (g, pos(s), 0)),
        scratch_shapes=[pltpu.VMEM((nb, GLA_HEADS, LANE, LANE), F32)],
        compiler_params=_cparams(("arbitrary", "arbitrary")),
        name="gla_scan_bwd" if rev else "gla_scan_fwd",
    )(pg3, wa, ba)


def _mlstm_kernel_old(qk_ref, v_ref, g_ref, gt_ref, gbr_ref, gbc_ref, o_ref, st_ref, m_ref):
    d, s, g = pl.program_id(0), pl.program_id(1), pl.program_id(2)
    nbb = qk_ref.shape[0]
    b0 = g * nbb

    @pl.when(s == 0)
    def _():
        st_ref[pl.ds(b0, nbb)] = jnp.zeros((nbb,) + st_ref.shape[1:], F32)
        m_ref[pl.ds(b0, nbb)] = jnp.zeros((nbb,) + m_ref.shape[1:], F32)

    fwd = d == 0
    valid = _tri(d)
    tri = valid.astype(F32)
    r0s = _chunk_rows(d)
    cidx = [jnp.where(fwd, j, N_SUB - 1 - j) for j in range(N_SUB)]
    lane = lax.broadcasted_iota(jnp.int32, (CHUNK, LANE), 1)
    inst = [(bb, j) for bb in range(nbb) for j in range(N_SUB)]
    hinst = [(bb, j, h) for bb, j in inst for h in range(ML_HEADS)]
    gbr, gbc = gbr_ref[...], gbc_ref[...]

    def pick(a, h, kind, axis):
        i0, i1 = kind * ML_HEADS + h, (2 + kind) * ML_HEADS + h
        if axis == 1:
            return jnp.where(fwd, a[:, i0:i0 + 1], a[:, i1:i1 + 1])
        return jnp.where(fwd, a[i0:i0 + 1, :], a[i1:i1 + 1, :])

    gc, gr, fcum_c, fcum_r = {}, {}, {}, {}
    for bb, j in inst:
        i = (bb, j)
        gc[i] = g_ref[bb, pl.ds(r0s[j], CHUNK), :] + gbr
        gr[i] = gt_ref[bb, cidx[j]] + gbc
    for i in inst:
        fcum_c[i] = _dot(tri, _log_sigmoid(gc[i]), precision=HIGHEST)
        fcum_r[i] = _dot_nt(_log_sigmoid(gr[i]), tri, precision=HIGHEST)
    fc, lic, d_log, rmax = {}, {}, {}, {}
    for bb, j, h in hinst:
        i = (bb, j)
        fc[bb, j, h] = pick(fcum_c[i], h, 1, 1)
        lic[bb, j, h] = pick(gc[i], h, 0, 1)
        dl = jnp.where(valid, fc[bb, j, h] - pick(fcum_r[i], h, 1, 0) + pick(gr[i], h, 0, 0), NEG)
        d_log[bb, j, h] = dl
        rmax[bb, j, h] = jnp.max(dl, axis=-1, keepdims=True)
    m_prev, m_t, m_new, f_tot = {}, {}, {}, {}
    for bb in range(nbb):
        for h in range(ML_HEADS):
            m = m_ref[b0 + bb, h:h + 1, 0:1]
            for j in range(N_SUB):
                i = (bb, j, h)
                m_prev[i] = m
                m_t[i] = jnp.maximum(fc[i] + m, rmax[i])
                m = jnp.where(fwd, m_t[i][CHUNK - 1:CHUNK], m_t[i][0:1])
                m_new[i] = m
                f_tot[i] = jnp.where(fwd, fc[i][CHUNK - 1:CHUNK], fc[i][0:1])
            m_ref[b0 + bb, h:h + 1, :] = jnp.broadcast_to(m, (1, LANE))
    q, v, qk, w_prev, gdec, ds = {}, {}, {}, {}, {}, {}
    for bb, j, h in hinst:
        i = (bb, j, h)
        rs = pl.ds(r0s[j], CHUNK)
        sl = slice(h * LANE, (h + 1) * LANE)
        q[i] = qk_ref[bb, rs, sl].astype(BF16)
        k = qk_ref[bb, rs, HP + h * LANE:HP + (h + 1) * LANE]
        v[i] = jnp.where(lane == ML_DH, 1.0, v_ref[bb, rs, sl]).astype(BF16)
        w = jnp.exp(d_log[i] - m_t[i])
        w_prev[i] = jnp.exp(fc[i] + m_prev[i] - m_t[i])
        qk[i] = (_dot_nt(q[i], k.astype(BF16)) * w).astype(BF16)
        w_s = jnp.exp(f_tot[i] - fc[i] + lic[i] - m_new[i])
        gdec[i] = jnp.exp(f_tot[i] + m_prev[i] - m_new[i])
        ds[i] = _dot_tn(v[i], (k * w_s).astype(BF16))
    s_in = {}
    for bb in range(nbb):
        for h in range(ML_HEADS):
            st = st_ref[b0 + bb, h]
            for j in range(N_SUB):
                i = (bb, j, h)
                s_in[i] = st.astype(BF16)
                st = gdec[i] * st + ds[i]
            st_ref[b0 + bb, h] = st
    for bb, j, h in hinst:
        i = (bb, j, h)
        num = w_prev[i] * _dot_nt(q[i], s_in[i]) + _dot(qk[i], v[i])
        den = num[:, ML_DH:ML_DH + 1]
        hh = num / jnp.maximum(jnp.abs(den), jnp.exp(-m_t[i]))
        o_ref[0, bb, pl.ds(r0s[j], CHUNK), h * LANE:(h + 1) * LANE] = jnp.where(lane < ML_DH, hh, 0.0)


def _mlstm_old(qk3, pm3, gt4, gbr, gbc, *, rows):
    nb, l, _ = qk3.shape
    nbb = SCAN_BATCHES
    pos = functools.partial(_scan_pos, ncb=rows.ncb, ntb=rows.ntb)
    return pl.pallas_call(
        _mlstm_kernel,
        out_shape=jax.ShapeDtypeStruct((2, nb, l, HP), F32),
        grid=(2, rows.ntb, nb // nbb),
        in_specs=[pl.BlockSpec((nbb, SEG, 2 * HP), lambda d, s, g: (g, pos(d, s), 0)),
                  pl.BlockSpec((nbb, SEG, HP), lambda d, s, g: (g, pos(d, s), 2)),
                  pl.BlockSpec((nbb, SEG, LANE), lambda d, s, g: (g, pos(d, s), 4 * GLA_HEADS)),
                  pl.BlockSpec((nbb, N_SUB, 16, CHUNK), lambda d, s, g: (g, pos(d, s), 0, 0)),
                  pl.BlockSpec((1, LANE), lambda d, s, g: (0, 0)),
                  pl.BlockSpec((16, CHUNK), lambda d, s, g: (0, 0))],
        out_specs=pl.BlockSpec((1, nbb, SEG, HP), lambda d, s, g: (d, g, pos(d, s), 0)),
        scratch_shapes=[pltpu.VMEM((nb, ML_HEADS, LANE, LANE), F32), pltpu.VMEM((nb, 8, LANE), F32)],
        compiler_params=_cparams(("arbitrary", "arbitrary", "arbitrary")),
        name="mlstm_scan",
    )(qk3, pm3, pm3, gt4, gbr, gbc)


def _split_bf16(x, n):
    parts, r = [], x
    for _ in range(n):
        p = r.astype(BF16)
        parts.append(p)
        r = r - p.astype(F32)
    return parts


def _dot_exact01(a, b, lhs_is_01):
    if lhs_is_01:
        a = a.astype(BF16)
        terms = [_dot(a, p) for p in _split_bf16(b, 3)]
    else:
        b = b.astype(BF16)
        terms = [_dot(p, b) for p in _split_bf16(a, 3)]
    return terms[0] + terms[1] + terms[2]


def _cummax_rows(a, rev):
    n = a.shape[0]
    row = lax.broadcasted_iota(jnp.int32, a.shape, 0)
    k = 1
    while k < n:
        if rev:
            sh = jnp.where(row < n - k, pltpu.roll(a, n - k, axis=0), NEG)
        else:
            sh = jnp.where(row >= k, pltpu.roll(a, k, axis=0), NEG)
        a = jnp.maximum(a, sh)
        k *= 2
    return a


ML_GL = ML_HEADS


def _mlstm_kernel(rev, qk_ref, v_ref, g_ref, gt_ref, gbr_ref, gbt_ref, o_ref, st_ref, m_ref):
    s, g = pl.program_id(0), pl.program_id(1)
    nbb = qk_ref.shape[0]
    b0 = g * nbb

    @pl.when(s == 0)
    def _():
        st_ref[pl.ds(b0, nbb)] = jnp.zeros((nbb,) + st_ref.shape[1:], F32)
        m_ref[pl.ds(b0, nbb)] = jnp.zeros((nbb,) + m_ref.shape[1:], F32)

    valid = _tri(rev)
    tri = valid.astype(F32)
    r0s = _chunk_rows(rev)
    cs = [r // CHUNK for r in r0s]
    last = 0 if rev else CHUNK - 1
    inst = [(bb, j) for bb in range(nbb) for j in range(N_SUB)]
    heads = [slice(h * LANE, (h + 1) * LANE) for h in range(ML_HEADS)]

    r_sel = lax.broadcasted_iota(jnp.int32, (LANE, HP), 0)
    c_sel = lax.broadcasted_iota(jnp.int32, (LANE, HP), 1)
    sel_h = (r_sel == ML_GL + c_sel // LANE).astype(BF16)
    r_t = lax.broadcasted_iota(jnp.int32, (HP, HP), 0)
    c_t = lax.broadcasted_iota(jnp.int32, (HP, HP), 1)
    same = jnp.logical_and(r_t // LANE == c_t // LANE, jnp.logical_and(r_t % LANE < CHUNK, c_t % LANE < CHUNK))
    before = (r_t % LANE >= c_t % LANE) if rev else (r_t % LANE <= c_t % LANE)
    tri_b = jnp.logical_and(same, before).astype(BF16)
    r_v = lax.broadcasted_iota(jnp.int32, (CHUNK, HP), 0)
    c_v = lax.broadcasted_iota(jnp.int32, (CHUNK, HP), 1) % LANE
    valid4 = jnp.logical_and(c_v < CHUNK, (r_v <= c_v) if rev else (r_v >= c_v))
    lane4 = lax.broadcasted_iota(jnp.int32, (CHUNK, HP), 1) % LANE
    lane_c = lax.broadcasted_iota(jnp.int32, (CHUNK, LANE), 1)
    lane1 = lax.broadcasted_iota(jnp.int32, (1, LANE), 1)
    head_lane = jnp.logical_and(lane1 >= ML_GL, lane1 < ML_GL + ML_HEADS)
    gbr, gbt = gbr_ref[...], gbt_ref[...]

    gcs, fcm, cmx, a_row, grt = {}, {}, {}, {}, {}
    for bb, j in inst:
        gc = g_ref[bb, pl.ds(r0s[j], CHUNK), :] + gbr
        gcs[bb, j] = pltpu.roll(gc, ML_GL, axis=1)
        fcm[bb, j] = _dot_exact01(tri, _log_sigmoid(gc), lhs_is_01=True)
        grt[bb, j] = gt_ref[bb, cs[j]] + gbt
    row_id = lax.broadcasted_iota(jnp.int32, (len(inst), HP), 0)
    lfr = jnp.zeros((len(inst), HP), F32)
    for n, i in enumerate(inst):
        lfr = jnp.where(row_id == n, _log_sigmoid(grt[i][1:2]), lfr)
    fcr = _dot_exact01(lfr, tri_b, lhs_is_01=False)
    for n, i in enumerate(inst):
        a_row[i] = grt[i][0:1] - fcr[n:n + 1]
        cmx[i] = _cummax_rows(gcs[i] - fcm[i], rev)
    bx = {}
    e_neg, gd = {}, {}
    for bb in range(nbb):
        m_prev = m_ref[b0 + bb, 0:1, :]
        for j in range(N_SUB):
            i = (bb, j)
            m_t = fcm[i] + jnp.maximum(m_prev, cmx[i])
            m_new = m_t[last:last + 1]
            f_tot = fcm[i][last:last + 1]
            u = fcm[i] - m_t
            w_prev = jnp.exp(u + m_prev)
            w_s = jnp.exp(f_tot - fcm[i] + gcs[i] - m_new)
            gdec = jnp.broadcast_to(jnp.exp(f_tot + m_prev - m_new), (16, LANE))
            e_neg[i] = jnp.exp(-m_t)
            keep = lambda a: jnp.where(head_lane, a, 0.0)
            bx[i] = jnp.concatenate(_split_bf16(keep(u), 3) + _split_bf16(keep(w_prev), 2)
                                    + _split_bf16(keep(w_s), 2) + _split_bf16(keep(gdec), 2), axis=0)
            m_prev = m_new
        m_ref[b0 + bb] = jnp.broadcast_to(m_prev, (8, LANE))
    ub, wpb, wsb, gdb = {}, {}, {}, {}
    for i in inst:
        y = _dot(bx[i], sel_h)
        c = CHUNK
        ub[i] = y[0:c] + y[c:2 * c] + y[2 * c:3 * c]
        wpb[i] = y[3 * c:4 * c] + y[4 * c:5 * c]
        wsb[i] = y[5 * c:6 * c] + y[6 * c:7 * c]
        gdb[i] = y[7 * c:7 * c + 1] + y[7 * c + 16:7 * c + 17]
    q, v, qkw, ds = {}, {}, {}, {}
    for bb, j in inst:
        i = (bb, j)
        rs = pl.ds(r0s[j], CHUNK)
        w = jnp.where(valid4, jnp.exp(ub[i] + a_row[i]), 0.0)
        q[i] = qk_ref[bb, rs, 0:HP].astype(BF16)
        k = qk_ref[bb, rs, HP:2 * HP]
        kb = k.astype(BF16)
        kw = (k * wsb[i]).astype(BF16)
        v[i] = jnp.where(lane4 == ML_DH, 1.0, v_ref[bb, rs, :]).astype(BF16)
        for h, sl in enumerate(heads):
            sc = _dot_nt(q[i][:, sl], kb[:, sl])
            qkw[i, h] = (sc * w[:, h * LANE:h * LANE + CHUNK]).astype(BF16)
            ds[i, h] = _dot_tn(v[i][:, sl], kw[:, sl])
    s_in = {}
    for bb in range(nbb):
        for h, sl in enumerate(heads):
            st = st_ref[b0 + bb, h]
            for j in range(N_SUB):
                i = (bb, j)
                s_in[i, h] = st.astype(BF16)
                st = gdb[i][:, sl] * st + ds[i, h]
            st_ref[b0 + bb, h] = st
    num = {}
    for i in inst:
        parts = [_dot_nt(q[i][:, sl], s_in[i, h]) for h, sl in enumerate(heads)]
        intra = [_dot(qkw[i, h], v[i][:, sl]) for h, sl in enumerate(heads)]
        num[i] = wpb[i] * jnp.concatenate(parts, axis=1) + jnp.concatenate(intra, axis=1)
    for bb, j in inst:
        i = (bb, j)
        den = jnp.zeros((CHUNK, LANE), F32)
        for h, sl in enumerate(heads):
            dh = jnp.sum(jnp.where(lane_c == ML_DH, num[i][:, sl], 0.0), axis=-1, keepdims=True)
            den = jnp.where(lane_c == ML_GL + h, jnp.broadcast_to(dh, (CHUNK, LANE)), den)
        r1, r2 = _split_bf16(jnp.where(head_lane, 1.0 / jnp.maximum(jnp.abs(den), e_neg[i]), 0.0), 2)
        rb = _dot(r1, sel_h) + _dot(r2, sel_h)
        o_ref[bb, pl.ds(r0s[j], CHUNK), :] = jnp.where(lane4 < ML_DH, num[i] * rb, 0.0)


def _mlstm(qk3, pm3, gtl, gbr, gbt, *, rows, rev):
    nb, l, _ = qk3.shape
    nbb = SCAN_BATCHES
    dr = 1 if rev else 0
    pos = functools.partial(_scan_pos_static, rev, ncb=rows.ncb, ntb=rows.ntb)
    return pl.pallas_call(
        functools.partial(_mlstm_kernel, rev),
        out_shape=jax.ShapeDtypeStruct((nb, l, HP), F32),
        grid=(rows.ntb, nb // nbb),
        in_specs=[pl.BlockSpec((nbb, SEG, 2 * HP), lambda s, g: (g, pos(s), 0)),
                  pl.BlockSpec((nbb, SEG, HP), lambda s, g: (g, pos(s), 0)),
                  pl.BlockSpec((nbb, SEG, LANE), lambda s, g: (g, pos(s), 2 * ML_HEADS + dr)),
                  pl.BlockSpec((nbb, N_SUB, None, 2, HP), lambda s, g: (g, pos(s), dr, 0, 0)),
                  pl.BlockSpec((1, LANE), lambda s, g: (0, 0)),
                  pl.BlockSpec((2, HP), lambda s, g: (0, 0))],
        out_specs=pl.BlockSpec((nbb, SEG, HP), lambda s, g: (g, pos(s), 0)),
        scratch_shapes=[pltpu.VMEM((nb, ML_HEADS, LANE, LANE), F32), pltpu.VMEM((nb, 8, LANE), F32)],
        compiler_params=_cparams(("arbitrary", "arbitrary")),
        name="mlstm_scan_bwd" if rev else "mlstm_scan_fwd",
    )(qk3, pm3, pm3, gtl, gbr, gbt)


def _s5_kernel(nb, u_ref, bre_ref, bim_ref, are_ref, aim_ref, cre_ref, cim_ref, o_ref, xr_ref, xi_ref, st_ref):
    d, s = pl.program_id(0), pl.program_id(1)

    @pl.when(s == 0)
    def _():
        st_ref[...] = jnp.zeros(st_ref.shape, F32)

    u = u_ref[...].astype(BF16)
    xr_ref[...] = _dot(u, bre_ref[0])
    xi_ref[...] = _dot(u, bim_ref[0])
    ar, ai = are_ref[0], aim_ref[0]
    n_t = u_ref.shape[0] // nb

    def step(j, carry):
        sr, si = carry
        t = jnp.where(d == 0, j, n_t - 1 - j)
        r0 = pl.multiple_of(t * nb, nb)
        nr = ar * sr - ai * si + xr_ref[pl.ds(r0, nb), :]
        ni = ar * si + ai * sr + xi_ref[pl.ds(r0, nb), :]
        xr_ref[pl.ds(r0, nb), :] = nr
        xi_ref[pl.ds(r0, nb), :] = ni
        return nr, ni

    sr, si = lax.fori_loop(0, n_t, step, (st_ref[0], st_ref[1]), unroll=4)
    st_ref[0] = sr
    st_ref[1] = si
    o_ref[0] = _dot(xr_ref[...].astype(BF16), cre_ref[...]) - _dot(xi_ref[...].astype(BF16), cim_ref[...])


def _s5(ut, bre, bim, are, aim, cre, cim, *, nb, nc, nt):
    n_rows, ch = ut.shape
    tr = CHUNK * nb
    ns = bre.shape[-1]
    pos = functools.partial(_scan_pos, ncb=nc, ntb=nt)
    return pl.pallas_call(
        functools.partial(_s5_kernel, nb),
        out_shape=jax.ShapeDtypeStruct((2, n_rows, ch), F32),
        grid=(2, nt),
        in_specs=[pl.BlockSpec((tr, ch), lambda d, s: (pos(d, s), 0)),
                  pl.BlockSpec((1, ch, ns), lambda d, s: (d, 0, 0)),
                  pl.BlockSpec((1, ch, ns), lambda d, s: (d, 0, 0)),
                  pl.BlockSpec((1, nb, ns), lambda d, s: (d, 0, 0)),
                  pl.BlockSpec((1, nb, ns), lambda d, s: (d, 0, 0)),
                  pl.BlockSpec((ns, ch), lambda d, s: (0, 0)),
                  pl.BlockSpec((ns, ch), lambda d, s: (0, 0))],
        out_specs=pl.BlockSpec((1, tr, ch), lambda d, s: (d, pos(d, s), 0)),
        scratch_shapes=[pltpu.VMEM((tr, ns), F32), pltpu.VMEM((tr, ns), F32), pltpu.VMEM((2, nb, ns), F32)],
        compiler_params=_cparams(("arbitrary", "arbitrary")),
        name="s5_scan",
    )(ut, bre, bim, are, aim, cre, cim)


def _head_norm(o, gain, dim):
    parts = []
    for h in range(o.shape[1] // LANE):
        seg = o[:, h * LANE:(h + 1) * LANE]
        ms = jnp.sum(seg * seg, axis=-1, keepdims=True) * (1.0 / dim)
        parts.append(seg * lax.rsqrt(ms + EPS))
    return jnp.concatenate(parts, axis=1) * gain


def _mix_kernel(with_router, ogf_ref, ogb_ref, gg_ref, ys_ref, u_ref, omf_ref, omb_ref, mo_ref, h_ref, mod_ref,
                gn_ref, mn_ref, sd_ref, gw_ref, gb_ref, wo_ref, n2_ref, *rest):
    if with_router:
        wr_ref, ho_ref, f_ref, rt_ref = rest
    else:
        ho_ref, f_ref = rest
    gla = _head_norm(ogf_ref[...] + ogb_ref[...], gn_ref[...], GLA_DV) * _silu(gg_ref[...])
    z = _gelu_tanh(ys_ref[0] + ys_ref[1] + sd_ref[...] * u_ref[...])
    s5 = z * jax.nn.sigmoid(_dot(z.astype(BF16), gw_ref[...]) + gb_ref[...])
    ml = _head_norm(omf_ref[...] + omb_ref[...], mn_ref[...], ML_DH) * jax.nn.sigmoid(mo_ref[...])
    mix = (_dot(gla.astype(BF16), wo_ref[0:HP]) + _dot(s5.astype(BF16), wo_ref[HP:HP + 2 * LANE])
           + _dot(ml.astype(BF16), wo_ref[HP + 2 * LANE:]))
    m = mod_ref[0]
    hn = h_ref[...] + m[2:3] * mix
    ho_ref[...] = hn
    f = _rmsnorm(hn, n2_ref[...]) * (1.0 + m[4:5]) + m[3:4]
    f_ref[...] = f.astype(f_ref.dtype)
    if with_router:
        logits = _dot_nt(wr_ref[...], f, precision=HIGHEST)
        row = lax.broadcasted_iota(jnp.int32, logits.shape, 0)
        l0 = jnp.where(row < N_EXPERTS, logits, NEG)
        m1 = jnp.max(l0, axis=0, keepdims=True)
        i1 = jnp.min(jnp.where(l0 == m1, row, 2 * N_EXPERTS), axis=0, keepdims=True)
        l1 = jnp.where(row == i1, NEG, l0)
        m2 = jnp.max(l1, axis=0, keepdims=True)
        i2 = jnp.min(jnp.where(l1 == m2, row, 2 * N_EXPERTS), axis=0, keepdims=True)
        e = jnp.exp(m2 - m1)
        w1 = 1.0 / (1.0 + e)
        w2 = e / (1.0 + e)
        rt_ref[...] = jnp.where(row == 0, i1.astype(F32),
                                jnp.where(row == 1, i2.astype(F32),
                                          jnp.where(row == 2, w1, jnp.where(row == 3, w2, 0.0))))


def _mix(ogf, ogb, pg, ys, pu, omf, omb, pm, h, modtab, gn, mn, sd, gw, gb, wo, n2, wr, *, layer, rows, lat_only):
    d = h.shape[1]
    tm = SEG
    n = rows.n_blocks(lat_only)
    src, sel = rows.src(lat_only), rows.sel(lat_only)
    full = lambda a: pl.BlockSpec(a.shape, lambda i: (0,) * a.ndim)
    with_router = wr is not None
    in_specs = [pl.BlockSpec((tm, HP), lambda i: (src(i), 0)),
                pl.BlockSpec((tm, HP), lambda i: (src(i), 0)),
                pl.BlockSpec((tm, HP), lambda i: (src(i), 3)),
                pl.BlockSpec((2, tm, 2 * LANE), lambda i: (0, src(i), 0)),
                pl.BlockSpec((tm, 2 * LANE), lambda i: (src(i), 0)),
                pl.BlockSpec((tm, HP), lambda i: (src(i), 0)),
                pl.BlockSpec((tm, HP), lambda i: (src(i), 0)),
                pl.BlockSpec((tm, HP), lambda i: (src(i), 1)),
                pl.BlockSpec((tm, d), lambda i: (src(i), 0)),
                pl.BlockSpec((None, 1, 8, d), lambda i: (layer, sel(i), 0, 0)),
                full(gn), full(mn), full(sd), full(gw), full(gb), full(wo), full(n2)]
    args = [ogf, ogb, pg, ys, pu, omf, omb, pm, h, modtab, gn, mn, sd, gw, gb, wo, n2]
    out_shape = [jax.ShapeDtypeStruct((n * tm, d), F32), jax.ShapeDtypeStruct((n * tm, d), F32 if with_router else BF16)]
    out_specs = [pl.BlockSpec((tm, d), lambda i: (i, 0)), pl.BlockSpec((tm, d), lambda i: (i, 0))]
    if with_router:
        in_specs.append(full(wr))
        args.append(wr)
        out_shape.append(jax.ShapeDtypeStruct((2 * N_EXPERTS, n * tm), F32))
        out_specs.append(pl.BlockSpec((2 * N_EXPERTS, tm), lambda i: (0, i)))
    return pl.pallas_call(
        functools.partial(_mix_kernel, with_router),
        out_shape=tuple(out_shape),
        grid=(n,),
        in_specs=in_specs,
        out_specs=tuple(out_specs),
        compiler_params=_cparams(("arbitrary",)),
        name="mix_out",
    )(*args)


FF_TILE = 256


def _swiglu(xb, w1_ref, w3_ref, w2_ref, a_ref, lead=()):
    dff = w1_ref.shape[-1]
    for j in range(dff // FF_TILE):
        sl = slice(j * FF_TILE, (j + 1) * FF_TILE)
        h1 = _dot(xb, w1_ref[lead + (slice(None), sl)])
        h3 = _dot(xb, w3_ref[lead + (slice(None), sl)])
        a_ref[:, sl] = (_silu(h1) * h3).astype(BF16)
    return _dot(a_ref[...], w2_ref[lead + (slice(None), slice(None))])


FFN_TM = 1024


def _ffn_kernel(final, sel, f_ref, h_ref, mod_ref, w1_ref, w3_ref, w2_ref, *rest):
    if final:
        nf_ref, o_ref, a_ref = rest
    else:
        o_ref, a_ref = rest
    y = _swiglu(f_ref[...], w1_ref, w3_ref, w2_ref, a_ref)
    n_seg = f_ref.shape[0] // SEG
    for q in range(n_seg):
        rs = slice(q * SEG, (q + 1) * SEG)
        gate = mod_ref[sel(pl.program_id(0) * n_seg + q)][5:6]
        hn = h_ref[rs, :] + gate * y[rs]
        o_ref[rs, :] = _rmsnorm(hn, nf_ref[...]) if final else hn


def _ffn(f, h, modtab, w1, w3, w2, nf, *, layer, rows, lat_only):
    r, d = h.shape
    tm = FFN_TM
    assert r % tm == 0
    dff = w1.shape[-1]
    full = lambda a: pl.BlockSpec(a.shape, lambda i: (0,) * a.ndim)
    resident = lambda a: pl.BlockSpec(a.shape, lambda i: (0,) * a.ndim, pipeline_mode=pl.Buffered(1))
    in_specs = [pl.BlockSpec((tm, d), lambda i: (i, 0)),
                pl.BlockSpec((tm, d), lambda i: (i, 0)),
                pl.BlockSpec((None,) + modtab.shape[1:], lambda i: (layer, 0, 0, 0)),
                resident(w1), resident(w3), resident(w2)]
    args = [f, h, modtab, w1, w3, w2]
    if lat_only:
        in_specs.append(full(nf))
        args.append(nf)
    return pl.pallas_call(
        functools.partial(_ffn_kernel, lat_only, rows.sel(lat_only)),
        out_shape=jax.ShapeDtypeStruct((r, d), F32),
        grid=(r // tm,),
        in_specs=in_specs,
        out_specs=pl.BlockSpec((tm, d), lambda i: (i, 0)),
        scratch_shapes=[pltpu.VMEM((tm, dff), BF16)],
        compiler_params=_cparams(("arbitrary",)),
        name="ffn",
    )(*args)


MOE_TM = 512


def _moe_kernel(be_ref, live_ref, x_ref, rw_ref, w1_ref, w3_ref, w2_ref, o_ref, a_ref):
    i = pl.program_id(0)

    @pl.when(live_ref[i] > 0)
    def _():
        y = _swiglu(x_ref[...].astype(BF16), w1_ref, w3_ref, w2_ref, a_ref, lead=(0,))
        o_ref[...] = y * rw_ref[:, 0:1]

    @pl.when(live_ref[i] == 0)
    def _():
        o_ref[...] = jnp.zeros(o_ref.shape, F32)


def _moe_experts(block_expert, block_live, xg, rw, w1, w3, w2):
    n_rows, d = xg.shape
    dff = w1.shape[-1]
    tm = MOE_TM
    return pl.pallas_call(
        _moe_kernel,
        out_shape=jax.ShapeDtypeStruct((n_rows, d), F32),
        grid_spec=pltpu.PrefetchScalarGridSpec(
            num_scalar_prefetch=2,
            grid=(n_rows // tm,),
            in_specs=[pl.BlockSpec((tm, d), lambda i, be, lv: (i, 0)),
                      pl.BlockSpec((tm, LANE), lambda i, be, lv: (i, 0)),
                      pl.BlockSpec((1, d, dff), lambda i, be, lv: (be[i], 0, 0)),
                      pl.BlockSpec((1, d, dff), lambda i, be, lv: (be[i], 0, 0)),
                      pl.BlockSpec((1, dff, d), lambda i, be, lv: (be[i], 0, 0))],
            out_specs=pl.BlockSpec((tm, d), lambda i, be, lv: (i, 0)),
            scratch_shapes=[pltpu.VMEM((tm, dff), BF16)]),
        compiler_params=_cparams(("arbitrary",)),
        name="moe_experts",
    )(block_expert, block_live, xg, rw, w1, w3, w2)


SC_GATHER_ROWS = 64


def _gather_rows(table, idx):
    n_idx = idx.shape[0]
    _, d = table.shape
    info = plsc.get_sparse_core_info()
    n_cores, n_workers = info.num_cores, info.num_cores * info.num_subcores
    assert n_idx % (n_workers * SC_GATHER_ROWS) == 0
    per_worker = n_idx // n_workers
    mesh = plsc.VectorSubcoreMesh(core_axis_name="c", subcore_axis_name="s")

    @functools.partial(
        pl.kernel, mesh=mesh,
        out_type=jax.ShapeDtypeStruct((n_idx, d), table.dtype),
        scratch_types=[pltpu.VMEM((SC_GATHER_ROWS,), jnp.int32),
                       pltpu.VMEM((SC_GATHER_ROWS, d), table.dtype),
                       pltpu.SemaphoreType.DMA])
    def gather(table_hbm, idx_hbm, out_hbm, idx_v, rows_v, sem):
        base = (lax.axis_index("s") * n_cores + lax.axis_index("c")) * per_worker

        @pl.loop(0, per_worker // SC_GATHER_ROWS)
        def _(it):
            off = pl.multiple_of(base + it * SC_GATHER_ROWS, SC_GATHER_ROWS)
            pltpu.sync_copy(idx_hbm.at[pl.ds(off, SC_GATHER_ROWS)], idx_v)
            pltpu.async_copy(table_hbm.at[idx_v], rows_v, sem).wait()
            pltpu.sync_copy(rows_v, out_hbm.at[pl.ds(off, SC_GATHER_ROWS)])

    return gather(table, idx)


def _resid_kernel(final, h_ref, y0_ref, y1_ref, mod_ref, *rest):
    hn = h_ref[...] + mod_ref[0][5:6] * (y0_ref[...] + y1_ref[...])
    if final:
        nf_ref, o_ref = rest
        o_ref[...] = _rmsnorm(hn, nf_ref[...])
    else:
        (o_ref,) = rest
        o_ref[...] = hn


def _moe_resid(h, y0, y1, modtab, nf, *, layer, rows, lat_only):
    r, d = h.shape
    tm = SEG
    sel = rows.sel(lat_only)
    row = lambda i: (i, 0)
    in_specs = [pl.BlockSpec((tm, d), row), pl.BlockSpec((tm, d), row), pl.BlockSpec((tm, d), row),
                pl.BlockSpec((None, 1, 8, d), lambda i: (layer, sel(i), 0, 0))]
    args = [h, y0, y1, modtab]
    if lat_only:
        in_specs.append(pl.BlockSpec((1, d), lambda i: (0, 0)))
        args.append(nf)
    return pl.pallas_call(
        functools.partial(_resid_kernel, lat_only),
        out_shape=jax.ShapeDtypeStruct((r, d), F32),
        grid=(r // tm,),
        in_specs=in_specs,
        out_specs=pl.BlockSpec((tm, d), row),
        compiler_params=_cparams(("arbitrary",)),
        name="moe_resid",
    )(*args)


def _pad_heads(w, heads, dim):
    lead = w.shape[:-1]
    w = w.reshape(lead + (heads, dim))
    w = jnp.pad(w, [(0, 0)] * len(lead) + [(0, 0), (0, LANE - dim)])
    return w.reshape(lead + (heads * LANE,))


def _pad_last(w, to):
    return jnp.pad(w, [(0, 0)] * (w.ndim - 1) + [(0, to - w.shape[-1])])


def _pos_embed(n_tokens, d):
    n_grid_rows = n_tokens // GRID_W
    row, col = jnp.meshgrid(jnp.arange(n_grid_rows, dtype=F32), jnp.arange(GRID_W, dtype=F32), indexing='ij')
    n_freq = d // 4
    omega = jnp.exp(-math.log(POS_BASE) * jnp.arange(n_freq, dtype=F32) / n_freq)

    def axis_embed(p):
        ang = p.reshape(-1, 1) * omega
        return jnp.concatenate([jnp.sin(ang), jnp.cos(ang)], axis=-1)

    return jnp.concatenate([axis_embed(row), axis_embed(col)], axis=-1)


def _s5_discretise(lam_re, lam_im, log_dt, b_re, b_im):
    dt = jnp.exp(log_dt)[:, None]
    mag = jnp.exp(lam_re * dt)
    abar_re, abar_im = mag * jnp.cos(lam_im * dt), mag * jnp.sin(lam_im * dt)
    den = lam_re * lam_re + lam_im * lam_im
    pr, pi = abar_re - 1.0, abar_im
    coef_re = (pr * lam_re + pi * lam_im) / den
    coef_im = (pi * lam_re - pr * lam_im) / den
    bbar_re = coef_re[..., None] * b_re - coef_im[..., None] * b_im
    bbar_im = coef_re[..., None] * b_im + coef_im[..., None] * b_re
    return abar_re, abar_im, bbar_re, bbar_im


def _block_diag(m):
    g, a, b = m.shape
    eye = jnp.eye(g, dtype=m.dtype)
    return (eye[:, None, :, None] * m[:, :, None, :]).reshape(g * a, g * b)


def _route_plan(route, tm):
    n_tok = route.shape[1]
    n_assign = n_tok * TOP_K
    flat_e = route[0:TOP_K].astype(jnp.int32).reshape(-1)
    flat_w = route[TOP_K:2 * TOP_K].reshape(-1)
    onehot = (jnp.arange(N_EXPERTS, dtype=jnp.int32)[:, None] == flat_e[None, :]).astype(jnp.int32)
    csum = jnp.cumsum(onehot, axis=1)
    counts = csum[:, -1]
    padded = (counts + tm - 1) // tm * tm
    pend = jnp.cumsum(padded)
    pstart = pend - padded
    dest = jnp.sum(onehot * (csum - 1 + pstart[:, None]), axis=0)
    n_blocks = -(-n_assign // tm) + N_EXPERTS
    n_rows = n_blocks * tm
    row_assign = jnp.full((n_rows,), -1, jnp.int32).at[dest].set(jnp.arange(n_assign, dtype=jnp.int32))
    live = row_assign >= 0
    ra = jnp.maximum(row_assign, 0)
    row_token = ra % n_tok
    row_w = jnp.where(live, flat_w[ra], 0.0)
    block_start = jnp.arange(n_blocks, dtype=jnp.int32) * tm
    block_expert = jnp.minimum(jnp.searchsorted(pend, block_start, side='right'), N_EXPERTS - 1).astype(jnp.int32)
    block_live = (block_start < (pstart + counts)[block_expert]).astype(jnp.int32)
    return row_token, row_w, block_expert, block_live, dest.reshape(TOP_K, n_tok)


def kernel(x, c, ctx, c_ctx, w_ada, b_ada, norm1, norm2, w_in, w_out, gla_wa2, gla_ba, gla_norm, s5_lam_re, s5_lam_im, s5_log_dt, s5_b_re, s5_b_im, s5_c_re, s5_c_im, s5_d, s5_glu_w, s5_glu_b, ml_conv_w, ml_conv_b, ml_gate_b, ml_norm, ffn_w1, ffn_w3, ffn_w2, moe_router, moe_w1, moe_w3, moe_w2, norm_f):
    nb, n_lat, d = x.shape
    lc = ctx.shape[1]
    depth = w_ada.shape[0]
    l = lc + n_lat
    assert lc % SEG == 0 and n_lat % SEG == 0 and nb == 8 and nb % SCAN_BATCHES == 0
    rows = _Rows(nb, lc // SEG, l // SEG)

    h = _embed(ctx.reshape(nb * lc, d), x.reshape(nb * n_lat, d), _pos_embed(n_lat, d), rows)

    cond = jnp.zeros((16, d), F32).at[:nb].set(c).at[nb].set(c_ctx)
    mod = _modulation(cond, w_ada, b_ada)
    modtab = jnp.pad(mod.reshape(depth, 16, 6, d), ((0, 0), (0, 0), (0, 2), (0, 0)))

    dk, dv, dh = GLA_HEADS * GLA_DK, GLA_HEADS * GLA_DV, ML_HEADS * ML_DH
    s5c = s5_d.shape[-1]
    cuts = np.cumsum([dk, dk, dv, GLA_RANK, dv, s5c, dh, dh, dh, dh, 4 * ML_HEADS])

    for i in range(depth):
        last = i == depth - 1
        gq, gk, gv, glr, gg, su, mq, mk, mv, mo, mg = jnp.split(w_in[i], cuts[:-1], axis=-1)
        w_all = jnp.concatenate([
            _pad_heads(gq, GLA_HEADS, GLA_DK), _pad_heads(gk, GLA_HEADS, GLA_DK),
            _pad_heads(gv, GLA_HEADS, GLA_DV), _pad_heads(gg, GLA_HEADS, GLA_DV), _pad_last(glr, LANE),
            su,
            _pad_heads(mv, ML_HEADS, ML_DH), _pad_heads(mo, ML_HEADS, ML_DH),
            _pad_last(mg[:, :2 * ML_HEADS], LANE), _pad_last(mg[:, 2 * ML_HEADS:], LANE),
            _pad_heads(mq, ML_HEADS, ML_DH), _pad_heads(mk, ML_HEADS, ML_DH)],
            axis=-1).astype(BF16)
        wgt = mg.T.astype(BF16)
        cw = jnp.concatenate([_pad_heads(ml_conv_w[i][:, :dh], ML_HEADS, ML_DH),
                              _pad_heads(ml_conv_w[i][:, dh:], ML_HEADS, ML_DH)], axis=-1)
        cw = jnp.pad(cw, ((0, 8 - ML_CONV), (0, 0)))
        cb = jnp.concatenate([_pad_heads(ml_conv_b[i][:dh], ML_HEADS, ML_DH),
                              _pad_heads(ml_conv_b[i][dh:], ML_HEADS, ML_DH)])[None]
        post = jnp.concatenate([jnp.ones((HP,), F32), jnp.full((HP,), ML_DH ** -0.5, F32)])[None]
        pg, pu, pm, qk, gt = _proj(h, modtab, norm1[i][None], w_all, wgt, cw, cb, post, layer=i, rows=rows)

        wa = jnp.pad(_pad_heads(gla_wa2[i], GLA_HEADS, GLA_DK), ((0, 0), (0, LANE - GLA_RANK), (0, 0)))
        ba = _pad_heads(gla_ba[i], GLA_HEADS, GLA_DK)[:, None, :]
        pg3 = pg.reshape(nb, l, NG)
        ogf = _gla(pg3, wa[0], ba[0], rows=rows, rev=False).reshape(nb * l, HP)
        ogb = _gla(pg3, wa[1], ba[1], rows=rows, rev=True).reshape(nb * l, HP)

        bres, bims, ares, aims = [], [], [], []
        for dr in (0, 1):
            a_re, a_im, b_re, b_im = _s5_discretise(s5_lam_re[i, dr], s5_lam_im[i, dr], s5_log_dt[i, dr],
                                                    s5_b_re[i], s5_b_im[i])
            bres.append(_block_diag(jnp.swapaxes(b_re, 1, 2)))
            bims.append(_block_diag(jnp.swapaxes(b_im, 1, 2)))
            ares.append(jnp.broadcast_to(a_re.reshape(1, -1), (nb, a_re.size)))
            aims.append(jnp.broadcast_to(a_im.reshape(1, -1), (nb, a_im.size)))
        cre = _block_diag(jnp.swapaxes(s5_c_re[i], 1, 2)).astype(BF16)
        cim = _block_diag(jnp.swapaxes(s5_c_im[i], 1, 2)).astype(BF16)
        ut = pu.reshape(nb, l, s5c).swapaxes(0, 1).reshape(l * nb, s5c)
        yt = _s5(ut, jnp.stack(bres).astype(BF16), jnp.stack(bims).astype(BF16), jnp.stack(ares), jnp.stack(aims),
                 cre, cim, nb=nb, nc=lc // CHUNK, nt=l // CHUNK)
        ys = yt.reshape(2, l, nb, s5c).swapaxes(1, 2).reshape(2, nb * l, s5c)

        gb = ml_gate_b[i].reshape(2, 2, ML_HEADS)
        gbr = _pad_last(gb.reshape(2, 1, 2 * ML_HEADS), LANE)
        gbt = _pad_last(jnp.broadcast_to(gb[..., None], (2, 2, ML_HEADS, CHUNK)), LANE).reshape(2, 2, HP)
        gtl = _pad_last(gt.reshape(2, 2, ML_HEADS, nb, l // CHUNK, CHUNK).transpose(3, 4, 0, 1, 2, 5),
                        LANE).reshape(nb, l // CHUNK, 2, 2, HP)
        qk3, pm3 = qk.reshape(nb, l, 2 * HP), pm.reshape(nb, l, NM)
        omf = _mlstm(qk3, pm3, gtl, gbr[0], gbt[0], rows=rows, rev=False).reshape(nb * l, HP)
        omb = _mlstm(qk3, pm3, gtl, gbr[1], gbt[1], rows=rows, rev=True).reshape(nb * l, HP)

        wo = w_out[i]
        wo_p = jnp.concatenate([
            jnp.pad(wo[:dv].reshape(GLA_HEADS, GLA_DV, d), ((0, 0), (0, LANE - GLA_DV), (0, 0))).reshape(HP, d),
            wo[dv:dv + s5c],
            jnp.pad(wo[dv + s5c:].reshape(ML_HEADS, ML_DH, d), ((0, 0), (0, LANE - ML_DH), (0, 0))).reshape(HP, d)],
            axis=0).astype(BF16)
        gn = jnp.tile(_pad_last(gla_norm[i], LANE), GLA_HEADS)[None]
        mn = jnp.tile(_pad_last(ml_norm[i], LANE), ML_HEADS)[None]
        is_moe = i % 2 == 1
        j = i // 2
        wr = jnp.pad(moe_router[j].T, ((0, N_EXPERTS), (0, 0))) if is_moe else None
        outs = _mix(ogf, ogb, pg, ys, pu, omf, omb, pm, h, modtab, gn, mn, s5_d[i][None], s5_glu_w[i].astype(BF16),
                    s5_glu_b[i][None], wo_p, norm2[i][None], wr, layer=i, rows=rows, lat_only=last)
        if not is_moe:
            h, f = outs
            h = _ffn(f, h, modtab, ffn_w1[j].astype(BF16), ffn_w3[j].astype(BF16), ffn_w2[j].astype(BF16),
                     norm_f[None], layer=i, rows=rows, lat_only=last)
        else:
            h, f, route = outs
            row_token, row_w, block_expert, block_live, dest = _route_plan(route, MOE_TM)
            xg = _gather_rows(f, row_token)
            rw = jnp.broadcast_to(row_w[:, None], (row_w.shape[0], LANE))
            yg = _moe_experts(block_expert, block_live, xg, rw, moe_w1[j].astype(BF16), moe_w3[j].astype(BF16),
                              moe_w2[j].astype(BF16))
            y0 = _gather_rows(yg, dest[0])
            y1 = _gather_rows(yg, dest[1])
            h = _moe_resid(h, y0, y1, modtab, norm_f[None], layer=i, rows=rows, lat_only=last)
    return h.reshape(nb, n_lat, d)
```

```python
import functools
import math

import numpy as np
import jax
import jax.numpy as jnp
from jax import lax
from jax.experimental import pallas as pl
from jax.experimental.pallas import tpu as pltpu
from jax.experimental.pallas import tpu_sc as plsc

F32 = jnp.float32
BF16 = jnp.bfloat16
HIGHEST = lax.Precision.HIGHEST

GRID_W = 64
POS_BASE = 10000.0
EPS = 1e-6
GLA_HEADS, GLA_DK, GLA_DV, GLA_RANK, GLA_GATE_NORM = 4, 48, 96, 16, 16.0
S5_GROUP, S5_STATE = 16, 64
ML_HEADS, ML_DH, ML_CONV = 4, 96, 3
N_EXPERTS, TOP_K = 8, 2

LANE = 128
CHUNK = 64
SEG = 256
N_SUB = SEG // CHUNK
SCAN_BATCHES = 4
S5_STEPS = 128
NEG = -1e30
VMEM_LIMIT = 56 * 1024 * 1024

HP = LANE * GLA_HEADS
GLA_KP = 64
GQ = GLA_HEADS * GLA_KP
NG = 2 * GQ + 2 * HP + LANE
NM = 2 * HP + 2 * LANE


def _cparams(sem):
    return pltpu.CompilerParams(dimension_semantics=sem, vmem_limit_bytes=VMEM_LIMIT)


def _dot(a, b, **kw):
    return jnp.dot(a, b, preferred_element_type=F32, **kw)


def _dot_nt(a, b, **kw):
    return lax.dot_general(a, b, (((1,), (1,)), ((), ())), preferred_element_type=F32, **kw)


def _dot_tn(a, b, **kw):
    return lax.dot_general(a, b, (((0,), (0,)), ((), ())), preferred_element_type=F32, **kw)


def _log_sigmoid(x):
    return jnp.minimum(x, 0.0) - jnp.log1p(jnp.exp(-jnp.abs(x)))


def _silu(x):
    return x * jax.nn.sigmoid(x)


def _gelu_tanh(x):
    return 0.5 * x * (1.0 + jnp.tanh(math.sqrt(2.0 / math.pi) * (x + 0.044715 * (x * x * x))))


def _rmsnorm(x, g):
    return x * lax.rsqrt(jnp.mean(x * x, axis=-1, keepdims=True) + EPS) * g


class _Rows:
    def __init__(self, nb, ncb, ntb):
        self.nb, self.ncb, self.ntb, self.nlb = nb, ncb, ntb, ntb - ncb

    def n_blocks(self, lat_only):
        return self.nb * (self.nlb if lat_only else self.ntb)

    def src(self, lat_only):
        if lat_only:
            return lambda i: (i // self.nlb) * self.ntb + self.ncb + i % self.nlb
        return lambda i: i

    def sel(self, lat_only):
        if lat_only:
            return lambda i: i // self.nlb
        return lambda i: jnp.where(i % self.ntb < self.ncb, self.nb, i // self.ntb)


def _embed_kernel(ncb, ntb, ctx_ref, x_ref, pos_ref, o_ref):
    j = pl.program_id(0) % ntb

    @pl.when(j < ncb)
    def _():
        o_ref[...] = ctx_ref[...]

    @pl.when(j >= ncb)
    def _():
        o_ref[...] = x_ref[...] + pos_ref[...]


def _embed(ctx2, x2, pos, rows):
    d = ctx2.shape[1]
    ncb, ntb, nlb = rows.ncb, rows.ntb, rows.nlb
    return pl.pallas_call(
        functools.partial(_embed_kernel, ncb, ntb),
        out_shape=jax.ShapeDtypeStruct((rows.nb * ntb * SEG, d), F32),
        grid=(rows.nb * ntb,),
        in_specs=[pl.BlockSpec((SEG, d), lambda i: ((i // ntb) * ncb + jnp.minimum(i % ntb, ncb - 1), 0)),
                  pl.BlockSpec((SEG, d), lambda i: ((i // ntb) * nlb + jnp.maximum(i % ntb - ncb, 0), 0)),
                  pl.BlockSpec((SEG, d), lambda i: (jnp.maximum(i % ntb - ncb, 0), 0))],
        out_specs=pl.BlockSpec((SEG, d), lambda i: (i, 0)),
        compiler_params=_cparams(("arbitrary",)),
        name="embed",
    )(ctx2, x2, pos)


def _mod_kernel(c_ref, w_ref, b_ref, o_ref):
    s = _silu(c_ref[...])
    o_ref[0] = _dot(s, w_ref[0], precision=HIGHEST) + b_ref[0]


def _modulation(cond, w_ada, b_ada):
    depth, d, n6 = w_ada.shape
    tn = n6 // 4
    n_rows = cond.shape[0]
    return pl.pallas_call(
        _mod_kernel,
        out_shape=jax.ShapeDtypeStruct((depth, n_rows, n6), F32),
        grid=(depth, n6 // tn),
        in_specs=[pl.BlockSpec((n_rows, d), lambda l, j: (0, 0)),
                  pl.BlockSpec((1, d, tn), lambda l, j: (l, 0, j)),
                  pl.BlockSpec((1, 1, tn), lambda l, j: (l, 0, j))],
        out_specs=pl.BlockSpec((1, n_rows, tn), lambda l, j: (l, 0, j)),
        compiler_params=_cparams(("arbitrary", "arbitrary")),
        name="modulation",
    )(cond, w_ada, b_ada.reshape(depth, 1, n6))


def _proj_kernel(ncb, ntb, h_ref, hp_ref, hn_ref, mod_ref, g_ref, w_ref, wgt_ref, cw_ref, cb_ref, post_ref,
                 pg_ref, pu_ref, pm_ref, qk_ref, gt_ref):
    j = pl.program_id(0) % ntb
    m = mod_ref[0]
    act = lambda x: (_rmsnorm(x, g_ref[...]) * (1.0 + m[1:2]) + m[0:1]).astype(BF16)
    ab = act(h_ref[...])
    c1, c2 = NG + 2 * LANE, NG + 2 * LANE + NM
    pg_ref[...] = _dot(ab, w_ref[:, 0:NG])
    pu_ref[...] = _dot(ab, w_ref[:, NG:c1])
    pm_ref[...] = _dot(ab, w_ref[:, c1:c2])
    gt_ref[...] = _dot_nt(wgt_ref[...], ab)
    x = _dot(ab, w_ref[:, c2:])
    halo = _dot(act(jnp.concatenate([hp_ref[...], hn_ref[...]], axis=0)), w_ref[:, c2:])
    tm = x.shape[0]
    first = jnp.logical_or(j == 0, j == ncb).astype(F32)
    last = jnp.logical_or(j == ncb - 1, j == ntb - 1).astype(F32)
    row = lax.broadcasted_iota(jnp.int32, x.shape, 0)
    xp = jnp.where(row == 0, halo[7:8, :] * (1.0 - first), pltpu.roll(x, 1, axis=0))
    xn = jnp.where(row == tm - 1, halo[8:9, :] * (1.0 - last), pltpu.roll(x, tm - 1, axis=0))
    y = cw_ref[0:1] * xp + cw_ref[1:2] * x + cw_ref[2:3] * xn + cb_ref[...]
    qk_ref[...] = _silu(y) * post_ref[...]


def _proj(h, modtab, g, w, wgt, cw, cb, post, *, layer, rows):
    r, d = h.shape
    tm = SEG
    t8 = tm // 8
    sel = rows.sel(False)
    full = lambda a: pl.BlockSpec(a.shape, lambda i: (0,) * a.ndim)
    return pl.pallas_call(
        functools.partial(_proj_kernel, rows.ncb, rows.ntb),
        out_shape=(jax.ShapeDtypeStruct((r, NG), F32), jax.ShapeDtypeStruct((r, 2 * LANE), F32),
                   jax.ShapeDtypeStruct((r, NM), F32), jax.ShapeDtypeStruct((r, 2 * HP), F32),
                   jax.ShapeDtypeStruct((16, r), F32)),
        grid=(r // tm,),
        in_specs=[pl.BlockSpec((tm, d), lambda i: (i, 0)),
                  pl.BlockSpec((8, d), lambda i: (jnp.maximum(i * t8 - 1, 0), 0)),
                  pl.BlockSpec((8, d), lambda i: (jnp.minimum((i + 1) * t8, r // 8 - 1), 0)),
                  pl.BlockSpec((None, 1, 8, d), lambda i: (layer, sel(i), 0, 0)),
                  full(g), full(w), full(wgt), full(cw), full(cb), full(post)],
        out_specs=(pl.BlockSpec((tm, NG), lambda i: (i, 0)),
                   pl.BlockSpec((tm, 2 * LANE), lambda i: (i, 0)),
                   pl.BlockSpec((tm, NM), lambda i: (i, 0)),
                   pl.BlockSpec((tm, 2 * HP), lambda i: (i, 0)),
                   pl.BlockSpec((16, tm), lambda i: (0, i))),
        compiler_params=_cparams(("arbitrary",)),
        name="proj",
    )(h, h, h, modtab, g, w, wgt, cw, cb, post)


def _conv_kernel(ncb, ntb, x_ref, prev_ref, next_ref, w_ref, b_ref, post_ref, o_ref):
    j = pl.program_id(0) % ntb
    x = x_ref[...]
    tm = x.shape[0]
    first = jnp.logical_or(j == 0, j == ncb).astype(F32)
    last = jnp.logical_or(j == ncb - 1, j == ntb - 1).astype(F32)
    row = lax.broadcasted_iota(jnp.int32, x.shape, 0)
    xp = jnp.where(row == 0, prev_ref[7:8, :] * (1.0 - first), pltpu.roll(x, 1, axis=0))
    xn = jnp.where(row == tm - 1, next_ref[0:1, :] * (1.0 - last), pltpu.roll(x, tm - 1, axis=0))
    y = w_ref[0:1] * xp + w_ref[1:2] * x + w_ref[2:3] * xn + b_ref[...]
    o_ref[...] = _silu(y) * post_ref[...]


def _conv(pm, w, b, post, *, rows):
    r = pm.shape[0]
    tm = SEG
    wq = 2 * HP
    t8 = tm // 8
    return pl.pallas_call(
        functools.partial(_conv_kernel, rows.ncb, rows.ntb),
        out_shape=jax.ShapeDtypeStruct((r, wq), F32),
        grid=(r // tm,),
        in_specs=[pl.BlockSpec((tm, wq), lambda i: (i, 0)),
                  pl.BlockSpec((8, wq), lambda i: (jnp.maximum(i * t8 - 1, 0), 0)),
                  pl.BlockSpec((8, wq), lambda i: (jnp.minimum((i + 1) * t8, r // 8 - 1), 0)),
                  pl.BlockSpec((8, wq), lambda i: (0, 0)),
                  pl.BlockSpec((1, wq), lambda i: (0, 0)),
                  pl.BlockSpec((1, wq), lambda i: (0, 0))],
        out_specs=pl.BlockSpec((tm, wq), lambda i: (i, 0)),
        compiler_params=_cparams(("arbitrary",)),
        name="conv",
    )(pm, pm, pm, w, b, post)


def _scan_pos(d, s, ncb, ntb):
    rev = jnp.where(s < ncb, ncb - 1 - s, ntb - 1 - (s - ncb))
    return jnp.where(d == 0, s, rev)


def _scan_pos_static(rev, s, ncb, ntb):
    if not rev:
        return s
    return jnp.where(s < ncb, ncb - 1 - s, ntb - 1 - (s - ncb))


def _tri(rev):
    r = lax.broadcasted_iota(jnp.int32, (CHUNK, CHUNK), 0)
    c = lax.broadcasted_iota(jnp.int32, (CHUNK, CHUNK), 1)
    return (r <= c) if rev else (r >= c)


def _chunk_rows(rev):
    return [(N_SUB - 1 - j if rev else j) * CHUNK for j in range(N_SUB)]


def _gla_kernel(rev, p_ref, wa_ref, ba_ref, o_ref, st_ref):
    s, g = pl.program_id(0), pl.program_id(1)
    nbb = p_ref.shape[0]
    b0 = g * nbb

    @pl.when(s == 0)
    def _():
        st_ref[pl.ds(b0, nbb)] = jnp.zeros((nbb,) + st_ref.shape[1:], F32)

    valid = _tri(rev)
    tri = valid.astype(F32)
    r0s = _chunk_rows(rev)
    wa, ba = wa_ref[...], ba_ref[...]
    inst = [(bb, j) for bb in range(nbb) for j in range(N_SUB)]
    heads = [slice(h * LANE, (h + 1) * LANE) for h in range(GLA_HEADS)]
    pairs = [slice((h // 2) * LANE, (h // 2 + 1) * LANE) for h in range(GLA_HEADS)]
    lane = lax.broadcasted_iota(jnp.int32, (CHUNK, LANE), 1)
    own = [(lane // GLA_KP) == (h % 2) for h in range(GLA_HEADS)]

    la = {}
    for bb, j in inst:
        lr = p_ref[bb, pl.ds(r0s[j], CHUNK), 2 * GQ + 2 * HP:NG]
        la[bb, j] = _log_sigmoid(_dot(lr, wa) + ba) * (1.0 / GLA_GATE_NORM)
    bc, e_last = {}, {}
    for i in inst:
        bc[i] = _dot_exact01(tri, la[i], lhs_is_01=True, pieces=2)
        e_last[i] = jnp.exp(jnp.sum(la[i], axis=0, keepdims=True))
    q_in, k_in, k_out, v = {}, {}, {}, {}
    for bb, j in inst:
        i = (bb, j)
        rs = pl.ds(r0s[j], CHUNK)
        qs = (p_ref[bb, rs, 0:GQ] * (GLA_DK ** -0.5) * jnp.exp(bc[i])).astype(BF16)
        for h in range(GLA_HEADS):
            q_in[i, h] = jnp.where(own[h], qs[:, pairs[h]], jnp.zeros_like(qs[:, pairs[h]]))
        kd = p_ref[bb, rs, GQ:2 * GQ] * jnp.exp(-bc[i])
        k_out[i] = (kd * e_last[i]).astype(BF16)
        k_in[i] = kd.astype(BF16)
        v[i] = p_ref[bb, rs, 2 * GQ:2 * GQ + HP].astype(BF16)
    att = {}
    for i in inst:
        for h in range(GLA_HEADS):
            att[i, h] = jnp.where(valid, _dot_nt(q_in[i, h], k_in[i][:, pairs[h]]), 0.0).astype(BF16)
    o_intra, ds = {}, {}
    for i in inst:
        for h, sl in enumerate(heads):
            o_intra[i, h] = _dot(att[i, h], v[i][:, sl])
            ds[i, h] = _dot_tn(v[i][:, sl], k_out[i][:, pairs[h]])
    s_in = {}
    for bb in range(nbb):
        for h in range(GLA_HEADS):
            st = st_ref[b0 + bb, h]
            for j in range(N_SUB):
                s_in[(bb, j), h] = st.astype(BF16)
                st = st * e_last[bb, j][:, pairs[h]] + ds[(bb, j), h]
            st_ref[b0 + bb, h] = st
    for bb, j in inst:
        for h, sl in enumerate(heads):
            o = o_intra[(bb, j), h] + _dot_nt(q_in[(bb, j), h], s_in[(bb, j), h])
            o_ref[bb, pl.ds(r0s[j], CHUNK), sl] = o


def _gla(pg3, wa, ba, *, rows, rev):
    nb, l, _ = pg3.shape
    nbb = SCAN_BATCHES
    pos = functools.partial(_scan_pos_static, rev, ncb=rows.ncb, ntb=rows.ntb)
    return pl.pallas_call(
        functools.partial(_gla_kernel, rev),
        out_shape=jax.ShapeDtypeStruct((nb, l, HP), F32),
        grid=(rows.ntb, nb // nbb),
        in_specs=[pl.BlockSpec((nbb, SEG, NG), lambda s, g: (g, pos(s), 0)),
                  pl.BlockSpec((LANE, GQ), lambda s, g: (0, 0)),
                  pl.BlockSpec((1, GQ), lambda s, g: (0, 0))],
        out_specs=pl.BlockSpec((nbb, SEG, HP), lambda s, g: (g, pos(s), 0)),
        scratch_shapes=[pltpu.VMEM((nb, GLA_HEADS, LANE, LANE), F32)],
        compiler_params=_cparams(("arbitrary", "arbitrary")),
        name="gla_scan_bwd" if rev else "gla_scan_fwd",
    )(pg3, wa, ba)


def _mlstm_kernel_old(qk_ref, v_ref, g_ref, gt_ref, gbr_ref, gbc_ref, o_ref, st_ref, m_ref):
    d, s, g = pl.program_id(0), pl.program_id(1), pl.program_id(2)
    nbb = qk_ref.shape[0]
    b0 = g * nbb

    @pl.when(s == 0)
    def _():
        st_ref[pl.ds(b0, nbb)] = jnp.zeros((nbb,) + st_ref.shape[1:], F32)
        m_ref[pl.ds(b0, nbb)] = jnp.zeros((nbb,) + m_ref.shape[1:], F32)

    fwd = d == 0
    valid = _tri(d)
    tri = valid.astype(F32)
    r0s = _chunk_rows(d)
    cidx = [jnp.where(fwd, j, N_SUB - 1 - j) for j in range(N_SUB)]
    lane = lax.broadcasted_iota(jnp.int32, (CHUNK, LANE), 1)
    inst = [(bb, j) for bb in range(nbb) for j in range(N_SUB)]
    hinst = [(bb, j, h) for bb, j in inst for h in range(ML_HEADS)]
    gbr, gbc = gbr_ref[...], gbc_ref[...]

    def pick(a, h, kind, axis):
        i0, i1 = kind * ML_HEADS + h, (2 + kind) * ML_HEADS + h
        if axis == 1:
            return jnp.where(fwd, a[:, i0:i0 + 1], a[:, i1:i1 + 1])
        return jnp.where(fwd, a[i0:i0 + 1, :], a[i1:i1 + 1, :])

    gc, gr, fcum_c, fcum_r = {}, {}, {}, {}
    for bb, j in inst:
        i = (bb, j)
        gc[i] = g_ref[bb, pl.ds(r0s[j], CHUNK), :] + gbr
        gr[i] = gt_ref[bb, cidx[j]] + gbc
    for i in inst:
        fcum_c[i] = _dot(tri, _log_sigmoid(gc[i]), precision=HIGHEST)
        fcum_r[i] = _dot_nt(_log_sigmoid(gr[i]), tri, precision=HIGHEST)
    fc, lic, d_log, rmax = {}, {}, {}, {}
    for bb, j, h in hinst:
        i = (bb, j)
        fc[bb, j, h] = pick(fcum_c[i], h, 1, 1)
        lic[bb, j, h] = pick(gc[i], h, 0, 1)
        dl = jnp.where(valid, fc[bb, j, h] - pick(fcum_r[i], h, 1, 0) + pick(gr[i], h, 0, 0), NEG)
        d_log[bb, j, h] = dl
        rmax[bb, j, h] = jnp.max(dl, axis=-1, keepdims=True)
    m_prev, m_t, m_new, f_tot = {}, {}, {}, {}
    for bb in range(nbb):
        for h in range(ML_HEADS):
            m = m_ref[b0 + bb, h:h + 1, 0:1]
            for j in range(N_SUB):
                i = (bb, j, h)
                m_prev[i] = m
                m_t[i] = jnp.maximum(fc[i] + m, rmax[i])
                m = jnp.where(fwd, m_t[i][CHUNK - 1:CHUNK], m_t[i][0:1])
                m_new[i] = m
                f_tot[i] = jnp.where(fwd, fc[i][CHUNK - 1:CHUNK], fc[i][0:1])
            m_ref[b0 + bb, h:h + 1, :] = jnp.broadcast_to(m, (1, LANE))
    q, v, qk, w_prev, gdec, ds = {}, {}, {}, {}, {}, {}
    for bb, j, h in hinst:
        i = (bb, j, h)
        rs = pl.ds(r0s[j], CHUNK)
        sl = slice(h * LANE, (h + 1) * LANE)
        q[i] = qk_ref[bb, rs, sl].astype(BF16)
        k = qk_ref[bb, rs, HP + h * LANE:HP + (h + 1) * LANE]
        v[i] = jnp.where(lane == ML_DH, 1.0, v_ref[bb, rs, sl]).astype(BF16)
        w = jnp.exp(d_log[i] - m_t[i])
        w_prev[i] = jnp.exp(fc[i] + m_prev[i] - m_t[i])
        qk[i] = (_dot_nt(q[i], k.astype(BF16)) * w).astype(BF16)
        w_s = jnp.exp(f_tot[i] - fc[i] + lic[i] - m_new[i])
        gdec[i] = jnp.exp(f_tot[i] + m_prev[i] - m_new[i])
        ds[i] = _dot_tn(v[i], (k * w_s).astype(BF16))
    s_in = {}
    for bb in range(nbb):
        for h in range(ML_HEADS):
            st = st_ref[b0 + bb, h]
            for j in range(N_SUB):
                i = (bb, j, h)
                s_in[i] = st.astype(BF16)
                st = gdec[i] * st + ds[i]
            st_ref[b0 + bb, h] = st
    for bb, j, h in hinst:
        i = (bb, j, h)
        num = w_prev[i] * _dot_nt(q[i], s_in[i]) + _dot(qk[i], v[i])
        den = num[:, ML_DH:ML_DH + 1]
        hh = num / jnp.maximum(jnp.abs(den), jnp.exp(-m_t[i]))
        o_ref[0, bb, pl.ds(r0s[j], CHUNK), h * LANE:(h + 1) * LANE] = jnp.where(lane < ML_DH, hh, 0.0)


def _mlstm_old(qk3, pm3, gt4, gbr, gbc, *, rows):
    nb, l, _ = qk3.shape
    nbb = SCAN_BATCHES
    pos = functools.partial(_scan_pos, ncb=rows.ncb, ntb=rows.ntb)
    return pl.pallas_call(
        _mlstm_kernel,
        out_shape=jax.ShapeDtypeStruct((2, nb, l, HP), F32),
        grid=(2, rows.ntb, nb // nbb),
        in_specs=[pl.BlockSpec((nbb, SEG, 2 * HP), lambda d, s, g: (g, pos(d, s), 0)),
                  pl.BlockSpec((nbb, SEG, HP), lambda d, s, g: (g, pos(d, s), 2)),
                  pl.BlockSpec((nbb, SEG, LANE), lambda d, s, g: (g, pos(d, s), 4 * GLA_HEADS)),
                  pl.BlockSpec((nbb, N_SUB, 16, CHUNK), lambda d, s, g: (g, pos(d, s), 0, 0)),
                  pl.BlockSpec((1, LANE), lambda d, s, g: (0, 0)),
                  pl.BlockSpec((16, CHUNK), lambda d, s, g: (0, 0))],
        out_specs=pl.BlockSpec((1, nbb, SEG, HP), lambda d, s, g: (d, g, pos(d, s), 0)),
        scratch_shapes=[pltpu.VMEM((nb, ML_HEADS, LANE, LANE), F32), pltpu.VMEM((nb, 8, LANE), F32)],
        compiler_params=_cparams(("arbitrary", "arbitrary", "arbitrary")),
        name="mlstm_scan",
    )(qk3, pm3, pm3, gt4, gbr, gbc)


def _split_bf16(x, n):
    parts, r = [], x
    for _ in range(n):
        p = r.astype(BF16)
        parts.append(p)
        r = r - p.astype(F32)
    return parts


def _dot_exact01(a, b, lhs_is_01, pieces=3):
    if lhs_is_01:
        a = a.astype(BF16)
        terms = [_dot(a, p) for p in _split_bf16(b, pieces)]
    else:
        b = b.astype(BF16)
        terms = [_dot(p, b) for p in _split_bf16(a, pieces)]
    return functools.reduce(lambda x, y: x + y, terms)


def _cummax_rows(a, rev):
    n = a.shape[0]
    row = lax.broadcasted_iota(jnp.int32, a.shape, 0)
    k = 1
    while k < n:
        if rev:
            sh = jnp.where(row < n - k, pltpu.roll(a, n - k, axis=0), NEG)
        else:
            sh = jnp.where(row >= k, pltpu.roll(a, k, axis=0), NEG)
        a = jnp.maximum(a, sh)
        k *= 2
    return a


ML_GL = ML_HEADS


def _mlstm_kernel(rev, qk_ref, v_ref, g_ref, gt_ref, gbr_ref, gbt_ref, o_ref, st_ref, m_ref):
    s, g = pl.program_id(0), pl.program_id(1)
    nbb = qk_ref.shape[0]
    b0 = g * nbb

    @pl.when(s == 0)
    def _():
        st_ref[pl.ds(b0, nbb)] = jnp.zeros((nbb,) + st_ref.shape[1:], F32)
        m_ref[pl.ds(b0, nbb)] = jnp.zeros((nbb,) + m_ref.shape[1:], F32)

    valid = _tri(rev)
    tri = valid.astype(F32)
    r0s = _chunk_rows(rev)
    cs = [r // CHUNK for r in r0s]
    last = 0 if rev else CHUNK - 1
    inst = [(bb, j) for bb in range(nbb) for j in range(N_SUB)]
    heads = [slice(h * LANE, (h + 1) * LANE) for h in range(ML_HEADS)]

    r_sel = lax.broadcasted_iota(jnp.int32, (LANE, HP), 0)
    c_sel = lax.broadcasted_iota(jnp.int32, (LANE, HP), 1)
    sel_h = (r_sel == ML_GL + c_sel // LANE).astype(BF16)
    r_t = lax.broadcasted_iota(jnp.int32, (HP, HP), 0)
    c_t = lax.broadcasted_iota(jnp.int32, (HP, HP), 1)
    same = jnp.logical_and(r_t // LANE == c_t // LANE, jnp.logical_and(r_t % LANE < CHUNK, c_t % LANE < CHUNK))
    before = (r_t % LANE >= c_t % LANE) if rev else (r_t % LANE <= c_t % LANE)
    tri_b = jnp.logical_and(same, before).astype(BF16)
    r_v = lax.broadcasted_iota(jnp.int32, (CHUNK, HP), 0)
    c_v = lax.broadcasted_iota(jnp.int32, (CHUNK, HP), 1) % LANE
    valid4 = jnp.logical_and(c_v < CHUNK, (r_v <= c_v) if rev else (r_v >= c_v))
    lane4 = lax.broadcasted_iota(jnp.int32, (CHUNK, HP), 1) % LANE
    lane_c = lax.broadcasted_iota(jnp.int32, (CHUNK, LANE), 1)
    lane1 = lax.broadcasted_iota(jnp.int32, (1, LANE), 1)
    head_lane = jnp.logical_and(lane1 >= ML_GL, lane1 < ML_GL + ML_HEADS)
    gbr, gbt = gbr_ref[...], gbt_ref[...]

    gcs, fcm, cmx, a_row, grt = {}, {}, {}, {}, {}
    for bb, j in inst:
        gc = g_ref[bb, pl.ds(r0s[j], CHUNK), :] + gbr
        gcs[bb, j] = pltpu.roll(gc, ML_GL, axis=1)
        fcm[bb, j] = _dot_exact01(tri, _log_sigmoid(gc), lhs_is_01=True)
        grt[bb, j] = gt_ref[bb, cs[j]] + gbt
    row_id = lax.broadcasted_iota(jnp.int32, (len(inst), HP), 0)
    lfr = jnp.zeros((len(inst), HP), F32)
    for n, i in enumerate(inst):
        lfr = jnp.where(row_id == n, _log_sigmoid(grt[i][1:2]), lfr)
    fcr = _dot_exact01(lfr, tri_b, lhs_is_01=False)
    for n, i in enumerate(inst):
        a_row[i] = grt[i][0:1] - fcr[n:n + 1]
        cmx[i] = _cummax_rows(gcs[i] - fcm[i], rev)
    bx = {}
    e_neg, gd = {}, {}
    for bb in range(nbb):
        m_prev = m_ref[b0 + bb, 0:1, :]
        for j in range(N_SUB):
            i = (bb, j)
            m_t = fcm[i] + jnp.maximum(m_prev, cmx[i])
            m_new = m_t[last:last + 1]
            f_tot = fcm[i][last:last + 1]
            u = fcm[i] - m_t
            w_prev = jnp.exp(u + m_prev)
            w_s = jnp.exp(f_tot - fcm[i] + gcs[i] - m_new)
            gdec = jnp.broadcast_to(jnp.exp(f_tot + m_prev - m_new), (16, LANE))
            e_neg[i] = jnp.exp(-m_t)
            keep = lambda a: jnp.where(head_lane, a, 0.0)
            bx[i] = jnp.concatenate(_split_bf16(keep(u), 3) + _split_bf16(keep(w_prev), 2)
                                    + _split_bf16(keep(w_s), 2) + _split_bf16(keep(gdec), 2), axis=0)
            m_prev = m_new
        m_ref[b0 + bb] = jnp.broadcast_to(m_prev, (8, LANE))
    ub, wpb, wsb, gdb = {}, {}, {}, {}
    for i in inst:
        y = _dot(bx[i], sel_h)
        c = CHUNK
        ub[i] = y[0:c] + y[c:2 * c] + y[2 * c:3 * c]
        wpb[i] = y[3 * c:4 * c] + y[4 * c:5 * c]
        wsb[i] = y[5 * c:6 * c] + y[6 * c:7 * c]
        gdb[i] = y[7 * c:7 * c + 1] + y[7 * c + 16:7 * c + 17]
    q, v, qkw, ds = {}, {}, {}, {}
    for bb, j in inst:
        i = (bb, j)
        rs = pl.ds(r0s[j], CHUNK)
        w = jnp.where(valid4, jnp.exp(ub[i] + a_row[i]), 0.0)
        q[i] = qk_ref[bb, rs, 0:HP].astype(BF16)
        k = qk_ref[bb, rs, HP:2 * HP]
        kb = k.astype(BF16)
        kw = (k * wsb[i]).astype(BF16)
        v[i] = jnp.where(lane4 == ML_DH, 1.0, v_ref[bb, rs, :]).astype(BF16)
        for h, sl in enumerate(heads):
            sc = _dot_nt(q[i][:, sl], kb[:, sl])
            qkw[i, h] = (sc * w[:, h * LANE:h * LANE + CHUNK]).astype(BF16)
            ds[i, h] = _dot_tn(v[i][:, sl], kw[:, sl])
    s_in = {}
    for bb in range(nbb):
        for h, sl in enumerate(heads):
            st = st_ref[b0 + bb, h]
            for j in range(N_SUB):
                i = (bb, j)
                s_in[i, h] = st.astype(BF16)
                st = gdb[i][:, sl] * st + ds[i, h]
            st_ref[b0 + bb, h] = st
    num = {}
    for i in inst:
        parts = [_dot_nt(q[i][:, sl], s_in[i, h]) for h, sl in enumerate(heads)]
        intra = [_dot(qkw[i, h], v[i][:, sl]) for h, sl in enumerate(heads)]
        num[i] = wpb[i] * jnp.concatenate(parts, axis=1) + jnp.concatenate(intra, axis=1)
    for bb, j in inst:
        i = (bb, j)
        den = jnp.zeros((CHUNK, LANE), F32)
        for h, sl in enumerate(heads):
            dh = jnp.sum(jnp.where(lane_c == ML_DH, num[i][:, sl], 0.0), axis=-1, keepdims=True)
            den = jnp.where(lane_c == ML_GL + h, jnp.broadcast_to(dh, (CHUNK, LANE)), den)
        r1, r2 = _split_bf16(jnp.where(head_lane, 1.0 / jnp.maximum(jnp.abs(den), e_neg[i]), 0.0), 2)
        rb = _dot(r1, sel_h) + _dot(r2, sel_h)
        o_ref[bb, pl.ds(r0s[j], CHUNK), :] = jnp.where(lane4 < ML_DH, num[i] * rb, 0.0)


def _mlstm(qk3, pm3, gtl, gbr, gbt, *, rows, rev):
    nb, l, _ = qk3.shape
    nbb = SCAN_BATCHES
    dr = 1 if rev else 0
    pos = functools.partial(_scan_pos_static, rev, ncb=rows.ncb, ntb=rows.ntb)
    return pl.pallas_call(
        functools.partial(_mlstm_kernel, rev),
        out_shape=jax.ShapeDtypeStruct((nb, l, HP), F32),
        grid=(rows.ntb, nb // nbb),
        in_specs=[pl.BlockSpec((nbb, SEG, 2 * HP), lambda s, g: (g, pos(s), 0)),
                  pl.BlockSpec((nbb, SEG, HP), lambda s, g: (g, pos(s), 0)),
                  pl.BlockSpec((nbb, SEG, LANE), lambda s, g: (g, pos(s), 2 * ML_HEADS + dr)),
                  pl.BlockSpec((nbb, N_SUB, None, 2, HP), lambda s, g: (g, pos(s), dr, 0, 0)),
                  pl.BlockSpec((1, LANE), lambda s, g: (0, 0)),
                  pl.BlockSpec((2, HP), lambda s, g: (0, 0))],
        out_specs=pl.BlockSpec((nbb, SEG, HP), lambda s, g: (g, pos(s), 0)),
        scratch_shapes=[pltpu.VMEM((nb, ML_HEADS, LANE, LANE), F32), pltpu.VMEM((nb, 8, LANE), F32)],
        compiler_params=_cparams(("arbitrary", "arbitrary")),
        name="mlstm_scan_bwd" if rev else "mlstm_scan_fwd",
    )(qk3, pm3, pm3, gtl, gbr, gbt)


def _s5_kernel(nb, u_ref, bre_ref, bim_ref, are_ref, aim_ref, cre_ref, cim_ref, o_ref, xr_ref, xi_ref, st_ref):
    d, s = pl.program_id(0), pl.program_id(1)

    @pl.when(s == 0)
    def _():
        st_ref[...] = jnp.zeros(st_ref.shape, F32)

    u = u_ref[...].astype(BF16)
    xr_ref[...] = _dot(u, bre_ref[0])
    xi_ref[...] = _dot(u, bim_ref[0])
    ar, ai = are_ref[0], aim_ref[0]
    n_t = u_ref.shape[0] // nb

    def step(j, carry):
        sr, si = carry
        t = jnp.where(d == 0, j, n_t - 1 - j)
        r0 = pl.multiple_of(t * nb, nb)
        nr = ar * sr - ai * si + xr_ref[pl.ds(r0, nb), :]
        ni = ar * si + ai * sr + xi_ref[pl.ds(r0, nb), :]
        xr_ref[pl.ds(r0, nb), :] = nr
        xi_ref[pl.ds(r0, nb), :] = ni
        return nr, ni

    sr, si = lax.fori_loop(0, n_t, step, (st_ref[0], st_ref[1]), unroll=4)
    st_ref[0] = sr
    st_ref[1] = si
    o_ref[0] = _dot(xr_ref[...].astype(BF16), cre_ref[...]) - _dot(xi_ref[...].astype(BF16), cim_ref[...])


def _s5(ut, bre, bim, are, aim, cre, cim, *, nb, nc, nt):
    n_rows, ch = ut.shape
    tr = S5_STEPS * nb
    ns = bre.shape[-1]
    pos = functools.partial(_scan_pos, ncb=nc, ntb=nt)
    return pl.pallas_call(
        functools.partial(_s5_kernel, nb),
        out_shape=jax.ShapeDtypeStruct((2, n_rows, ch), F32),
        grid=(2, nt),
        in_specs=[pl.BlockSpec((tr, ch), lambda d, s: (pos(d, s), 0)),
                  pl.BlockSpec((1, ch, ns), lambda d, s: (d, 0, 0)),
                  pl.BlockSpec((1, ch, ns), lambda d, s: (d, 0, 0)),
                  pl.BlockSpec((1, nb, ns), lambda d, s: (d, 0, 0)),
                  pl.BlockSpec((1, nb, ns), lambda d, s: (d, 0, 0)),
                  pl.BlockSpec((ns, ch), lambda d, s: (0, 0)),
                  pl.BlockSpec((ns, ch), lambda d, s: (0, 0))],
        out_specs=pl.BlockSpec((1, tr, ch), lambda d, s: (d, pos(d, s), 0)),
        scratch_shapes=[pltpu.VMEM((tr, ns), F32), pltpu.VMEM((tr, ns), F32), pltpu.VMEM((2, nb, ns), F32)],
        compiler_params=_cparams(("arbitrary", "arbitrary")),
        name="s5_scan",
    )(ut, bre, bim, are, aim, cre, cim)


def _head_norm(o, gain, dim):
    parts = []
    for h in range(o.shape[1] // LANE):
        seg = o[:, h * LANE:(h + 1) * LANE]
        ms = jnp.sum(seg * seg, axis=-1, keepdims=True) * (1.0 / dim)
        parts.append(seg * lax.rsqrt(ms + EPS))
    return jnp.concatenate(parts, axis=1) * gain


def _mix_kernel(with_router, ogf_ref, ogb_ref, gg_ref, ys_ref, u_ref, omf_ref, omb_ref, mo_ref, h_ref, mod_ref,
                gn_ref, mn_ref, sd_ref, gw_ref, gb_ref, wo_ref, n2_ref, *rest):
    if with_router:
        wr_ref, ho_ref, f_ref, rt_ref = rest
    else:
        ho_ref, f_ref = rest
    gla = _head_norm(ogf_ref[...] + ogb_ref[...], gn_ref[...], GLA_DV) * _silu(gg_ref[...])
    z = _gelu_tanh(ys_ref[0] + ys_ref[1] + sd_ref[...] * u_ref[...])
    s5 = z * jax.nn.sigmoid(_dot(z.astype(BF16), gw_ref[...]) + gb_ref[...])
    ml = _head_norm(omf_ref[...] + omb_ref[...], mn_ref[...], ML_DH) * jax.nn.sigmoid(mo_ref[...])
    mix = (_dot(gla.astype(BF16), wo_ref[0:HP]) + _dot(s5.astype(BF16), wo_ref[HP:HP + 2 * LANE])
           + _dot(ml.astype(BF16), wo_ref[HP + 2 * LANE:]))
    m = mod_ref[0]
    hn = h_ref[...] + m[2:3] * mix
    ho_ref[...] = hn
    f = _rmsnorm(hn, n2_ref[...]) * (1.0 + m[4:5]) + m[3:4]
    f_ref[...] = f.astype(f_ref.dtype)
    if with_router:
        logits = _dot_nt(wr_ref[...], f, precision=HIGHEST)
        row = lax.broadcasted_iota(jnp.int32, logits.shape, 0)
        l0 = jnp.where(row < N_EXPERTS, logits, NEG)
        m1 = jnp.max(l0, axis=0, keepdims=True)
        i1 = jnp.min(jnp.where(l0 == m1, row, 2 * N_EXPERTS), axis=0, keepdims=True)
        l1 = jnp.where(row == i1, NEG, l0)
        m2 = jnp.max(l1, axis=0, keepdims=True)
        i2 = jnp.min(jnp.where(l1 == m2, row, 2 * N_EXPERTS), axis=0, keepdims=True)
        e = jnp.exp(m2 - m1)
        w1 = 1.0 / (1.0 + e)
        w2 = e / (1.0 + e)
        rt_ref[...] = jnp.where(row == 0, i1.astype(F32),
                                jnp.where(row == 1, i2.astype(F32),
                                          jnp.where(row == 2, w1, jnp.where(row == 3, w2, 0.0))))


def _mix(ogf, ogb, pg, ys, pu, omf, omb, pm, h, modtab, gn, mn, sd, gw, gb, wo, n2, wr, *, layer, rows, lat_only):
    d = h.shape[1]
    tm = SEG
    n = rows.n_blocks(lat_only)
    src, sel = rows.src(lat_only), rows.sel(lat_only)
    full = lambda a: pl.BlockSpec(a.shape, lambda i: (0,) * a.ndim)
    with_router = wr is not None
    in_specs = [pl.BlockSpec((tm, HP), lambda i: (src(i), 0)),
                pl.BlockSpec((tm, HP), lambda i: (src(i), 0)),
                pl.BlockSpec((tm, HP), lambda i: (src(i), (2 * GQ + HP) // HP)),
                pl.BlockSpec((2, tm, 2 * LANE), lambda i: (0, src(i), 0)),
                pl.BlockSpec((tm, 2 * LANE), lambda i: (src(i), 0)),
                pl.BlockSpec((tm, HP), lambda i: (src(i), 0)),
                pl.BlockSpec((tm, HP), lambda i: (src(i), 0)),
                pl.BlockSpec((tm, HP), lambda i: (src(i), 1)),
                pl.BlockSpec((tm, d), lambda i: (src(i), 0)),
                pl.BlockSpec((None, 1, 8, d), lambda i: (layer, sel(i), 0, 0)),
                full(gn), full(mn), full(sd), full(gw), full(gb), full(wo), full(n2)]
    args = [ogf, ogb, pg, ys, pu, omf, omb, pm, h, modtab, gn, mn, sd, gw, gb, wo, n2]
    out_shape = [jax.ShapeDtypeStruct((n * tm, d), F32), jax.ShapeDtypeStruct((n * tm, d), F32 if with_router else BF16)]
    out_specs = [pl.BlockSpec((tm, d), lambda i: (i, 0)), pl.BlockSpec((tm, d), lambda i: (i, 0))]
    if with_router:
        in_specs.append(full(wr))
        args.append(wr)
        out_shape.append(jax.ShapeDtypeStruct((2 * N_EXPERTS, n * tm), F32))
        out_specs.append(pl.BlockSpec((2 * N_EXPERTS, tm), lambda i: (0, i)))
    return pl.pallas_call(
        functools.partial(_mix_kernel, with_router),
        out_shape=tuple(out_shape),
        grid=(n,),
        in_specs=in_specs,
        out_specs=tuple(out_specs),
        compiler_params=_cparams(("arbitrary",)),
        name="mix_out",
    )(*args)


FF_TILE = 256


def _swiglu(xb, w1_ref, w3_ref, w2_ref, a_ref, lead=()):
    dff = w1_ref.shape[-1]
    for j in range(dff // FF_TILE):
        sl = slice(j * FF_TILE, (j + 1) * FF_TILE)
        h1 = _dot(xb, w1_ref[lead + (slice(None), sl)])
        h3 = _dot(xb, w3_ref[lead + (slice(None), sl)])
        a_ref[:, sl] = (_silu(h1) * h3).astype(BF16)
    return _dot(a_ref[...], w2_ref[lead + (slice(None), slice(None))])


FFN_TM = 1024


def _ffn_kernel(final, sel, f_ref, h_ref, mod_ref, w1_ref, w3_ref, w2_ref, *rest):
    if final:
        nf_ref, o_ref, a_ref = rest
    else:
        o_ref, a_ref = rest
    y = _swiglu(f_ref[...], w1_ref, w3_ref, w2_ref, a_ref)
    n_seg = f_ref.shape[0] // SEG
    for q in range(n_seg):
        rs = slice(q * SEG, (q + 1) * SEG)
        gate = mod_ref[sel(pl.program_id(0) * n_seg + q)][5:6]
        hn = h_ref[rs, :] + gate * y[rs]
        o_ref[rs, :] = _rmsnorm(hn, nf_ref[...]) if final else hn


def _ffn(f, h, modtab, w1, w3, w2, nf, *, layer, rows, lat_only):
    r, d = h.shape
    tm = FFN_TM
    assert r % tm == 0
    dff = w1.shape[-1]
    full = lambda a: pl.BlockSpec(a.shape, lambda i: (0,) * a.ndim)
    resident = lambda a: pl.BlockSpec(a.shape, lambda i: (0,) * a.ndim, pipeline_mode=pl.Buffered(1))
    in_specs = [pl.BlockSpec((tm, d), lambda i: (i, 0)),
                pl.BlockSpec((tm, d), lambda i: (i, 0)),
                pl.BlockSpec((None,) + modtab.shape[1:], lambda i: (layer, 0, 0, 0)),
                resident(w1), resident(w3), resident(w2)]
    args = [f, h, modtab, w1, w3, w2]
    if lat_only:
        in_specs.append(full(nf))
        args.append(nf)
    return pl.pallas_call(
        functools.partial(_ffn_kernel, lat_only, rows.sel(lat_only)),
        out_shape=jax.ShapeDtypeStruct((r, d), F32),
        grid=(r // tm,),
        in_specs=in_specs,
        out_specs=pl.BlockSpec((tm, d), lambda i: (i, 0)),
        scratch_shapes=[pltpu.VMEM((tm, dff), BF16)],
        compiler_params=_cparams(("arbitrary",)),
        name="ffn",
    )(*args)


MOE_TM = 512


def _moe_kernel(be_ref, live_ref, x_ref, rw_ref, w1_ref, w3_ref, w2_ref, o_ref, a_ref):
    i = pl.program_id(0)

    @pl.when(live_ref[i] > 0)
    def _():
        y = _swiglu(x_ref[...].astype(BF16), w1_ref, w3_ref, w2_ref, a_ref, lead=(0,))
        o_ref[...] = y * rw_ref[:, 0:1]

    @pl.when(live_ref[i] == 0)
    def _():
        o_ref[...] = jnp.zeros(o_ref.shape, F32)


def _moe_experts(block_expert, block_live, xg, rw, w1, w3, w2):
    n_rows, d = xg.shape
    dff = w1.shape[-1]
    tm = MOE_TM
    return pl.pallas_call(
        _moe_kernel,
        out_shape=jax.ShapeDtypeStruct((n_rows, d), F32),
        grid_spec=pltpu.PrefetchScalarGridSpec(
            num_scalar_prefetch=2,
            grid=(n_rows // tm,),
            in_specs=[pl.BlockSpec((tm, d), lambda i, be, lv: (i, 0)),
                      pl.BlockSpec((tm, LANE), lambda i, be, lv: (i, 0)),
                      pl.BlockSpec((1, d, dff), lambda i, be, lv: (be[i], 0, 0)),
                      pl.BlockSpec((1, d, dff), lambda i, be, lv: (be[i], 0, 0)),
                      pl.BlockSpec((1, dff, d), lambda i, be, lv: (be[i], 0, 0))],
            out_specs=pl.BlockSpec((tm, d), lambda i, be, lv: (i, 0)),
            scratch_shapes=[pltpu.VMEM((tm, dff), BF16)]),
        compiler_params=_cparams(("arbitrary",)),
        name="moe_experts",
    )(block_expert, block_live, xg, rw, w1, w3, w2)


SC_GATHER_ROWS = 64


def _gather_rows(table, idx):
    n_idx = idx.shape[0]
    _, d = table.shape
    info = plsc.get_sparse_core_info()
    n_cores, n_workers = info.num_cores, info.num_cores * info.num_subcores
    assert n_idx % (n_workers * SC_GATHER_ROWS) == 0
    per_worker = n_idx // n_workers
    mesh = plsc.VectorSubcoreMesh(core_axis_name="c", subcore_axis_name="s")

    @functools.partial(
        pl.kernel, mesh=mesh,
        out_type=jax.ShapeDtypeStruct((n_idx, d), table.dtype),
        scratch_types=[pltpu.VMEM((SC_GATHER_ROWS,), jnp.int32),
                       pltpu.VMEM((SC_GATHER_ROWS, d), table.dtype),
                       pltpu.SemaphoreType.DMA])
    def gather(table_hbm, idx_hbm, out_hbm, idx_v, rows_v, sem):
        base = (lax.axis_index("s") * n_cores + lax.axis_index("c")) * per_worker

        @pl.loop(0, per_worker // SC_GATHER_ROWS)
        def _(it):
            off = pl.multiple_of(base + it * SC_GATHER_ROWS, SC_GATHER_ROWS)
            pltpu.sync_copy(idx_hbm.at[pl.ds(off, SC_GATHER_ROWS)], idx_v)
            pltpu.async_copy(table_hbm.at[idx_v], rows_v, sem).wait()
            pltpu.sync_copy(rows_v, out_hbm.at[pl.ds(off, SC_GATHER_ROWS)])

    return gather(table, idx)


def _resid_kernel(final, h_ref, y0_ref, y1_ref, mod_ref, *rest):
    hn = h_ref[...] + mod_ref[0][5:6] * (y0_ref[...] + y1_ref[...])
    if final:
        nf_ref, o_ref = rest
        o_ref[...] = _rmsnorm(hn, nf_ref[...])
    else:
        (o_ref,) = rest
        o_ref[...] = hn


def _moe_resid(h, y0, y1, modtab, nf, *, layer, rows, lat_only):
    r, d = h.shape
    tm = SEG
    sel = rows.sel(lat_only)
    row = lambda i: (i, 0)
    in_specs = [pl.BlockSpec((tm, d), row), pl.BlockSpec((tm, d), row), pl.BlockSpec((tm, d), row),
                pl.BlockSpec((None, 1, 8, d), lambda i: (layer, sel(i), 0, 0))]
    args = [h, y0, y1, modtab]
    if lat_only:
        in_specs.append(pl.BlockSpec((1, d), lambda i: (0, 0)))
        args.append(nf)
    return pl.pallas_call(
        functools.partial(_resid_kernel, lat_only),
        out_shape=jax.ShapeDtypeStruct((r, d), F32),
        grid=(r // tm,),
        in_specs=in_specs,
        out_specs=pl.BlockSpec((tm, d), row),
        compiler_params=_cparams(("arbitrary",)),
        name="moe_resid",
    )(*args)


CAST_ROWS = 512


def _cast_kernel(x_ref, o_ref):
    o_ref[...] = x_ref[...].astype(o_ref.dtype)


def _to_bf16(w, j):
    lead, (r, c) = w.shape[1:-2], w.shape[-2:]
    n = int(np.prod(lead, dtype=np.int64))
    w3 = w.reshape((-1, r, c))
    tr = next(t for t in (CAST_ROWS, CAST_ROWS // 2, r) if r % t == 0)
    out = pl.pallas_call(
        _cast_kernel,
        out_shape=jax.ShapeDtypeStruct((n, r, c), BF16),
        grid=(n, r // tr),
        in_specs=[pl.BlockSpec((1, tr, c), lambda e, i: (j * n + e, i, 0))],
        out_specs=pl.BlockSpec((1, tr, c), lambda e, i: (e, i, 0)),
        compiler_params=_cparams(("arbitrary", "arbitrary")),
        name="to_bf16",
    )(w3)
    return out.reshape(lead + (r, c))


def _pad_heads(w, heads, dim, to=LANE):
    lead = w.shape[:-1]
    w = w.reshape(lead + (heads, dim))
    w = jnp.pad(w, [(0, 0)] * len(lead) + [(0, 0), (0, to - dim)])
    return w.reshape(lead + (heads * to,))


def _pad_last(w, to):
    return jnp.pad(w, [(0, 0)] * (w.ndim - 1) + [(0, to - w.shape[-1])])


def _pos_embed(n_tokens, d):
    n_grid_rows = n_tokens // GRID_W
    row, col = jnp.meshgrid(jnp.arange(n_grid_rows, dtype=F32), jnp.arange(GRID_W, dtype=F32), indexing='ij')
    n_freq = d // 4
    omega = jnp.exp(-math.log(POS_BASE) * jnp.arange(n_freq, dtype=F32) / n_freq)

    def axis_embed(p):
        ang = p.reshape(-1, 1) * omega
        return jnp.concatenate([jnp.sin(ang), jnp.cos(ang)], axis=-1)

    return jnp.concatenate([axis_embed(row), axis_embed(col)], axis=-1)


def _s5_discretise(lam_re, lam_im, log_dt, b_re, b_im):
    dt = jnp.exp(log_dt)[:, None]
    mag = jnp.exp(lam_re * dt)
    abar_re, abar_im = mag * jnp.cos(lam_im * dt), mag * jnp.sin(lam_im * dt)
    den = lam_re * lam_re + lam_im * lam_im
    pr, pi = abar_re - 1.0, abar_im
    coef_re = (pr * lam_re + pi * lam_im) / den
    coef_im = (pi * lam_re - pr * lam_im) / den
    bbar_re = coef_re[..., None] * b_re - coef_im[..., None] * b_im
    bbar_im = coef_re[..., None] * b_im + coef_im[..., None] * b_re
    return abar_re, abar_im, bbar_re, bbar_im


def _block_diag(m):
    g, a, b = m.shape
    eye = jnp.eye(g, dtype=m.dtype)
    return (eye[:, None, :, None] * m[:, :, None, :]).reshape(g * a, g * b)


def _route_plan(route, tm):
    n_tok = route.shape[1]
    n_assign = n_tok * TOP_K
    flat_e = route[0:TOP_K].astype(jnp.int32).reshape(-1)
    flat_w = route[TOP_K:2 * TOP_K].reshape(-1)
    onehot = (jnp.arange(N_EXPERTS, dtype=jnp.int32)[:, None] == flat_e[None, :]).astype(jnp.int32)
    csum = jnp.cumsum(onehot, axis=1)
    counts = csum[:, -1]
    padded = (counts + tm - 1) // tm * tm
    pend = jnp.cumsum(padded)
    pstart = pend - padded
    dest = jnp.sum(onehot * (csum - 1 + pstart[:, None]), axis=0)
    n_blocks = -(-n_assign // tm) + N_EXPERTS
    n_rows = n_blocks * tm
    row_assign = jnp.full((n_rows,), -1, jnp.int32).at[dest].set(jnp.arange(n_assign, dtype=jnp.int32))
    live = row_assign >= 0
    ra = jnp.maximum(row_assign, 0)
    row_token = jnp.where(live, ra % n_tok, jnp.arange(n_rows, dtype=jnp.int32) % n_tok)
    row_w = jnp.where(live, flat_w[ra], 0.0)
    block_start = jnp.arange(n_blocks, dtype=jnp.int32) * tm
    block_expert = jnp.minimum(jnp.searchsorted(pend, block_start, side='right'), N_EXPERTS - 1).astype(jnp.int32)
    block_live = (block_start < (pstart + counts)[block_expert]).astype(jnp.int32)
    return row_token, row_w, block_expert, block_live, dest.reshape(TOP_K, n_tok)


def kernel(x, c, ctx, c_ctx, w_ada, b_ada, norm1, norm2, w_in, w_out, gla_wa2, gla_ba, gla_norm, s5_lam_re, s5_lam_im, s5_log_dt, s5_b_re, s5_b_im, s5_c_re, s5_c_im, s5_d, s5_glu_w, s5_glu_b, ml_conv_w, ml_conv_b, ml_gate_b, ml_norm, ffn_w1, ffn_w3, ffn_w2, moe_router, moe_w1, moe_w3, moe_w2, norm_f):
    nb, n_lat, d = x.shape
    lc = ctx.shape[1]
    depth = w_ada.shape[0]
    l = lc + n_lat
    assert lc % SEG == 0 and n_lat % SEG == 0 and nb == 8 and nb % SCAN_BATCHES == 0
    rows = _Rows(nb, lc // SEG, l // SEG)

    h = _embed(ctx.reshape(nb * lc, d), x.reshape(nb * n_lat, d), _pos_embed(n_lat, d), rows)

    cond = jnp.zeros((16, d), F32).at[:nb].set(c).at[nb].set(c_ctx)
    mod = _modulation(cond, w_ada, b_ada)
    modtab = jnp.pad(mod.reshape(depth, 16, 6, d), ((0, 0), (0, 0), (0, 2), (0, 0)))

    dk, dv, dh = GLA_HEADS * GLA_DK, GLA_HEADS * GLA_DV, ML_HEADS * ML_DH
    s5c = s5_d.shape[-1]
    cuts = np.cumsum([dk, dk, dv, GLA_RANK, dv, s5c, dh, dh, dh, dh, 4 * ML_HEADS])

    for i in range(depth):
        last = i == depth - 1
        gq, gk, gv, glr, gg, su, mq, mk, mv, mo, mg = jnp.split(w_in[i], cuts[:-1], axis=-1)
        w_all = jnp.concatenate([
            _pad_heads(gq, GLA_HEADS, GLA_DK, GLA_KP), _pad_heads(gk, GLA_HEADS, GLA_DK, GLA_KP),
            _pad_heads(gv, GLA_HEADS, GLA_DV), _pad_heads(gg, GLA_HEADS, GLA_DV), _pad_last(glr, LANE),
            su,
            _pad_heads(mv, ML_HEADS, ML_DH), _pad_heads(mo, ML_HEADS, ML_DH),
            _pad_last(mg[:, :2 * ML_HEADS], LANE), _pad_last(mg[:, 2 * ML_HEADS:], LANE),
            _pad_heads(mq, ML_HEADS, ML_DH), _pad_heads(mk, ML_HEADS, ML_DH)],
            axis=-1).astype(BF16)
        wgt = mg.T.astype(BF16)
        cw = jnp.concatenate([_pad_heads(ml_conv_w[i][:, :dh], ML_HEADS, ML_DH),
                              _pad_heads(ml_conv_w[i][:, dh:], ML_HEADS, ML_DH)], axis=-1)
        cw = jnp.pad(cw, ((0, 8 - ML_CONV), (0, 0)))
        cb = jnp.concatenate([_pad_heads(ml_conv_b[i][:dh], ML_HEADS, ML_DH),
                              _pad_heads(ml_conv_b[i][dh:], ML_HEADS, ML_DH)])[None]
        post = jnp.concatenate([jnp.ones((HP,), F32), jnp.full((HP,), ML_DH ** -0.5, F32)])[None]
        pg, pu, pm, qk, gt = _proj(h, modtab, norm1[i][None], w_all, wgt, cw, cb, post, layer=i, rows=rows)

        wa = jnp.pad(_pad_heads(gla_wa2[i], GLA_HEADS, GLA_DK, GLA_KP), ((0, 0), (0, LANE - GLA_RANK), (0, 0)))
        ba = _pad_heads(gla_ba[i], GLA_HEADS, GLA_DK, GLA_KP)[:, None, :]
        pg3 = pg.reshape(nb, l, NG)
        ogf = _gla(pg3, wa[0], ba[0], rows=rows, rev=False).reshape(nb * l, HP)
        ogb = _gla(pg3, wa[1], ba[1], rows=rows, rev=True).reshape(nb * l, HP)

        bres, bims, ares, aims = [], [], [], []
        for dr in (0, 1):
            a_re, a_im, b_re, b_im = _s5_discretise(s5_lam_re[i, dr], s5_lam_im[i, dr], s5_log_dt[i, dr],
                                                    s5_b_re[i], s5_b_im[i])
            bres.append(_block_diag(jnp.swapaxes(b_re, 1, 2)))
            bims.append(_block_diag(jnp.swapaxes(b_im, 1, 2)))
            ares.append(jnp.broadcast_to(a_re.reshape(1, -1), (nb, a_re.size)))
            aims.append(jnp.broadcast_to(a_im.reshape(1, -1), (nb, a_im.size)))
        cre = _block_diag(jnp.swapaxes(s5_c_re[i], 1, 2)).astype(BF16)
        cim = _block_diag(jnp.swapaxes(s5_c_im[i], 1, 2)).astype(BF16)
        ut = pu.reshape(nb, l, s5c).swapaxes(0, 1).reshape(l * nb, s5c)
        yt = _s5(ut, jnp.stack(bres).astype(BF16), jnp.stack(bims).astype(BF16), jnp.stack(ares), jnp.stack(aims),
                 cre, cim, nb=nb, nc=lc // S5_STEPS, nt=l // S5_STEPS)
        ys = yt.reshape(2, l, nb, s5c).swapaxes(1, 2).reshape(2, nb * l, s5c)

        gb = ml_gate_b[i].reshape(2, 2, ML_HEADS)
        gbr = _pad_last(gb.reshape(2, 1, 2 * ML_HEADS), LANE)
        gbt = _pad_last(jnp.broadcast_to(gb[..., None], (2, 2, ML_HEADS, CHUNK)), LANE).reshape(2, 2, HP)
        gtl = _pad_last(gt.reshape(2, 2, ML_HEADS, nb, l // CHUNK, CHUNK).transpose(3, 4, 0, 1, 2, 5),
                        LANE).reshape(nb, l // CHUNK, 2, 2, HP)
        qk3, pm3 = qk.reshape(nb, l, 2 * HP), pm.reshape(nb, l, NM)
        omf = _mlstm(qk3, pm3, gtl, gbr[0], gbt[0], rows=rows, rev=False).reshape(nb * l, HP)
        omb = _mlstm(qk3, pm3, gtl, gbr[1], gbt[1], rows=rows, rev=True).reshape(nb * l, HP)

        wo = w_out[i]
        wo_p = jnp.concatenate([
            jnp.pad(wo[:dv].reshape(GLA_HEADS, GLA_DV, d), ((0, 0), (0, LANE - GLA_DV), (0, 0))).reshape(HP, d),
            wo[dv:dv + s5c],
            jnp.pad(wo[dv + s5c:].reshape(ML_HEADS, ML_DH, d), ((0, 0), (0, LANE - ML_DH), (0, 0))).reshape(HP, d)],
            axis=0).astype(BF16)
        gn = jnp.tile(_pad_last(gla_norm[i], LANE), GLA_HEADS)[None]
        mn = jnp.tile(_pad_last(ml_norm[i], LANE), ML_HEADS)[None]
        is_moe = i % 2 == 1
        j = i // 2
        wr = jnp.pad(moe_router[j].T, ((0, N_EXPERTS), (0, 0))) if is_moe else None
        outs = _mix(ogf, ogb, pg, ys, pu, omf, omb, pm, h, modtab, gn, mn, s5_d[i][None], s5_glu_w[i].astype(BF16),
                    s5_glu_b[i][None], wo_p, norm2[i][None], wr, layer=i, rows=rows, lat_only=last)
        if not is_moe:
            h, f = outs
            h = _ffn(f, h, modtab, _to_bf16(ffn_w1, j), _to_bf16(ffn_w3, j), _to_bf16(ffn_w2, j),
                     norm_f[None], layer=i, rows=rows, lat_only=last)
        else:
            h, f, route = outs
            row_token, row_w, block_expert, block_live, dest = _route_plan(route, MOE_TM)
            xg = _gather_rows(f, row_token)
            rw = jnp.broadcast_to(row_w[:, None], (row_w.shape[0], LANE))
            yg = _moe_experts(block_expert, block_live, xg, rw, _to_bf16(moe_w1, j), _to_bf16(moe_w3, j),
                              _to_bf16(moe_w2, j))
            y0 = _gather_rows(yg, dest[0])
            y1 = _gather_rows(yg, dest[1])
            h = _moe_resid(h, y0, y1, modtab, norm_f[None], layer=i, rows=rows, lat_only=last)
    return h.reshape(nb, n_lat, d)
```

```python
import functools
import math

import numpy as np
import jax
import jax.numpy as jnp
from jax import lax
from jax.experimental import pallas as pl
from jax.experimental.pallas import tpu as pltpu
from jax.experimental.pallas import tpu_sc as plsc

F32 = jnp.float32
BF16 = jnp.bfloat16
HIGHEST = lax.Precision.HIGHEST

GRID_W = 64
POS_BASE = 10000.0
EPS = 1e-6
GLA_HEADS, GLA_DK, GLA_DV, GLA_RANK, GLA_GATE_NORM = 4, 48, 96, 16, 16.0
S5_GROUP, S5_STATE = 16, 64
ML_HEADS, ML_DH, ML_CONV = 4, 96, 3
N_EXPERTS, TOP_K = 8, 2

LANE = 128
CHUNK = 64
SEG = 256
N_SUB = SEG // CHUNK
SCAN_BATCHES = 4
S5_STEPS = 128
S5_SUB = 32
NEG = -1e30
VMEM_LIMIT = 56 * 1024 * 1024

HP = LANE * GLA_HEADS
GLA_KP = 64
GQ = GLA_HEADS * GLA_KP
NG = 2 * GQ + 2 * HP + LANE
NM = 2 * HP + 2 * LANE


def _cparams(sem):
    return pltpu.CompilerParams(dimension_semantics=sem, vmem_limit_bytes=VMEM_LIMIT)


def _dot(a, b, **kw):
    return jnp.dot(a, b, preferred_element_type=F32, **kw)


def _dot_nt(a, b, **kw):
    return lax.dot_general(a, b, (((1,), (1,)), ((), ())), preferred_element_type=F32, **kw)


def _dot_tn(a, b, **kw):
    return lax.dot_general(a, b, (((0,), (0,)), ((), ())), preferred_element_type=F32, **kw)


def _log_sigmoid(x):
    return jnp.minimum(x, 0.0) - jnp.log1p(jnp.exp(-jnp.abs(x)))


def _silu(x):
    return x * jax.nn.sigmoid(x)


def _gelu_tanh(x):
    return 0.5 * x * (1.0 + jnp.tanh(math.sqrt(2.0 / math.pi) * (x + 0.044715 * (x * x * x))))


def _rmsnorm(x, g):
    return x * lax.rsqrt(jnp.mean(x * x, axis=-1, keepdims=True) + EPS) * g


class _Rows:
    def __init__(self, nb, ncb, ntb):
        self.nb, self.ncb, self.ntb, self.nlb = nb, ncb, ntb, ntb - ncb

    def n_blocks(self, lat_only):
        return self.nb * (self.nlb if lat_only else self.ntb)

    def src(self, lat_only):
        if lat_only:
            return lambda i: (i // self.nlb) * self.ntb + self.ncb + i % self.nlb
        return lambda i: i

    def sel(self, lat_only):
        if lat_only:
            return lambda i: i // self.nlb
        return lambda i: jnp.where(i % self.ntb < self.ncb, self.nb, i // self.ntb)


def _embed_kernel(ncb, ntb, ctx_ref, x_ref, pos_ref, o_ref):
    j = pl.program_id(0) % ntb

    @pl.when(j < ncb)
    def _():
        o_ref[...] = ctx_ref[...]

    @pl.when(j >= ncb)
    def _():
        o_ref[...] = x_ref[...] + pos_ref[...]


def _embed(ctx2, x2, pos, rows):
    d = ctx2.shape[1]
    ncb, ntb, nlb = rows.ncb, rows.ntb, rows.nlb
    return pl.pallas_call(
        functools.partial(_embed_kernel, ncb, ntb),
        out_shape=jax.ShapeDtypeStruct((rows.nb * ntb * SEG, d), F32),
        grid=(rows.nb * ntb,),
        in_specs=[pl.BlockSpec((SEG, d), lambda i: ((i // ntb) * ncb + jnp.minimum(i % ntb, ncb - 1), 0)),
                  pl.BlockSpec((SEG, d), lambda i: ((i // ntb) * nlb + jnp.maximum(i % ntb - ncb, 0), 0)),
                  pl.BlockSpec((SEG, d), lambda i: (jnp.maximum(i % ntb - ncb, 0), 0))],
        out_specs=pl.BlockSpec((SEG, d), lambda i: (i, 0)),
        compiler_params=_cparams(("arbitrary",)),
        name="embed",
    )(ctx2, x2, pos)


def _mod_kernel(c_ref, w_ref, b_ref, o_ref):
    s = _silu(c_ref[...])
    o_ref[0] = _dot(s, w_ref[0], precision=HIGHEST) + b_ref[0]


def _modulation(cond, w_ada, b_ada):
    depth, d, n6 = w_ada.shape
    tn = n6 // 4
    n_rows = cond.shape[0]
    return pl.pallas_call(
        _mod_kernel,
        out_shape=jax.ShapeDtypeStruct((depth, n_rows, n6), F32),
        grid=(depth, n6 // tn),
        in_specs=[pl.BlockSpec((n_rows, d), lambda l, j: (0, 0)),
                  pl.BlockSpec((1, d, tn), lambda l, j: (l, 0, j)),
                  pl.BlockSpec((1, 1, tn), lambda l, j: (l, 0, j))],
        out_specs=pl.BlockSpec((1, n_rows, tn), lambda l, j: (l, 0, j)),
        compiler_params=_cparams(("arbitrary", "arbitrary")),
        name="modulation",
    )(cond, w_ada, b_ada.reshape(depth, 1, n6))


def _proj_kernel(ncb, ntb, h_ref, hp_ref, hn_ref, mod_ref, g_ref, w_ref, wgt_ref, cw_ref, cb_ref, post_ref,
                 pg_ref, pu_ref, pm_ref, qk_ref, gt_ref):
    j = pl.program_id(0) % ntb
    m = mod_ref[0]
    act = lambda x: (_rmsnorm(x, g_ref[...]) * (1.0 + m[1:2]) + m[0:1]).astype(BF16)
    ab = act(h_ref[...])
    c1, c2 = NG + 2 * LANE, NG + 2 * LANE + NM
    pg_ref[...] = _dot(ab, w_ref[:, 0:NG])
    pu_ref[...] = _dot(ab, w_ref[:, NG:c1])
    pm_ref[...] = _dot(ab, w_ref[:, c1:c2])
    gt_ref[...] = _dot_nt(wgt_ref[...], ab)
    x = _dot(ab, w_ref[:, c2:])
    halo = _dot(act(jnp.concatenate([hp_ref[...], hn_ref[...]], axis=0)), w_ref[:, c2:])
    tm = x.shape[0]
    first = jnp.logical_or(j == 0, j == ncb).astype(F32)
    last = jnp.logical_or(j == ncb - 1, j == ntb - 1).astype(F32)
    row = lax.broadcasted_iota(jnp.int32, x.shape, 0)
    xp = jnp.where(row == 0, halo[7:8, :] * (1.0 - first), pltpu.roll(x, 1, axis=0))
    xn = jnp.where(row == tm - 1, halo[8:9, :] * (1.0 - last), pltpu.roll(x, tm - 1, axis=0))
    y = cw_ref[0:1] * xp + cw_ref[1:2] * x + cw_ref[2:3] * xn + cb_ref[...]
    qk_ref[...] = _silu(y) * post_ref[...]


def _proj(h, modtab, g, w, wgt, cw, cb, post, *, layer, rows):
    r, d = h.shape
    tm = SEG
    t8 = tm // 8
    sel = rows.sel(False)
    full = lambda a: pl.BlockSpec(a.shape, lambda i: (0,) * a.ndim)
    return pl.pallas_call(
        functools.partial(_proj_kernel, rows.ncb, rows.ntb),
        out_shape=(jax.ShapeDtypeStruct((r, NG), F32), jax.ShapeDtypeStruct((r, 2 * LANE), F32),
                   jax.ShapeDtypeStruct((r, NM), F32), jax.ShapeDtypeStruct((r, 2 * HP), F32),
                   jax.ShapeDtypeStruct((16, r), F32)),
        grid=(r // tm,),
        in_specs=[pl.BlockSpec((tm, d), lambda i: (i, 0)),
                  pl.BlockSpec((8, d), lambda i: (jnp.maximum(i * t8 - 1, 0), 0)),
                  pl.BlockSpec((8, d), lambda i: (jnp.minimum((i + 1) * t8, r // 8 - 1), 0)),
                  pl.BlockSpec((None, 1, 8, d), lambda i: (layer, sel(i), 0, 0)),
                  full(g), full(w), full(wgt), full(cw), full(cb), full(post)],
        out_specs=(pl.BlockSpec((tm, NG), lambda i: (i, 0)),
                   pl.BlockSpec((tm, 2 * LANE), lambda i: (i, 0)),
                   pl.BlockSpec((tm, NM), lambda i: (i, 0)),
                   pl.BlockSpec((tm, 2 * HP), lambda i: (i, 0)),
                   pl.BlockSpec((16, tm), lambda i: (0, i))),
        compiler_params=_cparams(("arbitrary",)),
        name="proj",
    )(h, h, h, modtab, g, w, wgt, cw, cb, post)


def _conv_kernel(ncb, ntb, x_ref, prev_ref, next_ref, w_ref, b_ref, post_ref, o_ref):
    j = pl.program_id(0) % ntb
    x = x_ref[...]
    tm = x.shape[0]
    first = jnp.logical_or(j == 0, j == ncb).astype(F32)
    last = jnp.logical_or(j == ncb - 1, j == ntb - 1).astype(F32)
    row = lax.broadcasted_iota(jnp.int32, x.shape, 0)
    xp = jnp.where(row == 0, prev_ref[7:8, :] * (1.0 - first), pltpu.roll(x, 1, axis=0))
    xn = jnp.where(row == tm - 1, next_ref[0:1, :] * (1.0 - last), pltpu.roll(x, tm - 1, axis=0))
    y = w_ref[0:1] * xp + w_ref[1:2] * x + w_ref[2:3] * xn + b_ref[...]
    o_ref[...] = _silu(y) * post_ref[...]


def _conv(pm, w, b, post, *, rows):
    r = pm.shape[0]
    tm = SEG
    wq = 2 * HP
    t8 = tm // 8
    return pl.pallas_call(
        functools.partial(_conv_kernel, rows.ncb, rows.ntb),
        out_shape=jax.ShapeDtypeStruct((r, wq), F32),
        grid=(r // tm,),
        in_specs=[pl.BlockSpec((tm, wq), lambda i: (i, 0)),
                  pl.BlockSpec((8, wq), lambda i: (jnp.maximum(i * t8 - 1, 0), 0)),
                  pl.BlockSpec((8, wq), lambda i: (jnp.minimum((i + 1) * t8, r // 8 - 1), 0)),
                  pl.BlockSpec((8, wq), lambda i: (0, 0)),
                  pl.BlockSpec((1, wq), lambda i: (0, 0)),
                  pl.BlockSpec((1, wq), lambda i: (0, 0))],
        out_specs=pl.BlockSpec((tm, wq), lambda i: (i, 0)),
        compiler_params=_cparams(("arbitrary",)),
        name="conv",
    )(pm, pm, pm, w, b, post)


def _scan_pos(d, s, ncb, ntb):
    rev = jnp.where(s < ncb, ncb - 1 - s, ntb - 1 - (s - ncb))
    return jnp.where(d == 0, s, rev)


def _scan_pos_static(rev, s, ncb, ntb):
    if not rev:
        return s
    return jnp.where(s < ncb, ncb - 1 - s, ntb - 1 - (s - ncb))


def _tri(rev):
    r = lax.broadcasted_iota(jnp.int32, (CHUNK, CHUNK), 0)
    c = lax.broadcasted_iota(jnp.int32, (CHUNK, CHUNK), 1)
    return (r <= c) if rev else (r >= c)


def _chunk_rows(rev):
    return [(N_SUB - 1 - j if rev else j) * CHUNK for j in range(N_SUB)]


def _gla_kernel(rev, p_ref, wa_ref, ba_ref, o_ref, st_ref):
    s, g = pl.program_id(0), pl.program_id(1)
    nbb = p_ref.shape[0]
    b0 = g * nbb

    @pl.when(s == 0)
    def _():
        st_ref[pl.ds(b0, nbb)] = jnp.zeros((nbb,) + st_ref.shape[1:], F32)

    valid = _tri(rev)
    tri = valid.astype(F32)
    r0s = _chunk_rows(rev)
    wa, ba = wa_ref[...], ba_ref[...]
    inst = [(bb, j) for bb in range(nbb) for j in range(N_SUB)]
    heads = [slice(h * LANE, (h + 1) * LANE) for h in range(GLA_HEADS)]
    pairs = [slice((h // 2) * LANE, (h // 2 + 1) * LANE) for h in range(GLA_HEADS)]
    lane = lax.broadcasted_iota(jnp.int32, (CHUNK, LANE), 1)
    own = [(lane // GLA_KP) == (h % 2) for h in range(GLA_HEADS)]

    la = {}
    for bb, j in inst:
        lr = p_ref[bb, pl.ds(r0s[j], CHUNK), 2 * GQ + 2 * HP:NG]
        la[bb, j] = _log_sigmoid(_dot(lr, wa) + ba) * (1.0 / GLA_GATE_NORM)
    bc, e_last = {}, {}
    for i in inst:
        bc[i] = _dot_exact01(tri, la[i], lhs_is_01=True, pieces=2)
        e_last[i] = jnp.exp(jnp.sum(la[i], axis=0, keepdims=True))
    q_in, k_in, k_out, v = {}, {}, {}, {}
    for bb, j in inst:
        i = (bb, j)
        rs = pl.ds(r0s[j], CHUNK)
        qs = (p_ref[bb, rs, 0:GQ] * (GLA_DK ** -0.5) * jnp.exp(bc[i])).astype(BF16)
        for h in range(GLA_HEADS):
            q_in[i, h] = jnp.where(own[h], qs[:, pairs[h]], jnp.zeros_like(qs[:, pairs[h]]))
        kd = p_ref[bb, rs, GQ:2 * GQ] * jnp.exp(-bc[i])
        k_out[i] = (kd * e_last[i]).astype(BF16)
        k_in[i] = kd.astype(BF16)
        v[i] = p_ref[bb, rs, 2 * GQ:2 * GQ + HP].astype(BF16)
    att = {}
    for i in inst:
        for h in range(GLA_HEADS):
            att[i, h] = jnp.where(valid, _dot_nt(q_in[i, h], k_in[i][:, pairs[h]]), 0.0).astype(BF16)
    o_intra, ds = {}, {}
    for i in inst:
        for h, sl in enumerate(heads):
            o_intra[i, h] = _dot(att[i, h], v[i][:, sl])
            ds[i, h] = _dot_tn(v[i][:, sl], k_out[i][:, pairs[h]])
    s_in = {}
    for bb in range(nbb):
        for h in range(GLA_HEADS):
            st = st_ref[b0 + bb, h]
            for j in range(N_SUB):
                s_in[(bb, j), h] = st.astype(BF16)
                st = st * e_last[bb, j][:, pairs[h]] + ds[(bb, j), h]
            st_ref[b0 + bb, h] = st
    for bb, j in inst:
        for h, sl in enumerate(heads):
            o = o_intra[(bb, j), h] + _dot_nt(q_in[(bb, j), h], s_in[(bb, j), h])
            o_ref[bb, pl.ds(r0s[j], CHUNK), sl] = o


def _gla(pg3, wa, ba, *, rows, rev):
    nb, l, _ = pg3.shape
    nbb = SCAN_BATCHES
    pos = functools.partial(_scan_pos_static, rev, ncb=rows.ncb, ntb=rows.ntb)
    return pl.pallas_call(
        functools.partial(_gla_kernel, rev),
        out_shape=jax.ShapeDtypeStruct((nb, l, HP), F32),
        grid=(rows.ntb, nb // nbb),
        in_specs=[pl.BlockSpec((nbb, SEG, NG), lambda s, g: (g, pos(s), 0)),
                  pl.BlockSpec((LANE, GQ), lambda s, g: (0, 0)),
                  pl.BlockSpec((1, GQ), lambda s, g: (0, 0))],
        out_specs=pl.BlockSpec((nbb, SEG, HP), lambda s, g: (g, pos(s), 0)),
        scratch_shapes=[pltpu.VMEM((nb, GLA_HEADS, LANE, LANE), F32)],
        compiler_params=_cparams(("arbitrary", "arbitrary")),
        name="gla_scan_bwd" if rev else "gla_scan_fwd",
    )(pg3, wa, ba)


def _mlstm_kernel_old(qk_ref, v_ref, g_ref, gt_ref, gbr_ref, gbc_ref, o_ref, st_ref, m_ref):
    d, s, g = pl.program_id(0), pl.program_id(1), pl.program_id(2)
    nbb = qk_ref.shape[0]
    b0 = g * nbb

    @pl.when(s == 0)
    def _():
        st_ref[pl.ds(b0, nbb)] = jnp.zeros((nbb,) + st_ref.shape[1:], F32)
        m_ref[pl.ds(b0, nbb)] = jnp.zeros((nbb,) + m_ref.shape[1:], F32)

    fwd = d == 0
    valid = _tri(d)
    tri = valid.astype(F32)
    r0s = _chunk_rows(d)
    cidx = [jnp.where(fwd, j, N_SUB - 1 - j) for j in range(N_SUB)]
    lane = lax.broadcasted_iota(jnp.int32, (CHUNK, LANE), 1)
    inst = [(bb, j) for bb in range(nbb) for j in range(N_SUB)]
    hinst = [(bb, j, h) for bb, j in inst for h in range(ML_HEADS)]
    gbr, gbc = gbr_ref[...], gbc_ref[...]

    def pick(a, h, kind, axis):
        i0, i1 = kind * ML_HEADS + h, (2 + kind) * ML_HEADS + h
        if axis == 1:
            return jnp.where(fwd, a[:, i0:i0 + 1], a[:, i1:i1 + 1])
        return jnp.where(fwd, a[i0:i0 + 1, :], a[i1:i1 + 1, :])

    gc, gr, fcum_c, fcum_r = {}, {}, {}, {}
    for bb, j in inst:
        i = (bb, j)
        gc[i] = g_ref[bb, pl.ds(r0s[j], CHUNK), :] + gbr
        gr[i] = gt_ref[bb, cidx[j]] + gbc
    for i in inst:
        fcum_c[i] = _dot(tri, _log_sigmoid(gc[i]), precision=HIGHEST)
        fcum_r[i] = _dot_nt(_log_sigmoid(gr[i]), tri, precision=HIGHEST)
    fc, lic, d_log, rmax = {}, {}, {}, {}
    for bb, j, h in hinst:
        i = (bb, j)
        fc[bb, j, h] = pick(fcum_c[i], h, 1, 1)
        lic[bb, j, h] = pick(gc[i], h, 0, 1)
        dl = jnp.where(valid, fc[bb, j, h] - pick(fcum_r[i], h, 1, 0) + pick(gr[i], h, 0, 0), NEG)
        d_log[bb, j, h] = dl
        rmax[bb, j, h] = jnp.max(dl, axis=-1, keepdims=True)
    m_prev, m_t, m_new, f_tot = {}, {}, {}, {}
    for bb in range(nbb):
        for h in range(ML_HEADS):
            m = m_ref[b0 + bb, h:h + 1, 0:1]
            for j in range(N_SUB):
                i = (bb, j, h)
                m_prev[i] = m
                m_t[i] = jnp.maximum(fc[i] + m, rmax[i])
                m = jnp.where(fwd, m_t[i][CHUNK - 1:CHUNK], m_t[i][0:1])
                m_new[i] = m
                f_tot[i] = jnp.where(fwd, fc[i][CHUNK - 1:CHUNK], fc[i][0:1])
            m_ref[b0 + bb, h:h + 1, :] = jnp.broadcast_to(m, (1, LANE))
    q, v, qk, w_prev, gdec, ds = {}, {}, {}, {}, {}, {}
    for bb, j, h in hinst:
        i = (bb, j, h)
        rs = pl.ds(r0s[j], CHUNK)
        sl = slice(h * LANE, (h + 1) * LANE)
        q[i] = qk_ref[bb, rs, sl].astype(BF16)
        k = qk_ref[bb, rs, HP + h * LANE:HP + (h + 1) * LANE]
        v[i] = jnp.where(lane == ML_DH, 1.0, v_ref[bb, rs, sl]).astype(BF16)
        w = jnp.exp(d_log[i] - m_t[i])
        w_prev[i] = jnp.exp(fc[i] + m_prev[i] - m_t[i])
        qk[i] = (_dot_nt(q[i], k.astype(BF16)) * w).astype(BF16)
        w_s = jnp.exp(f_tot[i] - fc[i] + lic[i] - m_new[i])
        gdec[i] = jnp.exp(f_tot[i] + m_prev[i] - m_new[i])
        ds[i] = _dot_tn(v[i], (k * w_s).astype(BF16))
    s_in = {}
    for bb in range(nbb):
        for h in range(ML_HEADS):
            st = st_ref[b0 + bb, h]
            for j in range(N_SUB):
                i = (bb, j, h)
                s_in[i] = st.astype(BF16)
                st = gdec[i] * st + ds[i]
            st_ref[b0 + bb, h] = st
    for bb, j, h in hinst:
        i = (bb, j, h)
        num = w_prev[i] * _dot_nt(q[i], s_in[i]) + _dot(qk[i], v[i])
        den = num[:, ML_DH:ML_DH + 1]
        hh = num / jnp.maximum(jnp.abs(den), jnp.exp(-m_t[i]))
        o_ref[0, bb, pl.ds(r0s[j], CHUNK), h * LANE:(h + 1) * LANE] = jnp.where(lane < ML_DH, hh, 0.0)


def _mlstm_old(qk3, pm3, gt4, gbr, gbc, *, rows):
    nb, l, _ = qk3.shape
    nbb = SCAN_BATCHES
    pos = functools.partial(_scan_pos, ncb=rows.ncb, ntb=rows.ntb)
    return pl.pallas_call(
        _mlstm_kernel,
        out_shape=jax.ShapeDtypeStruct((2, nb, l, HP), F32),
        grid=(2, rows.ntb, nb // nbb),
        in_specs=[pl.BlockSpec((nbb, SEG, 2 * HP), lambda d, s, g: (g, pos(d, s), 0)),
                  pl.BlockSpec((nbb, SEG, HP), lambda d, s, g: (g, pos(d, s), 2)),
                  pl.BlockSpec((nbb, SEG, LANE), lambda d, s, g: (g, pos(d, s), 4 * GLA_HEADS)),
                  pl.BlockSpec((nbb, N_SUB, 16, CHUNK), lambda d, s, g: (g, pos(d, s), 0, 0)),
                  pl.BlockSpec((1, LANE), lambda d, s, g: (0, 0)),
                  pl.BlockSpec((16, CHUNK), lambda d, s, g: (0, 0))],
        out_specs=pl.BlockSpec((1, nbb, SEG, HP), lambda d, s, g: (d, g, pos(d, s), 0)),
        scratch_shapes=[pltpu.VMEM((nb, ML_HEADS, LANE, LANE), F32), pltpu.VMEM((nb, 8, LANE), F32)],
        compiler_params=_cparams(("arbitrary", "arbitrary", "arbitrary")),
        name="mlstm_scan",
    )(qk3, pm3, pm3, gt4, gbr, gbc)


def _split_bf16(x, n):
    parts, r = [], x
    for _ in range(n):
        p = r.astype(BF16)
        parts.append(p)
        r = r - p.astype(F32)
    return parts


def _dot_exact01(a, b, lhs_is_01, pieces=3):
    if lhs_is_01:
        a = a.astype(BF16)
        terms = [_dot(a, p) for p in _split_bf16(b, pieces)]
    else:
        b = b.astype(BF16)
        terms = [_dot(p, b) for p in _split_bf16(a, pieces)]
    return functools.reduce(lambda x, y: x + y, terms)


def _cummax_rows(a, rev):
    n = a.shape[0]
    row = lax.broadcasted_iota(jnp.int32, a.shape, 0)
    k = 1
    while k < n:
        if rev:
            sh = jnp.where(row < n - k, pltpu.roll(a, n - k, axis=0), NEG)
        else:
            sh = jnp.where(row >= k, pltpu.roll(a, k, axis=0), NEG)
        a = jnp.maximum(a, sh)
        k *= 2
    return a


ML_GL = ML_HEADS


def _mlstm_kernel(rev, qk_ref, v_ref, g_ref, gt_ref, gbr_ref, gbt_ref, o_ref, st_ref, m_ref):
    s, g = pl.program_id(0), pl.program_id(1)
    nbb = qk_ref.shape[0]
    b0 = g * nbb

    @pl.when(s == 0)
    def _():
        st_ref[pl.ds(b0, nbb)] = jnp.zeros((nbb,) + st_ref.shape[1:], F32)
        m_ref[pl.ds(b0, nbb)] = jnp.zeros((nbb,) + m_ref.shape[1:], F32)

    valid = _tri(rev)
    tri = valid.astype(F32)
    r0s = _chunk_rows(rev)
    cs = [r // CHUNK for r in r0s]
    last = 0 if rev else CHUNK - 1
    inst = [(bb, j) for bb in range(nbb) for j in range(N_SUB)]
    heads = [slice(h * LANE, (h + 1) * LANE) for h in range(ML_HEADS)]

    r_sel = lax.broadcasted_iota(jnp.int32, (LANE, HP), 0)
    c_sel = lax.broadcasted_iota(jnp.int32, (LANE, HP), 1)
    sel_h = (r_sel == ML_GL + c_sel // LANE).astype(BF16)
    r_t = lax.broadcasted_iota(jnp.int32, (HP, HP), 0)
    c_t = lax.broadcasted_iota(jnp.int32, (HP, HP), 1)
    same = jnp.logical_and(r_t // LANE == c_t // LANE, jnp.logical_and(r_t % LANE < CHUNK, c_t % LANE < CHUNK))
    before = (r_t % LANE >= c_t % LANE) if rev else (r_t % LANE <= c_t % LANE)
    tri_b = jnp.logical_and(same, before).astype(BF16)
    r_v = lax.broadcasted_iota(jnp.int32, (CHUNK, HP), 0)
    c_v = lax.broadcasted_iota(jnp.int32, (CHUNK, HP), 1) % LANE
    valid4 = jnp.logical_and(c_v < CHUNK, (r_v <= c_v) if rev else (r_v >= c_v))
    lane4 = lax.broadcasted_iota(jnp.int32, (CHUNK, HP), 1) % LANE
    lane_c = lax.broadcasted_iota(jnp.int32, (CHUNK, LANE), 1)
    lane1 = lax.broadcasted_iota(jnp.int32, (1, LANE), 1)
    head_lane = jnp.logical_and(lane1 >= ML_GL, lane1 < ML_GL + ML_HEADS)
    gbr, gbt = gbr_ref[...], gbt_ref[...]

    gcs, fcm, cmx, a_row, grt = {}, {}, {}, {}, {}
    for bb, j in inst:
        gc = g_ref[bb, pl.ds(r0s[j], CHUNK), :] + gbr
        gcs[bb, j] = pltpu.roll(gc, ML_GL, axis=1)
        fcm[bb, j] = _dot_exact01(tri, _log_sigmoid(gc), lhs_is_01=True)
        grt[bb, j] = gt_ref[bb, cs[j]] + gbt
    row_id = lax.broadcasted_iota(jnp.int32, (len(inst), HP), 0)
    lfr = jnp.zeros((len(inst), HP), F32)
    for n, i in enumerate(inst):
        lfr = jnp.where(row_id == n, _log_sigmoid(grt[i][1:2]), lfr)
    fcr = _dot_exact01(lfr, tri_b, lhs_is_01=False)
    for n, i in enumerate(inst):
        a_row[i] = grt[i][0:1] - fcr[n:n + 1]
        cmx[i] = _cummax_rows(gcs[i] - fcm[i], rev)
    bx = {}
    e_neg, gd = {}, {}
    for bb in range(nbb):
        m_prev = m_ref[b0 + bb, 0:1, :]
        for j in range(N_SUB):
            i = (bb, j)
            m_t = fcm[i] + jnp.maximum(m_prev, cmx[i])
            m_new = m_t[last:last + 1]
            f_tot = fcm[i][last:last + 1]
            u = fcm[i] - m_t
            w_prev = jnp.exp(u + m_prev)
            w_s = jnp.exp(f_tot - fcm[i] + gcs[i] - m_new)
            gdec = jnp.broadcast_to(jnp.exp(f_tot + m_prev - m_new), (16, LANE))
            e_neg[i] = jnp.exp(-m_t)
            keep = lambda a: jnp.where(head_lane, a, 0.0)
            bx[i] = jnp.concatenate(_split_bf16(keep(u), 3) + _split_bf16(keep(w_prev), 2)
                                    + _split_bf16(keep(w_s), 2) + _split_bf16(keep(gdec), 2), axis=0)
            m_prev = m_new
        m_ref[b0 + bb] = jnp.broadcast_to(m_prev, (8, LANE))
    ub, wpb, wsb, gdb = {}, {}, {}, {}
    for i in inst:
        y = _dot(bx[i], sel_h)
        c = CHUNK
        ub[i] = y[0:c] + y[c:2 * c] + y[2 * c:3 * c]
        wpb[i] = y[3 * c:4 * c] + y[4 * c:5 * c]
        wsb[i] = y[5 * c:6 * c] + y[6 * c:7 * c]
        gdb[i] = y[7 * c:7 * c + 1] + y[7 * c + 16:7 * c + 17]
    q, v, qkw, ds = {}, {}, {}, {}
    for bb, j in inst:
        i = (bb, j)
        rs = pl.ds(r0s[j], CHUNK)
        w = jnp.where(valid4, jnp.exp(ub[i] + a_row[i]), 0.0)
        q[i] = qk_ref[bb, rs, 0:HP].astype(BF16)
        k = qk_ref[bb, rs, HP:2 * HP]
        kb = k.astype(BF16)
        kw = (k * wsb[i]).astype(BF16)
        v[i] = jnp.where(lane4 == ML_DH, 1.0, v_ref[bb, rs, :]).astype(BF16)
        for h, sl in enumerate(heads):
            sc = _dot_nt(q[i][:, sl], kb[:, sl])
            qkw[i, h] = (sc * w[:, h * LANE:h * LANE + CHUNK]).astype(BF16)
            ds[i, h] = _dot_tn(v[i][:, sl], kw[:, sl])
    s_in = {}
    for bb in range(nbb):
        for h, sl in enumerate(heads):
            st = st_ref[b0 + bb, h]
            for j in range(N_SUB):
                i = (bb, j)
                s_in[i, h] = st.astype(BF16)
                st = gdb[i][:, sl] * st + ds[i, h]
            st_ref[b0 + bb, h] = st
    num = {}
    for i in inst:
        parts = [_dot_nt(q[i][:, sl], s_in[i, h]) for h, sl in enumerate(heads)]
        intra = [_dot(qkw[i, h], v[i][:, sl]) for h, sl in enumerate(heads)]
        num[i] = wpb[i] * jnp.concatenate(parts, axis=1) + jnp.concatenate(intra, axis=1)
    for bb, j in inst:
        i = (bb, j)
        den = jnp.zeros((CHUNK, LANE), F32)
        for h, sl in enumerate(heads):
            dh = jnp.sum(jnp.where(lane_c == ML_DH, num[i][:, sl], 0.0), axis=-1, keepdims=True)
            den = jnp.where(lane_c == ML_GL + h, jnp.broadcast_to(dh, (CHUNK, LANE)), den)
        r1, r2 = _split_bf16(jnp.where(head_lane, 1.0 / jnp.maximum(jnp.abs(den), e_neg[i]), 0.0), 2)
        rb = _dot(r1, sel_h) + _dot(r2, sel_h)
        o_ref[bb, pl.ds(r0s[j], CHUNK), :] = jnp.where(lane4 < ML_DH, num[i] * rb, 0.0)


def _mlstm(qk3, pm3, gtl, gbr, gbt, *, rows, rev):
    nb, l, _ = qk3.shape
    nbb = SCAN_BATCHES
    dr = 1 if rev else 0
    pos = functools.partial(_scan_pos_static, rev, ncb=rows.ncb, ntb=rows.ntb)
    return pl.pallas_call(
        functools.partial(_mlstm_kernel, rev),
        out_shape=jax.ShapeDtypeStruct((nb, l, HP), F32),
        grid=(rows.ntb, nb // nbb),
        in_specs=[pl.BlockSpec((nbb, SEG, 2 * HP), lambda s, g: (g, pos(s), 0)),
                  pl.BlockSpec((nbb, SEG, HP), lambda s, g: (g, pos(s), 0)),
                  pl.BlockSpec((nbb, SEG, LANE), lambda s, g: (g, pos(s), 2 * ML_HEADS + dr)),
                  pl.BlockSpec((nbb, N_SUB, None, 2, HP), lambda s, g: (g, pos(s), dr, 0, 0)),
                  pl.BlockSpec((1, LANE), lambda s, g: (0, 0)),
                  pl.BlockSpec((2, HP), lambda s, g: (0, 0))],
        out_specs=pl.BlockSpec((nbb, SEG, HP), lambda s, g: (g, pos(s), 0)),
        scratch_shapes=[pltpu.VMEM((nb, ML_HEADS, LANE, LANE), F32), pltpu.VMEM((nb, 8, LANE), F32)],
        compiler_params=_cparams(("arbitrary", "arbitrary")),
        name="mlstm_scan_bwd" if rev else "mlstm_scan_fwd",
    )(qk3, pm3, pm3, gtl, gbr, gbt)


def _s5_kernel(nb, u_ref, bre_ref, bim_ref, are_ref, aim_ref, cre_ref, cim_ref, o_ref, xr_ref, xi_ref, st_ref):
    d, s = pl.program_id(0), pl.program_id(1)

    @pl.when(s == 0)
    def _():
        st_ref[...] = jnp.zeros(st_ref.shape, F32)

    n_sub = u_ref.shape[0] // (S5_SUB * nb)
    sub_rows = S5_SUB * nb

    def run(rev):
        ar, ai = are_ref[0], aim_ref[0]
        order = list(range(n_sub))[::-1] if rev else list(range(n_sub))

        def project_in(q):
            rs = slice(q * sub_rows, (q + 1) * sub_rows)
            u = u_ref[rs, :].astype(BF16)
            xr_ref[rs, :] = _dot(u, bre_ref[0])
            xi_ref[rs, :] = _dot(u, bim_ref[0])

        sr, si = st_ref[0], st_ref[1]
        project_in(order[0])
        for n, q in enumerate(order):
            if n + 1 < n_sub:
                project_in(order[n + 1])
            for j in range(S5_SUB):
                r0 = (q * S5_SUB + (S5_SUB - 1 - j if rev else j)) * nb
                nr = ar * sr - ai * si + xr_ref[r0:r0 + nb, :]
                ni = ar * si + ai * sr + xi_ref[r0:r0 + nb, :]
                xr_ref[r0:r0 + nb, :] = nr
                xi_ref[r0:r0 + nb, :] = ni
                sr, si = nr, ni
            rs = slice(q * sub_rows, (q + 1) * sub_rows)
            o_ref[0, rs, :] = (_dot(xr_ref[rs, :].astype(BF16), cre_ref[...])
                               - _dot(xi_ref[rs, :].astype(BF16), cim_ref[...]))
        st_ref[0] = sr
        st_ref[1] = si

    @pl.when(d == 0)
    def _():
        run(False)

    @pl.when(d == 1)
    def _():
        run(True)


def _s5(ut, bre, bim, are, aim, cre, cim, *, nb, nc, nt):
    n_rows, ch = ut.shape
    tr = S5_STEPS * nb
    ns = bre.shape[-1]
    pos = functools.partial(_scan_pos, ncb=nc, ntb=nt)
    return pl.pallas_call(
        functools.partial(_s5_kernel, nb),
        out_shape=jax.ShapeDtypeStruct((2, n_rows, ch), F32),
        grid=(2, nt),
        in_specs=[pl.BlockSpec((tr, ch), lambda d, s: (pos(d, s), 0)),
                  pl.BlockSpec((1, ch, ns), lambda d, s: (d, 0, 0)),
                  pl.BlockSpec((1, ch, ns), lambda d, s: (d, 0, 0)),
                  pl.BlockSpec((1, nb, ns), lambda d, s: (d, 0, 0)),
                  pl.BlockSpec((1, nb, ns), lambda d, s: (d, 0, 0)),
                  pl.BlockSpec((ns, ch), lambda d, s: (0, 0)),
                  pl.BlockSpec((ns, ch), lambda d, s: (0, 0))],
        out_specs=pl.BlockSpec((1, tr, ch), lambda d, s: (d, pos(d, s), 0)),
        scratch_shapes=[pltpu.VMEM((tr, ns), F32), pltpu.VMEM((tr, ns), F32), pltpu.VMEM((2, nb, ns), F32)],
        compiler_params=_cparams(("arbitrary", "arbitrary")),
        name="s5_scan",
    )(ut, bre, bim, are, aim, cre, cim)


def _head_norm(o, gain, dim):
    parts = []
    for h in range(o.shape[1] // LANE):
        seg = o[:, h * LANE:(h + 1) * LANE]
        ms = jnp.sum(seg * seg, axis=-1, keepdims=True) * (1.0 / dim)
        parts.append(seg * lax.rsqrt(ms + EPS))
    return jnp.concatenate(parts, axis=1) * gain


def _mix_kernel(with_router, ogf_ref, ogb_ref, gg_ref, ys_ref, u_ref, omf_ref, omb_ref, mo_ref, h_ref, mod_ref,
                gn_ref, mn_ref, sd_ref, gw_ref, gb_ref, wo_ref, n2_ref, *rest):
    if with_router:
        wr_ref, ho_ref, f_ref, rt_ref = rest
    else:
        ho_ref, f_ref = rest
    gla = _head_norm(ogf_ref[...] + ogb_ref[...], gn_ref[...], GLA_DV) * _silu(gg_ref[...])
    z = _gelu_tanh(ys_ref[0] + ys_ref[1] + sd_ref[...] * u_ref[...])
    s5 = z * jax.nn.sigmoid(_dot(z.astype(BF16), gw_ref[...]) + gb_ref[...])
    ml = _head_norm(omf_ref[...] + omb_ref[...], mn_ref[...], ML_DH) * jax.nn.sigmoid(mo_ref[...])
    mix = (_dot(gla.astype(BF16), wo_ref[0:HP]) + _dot(s5.astype(BF16), wo_ref[HP:HP + 2 * LANE])
           + _dot(ml.astype(BF16), wo_ref[HP + 2 * LANE:]))
    m = mod_ref[0]
    hn = h_ref[...] + m[2:3] * mix
    ho_ref[...] = hn
    f = _rmsnorm(hn, n2_ref[...]) * (1.0 + m[4:5]) + m[3:4]
    f_ref[...] = f.astype(f_ref.dtype)
    if with_router:
        logits = _dot_nt(wr_ref[...], f, precision=HIGHEST)
        row = lax.broadcasted_iota(jnp.int32, logits.shape, 0)
        l0 = jnp.where(row < N_EXPERTS, logits, NEG)
        m1 = jnp.max(l0, axis=0, keepdims=True)
        i1 = jnp.min(jnp.where(l0 == m1, row, 2 * N_EXPERTS), axis=0, keepdims=True)
        l1 = jnp.where(row == i1, NEG, l0)
        m2 = jnp.max(l1, axis=0, keepdims=True)
        i2 = jnp.min(jnp.where(l1 == m2, row, 2 * N_EXPERTS), axis=0, keepdims=True)
        e = jnp.exp(m2 - m1)
        w1 = 1.0 / (1.0 + e)
        w2 = e / (1.0 + e)
        rt_ref[...] = jnp.where(row == 0, i1.astype(F32),
                                jnp.where(row == 1, i2.astype(F32),
                                          jnp.where(row == 2, w1, jnp.where(row == 3, w2, 0.0))))


def _mix(ogf, ogb, pg, ys, pu, omf, omb, pm, h, modtab, gn, mn, sd, gw, gb, wo, n2, wr, *, layer, rows, lat_only):
    d = h.shape[1]
    tm = SEG
    n = rows.n_blocks(lat_only)
    src, sel = rows.src(lat_only), rows.sel(lat_only)
    full = lambda a: pl.BlockSpec(a.shape, lambda i: (0,) * a.ndim)
    with_router = wr is not None
    in_specs = [pl.BlockSpec((tm, HP), lambda i: (src(i), 0)),
                pl.BlockSpec((tm, HP), lambda i: (src(i), 0)),
                pl.BlockSpec((tm, HP), lambda i: (src(i), (2 * GQ + HP) // HP)),
                pl.BlockSpec((2, tm, 2 * LANE), lambda i: (0, src(i), 0)),
                pl.BlockSpec((tm, 2 * LANE), lambda i: (src(i), 0)),
                pl.BlockSpec((tm, HP), lambda i: (src(i), 0)),
                pl.BlockSpec((tm, HP), lambda i: (src(i), 0)),
                pl.BlockSpec((tm, HP), lambda i: (src(i), 1)),
                pl.BlockSpec((tm, d), lambda i: (src(i), 0)),
                pl.BlockSpec((None, 1, 8, d), lambda i: (layer, sel(i), 0, 0)),
                full(gn), full(mn), full(sd), full(gw), full(gb), full(wo), full(n2)]
    args = [ogf, ogb, pg, ys, pu, omf, omb, pm, h, modtab, gn, mn, sd, gw, gb, wo, n2]
    out_shape = [jax.ShapeDtypeStruct((n * tm, d), F32), jax.ShapeDtypeStruct((n * tm, d), F32 if with_router else BF16)]
    out_specs = [pl.BlockSpec((tm, d), lambda i: (i, 0)), pl.BlockSpec((tm, d), lambda i: (i, 0))]
    if with_router:
        in_specs.append(full(wr))
        args.append(wr)
        out_shape.append(jax.ShapeDtypeStruct((2 * N_EXPERTS, n * tm), F32))
        out_specs.append(pl.BlockSpec((2 * N_EXPERTS, tm), lambda i: (0, i)))
    return pl.pallas_call(
        functools.partial(_mix_kernel, with_router),
        out_shape=tuple(out_shape),
        grid=(n,),
        in_specs=in_specs,
        out_specs=tuple(out_specs),
        compiler_params=_cparams(("arbitrary",)),
        name="mix_out",
    )(*args)


FF_TILE = 256


def _swiglu(xb, w1_ref, w3_ref, w2_ref, a_ref, lead=()):
    dff = w1_ref.shape[-1]
    for j in range(dff // FF_TILE):
        sl = slice(j * FF_TILE, (j + 1) * FF_TILE)
        h1 = _dot(xb, w1_ref[lead + (slice(None), sl)])
        h3 = _dot(xb, w3_ref[lead + (slice(None), sl)])
        a_ref[:, sl] = (_silu(h1) * h3).astype(BF16)
    return _dot(a_ref[...], w2_ref[lead + (slice(None), slice(None))])


FFN_TM = 1024


def _ffn_kernel(final, sel, f_ref, h_ref, mod_ref, w1_ref, w3_ref, w2_ref, *rest):
    if final:
        nf_ref, o_ref, a_ref = rest
    else:
        o_ref, a_ref = rest
    y = _swiglu(f_ref[...], w1_ref, w3_ref, w2_ref, a_ref)
    n_seg = f_ref.shape[0] // SEG
    for q in range(n_seg):
        rs = slice(q * SEG, (q + 1) * SEG)
        gate = mod_ref[sel(pl.program_id(0) * n_seg + q)][5:6]
        hn = h_ref[rs, :] + gate * y[rs]
        o_ref[rs, :] = _rmsnorm(hn, nf_ref[...]) if final else hn


def _ffn(f, h, modtab, w1, w3, w2, nf, *, layer, rows, lat_only):
    r, d = h.shape
    tm = FFN_TM
    assert r % tm == 0
    dff = w1.shape[-1]
    full = lambda a: pl.BlockSpec(a.shape, lambda i: (0,) * a.ndim)
    resident = lambda a: pl.BlockSpec(a.shape, lambda i: (0,) * a.ndim, pipeline_mode=pl.Buffered(1))
    in_specs = [pl.BlockSpec((tm, d), lambda i: (i, 0)),
                pl.BlockSpec((tm, d), lambda i: (i, 0)),
                pl.BlockSpec((None,) + modtab.shape[1:], lambda i: (layer, 0, 0, 0)),
                resident(w1), resident(w3), resident(w2)]
    args = [f, h, modtab, w1, w3, w2]
    if lat_only:
        in_specs.append(full(nf))
        args.append(nf)
    return pl.pallas_call(
        functools.partial(_ffn_kernel, lat_only, rows.sel(lat_only)),
        out_shape=jax.ShapeDtypeStruct((r, d), F32),
        grid=(r // tm,),
        in_specs=in_specs,
        out_specs=pl.BlockSpec((tm, d), lambda i: (i, 0)),
        scratch_shapes=[pltpu.VMEM((tm, dff), BF16)],
        compiler_params=_cparams(("arbitrary",)),
        name="ffn",
    )(*args)


MOE_TM = 512


def _moe_kernel(be_ref, live_ref, x_ref, rw_ref, w1_ref, w3_ref, w2_ref, o_ref, a_ref):
    i = pl.program_id(0)

    @pl.when(live_ref[i] > 0)
    def _():
        y = _swiglu(x_ref[...].astype(BF16), w1_ref, w3_ref, w2_ref, a_ref, lead=(0,))
        o_ref[...] = y * rw_ref[:, 0:1]

    @pl.when(live_ref[i] == 0)
    def _():
        o_ref[...] = jnp.zeros(o_ref.shape, F32)


def _moe_experts(block_expert, block_live, xg, rw, w1, w3, w2):
    n_rows, d = xg.shape
    dff = w1.shape[-1]
    tm = MOE_TM
    return pl.pallas_call(
        _moe_kernel,
        out_shape=jax.ShapeDtypeStruct((n_rows, d), F32),
        grid_spec=pltpu.PrefetchScalarGridSpec(
            num_scalar_prefetch=2,
            grid=(n_rows // tm,),
            in_specs=[pl.BlockSpec((tm, d), lambda i, be, lv: (i, 0)),
                      pl.BlockSpec((tm, LANE), lambda i, be, lv: (i, 0)),
                      pl.BlockSpec((1, d, dff), lambda i, be, lv: (be[i], 0, 0)),
                      pl.BlockSpec((1, d, dff), lambda i, be, lv: (be[i], 0, 0)),
                      pl.BlockSpec((1, dff, d), lambda i, be, lv: (be[i], 0, 0))],
            out_specs=pl.BlockSpec((tm, d), lambda i, be, lv: (i, 0)),
            scratch_shapes=[pltpu.VMEM((tm, dff), BF16)]),
        compiler_params=_cparams(("arbitrary",)),
        name="moe_experts",
    )(block_expert, block_live, xg, rw, w1, w3, w2)


SC_GATHER_ROWS = 64


def _gather_rows(table, idx):
    n_idx = idx.shape[0]
    _, d = table.shape
    info = plsc.get_sparse_core_info()
    n_cores, n_workers = info.num_cores, info.num_cores * info.num_subcores
    assert n_idx % (n_workers * SC_GATHER_ROWS) == 0
    per_worker = n_idx // n_workers
    mesh = plsc.VectorSubcoreMesh(core_axis_name="c", subcore_axis_name="s")

    @functools.partial(
        pl.kernel, mesh=mesh,
        out_type=jax.ShapeDtypeStruct((n_idx, d), table.dtype),
        scratch_types=[pltpu.VMEM((SC_GATHER_ROWS,), jnp.int32),
                       pltpu.VMEM((SC_GATHER_ROWS, d), table.dtype),
                       pltpu.SemaphoreType.DMA])
    def gather(table_hbm, idx_hbm, out_hbm, idx_v, rows_v, sem):
        base = (lax.axis_index("s") * n_cores + lax.axis_index("c")) * per_worker

        @pl.loop(0, per_worker // SC_GATHER_ROWS)
        def _(it):
            off = pl.multiple_of(base + it * SC_GATHER_ROWS, SC_GATHER_ROWS)
            pltpu.sync_copy(idx_hbm.at[pl.ds(off, SC_GATHER_ROWS)], idx_v)
            pltpu.async_copy(table_hbm.at[idx_v], rows_v, sem).wait()
            pltpu.sync_copy(rows_v, out_hbm.at[pl.ds(off, SC_GATHER_ROWS)])

    return gather(table, idx)


def _resid_kernel(final, h_ref, y0_ref, y1_ref, mod_ref, *rest):
    hn = h_ref[...] + mod_ref[0][5:6] * (y0_ref[...] + y1_ref[...])
    if final:
        nf_ref, o_ref = rest
        o_ref[...] = _rmsnorm(hn, nf_ref[...])
    else:
        (o_ref,) = rest
        o_ref[...] = hn


def _moe_resid(h, y0, y1, modtab, nf, *, layer, rows, lat_only):
    r, d = h.shape
    tm = SEG
    sel = rows.sel(lat_only)
    row = lambda i: (i, 0)
    in_specs = [pl.BlockSpec((tm, d), row), pl.BlockSpec((tm, d), row), pl.BlockSpec((tm, d), row),
                pl.BlockSpec((None, 1, 8, d), lambda i: (layer, sel(i), 0, 0))]
    args = [h, y0, y1, modtab]
    if lat_only:
        in_specs.append(pl.BlockSpec((1, d), lambda i: (0, 0)))
        args.append(nf)
    return pl.pallas_call(
        functools.partial(_resid_kernel, lat_only),
        out_shape=jax.ShapeDtypeStruct((r, d), F32),
        grid=(r // tm,),
        in_specs=in_specs,
        out_specs=pl.BlockSpec((tm, d), row),
        compiler_params=_cparams(("arbitrary",)),
        name="moe_resid",
    )(*args)


CAST_PARTS = 4


def _cast_kernel(*refs):
    o_ref = refs[-1]
    tr = refs[0].shape[1]
    for k, x_ref in enumerate(refs[:-1]):
        o_ref[0, k * tr:(k + 1) * tr, :] = x_ref[0].astype(o_ref.dtype)


def _to_bf16(w, j):
    lead, (r, c) = w.shape[1:-2], w.shape[-2:]
    n = int(np.prod(lead, dtype=np.int64))
    w3 = w.reshape((-1, r, c))
    tr = r // CAST_PARTS
    assert tr * CAST_PARTS == r and tr % 16 == 0
    band = lambda k: pl.BlockSpec((1, tr, c), lambda e: (j * n + e, k, 0))
    out = pl.pallas_call(
        _cast_kernel,
        out_shape=jax.ShapeDtypeStruct((n, r, c), BF16),
        grid=(n,),
        in_specs=[band(k) for k in range(CAST_PARTS)],
        out_specs=pl.BlockSpec((1, r, c), lambda e: (e, 0, 0)),
        compiler_params=_cparams(("arbitrary",)),
        name="to_bf16",
    )(*([w3] * CAST_PARTS))
    return out.reshape(lead + (r, c))


def _pad_heads(w, heads, dim, to=LANE):
    lead = w.shape[:-1]
    w = w.reshape(lead + (heads, dim))
    w = jnp.pad(w, [(0, 0)] * len(lead) + [(0, 0), (0, to - dim)])
    return w.reshape(lead + (heads * to,))


def _pad_last(w, to):
    return jnp.pad(w, [(0, 0)] * (w.ndim - 1) + [(0, to - w.shape[-1])])


def _pos_embed(n_tokens, d):
    n_grid_rows = n_tokens // GRID_W
    row, col = jnp.meshgrid(jnp.arange(n_grid_rows, dtype=F32), jnp.arange(GRID_W, dtype=F32), indexing='ij')
    n_freq = d // 4
    omega = jnp.exp(-math.log(POS_BASE) * jnp.arange(n_freq, dtype=F32) / n_freq)

    def axis_embed(p):
        ang = p.reshape(-1, 1) * omega
        return jnp.concatenate([jnp.sin(ang), jnp.cos(ang)], axis=-1)

    return jnp.concatenate([axis_embed(row), axis_embed(col)], axis=-1)


def _s5_discretise(lam_re, lam_im, log_dt, b_re, b_im):
    dt = jnp.exp(log_dt)[:, None]
    mag = jnp.exp(lam_re * dt)
    abar_re, abar_im = mag * jnp.cos(lam_im * dt), mag * jnp.sin(lam_im * dt)
    den = lam_re * lam_re + lam_im * lam_im
    pr, pi = abar_re - 1.0, abar_im
    coef_re = (pr * lam_re + pi * lam_im) / den
    coef_im = (pi * lam_re - pr * lam_im) / den
    bbar_re = coef_re[..., None] * b_re - coef_im[..., None] * b_im
    bbar_im = coef_re[..., None] * b_im + coef_im[..., None] * b_re
    return abar_re, abar_im, bbar_re, bbar_im


def _block_diag(m):
    g, a, b = m.shape
    eye = jnp.eye(g, dtype=m.dtype)
    return (eye[:, None, :, None] * m[:, :, None, :]).reshape(g * a, g * b)


def _route_plan(route, tm):
    n_tok = route.shape[1]
    n_assign = n_tok * TOP_K
    flat_e = route[0:TOP_K].astype(jnp.int32).reshape(-1)
    flat_w = route[TOP_K:2 * TOP_K].reshape(-1)
    onehot = (jnp.arange(N_EXPERTS, dtype=jnp.int32)[:, None] == flat_e[None, :]).astype(jnp.int32)
    csum = jnp.cumsum(onehot, axis=1)
    counts = csum[:, -1]
    padded = (counts + tm - 1) // tm * tm
    pend = jnp.cumsum(padded)
    pstart = pend - padded
    dest = jnp.sum(onehot * (csum - 1 + pstart[:, None]), axis=0)
    n_blocks = -(-n_assign // tm) + N_EXPERTS
    n_rows = n_blocks * tm
    row_assign = jnp.full((n_rows,), -1, jnp.int32).at[dest].set(jnp.arange(n_assign, dtype=jnp.int32))
    live = row_assign >= 0
    ra = jnp.maximum(row_assign, 0)
    row_token = jnp.where(live, ra % n_tok, jnp.arange(n_rows, dtype=jnp.int32) % n_tok)
    row_w = jnp.where(live, flat_w[ra], 0.0)
    block_start = jnp.arange(n_blocks, dtype=jnp.int32) * tm
    block_expert = jnp.minimum(jnp.searchsorted(pend, block_start, side='right'), N_EXPERTS - 1).astype(jnp.int32)
    block_live = (block_start < (pstart + counts)[block_expert]).astype(jnp.int32)
    return row_token, row_w, block_expert, block_live, dest.reshape(TOP_K, n_tok)


def kernel(x, c, ctx, c_ctx, w_ada, b_ada, norm1, norm2, w_in, w_out, gla_wa2, gla_ba, gla_norm, s5_lam_re, s5_lam_im, s5_log_dt, s5_b_re, s5_b_im, s5_c_re, s5_c_im, s5_d, s5_glu_w, s5_glu_b, ml_conv_w, ml_conv_b, ml_gate_b, ml_norm, ffn_w1, ffn_w3, ffn_w2, moe_router, moe_w1, moe_w3, moe_w2, norm_f):
    nb, n_lat, d = x.shape
    lc = ctx.shape[1]
    depth = w_ada.shape[0]
    l = lc + n_lat
    assert lc % SEG == 0 and n_lat % SEG == 0 and nb == 8 and nb % SCAN_BATCHES == 0
    rows = _Rows(nb, lc // SEG, l // SEG)

    h = _embed(ctx.reshape(nb * lc, d), x.reshape(nb * n_lat, d), _pos_embed(n_lat, d), rows)

    cond = jnp.zeros((16, d), F32).at[:nb].set(c).at[nb].set(c_ctx)
    mod = _modulation(cond, w_ada, b_ada)
    modtab = jnp.pad(mod.reshape(depth, 16, 6, d), ((0, 0), (0, 0), (0, 2), (0, 0)))

    dk, dv, dh = GLA_HEADS * GLA_DK, GLA_HEADS * GLA_DV, ML_HEADS * ML_DH
    s5c = s5_d.shape[-1]
    cuts = np.cumsum([dk, dk, dv, GLA_RANK, dv, s5c, dh, dh, dh, dh, 4 * ML_HEADS])

    for i in range(depth):
        last = i == depth - 1
        gq, gk, gv, glr, gg, su, mq, mk, mv, mo, mg = jnp.split(w_in[i], cuts[:-1], axis=-1)
        w_all = jnp.concatenate([
            _pad_heads(gq, GLA_HEADS, GLA_DK, GLA_KP), _pad_heads(gk, GLA_HEADS, GLA_DK, GLA_KP),
            _pad_heads(gv, GLA_HEADS, GLA_DV), _pad_heads(gg, GLA_HEADS, GLA_DV), _pad_last(glr, LANE),
            su,
            _pad_heads(mv, ML_HEADS, ML_DH), _pad_heads(mo, ML_HEADS, ML_DH),
            _pad_last(mg[:, :2 * ML_HEADS], LANE), _pad_last(mg[:, 2 * ML_HEADS:], LANE),
            _pad_heads(mq, ML_HEADS, ML_DH), _pad_heads(mk, ML_HEADS, ML_DH)],
            axis=-1).astype(BF16)
        wgt = mg.T.astype(BF16)
        cw = jnp.concatenate([_pad_heads(ml_conv_w[i][:, :dh], ML_HEADS, ML_DH),
                              _pad_heads(ml_conv_w[i][:, dh:], ML_HEADS, ML_DH)], axis=-1)
        cw = jnp.pad(cw, ((0, 8 - ML_CONV), (0, 0)))
        cb = jnp.concatenate([_pad_heads(ml_conv_b[i][:dh], ML_HEADS, ML_DH),
                              _pad_heads(ml_conv_b[i][dh:], ML_HEADS, ML_DH)])[None]
        post = jnp.concatenate([jnp.ones((HP,), F32), jnp.full((HP,), ML_DH ** -0.5, F32)])[None]
        pg, pu, pm, qk, gt = _proj(h, modtab, norm1[i][None], w_all, wgt, cw, cb, post, layer=i, rows=rows)

        wa = jnp.pad(_pad_heads(gla_wa2[i], GLA_HEADS, GLA_DK, GLA_KP), ((0, 0), (0, LANE - GLA_RANK), (0, 0)))
        ba = _pad_heads(gla_ba[i], GLA_HEADS, GLA_DK, GLA_KP)[:, None, :]
        pg3 = pg.reshape(nb, l, NG)
        ogf = _gla(pg3, wa[0], ba[0], rows=rows, rev=False).reshape(nb * l, HP)
        ogb = _gla(pg3, wa[1], ba[1], rows=rows, rev=True).reshape(nb * l, HP)

        bres, bims, ares, aims = [], [], [], []
        for dr in (0, 1):
            a_re, a_im, b_re, b_im = _s5_discretise(s5_lam_re[i, dr], s5_lam_im[i, dr], s5_log_dt[i, dr],
                                                    s5_b_re[i], s5_b_im[i])
            bres.append(_block_diag(jnp.swapaxes(b_re, 1, 2)))
            bims.append(_block_diag(jnp.swapaxes(b_im, 1, 2)))
            ares.append(jnp.broadcast_to(a_re.reshape(1, -1), (nb, a_re.size)))
            aims.append(jnp.broadcast_to(a_im.reshape(1, -1), (nb, a_im.size)))
        cre = _block_diag(jnp.swapaxes(s5_c_re[i], 1, 2)).astype(BF16)
        cim = _block_diag(jnp.swapaxes(s5_c_im[i], 1, 2)).astype(BF16)
        ut = pu.reshape(nb, l, s5c).swapaxes(0, 1).reshape(l * nb, s5c)
        yt = _s5(ut, jnp.stack(bres).astype(BF16), jnp.stack(bims).astype(BF16), jnp.stack(ares), jnp.stack(aims),
                 cre, cim, nb=nb, nc=lc // S5_STEPS, nt=l // S5_STEPS)
        ys = yt.reshape(2, l, nb, s5c).swapaxes(1, 2).reshape(2, nb * l, s5c)

        gb = ml_gate_b[i].reshape(2, 2, ML_HEADS)
        gbr = _pad_last(gb.reshape(2, 1, 2 * ML_HEADS), LANE)
        gbt = _pad_last(jnp.broadcast_to(gb[..., None], (2, 2, ML_HEADS, CHUNK)), LANE).reshape(2, 2, HP)
        gtl = _pad_last(gt.reshape(2, 2, ML_HEADS, nb, l // CHUNK, CHUNK).transpose(3, 4, 0, 1, 2, 5),
                        LANE).reshape(nb, l // CHUNK, 2, 2, HP)
        qk3, pm3 = qk.reshape(nb, l, 2 * HP), pm.reshape(nb, l, NM)
        omf = _mlstm(qk3, pm3, gtl, gbr[0], gbt[0], rows=rows, rev=False).reshape(nb * l, HP)
        omb = _mlstm(qk3, pm3, gtl, gbr[1], gbt[1], rows=rows, rev=True).reshape(nb * l, HP)

        wo = w_out[i]
        wo_p = jnp.concatenate([
            jnp.pad(wo[:dv].reshape(GLA_HEADS, GLA_DV, d), ((0, 0), (0, LANE - GLA_DV), (0, 0))).reshape(HP, d),
            wo[dv:dv + s5c],
            jnp.pad(wo[dv + s5c:].reshape(ML_HEADS, ML_DH, d), ((0, 0), (0, LANE - ML_DH), (0, 0))).reshape(HP, d)],
            axis=0).astype(BF16)
        gn = jnp.tile(_pad_last(gla_norm[i], LANE), GLA_HEADS)[None]
        mn = jnp.tile(_pad_last(ml_norm[i], LANE), ML_HEADS)[None]
        is_moe = i % 2 == 1
        j = i // 2
        wr = jnp.pad(moe_router[j].T, ((0, N_EXPERTS), (0, 0))) if is_moe else None
        outs = _mix(ogf, ogb, pg, ys, pu, omf, omb, pm, h, modtab, gn, mn, s5_d[i][None], s5_glu_w[i].astype(BF16),
                    s5_glu_b[i][None], wo_p, norm2[i][None], wr, layer=i, rows=rows, lat_only=last)
        if not is_moe:
            h, f = outs
            h = _ffn(f, h, modtab, _to_bf16(ffn_w1, j), _to_bf16(ffn_w3, j), _to_bf16(ffn_w2, j),
                     norm_f[None], layer=i, rows=rows, lat_only=last)
        else:
            h, f, route = outs
            row_token, row_w, block_expert, block_live, dest = _route_plan(route, MOE_TM)
            xg = _gather_rows(f, row_token)
            rw = jnp.broadcast_to(row_w[:, None], (row_w.shape[0], LANE))
            yg = _moe_experts(block_expert, block_live, xg, rw, _to_bf16(moe_w1, j), _to_bf16(moe_w3, j),
                              _to_bf16(moe_w2, j))
            y0 = _gather_rows(yg, dest[0])
            y1 = _gather_rows(yg, dest[1])
            h = _moe_resid(h, y0, y1, modtab, norm_f[None], layer=i, rows=rows, lat_only=last)
    return h.reshape(nb, n_lat, d)
```

```python
import functools
import math

import numpy as np
import jax
import jax.numpy as jnp
from jax import lax
from jax.experimental import pallas as pl
from jax.experimental.pallas import tpu as pltpu
from jax.experimental.pallas import tpu_sc as plsc

F32 = jnp.float32
BF16 = jnp.bfloat16
HIGHEST = lax.Precision.HIGHEST

GRID_W = 64
POS_BASE = 10000.0
EPS = 1e-6
GLA_HEADS, GLA_DK, GLA_DV, GLA_RANK, GLA_GATE_NORM = 4, 48, 96, 16, 16.0
S5_GROUP, S5_STATE = 16, 64
ML_HEADS, ML_DH, ML_CONV = 4, 96, 3
N_EXPERTS, TOP_K = 8, 2

LANE = 128
CHUNK = 64
SEG = 256
N_SUB = SEG // CHUNK
SCAN_BATCHES = 4
S5_STEPS = 128
S5_SUB = 32
NEG = -1e30
VMEM_LIMIT = 56 * 1024 * 1024

HP = LANE * GLA_HEADS
GLA_KP = 64
GQ = GLA_HEADS * GLA_KP
NG = 2 * GQ + 2 * HP + LANE
NM = 2 * HP + 2 * LANE


def _cparams(sem):
    return pltpu.CompilerParams(dimension_semantics=sem, vmem_limit_bytes=VMEM_LIMIT)


def _dot(a, b, **kw):
    return jnp.dot(a, b, preferred_element_type=F32, **kw)


def _dot_nt(a, b, **kw):
    return lax.dot_general(a, b, (((1,), (1,)), ((), ())), preferred_element_type=F32, **kw)


def _dot_tn(a, b, **kw):
    return lax.dot_general(a, b, (((0,), (0,)), ((), ())), preferred_element_type=F32, **kw)


def _log_sigmoid(x):
    return jnp.minimum(x, 0.0) - jnp.log1p(jnp.exp(-jnp.abs(x)))


def _silu(x):
    return x * jax.nn.sigmoid(x)


def _gelu_tanh(x):
    return 0.5 * x * (1.0 + jnp.tanh(math.sqrt(2.0 / math.pi) * (x + 0.044715 * (x * x * x))))


def _rmsnorm(x, g):
    return x * lax.rsqrt(jnp.mean(x * x, axis=-1, keepdims=True) + EPS) * g


class _Rows:
    def __init__(self, nb, ncb, ntb):
        self.nb, self.ncb, self.ntb, self.nlb = nb, ncb, ntb, ntb - ncb

    def n_blocks(self, lat_only):
        return self.nb * (self.nlb if lat_only else self.ntb)

    def src(self, lat_only):
        if lat_only:
            return lambda i: (i // self.nlb) * self.ntb + self.ncb + i % self.nlb
        return lambda i: i

    def sel(self, lat_only):
        if lat_only:
            return lambda i: i // self.nlb
        return lambda i: jnp.where(i % self.ntb < self.ncb, self.nb, i // self.ntb)


def _embed_kernel(ncb, ntb, ctx_ref, x_ref, pos_ref, o_ref):
    j = pl.program_id(0) % ntb

    @pl.when(j < ncb)
    def _():
        o_ref[...] = ctx_ref[...]

    @pl.when(j >= ncb)
    def _():
        o_ref[...] = x_ref[...] + pos_ref[...]


def _embed(ctx2, x2, pos, rows):
    d = ctx2.shape[1]
    ncb, ntb, nlb = rows.ncb, rows.ntb, rows.nlb
    return pl.pallas_call(
        functools.partial(_embed_kernel, ncb, ntb),
        out_shape=jax.ShapeDtypeStruct((rows.nb * ntb * SEG, d), F32),
        grid=(rows.nb * ntb,),
        in_specs=[pl.BlockSpec((SEG, d), lambda i: ((i // ntb) * ncb + jnp.minimum(i % ntb, ncb - 1), 0)),
                  pl.BlockSpec((SEG, d), lambda i: ((i // ntb) * nlb + jnp.maximum(i % ntb - ncb, 0), 0)),
                  pl.BlockSpec((SEG, d), lambda i: (jnp.maximum(i % ntb - ncb, 0), 0))],
        out_specs=pl.BlockSpec((SEG, d), lambda i: (i, 0)),
        compiler_params=_cparams(("arbitrary",)),
        name="embed",
    )(ctx2, x2, pos)


def _mod_kernel(c_ref, w_ref, b_ref, o_ref):
    s = _silu(c_ref[...])
    o_ref[0] = _dot(s, w_ref[0], precision=HIGHEST) + b_ref[0]


def _modulation(cond, w_ada, b_ada):
    depth, d, n6 = w_ada.shape
    tn = n6 // 4
    n_rows = cond.shape[0]
    return pl.pallas_call(
        _mod_kernel,
        out_shape=jax.ShapeDtypeStruct((depth, n_rows, n6), F32),
        grid=(depth, n6 // tn),
        in_specs=[pl.BlockSpec((n_rows, d), lambda l, j: (0, 0)),
                  pl.BlockSpec((1, d, tn), lambda l, j: (l, 0, j)),
                  pl.BlockSpec((1, 1, tn), lambda l, j: (l, 0, j))],
        out_specs=pl.BlockSpec((1, n_rows, tn), lambda l, j: (l, 0, j)),
        compiler_params=_cparams(("arbitrary", "arbitrary")),
        name="modulation",
    )(cond, w_ada, b_ada.reshape(depth, 1, n6))


def _proj_kernel(ncb, ntb, h_ref, hp_ref, hn_ref, mod_ref, g_ref, w_ref, wgt_ref, cw_ref, cb_ref, post_ref,
                 pg_ref, pu_ref, pm_ref, qk_ref, gt_ref):
    j = pl.program_id(0) % ntb
    m = mod_ref[0]
    act = lambda x: (_rmsnorm(x, g_ref[...]) * (1.0 + m[1:2]) + m[0:1]).astype(BF16)
    ab = act(h_ref[...])
    c1, c2 = NG + 2 * LANE, NG + 2 * LANE + NM
    pg_ref[...] = _dot(ab, w_ref[:, 0:NG])
    pu_ref[...] = _dot(ab, w_ref[:, NG:c1])
    pm_ref[...] = _dot(ab, w_ref[:, c1:c2])
    gt_ref[...] = _dot_nt(wgt_ref[...], ab)
    tm = h_ref.shape[0]
    he = jnp.concatenate([hp_ref[...], h_ref[...], hn_ref[...]], axis=0)
    xe = _dot(act(he), w_ref[:, c2:])
    n_e = tm + 16
    first = jnp.logical_or(j == 0, j == ncb).astype(F32)
    last = jnp.logical_or(j == ncb - 1, j == ntb - 1).astype(F32)
    row = lax.broadcasted_iota(jnp.int32, (tm, xe.shape[1]), 0)
    xp = pltpu.roll(xe, 1, axis=0)[8:8 + tm] * jnp.where(row == 0, 1.0 - first, 1.0)
    xn = pltpu.roll(xe, n_e - 1, axis=0)[8:8 + tm] * jnp.where(row == tm - 1, 1.0 - last, 1.0)
    y = cw_ref[0:1] * xp + cw_ref[1:2] * xe[8:8 + tm] + cw_ref[2:3] * xn + cb_ref[...]
    qk_ref[...] = _silu(y) * post_ref[...]


def _proj(h, modtab, g, w, wgt, cw, cb, post, *, layer, rows):
    r, d = h.shape
    tm = SEG
    t8 = tm // 8
    sel = rows.sel(False)
    full = lambda a: pl.BlockSpec(a.shape, lambda i: (0,) * a.ndim)
    return pl.pallas_call(
        functools.partial(_proj_kernel, rows.ncb, rows.ntb),
        out_shape=(jax.ShapeDtypeStruct((r, NG), F32), jax.ShapeDtypeStruct((r, 2 * LANE), F32),
                   jax.ShapeDtypeStruct((r, NM), F32), jax.ShapeDtypeStruct((r, 2 * HP), F32),
                   jax.ShapeDtypeStruct((16, r), F32)),
        grid=(r // tm,),
        in_specs=[pl.BlockSpec((tm, d), lambda i: (i, 0)),
                  pl.BlockSpec((8, d), lambda i: (jnp.maximum(i * t8 - 1, 0), 0)),
                  pl.BlockSpec((8, d), lambda i: (jnp.minimum((i + 1) * t8, r // 8 - 1), 0)),
                  pl.BlockSpec((None, 1, 8, d), lambda i: (layer, sel(i), 0, 0)),
                  full(g), full(w), full(wgt), full(cw), full(cb), full(post)],
        out_specs=(pl.BlockSpec((tm, NG), lambda i: (i, 0)),
                   pl.BlockSpec((tm, 2 * LANE), lambda i: (i, 0)),
                   pl.BlockSpec((tm, NM), lambda i: (i, 0)),
                   pl.BlockSpec((tm, 2 * HP), lambda i: (i, 0)),
                   pl.BlockSpec((16, tm), lambda i: (0, i))),
        compiler_params=_cparams(("arbitrary",)),
        name="proj",
    )(h, h, h, modtab, g, w, wgt, cw, cb, post)


def _conv_kernel(ncb, ntb, x_ref, prev_ref, next_ref, w_ref, b_ref, post_ref, o_ref):
    j = pl.program_id(0) % ntb
    x = x_ref[...]
    tm = x.shape[0]
    first = jnp.logical_or(j == 0, j == ncb).astype(F32)
    last = jnp.logical_or(j == ncb - 1, j == ntb - 1).astype(F32)
    row = lax.broadcasted_iota(jnp.int32, x.shape, 0)
    xp = jnp.where(row == 0, prev_ref[7:8, :] * (1.0 - first), pltpu.roll(x, 1, axis=0))
    xn = jnp.where(row == tm - 1, next_ref[0:1, :] * (1.0 - last), pltpu.roll(x, tm - 1, axis=0))
    y = w_ref[0:1] * xp + w_ref[1:2] * x + w_ref[2:3] * xn + b_ref[...]
    o_ref[...] = _silu(y) * post_ref[...]


def _conv(pm, w, b, post, *, rows):
    r = pm.shape[0]
    tm = SEG
    wq = 2 * HP
    t8 = tm // 8
    return pl.pallas_call(
        functools.partial(_conv_kernel, rows.ncb, rows.ntb),
        out_shape=jax.ShapeDtypeStruct((r, wq), F32),
        grid=(r // tm,),
        in_specs=[pl.BlockSpec((tm, wq), lambda i: (i, 0)),
                  pl.BlockSpec((8, wq), lambda i: (jnp.maximum(i * t8 - 1, 0), 0)),
                  pl.BlockSpec((8, wq), lambda i: (jnp.minimum((i + 1) * t8, r // 8 - 1), 0)),
                  pl.BlockSpec((8, wq), lambda i: (0, 0)),
                  pl.BlockSpec((1, wq), lambda i: (0, 0)),
                  pl.BlockSpec((1, wq), lambda i: (0, 0))],
        out_specs=pl.BlockSpec((tm, wq), lambda i: (i, 0)),
        compiler_params=_cparams(("arbitrary",)),
        name="conv",
    )(pm, pm, pm, w, b, post)


def _scan_pos(d, s, ncb, ntb):
    rev = jnp.where(s < ncb, ncb - 1 - s, ntb - 1 - (s - ncb))
    return jnp.where(d == 0, s, rev)


def _scan_pos_static(rev, s, ncb, ntb):
    if not rev:
        return s
    return jnp.where(s < ncb, ncb - 1 - s, ntb - 1 - (s - ncb))


def _tri(rev):
    r = lax.broadcasted_iota(jnp.int32, (CHUNK, CHUNK), 0)
    c = lax.broadcasted_iota(jnp.int32, (CHUNK, CHUNK), 1)
    return (r <= c) if rev else (r >= c)


def _chunk_rows(rev):
    return [(N_SUB - 1 - j if rev else j) * CHUNK for j in range(N_SUB)]


def _gla_kernel(rev, p_ref, wa_ref, ba_ref, o_ref, st_ref):
    s, g = pl.program_id(0), pl.program_id(1)
    nbb = p_ref.shape[0]
    b0 = g * nbb

    @pl.when(s == 0)
    def _():
        st_ref[pl.ds(b0, nbb)] = jnp.zeros((nbb,) + st_ref.shape[1:], F32)

    valid = _tri(rev)
    tri = valid.astype(F32)
    r0s = _chunk_rows(rev)
    wa, ba = wa_ref[...], ba_ref[...]
    inst = [(bb, j) for bb in range(nbb) for j in range(N_SUB)]
    heads = [slice(h * LANE, (h + 1) * LANE) for h in range(GLA_HEADS)]
    pairs = [slice((h // 2) * LANE, (h // 2 + 1) * LANE) for h in range(GLA_HEADS)]
    lane = lax.broadcasted_iota(jnp.int32, (CHUNK, LANE), 1)
    own = [(lane // GLA_KP) == (h % 2) for h in range(GLA_HEADS)]

    la = {}
    for bb, j in inst:
        lr = p_ref[bb, pl.ds(r0s[j], CHUNK), 2 * GQ + 2 * HP:NG]
        la[bb, j] = _log_sigmoid(_dot(lr, wa) + ba) * (1.0 / GLA_GATE_NORM)
    bc, e_last = {}, {}
    for i in inst:
        bc[i] = _dot_exact01(tri, la[i], lhs_is_01=True, pieces=2)
        e_last[i] = jnp.exp(jnp.sum(la[i], axis=0, keepdims=True))
    q_in, k_in, k_out, v = {}, {}, {}, {}
    for bb, j in inst:
        i = (bb, j)
        rs = pl.ds(r0s[j], CHUNK)
        qs = (p_ref[bb, rs, 0:GQ] * (GLA_DK ** -0.5) * jnp.exp(bc[i])).astype(BF16)
        for h in range(GLA_HEADS):
            q_in[i, h] = jnp.where(own[h], qs[:, pairs[h]], jnp.zeros_like(qs[:, pairs[h]]))
        kd = p_ref[bb, rs, GQ:2 * GQ] * jnp.exp(-bc[i])
        k_out[i] = (kd * e_last[i]).astype(BF16)
        k_in[i] = kd.astype(BF16)
        v[i] = p_ref[bb, rs, 2 * GQ:2 * GQ + HP].astype(BF16)
    att = {}
    for i in inst:
        for h in range(GLA_HEADS):
            att[i, h] = jnp.where(valid, _dot_nt(q_in[i, h], k_in[i][:, pairs[h]]), 0.0).astype(BF16)
    o_intra, ds = {}, {}
    for i in inst:
        for h, sl in enumerate(heads):
            o_intra[i, h] = _dot(att[i, h], v[i][:, sl])
            ds[i, h] = _dot_tn(v[i][:, sl], k_out[i][:, pairs[h]])
    s_in = {}
    for bb in range(nbb):
        for h in range(GLA_HEADS):
            st = st_ref[b0 + bb, h]
            for j in range(N_SUB):
                s_in[(bb, j), h] = st.astype(BF16)
                st = st * e_last[bb, j][:, pairs[h]] + ds[(bb, j), h]
            st_ref[b0 + bb, h] = st
    for bb, j in inst:
        for h, sl in enumerate(heads):
            o = o_intra[(bb, j), h] + _dot_nt(q_in[(bb, j), h], s_in[(bb, j), h])
            o_ref[bb, pl.ds(r0s[j], CHUNK), sl] = o.astype(o_ref.dtype)


def _gla(pg3, wa, ba, *, rows, rev):
    nb, l, _ = pg3.shape
    nbb = SCAN_BATCHES
    pos = functools.partial(_scan_pos_static, rev, ncb=rows.ncb, ntb=rows.ntb)
    return pl.pallas_call(
        functools.partial(_gla_kernel, rev),
        out_shape=jax.ShapeDtypeStruct((nb, l, HP), BF16),
        grid=(rows.ntb, nb // nbb),
        in_specs=[pl.BlockSpec((nbb, SEG, NG), lambda s, g: (g, pos(s), 0)),
                  pl.BlockSpec((LANE, GQ), lambda s, g: (0, 0)),
                  pl.BlockSpec((1, GQ), lambda s, g: (0, 0))],
        out_specs=pl.BlockSpec((nbb, SEG, HP), lambda s, g: (g, pos(s), 0)),
        scratch_shapes=[pltpu.VMEM((nb, GLA_HEADS, LANE, LANE), F32)],
        compiler_params=_cparams(("arbitrary", "arbitrary")),
        name="gla_scan_bwd" if rev else "gla_scan_fwd",
    )(pg3, wa, ba)


def _mlstm_kernel_old(qk_ref, v_ref, g_ref, gt_ref, gbr_ref, gbc_ref, o_ref, st_ref, m_ref):
    d, s, g = pl.program_id(0), pl.program_id(1), pl.program_id(2)
    nbb = qk_ref.shape[0]
    b0 = g * nbb

    @pl.when(s == 0)
    def _():
        st_ref[pl.ds(b0, nbb)] = jnp.zeros((nbb,) + st_ref.shape[1:], F32)
        m_ref[pl.ds(b0, nbb)] = jnp.zeros((nbb,) + m_ref.shape[1:], F32)

    fwd = d == 0
    valid = _tri(d)
    tri = valid.astype(F32)
    r0s = _chunk_rows(d)
    cidx = [jnp.where(fwd, j, N_SUB - 1 - j) for j in range(N_SUB)]
    lane = lax.broadcasted_iota(jnp.int32, (CHUNK, LANE), 1)
    inst = [(bb, j) for bb in range(nbb) for j in range(N_SUB)]
    hinst = [(bb, j, h) for bb, j in inst for h in range(ML_HEADS)]
    gbr, gbc = gbr_ref[...], gbc_ref[...]

    def pick(a, h, kind, axis):
        i0, i1 = kind * ML_HEADS + h, (2 + kind) * ML_HEADS + h
        if axis == 1:
            return jnp.where(fwd, a[:, i0:i0 + 1], a[:, i1:i1 + 1])
        return jnp.where(fwd, a[i0:i0 + 1, :], a[i1:i1 + 1, :])

    gc, gr, fcum_c, fcum_r = {}, {}, {}, {}
    for bb, j in inst:
        i = (bb, j)
        gc[i] = g_ref[bb, pl.ds(r0s[j], CHUNK), :] + gbr
        gr[i] = gt_ref[bb, cidx[j]] + gbc
    for i in inst:
        fcum_c[i] = _dot(tri, _log_sigmoid(gc[i]), precision=HIGHEST)
        fcum_r[i] = _dot_nt(_log_sigmoid(gr[i]), tri, precision=HIGHEST)
    fc, lic, d_log, rmax = {}, {}, {}, {}
    for bb, j, h in hinst:
        i = (bb, j)
        fc[bb, j, h] = pick(fcum_c[i], h, 1, 1)
        lic[bb, j, h] = pick(gc[i], h, 0, 1)
        dl = jnp.where(valid, fc[bb, j, h] - pick(fcum_r[i], h, 1, 0) + pick(gr[i], h, 0, 0), NEG)
        d_log[bb, j, h] = dl
        rmax[bb, j, h] = jnp.max(dl, axis=-1, keepdims=True)
    m_prev, m_t, m_new, f_tot = {}, {}, {}, {}
    for bb in range(nbb):
        for h in range(ML_HEADS):
            m = m_ref[b0 + bb, h:h + 1, 0:1]
            for j in range(N_SUB):
                i = (bb, j, h)
                m_prev[i] = m
                m_t[i] = jnp.maximum(fc[i] + m, rmax[i])
                m = jnp.where(fwd, m_t[i][CHUNK - 1:CHUNK], m_t[i][0:1])
                m_new[i] = m
                f_tot[i] = jnp.where(fwd, fc[i][CHUNK - 1:CHUNK], fc[i][0:1])
            m_ref[b0 + bb, h:h + 1, :] = jnp.broadcast_to(m, (1, LANE))
    q, v, qk, w_prev, gdec, ds = {}, {}, {}, {}, {}, {}
    for bb, j, h in hinst:
        i = (bb, j, h)
        rs = pl.ds(r0s[j], CHUNK)
        sl = slice(h * LANE, (h + 1) * LANE)
        q[i] = qk_ref[bb, rs, sl].astype(BF16)
        k = qk_ref[bb, rs, HP + h * LANE:HP + (h + 1) * LANE]
        v[i] = jnp.where(lane == ML_DH, 1.0, v_ref[bb, rs, sl]).astype(BF16)
        w = jnp.exp(d_log[i] - m_t[i])
        w_prev[i] = jnp.exp(fc[i] + m_prev[i] - m_t[i])
        qk[i] = (_dot_nt(q[i], k.astype(BF16)) * w).astype(BF16)
        w_s = jnp.exp(f_tot[i] - fc[i] + lic[i] - m_new[i])
        gdec[i] = jnp.exp(f_tot[i] + m_prev[i] - m_new[i])
        ds[i] = _dot_tn(v[i], (k * w_s).astype(BF16))
    s_in = {}
    for bb in range(nbb):
        for h in range(ML_HEADS):
            st = st_ref[b0 + bb, h]
            for j in range(N_SUB):
                i = (bb, j, h)
                s_in[i] = st.astype(BF16)
                st = gdec[i] * st + ds[i]
            st_ref[b0 + bb, h] = st
    for bb, j, h in hinst:
        i = (bb, j, h)
        num = w_prev[i] * _dot_nt(q[i], s_in[i]) + _dot(qk[i], v[i])
        den = num[:, ML_DH:ML_DH + 1]
        hh = num / jnp.maximum(jnp.abs(den), jnp.exp(-m_t[i]))
        o_ref[0, bb, pl.ds(r0s[j], CHUNK), h * LANE:(h + 1) * LANE] = jnp.where(lane < ML_DH, hh, 0.0)


def _mlstm_old(qk3, pm3, gt4, gbr, gbc, *, rows):
    nb, l, _ = qk3.shape
    nbb = SCAN_BATCHES
    pos = functools.partial(_scan_pos, ncb=rows.ncb, ntb=rows.ntb)
    return pl.pallas_call(
        _mlstm_kernel,
        out_shape=jax.ShapeDtypeStruct((2, nb, l, HP), F32),
        grid=(2, rows.ntb, nb // nbb),
        in_specs=[pl.BlockSpec((nbb, SEG, 2 * HP), lambda d, s, g: (g, pos(d, s), 0)),
                  pl.BlockSpec((nbb, SEG, HP), lambda d, s, g: (g, pos(d, s), 2)),
                  pl.BlockSpec((nbb, SEG, LANE), lambda d, s, g: (g, pos(d, s), 4 * GLA_HEADS)),
                  pl.BlockSpec((nbb, N_SUB, 16, CHUNK), lambda d, s, g: (g, pos(d, s), 0, 0)),
                  pl.BlockSpec((1, LANE), lambda d, s, g: (0, 0)),
                  pl.BlockSpec((16, CHUNK), lambda d, s, g: (0, 0))],
        out_specs=pl.BlockSpec((1, nbb, SEG, HP), lambda d, s, g: (d, g, pos(d, s), 0)),
        scratch_shapes=[pltpu.VMEM((nb, ML_HEADS, LANE, LANE), F32), pltpu.VMEM((nb, 8, LANE), F32)],
        compiler_params=_cparams(("arbitrary", "arbitrary", "arbitrary")),
        name="mlstm_scan",
    )(qk3, pm3, pm3, gt4, gbr, gbc)


def _split_bf16(x, n):
    parts, r = [], x
    for _ in range(n):
        p = r.astype(BF16)
        parts.append(p)
        r = r - p.astype(F32)
    return parts


def _dot_exact01(a, b, lhs_is_01, pieces=3):
    if lhs_is_01:
        a = a.astype(BF16)
        terms = [_dot(a, p) for p in _split_bf16(b, pieces)]
    else:
        b = b.astype(BF16)
        terms = [_dot(p, b) for p in _split_bf16(a, pieces)]
    return functools.reduce(lambda x, y: x + y, terms)


def _cummax_rows(a, rev):
    n = a.shape[0]
    row = lax.broadcasted_iota(jnp.int32, a.shape, 0)
    k = 1
    while k < n:
        if rev:
            sh = jnp.where(row < n - k, pltpu.roll(a, n - k, axis=0), NEG)
        else:
            sh = jnp.where(row >= k, pltpu.roll(a, k, axis=0), NEG)
        a = jnp.maximum(a, sh)
        k *= 2
    return a


ML_GL = ML_HEADS


def _mlstm_kernel(rev, qk_ref, v_ref, g_ref, gt_ref, gbr_ref, gbt_ref, o_ref, st_ref, m_ref):
    s, g = pl.program_id(0), pl.program_id(1)
    nbb = qk_ref.shape[0]
    b0 = g * nbb

    @pl.when(s == 0)
    def _():
        st_ref[pl.ds(b0, nbb)] = jnp.zeros((nbb,) + st_ref.shape[1:], F32)
        m_ref[pl.ds(b0, nbb)] = jnp.zeros((nbb,) + m_ref.shape[1:], F32)

    valid = _tri(rev)
    tri = valid.astype(F32)
    r0s = _chunk_rows(rev)
    cs = [r // CHUNK for r in r0s]
    last = 0 if rev else CHUNK - 1
    inst = [(bb, j) for bb in range(nbb) for j in range(N_SUB)]
    heads = [slice(h * LANE, (h + 1) * LANE) for h in range(ML_HEADS)]

    r_sel = lax.broadcasted_iota(jnp.int32, (LANE, HP), 0)
    c_sel = lax.broadcasted_iota(jnp.int32, (LANE, HP), 1)
    sel_h = (r_sel == ML_GL + c_sel // LANE).astype(BF16)
    r_t = lax.broadcasted_iota(jnp.int32, (HP, HP), 0)
    c_t = lax.broadcasted_iota(jnp.int32, (HP, HP), 1)
    same = jnp.logical_and(r_t // LANE == c_t // LANE, jnp.logical_and(r_t % LANE < CHUNK, c_t % LANE < CHUNK))
    before = (r_t % LANE >= c_t % LANE) if rev else (r_t % LANE <= c_t % LANE)
    tri_b = jnp.logical_and(same, before).astype(BF16)
    r_v = lax.broadcasted_iota(jnp.int32, (CHUNK, HP), 0)
    c_v = lax.broadcasted_iota(jnp.int32, (CHUNK, HP), 1) % LANE
    valid4 = jnp.logical_and(c_v < CHUNK, (r_v <= c_v) if rev else (r_v >= c_v))
    lane4 = lax.broadcasted_iota(jnp.int32, (CHUNK, HP), 1) % LANE
    lane_c = lax.broadcasted_iota(jnp.int32, (CHUNK, LANE), 1)
    lane1 = lax.broadcasted_iota(jnp.int32, (1, LANE), 1)
    head_lane = jnp.logical_and(lane1 >= ML_GL, lane1 < ML_GL + ML_HEADS)
    gbr, gbt = gbr_ref[...], gbt_ref[...]

    gcs, fcm, cmx, a_row, grt = {}, {}, {}, {}, {}
    for bb, j in inst:
        gc = g_ref[bb, pl.ds(r0s[j], CHUNK), :] + gbr
        gcs[bb, j] = pltpu.roll(gc, ML_GL, axis=1)
        fcm[bb, j] = _dot_exact01(tri, _log_sigmoid(gc), lhs_is_01=True)
        grt[bb, j] = gt_ref[bb, cs[j]] + gbt
    row_id = lax.broadcasted_iota(jnp.int32, (len(inst), HP), 0)
    lfr = jnp.zeros((len(inst), HP), F32)
    for n, i in enumerate(inst):
        lfr = jnp.where(row_id == n, _log_sigmoid(grt[i][1:2]), lfr)
    fcr = _dot_exact01(lfr, tri_b, lhs_is_01=False)
    for n, i in enumerate(inst):
        a_row[i] = grt[i][0:1] - fcr[n:n + 1]
        cmx[i] = _cummax_rows(gcs[i] - fcm[i], rev)
    bx = {}
    e_neg, gd = {}, {}
    for bb in range(nbb):
        m_prev = m_ref[b0 + bb, 0:1, :]
        for j in range(N_SUB):
            i = (bb, j)
            m_t = fcm[i] + jnp.maximum(m_prev, cmx[i])
            m_new = m_t[last:last + 1]
            f_tot = fcm[i][last:last + 1]
            u = fcm[i] - m_t
            w_prev = jnp.exp(u + m_prev)
            w_s = jnp.exp(f_tot - fcm[i] + gcs[i] - m_new)
            gdec = jnp.broadcast_to(jnp.exp(f_tot + m_prev - m_new), (16, LANE))
            e_neg[i] = jnp.exp(-m_t)
            keep = lambda a: jnp.where(head_lane, a, 0.0)
            bx[i] = jnp.concatenate(_split_bf16(keep(u), 2) + _split_bf16(keep(w_prev), 1)
                                    + _split_bf16(keep(w_s), 1) + _split_bf16(keep(gdec), 2), axis=0)
            m_prev = m_new
        m_ref[b0 + bb] = jnp.broadcast_to(m_prev, (8, LANE))
    ub, wpb, wsb, gdb = {}, {}, {}, {}
    for i in inst:
        y = _dot(bx[i], sel_h)
        c = CHUNK
        ub[i] = y[0:c] + y[c:2 * c]
        wpb[i] = y[2 * c:3 * c]
        wsb[i] = y[3 * c:4 * c]
        gdb[i] = y[4 * c:4 * c + 1] + y[4 * c + 16:4 * c + 17]
    q, v, qkw, ds = {}, {}, {}, {}
    for bb, j in inst:
        i = (bb, j)
        rs = pl.ds(r0s[j], CHUNK)
        w = jnp.where(valid4, jnp.exp(ub[i] + a_row[i]), 0.0)
        q[i] = qk_ref[bb, rs, 0:HP].astype(BF16)
        k = qk_ref[bb, rs, HP:2 * HP]
        kb = k.astype(BF16)
        kw = (k * wsb[i]).astype(BF16)
        v[i] = jnp.where(lane4 == ML_DH, 1.0, v_ref[bb, rs, :]).astype(BF16)
        for h, sl in enumerate(heads):
            sc = _dot_nt(q[i][:, sl], kb[:, sl])
            qkw[i, h] = (sc * w[:, h * LANE:h * LANE + CHUNK]).astype(BF16)
            ds[i, h] = _dot_tn(v[i][:, sl], kw[:, sl])
    s_in = {}
    for bb in range(nbb):
        for h, sl in enumerate(heads):
            st = st_ref[b0 + bb, h]
            for j in range(N_SUB):
                i = (bb, j)
                s_in[i, h] = st.astype(BF16)
                st = gdb[i][:, sl] * st + ds[i, h]
            st_ref[b0 + bb, h] = st
    num = {}
    for i in inst:
        parts = [_dot_nt(q[i][:, sl], s_in[i, h]) for h, sl in enumerate(heads)]
        intra = [_dot(qkw[i, h], v[i][:, sl]) for h, sl in enumerate(heads)]
        num[i] = wpb[i] * jnp.concatenate(parts, axis=1) + jnp.concatenate(intra, axis=1)
    for bb, j in inst:
        i = (bb, j)
        den = jnp.zeros((CHUNK, LANE), F32)
        for h, sl in enumerate(heads):
            dh = jnp.sum(jnp.where(lane_c == ML_DH, num[i][:, sl], 0.0), axis=-1, keepdims=True)
            den = jnp.where(lane_c == ML_GL + h, jnp.broadcast_to(dh, (CHUNK, LANE)), den)
        r1, r2 = _split_bf16(jnp.where(head_lane, 1.0 / jnp.maximum(jnp.abs(den), e_neg[i]), 0.0), 2)
        rb = _dot(jnp.concatenate([r1, r2], axis=0), sel_h)
        rb = rb[0:CHUNK] + rb[CHUNK:2 * CHUNK]
        o_ref[bb, pl.ds(r0s[j], CHUNK), :] = jnp.where(lane4 < ML_DH, num[i] * rb, 0.0).astype(o_ref.dtype)


def _mlstm(qk3, pm3, gtl, gbr, gbt, *, rows, rev):
    nb, l, _ = qk3.shape
    nbb = SCAN_BATCHES
    dr = 1 if rev else 0
    pos = functools.partial(_scan_pos_static, rev, ncb=rows.ncb, ntb=rows.ntb)
    return pl.pallas_call(
        functools.partial(_mlstm_kernel, rev),
        out_shape=jax.ShapeDtypeStruct((nb, l, HP), BF16),
        grid=(rows.ntb, nb // nbb),
        in_specs=[pl.BlockSpec((nbb, SEG, 2 * HP), lambda s, g: (g, pos(s), 0)),
                  pl.BlockSpec((nbb, SEG, HP), lambda s, g: (g, pos(s), 0)),
                  pl.BlockSpec((nbb, SEG, LANE), lambda s, g: (g, pos(s), 2 * ML_HEADS + dr)),
                  pl.BlockSpec((nbb, N_SUB, None, 2, HP), lambda s, g: (g, pos(s), dr, 0, 0)),
                  pl.BlockSpec((1, LANE), lambda s, g: (0, 0)),
                  pl.BlockSpec((2, HP), lambda s, g: (0, 0))],
        out_specs=pl.BlockSpec((nbb, SEG, HP), lambda s, g: (g, pos(s), 0)),
        scratch_shapes=[pltpu.VMEM((nb, ML_HEADS, LANE, LANE), F32), pltpu.VMEM((nb, 8, LANE), F32)],
        compiler_params=_cparams(("arbitrary", "arbitrary")),
        name="mlstm_scan_bwd" if rev else "mlstm_scan_fwd",
    )(qk3, pm3, pm3, gtl, gbr, gbt)


def _s5_kernel(nb, u_ref, bre_ref, bim_ref, are_ref, aim_ref, cre_ref, cim_ref, o_ref, xr_ref, xi_ref, st_ref):
    d, s = pl.program_id(0), pl.program_id(1)

    @pl.when(s == 0)
    def _():
        st_ref[...] = jnp.zeros(st_ref.shape, F32)

    n_sub = u_ref.shape[0] // (S5_SUB * nb)
    sub_rows = S5_SUB * nb

    def run(rev):
        ar, ai = are_ref[0], aim_ref[0]
        order = list(range(n_sub))[::-1] if rev else list(range(n_sub))

        def project_in(q):
            rs = slice(q * sub_rows, (q + 1) * sub_rows)
            u = u_ref[rs, :].astype(BF16)
            xr_ref[rs, :] = _dot(u, bre_ref[0])
            xi_ref[rs, :] = _dot(u, bim_ref[0])

        sr, si = st_ref[0], st_ref[1]
        project_in(order[0])
        for n, q in enumerate(order):
            if n + 1 < n_sub:
                project_in(order[n + 1])
            for j in range(S5_SUB):
                r0 = (q * S5_SUB + (S5_SUB - 1 - j if rev else j)) * nb
                nr = ar * sr - ai * si + xr_ref[r0:r0 + nb, :]
                ni = ar * si + ai * sr + xi_ref[r0:r0 + nb, :]
                xr_ref[r0:r0 + nb, :] = nr
                xi_ref[r0:r0 + nb, :] = ni
                sr, si = nr, ni
            rs = slice(q * sub_rows, (q + 1) * sub_rows)
            o_ref[0, rs, :] = (_dot(xr_ref[rs, :].astype(BF16), cre_ref[...])
                               - _dot(xi_ref[rs, :].astype(BF16), cim_ref[...]))
        st_ref[0] = sr
        st_ref[1] = si

    @pl.when(d == 0)
    def _():
        run(False)

    @pl.when(d == 1)
    def _():
        run(True)


def _s5(ut, bre, bim, are, aim, cre, cim, *, nb, nc, nt):
    n_rows, ch = ut.shape
    tr = S5_STEPS * nb
    ns = bre.shape[-1]
    pos = functools.partial(_scan_pos, ncb=nc, ntb=nt)
    return pl.pallas_call(
        functools.partial(_s5_kernel, nb),
        out_shape=jax.ShapeDtypeStruct((2, n_rows, ch), F32),
        grid=(2, nt),
        in_specs=[pl.BlockSpec((tr, ch), lambda d, s: (pos(d, s), 0)),
                  pl.BlockSpec((1, ch, ns), lambda d, s: (d, 0, 0)),
                  pl.BlockSpec((1, ch, ns), lambda d, s: (d, 0, 0)),
                  pl.BlockSpec((1, nb, ns), lambda d, s: (d, 0, 0)),
                  pl.BlockSpec((1, nb, ns), lambda d, s: (d, 0, 0)),
                  pl.BlockSpec((ns, ch), lambda d, s: (0, 0)),
                  pl.BlockSpec((ns, ch), lambda d, s: (0, 0))],
        out_specs=pl.BlockSpec((1, tr, ch), lambda d, s: (d, pos(d, s), 0)),
        scratch_shapes=[pltpu.VMEM((tr, ns), F32), pltpu.VMEM((tr, ns), F32), pltpu.VMEM((2, nb, ns), F32)],
        compiler_params=_cparams(("arbitrary", "arbitrary")),
        name="s5_scan",
    )(ut, bre, bim, are, aim, cre, cim)


def _head_norm(o, gain, dim):
    parts = []
    for h in range(o.shape[1] // LANE):
        seg = o[:, h * LANE:(h + 1) * LANE]
        ms = jnp.sum(seg * seg, axis=-1, keepdims=True) * (1.0 / dim)
        parts.append(seg * lax.rsqrt(ms + EPS))
    return jnp.concatenate(parts, axis=1) * gain


def _mix_kernel(with_router, ogf_ref, ogb_ref, gg_ref, ys_ref, u_ref, omf_ref, omb_ref, mo_ref, h_ref, mod_ref,
                gn_ref, mn_ref, sd_ref, gw_ref, gb_ref, wo_ref, n2_ref, *rest):
    if with_router:
        wr_ref, ho_ref, f_ref, rt_ref = rest
    else:
        ho_ref, f_ref = rest
    gla = _head_norm(ogf_ref[...].astype(F32) + ogb_ref[...].astype(F32), gn_ref[...], GLA_DV) * _silu(gg_ref[...])
    z = _gelu_tanh(ys_ref[0] + ys_ref[1] + sd_ref[...] * u_ref[...])
    s5 = z * jax.nn.sigmoid(_dot(z.astype(BF16), gw_ref[...]) + gb_ref[...])
    ml = _head_norm(omf_ref[...].astype(F32) + omb_ref[...].astype(F32), mn_ref[...], ML_DH) * jax.nn.sigmoid(mo_ref[...])
    mix = (_dot(gla.astype(BF16), wo_ref[0:HP]) + _dot(s5.astype(BF16), wo_ref[HP:HP + 2 * LANE])
           + _dot(ml.astype(BF16), wo_ref[HP + 2 * LANE:]))
    m = mod_ref[0]
    hn = h_ref[...] + m[2:3] * mix
    ho_ref[...] = hn
    f = _rmsnorm(hn, n2_ref[...]) * (1.0 + m[4:5]) + m[3:4]
    f_ref[...] = f.astype(f_ref.dtype)
    if with_router:
        f_hi, f_lo = _split_bf16(f, 2)
        logits = (_dot(f_hi, wr_ref[0]) + _dot(f_lo, wr_ref[0]) + _dot(f_hi, wr_ref[1])).T[0:2 * N_EXPERTS]
        row = lax.broadcasted_iota(jnp.int32, logits.shape, 0)
        l0 = jnp.where(row < N_EXPERTS, logits, NEG)
        m1 = jnp.max(l0, axis=0, keepdims=True)
        i1 = jnp.min(jnp.where(l0 == m1, row, 2 * N_EXPERTS), axis=0, keepdims=True)
        l1 = jnp.where(row == i1, NEG, l0)
        m2 = jnp.max(l1, axis=0, keepdims=True)
        i2 = jnp.min(jnp.where(l1 == m2, row, 2 * N_EXPERTS), axis=0, keepdims=True)
        e = jnp.exp(m2 - m1)
        w1 = 1.0 / (1.0 + e)
        w2 = e / (1.0 + e)
        rt_ref[...] = jnp.where(row == 0, i1.astype(F32),
                                jnp.where(row == 1, i2.astype(F32),
                                          jnp.where(row == 2, w1, jnp.where(row == 3, w2, 0.0))))


def _mix(ogf, ogb, pg, ys, pu, omf, omb, pm, h, modtab, gn, mn, sd, gw, gb, wo, n2, wr, *, layer, rows, lat_only):
    d = h.shape[1]
    tm = SEG
    n = rows.n_blocks(lat_only)
    src, sel = rows.src(lat_only), rows.sel(lat_only)
    full = lambda a: pl.BlockSpec(a.shape, lambda i: (0,) * a.ndim)
    with_router = wr is not None
    in_specs = [pl.BlockSpec((tm, HP), lambda i: (src(i), 0)),
                pl.BlockSpec((tm, HP), lambda i: (src(i), 0)),
                pl.BlockSpec((tm, HP), lambda i: (src(i), (2 * GQ + HP) // HP)),
                pl.BlockSpec((2, tm, 2 * LANE), lambda i: (0, src(i), 0)),
                pl.BlockSpec((tm, 2 * LANE), lambda i: (src(i), 0)),
                pl.BlockSpec((tm, HP), lambda i: (src(i), 0)),
                pl.BlockSpec((tm, HP), lambda i: (src(i), 0)),
                pl.BlockSpec((tm, HP), lambda i: (src(i), 1)),
                pl.BlockSpec((tm, d), lambda i: (src(i), 0)),
                pl.BlockSpec((None, 1, 8, d), lambda i: (layer, sel(i), 0, 0)),
                full(gn), full(mn), full(sd), full(gw), full(gb), full(wo), full(n2)]
    args = [ogf, ogb, pg, ys, pu, omf, omb, pm, h, modtab, gn, mn, sd, gw, gb, wo, n2]
    out_shape = [jax.ShapeDtypeStruct((n * tm, d), F32), jax.ShapeDtypeStruct((n * tm, d), F32 if with_router else BF16)]
    out_specs = [pl.BlockSpec((tm, d), lambda i: (i, 0)), pl.BlockSpec((tm, d), lambda i: (i, 0))]
    if with_router:
        in_specs.append(full(wr))
        args.append(wr)
        out_shape.append(jax.ShapeDtypeStruct((2 * N_EXPERTS, n * tm), F32))
        out_specs.append(pl.BlockSpec((2 * N_EXPERTS, tm), lambda i: (0, i)))
    return pl.pallas_call(
        functools.partial(_mix_kernel, with_router),
        out_shape=tuple(out_shape),
        grid=(n,),
        in_specs=in_specs,
        out_specs=tuple(out_specs),
        compiler_params=_cparams(("arbitrary",)),
        name="mix_out",
    )(*args)


FF_TILE = 256


def _swiglu(xb, w1_ref, w3_ref, w2_ref, a_ref, lead=()):
    dff = w1_ref.shape[-1]
    for j in range(dff // FF_TILE):
        sl = slice(j * FF_TILE, (j + 1) * FF_TILE)
        h1 = _dot(xb, w1_ref[lead + (slice(None), sl)])
        h3 = _dot(xb, w3_ref[lead + (slice(None), sl)])
        a_ref[:, sl] = (_silu(h1) * h3).astype(BF16)
    return _dot(a_ref[...], w2_ref[lead + (slice(None), slice(None))])


FFN_TM = 1024


def _ffn_kernel(final, sel, f_ref, h_ref, mod_ref, w1_ref, w3_ref, w2_ref, *rest):
    if final:
        nf_ref, o_ref, a_ref = rest
    else:
        o_ref, a_ref = rest
    y = _swiglu(f_ref[...], w1_ref, w3_ref, w2_ref, a_ref)
    n_seg = f_ref.shape[0] // SEG
    for q in range(n_seg):
        rs = slice(q * SEG, (q + 1) * SEG)
        gate = mod_ref[sel(pl.program_id(0) * n_seg + q)][5:6]
        hn = h_ref[rs, :] + gate * y[rs]
        o_ref[rs, :] = _rmsnorm(hn, nf_ref[...]) if final else hn


def _ffn(f, h, modtab, w1, w3, w2, nf, *, layer, rows, lat_only):
    r, d = h.shape
    tm = FFN_TM
    assert r % tm == 0
    dff = w1.shape[-1]
    full = lambda a: pl.BlockSpec(a.shape, lambda i: (0,) * a.ndim)
    resident = lambda a: pl.BlockSpec(a.shape, lambda i: (0,) * a.ndim, pipeline_mode=pl.Buffered(1))
    in_specs = [pl.BlockSpec((tm, d), lambda i: (i, 0)),
                pl.BlockSpec((tm, d), lambda i: (i, 0)),
                pl.BlockSpec((None,) + modtab.shape[1:], lambda i: (layer, 0, 0, 0)),
                resident(w1), resident(w3), resident(w2)]
    args = [f, h, modtab, w1, w3, w2]
    if lat_only:
        in_specs.append(full(nf))
        args.append(nf)
    return pl.pallas_call(
        functools.partial(_ffn_kernel, lat_only, rows.sel(lat_only)),
        out_shape=jax.ShapeDtypeStruct((r, d), F32),
        grid=(r // tm,),
        in_specs=in_specs,
        out_specs=pl.BlockSpec((tm, d), lambda i: (i, 0)),
        scratch_shapes=[pltpu.VMEM((tm, dff), BF16)],
        compiler_params=_cparams(("arbitrary",)),
        name="ffn",
    )(*args)


MOE_TM = 512


def _moe_kernel(be_ref, live_ref, x_ref, rw_ref, w1_ref, w3_ref, w2_ref, o_ref, a_ref):
    i = pl.program_id(0)

    @pl.when(live_ref[i] > 0)
    def _():
        y = _swiglu(x_ref[...].astype(BF16), w1_ref, w3_ref, w2_ref, a_ref, lead=(0,))
        o_ref[...] = y * rw_ref[:, 0:1]

    @pl.when(live_ref[i] == 0)
    def _():
        o_ref[...] = jnp.zeros(o_ref.shape, F32)


def _moe_experts(block_expert, block_live, xg, rw, w1, w3, w2):
    n_rows, d = xg.shape
    dff = w1.shape[-1]
    tm = MOE_TM
    return pl.pallas_call(
        _moe_kernel,
        out_shape=jax.ShapeDtypeStruct((n_rows, d), F32),
        grid_spec=pltpu.PrefetchScalarGridSpec(
            num_scalar_prefetch=2,
            grid=(n_rows // tm,),
            in_specs=[pl.BlockSpec((tm, d), lambda i, be, lv: (i, 0)),
                      pl.BlockSpec((tm, LANE), lambda i, be, lv: (i, 0)),
                      pl.BlockSpec((1, d, dff), lambda i, be, lv: (be[i], 0, 0)),
                      pl.BlockSpec((1, d, dff), lambda i, be, lv: (be[i], 0, 0)),
                      pl.BlockSpec((1, dff, d), lambda i, be, lv: (be[i], 0, 0))],
            out_specs=pl.BlockSpec((tm, d), lambda i, be, lv: (i, 0)),
            scratch_shapes=[pltpu.VMEM((tm, dff), BF16)]),
        compiler_params=_cparams(("arbitrary",)),
        name="moe_experts",
    )(block_expert, block_live, xg, rw, w1, w3, w2)


SC_GATHER_ROWS = 64


def _gather_rows(table, idx):
    n_idx = idx.shape[0]
    _, d = table.shape
    info = plsc.get_sparse_core_info()
    n_cores, n_workers = info.num_cores, info.num_cores * info.num_subcores
    assert n_idx % (n_workers * SC_GATHER_ROWS) == 0
    per_worker = n_idx // n_workers
    mesh = plsc.VectorSubcoreMesh(core_axis_name="c", subcore_axis_name="s")

    @functools.partial(
        pl.kernel, mesh=mesh,
        out_type=jax.ShapeDtypeStruct((n_idx, d), table.dtype),
        scratch_types=[pltpu.VMEM((SC_GATHER_ROWS,), jnp.int32),
                       pltpu.VMEM((SC_GATHER_ROWS, d), table.dtype),
                       pltpu.SemaphoreType.DMA])
    def gather(table_hbm, idx_hbm, out_hbm, idx_v, rows_v, sem):
        base = (lax.axis_index("s") * n_cores + lax.axis_index("c")) * per_worker

        @pl.loop(0, per_worker // SC_GATHER_ROWS)
        def _(it):
            off = pl.multiple_of(base + it * SC_GATHER_ROWS, SC_GATHER_ROWS)
            pltpu.sync_copy(idx_hbm.at[pl.ds(off, SC_GATHER_ROWS)], idx_v)
            pltpu.async_copy(table_hbm.at[idx_v], rows_v, sem).wait()
            pltpu.sync_copy(rows_v, out_hbm.at[pl.ds(off, SC_GATHER_ROWS)])

    return gather(table, idx)


def _resid_kernel(final, h_ref, y0_ref, y1_ref, mod_ref, *rest):
    hn = h_ref[...] + mod_ref[0][5:6] * (y0_ref[...] + y1_ref[...])
    if final:
        nf_ref, o_ref = rest
        o_ref[...] = _rmsnorm(hn, nf_ref[...])
    else:
        (o_ref,) = rest
        o_ref[...] = hn


def _moe_resid(h, y0, y1, modtab, nf, *, layer, rows, lat_only):
    r, d = h.shape
    tm = SEG
    sel = rows.sel(lat_only)
    row = lambda i: (i, 0)
    in_specs = [pl.BlockSpec((tm, d), row), pl.BlockSpec((tm, d), row), pl.BlockSpec((tm, d), row),
                pl.BlockSpec((None, 1, 8, d), lambda i: (layer, sel(i), 0, 0))]
    args = [h, y0, y1, modtab]
    if lat_only:
        in_specs.append(pl.BlockSpec((1, d), lambda i: (0, 0)))
        args.append(nf)
    return pl.pallas_call(
        functools.partial(_resid_kernel, lat_only),
        out_shape=jax.ShapeDtypeStruct((r, d), F32),
        grid=(r // tm,),
        in_specs=in_specs,
        out_specs=pl.BlockSpec((tm, d), row),
        compiler_params=_cparams(("arbitrary",)),
        name="moe_resid",
    )(*args)


CAST_PARTS = 4


def _cast_kernel(*refs):
    o_ref = refs[-1]
    tr = refs[0].shape[1]
    for k, x_ref in enumerate(refs[:-1]):
        o_ref[0, k * tr:(k + 1) * tr, :] = x_ref[0].astype(o_ref.dtype)


def _to_bf16(w, j):
    lead, (r, c) = w.shape[1:-2], w.shape[-2:]
    n = int(np.prod(lead, dtype=np.int64))
    w3 = w.reshape((-1, r, c))
    tr = r // CAST_PARTS
    assert tr * CAST_PARTS == r and tr % 16 == 0
    band = lambda k: pl.BlockSpec((1, tr, c), lambda e: (j * n + e, k, 0))
    out = pl.pallas_call(
        _cast_kernel,
        out_shape=jax.ShapeDtypeStruct((n, r, c), BF16),
        grid=(n,),
        in_specs=[band(k) for k in range(CAST_PARTS)],
        out_specs=pl.BlockSpec((1, r, c), lambda e: (e, 0, 0)),
        compiler_params=_cparams(("arbitrary",)),
        name="to_bf16",
    )(*([w3] * CAST_PARTS))
    return out.reshape(lead + (r, c))


def _pad_heads(w, heads, dim, to=LANE):
    lead = w.shape[:-1]
    w = w.reshape(lead + (heads, dim))
    w = jnp.pad(w, [(0, 0)] * len(lead) + [(0, 0), (0, to - dim)])
    return w.reshape(lead + (heads * to,))


def _pad_last(w, to):
    return jnp.pad(w, [(0, 0)] * (w.ndim - 1) + [(0, to - w.shape[-1])])


def _pos_embed(n_tokens, d):
    n_grid_rows = n_tokens // GRID_W
    row, col = jnp.meshgrid(jnp.arange(n_grid_rows, dtype=F32), jnp.arange(GRID_W, dtype=F32), indexing='ij')
    n_freq = d // 4
    omega = jnp.exp(-math.log(POS_BASE) * jnp.arange(n_freq, dtype=F32) / n_freq)

    def axis_embed(p):
        ang = p.reshape(-1, 1) * omega
        return jnp.concatenate([jnp.sin(ang), jnp.cos(ang)], axis=-1)

    return jnp.concatenate([axis_embed(row), axis_embed(col)], axis=-1)


def _s5_discretise(lam_re, lam_im, log_dt, b_re, b_im):
    dt = jnp.exp(log_dt)[:, None]
    mag = jnp.exp(lam_re * dt)
    abar_re, abar_im = mag * jnp.cos(lam_im * dt), mag * jnp.sin(lam_im * dt)
    den = lam_re * lam_re + lam_im * lam_im
    pr, pi = abar_re - 1.0, abar_im
    coef_re = (pr * lam_re + pi * lam_im) / den
    coef_im = (pi * lam_re - pr * lam_im) / den
    bbar_re = coef_re[..., None] * b_re - coef_im[..., None] * b_im
    bbar_im = coef_re[..., None] * b_im + coef_im[..., None] * b_re
    return abar_re, abar_im, bbar_re, bbar_im


def _block_diag(m):
    g, a, b = m.shape
    eye = jnp.eye(g, dtype=m.dtype)
    return (eye[:, None, :, None] * m[:, :, None, :]).reshape(g * a, g * b)


def _route_plan(route, tm):
    n_tok = route.shape[1]
    n_assign = n_tok * TOP_K
    flat_e = route[0:TOP_K].astype(jnp.int32).reshape(-1)
    flat_w = route[TOP_K:2 * TOP_K].reshape(-1)
    onehot = (jnp.arange(N_EXPERTS, dtype=jnp.int32)[:, None] == flat_e[None, :]).astype(jnp.int32)
    csum = jnp.cumsum(onehot, axis=1)
    counts = csum[:, -1]
    padded = (counts + tm - 1) // tm * tm
    pend = jnp.cumsum(padded)
    pstart = pend - padded
    dest = jnp.sum(onehot * (csum - 1 + pstart[:, None]), axis=0)
    n_blocks = -(-n_assign // tm) + N_EXPERTS
    n_rows = n_blocks * tm
    row_assign = jnp.full((n_rows,), -1, jnp.int32).at[dest].set(jnp.arange(n_assign, dtype=jnp.int32))
    live = row_assign >= 0
    ra = jnp.maximum(row_assign, 0)
    row_token = jnp.where(live, ra % n_tok, jnp.arange(n_rows, dtype=jnp.int32) % n_tok)
    row_w = jnp.where(live, flat_w[ra], 0.0)
    block_start = jnp.arange(n_blocks, dtype=jnp.int32) * tm
    block_expert = jnp.minimum(jnp.searchsorted(pend, block_start, side='right'), N_EXPERTS - 1).astype(jnp.int32)
    block_live = (block_start < (pstart + counts)[block_expert]).astype(jnp.int32)
    return row_token, row_w, block_expert, block_live, dest.reshape(TOP_K, n_tok)


def kernel(x, c, ctx, c_ctx, w_ada, b_ada, norm1, norm2, w_in, w_out, gla_wa2, gla_ba, gla_norm, s5_lam_re, s5_lam_im, s5_log_dt, s5_b_re, s5_b_im, s5_c_re, s5_c_im, s5_d, s5_glu_w, s5_glu_b, ml_conv_w, ml_conv_b, ml_gate_b, ml_norm, ffn_w1, ffn_w3, ffn_w2, moe_router, moe_w1, moe_w3, moe_w2, norm_f):
    nb, n_lat, d = x.shape
    lc = ctx.shape[1]
    depth = w_ada.shape[0]
    l = lc + n_lat
    assert lc % SEG == 0 and n_lat % SEG == 0 and nb == 8 and nb % SCAN_BATCHES == 0
    rows = _Rows(nb, lc // SEG, l // SEG)

    h = _embed(ctx.reshape(nb * lc, d), x.reshape(nb * n_lat, d), _pos_embed(n_lat, d), rows)

    cond = jnp.zeros((16, d), F32).at[:nb].set(c).at[nb].set(c_ctx)
    mod = _modulation(cond, w_ada, b_ada)
    modtab = jnp.pad(mod.reshape(depth, 16, 6, d), ((0, 0), (0, 0), (0, 2), (0, 0)))

    dk, dv, dh = GLA_HEADS * GLA_DK, GLA_HEADS * GLA_DV, ML_HEADS * ML_DH
    s5c = s5_d.shape[-1]
    cuts = np.cumsum([dk, dk, dv, GLA_RANK, dv, s5c, dh, dh, dh, dh, 4 * ML_HEADS])

    for i in range(depth):
        last = i == depth - 1
        gq, gk, gv, glr, gg, su, mq, mk, mv, mo, mg = jnp.split(w_in[i], cuts[:-1], axis=-1)
        w_all = jnp.concatenate([
            _pad_heads(gq, GLA_HEADS, GLA_DK, GLA_KP), _pad_heads(gk, GLA_HEADS, GLA_DK, GLA_KP),
            _pad_heads(gv, GLA_HEADS, GLA_DV), _pad_heads(gg, GLA_HEADS, GLA_DV), _pad_last(glr, LANE),
            su,
            _pad_heads(mv, ML_HEADS, ML_DH), _pad_heads(mo, ML_HEADS, ML_DH),
            _pad_last(mg[:, :2 * ML_HEADS], LANE), _pad_last(mg[:, 2 * ML_HEADS:], LANE),
            _pad_heads(mq, ML_HEADS, ML_DH), _pad_heads(mk, ML_HEADS, ML_DH)],
            axis=-1).astype(BF16)
        wgt = mg.T.astype(BF16)
        cw = jnp.concatenate([_pad_heads(ml_conv_w[i][:, :dh], ML_HEADS, ML_DH),
                              _pad_heads(ml_conv_w[i][:, dh:], ML_HEADS, ML_DH)], axis=-1)
        cw = jnp.pad(cw, ((0, 8 - ML_CONV), (0, 0)))
        cb = jnp.concatenate([_pad_heads(ml_conv_b[i][:dh], ML_HEADS, ML_DH),
                              _pad_heads(ml_conv_b[i][dh:], ML_HEADS, ML_DH)])[None]
        post = jnp.concatenate([jnp.ones((HP,), F32), jnp.full((HP,), ML_DH ** -0.5, F32)])[None]
        pg, pu, pm, qk, gt = _proj(h, modtab, norm1[i][None], w_all, wgt, cw, cb, post, layer=i, rows=rows)

        wa = jnp.pad(_pad_heads(gla_wa2[i], GLA_HEADS, GLA_DK, GLA_KP), ((0, 0), (0, LANE - GLA_RANK), (0, 0)))
        ba = _pad_heads(gla_ba[i], GLA_HEADS, GLA_DK, GLA_KP)[:, None, :]
        pg3 = pg.reshape(nb, l, NG)
        ogf = _gla(pg3, wa[0], ba[0], rows=rows, rev=False).reshape(nb * l, HP)
        ogb = _gla(pg3, wa[1], ba[1], rows=rows, rev=True).reshape(nb * l, HP)

        bres, bims, ares, aims = [], [], [], []
        for dr in (0, 1):
            a_re, a_im, b_re, b_im = _s5_discretise(s5_lam_re[i, dr], s5_lam_im[i, dr], s5_log_dt[i, dr],
                                                    s5_b_re[i], s5_b_im[i])
            bres.append(_block_diag(jnp.swapaxes(b_re, 1, 2)))
            bims.append(_block_diag(jnp.swapaxes(b_im, 1, 2)))
            ares.append(jnp.broadcast_to(a_re.reshape(1, -1), (nb, a_re.size)))
            aims.append(jnp.broadcast_to(a_im.reshape(1, -1), (nb, a_im.size)))
        cre = _block_diag(jnp.swapaxes(s5_c_re[i], 1, 2)).astype(BF16)
        cim = _block_diag(jnp.swapaxes(s5_c_im[i], 1, 2)).astype(BF16)
        ut = pu.reshape(nb, l, s5c).swapaxes(0, 1).reshape(l * nb, s5c)
        yt = _s5(ut, jnp.stack(bres).astype(BF16), jnp.stack(bims).astype(BF16), jnp.stack(ares), jnp.stack(aims),
                 cre, cim, nb=nb, nc=lc // S5_STEPS, nt=l // S5_STEPS)
        ys = yt.reshape(2, l, nb, s5c).swapaxes(1, 2).reshape(2, nb * l, s5c)

        gb = ml_gate_b[i].reshape(2, 2, ML_HEADS)
        gbr = _pad_last(gb.reshape(2, 1, 2 * ML_HEADS), LANE)
        gbt = _pad_last(jnp.broadcast_to(gb[..., None], (2, 2, ML_HEADS, CHUNK)), LANE).reshape(2, 2, HP)
        gtl = _pad_last(gt.reshape(2, 2, ML_HEADS, nb, l // CHUNK, CHUNK).transpose(3, 4, 0, 1, 2, 5),
                        LANE).reshape(nb, l // CHUNK, 2, 2, HP)
        qk3, pm3 = qk.reshape(nb, l, 2 * HP), pm.reshape(nb, l, NM)
        omf = _mlstm(qk3, pm3, gtl, gbr[0], gbt[0], rows=rows, rev=False).reshape(nb * l, HP)
        omb = _mlstm(qk3, pm3, gtl, gbr[1], gbt[1], rows=rows, rev=True).reshape(nb * l, HP)

        wo = w_out[i]
        wo_p = jnp.concatenate([
            jnp.pad(wo[:dv].reshape(GLA_HEADS, GLA_DV, d), ((0, 0), (0, LANE - GLA_DV), (0, 0))).reshape(HP, d),
            wo[dv:dv + s5c],
            jnp.pad(wo[dv + s5c:].reshape(ML_HEADS, ML_DH, d), ((0, 0), (0, LANE - ML_DH), (0, 0))).reshape(HP, d)],
            axis=0).astype(BF16)
        gn = jnp.tile(_pad_last(gla_norm[i], LANE), GLA_HEADS)[None]
        mn = jnp.tile(_pad_last(ml_norm[i], LANE), ML_HEADS)[None]
        is_moe = i % 2 == 1
        j = i // 2
        wr = jnp.stack(_split_bf16(_pad_last(moe_router[j], LANE), 2)) if is_moe else None
        outs = _mix(ogf, ogb, pg, ys, pu, omf, omb, pm, h, modtab, gn, mn, s5_d[i][None], s5_glu_w[i].astype(BF16),
                    s5_glu_b[i][None], wo_p, norm2[i][None], wr, layer=i, rows=rows, lat_only=last)
        if not is_moe:
            h, f = outs
            h = _ffn(f, h, modtab, _to_bf16(ffn_w1, j), _to_bf16(ffn_w3, j), _to_bf16(ffn_w2, j),
                     norm_f[None], layer=i, rows=rows, lat_only=last)
        else:
            h, f, route = outs
            row_token, row_w, block_expert, block_live, dest = _route_plan(route, MOE_TM)
            xg = _gather_rows(f, row_token)
            rw = jnp.broadcast_to(row_w[:, None], (row_w.shape[0], LANE))
            yg = _moe_experts(block_expert, block_live, xg, rw, _to_bf16(moe_w1, j), _to_bf16(moe_w3, j),
                              _to_bf16(moe_w2, j))
            y0 = _gather_rows(yg, dest[0])
            y1 = _gather_rows(yg, dest[1])
            h = _moe_resid(h, y0, y1, modtab, norm_f[None], layer=i, rows=rows, lat_only=last)
    return h.reshape(nb, n_lat, d)
```

```python
import functools
import math

import numpy as np
import jax
import jax.numpy as jnp
from jax import lax
from jax.experimental import pallas as pl
from jax.experimental.pallas import tpu as pltpu
from jax.experimental.pallas import tpu_sc as plsc

F32 = jnp.float32
BF16 = jnp.bfloat16
HIGHEST = lax.Precision.HIGHEST

GRID_W = 64
POS_BASE = 10000.0
EPS = 1e-6
GLA_HEADS, GLA_DK, GLA_DV, GLA_RANK, GLA_GATE_NORM = 4, 48, 96, 16, 16.0
S5_GROUP, S5_STATE = 16, 64
ML_HEADS, ML_DH, ML_CONV = 4, 96, 3
N_EXPERTS, TOP_K = 8, 2

LANE = 128
CHUNK = 64
SEG = 256
N_SUB = SEG // CHUNK
SCAN_BATCHES = 4
S5_STEPS = 128
S5_SUB = 32
NEG = -1e30
VMEM_LIMIT = 56 * 1024 * 1024

HP = LANE * GLA_HEADS
GLA_KP = 64
GQ = GLA_HEADS * GLA_KP
NG = 2 * GQ + 2 * HP + LANE
NM = 2 * HP + 2 * LANE


def _cparams(sem):
    return pltpu.CompilerParams(dimension_semantics=sem, vmem_limit_bytes=VMEM_LIMIT)


def _dot(a, b, **kw):
    return jnp.dot(a, b, preferred_element_type=F32, **kw)


def _dot_nt(a, b, **kw):
    return lax.dot_general(a, b, (((1,), (1,)), ((), ())), preferred_element_type=F32, **kw)


def _dot_tn(a, b, **kw):
    return lax.dot_general(a, b, (((0,), (0,)), ((), ())), preferred_element_type=F32, **kw)


def _log_sigmoid(x):
    return jnp.minimum(x, 0.0) - jnp.log1p(jnp.exp(-jnp.abs(x)))


def _silu(x):
    return x * jax.nn.sigmoid(x)


def _gelu_tanh(x):
    return 0.5 * x * (1.0 + jnp.tanh(math.sqrt(2.0 / math.pi) * (x + 0.044715 * (x * x * x))))


def _rmsnorm(x, g):
    return x * lax.rsqrt(jnp.mean(x * x, axis=-1, keepdims=True) + EPS) * g


class _Rows:
    def __init__(self, nb, ncb, ntb):
        self.nb, self.ncb, self.ntb, self.nlb = nb, ncb, ntb, ntb - ncb

    def n_blocks(self, lat_only):
        return self.nb * (self.nlb if lat_only else self.ntb)

    def src(self, lat_only):
        if lat_only:
            return lambda i: (i // self.nlb) * self.ntb + self.ncb + i % self.nlb
        return lambda i: i

    def sel(self, lat_only):
        if lat_only:
            return lambda i: i // self.nlb
        return lambda i: jnp.where(i % self.ntb < self.ncb, self.nb, i // self.ntb)


def _embed_kernel(ncb, ntb, ctx_ref, x_ref, pos_ref, o_ref):
    j = pl.program_id(0) % ntb

    @pl.when(j < ncb)
    def _():
        o_ref[...] = ctx_ref[...]

    @pl.when(j >= ncb)
    def _():
        o_ref[...] = x_ref[...] + pos_ref[...]


def _embed(ctx2, x2, pos, rows):
    d = ctx2.shape[1]
    ncb, ntb, nlb = rows.ncb, rows.ntb, rows.nlb
    return pl.pallas_call(
        functools.partial(_embed_kernel, ncb, ntb),
        out_shape=jax.ShapeDtypeStruct((rows.nb * ntb * SEG, d), F32),
        grid=(rows.nb * ntb,),
        in_specs=[pl.BlockSpec((SEG, d), lambda i: ((i // ntb) * ncb + jnp.minimum(i % ntb, ncb - 1), 0)),
                  pl.BlockSpec((SEG, d), lambda i: ((i // ntb) * nlb + jnp.maximum(i % ntb - ncb, 0), 0)),
                  pl.BlockSpec((SEG, d), lambda i: (jnp.maximum(i % ntb - ncb, 0), 0))],
        out_specs=pl.BlockSpec((SEG, d), lambda i: (i, 0)),
        compiler_params=_cparams(("arbitrary",)),
        name="embed",
    )(ctx2, x2, pos)


def _mod_kernel(c_ref, w_ref, b_ref, o_ref):
    s = _silu(c_ref[...])
    o_ref[0] = _dot(s, w_ref[0], precision=HIGHEST) + b_ref[0]


def _modulation(cond, w_ada, b_ada):
    depth, d, n6 = w_ada.shape
    tn = n6 // 4
    n_rows = cond.shape[0]
    return pl.pallas_call(
        _mod_kernel,
        out_shape=jax.ShapeDtypeStruct((depth, n_rows, n6), F32),
        grid=(depth, n6 // tn),
        in_specs=[pl.BlockSpec((n_rows, d), lambda l, j: (0, 0)),
                  pl.BlockSpec((1, d, tn), lambda l, j: (l, 0, j)),
                  pl.BlockSpec((1, 1, tn), lambda l, j: (l, 0, j))],
        out_specs=pl.BlockSpec((1, n_rows, tn), lambda l, j: (l, 0, j)),
        compiler_params=_cparams(("arbitrary", "arbitrary")),
        name="modulation",
    )(cond, w_ada, b_ada.reshape(depth, 1, n6))


def _proj_kernel(ncb, ntb, h_ref, hp_ref, hn_ref, mod_ref, g_ref, w_ref, wgt_ref, cw_ref, cb_ref, post_ref,
                 pg_ref, pu_ref, pm_ref, qk_ref, gt_ref):
    j = pl.program_id(0) % ntb
    m = mod_ref[0]
    act = lambda x: (_rmsnorm(x, g_ref[...]) * (1.0 + m[1:2]) + m[0:1]).astype(BF16)
    ab = act(h_ref[...])
    c1, c2 = NG + 2 * LANE, NG + 2 * LANE + NM
    pg_ref[...] = _dot(ab, w_ref[:, 0:NG])
    pu_ref[...] = _dot(ab, w_ref[:, NG:c1])
    pm_ref[...] = _dot(ab, w_ref[:, c1:c2])
    gt_ref[...] = _dot_nt(wgt_ref[...], ab)
    tm = h_ref.shape[0]
    he = jnp.concatenate([hp_ref[...], h_ref[...], hn_ref[...]], axis=0)
    xe = _dot(act(he), w_ref[:, c2:])
    n_e = tm + 16
    first = jnp.logical_or(j == 0, j == ncb).astype(F32)
    last = jnp.logical_or(j == ncb - 1, j == ntb - 1).astype(F32)
    row = lax.broadcasted_iota(jnp.int32, (tm, xe.shape[1]), 0)
    xp = pltpu.roll(xe, 1, axis=0)[8:8 + tm] * jnp.where(row == 0, 1.0 - first, 1.0)
    xn = pltpu.roll(xe, n_e - 1, axis=0)[8:8 + tm] * jnp.where(row == tm - 1, 1.0 - last, 1.0)
    y = cw_ref[0:1] * xp + cw_ref[1:2] * xe[8:8 + tm] + cw_ref[2:3] * xn + cb_ref[...]
    qk_ref[...] = _silu(y) * post_ref[...]


def _proj(h, modtab, g, w, wgt, cw, cb, post, *, layer, rows):
    r, d = h.shape
    tm = SEG
    t8 = tm // 8
    sel = rows.sel(False)
    full = lambda a: pl.BlockSpec(a.shape, lambda i: (0,) * a.ndim)
    return pl.pallas_call(
        functools.partial(_proj_kernel, rows.ncb, rows.ntb),
        out_shape=(jax.ShapeDtypeStruct((r, NG), F32), jax.ShapeDtypeStruct((r, 2 * LANE), F32),
                   jax.ShapeDtypeStruct((r, NM), F32), jax.ShapeDtypeStruct((r, 2 * HP), F32),
                   jax.ShapeDtypeStruct((16, r), F32)),
        grid=(r // tm,),
        in_specs=[pl.BlockSpec((tm, d), lambda i: (i, 0)),
                  pl.BlockSpec((8, d), lambda i: (jnp.maximum(i * t8 - 1, 0), 0)),
                  pl.BlockSpec((8, d), lambda i: (jnp.minimum((i + 1) * t8, r // 8 - 1), 0)),
                  pl.BlockSpec((None, 1, 8, d), lambda i: (layer, sel(i), 0, 0)),
                  full(g), full(w), full(wgt), full(cw), full(cb), full(post)],
        out_specs=(pl.BlockSpec((tm, NG), lambda i: (i, 0)),
                   pl.BlockSpec((tm, 2 * LANE), lambda i: (i, 0)),
                   pl.BlockSpec((tm, NM), lambda i: (i, 0)),
                   pl.BlockSpec((tm, 2 * HP), lambda i: (i, 0)),
                   pl.BlockSpec((16, tm), lambda i: (0, i))),
        compiler_params=_cparams(("arbitrary",)),
        name="proj",
    )(h, h, h, modtab, g, w, wgt, cw, cb, post)


def _conv_kernel(ncb, ntb, x_ref, prev_ref, next_ref, w_ref, b_ref, post_ref, o_ref):
    j = pl.program_id(0) % ntb
    x = x_ref[...]
    tm = x.shape[0]
    first = jnp.logical_or(j == 0, j == ncb).astype(F32)
    last = jnp.logical_or(j == ncb - 1, j == ntb - 1).astype(F32)
    row = lax.broadcasted_iota(jnp.int32, x.shape, 0)
    xp = jnp.where(row == 0, prev_ref[7:8, :] * (1.0 - first), pltpu.roll(x, 1, axis=0))
    xn = jnp.where(row == tm - 1, next_ref[0:1, :] * (1.0 - last), pltpu.roll(x, tm - 1, axis=0))
    y = w_ref[0:1] * xp + w_ref[1:2] * x + w_ref[2:3] * xn + b_ref[...]
    o_ref[...] = _silu(y) * post_ref[...]


def _conv(pm, w, b, post, *, rows):
    r = pm.shape[0]
    tm = SEG
    wq = 2 * HP
    t8 = tm // 8
    return pl.pallas_call(
        functools.partial(_conv_kernel, rows.ncb, rows.ntb),
        out_shape=jax.ShapeDtypeStruct((r, wq), F32),
        grid=(r // tm,),
        in_specs=[pl.BlockSpec((tm, wq), lambda i: (i, 0)),
                  pl.BlockSpec((8, wq), lambda i: (jnp.maximum(i * t8 - 1, 0), 0)),
                  pl.BlockSpec((8, wq), lambda i: (jnp.minimum((i + 1) * t8, r // 8 - 1), 0)),
                  pl.BlockSpec((8, wq), lambda i: (0, 0)),
                  pl.BlockSpec((1, wq), lambda i: (0, 0)),
                  pl.BlockSpec((1, wq), lambda i: (0, 0))],
        out_specs=pl.BlockSpec((tm, wq), lambda i: (i, 0)),
        compiler_params=_cparams(("arbitrary",)),
        name="conv",
    )(pm, pm, pm, w, b, post)


def _scan_pos(d, s, ncb, ntb):
    rev = jnp.where(s < ncb, ncb - 1 - s, ntb - 1 - (s - ncb))
    return jnp.where(d == 0, s, rev)


def _scan_pos_static(rev, s, ncb, ntb):
    if not rev:
        return s
    return jnp.where(s < ncb, ncb - 1 - s, ntb - 1 - (s - ncb))


def _tri(rev):
    r = lax.broadcasted_iota(jnp.int32, (CHUNK, CHUNK), 0)
    c = lax.broadcasted_iota(jnp.int32, (CHUNK, CHUNK), 1)
    return (r <= c) if rev else (r >= c)


def _chunk_rows(rev):
    return [(N_SUB - 1 - j if rev else j) * CHUNK for j in range(N_SUB)]


def _gla_kernel(rev, p_ref, wa_ref, ba_ref, o_ref, st_ref):
    s, g = pl.program_id(0), pl.program_id(1)
    nbb = p_ref.shape[0]
    b0 = g * nbb

    @pl.when(s == 0)
    def _():
        st_ref[pl.ds(b0, nbb)] = jnp.zeros((nbb,) + st_ref.shape[1:], F32)

    valid = _tri(rev)
    tri = valid.astype(F32)
    r0s = _chunk_rows(rev)
    wa, ba = wa_ref[...], ba_ref[...]
    inst = [(bb, j) for bb in range(nbb) for j in range(N_SUB)]
    heads = [slice(h * LANE, (h + 1) * LANE) for h in range(GLA_HEADS)]
    pairs = [slice((h // 2) * LANE, (h // 2 + 1) * LANE) for h in range(GLA_HEADS)]
    lane = lax.broadcasted_iota(jnp.int32, (CHUNK, LANE), 1)
    own = [(lane // GLA_KP) == (h % 2) for h in range(GLA_HEADS)]

    la = {}
    for bb, j in inst:
        lr = p_ref[bb, pl.ds(r0s[j], CHUNK), 2 * GQ + 2 * HP:NG]
        la[bb, j] = _log_sigmoid(_dot(lr, wa) + ba) * (1.0 / GLA_GATE_NORM)
    bc, e_last = {}, {}
    for i in inst:
        bc[i] = _dot_exact01(tri, la[i], lhs_is_01=True, pieces=2)
        e_last[i] = jnp.exp(jnp.sum(la[i], axis=0, keepdims=True))
    q_in, k_in, k_out, v = {}, {}, {}, {}
    for bb, j in inst:
        i = (bb, j)
        rs = pl.ds(r0s[j], CHUNK)
        qs = (p_ref[bb, rs, 0:GQ] * (GLA_DK ** -0.5) * jnp.exp(bc[i])).astype(BF16)
        for h in range(GLA_HEADS):
            q_in[i, h] = jnp.where(own[h], qs[:, pairs[h]], jnp.zeros_like(qs[:, pairs[h]]))
        kd = p_ref[bb, rs, GQ:2 * GQ] * jnp.exp(-bc[i])
        k_out[i] = (kd * e_last[i]).astype(BF16)
        k_in[i] = kd.astype(BF16)
        v[i] = p_ref[bb, rs, 2 * GQ:2 * GQ + HP].astype(BF16)
    att = {}
    for i in inst:
        for h in range(GLA_HEADS):
            att[i, h] = jnp.where(valid, _dot_nt(q_in[i, h], k_in[i][:, pairs[h]]), 0.0).astype(BF16)
    o_intra, ds = {}, {}
    for i in inst:
        for h, sl in enumerate(heads):
            o_intra[i, h] = _dot(att[i, h], v[i][:, sl])
            ds[i, h] = _dot_tn(v[i][:, sl], k_out[i][:, pairs[h]])
    s_in = {}
    for bb in range(nbb):
        for h in range(GLA_HEADS):
            st = st_ref[b0 + bb, h]
            for j in range(N_SUB):
                s_in[(bb, j), h] = st.astype(BF16)
                st = st * e_last[bb, j][:, pairs[h]] + ds[(bb, j), h]
            st_ref[b0 + bb, h] = st
    for bb, j in inst:
        for h, sl in enumerate(heads):
            o = o_intra[(bb, j), h] + _dot_nt(q_in[(bb, j), h], s_in[(bb, j), h])
            o_ref[bb, pl.ds(r0s[j], CHUNK), sl] = o.astype(o_ref.dtype)


def _gla(pg3, wa, ba, *, rows, rev):
    nb, l, _ = pg3.shape
    nbb = SCAN_BATCHES
    pos = functools.partial(_scan_pos_static, rev, ncb=rows.ncb, ntb=rows.ntb)
    return pl.pallas_call(
        functools.partial(_gla_kernel, rev),
        out_shape=jax.ShapeDtypeStruct((nb, l, HP), BF16),
        grid=(rows.ntb, nb // nbb),
        in_specs=[pl.BlockSpec((nbb, SEG, NG), lambda s, g: (g, pos(s), 0)),
                  pl.BlockSpec((LANE, GQ), lambda s, g: (0, 0)),
                  pl.BlockSpec((1, GQ), lambda s, g: (0, 0))],
        out_specs=pl.BlockSpec((nbb, SEG, HP), lambda s, g: (g, pos(s), 0)),
        scratch_shapes=[pltpu.VMEM((nb, GLA_HEADS, LANE, LANE), F32)],
        compiler_params=_cparams(("arbitrary", "arbitrary")),
        name="gla_scan_bwd" if rev else "gla_scan_fwd",
    )(pg3, wa, ba)


def _mlstm_kernel_old(qk_ref, v_ref, g_ref, gt_ref, gbr_ref, gbc_ref, o_ref, st_ref, m_ref):
    d, s, g = pl.program_id(0), pl.program_id(1), pl.program_id(2)
    nbb = qk_ref.shape[0]
    b0 = g * nbb

    @pl.when(s == 0)
    def _():
        st_ref[pl.ds(b0, nbb)] = jnp.zeros((nbb,) + st_ref.shape[1:], F32)
        m_ref[pl.ds(b0, nbb)] = jnp.zeros((nbb,) + m_ref.shape[1:], F32)

    fwd = d == 0
    valid = _tri(d)
    tri = valid.astype(F32)
    r0s = _chunk_rows(d)
    cidx = [jnp.where(fwd, j, N_SUB - 1 - j) for j in range(N_SUB)]
    lane = lax.broadcasted_iota(jnp.int32, (CHUNK, LANE), 1)
    inst = [(bb, j) for bb in range(nbb) for j in range(N_SUB)]
    hinst = [(bb, j, h) for bb, j in inst for h in range(ML_HEADS)]
    gbr, gbc = gbr_ref[...], gbc_ref[...]

    def pick(a, h, kind, axis):
        i0, i1 = kind * ML_HEADS + h, (2 + kind) * ML_HEADS + h
        if axis == 1:
            return jnp.where(fwd, a[:, i0:i0 + 1], a[:, i1:i1 + 1])
        return jnp.where(fwd, a[i0:i0 + 1, :], a[i1:i1 + 1, :])

    gc, gr, fcum_c, fcum_r = {}, {}, {}, {}
    for bb, j in inst:
        i = (bb, j)
        gc[i] = g_ref[bb, pl.ds(r0s[j], CHUNK), :] + gbr
        gr[i] = gt_ref[bb, cidx[j]] + gbc
    for i in inst:
        fcum_c[i] = _dot(tri, _log_sigmoid(gc[i]), precision=HIGHEST)
        fcum_r[i] = _dot_nt(_log_sigmoid(gr[i]), tri, precision=HIGHEST)
    fc, lic, d_log, rmax = {}, {}, {}, {}
    for bb, j, h in hinst:
        i = (bb, j)
        fc[bb, j, h] = pick(fcum_c[i], h, 1, 1)
        lic[bb, j, h] = pick(gc[i], h, 0, 1)
        dl = jnp.where(valid, fc[bb, j, h] - pick(fcum_r[i], h, 1, 0) + pick(gr[i], h, 0, 0), NEG)
        d_log[bb, j, h] = dl
        rmax[bb, j, h] = jnp.max(dl, axis=-1, keepdims=True)
    m_prev, m_t, m_new, f_tot = {}, {}, {}, {}
    for bb in range(nbb):
        for h in range(ML_HEADS):
            m = m_ref[b0 + bb, h:h + 1, 0:1]
            for j in range(N_SUB):
                i = (bb, j, h)
                m_prev[i] = m
                m_t[i] = jnp.maximum(fc[i] + m, rmax[i])
                m = jnp.where(fwd, m_t[i][CHUNK - 1:CHUNK], m_t[i][0:1])
                m_new[i] = m
                f_tot[i] = jnp.where(fwd, fc[i][CHUNK - 1:CHUNK], fc[i][0:1])
            m_ref[b0 + bb, h:h + 1, :] = jnp.broadcast_to(m, (1, LANE))
    q, v, qk, w_prev, gdec, ds = {}, {}, {}, {}, {}, {}
    for bb, j, h in hinst:
        i = (bb, j, h)
        rs = pl.ds(r0s[j], CHUNK)
        sl = slice(h * LANE, (h + 1) * LANE)
        q[i] = qk_ref[bb, rs, sl].astype(BF16)
        k = qk_ref[bb, rs, HP + h * LANE:HP + (h + 1) * LANE]
        v[i] = jnp.where(lane == ML_DH, 1.0, v_ref[bb, rs, sl]).astype(BF16)
        w = jnp.exp(d_log[i] - m_t[i])
        w_prev[i] = jnp.exp(fc[i] + m_prev[i] - m_t[i])
        qk[i] = (_dot_nt(q[i], k.astype(BF16)) * w).astype(BF16)
        w_s = jnp.exp(f_tot[i] - fc[i] + lic[i] - m_new[i])
        gdec[i] = jnp.exp(f_tot[i] + m_prev[i] - m_new[i])
        ds[i] = _dot_tn(v[i], (k * w_s).astype(BF16))
    s_in = {}
    for bb in range(nbb):
        for h in range(ML_HEADS):
            st = st_ref[b0 + bb, h]
            for j in range(N_SUB):
                i = (bb, j, h)
                s_in[i] = st.astype(BF16)
                st = gdec[i] * st + ds[i]
            st_ref[b0 + bb, h] = st
    for bb, j, h in hinst:
        i = (bb, j, h)
        num = w_prev[i] * _dot_nt(q[i], s_in[i]) + _dot(qk[i], v[i])
        den = num[:, ML_DH:ML_DH + 1]
        hh = num / jnp.maximum(jnp.abs(den), jnp.exp(-m_t[i]))
        o_ref[0, bb, pl.ds(r0s[j], CHUNK), h * LANE:(h + 1) * LANE] = jnp.where(lane < ML_DH, hh, 0.0)


def _mlstm_old(qk3, pm3, gt4, gbr, gbc, *, rows):
    nb, l, _ = qk3.shape
    nbb = SCAN_BATCHES
    pos = functools.partial(_scan_pos, ncb=rows.ncb, ntb=rows.ntb)
    return pl.pallas_call(
        _mlstm_kernel,
        out_shape=jax.ShapeDtypeStruct((2, nb, l, HP), F32),
        grid=(2, rows.ntb, nb // nbb),
        in_specs=[pl.BlockSpec((nbb, SEG, 2 * HP), lambda d, s, g: (g, pos(d, s), 0)),
                  pl.BlockSpec((nbb, SEG, HP), lambda d, s, g: (g, pos(d, s), 2)),
                  pl.BlockSpec((nbb, SEG, LANE), lambda d, s, g: (g, pos(d, s), 4 * GLA_HEADS)),
                  pl.BlockSpec((nbb, N_SUB, 16, CHUNK), lambda d, s, g: (g, pos(d, s), 0, 0)),
                  pl.BlockSpec((1, LANE), lambda d, s, g: (0, 0)),
                  pl.BlockSpec((16, CHUNK), lambda d, s, g: (0, 0))],
        out_specs=pl.BlockSpec((1, nbb, SEG, HP), lambda d, s, g: (d, g, pos(d, s), 0)),
        scratch_shapes=[pltpu.VMEM((nb, ML_HEADS, LANE, LANE), F32), pltpu.VMEM((nb, 8, LANE), F32)],
        compiler_params=_cparams(("arbitrary", "arbitrary", "arbitrary")),
        name="mlstm_scan",
    )(qk3, pm3, pm3, gt4, gbr, gbc)


def _split_bf16(x, n):
    parts, r = [], x
    for _ in range(n):
        p = r.astype(BF16)
        parts.append(p)
        r = r - p.astype(F32)
    return parts


def _dot_exact01(a, b, lhs_is_01, pieces=3):
    if lhs_is_01:
        a = a.astype(BF16)
        terms = [_dot(a, p) for p in _split_bf16(b, pieces)]
    else:
        b = b.astype(BF16)
        terms = [_dot(p, b) for p in _split_bf16(a, pieces)]
    return functools.reduce(lambda x, y: x + y, terms)


def _cummax_rows(a, rev):
    n = a.shape[0]
    row = lax.broadcasted_iota(jnp.int32, a.shape, 0)
    k = 1
    while k < n:
        if rev:
            sh = jnp.where(row < n - k, pltpu.roll(a, n - k, axis=0), NEG)
        else:
            sh = jnp.where(row >= k, pltpu.roll(a, k, axis=0), NEG)
        a = jnp.maximum(a, sh)
        k *= 2
    return a


ML_GL = ML_HEADS


def _mlstm_kernel(rev, qk_ref, v_ref, g_ref, gt_ref, gbr_ref, gbt_ref, o_ref, st_ref, m_ref):
    s, g = pl.program_id(0), pl.program_id(1)
    nbb = qk_ref.shape[0]
    b0 = g * nbb

    @pl.when(s == 0)
    def _():
        st_ref[pl.ds(b0, nbb)] = jnp.zeros((nbb,) + st_ref.shape[1:], F32)
        m_ref[pl.ds(b0, nbb)] = jnp.zeros((nbb,) + m_ref.shape[1:], F32)

    valid = _tri(rev)
    tri = valid.astype(F32)
    r0s = _chunk_rows(rev)
    cs = [r // CHUNK for r in r0s]
    last = 0 if rev else CHUNK - 1
    inst = [(bb, j) for bb in range(nbb) for j in range(N_SUB)]
    heads = [slice(h * LANE, (h + 1) * LANE) for h in range(ML_HEADS)]

    r_sel = lax.broadcasted_iota(jnp.int32, (LANE, HP), 0)
    c_sel = lax.broadcasted_iota(jnp.int32, (LANE, HP), 1)
    sel_h = (r_sel == ML_GL + c_sel // LANE).astype(BF16)
    r_t = lax.broadcasted_iota(jnp.int32, (HP, HP), 0)
    c_t = lax.broadcasted_iota(jnp.int32, (HP, HP), 1)
    same = jnp.logical_and(r_t // LANE == c_t // LANE, jnp.logical_and(r_t % LANE < CHUNK, c_t % LANE < CHUNK))
    before = (r_t % LANE >= c_t % LANE) if rev else (r_t % LANE <= c_t % LANE)
    tri_b = jnp.logical_and(same, before).astype(BF16)
    r_v = lax.broadcasted_iota(jnp.int32, (CHUNK, HP), 0)
    c_v = lax.broadcasted_iota(jnp.int32, (CHUNK, HP), 1) % LANE
    valid4 = jnp.logical_and(c_v < CHUNK, (r_v <= c_v) if rev else (r_v >= c_v))
    lane4 = lax.broadcasted_iota(jnp.int32, (CHUNK, HP), 1) % LANE
    lane_c = lax.broadcasted_iota(jnp.int32, (CHUNK, LANE), 1)
    lane1 = lax.broadcasted_iota(jnp.int32, (1, LANE), 1)
    head_lane = jnp.logical_and(lane1 >= ML_GL, lane1 < ML_GL + ML_HEADS)
    gbr, gbt = gbr_ref[...], gbt_ref[...]

    gcs, fcm, cmx, a_row, grt = {}, {}, {}, {}, {}
    for bb, j in inst:
        gc = g_ref[bb, pl.ds(r0s[j], CHUNK), :] + gbr
        gcs[bb, j] = pltpu.roll(gc, ML_GL, axis=1)
        fcm[bb, j] = _dot_exact01(tri, _log_sigmoid(gc), lhs_is_01=True)
        grt[bb, j] = gt_ref[bb, cs[j]] + gbt
    row_id = lax.broadcasted_iota(jnp.int32, (len(inst), HP), 0)
    lfr = jnp.zeros((len(inst), HP), F32)
    for n, i in enumerate(inst):
        lfr = jnp.where(row_id == n, _log_sigmoid(grt[i][1:2]), lfr)
    fcr = _dot_exact01(lfr, tri_b, lhs_is_01=False)
    for n, i in enumerate(inst):
        a_row[i] = grt[i][0:1] - fcr[n:n + 1]
        cmx[i] = _cummax_rows(gcs[i] - fcm[i], rev)
    bx = {}
    e_neg, gd = {}, {}
    for bb in range(nbb):
        m_prev = m_ref[b0 + bb, 0:1, :]
        for j in range(N_SUB):
            i = (bb, j)
            m_t = fcm[i] + jnp.maximum(m_prev, cmx[i])
            m_new = m_t[last:last + 1]
            f_tot = fcm[i][last:last + 1]
            u = fcm[i] - m_t
            w_prev = jnp.exp(u + m_prev)
            w_s = jnp.exp(f_tot - fcm[i] + gcs[i] - m_new)
            gdec = jnp.broadcast_to(jnp.exp(f_tot + m_prev - m_new), (16, LANE))
            e_neg[i] = jnp.exp(-m_t)
            keep = lambda a: jnp.where(head_lane, a, 0.0)
            bx[i] = jnp.concatenate(_split_bf16(keep(u), 2) + _split_bf16(keep(w_prev), 1)
                                    + _split_bf16(keep(w_s), 1) + _split_bf16(keep(gdec), 2), axis=0)
            m_prev = m_new
        m_ref[b0 + bb] = jnp.broadcast_to(m_prev, (8, LANE))
    ub, wpb, wsb, gdb = {}, {}, {}, {}
    for i in inst:
        y = _dot(bx[i], sel_h)
        c = CHUNK
        ub[i] = y[0:c] + y[c:2 * c]
        wpb[i] = y[2 * c:3 * c]
        wsb[i] = y[3 * c:4 * c]
        gdb[i] = y[4 * c:4 * c + 1] + y[4 * c + 16:4 * c + 17]
    q, v, qkw, ds = {}, {}, {}, {}
    for bb, j in inst:
        i = (bb, j)
        rs = pl.ds(r0s[j], CHUNK)
        w = jnp.where(valid4, jnp.exp(ub[i] + a_row[i]), 0.0)
        q[i] = qk_ref[bb, rs, 0:HP].astype(BF16)
        k = qk_ref[bb, rs, HP:2 * HP]
        kb = k.astype(BF16)
        kw = (k * wsb[i]).astype(BF16)
        v[i] = jnp.where(lane4 == ML_DH, 1.0, v_ref[bb, rs, :]).astype(BF16)
        for h, sl in enumerate(heads):
            sc = _dot_nt(q[i][:, sl], kb[:, sl])
            qkw[i, h] = (sc * w[:, h * LANE:h * LANE + CHUNK]).astype(BF16)
            ds[i, h] = _dot_tn(v[i][:, sl], kw[:, sl])
    s_in = {}
    for bb in range(nbb):
        for h, sl in enumerate(heads):
            st = st_ref[b0 + bb, h]
            for j in range(N_SUB):
                i = (bb, j)
                s_in[i, h] = st.astype(BF16)
                st = gdb[i][:, sl] * st + ds[i, h]
            st_ref[b0 + bb, h] = st
    num = {}
    for i in inst:
        parts = [_dot_nt(q[i][:, sl], s_in[i, h]) for h, sl in enumerate(heads)]
        intra = [_dot(qkw[i, h], v[i][:, sl]) for h, sl in enumerate(heads)]
        num[i] = wpb[i] * jnp.concatenate(parts, axis=1) + jnp.concatenate(intra, axis=1)
    for bb, j in inst:
        i = (bb, j)
        den = jnp.zeros((CHUNK, LANE), F32)
        for h, sl in enumerate(heads):
            dh = jnp.sum(jnp.where(lane_c == ML_DH, num[i][:, sl], 0.0), axis=-1, keepdims=True)
            den = jnp.where(lane_c == ML_GL + h, jnp.broadcast_to(dh, (CHUNK, LANE)), den)
        r1, r2 = _split_bf16(jnp.where(head_lane, 1.0 / jnp.maximum(jnp.abs(den), e_neg[i]), 0.0), 2)
        rb = _dot(jnp.concatenate([r1, r2], axis=0), sel_h)
        rb = rb[0:CHUNK] + rb[CHUNK:2 * CHUNK]
        o_ref[bb, pl.ds(r0s[j], CHUNK), :] = jnp.where(lane4 < ML_DH, num[i] * rb, 0.0).astype(o_ref.dtype)


def _mlstm(qk3, pm3, gtl, gbr, gbt, *, rows, rev):
    nb, l, _ = qk3.shape
    nbb = SCAN_BATCHES
    dr = 1 if rev else 0
    pos = functools.partial(_scan_pos_static, rev, ncb=rows.ncb, ntb=rows.ntb)
    return pl.pallas_call(
        functools.partial(_mlstm_kernel, rev),
        out_shape=jax.ShapeDtypeStruct((nb, l, HP), BF16),
        grid=(rows.ntb, nb // nbb),
        in_specs=[pl.BlockSpec((nbb, SEG, 2 * HP), lambda s, g: (g, pos(s), 0)),
                  pl.BlockSpec((nbb, SEG, HP), lambda s, g: (g, pos(s), 0)),
                  pl.BlockSpec((nbb, SEG, LANE), lambda s, g: (g, pos(s), 2 * ML_HEADS + dr)),
                  pl.BlockSpec((nbb, N_SUB, None, 2, HP), lambda s, g: (g, pos(s), dr, 0, 0)),
                  pl.BlockSpec((1, LANE), lambda s, g: (0, 0)),
                  pl.BlockSpec((2, HP), lambda s, g: (0, 0))],
        out_specs=pl.BlockSpec((nbb, SEG, HP), lambda s, g: (g, pos(s), 0)),
        scratch_shapes=[pltpu.VMEM((nb, ML_HEADS, LANE, LANE), F32), pltpu.VMEM((nb, 8, LANE), F32)],
        compiler_params=_cparams(("arbitrary", "arbitrary")),
        name="mlstm_scan_bwd" if rev else "mlstm_scan_fwd",
    )(qk3, pm3, pm3, gtl, gbr, gbt)


def _s5_kernel(nb, u_ref, bre_ref, bim_ref, are_ref, aim_ref, cre_ref, cim_ref, o_ref, xr_ref, xi_ref, st_ref):
    d, s = pl.program_id(0), pl.program_id(1)

    @pl.when(s == 0)
    def _():
        st_ref[...] = jnp.zeros(st_ref.shape, F32)

    n_sub = u_ref.shape[0] // (S5_SUB * nb)
    sub_rows = S5_SUB * nb

    def run(rev):
        ar, ai = are_ref[0], aim_ref[0]
        order = list(range(n_sub))[::-1] if rev else list(range(n_sub))

        def project_in(q):
            rs = slice(q * sub_rows, (q + 1) * sub_rows)
            u = u_ref[rs, :].astype(BF16)
            xr_ref[rs, :] = _dot(u, bre_ref[0])
            xi_ref[rs, :] = _dot(u, bim_ref[0])

        sr, si = st_ref[0], st_ref[1]
        project_in(order[0])
        for n, q in enumerate(order):
            if n + 1 < n_sub:
                project_in(order[n + 1])
            for j in range(S5_SUB):
                r0 = (q * S5_SUB + (S5_SUB - 1 - j if rev else j)) * nb
                nr = ar * sr - ai * si + xr_ref[r0:r0 + nb, :]
                ni = ar * si + ai * sr + xi_ref[r0:r0 + nb, :]
                xr_ref[r0:r0 + nb, :] = nr
                xi_ref[r0:r0 + nb, :] = ni
                sr, si = nr, ni
            rs = slice(q * sub_rows, (q + 1) * sub_rows)
            o_ref[0, rs, :] = (_dot(xr_ref[rs, :].astype(BF16), cre_ref[...])
                               - _dot(xi_ref[rs, :].astype(BF16), cim_ref[...]))
        st_ref[0] = sr
        st_ref[1] = si

    @pl.when(d == 0)
    def _():
        run(False)

    @pl.when(d == 1)
    def _():
        run(True)


def _s5(ut, bre, bim, are, aim, cre, cim, *, nb, nc, nt):
    n_rows, ch = ut.shape
    tr = S5_STEPS * nb
    ns = bre.shape[-1]
    pos = functools.partial(_scan_pos, ncb=nc, ntb=nt)
    return pl.pallas_call(
        functools.partial(_s5_kernel, nb),
        out_shape=jax.ShapeDtypeStruct((2, n_rows, ch), F32),
        grid=(2, nt),
        in_specs=[pl.BlockSpec((tr, ch), lambda d, s: (pos(d, s), 0)),
                  pl.BlockSpec((1, ch, ns), lambda d, s: (d, 0, 0)),
                  pl.BlockSpec((1, ch, ns), lambda d, s: (d, 0, 0)),
                  pl.BlockSpec((1, nb, ns), lambda d, s: (d, 0, 0)),
                  pl.BlockSpec((1, nb, ns), lambda d, s: (d, 0, 0)),
                  pl.BlockSpec((ns, ch), lambda d, s: (0, 0)),
                  pl.BlockSpec((ns, ch), lambda d, s: (0, 0))],
        out_specs=pl.BlockSpec((1, tr, ch), lambda d, s: (d, pos(d, s), 0)),
        scratch_shapes=[pltpu.VMEM((tr, ns), F32), pltpu.VMEM((tr, ns), F32), pltpu.VMEM((2, nb, ns), F32)],
        compiler_params=_cparams(("arbitrary", "arbitrary")),
        name="s5_scan",
    )(ut, bre, bim, are, aim, cre, cim)


def _head_norm(o, gain, dim):
    parts = []
    for h in range(o.shape[1] // LANE):
        seg = o[:, h * LANE:(h + 1) * LANE]
        ms = jnp.sum(seg * seg, axis=-1, keepdims=True) * (1.0 / dim)
        parts.append(seg * lax.rsqrt(ms + EPS))
    return jnp.concatenate(parts, axis=1) * gain


def _mix_kernel(with_router, ogf_ref, ogb_ref, gg_ref, ys_ref, u_ref, omf_ref, omb_ref, mo_ref, h_ref, mod_ref,
                gn_ref, mn_ref, sd_ref, gw_ref, gb_ref, wo_ref, n2_ref, *rest):
    if with_router:
        wr_ref, ho_ref, f_ref, rt_ref = rest
    else:
        ho_ref, f_ref = rest
    gla = _head_norm(ogf_ref[...].astype(F32) + ogb_ref[...].astype(F32), gn_ref[...], GLA_DV) * _silu(gg_ref[...])
    z = _gelu_tanh(ys_ref[0] + ys_ref[1] + sd_ref[...] * u_ref[...])
    s5 = z * jax.nn.sigmoid(_dot(z.astype(BF16), gw_ref[...]) + gb_ref[...])
    ml = _head_norm(omf_ref[...].astype(F32) + omb_ref[...].astype(F32), mn_ref[...], ML_DH) * jax.nn.sigmoid(mo_ref[...])
    mix = (_dot(gla.astype(BF16), wo_ref[0:HP]) + _dot(s5.astype(BF16), wo_ref[HP:HP + 2 * LANE])
           + _dot(ml.astype(BF16), wo_ref[HP + 2 * LANE:]))
    m = mod_ref[0]
    hn = h_ref[...] + m[2:3] * mix
    ho_ref[...] = hn
    f = _rmsnorm(hn, n2_ref[...]) * (1.0 + m[4:5]) + m[3:4]
    if not with_router:
        f_ref[...] = f.astype(f_ref.dtype)
    if with_router:
        f_hi, f_lo = _split_bf16(f, 2)
        logits = (_dot(f_hi, wr_ref[0]) + _dot(f_lo, wr_ref[0]) + _dot(f_hi, wr_ref[1])).T[0:2 * N_EXPERTS]
        row = lax.broadcasted_iota(jnp.int32, logits.shape, 0)
        l0 = jnp.where(row < N_EXPERTS, logits, NEG)
        m1 = jnp.max(l0, axis=0, keepdims=True)
        i1 = jnp.min(jnp.where(l0 == m1, row, 2 * N_EXPERTS), axis=0, keepdims=True)
        l1 = jnp.where(row == i1, NEG, l0)
        m2 = jnp.max(l1, axis=0, keepdims=True)
        i2 = jnp.min(jnp.where(l1 == m2, row, 2 * N_EXPERTS), axis=0, keepdims=True)
        e = jnp.exp(m2 - m1)
        w1 = 1.0 / (1.0 + e)
        w2 = e / (1.0 + e)
        rt = jnp.where(row == 0, i1.astype(F32),
                       jnp.where(row == 1, i2.astype(F32),
                                 jnp.where(row == 2, w1, jnp.where(row == 3, w2, 0.0))))
        rt_ref[...] = rt
        d = f.shape[1]
        f_ref[:, 0:d] = f
        f_ref[:, d:] = jnp.concatenate([rt, jnp.zeros((LANE - rt.shape[0], rt.shape[1]), F32)], axis=0).T


def _mix(ogf, ogb, pg, ys, pu, omf, omb, pm, h, modtab, gn, mn, sd, gw, gb, wo, n2, wr, *, layer, rows, lat_only):
    d = h.shape[1]
    tm = SEG
    n = rows.n_blocks(lat_only)
    src, sel = rows.src(lat_only), rows.sel(lat_only)
    full = lambda a: pl.BlockSpec(a.shape, lambda i: (0,) * a.ndim)
    with_router = wr is not None
    in_specs = [pl.BlockSpec((tm, HP), lambda i: (src(i), 0)),
                pl.BlockSpec((tm, HP), lambda i: (src(i), 0)),
                pl.BlockSpec((tm, HP), lambda i: (src(i), (2 * GQ + HP) // HP)),
                pl.BlockSpec((2, tm, 2 * LANE), lambda i: (0, src(i), 0)),
                pl.BlockSpec((tm, 2 * LANE), lambda i: (src(i), 0)),
                pl.BlockSpec((tm, HP), lambda i: (src(i), 0)),
                pl.BlockSpec((tm, HP), lambda i: (src(i), 0)),
                pl.BlockSpec((tm, HP), lambda i: (src(i), 1)),
                pl.BlockSpec((tm, d), lambda i: (src(i), 0)),
                pl.BlockSpec((None, 1, 8, d), lambda i: (layer, sel(i), 0, 0)),
                full(gn), full(mn), full(sd), full(gw), full(gb), full(wo), full(n2)]
    args = [ogf, ogb, pg, ys, pu, omf, omb, pm, h, modtab, gn, mn, sd, gw, gb, wo, n2]
    fw, fdt = (d + LANE, F32) if with_router else (d, BF16)
    out_shape = [jax.ShapeDtypeStruct((n * tm, d), F32), jax.ShapeDtypeStruct((n * tm, fw), fdt)]
    out_specs = [pl.BlockSpec((tm, d), lambda i: (i, 0)), pl.BlockSpec((tm, fw), lambda i: (i, 0))]
    if with_router:
        in_specs.append(full(wr))
        args.append(wr)
        out_shape.append(jax.ShapeDtypeStruct((2 * N_EXPERTS, n * tm), F32))
        out_specs.append(pl.BlockSpec((2 * N_EXPERTS, tm), lambda i: (0, i)))
    return pl.pallas_call(
        functools.partial(_mix_kernel, with_router),
        out_shape=tuple(out_shape),
        grid=(n,),
        in_specs=in_specs,
        out_specs=tuple(out_specs),
        compiler_params=_cparams(("arbitrary",)),
        name="mix_out",
    )(*args)


FF_TILE = 256


def _swiglu(xb, w1_ref, w3_ref, w2_ref, a_ref, lead=()):
    dff = w1_ref.shape[-1]
    for j in range(dff // FF_TILE):
        sl = slice(j * FF_TILE, (j + 1) * FF_TILE)
        h1 = _dot(xb, w1_ref[lead + (slice(None), sl)])
        h3 = _dot(xb, w3_ref[lead + (slice(None), sl)])
        a_ref[:, sl] = (_silu(h1) * h3).astype(BF16)
    return _dot(a_ref[...], w2_ref[lead + (slice(None), slice(None))])


FFN_TM = 1024


def _ffn_kernel(final, sel, f_ref, h_ref, mod_ref, w1_ref, w3_ref, w2_ref, *rest):
    if final:
        nf_ref, o_ref, a_ref = rest
    else:
        o_ref, a_ref = rest
    y = _swiglu(f_ref[...], w1_ref, w3_ref, w2_ref, a_ref)
    n_seg = f_ref.shape[0] // SEG
    for q in range(n_seg):
        rs = slice(q * SEG, (q + 1) * SEG)
        gate = mod_ref[sel(pl.program_id(0) * n_seg + q)][5:6]
        hn = h_ref[rs, :] + gate * y[rs]
        o_ref[rs, :] = _rmsnorm(hn, nf_ref[...]) if final else hn


def _ffn(f, h, modtab, w1, w3, w2, nf, *, layer, rows, lat_only):
    r, d = h.shape
    tm = FFN_TM
    assert r % tm == 0
    dff = w1.shape[-1]
    full = lambda a: pl.BlockSpec(a.shape, lambda i: (0,) * a.ndim)
    resident = lambda a: pl.BlockSpec(a.shape, lambda i: (0,) * a.ndim, pipeline_mode=pl.Buffered(1))
    in_specs = [pl.BlockSpec((tm, d), lambda i: (i, 0)),
                pl.BlockSpec((tm, d), lambda i: (i, 0)),
                pl.BlockSpec((None,) + modtab.shape[1:], lambda i: (layer, 0, 0, 0)),
                resident(w1), resident(w3), resident(w2)]
    args = [f, h, modtab, w1, w3, w2]
    if lat_only:
        in_specs.append(full(nf))
        args.append(nf)
    return pl.pallas_call(
        functools.partial(_ffn_kernel, lat_only, rows.sel(lat_only)),
        out_shape=jax.ShapeDtypeStruct((r, d), F32),
        grid=(r // tm,),
        in_specs=in_specs,
        out_specs=pl.BlockSpec((tm, d), lambda i: (i, 0)),
        scratch_shapes=[pltpu.VMEM((tm, dff), BF16)],
        compiler_params=_cparams(("arbitrary",)),
        name="ffn",
    )(*args)


MOE_TM = 512


def _moe_kernel(be_ref, live_ref, x_ref, w1_ref, w3_ref, w2_ref, o_ref, a_ref):
    i = pl.program_id(0)
    d = o_ref.shape[1]

    @pl.when(live_ref[i] > 0)
    def _():
        y = _swiglu(x_ref[:, 0:d].astype(BF16), w1_ref, w3_ref, w2_ref, a_ref, lead=(0,))
        tail = x_ref[:, d:]
        mine = tail[:, 0:1] == be_ref[i].astype(F32)
        o_ref[...] = y * jnp.where(mine, tail[:, 2:3], tail[:, 3:4])

    @pl.when(live_ref[i] == 0)
    def _():
        o_ref[...] = jnp.zeros(o_ref.shape, F32)


def _moe_experts(block_expert, block_live, xg, w1, w3, w2):
    n_rows, dx = xg.shape
    d = w1.shape[-2]
    dff = w1.shape[-1]
    tm = MOE_TM
    return pl.pallas_call(
        _moe_kernel,
        out_shape=jax.ShapeDtypeStruct((n_rows, d), F32),
        grid_spec=pltpu.PrefetchScalarGridSpec(
            num_scalar_prefetch=2,
            grid=(n_rows // tm,),
            in_specs=[pl.BlockSpec((tm, dx), lambda i, be, lv: (i, 0)),
                      pl.BlockSpec((1, d, dff), lambda i, be, lv: (be[i], 0, 0)),
                      pl.BlockSpec((1, d, dff), lambda i, be, lv: (be[i], 0, 0)),
                      pl.BlockSpec((1, dff, d), lambda i, be, lv: (be[i], 0, 0))],
            out_specs=pl.BlockSpec((tm, d), lambda i, be, lv: (i, 0)),
            scratch_shapes=[pltpu.VMEM((tm, dff), BF16)]),
        compiler_params=_cparams(("arbitrary",)),
        name="moe_experts",
    )(block_expert, block_live, xg, w1, w3, w2)


SC_GATHER_ROWS = 64


def _gather_rows(table, idx):
    n_idx = idx.shape[0]
    _, d = table.shape
    info = plsc.get_sparse_core_info()
    n_cores, n_workers = info.num_cores, info.num_cores * info.num_subcores
    assert n_idx % (n_workers * SC_GATHER_ROWS) == 0
    per_worker = n_idx // n_workers
    mesh = plsc.VectorSubcoreMesh(core_axis_name="c", subcore_axis_name="s")

    @functools.partial(
        pl.kernel, mesh=mesh,
        out_type=jax.ShapeDtypeStruct((n_idx, d), table.dtype),
        scratch_types=[pltpu.VMEM((SC_GATHER_ROWS,), jnp.int32),
                       pltpu.VMEM((SC_GATHER_ROWS, d), table.dtype),
                       pltpu.SemaphoreType.DMA])
    def gather(table_hbm, idx_hbm, out_hbm, idx_v, rows_v, sem):
        base = (lax.axis_index("s") * n_cores + lax.axis_index("c")) * per_worker

        @pl.loop(0, per_worker // SC_GATHER_ROWS)
        def _(it):
            off = pl.multiple_of(base + it * SC_GATHER_ROWS, SC_GATHER_ROWS)
            pltpu.sync_copy(idx_hbm.at[pl.ds(off, SC_GATHER_ROWS)], idx_v)
            pltpu.async_copy(table_hbm.at[idx_v], rows_v, sem).wait()
            pltpu.sync_copy(rows_v, out_hbm.at[pl.ds(off, SC_GATHER_ROWS)])

    return gather(table, idx)


def _scatter_rows(src, dest, n_rows):
    n_tok, d = src.shape
    info = plsc.get_sparse_core_info()
    n_cores, n_workers = info.num_cores, info.num_cores * info.num_subcores
    assert n_tok % (n_workers * SC_GATHER_ROWS) == 0
    per_worker = n_tok // n_workers
    mesh = plsc.VectorSubcoreMesh(core_axis_name="c", subcore_axis_name="s")

    @functools.partial(
        pl.kernel, mesh=mesh,
        out_type=jax.ShapeDtypeStruct((n_rows, d), src.dtype),
        scratch_types=[pltpu.VMEM((SC_GATHER_ROWS,), jnp.int32),
                       pltpu.VMEM((SC_GATHER_ROWS, d), src.dtype),
                       pltpu.SemaphoreType.DMA])
    def scatter(src_hbm, dest_hbm, out_hbm, idx_v, rows_v, sem):
        base = (lax.axis_index("s") * n_cores + lax.axis_index("c")) * per_worker

        @pl.loop(0, per_worker // SC_GATHER_ROWS)
        def _(it):
            off = pl.multiple_of(base + it * SC_GATHER_ROWS, SC_GATHER_ROWS)
            pltpu.sync_copy(src_hbm.at[pl.ds(off, SC_GATHER_ROWS)], rows_v)
            for k in range(TOP_K):
                pltpu.sync_copy(dest_hbm.at[pl.ds(k * n_tok + off, SC_GATHER_ROWS)], idx_v)
                pltpu.async_copy(rows_v, out_hbm.at[idx_v], sem).wait()

    return scatter(src, dest.reshape(-1))


def _resid_kernel(final, h_ref, y0_ref, y1_ref, mod_ref, *rest):
    hn = h_ref[...] + mod_ref[0][5:6] * (y0_ref[...] + y1_ref[...])
    if final:
        nf_ref, o_ref = rest
        o_ref[...] = _rmsnorm(hn, nf_ref[...])
    else:
        (o_ref,) = rest
        o_ref[...] = hn


def _moe_resid(h, y0, y1, modtab, nf, *, layer, rows, lat_only):
    r, d = h.shape
    tm = SEG
    sel = rows.sel(lat_only)
    row = lambda i: (i, 0)
    in_specs = [pl.BlockSpec((tm, d), row), pl.BlockSpec((tm, d), row), pl.BlockSpec((tm, d), row),
                pl.BlockSpec((None, 1, 8, d), lambda i: (layer, sel(i), 0, 0))]
    args = [h, y0, y1, modtab]
    if lat_only:
        in_specs.append(pl.BlockSpec((1, d), lambda i: (0, 0)))
        args.append(nf)
    return pl.pallas_call(
        functools.partial(_resid_kernel, lat_only),
        out_shape=jax.ShapeDtypeStruct((r, d), F32),
        grid=(r // tm,),
        in_specs=in_specs,
        out_specs=pl.BlockSpec((tm, d), row),
        compiler_params=_cparams(("arbitrary",)),
        name="moe_resid",
    )(*args)


CAST_PARTS = 8


def _cast_kernel(*refs):
    o_ref = refs[-1]
    tr = refs[0].shape[1]
    for k, x_ref in enumerate(refs[:-1]):
        o_ref[0, k * tr:(k + 1) * tr, :] = x_ref[0].astype(o_ref.dtype)


def _to_bf16(w, j):
    lead, (r, c) = w.shape[1:-2], w.shape[-2:]
    n = int(np.prod(lead, dtype=np.int64))
    w3 = w.reshape((-1, r, c))
    tr = r // CAST_PARTS
    assert tr * CAST_PARTS == r and tr % 16 == 0
    band = lambda k: pl.BlockSpec((1, tr, c), lambda e: (j * n + e, k, 0))
    out = pl.pallas_call(
        _cast_kernel,
        out_shape=jax.ShapeDtypeStruct((n, r, c), BF16),
        grid=(n,),
        in_specs=[band(k) for k in range(CAST_PARTS)],
        out_specs=pl.BlockSpec((1, r, c), lambda e: (e, 0, 0)),
        compiler_params=_cparams(("arbitrary",)),
        name="to_bf16",
    )(*([w3] * CAST_PARTS))
    return out.reshape(lead + (r, c))


def _pad_heads(w, heads, dim, to=LANE):
    lead = w.shape[:-1]
    w = w.reshape(lead + (heads, dim))
    w = jnp.pad(w, [(0, 0)] * len(lead) + [(0, 0), (0, to - dim)])
    return w.reshape(lead + (heads * to,))


def _pad_last(w, to):
    return jnp.pad(w, [(0, 0)] * (w.ndim - 1) + [(0, to - w.shape[-1])])


def _pos_embed(n_tokens, d):
    n_grid_rows = n_tokens // GRID_W
    row, col = jnp.meshgrid(jnp.arange(n_grid_rows, dtype=F32), jnp.arange(GRID_W, dtype=F32), indexing='ij')
    n_freq = d // 4
    omega = jnp.exp(-math.log(POS_BASE) * jnp.arange(n_freq, dtype=F32) / n_freq)

    def axis_embed(p):
        ang = p.reshape(-1, 1) * omega
        return jnp.concatenate([jnp.sin(ang), jnp.cos(ang)], axis=-1)

    return jnp.concatenate([axis_embed(row), axis_embed(col)], axis=-1)


def _s5_discretise(lam_re, lam_im, log_dt, b_re, b_im):
    dt = jnp.exp(log_dt)[:, None]
    mag = jnp.exp(lam_re * dt)
    abar_re, abar_im = mag * jnp.cos(lam_im * dt), mag * jnp.sin(lam_im * dt)
    den = lam_re * lam_re + lam_im * lam_im
    pr, pi = abar_re - 1.0, abar_im
    coef_re = (pr * lam_re + pi * lam_im) / den
    coef_im = (pi * lam_re - pr * lam_im) / den
    bbar_re = coef_re[..., None] * b_re - coef_im[..., None] * b_im
    bbar_im = coef_re[..., None] * b_im + coef_im[..., None] * b_re
    return abar_re, abar_im, bbar_re, bbar_im


def _block_diag(m):
    g, a, b = m.shape
    eye = jnp.eye(g, dtype=m.dtype)
    return (eye[:, None, :, None] * m[:, :, None, :]).reshape(g * a, g * b)


def _route_plan(route, tm):
    n_tok = route.shape[1]
    n_assign = n_tok * TOP_K
    flat_e = route[0:TOP_K].astype(jnp.int32).reshape(-1)
    onehot = (jnp.arange(N_EXPERTS, dtype=jnp.int32)[:, None] == flat_e[None, :]).astype(jnp.int32)
    csum = jnp.cumsum(onehot, axis=1)
    counts = csum[:, -1]
    padded = (counts + tm - 1) // tm * tm
    pend = jnp.cumsum(padded)
    pstart = pend - padded
    dest = jnp.sum(onehot * (csum - 1 + pstart[:, None]), axis=0)
    n_blocks = -(-n_assign // tm) + N_EXPERTS
    block_start = jnp.arange(n_blocks, dtype=jnp.int32) * tm
    block_expert = jnp.minimum(jnp.searchsorted(pend, block_start, side='right'), N_EXPERTS - 1).astype(jnp.int32)
    block_live = (block_start < (pstart + counts)[block_expert]).astype(jnp.int32)
    return n_blocks * tm, block_expert, block_live, dest.reshape(TOP_K, n_tok)


def kernel(x, c, ctx, c_ctx, w_ada, b_ada, norm1, norm2, w_in, w_out, gla_wa2, gla_ba, gla_norm, s5_lam_re, s5_lam_im, s5_log_dt, s5_b_re, s5_b_im, s5_c_re, s5_c_im, s5_d, s5_glu_w, s5_glu_b, ml_conv_w, ml_conv_b, ml_gate_b, ml_norm, ffn_w1, ffn_w3, ffn_w2, moe_router, moe_w1, moe_w3, moe_w2, norm_f):
    nb, n_lat, d = x.shape
    lc = ctx.shape[1]
    depth = w_ada.shape[0]
    l = lc + n_lat
    assert lc % SEG == 0 and n_lat % SEG == 0 and nb == 8 and nb % SCAN_BATCHES == 0
    rows = _Rows(nb, lc // SEG, l // SEG)

    h = _embed(ctx.reshape(nb * lc, d), x.reshape(nb * n_lat, d), _pos_embed(n_lat, d), rows)

    cond = jnp.zeros((16, d), F32).at[:nb].set(c).at[nb].set(c_ctx)
    mod = _modulation(cond, w_ada, b_ada)
    modtab = jnp.pad(mod.reshape(depth, 16, 6, d), ((0, 0), (0, 0), (0, 2), (0, 0)))

    dk, dv, dh = GLA_HEADS * GLA_DK, GLA_HEADS * GLA_DV, ML_HEADS * ML_DH
    s5c = s5_d.shape[-1]
    cuts = np.cumsum([dk, dk, dv, GLA_RANK, dv, s5c, dh, dh, dh, dh, 4 * ML_HEADS])

    for i in range(depth):
        last = i == depth - 1
        gq, gk, gv, glr, gg, su, mq, mk, mv, mo, mg = jnp.split(w_in[i], cuts[:-1], axis=-1)
        w_all = jnp.concatenate([
            _pad_heads(gq, GLA_HEADS, GLA_DK, GLA_KP), _pad_heads(gk, GLA_HEADS, GLA_DK, GLA_KP),
            _pad_heads(gv, GLA_HEADS, GLA_DV), _pad_heads(gg, GLA_HEADS, GLA_DV), _pad_last(glr, LANE),
            su,
            _pad_heads(mv, ML_HEADS, ML_DH), _pad_heads(mo, ML_HEADS, ML_DH),
            _pad_last(mg[:, :2 * ML_HEADS], LANE), _pad_last(mg[:, 2 * ML_HEADS:], LANE),
            _pad_heads(mq, ML_HEADS, ML_DH), _pad_heads(mk, ML_HEADS, ML_DH)],
            axis=-1).astype(BF16)
        wgt = mg.T.astype(BF16)
        cw = jnp.concatenate([_pad_heads(ml_conv_w[i][:, :dh], ML_HEADS, ML_DH),
                              _pad_heads(ml_conv_w[i][:, dh:], ML_HEADS, ML_DH)], axis=-1)
        cw = jnp.pad(cw, ((0, 8 - ML_CONV), (0, 0)))
        cb = jnp.concatenate([_pad_heads(ml_conv_b[i][:dh], ML_HEADS, ML_DH),
                              _pad_heads(ml_conv_b[i][dh:], ML_HEADS, ML_DH)])[None]
        post = jnp.concatenate([jnp.ones((HP,), F32), jnp.full((HP,), ML_DH ** -0.5, F32)])[None]
        pg, pu, pm, qk, gt = _proj(h, modtab, norm1[i][None], w_all, wgt, cw, cb, post, layer=i, rows=rows)

        wa = jnp.pad(_pad_heads(gla_wa2[i], GLA_HEADS, GLA_DK, GLA_KP), ((0, 0), (0, LANE - GLA_RANK), (0, 0)))
        ba = _pad_heads(gla_ba[i], GLA_HEADS, GLA_DK, GLA_KP)[:, None, :]
        pg3 = pg.reshape(nb, l, NG)
        ogf = _gla(pg3, wa[0], ba[0], rows=rows, rev=False).reshape(nb * l, HP)
        ogb = _gla(pg3, wa[1], ba[1], rows=rows, rev=True).reshape(nb * l, HP)

        bres, bims, ares, aims = [], [], [], []
        for dr in (0, 1):
            a_re, a_im, b_re, b_im = _s5_discretise(s5_lam_re[i, dr], s5_lam_im[i, dr], s5_log_dt[i, dr],
                                                    s5_b_re[i], s5_b_im[i])
            bres.append(_block_diag(jnp.swapaxes(b_re, 1, 2)))
            bims.append(_block_diag(jnp.swapaxes(b_im, 1, 2)))
            ares.append(jnp.broadcast_to(a_re.reshape(1, -1), (nb, a_re.size)))
            aims.append(jnp.broadcast_to(a_im.reshape(1, -1), (nb, a_im.size)))
        cre = _block_diag(jnp.swapaxes(s5_c_re[i], 1, 2)).astype(BF16)
        cim = _block_diag(jnp.swapaxes(s5_c_im[i], 1, 2)).astype(BF16)
        ut = pu.reshape(nb, l, s5c).swapaxes(0, 1).reshape(l * nb, s5c)
        yt = _s5(ut, jnp.stack(bres).astype(BF16), jnp.stack(bims).astype(BF16), jnp.stack(ares), jnp.stack(aims),
                 cre, cim, nb=nb, nc=lc // S5_STEPS, nt=l // S5_STEPS)
        ys = yt.reshape(2, l, nb, s5c).swapaxes(1, 2).reshape(2, nb * l, s5c)

        gb = ml_gate_b[i].reshape(2, 2, ML_HEADS)
        gbr = _pad_last(gb.reshape(2, 1, 2 * ML_HEADS), LANE)
        gbt = _pad_last(jnp.broadcast_to(gb[..., None], (2, 2, ML_HEADS, CHUNK)), LANE).reshape(2, 2, HP)
        gtl = _pad_last(gt.reshape(2, 2, ML_HEADS, nb, l // CHUNK, CHUNK).transpose(3, 4, 0, 1, 2, 5),
                        LANE).reshape(nb, l // CHUNK, 2, 2, HP)
        qk3, pm3 = qk.reshape(nb, l, 2 * HP), pm.reshape(nb, l, NM)
        omf = _mlstm(qk3, pm3, gtl, gbr[0], gbt[0], rows=rows, rev=False).reshape(nb * l, HP)
        omb = _mlstm(qk3, pm3, gtl, gbr[1], gbt[1], rows=rows, rev=True).reshape(nb * l, HP)

        wo = w_out[i]
        wo_p = jnp.concatenate([
            jnp.pad(wo[:dv].reshape(GLA_HEADS, GLA_DV, d), ((0, 0), (0, LANE - GLA_DV), (0, 0))).reshape(HP, d),
            wo[dv:dv + s5c],
            jnp.pad(wo[dv + s5c:].reshape(ML_HEADS, ML_DH, d), ((0, 0), (0, LANE - ML_DH), (0, 0))).reshape(HP, d)],
            axis=0).astype(BF16)
        gn = jnp.tile(_pad_last(gla_norm[i], LANE), GLA_HEADS)[None]
        mn = jnp.tile(_pad_last(ml_norm[i], LANE), ML_HEADS)[None]
        is_moe = i % 2 == 1
        j = i // 2
        wr = jnp.stack(_split_bf16(_pad_last(moe_router[j], LANE), 2)) if is_moe else None
        outs = _mix(ogf, ogb, pg, ys, pu, omf, omb, pm, h, modtab, gn, mn, s5_d[i][None], s5_glu_w[i].astype(BF16),
                    s5_glu_b[i][None], wo_p, norm2[i][None], wr, layer=i, rows=rows, lat_only=last)
        if not is_moe:
            h, f = outs
            h = _ffn(f, h, modtab, _to_bf16(ffn_w1, j), _to_bf16(ffn_w3, j), _to_bf16(ffn_w2, j),
                     norm_f[None], layer=i, rows=rows, lat_only=last)
        else:
            h, f, route = outs
            n_rows, block_expert, block_live, dest = _route_plan(route, MOE_TM)
            xg = _scatter_rows(f, dest, n_rows)
            yg = _moe_experts(block_expert, block_live, xg, _to_bf16(moe_w1, j), _to_bf16(moe_w3, j),
                              _to_bf16(moe_w2, j))
            y0 = _gather_rows(yg, dest[0])
            y1 = _gather_rows(yg, dest[1])
            h = _moe_resid(h, y0, y1, modtab, norm_f[None], layer=i, rows=rows, lat_only=last)
    return h.reshape(nb, n_lat, d)
```

```python
import functools
import math

import numpy as np
import jax
import jax.numpy as jnp
from jax import lax
from jax.experimental import pallas as pl
from jax.experimental.pallas import tpu as pltpu
from jax.experimental.pallas import tpu_sc as plsc

F32 = jnp.float32
BF16 = jnp.bfloat16
HIGHEST = lax.Precision.HIGHEST

GRID_W = 64
POS_BASE = 10000.0
EPS = 1e-6
GLA_HEADS, GLA_DK, GLA_DV, GLA_RANK, GLA_GATE_NORM = 4, 48, 96, 16, 16.0
S5_GROUP, S5_STATE = 16, 64
ML_HEADS, ML_DH, ML_CONV = 4, 96, 3
N_EXPERTS, TOP_K = 8, 2

LANE = 128
CHUNK = 64
SEG = 256
N_SUB = SEG // CHUNK
SCAN_BATCHES = 8
S5_STEPS = 128
S5_SUB = 32
NEG = -1e30
VMEM_LIMIT = 56 * 1024 * 1024

HP = LANE * GLA_HEADS
GLA_KP = 64
GQ = GLA_HEADS * GLA_KP
NG = 2 * GQ + 2 * HP + LANE
NM = 2 * HP + 2 * LANE


def _cparams(sem):
    return pltpu.CompilerParams(dimension_semantics=sem, vmem_limit_bytes=VMEM_LIMIT)


def _dot(a, b, **kw):
    return jnp.dot(a, b, preferred_element_type=F32, **kw)


def _dot_nt(a, b, **kw):
    return lax.dot_general(a, b, (((1,), (1,)), ((), ())), preferred_element_type=F32, **kw)


def _dot_tn(a, b, **kw):
    return lax.dot_general(a, b, (((0,), (0,)), ((), ())), preferred_element_type=F32, **kw)


def _log_sigmoid(x):
    return jnp.minimum(x, 0.0) - jnp.log1p(jnp.exp(-jnp.abs(x)))


def _silu(x):
    return x * jax.nn.sigmoid(x)


def _gelu_tanh(x):
    return 0.5 * x * (1.0 + jnp.tanh(math.sqrt(2.0 / math.pi) * (x + 0.044715 * (x * x * x))))


def _rmsnorm(x, g):
    return x * lax.rsqrt(jnp.mean(x * x, axis=-1, keepdims=True) + EPS) * g


class _Rows:
    def __init__(self, nb, ncb, ntb):
        self.nb, self.ncb, self.ntb, self.nlb = nb, ncb, ntb, ntb - ncb

    def n_blocks(self, lat_only):
        return self.nb * (self.nlb if lat_only else self.ntb)

    def src(self, lat_only):
        if lat_only:
            return lambda i: (i // self.nlb) * self.ntb + self.ncb + i % self.nlb
        return lambda i: i

    def sel(self, lat_only):
        if lat_only:
            return lambda i: i // self.nlb
        return lambda i: jnp.where(i % self.ntb < self.ncb, self.nb, i // self.ntb)


def _embed_kernel(ncb, ntb, ctx_ref, x_ref, pos_ref, o_ref):
    j = pl.program_id(0) % ntb

    @pl.when(j < ncb)
    def _():
        o_ref[...] = ctx_ref[...]

    @pl.when(j >= ncb)
    def _():
        o_ref[...] = x_ref[...] + pos_ref[...]


def _embed(ctx2, x2, pos, rows):
    d = ctx2.shape[1]
    ncb, ntb, nlb = rows.ncb, rows.ntb, rows.nlb
    return pl.pallas_call(
        functools.partial(_embed_kernel, ncb, ntb),
        out_shape=jax.ShapeDtypeStruct((rows.nb * ntb * SEG, d), F32),
        grid=(rows.nb * ntb,),
        in_specs=[pl.BlockSpec((SEG, d), lambda i: ((i // ntb) * ncb + jnp.minimum(i % ntb, ncb - 1), 0)),
                  pl.BlockSpec((SEG, d), lambda i: ((i // ntb) * nlb + jnp.maximum(i % ntb - ncb, 0), 0)),
                  pl.BlockSpec((SEG, d), lambda i: (jnp.maximum(i % ntb - ncb, 0), 0))],
        out_specs=pl.BlockSpec((SEG, d), lambda i: (i, 0)),
        compiler_params=_cparams(("arbitrary",)),
        name="embed",
    )(ctx2, x2, pos)


def _mod_kernel(c_ref, w_ref, b_ref, o_ref):
    s = _silu(c_ref[...])
    o_ref[0] = _dot(s, w_ref[0], precision=HIGHEST) + b_ref[0]


def _modulation(cond, w_ada, b_ada):
    depth, d, n6 = w_ada.shape
    tn = n6 // 4
    n_rows = cond.shape[0]
    return pl.pallas_call(
        _mod_kernel,
        out_shape=jax.ShapeDtypeStruct((depth, n_rows, n6), F32),
        grid=(depth, n6 // tn),
        in_specs=[pl.BlockSpec((n_rows, d), lambda l, j: (0, 0)),
                  pl.BlockSpec((1, d, tn), lambda l, j: (l, 0, j)),
                  pl.BlockSpec((1, 1, tn), lambda l, j: (l, 0, j))],
        out_specs=pl.BlockSpec((1, n_rows, tn), lambda l, j: (l, 0, j)),
        compiler_params=_cparams(("arbitrary", "arbitrary")),
        name="modulation",
    )(cond, w_ada, b_ada.reshape(depth, 1, n6))


PROJ_TM = 2 * SEG


def _proj_kernel(ncb, ntb, sel, h_ref, hp_ref, hn_ref, mod_ref, g_ref, w_ref, wgt_ref, cw_ref, cb_ref, post_ref,
                 pg_ref, pu_ref, pm_ref, qk_ref, gt_ref):
    tm = h_ref.shape[0]
    n_seg = tm // SEG
    blk = [pl.program_id(0) * n_seg + q for q in range(n_seg)]
    mods = [mod_ref[sel(bq)] for bq in blk]
    act = lambda x, m: _rmsnorm(x, g_ref[...]) * (1.0 + m[1:2]) + m[0:1]
    a_seg = [act(h_ref[q * SEG:(q + 1) * SEG, :], mods[q]) for q in range(n_seg)]
    ab = jnp.concatenate(a_seg, axis=0).astype(BF16)
    c1, c2 = NG + 2 * LANE, NG + 2 * LANE + NM
    pg_ref[...] = _dot(ab, w_ref[:, 0:NG])
    pu_ref[...] = _dot(ab, w_ref[:, NG:c1])
    pm_ref[...] = _dot(ab, w_ref[:, c1:c2])
    gt_ref[...] = _dot_nt(wgt_ref[...], ab)
    ae = jnp.concatenate([act(hp_ref[...], mods[0])] + a_seg + [act(hn_ref[...], mods[-1])], axis=0)
    xe = _dot(ae.astype(BF16), w_ref[:, c2:])
    n_e = tm + 16
    row = lax.broadcasted_iota(jnp.int32, (tm, xe.shape[1]), 0)
    keep_prev = jnp.ones((tm, xe.shape[1]), F32)
    keep_next = keep_prev
    for q, bq in enumerate(blk):
        j = bq % ntb
        first = jnp.logical_or(j == 0, j == ncb).astype(F32)
        last = jnp.logical_or(j == ncb - 1, j == ntb - 1).astype(F32)
        keep_prev = jnp.where(row == q * SEG, 1.0 - first, keep_prev)
        keep_next = jnp.where(row == (q + 1) * SEG - 1, 1.0 - last, keep_next)
    xp = pltpu.roll(xe, 1, axis=0)[8:8 + tm] * keep_prev
    xn = pltpu.roll(xe, n_e - 1, axis=0)[8:8 + tm] * keep_next
    y = cw_ref[0:1] * xp + cw_ref[1:2] * xe[8:8 + tm] + cw_ref[2:3] * xn + cb_ref[...]
    qk_ref[...] = _silu(y) * post_ref[...]


def _proj(h, modtab, g, w, wgt, cw, cb, post, *, layer, rows):
    r, d = h.shape
    tm = PROJ_TM
    assert r % tm == 0
    t8 = tm // 8
    full = lambda a: pl.BlockSpec(a.shape, lambda i: (0,) * a.ndim)
    return pl.pallas_call(
        functools.partial(_proj_kernel, rows.ncb, rows.ntb, rows.sel(False)),
        out_shape=(jax.ShapeDtypeStruct((r, NG), F32), jax.ShapeDtypeStruct((r, 2 * LANE), F32),
                   jax.ShapeDtypeStruct((r, NM), F32), jax.ShapeDtypeStruct((r, 2 * HP), F32),
                   jax.ShapeDtypeStruct((16, r), F32)),
        grid=(r // tm,),
        in_specs=[pl.BlockSpec((tm, d), lambda i: (i, 0)),
                  pl.BlockSpec((8, d), lambda i: (jnp.maximum(i * t8 - 1, 0), 0)),
                  pl.BlockSpec((8, d), lambda i: (jnp.minimum((i + 1) * t8, r // 8 - 1), 0)),
                  pl.BlockSpec((None,) + modtab.shape[1:], lambda i: (layer, 0, 0, 0)),
                  full(g), full(w), full(wgt), full(cw), full(cb), full(post)],
        out_specs=(pl.BlockSpec((tm, NG), lambda i: (i, 0)),
                   pl.BlockSpec((tm, 2 * LANE), lambda i: (i, 0)),
                   pl.BlockSpec((tm, NM), lambda i: (i, 0)),
                   pl.BlockSpec((tm, 2 * HP), lambda i: (i, 0)),
                   pl.BlockSpec((16, tm), lambda i: (0, i))),
        compiler_params=_cparams(("arbitrary",)),
        name="proj",
    )(h, h, h, modtab, g, w, wgt, cw, cb, post)


def _conv_kernel(ncb, ntb, x_ref, prev_ref, next_ref, w_ref, b_ref, post_ref, o_ref):
    j = pl.program_id(0) % ntb
    x = x_ref[...]
    tm = x.shape[0]
    first = jnp.logical_or(j == 0, j == ncb).astype(F32)
    last = jnp.logical_or(j == ncb - 1, j == ntb - 1).astype(F32)
    row = lax.broadcasted_iota(jnp.int32, x.shape, 0)
    xp = jnp.where(row == 0, prev_ref[7:8, :] * (1.0 - first), pltpu.roll(x, 1, axis=0))
    xn = jnp.where(row == tm - 1, next_ref[0:1, :] * (1.0 - last), pltpu.roll(x, tm - 1, axis=0))
    y = w_ref[0:1] * xp + w_ref[1:2] * x + w_ref[2:3] * xn + b_ref[...]
    o_ref[...] = _silu(y) * post_ref[...]


def _conv(pm, w, b, post, *, rows):
    r = pm.shape[0]
    tm = SEG
    wq = 2 * HP
    t8 = tm // 8
    return pl.pallas_call(
        functools.partial(_conv_kernel, rows.ncb, rows.ntb),
        out_shape=jax.ShapeDtypeStruct((r, wq), F32),
        grid=(r // tm,),
        in_specs=[pl.BlockSpec((tm, wq), lambda i: (i, 0)),
                  pl.BlockSpec((8, wq), lambda i: (jnp.maximum(i * t8 - 1, 0), 0)),
                  pl.BlockSpec((8, wq), lambda i: (jnp.minimum((i + 1) * t8, r // 8 - 1), 0)),
                  pl.BlockSpec((8, wq), lambda i: (0, 0)),
                  pl.BlockSpec((1, wq), lambda i: (0, 0)),
                  pl.BlockSpec((1, wq), lambda i: (0, 0))],
        out_specs=pl.BlockSpec((tm, wq), lambda i: (i, 0)),
        compiler_params=_cparams(("arbitrary",)),
        name="conv",
    )(pm, pm, pm, w, b, post)


def _scan_pos(d, s, ncb, ntb):
    rev = jnp.where(s < ncb, ncb - 1 - s, ntb - 1 - (s - ncb))
    return jnp.where(d == 0, s, rev)


def _scan_pos_static(rev, s, ncb, ntb):
    if not rev:
        return s
    return jnp.where(s < ncb, ncb - 1 - s, ntb - 1 - (s - ncb))


def _tri(rev):
    r = lax.broadcasted_iota(jnp.int32, (CHUNK, CHUNK), 0)
    c = lax.broadcasted_iota(jnp.int32, (CHUNK, CHUNK), 1)
    return (r <= c) if rev else (r >= c)


def _chunk_rows(rev):
    return [(N_SUB - 1 - j if rev else j) * CHUNK for j in range(N_SUB)]


def _gla_kernel(rev, p_ref, wa_ref, ba_ref, o_ref, st_ref):
    s, g = pl.program_id(0), pl.program_id(1)
    nbb = p_ref.shape[0]
    b0 = g * nbb

    @pl.when(s == 0)
    def _():
        st_ref[pl.ds(b0, nbb)] = jnp.zeros((nbb,) + st_ref.shape[1:], F32)

    valid = _tri(rev)
    tri = valid.astype(F32)
    r0s = _chunk_rows(rev)
    wa, ba = wa_ref[...], ba_ref[...]
    inst = [(bb, j) for bb in range(nbb) for j in range(N_SUB)]
    heads = [slice(h * LANE, (h + 1) * LANE) for h in range(GLA_HEADS)]
    pairs = [slice((h // 2) * LANE, (h // 2 + 1) * LANE) for h in range(GLA_HEADS)]
    lane = lax.broadcasted_iota(jnp.int32, (CHUNK, LANE), 1)
    own = [(lane // GLA_KP) == (h % 2) for h in range(GLA_HEADS)]

    la = {}
    for bb, j in inst:
        lr = p_ref[bb, pl.ds(r0s[j], CHUNK), 2 * GQ + 2 * HP:NG]
        la[bb, j] = _log_sigmoid(_dot(lr, wa) + ba) * (1.0 / GLA_GATE_NORM)
    bc, e_last = {}, {}
    for i in inst:
        bc[i] = _dot_exact01(tri, la[i], lhs_is_01=True, pieces=2)
        e_last[i] = jnp.exp(jnp.sum(la[i], axis=0, keepdims=True))
    q_in, k_in, k_out, v = {}, {}, {}, {}
    for bb, j in inst:
        i = (bb, j)
        rs = pl.ds(r0s[j], CHUNK)
        qs = (p_ref[bb, rs, 0:GQ] * (GLA_DK ** -0.5) * jnp.exp(bc[i])).astype(BF16)
        for h in range(GLA_HEADS):
            q_in[i, h] = jnp.where(own[h], qs[:, pairs[h]], jnp.zeros_like(qs[:, pairs[h]]))
        kd = p_ref[bb, rs, GQ:2 * GQ] * jnp.exp(-bc[i])
        k_out[i] = (kd * e_last[i]).astype(BF16)
        k_in[i] = kd.astype(BF16)
        v[i] = p_ref[bb, rs, 2 * GQ:2 * GQ + HP].astype(BF16)
    att = {}
    for i in inst:
        for h in range(GLA_HEADS):
            att[i, h] = jnp.where(valid, _dot_nt(q_in[i, h], k_in[i][:, pairs[h]]), 0.0).astype(BF16)
    o_intra, ds = {}, {}
    for i in inst:
        for h, sl in enumerate(heads):
            o_intra[i, h] = _dot(att[i, h], v[i][:, sl])
            ds[i, h] = _dot_tn(v[i][:, sl], k_out[i][:, pairs[h]])
    s_in = {}
    for bb in range(nbb):
        for h in range(GLA_HEADS):
            st = st_ref[b0 + bb, h]
            for j in range(N_SUB):
                s_in[(bb, j), h] = st.astype(BF16)
                st = st * e_last[bb, j][:, pairs[h]] + ds[(bb, j), h]
            st_ref[b0 + bb, h] = st
    for bb, j in inst:
        for h, sl in enumerate(heads):
            o = o_intra[(bb, j), h] + _dot_nt(q_in[(bb, j), h], s_in[(bb, j), h])
            o_ref[bb, pl.ds(r0s[j], CHUNK), sl] = o.astype(o_ref.dtype)


def _gla(pg3, wa, ba, *, rows, rev):
    nb, l, _ = pg3.shape
    nbb = SCAN_BATCHES
    pos = functools.partial(_scan_pos_static, rev, ncb=rows.ncb, ntb=rows.ntb)
    return pl.pallas_call(
        functools.partial(_gla_kernel, rev),
        out_shape=jax.ShapeDtypeStruct((nb, l, HP), BF16),
        grid=(rows.ntb, nb // nbb),
        in_specs=[pl.BlockSpec((nbb, SEG, NG), lambda s, g: (g, pos(s), 0)),
                  pl.BlockSpec((LANE, GQ), lambda s, g: (0, 0)),
                  pl.BlockSpec((1, GQ), lambda s, g: (0, 0))],
        out_specs=pl.BlockSpec((nbb, SEG, HP), lambda s, g: (g, pos(s), 0)),
        scratch_shapes=[pltpu.VMEM((nb, GLA_HEADS, LANE, LANE), F32)],
        compiler_params=_cparams(("arbitrary", "arbitrary")),
        name="gla_scan_bwd" if rev else "gla_scan_fwd",
    )(pg3, wa, ba)


def _mlstm_kernel_old(qk_ref, v_ref, g_ref, gt_ref, gbr_ref, gbc_ref, o_ref, st_ref, m_ref):
    d, s, g = pl.program_id(0), pl.program_id(1), pl.program_id(2)
    nbb = qk_ref.shape[0]
    b0 = g * nbb

    @pl.when(s == 0)
    def _():
        st_ref[pl.ds(b0, nbb)] = jnp.zeros((nbb,) + st_ref.shape[1:], F32)
        m_ref[pl.ds(b0, nbb)] = jnp.zeros((nbb,) + m_ref.shape[1:], F32)

    fwd = d == 0
    valid = _tri(d)
    tri = valid.astype(F32)
    r0s = _chunk_rows(d)
    cidx = [jnp.where(fwd, j, N_SUB - 1 - j) for j in range(N_SUB)]
    lane = lax.broadcasted_iota(jnp.int32, (CHUNK, LANE), 1)
    inst = [(bb, j) for bb in range(nbb) for j in range(N_SUB)]
    hinst = [(bb, j, h) for bb, j in inst for h in range(ML_HEADS)]
    gbr, gbc = gbr_ref[...], gbc_ref[...]

    def pick(a, h, kind, axis):
        i0, i1 = kind * ML_HEADS + h, (2 + kind) * ML_HEADS + h
        if axis == 1:
            return jnp.where(fwd, a[:, i0:i0 + 1], a[:, i1:i1 + 1])
        return jnp.where(fwd, a[i0:i0 + 1, :], a[i1:i1 + 1, :])

    gc, gr, fcum_c, fcum_r = {}, {}, {}, {}
    for bb, j in inst:
        i = (bb, j)
        gc[i] = g_ref[bb, pl.ds(r0s[j], CHUNK), :] + gbr
        gr[i] = gt_ref[bb, cidx[j]] + gbc
    for i in inst:
        fcum_c[i] = _dot(tri, _log_sigmoid(gc[i]), precision=HIGHEST)
        fcum_r[i] = _dot_nt(_log_sigmoid(gr[i]), tri, precision=HIGHEST)
    fc, lic, d_log, rmax = {}, {}, {}, {}
    for bb, j, h in hinst:
        i = (bb, j)
        fc[bb, j, h] = pick(fcum_c[i], h, 1, 1)
        lic[bb, j, h] = pick(gc[i], h, 0, 1)
        dl = jnp.where(valid, fc[bb, j, h] - pick(fcum_r[i], h, 1, 0) + pick(gr[i], h, 0, 0), NEG)
        d_log[bb, j, h] = dl
        rmax[bb, j, h] = jnp.max(dl, axis=-1, keepdims=True)
    m_prev, m_t, m_new, f_tot = {}, {}, {}, {}
    for bb in range(nbb):
        for h in range(ML_HEADS):
            m = m_ref[b0 + bb, h:h + 1, 0:1]
            for j in range(N_SUB):
                i = (bb, j, h)
                m_prev[i] = m
                m_t[i] = jnp.maximum(fc[i] + m, rmax[i])
                m = jnp.where(fwd, m_t[i][CHUNK - 1:CHUNK], m_t[i][0:1])
                m_new[i] = m
                f_tot[i] = jnp.where(fwd, fc[i][CHUNK - 1:CHUNK], fc[i][0:1])
            m_ref[b0 + bb, h:h + 1, :] = jnp.broadcast_to(m, (1, LANE))
    q, v, qk, w_prev, gdec, ds = {}, {}, {}, {}, {}, {}
    for bb, j, h in hinst:
        i = (bb, j, h)
        rs = pl.ds(r0s[j], CHUNK)
        sl = slice(h * LANE, (h + 1) * LANE)
        q[i] = qk_ref[bb, rs, sl].astype(BF16)
        k = qk_ref[bb, rs, HP + h * LANE:HP + (h + 1) * LANE]
        v[i] = jnp.where(lane == ML_DH, 1.0, v_ref[bb, rs, sl]).astype(BF16)
        w = jnp.exp(d_log[i] - m_t[i])
        w_prev[i] = jnp.exp(fc[i] + m_prev[i] - m_t[i])
        qk[i] = (_dot_nt(q[i], k.astype(BF16)) * w).astype(BF16)
        w_s = jnp.exp(f_tot[i] - fc[i] + lic[i] - m_new[i])
        gdec[i] = jnp.exp(f_tot[i] + m_prev[i] - m_new[i])
        ds[i] = _dot_tn(v[i], (k * w_s).astype(BF16))
    s_in = {}
    for bb in range(nbb):
        for h in range(ML_HEADS):
            st = st_ref[b0 + bb, h]
            for j in range(N_SUB):
                i = (bb, j, h)
                s_in[i] = st.astype(BF16)
                st = gdec[i] * st + ds[i]
            st_ref[b0 + bb, h] = st
    for bb, j, h in hinst:
        i = (bb, j, h)
        num = w_prev[i] * _dot_nt(q[i], s_in[i]) + _dot(qk[i], v[i])
        den = num[:, ML_DH:ML_DH + 1]
        hh = num / jnp.maximum(jnp.abs(den), jnp.exp(-m_t[i]))
        o_ref[0, bb, pl.ds(r0s[j], CHUNK), h * LANE:(h + 1) * LANE] = jnp.where(lane < ML_DH, hh, 0.0)


def _mlstm_old(qk3, pm3, gt4, gbr, gbc, *, rows):
    nb, l, _ = qk3.shape
    nbb = SCAN_BATCHES
    pos = functools.partial(_scan_pos, ncb=rows.ncb, ntb=rows.ntb)
    return pl.pallas_call(
        _mlstm_kernel,
        out_shape=jax.ShapeDtypeStruct((2, nb, l, HP), F32),
        grid=(2, rows.ntb, nb // nbb),
        in_specs=[pl.BlockSpec((nbb, SEG, 2 * HP), lambda d, s, g: (g, pos(d, s), 0)),
                  pl.BlockSpec((nbb, SEG, HP), lambda d, s, g: (g, pos(d, s), 2)),
                  pl.BlockSpec((nbb, SEG, LANE), lambda d, s, g: (g, pos(d, s), 4 * GLA_HEADS)),
                  pl.BlockSpec((nbb, N_SUB, 16, CHUNK), lambda d, s, g: (g, pos(d, s), 0, 0)),
                  pl.BlockSpec((1, LANE), lambda d, s, g: (0, 0)),
                  pl.BlockSpec((16, CHUNK), lambda d, s, g: (0, 0))],
        out_specs=pl.BlockSpec((1, nbb, SEG, HP), lambda d, s, g: (d, g, pos(d, s), 0)),
        scratch_shapes=[pltpu.VMEM((nb, ML_HEADS, LANE, LANE), F32), pltpu.VMEM((nb, 8, LANE), F32)],
        compiler_params=_cparams(("arbitrary", "arbitrary", "arbitrary")),
        name="mlstm_scan",
    )(qk3, pm3, pm3, gt4, gbr, gbc)


def _split_bf16(x, n):
    parts, r = [], x
    for _ in range(n):
        p = r.astype(BF16)
        parts.append(p)
        r = r - p.astype(F32)
    return parts


def _dot_exact01(a, b, lhs_is_01, pieces=3):
    if lhs_is_01:
        a = a.astype(BF16)
        terms = [_dot(a, p) for p in _split_bf16(b, pieces)]
    else:
        b = b.astype(BF16)
        terms = [_dot(p, b) for p in _split_bf16(a, pieces)]
    return functools.reduce(lambda x, y: x + y, terms)


def _cummax_rows(a, rev):
    n = a.shape[0]
    row = lax.broadcasted_iota(jnp.int32, a.shape, 0)
    k = 1
    while k < n:
        if rev:
            sh = jnp.where(row < n - k, pltpu.roll(a, n - k, axis=0), NEG)
        else:
            sh = jnp.where(row >= k, pltpu.roll(a, k, axis=0), NEG)
        a = jnp.maximum(a, sh)
        k *= 2
    return a


ML_GL = ML_HEADS


def _mlstm_kernel(rev, qk_ref, v_ref, g_ref, gt_ref, gbr_ref, gbt_ref, o_ref, st_ref, m_ref):
    s, g = pl.program_id(0), pl.program_id(1)
    nbb = qk_ref.shape[0]
    b0 = g * nbb

    @pl.when(s == 0)
    def _():
        st_ref[pl.ds(b0, nbb)] = jnp.zeros((nbb,) + st_ref.shape[1:], F32)
        m_ref[pl.ds(b0, nbb)] = jnp.zeros((nbb,) + m_ref.shape[1:], F32)

    valid = _tri(rev)
    tri = valid.astype(F32)
    r0s = _chunk_rows(rev)
    cs = [r // CHUNK for r in r0s]
    last = 0 if rev else CHUNK - 1
    inst = [(bb, j) for bb in range(nbb) for j in range(N_SUB)]
    heads = [slice(h * LANE, (h + 1) * LANE) for h in range(ML_HEADS)]

    r_sel = lax.broadcasted_iota(jnp.int32, (LANE, HP), 0)
    c_sel = lax.broadcasted_iota(jnp.int32, (LANE, HP), 1)
    sel_h = (r_sel == ML_GL + c_sel // LANE).astype(BF16)
    r_t = lax.broadcasted_iota(jnp.int32, (HP, HP), 0)
    c_t = lax.broadcasted_iota(jnp.int32, (HP, HP), 1)
    same = jnp.logical_and(r_t // LANE == c_t // LANE, jnp.logical_and(r_t % LANE < CHUNK, c_t % LANE < CHUNK))
    before = (r_t % LANE >= c_t % LANE) if rev else (r_t % LANE <= c_t % LANE)
    tri_b = jnp.logical_and(same, before).astype(BF16)
    r_v = lax.broadcasted_iota(jnp.int32, (CHUNK, HP), 0)
    c_v = lax.broadcasted_iota(jnp.int32, (CHUNK, HP), 1) % LANE
    valid4 = jnp.logical_and(c_v < CHUNK, (r_v <= c_v) if rev else (r_v >= c_v))
    lane4 = lax.broadcasted_iota(jnp.int32, (CHUNK, HP), 1) % LANE
    lane_c = lax.broadcasted_iota(jnp.int32, (CHUNK, LANE), 1)
    lane1 = lax.broadcasted_iota(jnp.int32, (1, LANE), 1)
    head_lane = jnp.logical_and(lane1 >= ML_GL, lane1 < ML_GL + ML_HEADS)
    gbr, gbt = gbr_ref[...], gbt_ref[...]

    gcs, fcm, cmx, a_row, grt = {}, {}, {}, {}, {}
    for bb, j in inst:
        gc = g_ref[bb, pl.ds(r0s[j], CHUNK), :] + gbr
        gcs[bb, j] = pltpu.roll(gc, ML_GL, axis=1)
        fcm[bb, j] = _dot_exact01(tri, _log_sigmoid(gc), lhs_is_01=True)
        grt[bb, j] = gt_ref[bb, cs[j]] + gbt
    row_id = lax.broadcasted_iota(jnp.int32, (len(inst), HP), 0)
    lfr = jnp.zeros((len(inst), HP), F32)
    for n, i in enumerate(inst):
        lfr = jnp.where(row_id == n, _log_sigmoid(grt[i][1:2]), lfr)
    fcr = _dot_exact01(lfr, tri_b, lhs_is_01=False)
    for n, i in enumerate(inst):
        a_row[i] = grt[i][0:1] - fcr[n:n + 1]
        cmx[i] = _cummax_rows(gcs[i] - fcm[i], rev)
    bx = {}
    e_neg, gd = {}, {}
    for bb in range(nbb):
        m_prev = m_ref[b0 + bb, 0:1, :]
        for j in range(N_SUB):
            i = (bb, j)
            m_t = fcm[i] + jnp.maximum(m_prev, cmx[i])
            m_new = m_t[last:last + 1]
            f_tot = fcm[i][last:last + 1]
            u = fcm[i] - m_t
            w_prev = jnp.exp(u + m_prev)
            w_s = jnp.exp(f_tot - fcm[i] + gcs[i] - m_new)
            gdec = jnp.broadcast_to(jnp.exp(f_tot + m_prev - m_new), (16, LANE))
            e_neg[i] = jnp.exp(-m_t)
            keep = lambda a: jnp.where(head_lane, a, 0.0)
            bx[i] = jnp.concatenate(_split_bf16(keep(u), 2) + _split_bf16(keep(w_prev), 1)
                                    + _split_bf16(keep(w_s), 1) + _split_bf16(keep(gdec), 2), axis=0)
            m_prev = m_new
        m_ref[b0 + bb] = jnp.broadcast_to(m_prev, (8, LANE))
    ub, wpb, wsb, gdb = {}, {}, {}, {}
    for i in inst:
        y = _dot(bx[i], sel_h)
        c = CHUNK
        ub[i] = y[0:c] + y[c:2 * c]
        wpb[i] = y[2 * c:3 * c]
        wsb[i] = y[3 * c:4 * c]
        gdb[i] = y[4 * c:4 * c + 1] + y[4 * c + 16:4 * c + 17]
    q, v, qkw, ds = {}, {}, {}, {}
    for bb, j in inst:
        i = (bb, j)
        rs = pl.ds(r0s[j], CHUNK)
        w = jnp.where(valid4, jnp.exp(ub[i] + a_row[i]), 0.0)
        q[i] = qk_ref[bb, rs, 0:HP].astype(BF16)
        k = qk_ref[bb, rs, HP:2 * HP]
        kb = k.astype(BF16)
        kw = (k * wsb[i]).astype(BF16)
        v[i] = jnp.where(lane4 == ML_DH, 1.0, v_ref[bb, rs, :]).astype(BF16)
        for h, sl in enumerate(heads):
            sc = _dot_nt(q[i][:, sl], kb[:, sl])
            qkw[i, h] = (sc * w[:, h * LANE:h * LANE + CHUNK]).astype(BF16)
            ds[i, h] = _dot_tn(v[i][:, sl], kw[:, sl])
    s_in = {}
    for bb in range(nbb):
        for h, sl in enumerate(heads):
            st = st_ref[b0 + bb, h]
            for j in range(N_SUB):
                i = (bb, j)
                s_in[i, h] = st.astype(BF16)
                st = gdb[i][:, sl] * st + ds[i, h]
            st_ref[b0 + bb, h] = st
    num = {}
    for i in inst:
        parts = [_dot_nt(q[i][:, sl], s_in[i, h]) for h, sl in enumerate(heads)]
        intra = [_dot(qkw[i, h], v[i][:, sl]) for h, sl in enumerate(heads)]
        num[i] = wpb[i] * jnp.concatenate(parts, axis=1) + jnp.concatenate(intra, axis=1)
    for bb, j in inst:
        i = (bb, j)
        den = jnp.zeros((CHUNK, LANE), F32)
        for h, sl in enumerate(heads):
            dh = jnp.sum(jnp.where(lane_c == ML_DH, num[i][:, sl], 0.0), axis=-1, keepdims=True)
            den = jnp.where(lane_c == ML_GL + h, jnp.broadcast_to(dh, (CHUNK, LANE)), den)
        r1, r2 = _split_bf16(jnp.where(head_lane, 1.0 / jnp.maximum(jnp.abs(den), e_neg[i]), 0.0), 2)
        rb = _dot(jnp.concatenate([r1, r2], axis=0), sel_h)
        rb = rb[0:CHUNK] + rb[CHUNK:2 * CHUNK]
        o_ref[bb, pl.ds(r0s[j], CHUNK), :] = jnp.where(lane4 < ML_DH, num[i] * rb, 0.0).astype(o_ref.dtype)


def _mlstm(qk3, pm3, gtl, gbr, gbt, *, rows, rev):
    nb, l, _ = qk3.shape
    nbb = SCAN_BATCHES
    dr = 1 if rev else 0
    pos = functools.partial(_scan_pos_static, rev, ncb=rows.ncb, ntb=rows.ntb)
    return pl.pallas_call(
        functools.partial(_mlstm_kernel, rev),
        out_shape=jax.ShapeDtypeStruct((nb, l, HP), BF16),
        grid=(rows.ntb, nb // nbb),
        in_specs=[pl.BlockSpec((nbb, SEG, 2 * HP), lambda s, g: (g, pos(s), 0)),
                  pl.BlockSpec((nbb, SEG, HP), lambda s, g: (g, pos(s), 0)),
                  pl.BlockSpec((nbb, SEG, LANE), lambda s, g: (g, pos(s), 2 * ML_HEADS + dr)),
                  pl.BlockSpec((nbb, N_SUB, None, 2, HP), lambda s, g: (g, pos(s), dr, 0, 0)),
                  pl.BlockSpec((1, LANE), lambda s, g: (0, 0)),
                  pl.BlockSpec((2, HP), lambda s, g: (0, 0))],
        out_specs=pl.BlockSpec((nbb, SEG, HP), lambda s, g: (g, pos(s), 0)),
        scratch_shapes=[pltpu.VMEM((nb, ML_HEADS, LANE, LANE), F32), pltpu.VMEM((nb, 8, LANE), F32)],
        compiler_params=_cparams(("arbitrary", "arbitrary")),
        name="mlstm_scan_bwd" if rev else "mlstm_scan_fwd",
    )(qk3, pm3, pm3, gtl, gbr, gbt)


def _s5_kernel(nb, u_ref, bre_ref, bim_ref, are_ref, aim_ref, cre_ref, cim_ref, o_ref, xr_ref, xi_ref, st_ref):
    d, s = pl.program_id(0), pl.program_id(1)

    @pl.when(s == 0)
    def _():
        st_ref[...] = jnp.zeros(st_ref.shape, F32)

    n_sub = u_ref.shape[0] // (S5_SUB * nb)
    sub_rows = S5_SUB * nb

    def run(rev):
        ar, ai = are_ref[0], aim_ref[0]
        order = list(range(n_sub))[::-1] if rev else list(range(n_sub))

        def project_in(q):
            rs = slice(q * sub_rows, (q + 1) * sub_rows)
            u = u_ref[rs, :].astype(BF16)
            xr_ref[rs, :] = _dot(u, bre_ref[0])
            xi_ref[rs, :] = _dot(u, bim_ref[0])

        sr, si = st_ref[0], st_ref[1]
        project_in(order[0])
        for n, q in enumerate(order):
            if n + 1 < n_sub:
                project_in(order[n + 1])
            for j in range(S5_SUB):
                r0 = (q * S5_SUB + (S5_SUB - 1 - j if rev else j)) * nb
                nr = ar * sr - ai * si + xr_ref[r0:r0 + nb, :]
                ni = ar * si + ai * sr + xi_ref[r0:r0 + nb, :]
                xr_ref[r0:r0 + nb, :] = nr
                xi_ref[r0:r0 + nb, :] = ni
                sr, si = nr, ni
            rs = slice(q * sub_rows, (q + 1) * sub_rows)
            o_ref[0, rs, :] = (_dot(xr_ref[rs, :].astype(BF16), cre_ref[...])
                               - _dot(xi_ref[rs, :].astype(BF16), cim_ref[...]))
        st_ref[0] = sr
        st_ref[1] = si

    @pl.when(d == 0)
    def _():
        run(False)

    @pl.when(d == 1)
    def _():
        run(True)


def _s5(ut, bre, bim, are, aim, cre, cim, *, nb, nc, nt):
    n_rows, ch = ut.shape
    tr = S5_STEPS * nb
    ns = bre.shape[-1]
    pos = functools.partial(_scan_pos, ncb=nc, ntb=nt)
    return pl.pallas_call(
        functools.partial(_s5_kernel, nb),
        out_shape=jax.ShapeDtypeStruct((2, n_rows, ch), F32),
        grid=(2, nt),
        in_specs=[pl.BlockSpec((tr, ch), lambda d, s: (pos(d, s), 0)),
                  pl.BlockSpec((1, ch, ns), lambda d, s: (d, 0, 0)),
                  pl.BlockSpec((1, ch, ns), lambda d, s: (d, 0, 0)),
                  pl.BlockSpec((1, nb, ns), lambda d, s: (d, 0, 0)),
                  pl.BlockSpec((1, nb, ns), lambda d, s: (d, 0, 0)),
                  pl.BlockSpec((ns, ch), lambda d, s: (0, 0)),
                  pl.BlockSpec((ns, ch), lambda d, s: (0, 0))],
        out_specs=pl.BlockSpec((1, tr, ch), lambda d, s: (d, pos(d, s), 0)),
        scratch_shapes=[pltpu.VMEM((tr, ns), F32), pltpu.VMEM((tr, ns), F32), pltpu.VMEM((2, nb, ns), F32)],
        compiler_params=_cparams(("arbitrary", "arbitrary")),
        name="s5_scan",
    )(ut, bre, bim, are, aim, cre, cim)


def _head_norm(o, gain, dim):
    parts = []
    for h in range(o.shape[1] // LANE):
        seg = o[:, h * LANE:(h + 1) * LANE]
        ms = jnp.sum(seg * seg, axis=-1, keepdims=True) * (1.0 / dim)
        parts.append(seg * lax.rsqrt(ms + EPS))
    return jnp.concatenate(parts, axis=1) * gain


def _mix_kernel(with_router, ogf_ref, ogb_ref, gg_ref, ys_ref, u_ref, omf_ref, omb_ref, mo_ref, h_ref, mod_ref,
                gn_ref, mn_ref, sd_ref, gw_ref, gb_ref, wo_ref, n2_ref, *rest):
    if with_router:
        wr_ref, ho_ref, f_ref, rt_ref = rest
    else:
        ho_ref, f_ref = rest
    gla = _head_norm(ogf_ref[...].astype(F32) + ogb_ref[...].astype(F32), gn_ref[...], GLA_DV) * _silu(gg_ref[...])
    z = _gelu_tanh(ys_ref[0] + ys_ref[1] + sd_ref[...] * u_ref[...])
    s5 = z * jax.nn.sigmoid(_dot(z.astype(BF16), gw_ref[...]) + gb_ref[...])
    ml = _head_norm(omf_ref[...].astype(F32) + omb_ref[...].astype(F32), mn_ref[...], ML_DH) * jax.nn.sigmoid(mo_ref[...])
    mix = (_dot(gla.astype(BF16), wo_ref[0:HP]) + _dot(s5.astype(BF16), wo_ref[HP:HP + 2 * LANE])
           + _dot(ml.astype(BF16), wo_ref[HP + 2 * LANE:]))
    m = mod_ref[0]
    hn = h_ref[...] + m[2:3] * mix
    ho_ref[...] = hn
    f = _rmsnorm(hn, n2_ref[...]) * (1.0 + m[4:5]) + m[3:4]
    if not with_router:
        f_ref[...] = f.astype(f_ref.dtype)
    if with_router:
        f_hi, f_lo = _split_bf16(f, 2)
        logits = (_dot(f_hi, wr_ref[0]) + _dot(f_lo, wr_ref[0]) + _dot(f_hi, wr_ref[1])).T[0:2 * N_EXPERTS]
        row = lax.broadcasted_iota(jnp.int32, logits.shape, 0)
        l0 = jnp.where(row < N_EXPERTS, logits, NEG)
        m1 = jnp.max(l0, axis=0, keepdims=True)
        i1 = jnp.min(jnp.where(l0 == m1, row, 2 * N_EXPERTS), axis=0, keepdims=True)
        l1 = jnp.where(row == i1, NEG, l0)
        m2 = jnp.max(l1, axis=0, keepdims=True)
        i2 = jnp.min(jnp.where(l1 == m2, row, 2 * N_EXPERTS), axis=0, keepdims=True)
        e = jnp.exp(m2 - m1)
        w1 = 1.0 / (1.0 + e)
        w2 = e / (1.0 + e)
        rt = jnp.where(row == 0, i1.astype(F32),
                       jnp.where(row == 1, i2.astype(F32),
                                 jnp.where(row == 2, w1, jnp.where(row == 3, w2, 0.0))))
        rt_ref[...] = rt
        d = f.shape[1]
        f_ref[:, 0:d] = f
        f_ref[:, d:] = jnp.concatenate([rt, jnp.zeros((LANE - rt.shape[0], rt.shape[1]), F32)], axis=0).T


def _mix(ogf, ogb, pg, ys, pu, omf, omb, pm, h, modtab, gn, mn, sd, gw, gb, wo, n2, wr, *, layer, rows, lat_only):
    d = h.shape[1]
    tm = SEG
    n = rows.n_blocks(lat_only)
    src, sel = rows.src(lat_only), rows.sel(lat_only)
    full = lambda a: pl.BlockSpec(a.shape, lambda i: (0,) * a.ndim)
    with_router = wr is not None
    in_specs = [pl.BlockSpec((tm, HP), lambda i: (src(i), 0)),
                pl.BlockSpec((tm, HP), lambda i: (src(i), 0)),
                pl.BlockSpec((tm, HP), lambda i: (src(i), (2 * GQ + HP) // HP)),
                pl.BlockSpec((2, tm, 2 * LANE), lambda i: (0, src(i), 0)),
                pl.BlockSpec((tm, 2 * LANE), lambda i: (src(i), 0)),
                pl.BlockSpec((tm, HP), lambda i: (src(i), 0)),
                pl.BlockSpec((tm, HP), lambda i: (src(i), 0)),
                pl.BlockSpec((tm, HP), lambda i: (src(i), 1)),
                pl.BlockSpec((tm, d), lambda i: (src(i), 0)),
                pl.BlockSpec((None, 1, 8, d), lambda i: (layer, sel(i), 0, 0)),
                full(gn), full(mn), full(sd), full(gw), full(gb), full(wo), full(n2)]
    args = [ogf, ogb, pg, ys, pu, omf, omb, pm, h, modtab, gn, mn, sd, gw, gb, wo, n2]
    fw, fdt = (d + LANE, F32) if with_router else (d, BF16)
    out_shape = [jax.ShapeDtypeStruct((n * tm, d), F32), jax.ShapeDtypeStruct((n * tm, fw), fdt)]
    out_specs = [pl.BlockSpec((tm, d), lambda i: (i, 0)), pl.BlockSpec((tm, fw), lambda i: (i, 0))]
    if with_router:
        in_specs.append(full(wr))
        args.append(wr)
        out_shape.append(jax.ShapeDtypeStruct((2 * N_EXPERTS, n * tm), F32))
        out_specs.append(pl.BlockSpec((2 * N_EXPERTS, tm), lambda i: (0, i)))
    return pl.pallas_call(
        functools.partial(_mix_kernel, with_router),
        out_shape=tuple(out_shape),
        grid=(n,),
        in_specs=in_specs,
        out_specs=tuple(out_specs),
        compiler_params=_cparams(("arbitrary",)),
        name="mix_out",
    )(*args)


FF_TILE = 256


def _swiglu(xb, w1_ref, w3_ref, w2_ref, a_ref, lead=()):
    dff = w1_ref.shape[-1]
    for j in range(dff // FF_TILE):
        sl = slice(j * FF_TILE, (j + 1) * FF_TILE)
        h1 = _dot(xb, w1_ref[lead + (slice(None), sl)])
        h3 = _dot(xb, w3_ref[lead + (slice(None), sl)])
        a_ref[:, sl] = (_silu(h1) * h3).astype(BF16)
    return _dot(a_ref[...], w2_ref[lead + (slice(None), slice(None))])


FFN_TM = 1024


def _ffn_kernel(final, sel, f_ref, h_ref, mod_ref, w1_ref, w3_ref, w2_ref, *rest):
    if final:
        nf_ref, o_ref, a_ref = rest
    else:
        o_ref, a_ref = rest
    y = _swiglu(f_ref[...], w1_ref, w3_ref, w2_ref, a_ref)
    n_seg = f_ref.shape[0] // SEG
    for q in range(n_seg):
        rs = slice(q * SEG, (q + 1) * SEG)
        gate = mod_ref[sel(pl.program_id(0) * n_seg + q)][5:6]
        hn = h_ref[rs, :] + gate * y[rs]
        o_ref[rs, :] = _rmsnorm(hn, nf_ref[...]) if final else hn


def _ffn(f, h, modtab, w1, w3, w2, nf, *, layer, rows, lat_only):
    r, d = h.shape
    tm = FFN_TM
    assert r % tm == 0
    dff = w1.shape[-1]
    full = lambda a: pl.BlockSpec(a.shape, lambda i: (0,) * a.ndim)
    resident = lambda a: pl.BlockSpec(a.shape, lambda i: (0,) * a.ndim, pipeline_mode=pl.Buffered(1))
    in_specs = [pl.BlockSpec((tm, d), lambda i: (i, 0)),
                pl.BlockSpec((tm, d), lambda i: (i, 0)),
                pl.BlockSpec((None,) + modtab.shape[1:], lambda i: (layer, 0, 0, 0)),
                resident(w1), resident(w3), resident(w2)]
    args = [f, h, modtab, w1, w3, w2]
    if lat_only:
        in_specs.append(full(nf))
        args.append(nf)
    return pl.pallas_call(
        functools.partial(_ffn_kernel, lat_only, rows.sel(lat_only)),
        out_shape=jax.ShapeDtypeStruct((r, d), F32),
        grid=(r // tm,),
        in_specs=in_specs,
        out_specs=pl.BlockSpec((tm, d), lambda i: (i, 0)),
        scratch_shapes=[pltpu.VMEM((tm, dff), BF16)],
        compiler_params=_cparams(("arbitrary",)),
        name="ffn",
    )(*args)


MOE_TM = 512


def _moe_kernel(be_ref, live_ref, x_ref, w1_ref, w3_ref, w2_ref, o_ref, a_ref):
    i = pl.program_id(0)
    d = o_ref.shape[1]

    @pl.when(live_ref[i] > 0)
    def _():
        y = _swiglu(x_ref[:, 0:d].astype(BF16), w1_ref, w3_ref, w2_ref, a_ref, lead=(0,))
        tail = x_ref[:, d:]
        mine = tail[:, 0:1] == be_ref[i].astype(F32)
        o_ref[...] = y * jnp.where(mine, tail[:, 2:3], tail[:, 3:4])

    @pl.when(live_ref[i] == 0)
    def _():
        o_ref[...] = jnp.zeros(o_ref.shape, F32)


def _moe_experts(block_expert, block_live, xg, w1, w3, w2):
    n_rows, dx = xg.shape
    d = w1.shape[-2]
    dff = w1.shape[-1]
    tm = MOE_TM
    return pl.pallas_call(
        _moe_kernel,
        out_shape=jax.ShapeDtypeStruct((n_rows, d), F32),
        grid_spec=pltpu.PrefetchScalarGridSpec(
            num_scalar_prefetch=2,
            grid=(n_rows // tm,),
            in_specs=[pl.BlockSpec((tm, dx), lambda i, be, lv: (i, 0)),
                      pl.BlockSpec((1, d, dff), lambda i, be, lv: (be[i], 0, 0)),
                      pl.BlockSpec((1, d, dff), lambda i, be, lv: (be[i], 0, 0)),
                      pl.BlockSpec((1, dff, d), lambda i, be, lv: (be[i], 0, 0))],
            out_specs=pl.BlockSpec((tm, d), lambda i, be, lv: (i, 0)),
            scratch_shapes=[pltpu.VMEM((tm, dff), BF16)]),
        compiler_params=_cparams(("arbitrary",)),
        name="moe_experts",
    )(block_expert, block_live, xg, w1, w3, w2)


SC_GATHER_ROWS = 64


def _gather_rows(table, idx):
    n_idx = idx.shape[0]
    _, d = table.shape
    info = plsc.get_sparse_core_info()
    n_cores, n_workers = info.num_cores, info.num_cores * info.num_subcores
    assert n_idx % (n_workers * SC_GATHER_ROWS) == 0
    per_worker = n_idx // n_workers
    mesh = plsc.VectorSubcoreMesh(core_axis_name="c", subcore_axis_name="s")

    @functools.partial(
        pl.kernel, mesh=mesh,
        out_type=jax.ShapeDtypeStruct((n_idx, d), table.dtype),
        scratch_types=[pltpu.VMEM((SC_GATHER_ROWS,), jnp.int32),
                       pltpu.VMEM((SC_GATHER_ROWS, d), table.dtype),
                       pltpu.SemaphoreType.DMA])
    def gather(table_hbm, idx_hbm, out_hbm, idx_v, rows_v, sem):
        base = (lax.axis_index("s") * n_cores + lax.axis_index("c")) * per_worker

        @pl.loop(0, per_worker // SC_GATHER_ROWS)
        def _(it):
            off = pl.multiple_of(base + it * SC_GATHER_ROWS, SC_GATHER_ROWS)
            pltpu.sync_copy(idx_hbm.at[pl.ds(off, SC_GATHER_ROWS)], idx_v)
            pltpu.async_copy(table_hbm.at[idx_v], rows_v, sem).wait()
            pltpu.sync_copy(rows_v, out_hbm.at[pl.ds(off, SC_GATHER_ROWS)])

    return gather(table, idx)


def _scatter_rows(src, dest, n_rows):
    n_tok, d = src.shape
    info = plsc.get_sparse_core_info()
    n_cores, n_workers = info.num_cores, info.num_cores * info.num_subcores
    assert n_tok % (n_workers * SC_GATHER_ROWS) == 0
    per_worker = n_tok // n_workers
    mesh = plsc.VectorSubcoreMesh(core_axis_name="c", subcore_axis_name="s")

    @functools.partial(
        pl.kernel, mesh=mesh,
        out_type=jax.ShapeDtypeStruct((n_rows, d), src.dtype),
        scratch_types=[pltpu.VMEM((SC_GATHER_ROWS,), jnp.int32),
                       pltpu.VMEM((SC_GATHER_ROWS, d), src.dtype),
                       pltpu.SemaphoreType.DMA])
    def scatter(src_hbm, dest_hbm, out_hbm, idx_v, rows_v, sem):
        base = (lax.axis_index("s") * n_cores + lax.axis_index("c")) * per_worker

        @pl.loop(0, per_worker // SC_GATHER_ROWS)
        def _(it):
            off = pl.multiple_of(base + it * SC_GATHER_ROWS, SC_GATHER_ROWS)
            pltpu.sync_copy(src_hbm.at[pl.ds(off, SC_GATHER_ROWS)], rows_v)
            for k in range(TOP_K):
                pltpu.sync_copy(dest_hbm.at[pl.ds(k * n_tok + off, SC_GATHER_ROWS)], idx_v)
                pltpu.async_copy(rows_v, out_hbm.at[idx_v], sem).wait()

    return scatter(src, dest.reshape(-1))


def _resid_kernel(final, h_ref, y0_ref, y1_ref, mod_ref, *rest):
    hn = h_ref[...] + mod_ref[0][5:6] * (y0_ref[...] + y1_ref[...])
    if final:
        nf_ref, o_ref = rest
        o_ref[...] = _rmsnorm(hn, nf_ref[...])
    else:
        (o_ref,) = rest
        o_ref[...] = hn


def _moe_resid(h, y0, y1, modtab, nf, *, layer, rows, lat_only):
    r, d = h.shape
    tm = SEG
    sel = rows.sel(lat_only)
    row = lambda i: (i, 0)
    in_specs = [pl.BlockSpec((tm, d), row), pl.BlockSpec((tm, d), row), pl.BlockSpec((tm, d), row),
                pl.BlockSpec((None, 1, 8, d), lambda i: (layer, sel(i), 0, 0))]
    args = [h, y0, y1, modtab]
    if lat_only:
        in_specs.append(pl.BlockSpec((1, d), lambda i: (0, 0)))
        args.append(nf)
    return pl.pallas_call(
        functools.partial(_resid_kernel, lat_only),
        out_shape=jax.ShapeDtypeStruct((r, d), F32),
        grid=(r // tm,),
        in_specs=in_specs,
        out_specs=pl.BlockSpec((tm, d), row),
        compiler_params=_cparams(("arbitrary",)),
        name="moe_resid",
    )(*args)


CAST_PARTS = 8


def _cast_kernel(*refs):
    o_ref = refs[-1]
    tr = refs[0].shape[1]
    for k, x_ref in enumerate(refs[:-1]):
        o_ref[0, k * tr:(k + 1) * tr, :] = x_ref[0].astype(o_ref.dtype)


def _to_bf16(w, j):
    lead, (r, c) = w.shape[1:-2], w.shape[-2:]
    n = int(np.prod(lead, dtype=np.int64))
    w3 = w.reshape((-1, r, c))
    tr = r // CAST_PARTS
    assert tr * CAST_PARTS == r and tr % 16 == 0
    band = lambda k: pl.BlockSpec((1, tr, c), lambda e: (j * n + e, k, 0))
    out = pl.pallas_call(
        _cast_kernel,
        out_shape=jax.ShapeDtypeStruct((n, r, c), BF16),
        grid=(n,),
        in_specs=[band(k) for k in range(CAST_PARTS)],
        out_specs=pl.BlockSpec((1, r, c), lambda e: (e, 0, 0)),
        compiler_params=_cparams(("arbitrary",)),
        name="to_bf16",
    )(*([w3] * CAST_PARTS))
    return out.reshape(lead + (r, c))


def _pad_heads(w, heads, dim, to=LANE):
    lead = w.shape[:-1]
    w = w.reshape(lead + (heads, dim))
    w = jnp.pad(w, [(0, 0)] * len(lead) + [(0, 0), (0, to - dim)])
    return w.reshape(lead + (heads * to,))


def _pad_last(w, to):
    return jnp.pad(w, [(0, 0)] * (w.ndim - 1) + [(0, to - w.shape[-1])])


def _pos_embed(n_tokens, d):
    n_grid_rows = n_tokens // GRID_W
    row, col = jnp.meshgrid(jnp.arange(n_grid_rows, dtype=F32), jnp.arange(GRID_W, dtype=F32), indexing='ij')
    n_freq = d // 4
    omega = jnp.exp(-math.log(POS_BASE) * jnp.arange(n_freq, dtype=F32) / n_freq)

    def axis_embed(p):
        ang = p.reshape(-1, 1) * omega
        return jnp.concatenate([jnp.sin(ang), jnp.cos(ang)], axis=-1)

    return jnp.concatenate([axis_embed(row), axis_embed(col)], axis=-1)


def _s5_discretise(lam_re, lam_im, log_dt, b_re, b_im):
    dt = jnp.exp(log_dt)[:, None]
    mag = jnp.exp(lam_re * dt)
    abar_re, abar_im = mag * jnp.cos(lam_im * dt), mag * jnp.sin(lam_im * dt)
    den = lam_re * lam_re + lam_im * lam_im
    pr, pi = abar_re - 1.0, abar_im
    coef_re = (pr * lam_re + pi * lam_im) / den
    coef_im = (pi * lam_re - pr * lam_im) / den
    bbar_re = coef_re[..., None] * b_re - coef_im[..., None] * b_im
    bbar_im = coef_re[..., None] * b_im + coef_im[..., None] * b_re
    return abar_re, abar_im, bbar_re, bbar_im


def _block_diag(m):
    g, a, b = m.shape
    eye = jnp.eye(g, dtype=m.dtype)
    return (eye[:, None, :, None] * m[:, :, None, :]).reshape(g * a, g * b)


def _route_plan(route, tm):
    n_tok = route.shape[1]
    n_assign = n_tok * TOP_K
    flat_e = route[0:TOP_K].astype(jnp.int32).reshape(-1)
    onehot = (jnp.arange(N_EXPERTS, dtype=jnp.int32)[:, None] == flat_e[None, :]).astype(jnp.int32)
    csum = jnp.cumsum(onehot, axis=1)
    counts = csum[:, -1]
    padded = (counts + tm - 1) // tm * tm
    pend = jnp.cumsum(padded)
    pstart = pend - padded
    dest = jnp.sum(onehot * (csum - 1 + pstart[:, None]), axis=0)
    n_blocks = -(-n_assign // tm) + N_EXPERTS
    block_start = jnp.arange(n_blocks, dtype=jnp.int32) * tm
    block_expert = jnp.minimum(jnp.searchsorted(pend, block_start, side='right'), N_EXPERTS - 1).astype(jnp.int32)
    block_live = (block_start < (pstart + counts)[block_expert]).astype(jnp.int32)
    return n_blocks * tm, block_expert, block_live, dest.reshape(TOP_K, n_tok)


def kernel(x, c, ctx, c_ctx, w_ada, b_ada, norm1, norm2, w_in, w_out, gla_wa2, gla_ba, gla_norm, s5_lam_re, s5_lam_im, s5_log_dt, s5_b_re, s5_b_im, s5_c_re, s5_c_im, s5_d, s5_glu_w, s5_glu_b, ml_conv_w, ml_conv_b, ml_gate_b, ml_norm, ffn_w1, ffn_w3, ffn_w2, moe_router, moe_w1, moe_w3, moe_w2, norm_f):
    nb, n_lat, d = x.shape
    lc = ctx.shape[1]
    depth = w_ada.shape[0]
    l = lc + n_lat
    assert lc % SEG == 0 and n_lat % SEG == 0 and nb == 8 and nb % SCAN_BATCHES == 0
    rows = _Rows(nb, lc // SEG, l // SEG)

    h = _embed(ctx.reshape(nb * lc, d), x.reshape(nb * n_lat, d), _pos_embed(n_lat, d), rows)

    cond = jnp.zeros((16, d), F32).at[:nb].set(c).at[nb].set(c_ctx)
    mod = _modulation(cond, w_ada, b_ada)
    modtab = jnp.pad(mod.reshape(depth, 16, 6, d), ((0, 0), (0, 0), (0, 2), (0, 0)))

    dk, dv, dh = GLA_HEADS * GLA_DK, GLA_HEADS * GLA_DV, ML_HEADS * ML_DH
    s5c = s5_d.shape[-1]
    cuts = np.cumsum([dk, dk, dv, GLA_RANK, dv, s5c, dh, dh, dh, dh, 4 * ML_HEADS])

    for i in range(depth):
        last = i == depth - 1
        gq, gk, gv, glr, gg, su, mq, mk, mv, mo, mg = jnp.split(w_in[i], cuts[:-1], axis=-1)
        w_all = jnp.concatenate([
            _pad_heads(gq, GLA_HEADS, GLA_DK, GLA_KP), _pad_heads(gk, GLA_HEADS, GLA_DK, GLA_KP),
            _pad_heads(gv, GLA_HEADS, GLA_DV), _pad_heads(gg, GLA_HEADS, GLA_DV), _pad_last(glr, LANE),
            su,
            _pad_heads(mv, ML_HEADS, ML_DH), _pad_heads(mo, ML_HEADS, ML_DH),
            _pad_last(mg[:, :2 * ML_HEADS], LANE), _pad_last(mg[:, 2 * ML_HEADS:], LANE),
            _pad_heads(mq, ML_HEADS, ML_DH), _pad_heads(mk, ML_HEADS, ML_DH)],
            axis=-1).astype(BF16)
        wgt = mg.T.astype(BF16)
        cw = jnp.concatenate([_pad_heads(ml_conv_w[i][:, :dh], ML_HEADS, ML_DH),
                              _pad_heads(ml_conv_w[i][:, dh:], ML_HEADS, ML_DH)], axis=-1)
        cw = jnp.pad(cw, ((0, 8 - ML_CONV), (0, 0)))
        cb = jnp.concatenate([_pad_heads(ml_conv_b[i][:dh], ML_HEADS, ML_DH),
                              _pad_heads(ml_conv_b[i][dh:], ML_HEADS, ML_DH)])[None]
        post = jnp.concatenate([jnp.ones((HP,), F32), jnp.full((HP,), ML_DH ** -0.5, F32)])[None]
        pg, pu, pm, qk, gt = _proj(h, modtab, norm1[i][None], w_all, wgt, cw, cb, post, layer=i, rows=rows)

        wa = jnp.pad(_pad_heads(gla_wa2[i], GLA_HEADS, GLA_DK, GLA_KP), ((0, 0), (0, LANE - GLA_RANK), (0, 0)))
        ba = _pad_heads(gla_ba[i], GLA_HEADS, GLA_DK, GLA_KP)[:, None, :]
        pg3 = pg.reshape(nb, l, NG)
        ogf = _gla(pg3, wa[0], ba[0], rows=rows, rev=False).reshape(nb * l, HP)
        ogb = _gla(pg3, wa[1], ba[1], rows=rows, rev=True).reshape(nb * l, HP)

        bres, bims, ares, aims = [], [], [], []
        for dr in (0, 1):
            a_re, a_im, b_re, b_im = _s5_discretise(s5_lam_re[i, dr], s5_lam_im[i, dr], s5_log_dt[i, dr],
                                                    s5_b_re[i], s5_b_im[i])
            bres.append(_block_diag(jnp.swapaxes(b_re, 1, 2)))
            bims.append(_block_diag(jnp.swapaxes(b_im, 1, 2)))
            ares.append(jnp.broadcast_to(a_re.reshape(1, -1), (nb, a_re.size)))
            aims.append(jnp.broadcast_to(a_im.reshape(1, -1), (nb, a_im.size)))
        cre = _block_diag(jnp.swapaxes(s5_c_re[i], 1, 2)).astype(BF16)
        cim = _block_diag(jnp.swapaxes(s5_c_im[i], 1, 2)).astype(BF16)
        ut = pu.reshape(nb, l, s5c).swapaxes(0, 1).reshape(l * nb, s5c)
        yt = _s5(ut, jnp.stack(bres).astype(BF16), jnp.stack(bims).astype(BF16), jnp.stack(ares), jnp.stack(aims),
                 cre, cim, nb=nb, nc=lc // S5_STEPS, nt=l // S5_STEPS)
        ys = yt.reshape(2, l, nb, s5c).swapaxes(1, 2).reshape(2, nb * l, s5c)

        gb = ml_gate_b[i].reshape(2, 2, ML_HEADS)
        gbr = _pad_last(gb.reshape(2, 1, 2 * ML_HEADS), LANE)
        gbt = _pad_last(jnp.broadcast_to(gb[..., None], (2, 2, ML_HEADS, CHUNK)), LANE).reshape(2, 2, HP)
        gtl = _pad_last(gt.reshape(2, 2, ML_HEADS, nb, l // CHUNK, CHUNK).transpose(3, 4, 0, 1, 2, 5),
                        LANE).reshape(nb, l // CHUNK, 2, 2, HP)
        qk3, pm3 = qk.reshape(nb, l, 2 * HP), pm.reshape(nb, l, NM)
        omf = _mlstm(qk3, pm3, gtl, gbr[0], gbt[0], rows=rows, rev=False).reshape(nb * l, HP)
        omb = _mlstm(qk3, pm3, gtl, gbr[1], gbt[1], rows=rows, rev=True).reshape(nb * l, HP)

        wo = w_out[i]
        wo_p = jnp.concatenate([
            jnp.pad(wo[:dv].reshape(GLA_HEADS, GLA_DV, d), ((0, 0), (0, LANE - GLA_DV), (0, 0))).reshape(HP, d),
            wo[dv:dv + s5c],
            jnp.pad(wo[dv + s5c:].reshape(ML_HEADS, ML_DH, d), ((0, 0), (0, LANE - ML_DH), (0, 0))).reshape(HP, d)],
            axis=0).astype(BF16)
        gn = jnp.tile(_pad_last(gla_norm[i], LANE), GLA_HEADS)[None]
        mn = jnp.tile(_pad_last(ml_norm[i], LANE), ML_HEADS)[None]
        is_moe = i % 2 == 1
        j = i // 2
        wr = jnp.stack(_split_bf16(_pad_last(moe_router[j], LANE), 2)) if is_moe else None
        outs = _mix(ogf, ogb, pg, ys, pu, omf, omb, pm, h, modtab, gn, mn, s5_d[i][None], s5_glu_w[i].astype(BF16),
                    s5_glu_b[i][None], wo_p, norm2[i][None], wr, layer=i, rows=rows, lat_only=last)
        if not is_moe:
            h, f = outs
            h = _ffn(f, h, modtab, _to_bf16(ffn_w1, j), _to_bf16(ffn_w3, j), _to_bf16(ffn_w2, j),
                     norm_f[None], layer=i, rows=rows, lat_only=last)
        else:
            h, f, route = outs
            n_rows, block_expert, block_live, dest = _route_plan(route, MOE_TM)
            xg = _scatter_rows(f, dest, n_rows)
            yg = _moe_experts(block_expert, block_live, xg, _to_bf16(moe_w1, j), _to_bf16(moe_w3, j),
                              _to_bf16(moe_w2, j))
            y0 = _gather_rows(yg, dest[0])
            y1 = _gather_rows(yg, dest[1])
            h = _moe_resid(h, y0, y1, modtab, norm_f[None], layer=i, rows=rows, lat_only=last)
    return h.reshape(nb, n_lat, d)
```

```python
import functools
import math

import numpy as np
import jax
import jax.numpy as jnp
from jax import lax
from jax.experimental import pallas as pl
from jax.experimental.pallas import tpu as pltpu
from jax.experimental.pallas import tpu_sc as plsc

F32 = jnp.float32
BF16 = jnp.bfloat16
HIGHEST = lax.Precision.HIGHEST

GRID_W = 64
POS_BASE = 10000.0
EPS = 1e-6
GLA_HEADS, GLA_DK, GLA_DV, GLA_RANK, GLA_GATE_NORM = 4, 48, 96, 16, 16.0
S5_GROUP, S5_STATE = 16, 64
ML_HEADS, ML_DH, ML_CONV = 4, 96, 3
N_EXPERTS, TOP_K = 8, 2

LANE = 128
CHUNK = 64
SEG = 256
N_SUB = SEG // CHUNK
SCAN_BATCHES = 8
S5_STEPS = 128
S5_SUB = 32
NEG = -1e30
VMEM_LIMIT = 56 * 1024 * 1024

HP = LANE * GLA_HEADS
GLA_KP = 64
GQ = GLA_HEADS * GLA_KP
NG = 2 * GQ + 2 * HP + LANE
NM = 2 * HP + 2 * LANE


def _cparams(sem):
    return pltpu.CompilerParams(dimension_semantics=sem, vmem_limit_bytes=VMEM_LIMIT)


def _dot(a, b, **kw):
    return jnp.dot(a, b, preferred_element_type=F32, **kw)


def _dot_nt(a, b, **kw):
    return lax.dot_general(a, b, (((1,), (1,)), ((), ())), preferred_element_type=F32, **kw)


def _dot_tn(a, b, **kw):
    return lax.dot_general(a, b, (((0,), (0,)), ((), ())), preferred_element_type=F32, **kw)


def _log_sigmoid(x):
    return jnp.minimum(x, 0.0) - jnp.log1p(jnp.exp(-jnp.abs(x)))


def _silu(x):
    return x * jax.nn.sigmoid(x)


def _gelu_tanh(x):
    return 0.5 * x * (1.0 + jnp.tanh(math.sqrt(2.0 / math.pi) * (x + 0.044715 * (x * x * x))))


def _rmsnorm(x, g):
    return x * lax.rsqrt(jnp.mean(x * x, axis=-1, keepdims=True) + EPS) * g


class _Rows:
    def __init__(self, nb, ncb, ntb):
        self.nb, self.ncb, self.ntb, self.nlb = nb, ncb, ntb, ntb - ncb

    def n_blocks(self, lat_only):
        return self.nb * (self.nlb if lat_only else self.ntb)

    def src(self, lat_only):
        if lat_only:
            return lambda i: (i // self.nlb) * self.ntb + self.ncb + i % self.nlb
        return lambda i: i

    def sel(self, lat_only):
        if lat_only:
            return lambda i: i // self.nlb
        return lambda i: jnp.where(i % self.ntb < self.ncb, self.nb, i // self.ntb)


def _embed_kernel(ncb, ntb, ctx_ref, x_ref, pos_ref, o_ref):
    j = pl.program_id(0) % ntb

    @pl.when(j < ncb)
    def _():
        o_ref[...] = ctx_ref[...]

    @pl.when(j >= ncb)
    def _():
        o_ref[...] = x_ref[...] + pos_ref[...]


def _embed(ctx2, x2, pos, rows):
    d = ctx2.shape[1]
    ncb, ntb, nlb = rows.ncb, rows.ntb, rows.nlb
    return pl.pallas_call(
        functools.partial(_embed_kernel, ncb, ntb),
        out_shape=jax.ShapeDtypeStruct((rows.nb * ntb * SEG, d), F32),
        grid=(rows.nb * ntb,),
        in_specs=[pl.BlockSpec((SEG, d), lambda i: ((i // ntb) * ncb + jnp.minimum(i % ntb, ncb - 1), 0)),
                  pl.BlockSpec((SEG, d), lambda i: ((i // ntb) * nlb + jnp.maximum(i % ntb - ncb, 0), 0)),
                  pl.BlockSpec((SEG, d), lambda i: (jnp.maximum(i % ntb - ncb, 0), 0))],
        out_specs=pl.BlockSpec((SEG, d), lambda i: (i, 0)),
        compiler_params=_cparams(("arbitrary",)),
        name="embed",
    )(ctx2, x2, pos)


def _mod_kernel(c_ref, w_ref, b_ref, o_ref):
    s = _silu(c_ref[...])
    o_ref[0] = _dot(s, w_ref[0], precision=HIGHEST) + b_ref[0]


def _modulation(cond, w_ada, b_ada):
    depth, d, n6 = w_ada.shape
    tn = n6 // 4
    n_rows = cond.shape[0]
    return pl.pallas_call(
        _mod_kernel,
        out_shape=jax.ShapeDtypeStruct((depth, n_rows, n6), F32),
        grid=(depth, n6 // tn),
        in_specs=[pl.BlockSpec((n_rows, d), lambda l, j: (0, 0)),
                  pl.BlockSpec((1, d, tn), lambda l, j: (l, 0, j)),
                  pl.BlockSpec((1, 1, tn), lambda l, j: (l, 0, j))],
        out_specs=pl.BlockSpec((1, n_rows, tn), lambda l, j: (l, 0, j)),
        compiler_params=_cparams(("arbitrary", "arbitrary")),
        name="modulation",
    )(cond, w_ada, b_ada.reshape(depth, 1, n6))


PROJ_TM = 2 * SEG


def _proj_kernel(ncb, ntb, sel, h_ref, hp_ref, hn_ref, mod_ref, g_ref, w_ref, wgt_ref, cw_ref, cb_ref, post_ref,
                 pg_ref, pu_ref, pm_ref, qk_ref, gt_ref):
    tm = h_ref.shape[0]
    n_seg = tm // SEG
    blk = [pl.program_id(0) * n_seg + q for q in range(n_seg)]
    mods = [mod_ref[sel(bq)] for bq in blk]
    act = lambda x, m: _rmsnorm(x, g_ref[...]) * (1.0 + m[1:2]) + m[0:1]
    a_seg = [act(h_ref[q * SEG:(q + 1) * SEG, :], mods[q]) for q in range(n_seg)]
    ab = jnp.concatenate(a_seg, axis=0).astype(BF16)
    c1, c2 = NG + 2 * LANE, NG + 2 * LANE + NM
    pg_ref[...] = _dot(ab, w_ref[:, 0:NG])
    pu_ref[...] = _dot(ab, w_ref[:, NG:c1])
    pm_ref[...] = _dot(ab, w_ref[:, c1:c2])
    gt_ref[...] = _dot_nt(wgt_ref[...], ab)
    ae = jnp.concatenate([act(hp_ref[...], mods[0])] + a_seg + [act(hn_ref[...], mods[-1])], axis=0)
    xe = _dot(ae.astype(BF16), w_ref[:, c2:])
    n_e = tm + 16
    row = lax.broadcasted_iota(jnp.int32, (tm, xe.shape[1]), 0)
    keep_prev = jnp.ones((tm, xe.shape[1]), F32)
    keep_next = keep_prev
    for q, bq in enumerate(blk):
        j = bq % ntb
        first = jnp.logical_or(j == 0, j == ncb).astype(F32)
        last = jnp.logical_or(j == ncb - 1, j == ntb - 1).astype(F32)
        keep_prev = jnp.where(row == q * SEG, 1.0 - first, keep_prev)
        keep_next = jnp.where(row == (q + 1) * SEG - 1, 1.0 - last, keep_next)
    xp = pltpu.roll(xe, 1, axis=0)[8:8 + tm] * keep_prev
    xn = pltpu.roll(xe, n_e - 1, axis=0)[8:8 + tm] * keep_next
    y = cw_ref[0:1] * xp + cw_ref[1:2] * xe[8:8 + tm] + cw_ref[2:3] * xn + cb_ref[...]
    qk_ref[...] = _silu(y) * post_ref[...]


def _proj(h, modtab, g, w, wgt, cw, cb, post, *, layer, rows):
    r, d = h.shape
    tm = PROJ_TM
    assert r % tm == 0
    t8 = tm // 8
    full = lambda a: pl.BlockSpec(a.shape, lambda i: (0,) * a.ndim)
    return pl.pallas_call(
        functools.partial(_proj_kernel, rows.ncb, rows.ntb, rows.sel(False)),
        out_shape=(jax.ShapeDtypeStruct((r, NG), F32), jax.ShapeDtypeStruct((r, 2 * LANE), F32),
                   jax.ShapeDtypeStruct((r, NM), F32), jax.ShapeDtypeStruct((r, 2 * HP), F32),
                   jax.ShapeDtypeStruct((16, r), F32)),
        grid=(r // tm,),
        in_specs=[pl.BlockSpec((tm, d), lambda i: (i, 0)),
                  pl.BlockSpec((8, d), lambda i: (jnp.maximum(i * t8 - 1, 0), 0)),
                  pl.BlockSpec((8, d), lambda i: (jnp.minimum((i + 1) * t8, r // 8 - 1), 0)),
                  pl.BlockSpec((None,) + modtab.shape[1:], lambda i: (layer, 0, 0, 0)),
                  full(g), full(w), full(wgt), full(cw), full(cb), full(post)],
        out_specs=(pl.BlockSpec((tm, NG), lambda i: (i, 0)),
                   pl.BlockSpec((tm, 2 * LANE), lambda i: (i, 0)),
                   pl.BlockSpec((tm, NM), lambda i: (i, 0)),
                   pl.BlockSpec((tm, 2 * HP), lambda i: (i, 0)),
                   pl.BlockSpec((16, tm), lambda i: (0, i))),
        compiler_params=_cparams(("arbitrary",)),
        name="proj",
    )(h, h, h, modtab, g, w, wgt, cw, cb, post)


def _conv_kernel(ncb, ntb, x_ref, prev_ref, next_ref, w_ref, b_ref, post_ref, o_ref):
    j = pl.program_id(0) % ntb
    x = x_ref[...]
    tm = x.shape[0]
    first = jnp.logical_or(j == 0, j == ncb).astype(F32)
    last = jnp.logical_or(j == ncb - 1, j == ntb - 1).astype(F32)
    row = lax.broadcasted_iota(jnp.int32, x.shape, 0)
    xp = jnp.where(row == 0, prev_ref[7:8, :] * (1.0 - first), pltpu.roll(x, 1, axis=0))
    xn = jnp.where(row == tm - 1, next_ref[0:1, :] * (1.0 - last), pltpu.roll(x, tm - 1, axis=0))
    y = w_ref[0:1] * xp + w_ref[1:2] * x + w_ref[2:3] * xn + b_ref[...]
    o_ref[...] = _silu(y) * post_ref[...]


def _conv(pm, w, b, post, *, rows):
    r = pm.shape[0]
    tm = SEG
    wq = 2 * HP
    t8 = tm // 8
    return pl.pallas_call(
        functools.partial(_conv_kernel, rows.ncb, rows.ntb),
        out_shape=jax.ShapeDtypeStruct((r, wq), F32),
        grid=(r // tm,),
        in_specs=[pl.BlockSpec((tm, wq), lambda i: (i, 0)),
                  pl.BlockSpec((8, wq), lambda i: (jnp.maximum(i * t8 - 1, 0), 0)),
                  pl.BlockSpec((8, wq), lambda i: (jnp.minimum((i + 1) * t8, r // 8 - 1), 0)),
                  pl.BlockSpec((8, wq), lambda i: (0, 0)),
                  pl.BlockSpec((1, wq), lambda i: (0, 0)),
                  pl.BlockSpec((1, wq), lambda i: (0, 0))],
        out_specs=pl.BlockSpec((tm, wq), lambda i: (i, 0)),
        compiler_params=_cparams(("arbitrary",)),
        name="conv",
    )(pm, pm, pm, w, b, post)


def _scan_pos(d, s, ncb, ntb):
    rev = jnp.where(s < ncb, ncb - 1 - s, ntb - 1 - (s - ncb))
    return jnp.where(d == 0, s, rev)


def _scan_pos_static(rev, s, ncb, ntb):
    if not rev:
        return s
    return jnp.where(s < ncb, ncb - 1 - s, ntb - 1 - (s - ncb))


def _tri(rev):
    r = lax.broadcasted_iota(jnp.int32, (CHUNK, CHUNK), 0)
    c = lax.broadcasted_iota(jnp.int32, (CHUNK, CHUNK), 1)
    return (r <= c) if rev else (r >= c)


def _chunk_rows(rev):
    return [(N_SUB - 1 - j if rev else j) * CHUNK for j in range(N_SUB)]


def _gla_kernel(rev, p_ref, wa_ref, ba_ref, o_ref, st_ref):
    s, g = pl.program_id(0), pl.program_id(1)
    nbb = p_ref.shape[0]
    b0 = g * nbb

    @pl.when(s == 0)
    def _():
        st_ref[pl.ds(b0, nbb)] = jnp.zeros((nbb,) + st_ref.shape[1:], F32)

    valid = _tri(rev)
    tri = valid.astype(F32)
    r0s = _chunk_rows(rev)
    wa, ba = wa_ref[...], ba_ref[...]
    inst = [(bb, j) for bb in range(nbb) for j in range(N_SUB)]
    heads = [slice(h * LANE, (h + 1) * LANE) for h in range(GLA_HEADS)]
    pairs = [slice((h // 2) * LANE, (h // 2 + 1) * LANE) for h in range(GLA_HEADS)]
    lane = lax.broadcasted_iota(jnp.int32, (CHUNK, LANE), 1)
    own = [(lane // GLA_KP) == (h % 2) for h in range(GLA_HEADS)]

    la = {}
    for bb, j in inst:
        lr = p_ref[bb, pl.ds(r0s[j], CHUNK), 2 * GQ + 2 * HP:NG]
        la[bb, j] = _log_sigmoid(_dot(lr, wa) + ba) * (1.0 / GLA_GATE_NORM)
    bc, e_last = {}, {}
    for i in inst:
        bc[i] = _dot_exact01(tri, la[i], lhs_is_01=True, pieces=2)
        e_last[i] = jnp.exp(jnp.sum(la[i], axis=0, keepdims=True))
    q_in, k_in, k_out, v = {}, {}, {}, {}
    for bb, j in inst:
        i = (bb, j)
        rs = pl.ds(r0s[j], CHUNK)
        qs = (p_ref[bb, rs, 0:GQ] * (GLA_DK ** -0.5) * jnp.exp(bc[i])).astype(BF16)
        for h in range(GLA_HEADS):
            q_in[i, h] = jnp.where(own[h], qs[:, pairs[h]], jnp.zeros_like(qs[:, pairs[h]]))
        kd = p_ref[bb, rs, GQ:2 * GQ] * jnp.exp(-bc[i])
        k_out[i] = (kd * e_last[i]).astype(BF16)
        k_in[i] = kd.astype(BF16)
        v[i] = p_ref[bb, rs, 2 * GQ:2 * GQ + HP].astype(BF16)
    att = {}
    for i in inst:
        for h in range(GLA_HEADS):
            att[i, h] = jnp.where(valid, _dot_nt(q_in[i, h], k_in[i][:, pairs[h]]), 0.0).astype(BF16)
    o_intra, ds = {}, {}
    for i in inst:
        for h, sl in enumerate(heads):
            o_intra[i, h] = _dot(att[i, h], v[i][:, sl])
            ds[i, h] = _dot_tn(v[i][:, sl], k_out[i][:, pairs[h]])
    s_in = {}
    for bb in range(nbb):
        for h in range(GLA_HEADS):
            st = st_ref[b0 + bb, h]
            for j in range(N_SUB):
                s_in[(bb, j), h] = st.astype(BF16)
                st = st * e_last[bb, j][:, pairs[h]] + ds[(bb, j), h]
            st_ref[b0 + bb, h] = st
    for bb, j in inst:
        for h, sl in enumerate(heads):
            o = o_intra[(bb, j), h] + _dot_nt(q_in[(bb, j), h], s_in[(bb, j), h])
            o_ref[bb, pl.ds(r0s[j], CHUNK), sl] = o.astype(o_ref.dtype)


def _gla(pg3, wa, ba, *, rows, rev):
    nb, l, _ = pg3.shape
    nbb = SCAN_BATCHES
    pos = functools.partial(_scan_pos_static, rev, ncb=rows.ncb, ntb=rows.ntb)
    return pl.pallas_call(
        functools.partial(_gla_kernel, rev),
        out_shape=jax.ShapeDtypeStruct((nb, l, HP), BF16),
        grid=(rows.ntb, nb // nbb),
        in_specs=[pl.BlockSpec((nbb, SEG, NG), lambda s, g: (g, pos(s), 0)),
                  pl.BlockSpec((LANE, GQ), lambda s, g: (0, 0)),
                  pl.BlockSpec((1, GQ), lambda s, g: (0, 0))],
        out_specs=pl.BlockSpec((nbb, SEG, HP), lambda s, g: (g, pos(s), 0)),
        scratch_shapes=[pltpu.VMEM((nb, GLA_HEADS, LANE, LANE), F32)],
        compiler_params=_cparams(("arbitrary", "arbitrary")),
        name="gla_scan_bwd" if rev else "gla_scan_fwd",
    )(pg3, wa, ba)


def _mlstm_kernel_old(qk_ref, v_ref, g_ref, gt_ref, gbr_ref, gbc_ref, o_ref, st_ref, m_ref):
    d, s, g = pl.program_id(0), pl.program_id(1), pl.program_id(2)
    nbb = qk_ref.shape[0]
    b0 = g * nbb

    @pl.when(s == 0)
    def _():
        st_ref[pl.ds(b0, nbb)] = jnp.zeros((nbb,) + st_ref.shape[1:], F32)
        m_ref[pl.ds(b0, nbb)] = jnp.zeros((nbb,) + m_ref.shape[1:], F32)

    fwd = d == 0
    valid = _tri(d)
    tri = valid.astype(F32)
    r0s = _chunk_rows(d)
    cidx = [jnp.where(fwd, j, N_SUB - 1 - j) for j in range(N_SUB)]
    lane = lax.broadcasted_iota(jnp.int32, (CHUNK, LANE), 1)
    inst = [(bb, j) for bb in range(nbb) for j in range(N_SUB)]
    hinst = [(bb, j, h) for bb, j in inst for h in range(ML_HEADS)]
    gbr, gbc = gbr_ref[...], gbc_ref[...]

    def pick(a, h, kind, axis):
        i0, i1 = kind * ML_HEADS + h, (2 + kind) * ML_HEADS + h
        if axis == 1:
            return jnp.where(fwd, a[:, i0:i0 + 1], a[:, i1:i1 + 1])
        return jnp.where(fwd, a[i0:i0 + 1, :], a[i1:i1 + 1, :])

    gc, gr, fcum_c, fcum_r = {}, {}, {}, {}
    for bb, j in inst:
        i = (bb, j)
        gc[i] = g_ref[bb, pl.ds(r0s[j], CHUNK), :] + gbr
        gr[i] = gt_ref[bb, cidx[j]] + gbc
    for i in inst:
        fcum_c[i] = _dot(tri, _log_sigmoid(gc[i]), precision=HIGHEST)
        fcum_r[i] = _dot_nt(_log_sigmoid(gr[i]), tri, precision=HIGHEST)
    fc, lic, d_log, rmax = {}, {}, {}, {}
    for bb, j, h in hinst:
        i = (bb, j)
        fc[bb, j, h] = pick(fcum_c[i], h, 1, 1)
        lic[bb, j, h] = pick(gc[i], h, 0, 1)
        dl = jnp.where(valid, fc[bb, j, h] - pick(fcum_r[i], h, 1, 0) + pick(gr[i], h, 0, 0), NEG)
        d_log[bb, j, h] = dl
        rmax[bb, j, h] = jnp.max(dl, axis=-1, keepdims=True)
    m_prev, m_t, m_new, f_tot = {}, {}, {}, {}
    for bb in range(nbb):
        for h in range(ML_HEADS):
            m = m_ref[b0 + bb, h:h + 1, 0:1]
            for j in range(N_SUB):
                i = (bb, j, h)
                m_prev[i] = m
                m_t[i] = jnp.maximum(fc[i] + m, rmax[i])
                m = jnp.where(fwd, m_t[i][CHUNK - 1:CHUNK], m_t[i][0:1])
                m_new[i] = m
                f_tot[i] = jnp.where(fwd, fc[i][CHUNK - 1:CHUNK], fc[i][0:1])
            m_ref[b0 + bb, h:h + 1, :] = jnp.broadcast_to(m, (1, LANE))
    q, v, qk, w_prev, gdec, ds = {}, {}, {}, {}, {}, {}
    for bb, j, h in hinst:
        i = (bb, j, h)
        rs = pl.ds(r0s[j], CHUNK)
        sl = slice(h * LANE, (h + 1) * LANE)
        q[i] = qk_ref[bb, rs, sl].astype(BF16)
        k = qk_ref[bb, rs, HP + h * LANE:HP + (h + 1) * LANE]
        v[i] = jnp.where(lane == ML_DH, 1.0, v_ref[bb, rs, sl]).astype(BF16)
        w = jnp.exp(d_log[i] - m_t[i])
        w_prev[i] = jnp.exp(fc[i] + m_prev[i] - m_t[i])
        qk[i] = (_dot_nt(q[i], k.astype(BF16)) * w).astype(BF16)
        w_s = jnp.exp(f_tot[i] - fc[i] + lic[i] - m_new[i])
        gdec[i] = jnp.exp(f_tot[i] + m_prev[i] - m_new[i])
        ds[i] = _dot_tn(v[i], (k * w_s).astype(BF16))
    s_in = {}
    for bb in range(nbb):
        for h in range(ML_HEADS):
            st = st_ref[b0 + bb, h]
            for j in range(N_SUB):
                i = (bb, j, h)
                s_in[i] = st.astype(BF16)
                st = gdec[i] * st + ds[i]
            st_ref[b0 + bb, h] = st
    for bb, j, h in hinst:
        i = (bb, j, h)
        num = w_prev[i] * _dot_nt(q[i], s_in[i]) + _dot(qk[i], v[i])
        den = num[:, ML_DH:ML_DH + 1]
        hh = num / jnp.maximum(jnp.abs(den), jnp.exp(-m_t[i]))
        o_ref[0, bb, pl.ds(r0s[j], CHUNK), h * LANE:(h + 1) * LANE] = jnp.where(lane < ML_DH, hh, 0.0)


def _mlstm_old(qk3, pm3, gt4, gbr, gbc, *, rows):
    nb, l, _ = qk3.shape
    nbb = SCAN_BATCHES
    pos = functools.partial(_scan_pos, ncb=rows.ncb, ntb=rows.ntb)
    return pl.pallas_call(
        _mlstm_kernel,
        out_shape=jax.ShapeDtypeStruct((2, nb, l, HP), F32),
        grid=(2, rows.ntb, nb // nbb),
        in_specs=[pl.BlockSpec((nbb, SEG, 2 * HP), lambda d, s, g: (g, pos(d, s), 0)),
                  pl.BlockSpec((nbb, SEG, HP), lambda d, s, g: (g, pos(d, s), 2)),
                  pl.BlockSpec((nbb, SEG, LANE), lambda d, s, g: (g, pos(d, s), 4 * GLA_HEADS)),
                  pl.BlockSpec((nbb, N_SUB, 16, CHUNK), lambda d, s, g: (g, pos(d, s), 0, 0)),
                  pl.BlockSpec((1, LANE), lambda d, s, g: (0, 0)),
                  pl.BlockSpec((16, CHUNK), lambda d, s, g: (0, 0))],
        out_specs=pl.BlockSpec((1, nbb, SEG, HP), lambda d, s, g: (d, g, pos(d, s), 0)),
        scratch_shapes=[pltpu.VMEM((nb, ML_HEADS, LANE, LANE), F32), pltpu.VMEM((nb, 8, LANE), F32)],
        compiler_params=_cparams(("arbitrary", "arbitrary", "arbitrary")),
        name="mlstm_scan",
    )(qk3, pm3, pm3, gt4, gbr, gbc)


def _split_bf16(x, n):
    parts, r = [], x
    for _ in range(n):
        p = r.astype(BF16)
        parts.append(p)
        r = r - p.astype(F32)
    return parts


def _dot_exact01(a, b, lhs_is_01, pieces=3):
    if lhs_is_01:
        a = a.astype(BF16)
        terms = [_dot(a, p) for p in _split_bf16(b, pieces)]
    else:
        b = b.astype(BF16)
        terms = [_dot(p, b) for p in _split_bf16(a, pieces)]
    return functools.reduce(lambda x, y: x + y, terms)


def _cummax_rows(a, rev):
    n = a.shape[0]
    row = lax.broadcasted_iota(jnp.int32, a.shape, 0)
    k = 1
    while k < n:
        if rev:
            sh = jnp.where(row < n - k, pltpu.roll(a, n - k, axis=0), NEG)
        else:
            sh = jnp.where(row >= k, pltpu.roll(a, k, axis=0), NEG)
        a = jnp.maximum(a, sh)
        k *= 2
    return a


ML_GL = ML_HEADS


def _mlstm_kernel(rev, qk_ref, v_ref, g_ref, gt_ref, gbr_ref, gbt_ref, o_ref, st_ref, m_ref):
    s, g = pl.program_id(0), pl.program_id(1)
    nbb = qk_ref.shape[0]
    b0 = g * nbb

    @pl.when(s == 0)
    def _():
        st_ref[pl.ds(b0, nbb)] = jnp.zeros((nbb,) + st_ref.shape[1:], F32)
        m_ref[pl.ds(b0, nbb)] = jnp.zeros((nbb,) + m_ref.shape[1:], F32)

    valid = _tri(rev)
    tri = valid.astype(F32)
    r0s = _chunk_rows(rev)
    cs = [r // CHUNK for r in r0s]
    last = 0 if rev else CHUNK - 1
    inst = [(bb, j) for bb in range(nbb) for j in range(N_SUB)]
    heads = [slice(h * LANE, (h + 1) * LANE) for h in range(ML_HEADS)]

    r_sel = lax.broadcasted_iota(jnp.int32, (LANE, HP), 0)
    c_sel = lax.broadcasted_iota(jnp.int32, (LANE, HP), 1)
    sel_h = (r_sel == ML_GL + c_sel // LANE).astype(BF16)
    r_t = lax.broadcasted_iota(jnp.int32, (HP, HP), 0)
    c_t = lax.broadcasted_iota(jnp.int32, (HP, HP), 1)
    same = jnp.logical_and(r_t // LANE == c_t // LANE, jnp.logical_and(r_t % LANE < CHUNK, c_t % LANE < CHUNK))
    before = (r_t % LANE >= c_t % LANE) if rev else (r_t % LANE <= c_t % LANE)
    tri_b = jnp.logical_and(same, before).astype(BF16)
    r_v = lax.broadcasted_iota(jnp.int32, (CHUNK, HP), 0)
    c_v = lax.broadcasted_iota(jnp.int32, (CHUNK, HP), 1) % LANE
    valid4 = jnp.logical_and(c_v < CHUNK, (r_v <= c_v) if rev else (r_v >= c_v))
    lane4 = lax.broadcasted_iota(jnp.int32, (CHUNK, HP), 1) % LANE
    lane_c = lax.broadcasted_iota(jnp.int32, (CHUNK, LANE), 1)
    lane1 = lax.broadcasted_iota(jnp.int32, (1, LANE), 1)
    head_lane = jnp.logical_and(lane1 >= ML_GL, lane1 < ML_GL + ML_HEADS)
    gbr, gbt = gbr_ref[...], gbt_ref[...]

    gcs, fcm, cmx, a_row, grt = {}, {}, {}, {}, {}
    for bb, j in inst:
        gc = g_ref[bb, pl.ds(r0s[j], CHUNK), :] + gbr
        gcs[bb, j] = pltpu.roll(gc, ML_GL, axis=1)
        fcm[bb, j] = _dot_exact01(tri, _log_sigmoid(gc), lhs_is_01=True)
        grt[bb, j] = gt_ref[bb, cs[j]] + gbt
    row_id = lax.broadcasted_iota(jnp.int32, (len(inst), HP), 0)
    lfr = jnp.zeros((len(inst), HP), F32)
    for n, i in enumerate(inst):
        lfr = jnp.where(row_id == n, _log_sigmoid(grt[i][1:2]), lfr)
    fcr = _dot_exact01(lfr, tri_b, lhs_is_01=False)
    for n, i in enumerate(inst):
        a_row[i] = grt[i][0:1] - fcr[n:n + 1]
        cmx[i] = _cummax_rows(gcs[i] - fcm[i], rev)
    bx = {}
    e_neg, gd = {}, {}
    for bb in range(nbb):
        m_prev = m_ref[b0 + bb, 0:1, :]
        for j in range(N_SUB):
            i = (bb, j)
            m_t = fcm[i] + jnp.maximum(m_prev, cmx[i])
            m_new = m_t[last:last + 1]
            f_tot = fcm[i][last:last + 1]
            u = fcm[i] - m_t
            w_prev = jnp.exp(u + m_prev)
            w_s = jnp.exp(f_tot - fcm[i] + gcs[i] - m_new)
            gdec = jnp.broadcast_to(jnp.exp(f_tot + m_prev - m_new), (16, LANE))
            e_neg[i] = jnp.exp(-m_t)
            keep = lambda a: jnp.where(head_lane, a, 0.0)
            bx[i] = jnp.concatenate(_split_bf16(keep(u), 2) + _split_bf16(keep(w_prev), 1)
                                    + _split_bf16(keep(w_s), 1) + _split_bf16(keep(gdec), 2), axis=0)
            m_prev = m_new
        m_ref[b0 + bb] = jnp.broadcast_to(m_prev, (8, LANE))
    ub, wpb, wsb, gdb = {}, {}, {}, {}
    for i in inst:
        y = _dot(bx[i], sel_h)
        c = CHUNK
        ub[i] = y[0:c] + y[c:2 * c]
        wpb[i] = y[2 * c:3 * c]
        wsb[i] = y[3 * c:4 * c]
        gdb[i] = y[4 * c:4 * c + 1] + y[4 * c + 16:4 * c + 17]
    q, v, qkw, ds = {}, {}, {}, {}
    for bb, j in inst:
        i = (bb, j)
        rs = pl.ds(r0s[j], CHUNK)
        w = jnp.where(valid4, jnp.exp(ub[i] + a_row[i]), 0.0)
        q[i] = qk_ref[bb, rs, 0:HP].astype(BF16)
        k = qk_ref[bb, rs, HP:2 * HP]
        kb = k.astype(BF16)
        kw = (k * wsb[i]).astype(BF16)
        v[i] = jnp.where(lane4 == ML_DH, 1.0, v_ref[bb, rs, :]).astype(BF16)
        for h, sl in enumerate(heads):
            sc = _dot_nt(q[i][:, sl], kb[:, sl])
            qkw[i, h] = (sc * w[:, h * LANE:h * LANE + CHUNK]).astype(BF16)
            ds[i, h] = _dot_tn(v[i][:, sl], kw[:, sl])
    s_in = {}
    for bb in range(nbb):
        for h, sl in enumerate(heads):
            st = st_ref[b0 + bb, h]
            for j in range(N_SUB):
                i = (bb, j)
                s_in[i, h] = st.astype(BF16)
                st = gdb[i][:, sl] * st + ds[i, h]
            st_ref[b0 + bb, h] = st
    num = {}
    for i in inst:
        parts = [_dot_nt(q[i][:, sl], s_in[i, h]) for h, sl in enumerate(heads)]
        intra = [_dot(qkw[i, h], v[i][:, sl]) for h, sl in enumerate(heads)]
        num[i] = wpb[i] * jnp.concatenate(parts, axis=1) + jnp.concatenate(intra, axis=1)
    for bb, j in inst:
        i = (bb, j)
        den = jnp.zeros((CHUNK, LANE), F32)
        for h, sl in enumerate(heads):
            dh = jnp.sum(jnp.where(lane_c == ML_DH, num[i][:, sl], 0.0), axis=-1, keepdims=True)
            den = jnp.where(lane_c == ML_GL + h, jnp.broadcast_to(dh, (CHUNK, LANE)), den)
        r1, r2 = _split_bf16(jnp.where(head_lane, 1.0 / jnp.maximum(jnp.abs(den), e_neg[i]), 0.0), 2)
        rb = _dot(jnp.concatenate([r1, r2], axis=0), sel_h)
        rb = rb[0:CHUNK] + rb[CHUNK:2 * CHUNK]
        o_ref[bb, pl.ds(r0s[j], CHUNK), :] = jnp.where(lane4 < ML_DH, num[i] * rb, 0.0).astype(o_ref.dtype)


def _mlstm(qk3, pm3, gtl, gbr, gbt, *, rows, rev):
    nb, l, _ = qk3.shape
    nbb = SCAN_BATCHES
    dr = 1 if rev else 0
    pos = functools.partial(_scan_pos_static, rev, ncb=rows.ncb, ntb=rows.ntb)
    return pl.pallas_call(
        functools.partial(_mlstm_kernel, rev),
        out_shape=jax.ShapeDtypeStruct((nb, l, HP), BF16),
        grid=(rows.ntb, nb // nbb),
        in_specs=[pl.BlockSpec((nbb, SEG, 2 * HP), lambda s, g: (g, pos(s), 0)),
                  pl.BlockSpec((nbb, SEG, HP), lambda s, g: (g, pos(s), 0)),
                  pl.BlockSpec((nbb, SEG, LANE), lambda s, g: (g, pos(s), 2 * ML_HEADS + dr)),
                  pl.BlockSpec((nbb, N_SUB, None, 2, HP), lambda s, g: (g, pos(s), dr, 0, 0)),
                  pl.BlockSpec((1, LANE), lambda s, g: (0, 0)),
                  pl.BlockSpec((2, HP), lambda s, g: (0, 0))],
        out_specs=pl.BlockSpec((nbb, SEG, HP), lambda s, g: (g, pos(s), 0)),
        scratch_shapes=[pltpu.VMEM((nb, ML_HEADS, LANE, LANE), F32), pltpu.VMEM((nb, 8, LANE), F32)],
        compiler_params=_cparams(("arbitrary", "arbitrary")),
        name="mlstm_scan_bwd" if rev else "mlstm_scan_fwd",
    )(qk3, pm3, pm3, gtl, gbr, gbt)


def _s5_kernel(nb, u_ref, bre_ref, bim_ref, are_ref, aim_ref, cre_ref, cim_ref, o_ref, xr_ref, xi_ref, st_ref):
    d, s = pl.program_id(0), pl.program_id(1)

    @pl.when(s == 0)
    def _():
        st_ref[...] = jnp.zeros(st_ref.shape, F32)

    n_sub = u_ref.shape[0] // (S5_SUB * nb)
    sub_rows = S5_SUB * nb

    def run(rev):
        ar, ai = are_ref[0], aim_ref[0]
        order = list(range(n_sub))[::-1] if rev else list(range(n_sub))

        def project_in(q):
            rs = slice(q * sub_rows, (q + 1) * sub_rows)
            u = u_ref[rs, :].astype(BF16)
            xr_ref[rs, :] = _dot(u, bre_ref[0])
            xi_ref[rs, :] = _dot(u, bim_ref[0])

        sr, si = st_ref[0], st_ref[1]
        project_in(order[0])
        for n, q in enumerate(order):
            if n + 1 < n_sub:
                project_in(order[n + 1])
            for j in range(S5_SUB):
                r0 = (q * S5_SUB + (S5_SUB - 1 - j if rev else j)) * nb
                nr = ar * sr - ai * si + xr_ref[r0:r0 + nb, :]
                ni = ar * si + ai * sr + xi_ref[r0:r0 + nb, :]
                xr_ref[r0:r0 + nb, :] = nr
                xi_ref[r0:r0 + nb, :] = ni
                sr, si = nr, ni
            rs = slice(q * sub_rows, (q + 1) * sub_rows)
            o_ref[0, rs, :] = (_dot(xr_ref[rs, :].astype(BF16), cre_ref[...])
                               - _dot(xi_ref[rs, :].astype(BF16), cim_ref[...]))
        st_ref[0] = sr
        st_ref[1] = si

    @pl.when(d == 0)
    def _():
        run(False)

    @pl.when(d == 1)
    def _():
        run(True)


def _s5(ut, bre, bim, are, aim, cre, cim, *, nb, nc, nt):
    n_rows, ch = ut.shape
    tr = S5_STEPS * nb
    ns = bre.shape[-1]
    pos = functools.partial(_scan_pos, ncb=nc, ntb=nt)
    return pl.pallas_call(
        functools.partial(_s5_kernel, nb),
        out_shape=jax.ShapeDtypeStruct((2, n_rows, ch), F32),
        grid=(2, nt),
        in_specs=[pl.BlockSpec((tr, ch), lambda d, s: (pos(d, s), 0)),
                  pl.BlockSpec((1, ch, ns), lambda d, s: (d, 0, 0)),
                  pl.BlockSpec((1, ch, ns), lambda d, s: (d, 0, 0)),
                  pl.BlockSpec((1, nb, ns), lambda d, s: (d, 0, 0)),
                  pl.BlockSpec((1, nb, ns), lambda d, s: (d, 0, 0)),
                  pl.BlockSpec((ns, ch), lambda d, s: (0, 0)),
                  pl.BlockSpec((ns, ch), lambda d, s: (0, 0))],
        out_specs=pl.BlockSpec((1, tr, ch), lambda d, s: (d, pos(d, s), 0)),
        scratch_shapes=[pltpu.VMEM((tr, ns), F32), pltpu.VMEM((tr, ns), F32), pltpu.VMEM((2, nb, ns), F32)],
        compiler_params=_cparams(("arbitrary", "arbitrary")),
        name="s5_scan",
    )(ut, bre, bim, are, aim, cre, cim)


def _head_norm(o, gain, dim):
    parts = []
    for h in range(o.shape[1] // LANE):
        seg = o[:, h * LANE:(h + 1) * LANE]
        ms = jnp.sum(seg * seg, axis=-1, keepdims=True) * (1.0 / dim)
        parts.append(seg * lax.rsqrt(ms + EPS))
    return jnp.concatenate(parts, axis=1) * gain


def _mix_kernel(with_router, ogf_ref, ogb_ref, gg_ref, ys_ref, u_ref, omf_ref, omb_ref, mo_ref, h_ref, mod_ref,
                gn_ref, mn_ref, sd_ref, gw_ref, gb_ref, wo_ref, n2_ref, *rest):
    if with_router:
        wr_ref, ho_ref, f_ref, rt_ref = rest
    else:
        ho_ref, f_ref = rest
    gla = _head_norm(ogf_ref[...].astype(F32) + ogb_ref[...].astype(F32), gn_ref[...], GLA_DV) * _silu(gg_ref[...])
    z = _gelu_tanh(ys_ref[0] + ys_ref[1] + sd_ref[...] * u_ref[...])
    s5 = z * jax.nn.sigmoid(_dot(z.astype(BF16), gw_ref[...]) + gb_ref[...])
    ml = _head_norm(omf_ref[...].astype(F32) + omb_ref[...].astype(F32), mn_ref[...], ML_DH) * jax.nn.sigmoid(mo_ref[...])
    mix = (_dot(gla.astype(BF16), wo_ref[0:HP]) + _dot(s5.astype(BF16), wo_ref[HP:HP + 2 * LANE])
           + _dot(ml.astype(BF16), wo_ref[HP + 2 * LANE:]))
    m = mod_ref[0]
    hn = h_ref[...] + m[2:3] * mix
    ho_ref[...] = hn
    f = _rmsnorm(hn, n2_ref[...]) * (1.0 + m[4:5]) + m[3:4]
    if not with_router:
        f_ref[...] = f.astype(f_ref.dtype)
    if with_router:
        f_hi, f_lo = _split_bf16(f, 2)
        logits = (_dot(f_hi, wr_ref[0]) + _dot(f_lo, wr_ref[0]) + _dot(f_hi, wr_ref[1])).T[0:2 * N_EXPERTS]
        row = lax.broadcasted_iota(jnp.int32, logits.shape, 0)
        l0 = jnp.where(row < N_EXPERTS, logits, NEG)
        m1 = jnp.max(l0, axis=0, keepdims=True)
        i1 = jnp.min(jnp.where(l0 == m1, row, 2 * N_EXPERTS), axis=0, keepdims=True)
        l1 = jnp.where(row == i1, NEG, l0)
        m2 = jnp.max(l1, axis=0, keepdims=True)
        i2 = jnp.min(jnp.where(l1 == m2, row, 2 * N_EXPERTS), axis=0, keepdims=True)
        e = jnp.exp(m2 - m1)
        w1 = 1.0 / (1.0 + e)
        w2 = e / (1.0 + e)
        rt = jnp.where(row == 0, i1.astype(F32),
                       jnp.where(row == 1, i2.astype(F32),
                                 jnp.where(row == 2, w1, jnp.where(row == 3, w2, 0.0))))
        rt_ref[...] = rt
        d = f.shape[1]
        f_ref[:, 0:d] = f
        f_ref[:, d:] = jnp.concatenate([rt, jnp.zeros((LANE - rt.shape[0], rt.shape[1]), F32)], axis=0).T


def _mix(ogf, ogb, pg, ys, pu, omf, omb, pm, h, modtab, gn, mn, sd, gw, gb, wo, n2, wr, *, layer, rows, lat_only):
    d = h.shape[1]
    tm = SEG
    n = rows.n_blocks(lat_only)
    src, sel = rows.src(lat_only), rows.sel(lat_only)
    full = lambda a: pl.BlockSpec(a.shape, lambda i: (0,) * a.ndim)
    with_router = wr is not None
    in_specs = [pl.BlockSpec((tm, HP), lambda i: (src(i), 0)),
                pl.BlockSpec((tm, HP), lambda i: (src(i), 0)),
                pl.BlockSpec((tm, HP), lambda i: (src(i), (2 * GQ + HP) // HP)),
                pl.BlockSpec((2, tm, 2 * LANE), lambda i: (0, src(i), 0)),
                pl.BlockSpec((tm, 2 * LANE), lambda i: (src(i), 0)),
                pl.BlockSpec((tm, HP), lambda i: (src(i), 0)),
                pl.BlockSpec((tm, HP), lambda i: (src(i), 0)),
                pl.BlockSpec((tm, HP), lambda i: (src(i), 1)),
                pl.BlockSpec((tm, d), lambda i: (src(i), 0)),
                pl.BlockSpec((None, 1, 8, d), lambda i: (layer, sel(i), 0, 0)),
                full(gn), full(mn), full(sd), full(gw), full(gb), full(wo), full(n2)]
    args = [ogf, ogb, pg, ys, pu, omf, omb, pm, h, modtab, gn, mn, sd, gw, gb, wo, n2]
    fw, fdt = (d + LANE, F32) if with_router else (d, BF16)
    out_shape = [jax.ShapeDtypeStruct((n * tm, d), F32), jax.ShapeDtypeStruct((n * tm, fw), fdt)]
    out_specs = [pl.BlockSpec((tm, d), lambda i: (i, 0)), pl.BlockSpec((tm, fw), lambda i: (i, 0))]
    if with_router:
        in_specs.append(full(wr))
        args.append(wr)
        out_shape.append(jax.ShapeDtypeStruct((2 * N_EXPERTS, n * tm), F32))
        out_specs.append(pl.BlockSpec((2 * N_EXPERTS, tm), lambda i: (0, i)))
    return pl.pallas_call(
        functools.partial(_mix_kernel, with_router),
        out_shape=tuple(out_shape),
        grid=(n,),
        in_specs=in_specs,
        out_specs=tuple(out_specs),
        compiler_params=_cparams(("arbitrary",)),
        name="mix_out",
    )(*args)


FF_TILE = 256


def _swiglu(xb, w1_ref, w3_ref, w2_ref, a_ref, lead=()):
    dff = w1_ref.shape[-1]
    for j in range(dff // FF_TILE):
        sl = slice(j * FF_TILE, (j + 1) * FF_TILE)
        h1 = _dot(xb, w1_ref[lead + (slice(None), sl)])
        h3 = _dot(xb, w3_ref[lead + (slice(None), sl)])
        a_ref[:, sl] = (_silu(h1) * h3).astype(BF16)
    return _dot(a_ref[...], w2_ref[lead + (slice(None), slice(None))])


FFN_TM = 1024
MIXFFN_TM = 2 * SEG


def _mix_ffn_kernel(sel, ogf_ref, ogb_ref, gg_ref, ys_ref, u_ref, omf_ref, omb_ref, mo_ref, h_ref, mod_ref,
                    gn_ref, mn_ref, sd_ref, gw_ref, gb_ref, wo_ref, n2_ref, w1_ref, w3_ref, w2_ref, o_ref,
                    a_ref, hn_ref, f_ref):
    n_seg = h_ref.shape[0] // SEG
    mods = []
    for q in range(n_seg):
        rs = slice(q * SEG, (q + 1) * SEG)
        m = mod_ref[sel(pl.program_id(0) * n_seg + q)]
        mods.append(m)
        gla = (_head_norm(ogf_ref[rs, :].astype(F32) + ogb_ref[rs, :].astype(F32), gn_ref[...], GLA_DV)
               * _silu(gg_ref[rs, :]))
        z = _gelu_tanh(ys_ref[0, rs, :] + ys_ref[1, rs, :] + sd_ref[...] * u_ref[rs, :])
        s5 = z * jax.nn.sigmoid(_dot(z.astype(BF16), gw_ref[...]) + gb_ref[...])
        ml = (_head_norm(omf_ref[rs, :].astype(F32) + omb_ref[rs, :].astype(F32), mn_ref[...], ML_DH)
              * jax.nn.sigmoid(mo_ref[rs, :]))
        mix = (_dot(gla.astype(BF16), wo_ref[0:HP]) + _dot(s5.astype(BF16), wo_ref[HP:HP + 2 * LANE])
               + _dot(ml.astype(BF16), wo_ref[HP + 2 * LANE:]))
        hn = h_ref[rs, :] + m[2:3] * mix
        hn_ref[rs, :] = hn
        f_ref[rs, :] = (_rmsnorm(hn, n2_ref[...]) * (1.0 + m[4:5]) + m[3:4]).astype(BF16)
    y = _swiglu(f_ref[...], w1_ref, w3_ref, w2_ref, a_ref)
    for q in range(n_seg):
        rs = slice(q * SEG, (q + 1) * SEG)
        o_ref[rs, :] = hn_ref[rs, :] + mods[q][5:6] * y[rs]


def _mix_ffn(ogf, ogb, pg, ys, pu, omf, omb, pm, h, modtab, gn, mn, sd, gw, gb, wo, n2, w1, w3, w2, *, layer, rows):
    r, d = h.shape
    tm = MIXFFN_TM
    assert r % tm == 0
    dff = w1.shape[-1]
    full = lambda a: pl.BlockSpec(a.shape, lambda i: (0,) * a.ndim)
    resident = lambda a: pl.BlockSpec(a.shape, lambda i: (0,) * a.ndim, pipeline_mode=pl.Buffered(1))
    in_specs = [pl.BlockSpec((tm, HP), lambda i: (i, 0)),
                pl.BlockSpec((tm, HP), lambda i: (i, 0)),
                pl.BlockSpec((tm, HP), lambda i: (i, (2 * GQ + HP) // HP)),
                pl.BlockSpec((2, tm, 2 * LANE), lambda i: (0, i, 0)),
                pl.BlockSpec((tm, 2 * LANE), lambda i: (i, 0)),
                pl.BlockSpec((tm, HP), lambda i: (i, 0)),
                pl.BlockSpec((tm, HP), lambda i: (i, 0)),
                pl.BlockSpec((tm, HP), lambda i: (i, 1)),
                pl.BlockSpec((tm, d), lambda i: (i, 0)),
                pl.BlockSpec((None,) + modtab.shape[1:], lambda i: (layer, 0, 0, 0)),
                full(gn), full(mn), full(sd), full(gw), full(gb), resident(wo), full(n2),
                resident(w1), resident(w3), resident(w2)]
    return pl.pallas_call(
        functools.partial(_mix_ffn_kernel, rows.sel(False)),
        out_shape=jax.ShapeDtypeStruct((r, d), F32),
        grid=(r // tm,),
        in_specs=in_specs,
        out_specs=pl.BlockSpec((tm, d), lambda i: (i, 0)),
        scratch_shapes=[pltpu.VMEM((tm, dff), BF16), pltpu.VMEM((tm, d), F32), pltpu.VMEM((tm, d), BF16)],
        compiler_params=_cparams(("arbitrary",)),
        name="mix_ffn",
    )(ogf, ogb, pg, ys, pu, omf, omb, pm, h, modtab, gn, mn, sd, gw, gb, wo, n2, w1, w3, w2)


def _ffn_kernel(final, sel, f_ref, h_ref, mod_ref, w1_ref, w3_ref, w2_ref, *rest):
    if final:
        nf_ref, o_ref, a_ref = rest
    else:
        o_ref, a_ref = rest
    y = _swiglu(f_ref[...], w1_ref, w3_ref, w2_ref, a_ref)
    n_seg = f_ref.shape[0] // SEG
    for q in range(n_seg):
        rs = slice(q * SEG, (q + 1) * SEG)
        gate = mod_ref[sel(pl.program_id(0) * n_seg + q)][5:6]
        hn = h_ref[rs, :] + gate * y[rs]
        o_ref[rs, :] = _rmsnorm(hn, nf_ref[...]) if final else hn


def _ffn(f, h, modtab, w1, w3, w2, nf, *, layer, rows, lat_only):
    r, d = h.shape
    tm = FFN_TM
    assert r % tm == 0
    dff = w1.shape[-1]
    full = lambda a: pl.BlockSpec(a.shape, lambda i: (0,) * a.ndim)
    resident = lambda a: pl.BlockSpec(a.shape, lambda i: (0,) * a.ndim, pipeline_mode=pl.Buffered(1))
    in_specs = [pl.BlockSpec((tm, d), lambda i: (i, 0)),
                pl.BlockSpec((tm, d), lambda i: (i, 0)),
                pl.BlockSpec((None,) + modtab.shape[1:], lambda i: (layer, 0, 0, 0)),
                resident(w1), resident(w3), resident(w2)]
    args = [f, h, modtab, w1, w3, w2]
    if lat_only:
        in_specs.append(full(nf))
        args.append(nf)
    return pl.pallas_call(
        functools.partial(_ffn_kernel, lat_only, rows.sel(lat_only)),
        out_shape=jax.ShapeDtypeStruct((r, d), F32),
        grid=(r // tm,),
        in_specs=in_specs,
        out_specs=pl.BlockSpec((tm, d), lambda i: (i, 0)),
        scratch_shapes=[pltpu.VMEM((tm, dff), BF16)],
        compiler_params=_cparams(("arbitrary",)),
        name="ffn",
    )(*args)


MOE_TM = 512


def _moe_kernel(be_ref, live_ref, x_ref, w1_ref, w3_ref, w2_ref, o_ref, a_ref):
    i = pl.program_id(0)
    d = o_ref.shape[1]

    @pl.when(live_ref[i] > 0)
    def _():
        y = _swiglu(x_ref[:, 0:d].astype(BF16), w1_ref, w3_ref, w2_ref, a_ref, lead=(0,))
        tail = x_ref[:, d:]
        mine = tail[:, 0:1] == be_ref[i].astype(F32)
        o_ref[...] = y * jnp.where(mine, tail[:, 2:3], tail[:, 3:4])

    @pl.when(live_ref[i] == 0)
    def _():
        o_ref[...] = jnp.zeros(o_ref.shape, F32)


def _moe_experts(block_expert, block_live, xg, w1, w3, w2):
    n_rows, dx = xg.shape
    d = w1.shape[-2]
    dff = w1.shape[-1]
    tm = MOE_TM
    return pl.pallas_call(
        _moe_kernel,
        out_shape=jax.ShapeDtypeStruct((n_rows, d), F32),
        grid_spec=pltpu.PrefetchScalarGridSpec(
            num_scalar_prefetch=2,
            grid=(n_rows // tm,),
            in_specs=[pl.BlockSpec((tm, dx), lambda i, be, lv: (i, 0)),
                      pl.BlockSpec((1, d, dff), lambda i, be, lv: (be[i], 0, 0)),
                      pl.BlockSpec((1, d, dff), lambda i, be, lv: (be[i], 0, 0)),
                      pl.BlockSpec((1, dff, d), lambda i, be, lv: (be[i], 0, 0))],
            out_specs=pl.BlockSpec((tm, d), lambda i, be, lv: (i, 0)),
            scratch_shapes=[pltpu.VMEM((tm, dff), BF16)]),
        compiler_params=_cparams(("arbitrary",)),
        name="moe_experts",
    )(block_expert, block_live, xg, w1, w3, w2)


SC_GATHER_ROWS = 64


def _gather_rows(table, idx):
    n_idx = idx.shape[0]
    _, d = table.shape
    info = plsc.get_sparse_core_info()
    n_cores, n_workers = info.num_cores, info.num_cores * info.num_subcores
    assert n_idx % (n_workers * SC_GATHER_ROWS) == 0
    per_worker = n_idx // n_workers
    mesh = plsc.VectorSubcoreMesh(core_axis_name="c", subcore_axis_name="s")

    @functools.partial(
        pl.kernel, mesh=mesh,
        out_type=jax.ShapeDtypeStruct((n_idx, d), table.dtype),
        scratch_types=[pltpu.VMEM((SC_GATHER_ROWS,), jnp.int32),
                       pltpu.VMEM((SC_GATHER_ROWS, d), table.dtype),
                       pltpu.SemaphoreType.DMA])
    def gather(table_hbm, idx_hbm, out_hbm, idx_v, rows_v, sem):
        base = (lax.axis_index("s") * n_cores + lax.axis_index("c")) * per_worker

        @pl.loop(0, per_worker // SC_GATHER_ROWS)
        def _(it):
            off = pl.multiple_of(base + it * SC_GATHER_ROWS, SC_GATHER_ROWS)
            pltpu.sync_copy(idx_hbm.at[pl.ds(off, SC_GATHER_ROWS)], idx_v)
            pltpu.async_copy(table_hbm.at[idx_v], rows_v, sem).wait()
            pltpu.sync_copy(rows_v, out_hbm.at[pl.ds(off, SC_GATHER_ROWS)])

    return gather(table, idx)


def _scatter_rows(src, dest, n_rows):
    n_tok, d = src.shape
    info = plsc.get_sparse_core_info()
    n_cores, n_workers = info.num_cores, info.num_cores * info.num_subcores
    assert n_tok % (n_workers * SC_GATHER_ROWS) == 0
    per_worker = n_tok // n_workers
    mesh = plsc.VectorSubcoreMesh(core_axis_name="c", subcore_axis_name="s")

    @functools.partial(
        pl.kernel, mesh=mesh,
        out_type=jax.ShapeDtypeStruct((n_rows, d), src.dtype),
        scratch_types=[pltpu.VMEM((SC_GATHER_ROWS,), jnp.int32),
                       pltpu.VMEM((SC_GATHER_ROWS, d), src.dtype),
                       pltpu.SemaphoreType.DMA])
    def scatter(src_hbm, dest_hbm, out_hbm, idx_v, rows_v, sem):
        base = (lax.axis_index("s") * n_cores + lax.axis_index("c")) * per_worker

        @pl.loop(0, per_worker // SC_GATHER_ROWS)
        def _(it):
            off = pl.multiple_of(base + it * SC_GATHER_ROWS, SC_GATHER_ROWS)
            pltpu.sync_copy(src_hbm.at[pl.ds(off, SC_GATHER_ROWS)], rows_v)
            for k in range(TOP_K):
                pltpu.sync_copy(dest_hbm.at[pl.ds(k * n_tok + off, SC_GATHER_ROWS)], idx_v)
                pltpu.async_copy(rows_v, out_hbm.at[idx_v], sem).wait()

    return scatter(src, dest.reshape(-1))


def _resid_kernel(final, h_ref, y0_ref, y1_ref, mod_ref, *rest):
    hn = h_ref[...] + mod_ref[0][5:6] * (y0_ref[...] + y1_ref[...])
    if final:
        nf_ref, o_ref = rest
        o_ref[...] = _rmsnorm(hn, nf_ref[...])
    else:
        (o_ref,) = rest
        o_ref[...] = hn


def _moe_resid(h, y0, y1, modtab, nf, *, layer, rows, lat_only):
    r, d = h.shape
    tm = SEG
    sel = rows.sel(lat_only)
    row = lambda i: (i, 0)
    in_specs = [pl.BlockSpec((tm, d), row), pl.BlockSpec((tm, d), row), pl.BlockSpec((tm, d), row),
                pl.BlockSpec((None, 1, 8, d), lambda i: (layer, sel(i), 0, 0))]
    args = [h, y0, y1, modtab]
    if lat_only:
        in_specs.append(pl.BlockSpec((1, d), lambda i: (0, 0)))
        args.append(nf)
    return pl.pallas_call(
        functools.partial(_resid_kernel, lat_only),
        out_shape=jax.ShapeDtypeStruct((r, d), F32),
        grid=(r // tm,),
        in_specs=in_specs,
        out_specs=pl.BlockSpec((tm, d), row),
        compiler_params=_cparams(("arbitrary",)),
        name="moe_resid",
    )(*args)


CAST_PARTS = 8


def _cast_kernel(*refs):
    o_ref = refs[-1]
    tr = refs[0].shape[1]
    for k, x_ref in enumerate(refs[:-1]):
        o_ref[0, k * tr:(k + 1) * tr, :] = x_ref[0].astype(o_ref.dtype)


def _to_bf16(w, j):
    lead, (r, c) = w.shape[1:-2], w.shape[-2:]
    n = int(np.prod(lead, dtype=np.int64))
    w3 = w.reshape((-1, r, c))
    tr = r // CAST_PARTS
    assert tr * CAST_PARTS == r and tr % 16 == 0
    band = lambda k: pl.BlockSpec((1, tr, c), lambda e: (j * n + e, k, 0))
    out = pl.pallas_call(
        _cast_kernel,
        out_shape=jax.ShapeDtypeStruct((n, r, c), BF16),
        grid=(n,),
        in_specs=[band(k) for k in range(CAST_PARTS)],
        out_specs=pl.BlockSpec((1, r, c), lambda e: (e, 0, 0)),
        compiler_params=_cparams(("arbitrary",)),
        name="to_bf16",
    )(*([w3] * CAST_PARTS))
    return out.reshape(lead + (r, c))


def _pad_heads(w, heads, dim, to=LANE):
    lead = w.shape[:-1]
    w = w.reshape(lead + (heads, dim))
    w = jnp.pad(w, [(0, 0)] * len(lead) + [(0, 0), (0, to - dim)])
    return w.reshape(lead + (heads * to,))


def _pad_last(w, to):
    return jnp.pad(w, [(0, 0)] * (w.ndim - 1) + [(0, to - w.shape[-1])])


def _pos_embed(n_tokens, d):
    n_grid_rows = n_tokens // GRID_W
    row, col = jnp.meshgrid(jnp.arange(n_grid_rows, dtype=F32), jnp.arange(GRID_W, dtype=F32), indexing='ij')
    n_freq = d // 4
    omega = jnp.exp(-math.log(POS_BASE) * jnp.arange(n_freq, dtype=F32) / n_freq)

    def axis_embed(p):
        ang = p.reshape(-1, 1) * omega
        return jnp.concatenate([jnp.sin(ang), jnp.cos(ang)], axis=-1)

    return jnp.concatenate([axis_embed(row), axis_embed(col)], axis=-1)


def _s5_discretise(lam_re, lam_im, log_dt, b_re, b_im):
    dt = jnp.exp(log_dt)[:, None]
    mag = jnp.exp(lam_re * dt)
    abar_re, abar_im = mag * jnp.cos(lam_im * dt), mag * jnp.sin(lam_im * dt)
    den = lam_re * lam_re + lam_im * lam_im
    pr, pi = abar_re - 1.0, abar_im
    coef_re = (pr * lam_re + pi * lam_im) / den
    coef_im = (pi * lam_re - pr * lam_im) / den
    bbar_re = coef_re[..., None] * b_re - coef_im[..., None] * b_im
    bbar_im = coef_re[..., None] * b_im + coef_im[..., None] * b_re
    return abar_re, abar_im, bbar_re, bbar_im


def _block_diag(m):
    g, a, b = m.shape
    eye = jnp.eye(g, dtype=m.dtype)
    return (eye[:, None, :, None] * m[:, :, None, :]).reshape(g * a, g * b)


def _route_plan(route, tm):
    n_tok = route.shape[1]
    n_assign = n_tok * TOP_K
    flat_e = route[0:TOP_K].astype(jnp.int32).reshape(-1)
    onehot = (jnp.arange(N_EXPERTS, dtype=jnp.int32)[:, None] == flat_e[None, :]).astype(jnp.int32)
    csum = jnp.cumsum(onehot, axis=1)
    counts = csum[:, -1]
    padded = (counts + tm - 1) // tm * tm
    pend = jnp.cumsum(padded)
    pstart = pend - padded
    dest = jnp.sum(onehot * (csum - 1 + pstart[:, None]), axis=0)
    n_blocks = -(-n_assign // tm) + N_EXPERTS
    block_start = jnp.arange(n_blocks, dtype=jnp.int32) * tm
    block_expert = jnp.minimum(jnp.searchsorted(pend, block_start, side='right'), N_EXPERTS - 1).astype(jnp.int32)
    block_live = (block_start < (pstart + counts)[block_expert]).astype(jnp.int32)
    return n_blocks * tm, block_expert, block_live, dest.reshape(TOP_K, n_tok)


def kernel(x, c, ctx, c_ctx, w_ada, b_ada, norm1, norm2, w_in, w_out, gla_wa2, gla_ba, gla_norm, s5_lam_re, s5_lam_im, s5_log_dt, s5_b_re, s5_b_im, s5_c_re, s5_c_im, s5_d, s5_glu_w, s5_glu_b, ml_conv_w, ml_conv_b, ml_gate_b, ml_norm, ffn_w1, ffn_w3, ffn_w2, moe_router, moe_w1, moe_w3, moe_w2, norm_f):
    nb, n_lat, d = x.shape
    lc = ctx.shape[1]
    depth = w_ada.shape[0]
    l = lc + n_lat
    assert lc % SEG == 0 and n_lat % SEG == 0 and nb == 8 and nb % SCAN_BATCHES == 0
    rows = _Rows(nb, lc // SEG, l // SEG)

    h = _embed(ctx.reshape(nb * lc, d), x.reshape(nb * n_lat, d), _pos_embed(n_lat, d), rows)

    cond = jnp.zeros((16, d), F32).at[:nb].set(c).at[nb].set(c_ctx)
    mod = _modulation(cond, w_ada, b_ada)
    modtab = jnp.pad(mod.reshape(depth, 16, 6, d), ((0, 0), (0, 0), (0, 2), (0, 0)))

    dk, dv, dh = GLA_HEADS * GLA_DK, GLA_HEADS * GLA_DV, ML_HEADS * ML_DH
    s5c = s5_d.shape[-1]
    cuts = np.cumsum([dk, dk, dv, GLA_RANK, dv, s5c, dh, dh, dh, dh, 4 * ML_HEADS])

    for i in range(depth):
        last = i == depth - 1
        gq, gk, gv, glr, gg, su, mq, mk, mv, mo, mg = jnp.split(w_in[i], cuts[:-1], axis=-1)
        w_all = jnp.concatenate([
            _pad_heads(gq, GLA_HEADS, GLA_DK, GLA_KP), _pad_heads(gk, GLA_HEADS, GLA_DK, GLA_KP),
            _pad_heads(gv, GLA_HEADS, GLA_DV), _pad_heads(gg, GLA_HEADS, GLA_DV), _pad_last(glr, LANE),
            su,
            _pad_heads(mv, ML_HEADS, ML_DH), _pad_heads(mo, ML_HEADS, ML_DH),
            _pad_last(mg[:, :2 * ML_HEADS], LANE), _pad_last(mg[:, 2 * ML_HEADS:], LANE),
            _pad_heads(mq, ML_HEADS, ML_DH), _pad_heads(mk, ML_HEADS, ML_DH)],
            axis=-1).astype(BF16)
        wgt = mg.T.astype(BF16)
        cw = jnp.concatenate([_pad_heads(ml_conv_w[i][:, :dh], ML_HEADS, ML_DH),
                              _pad_heads(ml_conv_w[i][:, dh:], ML_HEADS, ML_DH)], axis=-1)
        cw = jnp.pad(cw, ((0, 8 - ML_CONV), (0, 0)))
        cb = jnp.concatenate([_pad_heads(ml_conv_b[i][:dh], ML_HEADS, ML_DH),
                              _pad_heads(ml_conv_b[i][dh:], ML_HEADS, ML_DH)])[None]
        post = jnp.concatenate([jnp.ones((HP,), F32), jnp.full((HP,), ML_DH ** -0.5, F32)])[None]
        pg, pu, pm, qk, gt = _proj(h, modtab, norm1[i][None], w_all, wgt, cw, cb, post, layer=i, rows=rows)

        wa = jnp.pad(_pad_heads(gla_wa2[i], GLA_HEADS, GLA_DK, GLA_KP), ((0, 0), (0, LANE - GLA_RANK), (0, 0)))
        ba = _pad_heads(gla_ba[i], GLA_HEADS, GLA_DK, GLA_KP)[:, None, :]
        pg3 = pg.reshape(nb, l, NG)
        ogf = _gla(pg3, wa[0], ba[0], rows=rows, rev=False).reshape(nb * l, HP)
        ogb = _gla(pg3, wa[1], ba[1], rows=rows, rev=True).reshape(nb * l, HP)

        bres, bims, ares, aims = [], [], [], []
        for dr in (0, 1):
            a_re, a_im, b_re, b_im = _s5_discretise(s5_lam_re[i, dr], s5_lam_im[i, dr], s5_log_dt[i, dr],
                                                    s5_b_re[i], s5_b_im[i])
            bres.append(_block_diag(jnp.swapaxes(b_re, 1, 2)))
            bims.append(_block_diag(jnp.swapaxes(b_im, 1, 2)))
            ares.append(jnp.broadcast_to(a_re.reshape(1, -1), (nb, a_re.size)))
            aims.append(jnp.broadcast_to(a_im.reshape(1, -1), (nb, a_im.size)))
        cre = _block_diag(jnp.swapaxes(s5_c_re[i], 1, 2)).astype(BF16)
        cim = _block_diag(jnp.swapaxes(s5_c_im[i], 1, 2)).astype(BF16)
        ut = pu.reshape(nb, l, s5c).swapaxes(0, 1).reshape(l * nb, s5c)
        yt = _s5(ut, jnp.stack(bres).astype(BF16), jnp.stack(bims).astype(BF16), jnp.stack(ares), jnp.stack(aims),
                 cre, cim, nb=nb, nc=lc // S5_STEPS, nt=l // S5_STEPS)
        ys = yt.reshape(2, l, nb, s5c).swapaxes(1, 2).reshape(2, nb * l, s5c)

        gb = ml_gate_b[i].reshape(2, 2, ML_HEADS)
        gbr = _pad_last(gb.reshape(2, 1, 2 * ML_HEADS), LANE)
        gbt = _pad_last(jnp.broadcast_to(gb[..., None], (2, 2, ML_HEADS, CHUNK)), LANE).reshape(2, 2, HP)
        gtl = _pad_last(gt.reshape(2, 2, ML_HEADS, nb, l // CHUNK, CHUNK).transpose(3, 4, 0, 1, 2, 5),
                        LANE).reshape(nb, l // CHUNK, 2, 2, HP)
        qk3, pm3 = qk.reshape(nb, l, 2 * HP), pm.reshape(nb, l, NM)
        omf = _mlstm(qk3, pm3, gtl, gbr[0], gbt[0], rows=rows, rev=False).reshape(nb * l, HP)
        omb = _mlstm(qk3, pm3, gtl, gbr[1], gbt[1], rows=rows, rev=True).reshape(nb * l, HP)

        wo = w_out[i]
        wo_p = jnp.concatenate([
            jnp.pad(wo[:dv].reshape(GLA_HEADS, GLA_DV, d), ((0, 0), (0, LANE - GLA_DV), (0, 0))).reshape(HP, d),
            wo[dv:dv + s5c],
            jnp.pad(wo[dv + s5c:].reshape(ML_HEADS, ML_DH, d), ((0, 0), (0, LANE - ML_DH), (0, 0))).reshape(HP, d)],
            axis=0).astype(BF16)
        gn = jnp.tile(_pad_last(gla_norm[i], LANE), GLA_HEADS)[None]
        mn = jnp.tile(_pad_last(ml_norm[i], LANE), ML_HEADS)[None]
        is_moe = i % 2 == 1
        j = i // 2
        wr = jnp.stack(_split_bf16(_pad_last(moe_router[j], LANE), 2)) if is_moe else None
        mix_args = (ogf, ogb, pg, ys, pu, omf, omb, pm, h, modtab, gn, mn, s5_d[i][None], s5_glu_w[i].astype(BF16),
                    s5_glu_b[i][None], wo_p, norm2[i][None])
        if not is_moe and not last:
            h = _mix_ffn(*mix_args, _to_bf16(ffn_w1, j), _to_bf16(ffn_w3, j), _to_bf16(ffn_w2, j),
                         layer=i, rows=rows)
            continue
        outs = _mix(*mix_args, wr, layer=i, rows=rows, lat_only=last)
        if not is_moe:
            h, f = outs
            h = _ffn(f, h, modtab, _to_bf16(ffn_w1, j), _to_bf16(ffn_w3, j), _to_bf16(ffn_w2, j),
                     norm_f[None], layer=i, rows=rows, lat_only=last)
        else:
            h, f, route = outs
            n_rows, block_expert, block_live, dest = _route_plan(route, MOE_TM)
            xg = _scatter_rows(f, dest, n_rows)
            yg = _moe_experts(block_expert, block_live, xg, _to_bf16(moe_w1, j), _to_bf16(moe_w3, j),
                              _to_bf16(moe_w2, j))
            y0 = _gather_rows(yg, dest[0])
            y1 = _gather_rows(yg, dest[1])
            h = _moe_resid(h, y0, y1, modtab, norm_f[None], layer=i, rows=rows, lat_only=last)
    return h.reshape(nb, n_lat, d)
```

```python
import functools
import math

import numpy as np
import jax
import jax.numpy as jnp
from jax import lax
from jax.experimental import pallas as pl
from jax.experimental.pallas import tpu as pltpu
from jax.experimental.pallas import tpu_sc as plsc

F32 = jnp.float32
BF16 = jnp.bfloat16
HIGHEST = lax.Precision.HIGHEST

GRID_W = 64
POS_BASE = 10000.0
EPS = 1e-6
GLA_HEADS, GLA_DK, GLA_DV, GLA_RANK, GLA_GATE_NORM = 4, 48, 96, 16, 16.0
S5_GROUP, S5_STATE = 16, 64
ML_HEADS, ML_DH, ML_CONV = 4, 96, 3
N_EXPERTS, TOP_K = 8, 2

LANE = 128
CHUNK = 64
SEG = 256
N_SUB = SEG // CHUNK
SCAN_BATCHES = 8
S5_STEPS = 128
S5_SUB = 32
NEG = -1e30
VMEM_LIMIT = 56 * 1024 * 1024

HP = LANE * GLA_HEADS
GLA_KP = 64
GQ = GLA_HEADS * GLA_KP
NG = 2 * GQ + HP + LANE
NM = HP + 2 * LANE
NGO = 2 * HP


def _cparams(sem):
    return pltpu.CompilerParams(dimension_semantics=sem, vmem_limit_bytes=VMEM_LIMIT)


def _dot(a, b, **kw):
    return jnp.dot(a, b, preferred_element_type=F32, **kw)


def _dot_nt(a, b, **kw):
    return lax.dot_general(a, b, (((1,), (1,)), ((), ())), preferred_element_type=F32, **kw)


def _dot_tn(a, b, **kw):
    return lax.dot_general(a, b, (((0,), (0,)), ((), ())), preferred_element_type=F32, **kw)


def _log_sigmoid(x):
    return jnp.minimum(x, 0.0) - jnp.log1p(jnp.exp(-jnp.abs(x)))


def _silu(x):
    return x * jax.nn.sigmoid(x)


def _gelu_tanh(x):
    return 0.5 * x * (1.0 + jnp.tanh(math.sqrt(2.0 / math.pi) * (x + 0.044715 * (x * x * x))))


def _rmsnorm(x, g):
    return x * lax.rsqrt(jnp.mean(x * x, axis=-1, keepdims=True) + EPS) * g


class _Rows:
    def __init__(self, nb, ncb, ntb):
        self.nb, self.ncb, self.ntb, self.nlb = nb, ncb, ntb, ntb - ncb

    def n_blocks(self, lat_only):
        return self.nb * (self.nlb if lat_only else self.ntb)

    def src(self, lat_only):
        if lat_only:
            return lambda i: (i // self.nlb) * self.ntb + self.ncb + i % self.nlb
        return lambda i: i

    def sel(self, lat_only):
        if lat_only:
            return lambda i: i // self.nlb
        return lambda i: jnp.where(i % self.ntb < self.ncb, self.nb, i // self.ntb)


def _embed_kernel(ncb, ntb, ctx_ref, x_ref, pos_ref, o_ref):
    j = pl.program_id(0) % ntb

    @pl.when(j < ncb)
    def _():
        o_ref[...] = ctx_ref[...]

    @pl.when(j >= ncb)
    def _():
        o_ref[...] = x_ref[...] + pos_ref[...]


def _embed(ctx2, x2, pos, rows):
    d = ctx2.shape[1]
    ncb, ntb, nlb = rows.ncb, rows.ntb, rows.nlb
    return pl.pallas_call(
        functools.partial(_embed_kernel, ncb, ntb),
        out_shape=jax.ShapeDtypeStruct((rows.nb * ntb * SEG, d), F32),
        grid=(rows.nb * ntb,),
        in_specs=[pl.BlockSpec((SEG, d), lambda i: ((i // ntb) * ncb + jnp.minimum(i % ntb, ncb - 1), 0)),
                  pl.BlockSpec((SEG, d), lambda i: ((i // ntb) * nlb + jnp.maximum(i % ntb - ncb, 0), 0)),
                  pl.BlockSpec((SEG, d), lambda i: (jnp.maximum(i % ntb - ncb, 0), 0))],
        out_specs=pl.BlockSpec((SEG, d), lambda i: (i, 0)),
        compiler_params=_cparams(("arbitrary",)),
        name="embed",
    )(ctx2, x2, pos)


def _mod_kernel(c_ref, w_ref, b_ref, o_ref):
    s = _silu(c_ref[...])
    o_ref[0] = _dot(s, w_ref[0], precision=HIGHEST) + b_ref[0]


def _modulation(cond, w_ada, b_ada):
    depth, d, n6 = w_ada.shape
    tn = n6 // 4
    n_rows = cond.shape[0]
    return pl.pallas_call(
        _mod_kernel,
        out_shape=jax.ShapeDtypeStruct((depth, n_rows, n6), F32),
        grid=(depth, n6 // tn),
        in_specs=[pl.BlockSpec((n_rows, d), lambda l, j: (0, 0)),
                  pl.BlockSpec((1, d, tn), lambda l, j: (l, 0, j)),
                  pl.BlockSpec((1, 1, tn), lambda l, j: (l, 0, j))],
        out_specs=pl.BlockSpec((1, n_rows, tn), lambda l, j: (l, 0, j)),
        compiler_params=_cparams(("arbitrary", "arbitrary")),
        name="modulation",
    )(cond, w_ada, b_ada.reshape(depth, 1, n6))


PROJ_TM = 2 * SEG


def _proj_kernel(ncb, ntb, sel, h_ref, hp_ref, hn_ref, mod_ref, g_ref, w_ref, wgt_ref, cw_ref, cb_ref, post_ref,
                 pg_ref, pu_ref, pm_ref, qk_ref, go_ref, gt_ref):
    tm = h_ref.shape[0]
    n_seg = tm // SEG
    blk = [pl.program_id(0) * n_seg + q for q in range(n_seg)]
    mods = [mod_ref[sel(bq)] for bq in blk]
    act = lambda x, m: _rmsnorm(x, g_ref[...]) * (1.0 + m[1:2]) + m[0:1]
    a_seg = [act(h_ref[q * SEG:(q + 1) * SEG, :], mods[q]) for q in range(n_seg)]
    ab = jnp.concatenate(a_seg, axis=0).astype(BF16)
    c1, c2 = NG + 2 * LANE, NG + 2 * LANE + NM
    c3 = c2 + 2 * HP
    go_ref[...] = _dot(ab, w_ref[:, c3:]).astype(go_ref.dtype)
    pg_ref[...] = _dot(ab, w_ref[:, 0:NG])
    pu_ref[...] = _dot(ab, w_ref[:, NG:c1])
    pm_ref[...] = _dot(ab, w_ref[:, c1:c2])
    gt_ref[...] = _dot_nt(wgt_ref[...], ab)
    ae = jnp.concatenate([act(hp_ref[...], mods[0])] + a_seg + [act(hn_ref[...], mods[-1])], axis=0)
    xe = _dot(ae.astype(BF16), w_ref[:, c2:c3])
    n_e = tm + 16
    row = lax.broadcasted_iota(jnp.int32, (tm, xe.shape[1]), 0)
    keep_prev = jnp.ones((tm, xe.shape[1]), F32)
    keep_next = keep_prev
    for q, bq in enumerate(blk):
        j = bq % ntb
        first = jnp.logical_or(j == 0, j == ncb).astype(F32)
        last = jnp.logical_or(j == ncb - 1, j == ntb - 1).astype(F32)
        keep_prev = jnp.where(row == q * SEG, 1.0 - first, keep_prev)
        keep_next = jnp.where(row == (q + 1) * SEG - 1, 1.0 - last, keep_next)
    xp = pltpu.roll(xe, 1, axis=0)[8:8 + tm] * keep_prev
    xn = pltpu.roll(xe, n_e - 1, axis=0)[8:8 + tm] * keep_next
    y = cw_ref[0:1] * xp + cw_ref[1:2] * xe[8:8 + tm] + cw_ref[2:3] * xn + cb_ref[...]
    qk_ref[...] = _silu(y) * post_ref[...]


def _proj(h, modtab, g, w, wgt, cw, cb, post, *, layer, rows):
    r, d = h.shape
    tm = PROJ_TM
    assert r % tm == 0
    t8 = tm // 8
    full = lambda a: pl.BlockSpec(a.shape, lambda i: (0,) * a.ndim)
    return pl.pallas_call(
        functools.partial(_proj_kernel, rows.ncb, rows.ntb, rows.sel(False)),
        out_shape=(jax.ShapeDtypeStruct((r, NG), F32), jax.ShapeDtypeStruct((r, 2 * LANE), F32),
                   jax.ShapeDtypeStruct((r, NM), F32), jax.ShapeDtypeStruct((r, 2 * HP), F32),
                   jax.ShapeDtypeStruct((r, NGO), BF16),
                   jax.ShapeDtypeStruct((16, r), F32)),
        grid=(r // tm,),
        in_specs=[pl.BlockSpec((tm, d), lambda i: (i, 0)),
                  pl.BlockSpec((8, d), lambda i: (jnp.maximum(i * t8 - 1, 0), 0)),
                  pl.BlockSpec((8, d), lambda i: (jnp.minimum((i + 1) * t8, r // 8 - 1), 0)),
                  pl.BlockSpec((None,) + modtab.shape[1:], lambda i: (layer, 0, 0, 0)),
                  full(g), full(w), full(wgt), full(cw), full(cb), full(post)],
        out_specs=(pl.BlockSpec((tm, NG), lambda i: (i, 0)),
                   pl.BlockSpec((tm, 2 * LANE), lambda i: (i, 0)),
                   pl.BlockSpec((tm, NM), lambda i: (i, 0)),
                   pl.BlockSpec((tm, 2 * HP), lambda i: (i, 0)),
                   pl.BlockSpec((tm, NGO), lambda i: (i, 0)),
                   pl.BlockSpec((16, tm), lambda i: (0, i))),
        compiler_params=_cparams(("arbitrary",)),
        name="proj",
    )(h, h, h, modtab, g, w, wgt, cw, cb, post)


def _conv_kernel(ncb, ntb, x_ref, prev_ref, next_ref, w_ref, b_ref, post_ref, o_ref):
    j = pl.program_id(0) % ntb
    x = x_ref[...]
    tm = x.shape[0]
    first = jnp.logical_or(j == 0, j == ncb).astype(F32)
    last = jnp.logical_or(j == ncb - 1, j == ntb - 1).astype(F32)
    row = lax.broadcasted_iota(jnp.int32, x.shape, 0)
    xp = jnp.where(row == 0, prev_ref[7:8, :] * (1.0 - first), pltpu.roll(x, 1, axis=0))
    xn = jnp.where(row == tm - 1, next_ref[0:1, :] * (1.0 - last), pltpu.roll(x, tm - 1, axis=0))
    y = w_ref[0:1] * xp + w_ref[1:2] * x + w_ref[2:3] * xn + b_ref[...]
    o_ref[...] = _silu(y) * post_ref[...]


def _conv(pm, w, b, post, *, rows):
    r = pm.shape[0]
    tm = SEG
    wq = 2 * HP
    t8 = tm // 8
    return pl.pallas_call(
        functools.partial(_conv_kernel, rows.ncb, rows.ntb),
        out_shape=jax.ShapeDtypeStruct((r, wq), F32),
        grid=(r // tm,),
        in_specs=[pl.BlockSpec((tm, wq), lambda i: (i, 0)),
                  pl.BlockSpec((8, wq), lambda i: (jnp.maximum(i * t8 - 1, 0), 0)),
                  pl.BlockSpec((8, wq), lambda i: (jnp.minimum((i + 1) * t8, r // 8 - 1), 0)),
                  pl.BlockSpec((8, wq), lambda i: (0, 0)),
                  pl.BlockSpec((1, wq), lambda i: (0, 0)),
                  pl.BlockSpec((1, wq), lambda i: (0, 0))],
        out_specs=pl.BlockSpec((tm, wq), lambda i: (i, 0)),
        compiler_params=_cparams(("arbitrary",)),
        name="conv",
    )(pm, pm, pm, w, b, post)


def _scan_pos(d, s, ncb, ntb):
    rev = jnp.where(s < ncb, ncb - 1 - s, ntb - 1 - (s - ncb))
    return jnp.where(d == 0, s, rev)


def _scan_pos_static(rev, s, ncb, ntb):
    if not rev:
        return s
    return jnp.where(s < ncb, ncb - 1 - s, ntb - 1 - (s - ncb))


def _tri(rev):
    r = lax.broadcasted_iota(jnp.int32, (CHUNK, CHUNK), 0)
    c = lax.broadcasted_iota(jnp.int32, (CHUNK, CHUNK), 1)
    return (r <= c) if rev else (r >= c)


def _chunk_rows(rev):
    return [(N_SUB - 1 - j if rev else j) * CHUNK for j in range(N_SUB)]


def _gla_kernel(rev, p_ref, wa_ref, ba_ref, o_ref, st_ref):
    s, g = pl.program_id(0), pl.program_id(1)
    nbb = p_ref.shape[0]
    b0 = g * nbb

    @pl.when(s == 0)
    def _():
        st_ref[pl.ds(b0, nbb)] = jnp.zeros((nbb,) + st_ref.shape[1:], F32)

    valid = _tri(rev)
    tri = valid.astype(F32)
    r0s = _chunk_rows(rev)
    wa, ba = wa_ref[...], ba_ref[...]
    inst = [(bb, j) for bb in range(nbb) for j in range(N_SUB)]
    heads = [slice(h * LANE, (h + 1) * LANE) for h in range(GLA_HEADS)]
    pairs = [slice((h // 2) * LANE, (h // 2 + 1) * LANE) for h in range(GLA_HEADS)]
    lane = lax.broadcasted_iota(jnp.int32, (CHUNK, LANE), 1)
    own = [(lane // GLA_KP) == (h % 2) for h in range(GLA_HEADS)]

    la = {}
    for bb, j in inst:
        lr = p_ref[bb, pl.ds(r0s[j], CHUNK), 2 * GQ + HP:NG]
        la[bb, j] = _log_sigmoid(_dot(lr, wa) + ba) * (1.0 / GLA_GATE_NORM)
    bc, e_last = {}, {}
    for i in inst:
        bc[i] = _dot_exact01(tri, la[i], lhs_is_01=True, pieces=2)
        e_last[i] = jnp.exp(jnp.sum(la[i], axis=0, keepdims=True))
    q_in, k_in, k_out, v = {}, {}, {}, {}
    for bb, j in inst:
        i = (bb, j)
        rs = pl.ds(r0s[j], CHUNK)
        qs = (p_ref[bb, rs, 0:GQ] * (GLA_DK ** -0.5) * jnp.exp(bc[i])).astype(BF16)
        for h in range(GLA_HEADS):
            q_in[i, h] = jnp.where(own[h], qs[:, pairs[h]], jnp.zeros_like(qs[:, pairs[h]]))
        kd = p_ref[bb, rs, GQ:2 * GQ] * jnp.exp(-bc[i])
        k_out[i] = (kd * e_last[i]).astype(BF16)
        k_in[i] = kd.astype(BF16)
        v[i] = p_ref[bb, rs, 2 * GQ:2 * GQ + HP].astype(BF16)
    att = {}
    for i in inst:
        for h in range(GLA_HEADS):
            att[i, h] = jnp.where(valid, _dot_nt(q_in[i, h], k_in[i][:, pairs[h]]), 0.0).astype(BF16)
    o_intra, ds = {}, {}
    for i in inst:
        for h, sl in enumerate(heads):
            o_intra[i, h] = _dot(att[i, h], v[i][:, sl])
            ds[i, h] = _dot_tn(v[i][:, sl], k_out[i][:, pairs[h]])
    s_in = {}
    for bb in range(nbb):
        for h in range(GLA_HEADS):
            st = st_ref[b0 + bb, h]
            for j in range(N_SUB):
                s_in[(bb, j), h] = st.astype(BF16)
                st = st * e_last[bb, j][:, pairs[h]] + ds[(bb, j), h]
            st_ref[b0 + bb, h] = st
    for bb, j in inst:
        for h, sl in enumerate(heads):
            o = o_intra[(bb, j), h] + _dot_nt(q_in[(bb, j), h], s_in[(bb, j), h])
            o_ref[bb, pl.ds(r0s[j], CHUNK), sl] = o.astype(o_ref.dtype)


def _gla(pg3, wa, ba, *, rows, rev):
    nb, l, _ = pg3.shape
    nbb = SCAN_BATCHES
    pos = functools.partial(_scan_pos_static, rev, ncb=rows.ncb, ntb=rows.ntb)
    return pl.pallas_call(
        functools.partial(_gla_kernel, rev),
        out_shape=jax.ShapeDtypeStruct((nb, l, HP), BF16),
        grid=(rows.ntb, nb // nbb),
        in_specs=[pl.BlockSpec((nbb, SEG, NG), lambda s, g: (g, pos(s), 0)),
                  pl.BlockSpec((LANE, GQ), lambda s, g: (0, 0)),
                  pl.BlockSpec((1, GQ), lambda s, g: (0, 0))],
        out_specs=pl.BlockSpec((nbb, SEG, HP), lambda s, g: (g, pos(s), 0)),
        scratch_shapes=[pltpu.VMEM((nb, GLA_HEADS, LANE, LANE), F32)],
        compiler_params=_cparams(("arbitrary", "arbitrary")),
        name="gla_scan_bwd" if rev else "gla_scan_fwd",
    )(pg3, wa, ba)


def _mlstm_kernel_old(qk_ref, v_ref, g_ref, gt_ref, gbr_ref, gbc_ref, o_ref, st_ref, m_ref):
    d, s, g = pl.program_id(0), pl.program_id(1), pl.program_id(2)
    nbb = qk_ref.shape[0]
    b0 = g * nbb

    @pl.when(s == 0)
    def _():
        st_ref[pl.ds(b0, nbb)] = jnp.zeros((nbb,) + st_ref.shape[1:], F32)
        m_ref[pl.ds(b0, nbb)] = jnp.zeros((nbb,) + m_ref.shape[1:], F32)

    fwd = d == 0
    valid = _tri(d)
    tri = valid.astype(F32)
    r0s = _chunk_rows(d)
    cidx = [jnp.where(fwd, j, N_SUB - 1 - j) for j in range(N_SUB)]
    lane = lax.broadcasted_iota(jnp.int32, (CHUNK, LANE), 1)
    inst = [(bb, j) for bb in range(nbb) for j in range(N_SUB)]
    hinst = [(bb, j, h) for bb, j in inst for h in range(ML_HEADS)]
    gbr, gbc = gbr_ref[...], gbc_ref[...]

    def pick(a, h, kind, axis):
        i0, i1 = kind * ML_HEADS + h, (2 + kind) * ML_HEADS + h
        if axis == 1:
            return jnp.where(fwd, a[:, i0:i0 + 1], a[:, i1:i1 + 1])
        return jnp.where(fwd, a[i0:i0 + 1, :], a[i1:i1 + 1, :])

    gc, gr, fcum_c, fcum_r = {}, {}, {}, {}
    for bb, j in inst:
        i = (bb, j)
        gc[i] = g_ref[bb, pl.ds(r0s[j], CHUNK), :] + gbr
        gr[i] = gt_ref[bb, cidx[j]] + gbc
    for i in inst:
        fcum_c[i] = _dot(tri, _log_sigmoid(gc[i]), precision=HIGHEST)
        fcum_r[i] = _dot_nt(_log_sigmoid(gr[i]), tri, precision=HIGHEST)
    fc, lic, d_log, rmax = {}, {}, {}, {}
    for bb, j, h in hinst:
        i = (bb, j)
        fc[bb, j, h] = pick(fcum_c[i], h, 1, 1)
        lic[bb, j, h] = pick(gc[i], h, 0, 1)
        dl = jnp.where(valid, fc[bb, j, h] - pick(fcum_r[i], h, 1, 0) + pick(gr[i], h, 0, 0), NEG)
        d_log[bb, j, h] = dl
        rmax[bb, j, h] = jnp.max(dl, axis=-1, keepdims=True)
    m_prev, m_t, m_new, f_tot = {}, {}, {}, {}
    for bb in range(nbb):
        for h in range(ML_HEADS):
            m = m_ref[b0 + bb, h:h + 1, 0:1]
            for j in range(N_SUB):
                i = (bb, j, h)
                m_prev[i] = m
                m_t[i] = jnp.maximum(fc[i] + m, rmax[i])
                m = jnp.where(fwd, m_t[i][CHUNK - 1:CHUNK], m_t[i][0:1])
                m_new[i] = m
                f_tot[i] = jnp.where(fwd, fc[i][CHUNK - 1:CHUNK], fc[i][0:1])
            m_ref[b0 + bb, h:h + 1, :] = jnp.broadcast_to(m, (1, LANE))
    q, v, qk, w_prev, gdec, ds = {}, {}, {}, {}, {}, {}
    for bb, j, h in hinst:
        i = (bb, j, h)
        rs = pl.ds(r0s[j], CHUNK)
        sl = slice(h * LANE, (h + 1) * LANE)
        q[i] = qk_ref[bb, rs, sl].astype(BF16)
        k = qk_ref[bb, rs, HP + h * LANE:HP + (h + 1) * LANE]
        v[i] = jnp.where(lane == ML_DH, 1.0, v_ref[bb, rs, sl]).astype(BF16)
        w = jnp.exp(d_log[i] - m_t[i])
        w_prev[i] = jnp.exp(fc[i] + m_prev[i] - m_t[i])
        qk[i] = (_dot_nt(q[i], k.astype(BF16)) * w).astype(BF16)
        w_s = jnp.exp(f_tot[i] - fc[i] + lic[i] - m_new[i])
        gdec[i] = jnp.exp(f_tot[i] + m_prev[i] - m_new[i])
        ds[i] = _dot_tn(v[i], (k * w_s).astype(BF16))
    s_in = {}
    for bb in range(nbb):
        for h in range(ML_HEADS):
            st = st_ref[b0 + bb, h]
            for j in range(N_SUB):
                i = (bb, j, h)
                s_in[i] = st.astype(BF16)
                st = gdec[i] * st + ds[i]
            st_ref[b0 + bb, h] = st
    for bb, j, h in hinst:
        i = (bb, j, h)
        num = w_prev[i] * _dot_nt(q[i], s_in[i]) + _dot(qk[i], v[i])
        den = num[:, ML_DH:ML_DH + 1]
        hh = num / jnp.maximum(jnp.abs(den), jnp.exp(-m_t[i]))
        o_ref[0, bb, pl.ds(r0s[j], CHUNK), h * LANE:(h + 1) * LANE] = jnp.where(lane < ML_DH, hh, 0.0)


def _mlstm_old(qk3, pm3, gt4, gbr, gbc, *, rows):
    nb, l, _ = qk3.shape
    nbb = SCAN_BATCHES
    pos = functools.partial(_scan_pos, ncb=rows.ncb, ntb=rows.ntb)
    return pl.pallas_call(
        _mlstm_kernel,
        out_shape=jax.ShapeDtypeStruct((2, nb, l, HP), F32),
        grid=(2, rows.ntb, nb // nbb),
        in_specs=[pl.BlockSpec((nbb, SEG, 2 * HP), lambda d, s, g: (g, pos(d, s), 0)),
                  pl.BlockSpec((nbb, SEG, HP), lambda d, s, g: (g, pos(d, s), 2)),
                  pl.BlockSpec((nbb, SEG, LANE), lambda d, s, g: (g, pos(d, s), 4 * GLA_HEADS)),
                  pl.BlockSpec((nbb, N_SUB, 16, CHUNK), lambda d, s, g: (g, pos(d, s), 0, 0)),
                  pl.BlockSpec((1, LANE), lambda d, s, g: (0, 0)),
                  pl.BlockSpec((16, CHUNK), lambda d, s, g: (0, 0))],
        out_specs=pl.BlockSpec((1, nbb, SEG, HP), lambda d, s, g: (d, g, pos(d, s), 0)),
        scratch_shapes=[pltpu.VMEM((nb, ML_HEADS, LANE, LANE), F32), pltpu.VMEM((nb, 8, LANE), F32)],
        compiler_params=_cparams(("arbitrary", "arbitrary", "arbitrary")),
        name="mlstm_scan",
    )(qk3, pm3, pm3, gt4, gbr, gbc)


def _split_bf16(x, n):
    parts, r = [], x
    for _ in range(n):
        p = r.astype(BF16)
        parts.append(p)
        r = r - p.astype(F32)
    return parts


def _dot_exact01(a, b, lhs_is_01, pieces=3):
    if lhs_is_01:
        a = a.astype(BF16)
        terms = [_dot(a, p) for p in _split_bf16(b, pieces)]
    else:
        b = b.astype(BF16)
        terms = [_dot(p, b) for p in _split_bf16(a, pieces)]
    return functools.reduce(lambda x, y: x + y, terms)


def _cummax_rows(a, rev):
    n = a.shape[0]
    row = lax.broadcasted_iota(jnp.int32, a.shape, 0)
    k = 1
    while k < n:
        if rev:
            sh = jnp.where(row < n - k, pltpu.roll(a, n - k, axis=0), NEG)
        else:
            sh = jnp.where(row >= k, pltpu.roll(a, k, axis=0), NEG)
        a = jnp.maximum(a, sh)
        k *= 2
    return a


ML_GL = ML_HEADS


def _mlstm_kernel(rev, qk_ref, v_ref, g_ref, gt_ref, gbr_ref, gbt_ref, o_ref, st_ref, m_ref):
    s, g = pl.program_id(0), pl.program_id(1)
    nbb = qk_ref.shape[0]
    b0 = g * nbb

    @pl.when(s == 0)
    def _():
        st_ref[pl.ds(b0, nbb)] = jnp.zeros((nbb,) + st_ref.shape[1:], F32)
        m_ref[pl.ds(b0, nbb)] = jnp.zeros((nbb,) + m_ref.shape[1:], F32)

    valid = _tri(rev)
    tri = valid.astype(F32)
    r0s = _chunk_rows(rev)
    cs = [r // CHUNK for r in r0s]
    last = 0 if rev else CHUNK - 1
    inst = [(bb, j) for bb in range(nbb) for j in range(N_SUB)]
    heads = [slice(h * LANE, (h + 1) * LANE) for h in range(ML_HEADS)]

    r_sel = lax.broadcasted_iota(jnp.int32, (LANE, HP), 0)
    c_sel = lax.broadcasted_iota(jnp.int32, (LANE, HP), 1)
    sel_h = (r_sel == ML_GL + c_sel // LANE).astype(BF16)
    r_t = lax.broadcasted_iota(jnp.int32, (HP, HP), 0)
    c_t = lax.broadcasted_iota(jnp.int32, (HP, HP), 1)
    same = jnp.logical_and(r_t // LANE == c_t // LANE, jnp.logical_and(r_t % LANE < CHUNK, c_t % LANE < CHUNK))
    before = (r_t % LANE >= c_t % LANE) if rev else (r_t % LANE <= c_t % LANE)
    tri_b = jnp.logical_and(same, before).astype(BF16)
    r_v = lax.broadcasted_iota(jnp.int32, (CHUNK, HP), 0)
    c_v = lax.broadcasted_iota(jnp.int32, (CHUNK, HP), 1) % LANE
    valid4 = jnp.logical_and(c_v < CHUNK, (r_v <= c_v) if rev else (r_v >= c_v))
    lane4 = lax.broadcasted_iota(jnp.int32, (CHUNK, HP), 1) % LANE
    lane_c = lax.broadcasted_iota(jnp.int32, (CHUNK, LANE), 1)
    lane1 = lax.broadcasted_iota(jnp.int32, (1, LANE), 1)
    head_lane = jnp.logical_and(lane1 >= ML_GL, lane1 < ML_GL + ML_HEADS)
    gbr, gbt = gbr_ref[...], gbt_ref[...]

    gcs, fcm, cmx, a_row, grt = {}, {}, {}, {}, {}
    for bb, j in inst:
        gc = g_ref[bb, pl.ds(r0s[j], CHUNK), :] + gbr
        gcs[bb, j] = pltpu.roll(gc, ML_GL, axis=1)
        fcm[bb, j] = _dot_exact01(tri, _log_sigmoid(gc), lhs_is_01=True)
        grt[bb, j] = gt_ref[bb, cs[j]] + gbt
    row_id = lax.broadcasted_iota(jnp.int32, (len(inst), HP), 0)
    lfr = jnp.zeros((len(inst), HP), F32)
    for n, i in enumerate(inst):
        lfr = jnp.where(row_id == n, _log_sigmoid(grt[i][1:2]), lfr)
    fcr = _dot_exact01(lfr, tri_b, lhs_is_01=False)
    for n, i in enumerate(inst):
        a_row[i] = grt[i][0:1] - fcr[n:n + 1]
        cmx[i] = _cummax_rows(gcs[i] - fcm[i], rev)
    bx = {}
    e_neg, gd = {}, {}
    for bb in range(nbb):
        m_prev = m_ref[b0 + bb, 0:1, :]
        for j in range(N_SUB):
            i = (bb, j)
            m_t = fcm[i] + jnp.maximum(m_prev, cmx[i])
            m_new = m_t[last:last + 1]
            f_tot = fcm[i][last:last + 1]
            u = fcm[i] - m_t
            w_prev = jnp.exp(u + m_prev)
            w_s = jnp.exp(f_tot - fcm[i] + gcs[i] - m_new)
            gdec = jnp.broadcast_to(jnp.exp(f_tot + m_prev - m_new), (16, LANE))
            e_neg[i] = jnp.exp(-m_t)
            keep = lambda a: jnp.where(head_lane, a, 0.0)
            bx[i] = jnp.concatenate(_split_bf16(keep(u), 2) + _split_bf16(keep(w_prev), 1)
                                    + _split_bf16(keep(w_s), 1) + _split_bf16(keep(gdec), 2), axis=0)
            m_prev = m_new
        m_ref[b0 + bb] = jnp.broadcast_to(m_prev, (8, LANE))
    ub, wpb, wsb, gdb = {}, {}, {}, {}
    for i in inst:
        y = _dot(bx[i], sel_h)
        c = CHUNK
        ub[i] = y[0:c] + y[c:2 * c]
        wpb[i] = y[2 * c:3 * c]
        wsb[i] = y[3 * c:4 * c]
        gdb[i] = y[4 * c:4 * c + 1] + y[4 * c + 16:4 * c + 17]
    q, v, qkw, ds = {}, {}, {}, {}
    for bb, j in inst:
        i = (bb, j)
        rs = pl.ds(r0s[j], CHUNK)
        w = jnp.where(valid4, jnp.exp(ub[i] + a_row[i]), 0.0)
        q[i] = qk_ref[bb, rs, 0:HP].astype(BF16)
        k = qk_ref[bb, rs, HP:2 * HP]
        kb = k.astype(BF16)
        kw = (k * wsb[i]).astype(BF16)
        v[i] = jnp.where(lane4 == ML_DH, 1.0, v_ref[bb, rs, :]).astype(BF16)
        for h, sl in enumerate(heads):
            sc = _dot_nt(q[i][:, sl], kb[:, sl])
            qkw[i, h] = (sc * w[:, h * LANE:h * LANE + CHUNK]).astype(BF16)
            ds[i, h] = _dot_tn(v[i][:, sl], kw[:, sl])
    s_in = {}
    for bb in range(nbb):
        for h, sl in enumerate(heads):
            st = st_ref[b0 + bb, h]
            for j in range(N_SUB):
                i = (bb, j)
                s_in[i, h] = st.astype(BF16)
                st = gdb[i][:, sl] * st + ds[i, h]
            st_ref[b0 + bb, h] = st
    num = {}
    for i in inst:
        parts = [_dot_nt(q[i][:, sl], s_in[i, h]) for h, sl in enumerate(heads)]
        intra = [_dot(qkw[i, h], v[i][:, sl]) for h, sl in enumerate(heads)]
        num[i] = wpb[i] * jnp.concatenate(parts, axis=1) + jnp.concatenate(intra, axis=1)
    for bb, j in inst:
        i = (bb, j)
        den = jnp.zeros((CHUNK, LANE), F32)
        for h, sl in enumerate(heads):
            dh = jnp.sum(jnp.where(lane_c == ML_DH, num[i][:, sl], 0.0), axis=-1, keepdims=True)
            den = jnp.where(lane_c == ML_GL + h, jnp.broadcast_to(dh, (CHUNK, LANE)), den)
        r1, r2 = _split_bf16(jnp.where(head_lane, 1.0 / jnp.maximum(jnp.abs(den), e_neg[i]), 0.0), 2)
        rb = _dot(jnp.concatenate([r1, r2], axis=0), sel_h)
        rb = rb[0:CHUNK] + rb[CHUNK:2 * CHUNK]
        o_ref[bb, pl.ds(r0s[j], CHUNK), :] = jnp.where(lane4 < ML_DH, num[i] * rb, 0.0).astype(o_ref.dtype)


def _mlstm(qk3, pm3, gtl, gbr, gbt, *, rows, rev):
    nb, l, _ = qk3.shape
    nbb = SCAN_BATCHES
    dr = 1 if rev else 0
    pos = functools.partial(_scan_pos_static, rev, ncb=rows.ncb, ntb=rows.ntb)
    return pl.pallas_call(
        functools.partial(_mlstm_kernel, rev),
        out_shape=jax.ShapeDtypeStruct((nb, l, HP), BF16),
        grid=(rows.ntb, nb // nbb),
        in_specs=[pl.BlockSpec((nbb, SEG, 2 * HP), lambda s, g: (g, pos(s), 0)),
                  pl.BlockSpec((nbb, SEG, HP), lambda s, g: (g, pos(s), 0)),
                  pl.BlockSpec((nbb, SEG, LANE), lambda s, g: (g, pos(s), HP // LANE + dr)),
                  pl.BlockSpec((nbb, N_SUB, None, 2, HP), lambda s, g: (g, pos(s), dr, 0, 0)),
                  pl.BlockSpec((1, LANE), lambda s, g: (0, 0)),
                  pl.BlockSpec((2, HP), lambda s, g: (0, 0))],
        out_specs=pl.BlockSpec((nbb, SEG, HP), lambda s, g: (g, pos(s), 0)),
        scratch_shapes=[pltpu.VMEM((nb, ML_HEADS, LANE, LANE), F32), pltpu.VMEM((nb, 8, LANE), F32)],
        compiler_params=_cparams(("arbitrary", "arbitrary")),
        name="mlstm_scan_bwd" if rev else "mlstm_scan_fwd",
    )(qk3, pm3, pm3, gtl, gbr, gbt)


def _s5_kernel(nb, u_ref, bre_ref, bim_ref, are_ref, aim_ref, cre_ref, cim_ref, o_ref, xr_ref, xi_ref, st_ref):
    d, s = pl.program_id(0), pl.program_id(1)

    @pl.when(s == 0)
    def _():
        st_ref[...] = jnp.zeros(st_ref.shape, F32)

    n_sub = u_ref.shape[0] // (S5_SUB * nb)
    sub_rows = S5_SUB * nb

    def run(rev):
        ar, ai = are_ref[0], aim_ref[0]
        order = list(range(n_sub))[::-1] if rev else list(range(n_sub))

        def project_in(q):
            rs = slice(q * sub_rows, (q + 1) * sub_rows)
            u = u_ref[rs, :].astype(BF16)
            xr_ref[rs, :] = _dot(u, bre_ref[0])
            xi_ref[rs, :] = _dot(u, bim_ref[0])

        sr, si = st_ref[0], st_ref[1]
        project_in(order[0])
        for n, q in enumerate(order):
            if n + 1 < n_sub:
                project_in(order[n + 1])
            for j in range(S5_SUB):
                r0 = (q * S5_SUB + (S5_SUB - 1 - j if rev else j)) * nb
                nr = ar * sr - ai * si + xr_ref[r0:r0 + nb, :]
                ni = ar * si + ai * sr + xi_ref[r0:r0 + nb, :]
                xr_ref[r0:r0 + nb, :] = nr
                xi_ref[r0:r0 + nb, :] = ni
                sr, si = nr, ni
            rs = slice(q * sub_rows, (q + 1) * sub_rows)
            o_ref[0, rs, :] = (_dot(xr_ref[rs, :].astype(BF16), cre_ref[...])
                               - _dot(xi_ref[rs, :].astype(BF16), cim_ref[...])).astype(o_ref.dtype)
        st_ref[0] = sr
        st_ref[1] = si

    @pl.when(d == 0)
    def _():
        run(False)

    @pl.when(d == 1)
    def _():
        run(True)


def _s5(ut, bre, bim, are, aim, cre, cim, *, nb, nc, nt):
    n_rows, ch = ut.shape
    tr = S5_STEPS * nb
    ns = bre.shape[-1]
    pos = functools.partial(_scan_pos, ncb=nc, ntb=nt)
    return pl.pallas_call(
        functools.partial(_s5_kernel, nb),
        out_shape=jax.ShapeDtypeStruct((2, n_rows, ch), BF16),
        grid=(2, nt),
        in_specs=[pl.BlockSpec((tr, ch), lambda d, s: (pos(d, s), 0)),
                  pl.BlockSpec((1, ch, ns), lambda d, s: (d, 0, 0)),
                  pl.BlockSpec((1, ch, ns), lambda d, s: (d, 0, 0)),
                  pl.BlockSpec((1, nb, ns), lambda d, s: (d, 0, 0)),
                  pl.BlockSpec((1, nb, ns), lambda d, s: (d, 0, 0)),
                  pl.BlockSpec((ns, ch), lambda d, s: (0, 0)),
                  pl.BlockSpec((ns, ch), lambda d, s: (0, 0))],
        out_specs=pl.BlockSpec((1, tr, ch), lambda d, s: (d, pos(d, s), 0)),
        scratch_shapes=[pltpu.VMEM((tr, ns), F32), pltpu.VMEM((tr, ns), F32), pltpu.VMEM((2, nb, ns), F32)],
        compiler_params=_cparams(("arbitrary", "arbitrary")),
        name="s5_scan",
    )(ut, bre, bim, are, aim, cre, cim)


def _head_norm(o, gain, dim):
    parts = []
    for h in range(o.shape[1] // LANE):
        seg = o[:, h * LANE:(h + 1) * LANE]
        ms = jnp.sum(seg * seg, axis=-1, keepdims=True) * (1.0 / dim)
        parts.append(seg * lax.rsqrt(ms + EPS))
    return jnp.concatenate(parts, axis=1) * gain


def _mix_kernel(with_router, ogf_ref, ogb_ref, gg_ref, ys_ref, u_ref, omf_ref, omb_ref, mo_ref, h_ref, mod_ref,
                gn_ref, mn_ref, sd_ref, gw_ref, gb_ref, wo_ref, n2_ref, *rest):
    if with_router:
        wr_ref, ho_ref, f_ref, rt_ref = rest
    else:
        ho_ref, f_ref = rest
    gla = _head_norm(ogf_ref[...].astype(F32) + ogb_ref[...].astype(F32), gn_ref[...], GLA_DV) * _silu(gg_ref[...].astype(F32))
    z = _gelu_tanh(ys_ref[0].astype(F32) + ys_ref[1].astype(F32) + sd_ref[...] * u_ref[...])
    s5 = z * jax.nn.sigmoid(_dot(z.astype(BF16), gw_ref[...]) + gb_ref[...])
    ml = _head_norm(omf_ref[...].astype(F32) + omb_ref[...].astype(F32), mn_ref[...], ML_DH) * jax.nn.sigmoid(mo_ref[...].astype(F32))
    mix = (_dot(gla.astype(BF16), wo_ref[0:HP]) + _dot(s5.astype(BF16), wo_ref[HP:HP + 2 * LANE])
           + _dot(ml.astype(BF16), wo_ref[HP + 2 * LANE:]))
    m = mod_ref[0]
    hn = h_ref[...] + m[2:3] * mix
    ho_ref[...] = hn
    f = _rmsnorm(hn, n2_ref[...]) * (1.0 + m[4:5]) + m[3:4]
    if not with_router:
        f_ref[...] = f.astype(f_ref.dtype)
    if with_router:
        f_hi, f_lo = _split_bf16(f, 2)
        logits = (_dot(f_hi, wr_ref[0]) + _dot(f_lo, wr_ref[0]) + _dot(f_hi, wr_ref[1])).T[0:2 * N_EXPERTS]
        row = lax.broadcasted_iota(jnp.int32, logits.shape, 0)
        l0 = jnp.where(row < N_EXPERTS, logits, NEG)
        m1 = jnp.max(l0, axis=0, keepdims=True)
        i1 = jnp.min(jnp.where(l0 == m1, row, 2 * N_EXPERTS), axis=0, keepdims=True)
        l1 = jnp.where(row == i1, NEG, l0)
        m2 = jnp.max(l1, axis=0, keepdims=True)
        i2 = jnp.min(jnp.where(l1 == m2, row, 2 * N_EXPERTS), axis=0, keepdims=True)
        e = jnp.exp(m2 - m1)
        w1 = 1.0 / (1.0 + e)
        w2 = e / (1.0 + e)
        rt = jnp.where(row == 0, i1.astype(F32),
                       jnp.where(row == 1, i2.astype(F32),
                                 jnp.where(row == 2, w1, jnp.where(row == 3, w2, 0.0))))
        rt_ref[...] = rt
        d = f.shape[1]
        f_ref[:, 0:d] = f
        f_ref[:, d:] = jnp.concatenate([rt, jnp.zeros((LANE - rt.shape[0], rt.shape[1]), F32)], axis=0).T


def _mix(ogf, ogb, pg, ys, pu, omf, omb, pm, h, modtab, gn, mn, sd, gw, gb, wo, n2, wr, *, layer, rows, lat_only):
    d = h.shape[1]
    tm = SEG
    n = rows.n_blocks(lat_only)
    src, sel = rows.src(lat_only), rows.sel(lat_only)
    full = lambda a: pl.BlockSpec(a.shape, lambda i: (0,) * a.ndim)
    with_router = wr is not None
    in_specs = [pl.BlockSpec((tm, HP), lambda i: (src(i), 0)),
                pl.BlockSpec((tm, HP), lambda i: (src(i), 0)),
                pl.BlockSpec((tm, HP), lambda i: (src(i), 0)),
                pl.BlockSpec((2, tm, 2 * LANE), lambda i: (0, src(i), 0)),
                pl.BlockSpec((tm, 2 * LANE), lambda i: (src(i), 0)),
                pl.BlockSpec((tm, HP), lambda i: (src(i), 0)),
                pl.BlockSpec((tm, HP), lambda i: (src(i), 0)),
                pl.BlockSpec((tm, HP), lambda i: (src(i), 1)),
                pl.BlockSpec((tm, d), lambda i: (src(i), 0)),
                pl.BlockSpec((None, 1, 8, d), lambda i: (layer, sel(i), 0, 0)),
                full(gn), full(mn), full(sd), full(gw), full(gb), full(wo), full(n2)]
    args = [ogf, ogb, pg, ys, pu, omf, omb, pm, h, modtab, gn, mn, sd, gw, gb, wo, n2]
    fw, fdt = (d + LANE, F32) if with_router else (d, BF16)
    out_shape = [jax.ShapeDtypeStruct((n * tm, d), F32), jax.ShapeDtypeStruct((n * tm, fw), fdt)]
    out_specs = [pl.BlockSpec((tm, d), lambda i: (i, 0)), pl.BlockSpec((tm, fw), lambda i: (i, 0))]
    if with_router:
        in_specs.append(full(wr))
        args.append(wr)
        out_shape.append(jax.ShapeDtypeStruct((2 * N_EXPERTS, n * tm), F32))
        out_specs.append(pl.BlockSpec((2 * N_EXPERTS, tm), lambda i: (0, i)))
    return pl.pallas_call(
        functools.partial(_mix_kernel, with_router),
        out_shape=tuple(out_shape),
        grid=(n,),
        in_specs=in_specs,
        out_specs=tuple(out_specs),
        compiler_params=_cparams(("arbitrary",)),
        name="mix_out",
    )(*args)


FF_TILE = 256


def _swiglu(xb, w1_ref, w3_ref, w2_ref, a_ref, lead=()):
    dff = w1_ref.shape[-1]
    for j in range(dff // FF_TILE):
        sl = slice(j * FF_TILE, (j + 1) * FF_TILE)
        h1 = _dot(xb, w1_ref[lead + (slice(None), sl)])
        h3 = _dot(xb, w3_ref[lead + (slice(None), sl)])
        a_ref[:, sl] = (_silu(h1) * h3).astype(BF16)
    return _dot(a_ref[...], w2_ref[lead + (slice(None), slice(None))])


FFN_TM = 1024
MIXFFN_TM = 2 * SEG


def _mix_ffn_kernel(sel, ogf_ref, ogb_ref, gg_ref, ys_ref, u_ref, omf_ref, omb_ref, mo_ref, h_ref, mod_ref,
                    gn_ref, mn_ref, sd_ref, gw_ref, gb_ref, wo_ref, n2_ref, w1_ref, w3_ref, w2_ref, o_ref,
                    a_ref, hn_ref, f_ref):
    n_seg = h_ref.shape[0] // SEG
    mods = []
    for q in range(n_seg):
        rs = slice(q * SEG, (q + 1) * SEG)
        m = mod_ref[sel(pl.program_id(0) * n_seg + q)]
        mods.append(m)
        gla = (_head_norm(ogf_ref[rs, :].astype(F32) + ogb_ref[rs, :].astype(F32), gn_ref[...], GLA_DV)
               * _silu(gg_ref[rs, :].astype(F32)))
        z = _gelu_tanh(ys_ref[0, rs, :].astype(F32) + ys_ref[1, rs, :].astype(F32) + sd_ref[...] * u_ref[rs, :])
        s5 = z * jax.nn.sigmoid(_dot(z.astype(BF16), gw_ref[...]) + gb_ref[...])
        ml = (_head_norm(omf_ref[rs, :].astype(F32) + omb_ref[rs, :].astype(F32), mn_ref[...], ML_DH)
              * jax.nn.sigmoid(mo_ref[rs, :].astype(F32)))
        mix = (_dot(gla.astype(BF16), wo_ref[0:HP]) + _dot(s5.astype(BF16), wo_ref[HP:HP + 2 * LANE])
               + _dot(ml.astype(BF16), wo_ref[HP + 2 * LANE:]))
        hn = h_ref[rs, :] + m[2:3] * mix
        hn_ref[rs, :] = hn
        f_ref[rs, :] = (_rmsnorm(hn, n2_ref[...]) * (1.0 + m[4:5]) + m[3:4]).astype(BF16)
    y = _swiglu(f_ref[...], w1_ref, w3_ref, w2_ref, a_ref)
    for q in range(n_seg):
        rs = slice(q * SEG, (q + 1) * SEG)
        o_ref[rs, :] = hn_ref[rs, :] + mods[q][5:6] * y[rs]


def _mix_ffn(ogf, ogb, pg, ys, pu, omf, omb, pm, h, modtab, gn, mn, sd, gw, gb, wo, n2, w1, w3, w2, *, layer, rows):
    r, d = h.shape
    tm = MIXFFN_TM
    assert r % tm == 0
    dff = w1.shape[-1]
    full = lambda a: pl.BlockSpec(a.shape, lambda i: (0,) * a.ndim)
    resident = lambda a: pl.BlockSpec(a.shape, lambda i: (0,) * a.ndim, pipeline_mode=pl.Buffered(1))
    in_specs = [pl.BlockSpec((tm, HP), lambda i: (i, 0)),
                pl.BlockSpec((tm, HP), lambda i: (i, 0)),
                pl.BlockSpec((tm, HP), lambda i: (i, 0)),
                pl.BlockSpec((2, tm, 2 * LANE), lambda i: (0, i, 0)),
                pl.BlockSpec((tm, 2 * LANE), lambda i: (i, 0)),
                pl.BlockSpec((tm, HP), lambda i: (i, 0)),
                pl.BlockSpec((tm, HP), lambda i: (i, 0)),
                pl.BlockSpec((tm, HP), lambda i: (i, 1)),
                pl.BlockSpec((tm, d), lambda i: (i, 0)),
                pl.BlockSpec((None,) + modtab.shape[1:], lambda i: (layer, 0, 0, 0)),
                full(gn), full(mn), full(sd), full(gw), full(gb), resident(wo), full(n2),
                resident(w1), resident(w3), resident(w2)]
    return pl.pallas_call(
        functools.partial(_mix_ffn_kernel, rows.sel(False)),
        out_shape=jax.ShapeDtypeStruct((r, d), F32),
        grid=(r // tm,),
        in_specs=in_specs,
        out_specs=pl.BlockSpec((tm, d), lambda i: (i, 0)),
        scratch_shapes=[pltpu.VMEM((tm, dff), BF16), pltpu.VMEM((tm, d), F32), pltpu.VMEM((tm, d), BF16)],
        compiler_params=_cparams(("arbitrary",)),
        name="mix_ffn",
    )(ogf, ogb, pg, ys, pu, omf, omb, pm, h, modtab, gn, mn, sd, gw, gb, wo, n2, w1, w3, w2)


def _ffn_kernel(final, sel, f_ref, h_ref, mod_ref, w1_ref, w3_ref, w2_ref, *rest):
    if final:
        nf_ref, o_ref, a_ref = rest
    else:
        o_ref, a_ref = rest
    y = _swiglu(f_ref[...], w1_ref, w3_ref, w2_ref, a_ref)
    n_seg = f_ref.shape[0] // SEG
    for q in range(n_seg):
        rs = slice(q * SEG, (q + 1) * SEG)
        gate = mod_ref[sel(pl.program_id(0) * n_seg + q)][5:6]
        hn = h_ref[rs, :] + gate * y[rs]
        o_ref[rs, :] = _rmsnorm(hn, nf_ref[...]) if final else hn


def _ffn(f, h, modtab, w1, w3, w2, nf, *, layer, rows, lat_only):
    r, d = h.shape
    tm = FFN_TM
    assert r % tm == 0
    dff = w1.shape[-1]
    full = lambda a: pl.BlockSpec(a.shape, lambda i: (0,) * a.ndim)
    resident = lambda a: pl.BlockSpec(a.shape, lambda i: (0,) * a.ndim, pipeline_mode=pl.Buffered(1))
    in_specs = [pl.BlockSpec((tm, d), lambda i: (i, 0)),
                pl.BlockSpec((tm, d), lambda i: (i, 0)),
                pl.BlockSpec((None,) + modtab.shape[1:], lambda i: (layer, 0, 0, 0)),
                resident(w1), resident(w3), resident(w2)]
    args = [f, h, modtab, w1, w3, w2]
    if lat_only:
        in_specs.append(full(nf))
        args.append(nf)
    return pl.pallas_call(
        functools.partial(_ffn_kernel, lat_only, rows.sel(lat_only)),
        out_shape=jax.ShapeDtypeStruct((r, d), F32),
        grid=(r // tm,),
        in_specs=in_specs,
        out_specs=pl.BlockSpec((tm, d), lambda i: (i, 0)),
        scratch_shapes=[pltpu.VMEM((tm, dff), BF16)],
        compiler_params=_cparams(("arbitrary",)),
        name="ffn",
    )(*args)


MOE_TM = 512


def _moe_kernel(be_ref, live_ref, x_ref, w1_ref, w3_ref, w2_ref, o_ref, a_ref):
    i = pl.program_id(0)
    d = o_ref.shape[1]

    @pl.when(live_ref[i] > 0)
    def _():
        y = _swiglu(x_ref[:, 0:d].astype(BF16), w1_ref, w3_ref, w2_ref, a_ref, lead=(0,))
        tail = x_ref[:, d:]
        mine = tail[:, 0:1] == be_ref[i].astype(F32)
        o_ref[...] = y * jnp.where(mine, tail[:, 2:3], tail[:, 3:4])

    @pl.when(live_ref[i] == 0)
    def _():
        o_ref[...] = jnp.zeros(o_ref.shape, F32)


def _moe_experts(block_expert, block_live, xg, w1, w3, w2):
    n_rows, dx = xg.shape
    d = w1.shape[-2]
    dff = w1.shape[-1]
    tm = MOE_TM
    return pl.pallas_call(
        _moe_kernel,
        out_shape=jax.ShapeDtypeStruct((n_rows, d), F32),
        grid_spec=pltpu.PrefetchScalarGridSpec(
            num_scalar_prefetch=2,
            grid=(n_rows // tm,),
            in_specs=[pl.BlockSpec((tm, dx), lambda i, be, lv: (i, 0)),
                      pl.BlockSpec((1, d, dff), lambda i, be, lv: (be[i], 0, 0)),
                      pl.BlockSpec((1, d, dff), lambda i, be, lv: (be[i], 0, 0)),
                      pl.BlockSpec((1, dff, d), lambda i, be, lv: (be[i], 0, 0))],
            out_specs=pl.BlockSpec((tm, d), lambda i, be, lv: (i, 0)),
            scratch_shapes=[pltpu.VMEM((tm, dff), BF16)]),
        compiler_params=_cparams(("arbitrary",)),
        name="moe_experts",
    )(block_expert, block_live, xg, w1, w3, w2)


SC_GATHER_ROWS = 64


def _gather_rows(table, idx):
    n_idx = idx.shape[0]
    _, d = table.shape
    info = plsc.get_sparse_core_info()
    n_cores, n_workers = info.num_cores, info.num_cores * info.num_subcores
    assert n_idx % (n_workers * SC_GATHER_ROWS) == 0
    per_worker = n_idx // n_workers
    mesh = plsc.VectorSubcoreMesh(core_axis_name="c", subcore_axis_name="s")

    @functools.partial(
        pl.kernel, mesh=mesh,
        out_type=jax.ShapeDtypeStruct((n_idx, d), table.dtype),
        scratch_types=[pltpu.VMEM((SC_GATHER_ROWS,), jnp.int32),
                       pltpu.VMEM((SC_GATHER_ROWS, d), table.dtype),
                       pltpu.SemaphoreType.DMA])
    def gather(table_hbm, idx_hbm, out_hbm, idx_v, rows_v, sem):
        base = (lax.axis_index("s") * n_cores + lax.axis_index("c")) * per_worker

        @pl.loop(0, per_worker // SC_GATHER_ROWS)
        def _(it):
            off = pl.multiple_of(base + it * SC_GATHER_ROWS, SC_GATHER_ROWS)
            pltpu.sync_copy(idx_hbm.at[pl.ds(off, SC_GATHER_ROWS)], idx_v)
            pltpu.async_copy(table_hbm.at[idx_v], rows_v, sem).wait()
            pltpu.sync_copy(rows_v, out_hbm.at[pl.ds(off, SC_GATHER_ROWS)])

    return gather(table, idx)


def _scatter_rows(src, dest, n_rows):
    n_tok, d = src.shape
    info = plsc.get_sparse_core_info()
    n_cores, n_workers = info.num_cores, info.num_cores * info.num_subcores
    assert n_tok % (n_workers * SC_GATHER_ROWS) == 0
    per_worker = n_tok // n_workers
    mesh = plsc.VectorSubcoreMesh(core_axis_name="c", subcore_axis_name="s")

    @functools.partial(
        pl.kernel, mesh=mesh,
        out_type=jax.ShapeDtypeStruct((n_rows, d), src.dtype),
        scratch_types=[pltpu.VMEM((SC_GATHER_ROWS,), jnp.int32),
                       pltpu.VMEM((SC_GATHER_ROWS, d), src.dtype),
                       pltpu.SemaphoreType.DMA])
    def scatter(src_hbm, dest_hbm, out_hbm, idx_v, rows_v, sem):
        base = (lax.axis_index("s") * n_cores + lax.axis_index("c")) * per_worker

        @pl.loop(0, per_worker // SC_GATHER_ROWS)
        def _(it):
            off = pl.multiple_of(base + it * SC_GATHER_ROWS, SC_GATHER_ROWS)
            pltpu.sync_copy(src_hbm.at[pl.ds(off, SC_GATHER_ROWS)], rows_v)
            for k in range(TOP_K):
                pltpu.sync_copy(dest_hbm.at[pl.ds(k * n_tok + off, SC_GATHER_ROWS)], idx_v)
                pltpu.async_copy(rows_v, out_hbm.at[idx_v], sem).wait()

    return scatter(src, dest.reshape(-1))


def _resid_kernel(final, h_ref, y0_ref, y1_ref, mod_ref, *rest):
    hn = h_ref[...] + mod_ref[0][5:6] * (y0_ref[...] + y1_ref[...])
    if final:
        nf_ref, o_ref = rest
        o_ref[...] = _rmsnorm(hn, nf_ref[...])
    else:
        (o_ref,) = rest
        o_ref[...] = hn


def _moe_resid(h, y0, y1, modtab, nf, *, layer, rows, lat_only):
    r, d = h.shape
    tm = SEG
    sel = rows.sel(lat_only)
    row = lambda i: (i, 0)
    in_specs = [pl.BlockSpec((tm, d), row), pl.BlockSpec((tm, d), row), pl.BlockSpec((tm, d), row),
                pl.BlockSpec((None, 1, 8, d), lambda i: (layer, sel(i), 0, 0))]
    args = [h, y0, y1, modtab]
    if lat_only:
        in_specs.append(pl.BlockSpec((1, d), lambda i: (0, 0)))
        args.append(nf)
    return pl.pallas_call(
        functools.partial(_resid_kernel, lat_only),
        out_shape=jax.ShapeDtypeStruct((r, d), F32),
        grid=(r // tm,),
        in_specs=in_specs,
        out_specs=pl.BlockSpec((tm, d), row),
        compiler_params=_cparams(("arbitrary",)),
        name="moe_resid",
    )(*args)


CAST_PARTS = 8


def _cast_kernel(*refs):
    o_ref = refs[-1]
    tr = refs[0].shape[1]
    for k, x_ref in enumerate(refs[:-1]):
        o_ref[0, k * tr:(k + 1) * tr, :] = x_ref[0].astype(o_ref.dtype)


def _to_bf16(w, j):
    lead, (r, c) = w.shape[1:-2], w.shape[-2:]
    n = int(np.prod(lead, dtype=np.int64))
    w3 = w.reshape((-1, r, c))
    tr = r // CAST_PARTS
    assert tr * CAST_PARTS == r and tr % 16 == 0
    band = lambda k: pl.BlockSpec((1, tr, c), lambda e: (j * n + e, k, 0))
    out = pl.pallas_call(
        _cast_kernel,
        out_shape=jax.ShapeDtypeStruct((n, r, c), BF16),
        grid=(n,),
        in_specs=[band(k) for k in range(CAST_PARTS)],
        out_specs=pl.BlockSpec((1, r, c), lambda e: (e, 0, 0)),
        compiler_params=_cparams(("arbitrary",)),
        name="to_bf16",
    )(*([w3] * CAST_PARTS))
    return out.reshape(lead + (r, c))


def _pad_heads(w, heads, dim, to=LANE):
    lead = w.shape[:-1]
    w = w.reshape(lead + (heads, dim))
    w = jnp.pad(w, [(0, 0)] * len(lead) + [(0, 0), (0, to - dim)])
    return w.reshape(lead + (heads * to,))


def _pad_last(w, to):
    return jnp.pad(w, [(0, 0)] * (w.ndim - 1) + [(0, to - w.shape[-1])])


def _pos_embed(n_tokens, d):
    n_grid_rows = n_tokens // GRID_W
    row, col = jnp.meshgrid(jnp.arange(n_grid_rows, dtype=F32), jnp.arange(GRID_W, dtype=F32), indexing='ij')
    n_freq = d // 4
    omega = jnp.exp(-math.log(POS_BASE) * jnp.arange(n_freq, dtype=F32) / n_freq)

    def axis_embed(p):
        ang = p.reshape(-1, 1) * omega
        return jnp.concatenate([jnp.sin(ang), jnp.cos(ang)], axis=-1)

    return jnp.concatenate([axis_embed(row), axis_embed(col)], axis=-1)


def _s5_discretise(lam_re, lam_im, log_dt, b_re, b_im):
    dt = jnp.exp(log_dt)[:, None]
    mag = jnp.exp(lam_re * dt)
    abar_re, abar_im = mag * jnp.cos(lam_im * dt), mag * jnp.sin(lam_im * dt)
    den = lam_re * lam_re + lam_im * lam_im
    pr, pi = abar_re - 1.0, abar_im
    coef_re = (pr * lam_re + pi * lam_im) / den
    coef_im = (pi * lam_re - pr * lam_im) / den
    bbar_re = coef_re[..., None] * b_re - coef_im[..., None] * b_im
    bbar_im = coef_re[..., None] * b_im + coef_im[..., None] * b_re
    return abar_re, abar_im, bbar_re, bbar_im


def _block_diag(m):
    g, a, b = m.shape
    eye = jnp.eye(g, dtype=m.dtype)
    return (eye[:, None, :, None] * m[:, :, None, :]).reshape(g * a, g * b)


def _route_plan(route, tm):
    n_tok = route.shape[1]
    n_assign = n_tok * TOP_K
    flat_e = route[0:TOP_K].astype(jnp.int32).reshape(-1)
    onehot = (jnp.arange(N_EXPERTS, dtype=jnp.int32)[:, None] == flat_e[None, :]).astype(jnp.int32)
    csum = jnp.cumsum(onehot, axis=1)
    counts = csum[:, -1]
    padded = (counts + tm - 1) // tm * tm
    pend = jnp.cumsum(padded)
    pstart = pend - padded
    dest = jnp.sum(onehot * (csum - 1 + pstart[:, None]), axis=0)
    n_blocks = -(-n_assign // tm) + N_EXPERTS
    block_start = jnp.arange(n_blocks, dtype=jnp.int32) * tm
    block_expert = jnp.minimum(jnp.searchsorted(pend, block_start, side='right'), N_EXPERTS - 1).astype(jnp.int32)
    block_live = (block_start < (pstart + counts)[block_expert]).astype(jnp.int32)
    return n_blocks * tm, block_expert, block_live, dest.reshape(TOP_K, n_tok)


def kernel(x, c, ctx, c_ctx, w_ada, b_ada, norm1, norm2, w_in, w_out, gla_wa2, gla_ba, gla_norm, s5_lam_re, s5_lam_im, s5_log_dt, s5_b_re, s5_b_im, s5_c_re, s5_c_im, s5_d, s5_glu_w, s5_glu_b, ml_conv_w, ml_conv_b, ml_gate_b, ml_norm, ffn_w1, ffn_w3, ffn_w2, moe_router, moe_w1, moe_w3, moe_w2, norm_f):
    nb, n_lat, d = x.shape
    lc = ctx.shape[1]
    depth = w_ada.shape[0]
    l = lc + n_lat
    assert lc % SEG == 0 and n_lat % SEG == 0 and nb == 8 and nb % SCAN_BATCHES == 0
    rows = _Rows(nb, lc // SEG, l // SEG)

    h = _embed(ctx.reshape(nb * lc, d), x.reshape(nb * n_lat, d), _pos_embed(n_lat, d), rows)

    cond = jnp.zeros((16, d), F32).at[:nb].set(c).at[nb].set(c_ctx)
    mod = _modulation(cond, w_ada, b_ada)
    modtab = jnp.pad(mod.reshape(depth, 16, 6, d), ((0, 0), (0, 0), (0, 2), (0, 0)))

    dk, dv, dh = GLA_HEADS * GLA_DK, GLA_HEADS * GLA_DV, ML_HEADS * ML_DH
    s5c = s5_d.shape[-1]
    cuts = np.cumsum([dk, dk, dv, GLA_RANK, dv, s5c, dh, dh, dh, dh, 4 * ML_HEADS])

    for i in range(depth):
        last = i == depth - 1
        gq, gk, gv, glr, gg, su, mq, mk, mv, mo, mg = jnp.split(w_in[i], cuts[:-1], axis=-1)
        w_all = jnp.concatenate([
            _pad_heads(gq, GLA_HEADS, GLA_DK, GLA_KP), _pad_heads(gk, GLA_HEADS, GLA_DK, GLA_KP),
            _pad_heads(gv, GLA_HEADS, GLA_DV), _pad_last(glr, LANE),
            su,
            _pad_heads(mv, ML_HEADS, ML_DH),
            _pad_last(mg[:, :2 * ML_HEADS], LANE), _pad_last(mg[:, 2 * ML_HEADS:], LANE),
            _pad_heads(mq, ML_HEADS, ML_DH), _pad_heads(mk, ML_HEADS, ML_DH),
            _pad_heads(gg, GLA_HEADS, GLA_DV), _pad_heads(mo, ML_HEADS, ML_DH)],
            axis=-1).astype(BF16)
        wgt = mg.T.astype(BF16)
        cw = jnp.concatenate([_pad_heads(ml_conv_w[i][:, :dh], ML_HEADS, ML_DH),
                              _pad_heads(ml_conv_w[i][:, dh:], ML_HEADS, ML_DH)], axis=-1)
        cw = jnp.pad(cw, ((0, 8 - ML_CONV), (0, 0)))
        cb = jnp.concatenate([_pad_heads(ml_conv_b[i][:dh], ML_HEADS, ML_DH),
                              _pad_heads(ml_conv_b[i][dh:], ML_HEADS, ML_DH)])[None]
        post = jnp.concatenate([jnp.ones((HP,), F32), jnp.full((HP,), ML_DH ** -0.5, F32)])[None]
        pg, pu, pm, qk, go, gt = _proj(h, modtab, norm1[i][None], w_all, wgt, cw, cb, post, layer=i, rows=rows)

        wa = jnp.pad(_pad_heads(gla_wa2[i], GLA_HEADS, GLA_DK, GLA_KP), ((0, 0), (0, LANE - GLA_RANK), (0, 0)))
        ba = _pad_heads(gla_ba[i], GLA_HEADS, GLA_DK, GLA_KP)[:, None, :]
        pg3 = pg.reshape(nb, l, NG)
        ogf = _gla(pg3, wa[0], ba[0], rows=rows, rev=False).reshape(nb * l, HP)
        ogb = _gla(pg3, wa[1], ba[1], rows=rows, rev=True).reshape(nb * l, HP)

        bres, bims, ares, aims = [], [], [], []
        for dr in (0, 1):
            a_re, a_im, b_re, b_im = _s5_discretise(s5_lam_re[i, dr], s5_lam_im[i, dr], s5_log_dt[i, dr],
                                                    s5_b_re[i], s5_b_im[i])
            bres.append(_block_diag(jnp.swapaxes(b_re, 1, 2)))
            bims.append(_block_diag(jnp.swapaxes(b_im, 1, 2)))
            ares.append(jnp.broadcast_to(a_re.reshape(1, -1), (nb, a_re.size)))
            aims.append(jnp.broadcast_to(a_im.reshape(1, -1), (nb, a_im.size)))
        cre = _block_diag(jnp.swapaxes(s5_c_re[i], 1, 2)).astype(BF16)
        cim = _block_diag(jnp.swapaxes(s5_c_im[i], 1, 2)).astype(BF16)
        ut = pu.reshape(nb, l, s5c).swapaxes(0, 1).reshape(l * nb, s5c)
        yt = _s5(ut, jnp.stack(bres).astype(BF16), jnp.stack(bims).astype(BF16), jnp.stack(ares), jnp.stack(aims),
                 cre, cim, nb=nb, nc=lc // S5_STEPS, nt=l // S5_STEPS)
        ys = yt.reshape(2, l, nb, s5c).swapaxes(1, 2).reshape(2, nb * l, s5c)

        gb = ml_gate_b[i].reshape(2, 2, ML_HEADS)
        gbr = _pad_last(gb.reshape(2, 1, 2 * ML_HEADS), LANE)
        gbt = _pad_last(jnp.broadcast_to(gb[..., None], (2, 2, ML_HEADS, CHUNK)), LANE).reshape(2, 2, HP)
        gtl = _pad_last(gt.reshape(2, 2, ML_HEADS, nb, l // CHUNK, CHUNK).transpose(3, 4, 0, 1, 2, 5),
                        LANE).reshape(nb, l // CHUNK, 2, 2, HP)
        qk3, pm3 = qk.reshape(nb, l, 2 * HP), pm.reshape(nb, l, NM)
        omf = _mlstm(qk3, pm3, gtl, gbr[0], gbt[0], rows=rows, rev=False).reshape(nb * l, HP)
        omb = _mlstm(qk3, pm3, gtl, gbr[1], gbt[1], rows=rows, rev=True).reshape(nb * l, HP)

        wo = w_out[i]
        wo_p = jnp.concatenate([
            jnp.pad(wo[:dv].reshape(GLA_HEADS, GLA_DV, d), ((0, 0), (0, LANE - GLA_DV), (0, 0))).reshape(HP, d),
            wo[dv:dv + s5c],
            jnp.pad(wo[dv + s5c:].reshape(ML_HEADS, ML_DH, d), ((0, 0), (0, LANE - ML_DH), (0, 0))).reshape(HP, d)],
            axis=0).astype(BF16)
        gn = jnp.tile(_pad_last(gla_norm[i], LANE), GLA_HEADS)[None]
        mn = jnp.tile(_pad_last(ml_norm[i], LANE), ML_HEADS)[None]
        is_moe = i % 2 == 1
        j = i // 2
        wr = jnp.stack(_split_bf16(_pad_last(moe_router[j], LANE), 2)) if is_moe else None
        mix_args = (ogf, ogb, go, ys, pu, omf, omb, go, h, modtab, gn, mn, s5_d[i][None], s5_glu_w[i].astype(BF16),
                    s5_glu_b[i][None], wo_p, norm2[i][None])
        if not is_moe and not last:
            h = _mix_ffn(*mix_args, _to_bf16(ffn_w1, j), _to_bf16(ffn_w3, j), _to_bf16(ffn_w2, j),
                         layer=i, rows=rows)
            continue
        outs = _mix(*mix_args, wr, layer=i, rows=rows, lat_only=last)
        if not is_moe:
            h, f = outs
            h = _ffn(f, h, modtab, _to_bf16(ffn_w1, j), _to_bf16(ffn_w3, j), _to_bf16(ffn_w2, j),
                     norm_f[None], layer=i, rows=rows, lat_only=last)
        else:
            h, f, route = outs
            n_rows, block_expert, block_live, dest = _route_plan(route, MOE_TM)
            xg = _scatter_rows(f, dest, n_rows)
            yg = _moe_experts(block_expert, block_live, xg, _to_bf16(moe_w1, j), _to_bf16(moe_w3, j),
                              _to_bf16(moe_w2, j))
            y0 = _gather_rows(yg, dest[0])
            y1 = _gather_rows(yg, dest[1])
            h = _moe_resid(h, y0, y1, modtab, norm_f[None], layer=i, rows=rows, lat_only=last)
    return h.reshape(nb, n_lat, d)
```

```python
import functools
import math

import numpy as np
import jax
import jax.numpy as jnp
from jax import lax
from jax.experimental import pallas as pl
from jax.experimental.pallas import tpu as pltpu
from jax.experimental.pallas import tpu_sc as plsc

F32 = jnp.float32
BF16 = jnp.bfloat16
HIGHEST = lax.Precision.HIGHEST

GRID_W = 64
POS_BASE = 10000.0
EPS = 1e-6
GLA_HEADS, GLA_DK, GLA_DV, GLA_RANK, GLA_GATE_NORM = 4, 48, 96, 16, 16.0
S5_GROUP, S5_STATE = 16, 64
ML_HEADS, ML_DH, ML_CONV = 4, 96, 3
N_EXPERTS, TOP_K = 8, 2

LANE = 128
CHUNK = 64
SEG = 256
N_SUB = SEG // CHUNK
SCAN_BATCHES = 8
S5_STEPS = 128
S5_SUB = 32
NEG = -1e30
VMEM_LIMIT = 56 * 1024 * 1024

HP = LANE * GLA_HEADS
GLA_KP = 64
GQ = GLA_HEADS * GLA_KP
NG = 2 * GQ + HP + LANE
NM = HP + 2 * LANE
NGO = 2 * HP


def _cparams(sem):
    return pltpu.CompilerParams(dimension_semantics=sem, vmem_limit_bytes=VMEM_LIMIT)


def _dot(a, b, **kw):
    return jnp.dot(a, b, preferred_element_type=F32, **kw)


def _dot_nt(a, b, **kw):
    return lax.dot_general(a, b, (((1,), (1,)), ((), ())), preferred_element_type=F32, **kw)


def _dot_tn(a, b, **kw):
    return lax.dot_general(a, b, (((0,), (0,)), ((), ())), preferred_element_type=F32, **kw)


def _log_sigmoid(x):
    return jnp.minimum(x, 0.0) - jnp.log1p(jnp.exp(-jnp.abs(x)))


def _silu(x):
    return x * jax.nn.sigmoid(x)


def _gelu_tanh(x):
    return 0.5 * x * (1.0 + jnp.tanh(math.sqrt(2.0 / math.pi) * (x + 0.044715 * (x * x * x))))


def _rmsnorm(x, g):
    return x * lax.rsqrt(jnp.mean(x * x, axis=-1, keepdims=True) + EPS) * g


class _Rows:
    def __init__(self, nb, ncb, ntb):
        self.nb, self.ncb, self.ntb, self.nlb = nb, ncb, ntb, ntb - ncb

    def n_blocks(self, lat_only):
        return self.nb * (self.nlb if lat_only else self.ntb)

    def src(self, lat_only):
        if lat_only:
            return lambda i: (i // self.nlb) * self.ntb + self.ncb + i % self.nlb
        return lambda i: i

    def sel(self, lat_only):
        if lat_only:
            return lambda i: i // self.nlb
        return lambda i: jnp.where(i % self.ntb < self.ncb, self.nb, i // self.ntb)


def _embed_kernel(ncb, ntb, ctx_ref, x_ref, pos_ref, o_ref):
    j = pl.program_id(0) % ntb

    @pl.when(j < ncb)
    def _():
        o_ref[...] = ctx_ref[...]

    @pl.when(j >= ncb)
    def _():
        o_ref[...] = x_ref[...] + pos_ref[...]


def _embed(ctx2, x2, pos, rows):
    d = ctx2.shape[1]
    ncb, ntb, nlb = rows.ncb, rows.ntb, rows.nlb
    return pl.pallas_call(
        functools.partial(_embed_kernel, ncb, ntb),
        out_shape=jax.ShapeDtypeStruct((rows.nb * ntb * SEG, d), F32),
        grid=(rows.nb * ntb,),
        in_specs=[pl.BlockSpec((SEG, d), lambda i: ((i // ntb) * ncb + jnp.minimum(i % ntb, ncb - 1), 0)),
                  pl.BlockSpec((SEG, d), lambda i: ((i // ntb) * nlb + jnp.maximum(i % ntb - ncb, 0), 0)),
                  pl.BlockSpec((SEG, d), lambda i: (jnp.maximum(i % ntb - ncb, 0), 0))],
        out_specs=pl.BlockSpec((SEG, d), lambda i: (i, 0)),
        compiler_params=_cparams(("arbitrary",)),
        name="embed",
    )(ctx2, x2, pos)


def _mod_kernel(c_ref, w_ref, b_ref, o_ref):
    s = _silu(c_ref[...])
    o_ref[0] = _dot(s, w_ref[0], precision=HIGHEST) + b_ref[0]


def _modulation(cond, w_ada, b_ada):
    depth, d, n6 = w_ada.shape
    tn = n6 // 4
    n_rows = cond.shape[0]
    return pl.pallas_call(
        _mod_kernel,
        out_shape=jax.ShapeDtypeStruct((depth, n_rows, n6), F32),
        grid=(depth, n6 // tn),
        in_specs=[pl.BlockSpec((n_rows, d), lambda l, j: (0, 0)),
                  pl.BlockSpec((1, d, tn), lambda l, j: (l, 0, j)),
                  pl.BlockSpec((1, 1, tn), lambda l, j: (l, 0, j))],
        out_specs=pl.BlockSpec((1, n_rows, tn), lambda l, j: (l, 0, j)),
        compiler_params=_cparams(("arbitrary", "arbitrary")),
        name="modulation",
    )(cond, w_ada, b_ada.reshape(depth, 1, n6))


PROJ_TM = 2 * SEG


def _proj_kernel(ncb, ntb, sel, h_ref, hp_ref, hn_ref, mod_ref, g_ref, w_ref, wgt_ref, cw_ref, cb_ref, post_ref,
                 pg_ref, pu_ref, pm_ref, qk_ref, go_ref, gt_ref):
    tm = h_ref.shape[0]
    n_seg = tm // SEG
    blk = [pl.program_id(0) * n_seg + q for q in range(n_seg)]
    mods = [mod_ref[sel(bq)] for bq in blk]
    act = lambda x, m: _rmsnorm(x, g_ref[...]) * (1.0 + m[1:2]) + m[0:1]
    a_seg = [act(h_ref[q * SEG:(q + 1) * SEG, :], mods[q]) for q in range(n_seg)]
    ab = jnp.concatenate(a_seg, axis=0).astype(BF16)
    c1, c2 = NG + 2 * LANE, NG + 2 * LANE + NM
    c3 = c2 + 2 * HP
    go_ref[...] = _dot(ab, w_ref[:, c3:]).astype(go_ref.dtype)
    pg_ref[...] = _dot(ab, w_ref[:, 0:NG])
    pu_ref[...] = _dot(ab, w_ref[:, NG:c1])
    pm_ref[...] = _dot(ab, w_ref[:, c1:c2])
    gt_ref[...] = _dot_nt(wgt_ref[...], ab)
    ae = jnp.concatenate([act(hp_ref[...], mods[0])] + a_seg + [act(hn_ref[...], mods[-1])], axis=0)
    xe = _dot(ae.astype(BF16), w_ref[:, c2:c3])
    n_e = tm + 16
    row = lax.broadcasted_iota(jnp.int32, (tm, xe.shape[1]), 0)
    keep_prev = jnp.ones((tm, xe.shape[1]), F32)
    keep_next = keep_prev
    for q, bq in enumerate(blk):
        j = bq % ntb
        first = jnp.logical_or(j == 0, j == ncb).astype(F32)
        last = jnp.logical_or(j == ncb - 1, j == ntb - 1).astype(F32)
        keep_prev = jnp.where(row == q * SEG, 1.0 - first, keep_prev)
        keep_next = jnp.where(row == (q + 1) * SEG - 1, 1.0 - last, keep_next)
    xp = pltpu.roll(xe, 1, axis=0)[8:8 + tm] * keep_prev
    xn = pltpu.roll(xe, n_e - 1, axis=0)[8:8 + tm] * keep_next
    y = cw_ref[0:1] * xp + cw_ref[1:2] * xe[8:8 + tm] + cw_ref[2:3] * xn + cb_ref[...]
    qk_ref[...] = _silu(y) * post_ref[...]


def _proj(h, modtab, g, w, wgt, cw, cb, post, *, layer, rows):
    r, d = h.shape
    tm = PROJ_TM
    assert r % tm == 0
    t8 = tm // 8
    full = lambda a: pl.BlockSpec(a.shape, lambda i: (0,) * a.ndim)
    return pl.pallas_call(
        functools.partial(_proj_kernel, rows.ncb, rows.ntb, rows.sel(False)),
        out_shape=(jax.ShapeDtypeStruct((r, NG), F32), jax.ShapeDtypeStruct((r, 2 * LANE), F32),
                   jax.ShapeDtypeStruct((r, NM), F32), jax.ShapeDtypeStruct((r, 2 * HP), F32),
                   jax.ShapeDtypeStruct((r, NGO), BF16),
                   jax.ShapeDtypeStruct((16, r), F32)),
        grid=(r // tm,),
        in_specs=[pl.BlockSpec((tm, d), lambda i: (i, 0)),
                  pl.BlockSpec((8, d), lambda i: (jnp.maximum(i * t8 - 1, 0), 0)),
                  pl.BlockSpec((8, d), lambda i: (jnp.minimum((i + 1) * t8, r // 8 - 1), 0)),
                  pl.BlockSpec((None,) + modtab.shape[1:], lambda i: (layer, 0, 0, 0)),
                  full(g), full(w), full(wgt), full(cw), full(cb), full(post)],
        out_specs=(pl.BlockSpec((tm, NG), lambda i: (i, 0)),
                   pl.BlockSpec((tm, 2 * LANE), lambda i: (i, 0)),
                   pl.BlockSpec((tm, NM), lambda i: (i, 0)),
                   pl.BlockSpec((tm, 2 * HP), lambda i: (i, 0)),
                   pl.BlockSpec((tm, NGO), lambda i: (i, 0)),
                   pl.BlockSpec((16, tm), lambda i: (0, i))),
        compiler_params=_cparams(("arbitrary",)),
        name="proj",
    )(h, h, h, modtab, g, w, wgt, cw, cb, post)


def _scan_pos(d, s, ncb, ntb):
    rev = jnp.where(s < ncb, ncb - 1 - s, ntb - 1 - (s - ncb))
    return jnp.where(d == 0, s, rev)


def _scan_pos_static(rev, s, ncb, ntb):
    if not rev:
        return s
    return jnp.where(s < ncb, ncb - 1 - s, ntb - 1 - (s - ncb))


def _tri(rev):
    r = lax.broadcasted_iota(jnp.int32, (CHUNK, CHUNK), 0)
    c = lax.broadcasted_iota(jnp.int32, (CHUNK, CHUNK), 1)
    return (r <= c) if rev else (r >= c)


def _chunk_rows(rev):
    return [(N_SUB - 1 - j if rev else j) * CHUNK for j in range(N_SUB)]


def _gla_kernel(rev, p_ref, wa_ref, ba_ref, o_ref, st_ref):
    s, g = pl.program_id(0), pl.program_id(1)
    nbb = p_ref.shape[0]
    b0 = g * nbb

    @pl.when(s == 0)
    def _():
        st_ref[pl.ds(b0, nbb)] = jnp.zeros((nbb,) + st_ref.shape[1:], F32)

    valid = _tri(rev)
    tri = valid.astype(F32)
    r0s = _chunk_rows(rev)
    wa, ba = wa_ref[...], ba_ref[...]
    inst = [(bb, j) for bb in range(nbb) for j in range(N_SUB)]
    heads = [slice(h * LANE, (h + 1) * LANE) for h in range(GLA_HEADS)]
    pairs = [slice((h // 2) * LANE, (h // 2 + 1) * LANE) for h in range(GLA_HEADS)]
    lane = lax.broadcasted_iota(jnp.int32, (CHUNK, LANE), 1)
    own = [(lane // GLA_KP) == (h % 2) for h in range(GLA_HEADS)]

    la = {}
    for bb, j in inst:
        lr = p_ref[bb, pl.ds(r0s[j], CHUNK), 2 * GQ + HP:NG]
        la[bb, j] = _log_sigmoid(_dot(lr, wa) + ba) * (1.0 / GLA_GATE_NORM)
    bc, e_last = {}, {}
    for i in inst:
        bc[i] = _dot_exact01(tri, la[i], lhs_is_01=True, pieces=2)
        e_last[i] = jnp.exp(jnp.sum(la[i], axis=0, keepdims=True))
    q_in, k_in, k_out, v = {}, {}, {}, {}
    for bb, j in inst:
        i = (bb, j)
        rs = pl.ds(r0s[j], CHUNK)
        qs = (p_ref[bb, rs, 0:GQ] * (GLA_DK ** -0.5) * jnp.exp(bc[i])).astype(BF16)
        for h in range(GLA_HEADS):
            q_in[i, h] = jnp.where(own[h], qs[:, pairs[h]], jnp.zeros_like(qs[:, pairs[h]]))
        kd = p_ref[bb, rs, GQ:2 * GQ] * jnp.exp(-bc[i])
        k_out[i] = (kd * e_last[i]).astype(BF16)
        k_in[i] = kd.astype(BF16)
        v[i] = p_ref[bb, rs, 2 * GQ:2 * GQ + HP].astype(BF16)
    att = {}
    for i in inst:
        for h in range(GLA_HEADS):
            att[i, h] = jnp.where(valid, _dot_nt(q_in[i, h], k_in[i][:, pairs[h]]), 0.0).astype(BF16)
    o_intra, ds = {}, {}
    for i in inst:
        for h, sl in enumerate(heads):
            o_intra[i, h] = _dot(att[i, h], v[i][:, sl])
            ds[i, h] = _dot_tn(v[i][:, sl], k_out[i][:, pairs[h]])
    s_in = {}
    for bb in range(nbb):
        for h in range(GLA_HEADS):
            st = st_ref[b0 + bb, h]
            for j in range(N_SUB):
                s_in[(bb, j), h] = st.astype(BF16)
                st = st * e_last[bb, j][:, pairs[h]] + ds[(bb, j), h]
            st_ref[b0 + bb, h] = st
    for bb, j in inst:
        for h, sl in enumerate(heads):
            o = o_intra[(bb, j), h] + _dot_nt(q_in[(bb, j), h], s_in[(bb, j), h])
            o_ref[bb, pl.ds(r0s[j], CHUNK), sl] = o.astype(o_ref.dtype)


def _gla(pg3, wa, ba, *, rows, rev):
    nb, l, _ = pg3.shape
    nbb = SCAN_BATCHES
    pos = functools.partial(_scan_pos_static, rev, ncb=rows.ncb, ntb=rows.ntb)
    return pl.pallas_call(
        functools.partial(_gla_kernel, rev),
        out_shape=jax.ShapeDtypeStruct((nb, l, HP), BF16),
        grid=(rows.ntb, nb // nbb),
        in_specs=[pl.BlockSpec((nbb, SEG, NG), lambda s, g: (g, pos(s), 0)),
                  pl.BlockSpec((LANE, GQ), lambda s, g: (0, 0)),
                  pl.BlockSpec((1, GQ), lambda s, g: (0, 0))],
        out_specs=pl.BlockSpec((nbb, SEG, HP), lambda s, g: (g, pos(s), 0)),
        scratch_shapes=[pltpu.VMEM((nb, GLA_HEADS, LANE, LANE), F32)],
        compiler_params=_cparams(("arbitrary", "arbitrary")),
        name="gla_scan_bwd" if rev else "gla_scan_fwd",
    )(pg3, wa, ba)


def _split_bf16(x, n):
    parts, r = [], x
    for _ in range(n):
        p = r.astype(BF16)
        parts.append(p)
        r = r - p.astype(F32)
    return parts


def _dot_exact01(a, b, lhs_is_01, pieces=3):
    if lhs_is_01:
        a = a.astype(BF16)
        terms = [_dot(a, p) for p in _split_bf16(b, pieces)]
    else:
        b = b.astype(BF16)
        terms = [_dot(p, b) for p in _split_bf16(a, pieces)]
    return functools.reduce(lambda x, y: x + y, terms)


def _cummax_rows(a, rev):
    n = a.shape[0]
    row = lax.broadcasted_iota(jnp.int32, a.shape, 0)
    k = 1
    while k < n:
        if rev:
            sh = jnp.where(row < n - k, pltpu.roll(a, n - k, axis=0), NEG)
        else:
            sh = jnp.where(row >= k, pltpu.roll(a, k, axis=0), NEG)
        a = jnp.maximum(a, sh)
        k *= 2
    return a


ML_GL = ML_HEADS


def _mlstm_kernel(rev, qk_ref, v_ref, g_ref, gt_ref, gbr_ref, gbt_ref, o_ref, st_ref, m_ref):
    s, g = pl.program_id(0), pl.program_id(1)
    nbb = qk_ref.shape[0]
    b0 = g * nbb

    @pl.when(s == 0)
    def _():
        st_ref[pl.ds(b0, nbb)] = jnp.zeros((nbb,) + st_ref.shape[1:], F32)
        m_ref[pl.ds(b0, nbb)] = jnp.zeros((nbb,) + m_ref.shape[1:], F32)

    valid = _tri(rev)
    tri = valid.astype(F32)
    r0s = _chunk_rows(rev)
    cs = [r // CHUNK for r in r0s]
    last = 0 if rev else CHUNK - 1
    inst = [(bb, j) for bb in range(nbb) for j in range(N_SUB)]
    heads = [slice(h * LANE, (h + 1) * LANE) for h in range(ML_HEADS)]

    r_sel = lax.broadcasted_iota(jnp.int32, (LANE, HP), 0)
    c_sel = lax.broadcasted_iota(jnp.int32, (LANE, HP), 1)
    sel_h = (r_sel == ML_GL + c_sel // LANE).astype(BF16)
    r_t = lax.broadcasted_iota(jnp.int32, (HP, HP), 0)
    c_t = lax.broadcasted_iota(jnp.int32, (HP, HP), 1)
    same = jnp.logical_and(r_t // LANE == c_t // LANE, jnp.logical_and(r_t % LANE < CHUNK, c_t % LANE < CHUNK))
    before = (r_t % LANE >= c_t % LANE) if rev else (r_t % LANE <= c_t % LANE)
    tri_b = jnp.logical_and(same, before).astype(BF16)
    r_v = lax.broadcasted_iota(jnp.int32, (CHUNK, HP), 0)
    c_v = lax.broadcasted_iota(jnp.int32, (CHUNK, HP), 1) % LANE
    valid4 = jnp.logical_and(c_v < CHUNK, (r_v <= c_v) if rev else (r_v >= c_v))
    lane4 = lax.broadcasted_iota(jnp.int32, (CHUNK, HP), 1) % LANE
    lane_c = lax.broadcasted_iota(jnp.int32, (CHUNK, LANE), 1)
    lane1 = lax.broadcasted_iota(jnp.int32, (1, LANE), 1)
    head_lane = jnp.logical_and(lane1 >= ML_GL, lane1 < ML_GL + ML_HEADS)
    gbr, gbt = gbr_ref[...], gbt_ref[...]

    gcs, fcm, cmx, a_row, grt = {}, {}, {}, {}, {}
    for bb, j in inst:
        gc = g_ref[bb, pl.ds(r0s[j], CHUNK), :] + gbr
        gcs[bb, j] = pltpu.roll(gc, ML_GL, axis=1)
        fcm[bb, j] = _dot_exact01(tri, _log_sigmoid(gc), lhs_is_01=True)
        grt[bb, j] = gt_ref[bb, cs[j]] + gbt
    row_id = lax.broadcasted_iota(jnp.int32, (len(inst), HP), 0)
    lfr = jnp.zeros((len(inst), HP), F32)
    for n, i in enumerate(inst):
        lfr = jnp.where(row_id == n, _log_sigmoid(grt[i][1:2]), lfr)
    fcr = _dot_exact01(lfr, tri_b, lhs_is_01=False)
    for n, i in enumerate(inst):
        a_row[i] = grt[i][0:1] - fcr[n:n + 1]
        cmx[i] = _cummax_rows(gcs[i] - fcm[i], rev)
    bx = {}
    e_neg, gd = {}, {}
    for bb in range(nbb):
        m_prev = m_ref[b0 + bb, 0:1, :]
        for j in range(N_SUB):
            i = (bb, j)
            m_t = fcm[i] + jnp.maximum(m_prev, cmx[i])
            m_new = m_t[last:last + 1]
            f_tot = fcm[i][last:last + 1]
            u = fcm[i] - m_t
            w_prev = jnp.exp(u + m_prev)
            w_s = jnp.exp(f_tot - fcm[i] + gcs[i] - m_new)
            gdec = jnp.broadcast_to(jnp.exp(f_tot + m_prev - m_new), (16, LANE))
            e_neg[i] = jnp.exp(-m_t)
            keep = lambda a: jnp.where(head_lane, a, 0.0)
            bx[i] = jnp.concatenate(_split_bf16(keep(u), 2) + _split_bf16(keep(w_prev), 1)
                                    + _split_bf16(keep(w_s), 1) + _split_bf16(keep(gdec), 2), axis=0)
            m_prev = m_new
        m_ref[b0 + bb] = jnp.broadcast_to(m_prev, (8, LANE))
    ub, wpb, wsb, gdb = {}, {}, {}, {}
    for i in inst:
        y = _dot(bx[i], sel_h)
        c = CHUNK
        ub[i] = y[0:c] + y[c:2 * c]
        wpb[i] = y[2 * c:3 * c]
        wsb[i] = y[3 * c:4 * c]
        gdb[i] = y[4 * c:4 * c + 1] + y[4 * c + 16:4 * c + 17]
    q, v, qkw, ds = {}, {}, {}, {}
    for bb, j in inst:
        i = (bb, j)
        rs = pl.ds(r0s[j], CHUNK)
        w = jnp.where(valid4, jnp.exp(ub[i] + a_row[i]), 0.0)
        q[i] = qk_ref[bb, rs, 0:HP].astype(BF16)
        k = qk_ref[bb, rs, HP:2 * HP]
        kb = k.astype(BF16)
        kw = (k * wsb[i]).astype(BF16)
        v[i] = jnp.where(lane4 == ML_DH, 1.0, v_ref[bb, rs, :]).astype(BF16)
        for h, sl in enumerate(heads):
            sc = _dot_nt(q[i][:, sl], kb[:, sl])
            qkw[i, h] = (sc * w[:, h * LANE:h * LANE + CHUNK]).astype(BF16)
            ds[i, h] = _dot_tn(v[i][:, sl], kw[:, sl])
    s_in = {}
    for bb in range(nbb):
        for h, sl in enumerate(heads):
            st = st_ref[b0 + bb, h]
            for j in range(N_SUB):
                i = (bb, j)
                s_in[i, h] = st.astype(BF16)
                st = gdb[i][:, sl] * st + ds[i, h]
            st_ref[b0 + bb, h] = st
    num = {}
    for i in inst:
        parts = [_dot_nt(q[i][:, sl], s_in[i, h]) for h, sl in enumerate(heads)]
        intra = [_dot(qkw[i, h], v[i][:, sl]) for h, sl in enumerate(heads)]
        num[i] = wpb[i] * jnp.concatenate(parts, axis=1) + jnp.concatenate(intra, axis=1)
    for bb, j in inst:
        i = (bb, j)
        den = jnp.zeros((CHUNK, LANE), F32)
        for h, sl in enumerate(heads):
            dh = jnp.sum(jnp.where(lane_c == ML_DH, num[i][:, sl], 0.0), axis=-1, keepdims=True)
            den = jnp.where(lane_c == ML_GL + h, jnp.broadcast_to(dh, (CHUNK, LANE)), den)
        r1, r2 = _split_bf16(jnp.where(head_lane, 1.0 / jnp.maximum(jnp.abs(den), e_neg[i]), 0.0), 2)
        rb = _dot(jnp.concatenate([r1, r2], axis=0), sel_h)
        rb = rb[0:CHUNK] + rb[CHUNK:2 * CHUNK]
        o_ref[bb, pl.ds(r0s[j], CHUNK), :] = jnp.where(lane4 < ML_DH, num[i] * rb, 0.0).astype(o_ref.dtype)


def _mlstm(qk3, pm3, gtl, gbr, gbt, *, rows, rev):
    nb, l, _ = qk3.shape
    nbb = SCAN_BATCHES
    dr = 1 if rev else 0
    pos = functools.partial(_scan_pos_static, rev, ncb=rows.ncb, ntb=rows.ntb)
    return pl.pallas_call(
        functools.partial(_mlstm_kernel, rev),
        out_shape=jax.ShapeDtypeStruct((nb, l, HP), BF16),
        grid=(rows.ntb, nb // nbb),
        in_specs=[pl.BlockSpec((nbb, SEG, 2 * HP), lambda s, g: (g, pos(s), 0)),
                  pl.BlockSpec((nbb, SEG, HP), lambda s, g: (g, pos(s), 0)),
                  pl.BlockSpec((nbb, SEG, LANE), lambda s, g: (g, pos(s), HP // LANE + dr)),
                  pl.BlockSpec((nbb, N_SUB, None, 2, HP), lambda s, g: (g, pos(s), dr, 0, 0)),
                  pl.BlockSpec((1, LANE), lambda s, g: (0, 0)),
                  pl.BlockSpec((2, HP), lambda s, g: (0, 0))],
        out_specs=pl.BlockSpec((nbb, SEG, HP), lambda s, g: (g, pos(s), 0)),
        scratch_shapes=[pltpu.VMEM((nb, ML_HEADS, LANE, LANE), F32), pltpu.VMEM((nb, 8, LANE), F32)],
        compiler_params=_cparams(("arbitrary", "arbitrary")),
        name="mlstm_scan_bwd" if rev else "mlstm_scan_fwd",
    )(qk3, pm3, pm3, gtl, gbr, gbt)


def _s5_kernel(nb, u_ref, bre_ref, bim_ref, are_ref, aim_ref, cre_ref, cim_ref, o_ref, xr_ref, xi_ref, st_ref):
    d, s = pl.program_id(0), pl.program_id(1)

    @pl.when(s == 0)
    def _():
        st_ref[...] = jnp.zeros(st_ref.shape, F32)

    n_sub = u_ref.shape[0] // (S5_SUB * nb)
    sub_rows = S5_SUB * nb

    def run(rev):
        ar, ai = are_ref[0], aim_ref[0]
        order = list(range(n_sub))[::-1] if rev else list(range(n_sub))

        def project_in(q):
            rs = slice(q * sub_rows, (q + 1) * sub_rows)
            u = u_ref[rs, :].astype(BF16)
            xr_ref[rs, :] = _dot(u, bre_ref[0])
            xi_ref[rs, :] = _dot(u, bim_ref[0])

        sr, si = st_ref[0], st_ref[1]
        project_in(order[0])
        for n, q in enumerate(order):
            if n + 1 < n_sub:
                project_in(order[n + 1])
            for j in range(S5_SUB):
                r0 = (q * S5_SUB + (S5_SUB - 1 - j if rev else j)) * nb
                nr = ar * sr - ai * si + xr_ref[r0:r0 + nb, :]
                ni = ar * si + ai * sr + xi_ref[r0:r0 + nb, :]
                xr_ref[r0:r0 + nb, :] = nr
                xi_ref[r0:r0 + nb, :] = ni
                sr, si = nr, ni
            rs = slice(q * sub_rows, (q + 1) * sub_rows)
            o_ref[0, rs, :] = (_dot(xr_ref[rs, :].astype(BF16), cre_ref[...])
                               - _dot(xi_ref[rs, :].astype(BF16), cim_ref[...])).astype(o_ref.dtype)
        st_ref[0] = sr
        st_ref[1] = si

    @pl.when(d == 0)
    def _():
        run(False)

    @pl.when(d == 1)
    def _():
        run(True)


def _s5(ut, bre, bim, are, aim, cre, cim, *, nb, nc, nt):
    n_rows, ch = ut.shape
    tr = S5_STEPS * nb
    ns = bre.shape[-1]
    pos = functools.partial(_scan_pos, ncb=nc, ntb=nt)
    return pl.pallas_call(
        functools.partial(_s5_kernel, nb),
        out_shape=jax.ShapeDtypeStruct((2, n_rows, ch), BF16),
        grid=(2, nt),
        in_specs=[pl.BlockSpec((tr, ch), lambda d, s: (pos(d, s), 0)),
                  pl.BlockSpec((1, ch, ns), lambda d, s: (d, 0, 0)),
                  pl.BlockSpec((1, ch, ns), lambda d, s: (d, 0, 0)),
                  pl.BlockSpec((1, nb, ns), lambda d, s: (d, 0, 0)),
                  pl.BlockSpec((1, nb, ns), lambda d, s: (d, 0, 0)),
                  pl.BlockSpec((ns, ch), lambda d, s: (0, 0)),
                  pl.BlockSpec((ns, ch), lambda d, s: (0, 0))],
        out_specs=pl.BlockSpec((1, tr, ch), lambda d, s: (d, pos(d, s), 0)),
        scratch_shapes=[pltpu.VMEM((tr, ns), F32), pltpu.VMEM((tr, ns), F32), pltpu.VMEM((2, nb, ns), F32)],
        compiler_params=_cparams(("arbitrary", "arbitrary")),
        name="s5_scan",
    )(ut, bre, bim, are, aim, cre, cim)


def _head_norm(o, gain, dim):
    parts = []
    for h in range(o.shape[1] // LANE):
        seg = o[:, h * LANE:(h + 1) * LANE]
        ms = jnp.sum(seg * seg, axis=-1, keepdims=True) * (1.0 / dim)
        parts.append(seg * lax.rsqrt(ms + EPS))
    return jnp.concatenate(parts, axis=1) * gain


def _mix_kernel(with_router, ogf_ref, ogb_ref, gg_ref, ys_ref, u_ref, omf_ref, omb_ref, mo_ref, h_ref, mod_ref,
                gn_ref, mn_ref, sd_ref, gw_ref, gb_ref, wo_ref, n2_ref, *rest):
    if with_router:
        wr_ref, ho_ref, f_ref, rt_ref = rest
    else:
        ho_ref, f_ref = rest
    gla = _head_norm(ogf_ref[...].astype(F32) + ogb_ref[...].astype(F32), gn_ref[...], GLA_DV) * _silu(gg_ref[...].astype(F32))
    z = _gelu_tanh(ys_ref[0].astype(F32) + ys_ref[1].astype(F32) + sd_ref[...] * u_ref[...])
    s5 = z * jax.nn.sigmoid(_dot(z.astype(BF16), gw_ref[...]) + gb_ref[...])
    ml = _head_norm(omf_ref[...].astype(F32) + omb_ref[...].astype(F32), mn_ref[...], ML_DH) * jax.nn.sigmoid(mo_ref[...].astype(F32))
    mix = (_dot(gla.astype(BF16), wo_ref[0:HP]) + _dot(s5.astype(BF16), wo_ref[HP:HP + 2 * LANE])
           + _dot(ml.astype(BF16), wo_ref[HP + 2 * LANE:]))
    m = mod_ref[0]
    hn = h_ref[...] + m[2:3] * mix
    ho_ref[...] = hn
    f = _rmsnorm(hn, n2_ref[...]) * (1.0 + m[4:5]) + m[3:4]
    if not with_router:
        f_ref[...] = f.astype(f_ref.dtype)
    if with_router:
        f_hi, f_lo = _split_bf16(f, 2)
        logits = (_dot(f_hi, wr_ref[0]) + _dot(f_lo, wr_ref[0]) + _dot(f_hi, wr_ref[1])).T[0:2 * N_EXPERTS]
        row = lax.broadcasted_iota(jnp.int32, logits.shape, 0)
        l0 = jnp.where(row < N_EXPERTS, logits, NEG)
        m1 = jnp.max(l0, axis=0, keepdims=True)
        i1 = jnp.min(jnp.where(l0 == m1, row, 2 * N_EXPERTS), axis=0, keepdims=True)
        l1 = jnp.where(row == i1, NEG, l0)
        m2 = jnp.max(l1, axis=0, keepdims=True)
        i2 = jnp.min(jnp.where(l1 == m2, row, 2 * N_EXPERTS), axis=0, keepdims=True)
        e = jnp.exp(m2 - m1)
        w1 = 1.0 / (1.0 + e)
        w2 = e / (1.0 + e)
        rt = jnp.where(row == 0, i1.astype(F32),
                       jnp.where(row == 1, i2.astype(F32),
                                 jnp.where(row == 2, w1, jnp.where(row == 3, w2, 0.0))))
        rt_ref[...] = rt
        d = f.shape[1]
        f_ref[:, 0:d] = f
        f_ref[:, d:] = jnp.concatenate([rt, jnp.zeros((LANE - rt.shape[0], rt.shape[1]), F32)], axis=0).T


def _mix(ogf, ogb, pg, ys, pu, omf, omb, pm, h, modtab, gn, mn, sd, gw, gb, wo, n2, wr, *, layer, rows, lat_only):
    d = h.shape[1]
    tm = SEG
    n = rows.n_blocks(lat_only)
    src, sel = rows.src(lat_only), rows.sel(lat_only)
    full = lambda a: pl.BlockSpec(a.shape, lambda i: (0,) * a.ndim)
    with_router = wr is not None
    in_specs = [pl.BlockSpec((tm, HP), lambda i: (src(i), 0)),
                pl.BlockSpec((tm, HP), lambda i: (src(i), 0)),
                pl.BlockSpec((tm, HP), lambda i: (src(i), 0)),
                pl.BlockSpec((2, tm, 2 * LANE), lambda i: (0, src(i), 0)),
                pl.BlockSpec((tm, 2 * LANE), lambda i: (src(i), 0)),
                pl.BlockSpec((tm, HP), lambda i: (src(i), 0)),
                pl.BlockSpec((tm, HP), lambda i: (src(i), 0)),
                pl.BlockSpec((tm, HP), lambda i: (src(i), 1)),
                pl.BlockSpec((tm, d), lambda i: (src(i), 0)),
                pl.BlockSpec((None, 1, 8, d), lambda i: (layer, sel(i), 0, 0)),
                full(gn), full(mn), full(sd), full(gw), full(gb), full(wo), full(n2)]
    args = [ogf, ogb, pg, ys, pu, omf, omb, pm, h, modtab, gn, mn, sd, gw, gb, wo, n2]
    fw, fdt = (d + LANE, F32) if with_router else (d, BF16)
    out_shape = [jax.ShapeDtypeStruct((n * tm, d), F32), jax.ShapeDtypeStruct((n * tm, fw), fdt)]
    out_specs = [pl.BlockSpec((tm, d), lambda i: (i, 0)), pl.BlockSpec((tm, fw), lambda i: (i, 0))]
    if with_router:
        in_specs.append(full(wr))
        args.append(wr)
        out_shape.append(jax.ShapeDtypeStruct((2 * N_EXPERTS, n * tm), F32))
        out_specs.append(pl.BlockSpec((2 * N_EXPERTS, tm), lambda i: (0, i)))
    return pl.pallas_call(
        functools.partial(_mix_kernel, with_router),
        out_shape=tuple(out_shape),
        grid=(n,),
        in_specs=in_specs,
        out_specs=tuple(out_specs),
        compiler_params=_cparams(("arbitrary",)),
        name="mix_out",
    )(*args)


FF_TILE = 256


def _swiglu(xb, w1_ref, w3_ref, w2_ref, a_ref, lead=()):
    dff = w1_ref.shape[-1]
    for j in range(dff // FF_TILE):
        sl = slice(j * FF_TILE, (j + 1) * FF_TILE)
        h1 = _dot(xb, w1_ref[lead + (slice(None), sl)])
        h3 = _dot(xb, w3_ref[lead + (slice(None), sl)])
        a_ref[:, sl] = (_silu(h1) * h3).astype(BF16)
    return _dot(a_ref[...], w2_ref[lead + (slice(None), slice(None))])


FFN_TM = 1024
MIXFFN_TM = 2 * SEG


def _mix_ffn_kernel(sel, ogf_ref, ogb_ref, gg_ref, ys_ref, u_ref, omf_ref, omb_ref, mo_ref, h_ref, mod_ref,
                    gn_ref, mn_ref, sd_ref, gw_ref, gb_ref, wo_ref, n2_ref, w1_ref, w3_ref, w2_ref, o_ref,
                    a_ref, hn_ref, f_ref):
    n_seg = h_ref.shape[0] // SEG
    mods = []
    for q in range(n_seg):
        rs = slice(q * SEG, (q + 1) * SEG)
        m = mod_ref[sel(pl.program_id(0) * n_seg + q)]
        mods.append(m)
        gla = (_head_norm(ogf_ref[rs, :].astype(F32) + ogb_ref[rs, :].astype(F32), gn_ref[...], GLA_DV)
               * _silu(gg_ref[rs, :].astype(F32)))
        z = _gelu_tanh(ys_ref[0, rs, :].astype(F32) + ys_ref[1, rs, :].astype(F32) + sd_ref[...] * u_ref[rs, :])
        s5 = z * jax.nn.sigmoid(_dot(z.astype(BF16), gw_ref[...]) + gb_ref[...])
        ml = (_head_norm(omf_ref[rs, :].astype(F32) + omb_ref[rs, :].astype(F32), mn_ref[...], ML_DH)
              * jax.nn.sigmoid(mo_ref[rs, :].astype(F32)))
        mix = (_dot(gla.astype(BF16), wo_ref[0:HP]) + _dot(s5.astype(BF16), wo_ref[HP:HP + 2 * LANE])
               + _dot(ml.astype(BF16), wo_ref[HP + 2 * LANE:]))
        hn = h_ref[rs, :] + m[2:3] * mix
        hn_ref[rs, :] = hn
        f_ref[rs, :] = (_rmsnorm(hn, n2_ref[...]) * (1.0 + m[4:5]) + m[3:4]).astype(BF16)
    y = _swiglu(f_ref[...], w1_ref, w3_ref, w2_ref, a_ref)
    for q in range(n_seg):
        rs = slice(q * SEG, (q + 1) * SEG)
        o_ref[rs, :] = hn_ref[rs, :] + mods[q][5:6] * y[rs]


def _mix_ffn(ogf, ogb, pg, ys, pu, omf, omb, pm, h, modtab, gn, mn, sd, gw, gb, wo, n2, w1, w3, w2, *, layer, rows):
    r, d = h.shape
    tm = MIXFFN_TM
    assert r % tm == 0
    dff = w1.shape[-1]
    full = lambda a: pl.BlockSpec(a.shape, lambda i: (0,) * a.ndim)
    resident = lambda a: pl.BlockSpec(a.shape, lambda i: (0,) * a.ndim, pipeline_mode=pl.Buffered(1))
    in_specs = [pl.BlockSpec((tm, HP), lambda i: (i, 0)),
                pl.BlockSpec((tm, HP), lambda i: (i, 0)),
                pl.BlockSpec((tm, HP), lambda i: (i, 0)),
                pl.BlockSpec((2, tm, 2 * LANE), lambda i: (0, i, 0)),
                pl.BlockSpec((tm, 2 * LANE), lambda i: (i, 0)),
                pl.BlockSpec((tm, HP), lambda i: (i, 0)),
                pl.BlockSpec((tm, HP), lambda i: (i, 0)),
                pl.BlockSpec((tm, HP), lambda i: (i, 1)),
                pl.BlockSpec((tm, d), lambda i: (i, 0)),
                pl.BlockSpec((None,) + modtab.shape[1:], lambda i: (layer, 0, 0, 0)),
                full(gn), full(mn), full(sd), full(gw), full(gb), resident(wo), full(n2),
                resident(w1), resident(w3), resident(w2)]
    return pl.pallas_call(
        functools.partial(_mix_ffn_kernel, rows.sel(False)),
        out_shape=jax.ShapeDtypeStruct((r, d), F32),
        grid=(r // tm,),
        in_specs=in_specs,
        out_specs=pl.BlockSpec((tm, d), lambda i: (i, 0)),
        scratch_shapes=[pltpu.VMEM((tm, dff), BF16), pltpu.VMEM((tm, d), F32), pltpu.VMEM((tm, d), BF16)],
        compiler_params=_cparams(("arbitrary",)),
        name="mix_ffn",
    )(ogf, ogb, pg, ys, pu, omf, omb, pm, h, modtab, gn, mn, sd, gw, gb, wo, n2, w1, w3, w2)


def _ffn_kernel(final, sel, f_ref, h_ref, mod_ref, w1_ref, w3_ref, w2_ref, *rest):
    if final:
        nf_ref, o_ref, a_ref = rest
    else:
        o_ref, a_ref = rest
    y = _swiglu(f_ref[...], w1_ref, w3_ref, w2_ref, a_ref)
    n_seg = f_ref.shape[0] // SEG
    for q in range(n_seg):
        rs = slice(q * SEG, (q + 1) * SEG)
        gate = mod_ref[sel(pl.program_id(0) * n_seg + q)][5:6]
        hn = h_ref[rs, :] + gate * y[rs]
        o_ref[rs, :] = _rmsnorm(hn, nf_ref[...]) if final else hn


def _ffn(f, h, modtab, w1, w3, w2, nf, *, layer, rows, lat_only):
    r, d = h.shape
    tm = FFN_TM
    assert r % tm == 0
    dff = w1.shape[-1]
    full = lambda a: pl.BlockSpec(a.shape, lambda i: (0,) * a.ndim)
    resident = lambda a: pl.BlockSpec(a.shape, lambda i: (0,) * a.ndim, pipeline_mode=pl.Buffered(1))
    in_specs = [pl.BlockSpec((tm, d), lambda i: (i, 0)),
                pl.BlockSpec((tm, d), lambda i: (i, 0)),
                pl.BlockSpec((None,) + modtab.shape[1:], lambda i: (layer, 0, 0, 0)),
                resident(w1), resident(w3), resident(w2)]
    args = [f, h, modtab, w1, w3, w2]
    if lat_only:
        in_specs.append(full(nf))
        args.append(nf)
    return pl.pallas_call(
        functools.partial(_ffn_kernel, lat_only, rows.sel(lat_only)),
        out_shape=jax.ShapeDtypeStruct((r, d), F32),
        grid=(r // tm,),
        in_specs=in_specs,
        out_specs=pl.BlockSpec((tm, d), lambda i: (i, 0)),
        scratch_shapes=[pltpu.VMEM((tm, dff), BF16)],
        compiler_params=_cparams(("arbitrary",)),
        name="ffn",
    )(*args)


MOE_TM = 512


def _moe_kernel(be_ref, live_ref, x_ref, w1_ref, w3_ref, w2_ref, o_ref, a_ref):
    i = pl.program_id(0)
    d = w1_ref.shape[1]

    @pl.when(live_ref[i] > 0)
    def _():
        y = _swiglu(x_ref[:, 0:d].astype(BF16), w1_ref, w3_ref, w2_ref, a_ref, lead=(0,))
        tail = x_ref[:, d:]
        mine = tail[:, 0:1] == be_ref[i].astype(F32)
        o_ref[...] = _pack_bf16_pairs(y * jnp.where(mine, tail[:, 2:3], tail[:, 3:4]))

    @pl.when(live_ref[i] == 0)
    def _():
        o_ref[...] = jnp.zeros(o_ref.shape, F32)


def _moe_experts(block_expert, block_live, xg, w1, w3, w2):
    n_rows, dx = xg.shape
    d = w1.shape[-2]
    dff = w1.shape[-1]
    tm = MOE_TM
    return pl.pallas_call(
        _moe_kernel,
        out_shape=jax.ShapeDtypeStruct((n_rows, d // 2), F32),
        grid_spec=pltpu.PrefetchScalarGridSpec(
            num_scalar_prefetch=2,
            grid=(n_rows // tm,),
            in_specs=[pl.BlockSpec((tm, dx), lambda i, be, lv: (i, 0)),
                      pl.BlockSpec((1, d, dff), lambda i, be, lv: (be[i], 0, 0)),
                      pl.BlockSpec((1, d, dff), lambda i, be, lv: (be[i], 0, 0)),
                      pl.BlockSpec((1, dff, d), lambda i, be, lv: (be[i], 0, 0))],
            out_specs=pl.BlockSpec((tm, d // 2), lambda i, be, lv: (i, 0)),
            scratch_shapes=[pltpu.VMEM((tm, dff), BF16)]),
        compiler_params=_cparams(("arbitrary",)),
        name="moe_experts",
    )(block_expert, block_live, xg, w1, w3, w2)


SC_GATHER_ROWS = 64


def _gather_rows(table, idx):
    n_idx = idx.shape[0]
    _, d = table.shape
    info = plsc.get_sparse_core_info()
    n_cores, n_workers = info.num_cores, info.num_cores * info.num_subcores
    assert n_idx % (n_workers * SC_GATHER_ROWS) == 0
    per_worker = n_idx // n_workers
    mesh = plsc.VectorSubcoreMesh(core_axis_name="c", subcore_axis_name="s")

    @functools.partial(
        pl.kernel, mesh=mesh,
        out_type=jax.ShapeDtypeStruct((n_idx, d), table.dtype),
        scratch_types=[pltpu.VMEM((SC_GATHER_ROWS,), jnp.int32),
                       pltpu.VMEM((SC_GATHER_ROWS, d), table.dtype),
                       pltpu.SemaphoreType.DMA])
    def gather(table_hbm, idx_hbm, out_hbm, idx_v, rows_v, sem):
        base = (lax.axis_index("s") * n_cores + lax.axis_index("c")) * per_worker

        @pl.loop(0, per_worker // SC_GATHER_ROWS)
        def _(it):
            off = pl.multiple_of(base + it * SC_GATHER_ROWS, SC_GATHER_ROWS)
            pltpu.sync_copy(idx_hbm.at[pl.ds(off, SC_GATHER_ROWS)], idx_v)
            pltpu.async_copy(table_hbm.at[idx_v], rows_v, sem).wait()
            pltpu.sync_copy(rows_v, out_hbm.at[pl.ds(off, SC_GATHER_ROWS)])

    return gather(table, idx)


def _scatter_rows(src, dest, n_rows):
    n_tok, d = src.shape
    info = plsc.get_sparse_core_info()
    n_cores, n_workers = info.num_cores, info.num_cores * info.num_subcores
    assert n_tok % (n_workers * SC_GATHER_ROWS) == 0
    per_worker = n_tok // n_workers
    mesh = plsc.VectorSubcoreMesh(core_axis_name="c", subcore_axis_name="s")

    @functools.partial(
        pl.kernel, mesh=mesh,
        out_type=jax.ShapeDtypeStruct((n_rows, d), src.dtype),
        scratch_types=[pltpu.VMEM((SC_GATHER_ROWS,), jnp.int32),
                       pltpu.VMEM((SC_GATHER_ROWS, d), src.dtype),
                       pltpu.SemaphoreType.DMA])
    def scatter(src_hbm, dest_hbm, out_hbm, idx_v, rows_v, sem):
        base = (lax.axis_index("s") * n_cores + lax.axis_index("c")) * per_worker

        @pl.loop(0, per_worker // SC_GATHER_ROWS)
        def _(it):
            off = pl.multiple_of(base + it * SC_GATHER_ROWS, SC_GATHER_ROWS)
            pltpu.sync_copy(src_hbm.at[pl.ds(off, SC_GATHER_ROWS)], rows_v)
            for k in range(TOP_K):
                pltpu.sync_copy(dest_hbm.at[pl.ds(k * n_tok + off, SC_GATHER_ROWS)], idx_v)
                pltpu.async_copy(rows_v, out_hbm.at[idx_v], sem).wait()

    return scatter(src, dest.reshape(-1))


def _pack_bf16_pairs(y):
    n = y.shape[1] // 2
    hi = pltpu.bitcast(y[:, :n].astype(BF16).astype(F32), jnp.uint32)
    lo = pltpu.bitcast(y[:, n:].astype(BF16).astype(F32), jnp.uint32)
    return pltpu.bitcast(hi | (lo >> 16), F32)


def _unpack_bf16_pairs(w):
    u = pltpu.bitcast(w, jnp.uint32)
    return pltpu.bitcast(u & jnp.uint32(0xFFFF0000), F32), pltpu.bitcast(u << 16, F32)


def _resid_kernel(final, h_ref, y0_ref, y1_ref, mod_ref, *rest):
    a0, b0 = _unpack_bf16_pairs(y0_ref[...])
    a1, b1 = _unpack_bf16_pairs(y1_ref[...])
    hn = h_ref[...] + mod_ref[0][5:6] * jnp.concatenate([a0 + a1, b0 + b1], axis=1)
    if final:
        nf_ref, o_ref = rest
        o_ref[...] = _rmsnorm(hn, nf_ref[...])
    else:
        (o_ref,) = rest
        o_ref[...] = hn


def _moe_resid(h, y0, y1, modtab, nf, *, layer, rows, lat_only):
    r, d = h.shape
    tm = SEG
    sel = rows.sel(lat_only)
    row = lambda i: (i, 0)
    in_specs = [pl.BlockSpec((tm, d), row), pl.BlockSpec((tm, d // 2), row), pl.BlockSpec((tm, d // 2), row),
                pl.BlockSpec((None, 1, 8, d), lambda i: (layer, sel(i), 0, 0))]
    args = [h, y0, y1, modtab]
    if lat_only:
        in_specs.append(pl.BlockSpec((1, d), lambda i: (0, 0)))
        args.append(nf)
    return pl.pallas_call(
        functools.partial(_resid_kernel, lat_only),
        out_shape=jax.ShapeDtypeStruct((r, d), F32),
        grid=(r // tm,),
        in_specs=in_specs,
        out_specs=pl.BlockSpec((tm, d), row),
        compiler_params=_cparams(("arbitrary",)),
        name="moe_resid",
    )(*args)


CAST_PARTS = 8


def _cast_kernel(*refs):
    o_ref = refs[-1]
    tr = refs[0].shape[1]
    for k, x_ref in enumerate(refs[:-1]):
        o_ref[0, k * tr:(k + 1) * tr, :] = x_ref[0].astype(o_ref.dtype)


def _to_bf16(w, j):
    lead, (r, c) = w.shape[1:-2], w.shape[-2:]
    n = int(np.prod(lead, dtype=np.int64))
    w3 = w.reshape((-1, r, c))
    tr = r // CAST_PARTS
    assert tr * CAST_PARTS == r and tr % 16 == 0
    band = lambda k: pl.BlockSpec((1, tr, c), lambda e: (j * n + e, k, 0))
    out = pl.pallas_call(
        _cast_kernel,
        out_shape=jax.ShapeDtypeStruct((n, r, c), BF16),
        grid=(n,),
        in_specs=[band(k) for k in range(CAST_PARTS)],
        out_specs=pl.BlockSpec((1, r, c), lambda e: (e, 0, 0)),
        compiler_params=_cparams(("arbitrary",)),
        name="to_bf16",
    )(*([w3] * CAST_PARTS))
    return out.reshape(lead + (r, c))


def _pad_heads(w, heads, dim, to=LANE):
    lead = w.shape[:-1]
    w = w.reshape(lead + (heads, dim))
    w = jnp.pad(w, [(0, 0)] * len(lead) + [(0, 0), (0, to - dim)])
    return w.reshape(lead + (heads * to,))


def _pad_last(w, to):
    return jnp.pad(w, [(0, 0)] * (w.ndim - 1) + [(0, to - w.shape[-1])])


def _pos_embed(n_tokens, d):
    n_grid_rows = n_tokens // GRID_W
    row, col = jnp.meshgrid(jnp.arange(n_grid_rows, dtype=F32), jnp.arange(GRID_W, dtype=F32), indexing='ij')
    n_freq = d // 4
    omega = jnp.exp(-math.log(POS_BASE) * jnp.arange(n_freq, dtype=F32) / n_freq)

    def axis_embed(p):
        ang = p.reshape(-1, 1) * omega
        return jnp.concatenate([jnp.sin(ang), jnp.cos(ang)], axis=-1)

    return jnp.concatenate([axis_embed(row), axis_embed(col)], axis=-1)


def _s5_discretise(lam_re, lam_im, log_dt, b_re, b_im):
    dt = jnp.exp(log_dt)[:, None]
    mag = jnp.exp(lam_re * dt)
    abar_re, abar_im = mag * jnp.cos(lam_im * dt), mag * jnp.sin(lam_im * dt)
    den = lam_re * lam_re + lam_im * lam_im
    pr, pi = abar_re - 1.0, abar_im
    coef_re = (pr * lam_re + pi * lam_im) / den
    coef_im = (pi * lam_re - pr * lam_im) / den
    bbar_re = coef_re[..., None] * b_re - coef_im[..., None] * b_im
    bbar_im = coef_re[..., None] * b_im + coef_im[..., None] * b_re
    return abar_re, abar_im, bbar_re, bbar_im


def _block_diag(m):
    g, a, b = m.shape
    eye = jnp.eye(g, dtype=m.dtype)
    return (eye[:, None, :, None] * m[:, :, None, :]).reshape(g * a, g * b)


def _route_plan(route, tm):
    n_tok = route.shape[1]
    n_assign = n_tok * TOP_K
    flat_e = route[0:TOP_K].astype(jnp.int32).reshape(-1)
    onehot = (jnp.arange(N_EXPERTS, dtype=jnp.int32)[:, None] == flat_e[None, :]).astype(jnp.int32)
    csum = jnp.cumsum(onehot, axis=1)
    counts = csum[:, -1]
    padded = (counts + tm - 1) // tm * tm
    pend = jnp.cumsum(padded)
    pstart = pend - padded
    dest = jnp.sum(onehot * (csum - 1 + pstart[:, None]), axis=0)
    n_blocks = -(-n_assign // tm) + N_EXPERTS
    block_start = jnp.arange(n_blocks, dtype=jnp.int32) * tm
    block_expert = jnp.minimum(jnp.searchsorted(pend, block_start, side='right'), N_EXPERTS - 1).astype(jnp.int32)
    block_live = (block_start < (pstart + counts)[block_expert]).astype(jnp.int32)
    return n_blocks * tm, block_expert, block_live, dest.reshape(TOP_K, n_tok)


def kernel(x, c, ctx, c_ctx, w_ada, b_ada, norm1, norm2, w_in, w_out, gla_wa2, gla_ba, gla_norm, s5_lam_re, s5_lam_im, s5_log_dt, s5_b_re, s5_b_im, s5_c_re, s5_c_im, s5_d, s5_glu_w, s5_glu_b, ml_conv_w, ml_conv_b, ml_gate_b, ml_norm, ffn_w1, ffn_w3, ffn_w2, moe_router, moe_w1, moe_w3, moe_w2, norm_f):
    nb, n_lat, d = x.shape
    lc = ctx.shape[1]
    depth = w_ada.shape[0]
    l = lc + n_lat
    assert lc % SEG == 0 and n_lat % SEG == 0 and nb == 8 and nb % SCAN_BATCHES == 0
    rows = _Rows(nb, lc // SEG, l // SEG)

    h = _embed(ctx.reshape(nb * lc, d), x.reshape(nb * n_lat, d), _pos_embed(n_lat, d), rows)

    cond = jnp.zeros((16, d), F32).at[:nb].set(c).at[nb].set(c_ctx)
    mod = _modulation(cond, w_ada, b_ada)
    modtab = jnp.pad(mod.reshape(depth, 16, 6, d), ((0, 0), (0, 0), (0, 2), (0, 0)))

    dk, dv, dh = GLA_HEADS * GLA_DK, GLA_HEADS * GLA_DV, ML_HEADS * ML_DH
    s5c = s5_d.shape[-1]
    cuts = np.cumsum([dk, dk, dv, GLA_RANK, dv, s5c, dh, dh, dh, dh, 4 * ML_HEADS])

    for i in range(depth):
        last = i == depth - 1
        gq, gk, gv, glr, gg, su, mq, mk, mv, mo, mg = jnp.split(w_in[i], cuts[:-1], axis=-1)
        w_all = jnp.concatenate([
            _pad_heads(gq, GLA_HEADS, GLA_DK, GLA_KP), _pad_heads(gk, GLA_HEADS, GLA_DK, GLA_KP),
            _pad_heads(gv, GLA_HEADS, GLA_DV), _pad_last(glr, LANE),
            su,
            _pad_heads(mv, ML_HEADS, ML_DH),
            _pad_last(mg[:, :2 * ML_HEADS], LANE), _pad_last(mg[:, 2 * ML_HEADS:], LANE),
            _pad_heads(mq, ML_HEADS, ML_DH), _pad_heads(mk, ML_HEADS, ML_DH),
            _pad_heads(gg, GLA_HEADS, GLA_DV), _pad_heads(mo, ML_HEADS, ML_DH)],
            axis=-1).astype(BF16)
        wgt = mg.T.astype(BF16)
        cw = jnp.concatenate([_pad_heads(ml_conv_w[i][:, :dh], ML_HEADS, ML_DH),
                              _pad_heads(ml_conv_w[i][:, dh:], ML_HEADS, ML_DH)], axis=-1)
        cw = jnp.pad(cw, ((0, 8 - ML_CONV), (0, 0)))
        cb = jnp.concatenate([_pad_heads(ml_conv_b[i][:dh], ML_HEADS, ML_DH),
                              _pad_heads(ml_conv_b[i][dh:], ML_HEADS, ML_DH)])[None]
        post = jnp.concatenate([jnp.ones((HP,), F32), jnp.full((HP,), ML_DH ** -0.5, F32)])[None]
        pg, pu, pm, qk, go, gt = _proj(h, modtab, norm1[i][None], w_all, wgt, cw, cb, post, layer=i, rows=rows)

        wa = jnp.pad(_pad_heads(gla_wa2[i], GLA_HEADS, GLA_DK, GLA_KP), ((0, 0), (0, LANE - GLA_RANK), (0, 0)))
        ba = _pad_heads(gla_ba[i], GLA_HEADS, GLA_DK, GLA_KP)[:, None, :]
        pg3 = pg.reshape(nb, l, NG)
        ogf = _gla(pg3, wa[0], ba[0], rows=rows, rev=False).reshape(nb * l, HP)
        ogb = _gla(pg3, wa[1], ba[1], rows=rows, rev=True).reshape(nb * l, HP)

        bres, bims, ares, aims = [], [], [], []
        for dr in (0, 1):
            a_re, a_im, b_re, b_im = _s5_discretise(s5_lam_re[i, dr], s5_lam_im[i, dr], s5_log_dt[i, dr],
                                                    s5_b_re[i], s5_b_im[i])
            bres.append(_block_diag(jnp.swapaxes(b_re, 1, 2)))
            bims.append(_block_diag(jnp.swapaxes(b_im, 1, 2)))
            ares.append(jnp.broadcast_to(a_re.reshape(1, -1), (nb, a_re.size)))
            aims.append(jnp.broadcast_to(a_im.reshape(1, -1), (nb, a_im.size)))
        cre = _block_diag(jnp.swapaxes(s5_c_re[i], 1, 2)).astype(BF16)
        cim = _block_diag(jnp.swapaxes(s5_c_im[i], 1, 2)).astype(BF16)
        ut = pu.reshape(nb, l, s5c).swapaxes(0, 1).reshape(l * nb, s5c)
        yt = _s5(ut, jnp.stack(bres).astype(BF16), jnp.stack(bims).astype(BF16), jnp.stack(ares), jnp.stack(aims),
                 cre, cim, nb=nb, nc=lc // S5_STEPS, nt=l // S5_STEPS)
        ys = yt.reshape(2, l, nb, s5c).swapaxes(1, 2).reshape(2, nb * l, s5c)

        gb = ml_gate_b[i].reshape(2, 2, ML_HEADS)
        gbr = _pad_last(gb.reshape(2, 1, 2 * ML_HEADS), LANE)
        gbt = _pad_last(jnp.broadcast_to(gb[..., None], (2, 2, ML_HEADS, CHUNK)), LANE).reshape(2, 2, HP)
        gtl = _pad_last(gt.reshape(2, 2, ML_HEADS, nb, l // CHUNK, CHUNK).transpose(3, 4, 0, 1, 2, 5),
                        LANE).reshape(nb, l // CHUNK, 2, 2, HP)
        qk3, pm3 = qk.reshape(nb, l, 2 * HP), pm.reshape(nb, l, NM)
        omf = _mlstm(qk3, pm3, gtl, gbr[0], gbt[0], rows=rows, rev=False).reshape(nb * l, HP)
        omb = _mlstm(qk3, pm3, gtl, gbr[1], gbt[1], rows=rows, rev=True).reshape(nb * l, HP)

        wo = w_out[i]
        wo_p = jnp.concatenate([
            jnp.pad(wo[:dv].reshape(GLA_HEADS, GLA_DV, d), ((0, 0), (0, LANE - GLA_DV), (0, 0))).reshape(HP, d),
            wo[dv:dv + s5c],
            jnp.pad(wo[dv + s5c:].reshape(ML_HEADS, ML_DH, d), ((0, 0), (0, LANE - ML_DH), (0, 0))).reshape(HP, d)],
            axis=0).astype(BF16)
        gn = jnp.tile(_pad_last(gla_norm[i], LANE), GLA_HEADS)[None]
        mn = jnp.tile(_pad_last(ml_norm[i], LANE), ML_HEADS)[None]
        is_moe = i % 2 == 1
        j = i // 2
        wr = jnp.stack(_split_bf16(_pad_last(moe_router[j], LANE), 2)) if is_moe else None
        mix_args = (ogf, ogb, go, ys, pu, omf, omb, go, h, modtab, gn, mn, s5_d[i][None], s5_glu_w[i].astype(BF16),
                    s5_glu_b[i][None], wo_p, norm2[i][None])
        if not is_moe and not last:
            h = _mix_ffn(*mix_args, _to_bf16(ffn_w1, j), _to_bf16(ffn_w3, j), _to_bf16(ffn_w2, j),
                         layer=i, rows=rows)
            continue
        outs = _mix(*mix_args, wr, layer=i, rows=rows, lat_only=last)
        if not is_moe:
            h, f = outs
            h = _ffn(f, h, modtab, _to_bf16(ffn_w1, j), _to_bf16(ffn_w3, j), _to_bf16(ffn_w2, j),
                     norm_f[None], layer=i, rows=rows, lat_only=last)
        else:
            h, f, route = outs
            n_rows, block_expert, block_live, dest = _route_plan(route, MOE_TM)
            xg = _scatter_rows(f, dest, n_rows)
            yg = _moe_experts(block_expert, block_live, xg, _to_bf16(moe_w1, j), _to_bf16(moe_w3, j),
                              _to_bf16(moe_w2, j))
            y0 = _gather_rows(yg, dest[0])
            y1 = _gather_rows(yg, dest[1])
            h = _moe_resid(h, y0, y1, modtab, norm_f[None], layer=i, rows=rows, lat_only=last)
    return h.reshape(nb, n_lat, d)
```

```python
import functools
import math

import numpy as np
import jax
import jax.numpy as jnp
from jax import lax
from jax.experimental import pallas as pl
from jax.experimental.pallas import tpu as pltpu
from jax.experimental.pallas import tpu_sc as plsc

F32 = jnp.float32
BF16 = jnp.bfloat16
HIGHEST = lax.Precision.HIGHEST

GRID_W = 64
POS_BASE = 10000.0
EPS = 1e-6
GLA_HEADS, GLA_DK, GLA_DV, GLA_RANK, GLA_GATE_NORM = 4, 48, 96, 16, 16.0
S5_GROUP, S5_STATE = 16, 64
ML_HEADS, ML_DH, ML_CONV = 4, 96, 3
N_EXPERTS, TOP_K = 8, 2

LANE = 128
CHUNK = 64
SEG = 256
N_SUB = SEG // CHUNK
SCAN_BATCHES = 8
S5_STEPS = 128
S5_SUB = 32
NEG = -1e30
VMEM_LIMIT = 56 * 1024 * 1024

HP = LANE * GLA_HEADS
GLA_KP = 64
GQ = GLA_HEADS * GLA_KP
NG = 2 * GQ + HP + LANE
NM = HP + 2 * LANE
NGO = 2 * HP


def _cparams(sem):
    return pltpu.CompilerParams(dimension_semantics=sem, vmem_limit_bytes=VMEM_LIMIT)


def _dot(a, b, **kw):
    return jnp.dot(a, b, preferred_element_type=F32, **kw)


def _dot_nt(a, b, **kw):
    return lax.dot_general(a, b, (((1,), (1,)), ((), ())), preferred_element_type=F32, **kw)


def _dot_tn(a, b, **kw):
    return lax.dot_general(a, b, (((0,), (0,)), ((), ())), preferred_element_type=F32, **kw)


def _log_sigmoid(x):
    return jnp.minimum(x, 0.0) - jnp.log1p(jnp.exp(-jnp.abs(x)))


def _silu(x):
    return x * jax.nn.sigmoid(x)


def _gelu_tanh(x):
    return 0.5 * x * (1.0 + jnp.tanh(math.sqrt(2.0 / math.pi) * (x + 0.044715 * (x * x * x))))


def _rmsnorm(x, g):
    return x * lax.rsqrt(jnp.mean(x * x, axis=-1, keepdims=True) + EPS) * g


class _Rows:
    def __init__(self, nb, ncb, ntb):
        self.nb, self.ncb, self.ntb, self.nlb = nb, ncb, ntb, ntb - ncb

    def n_blocks(self, lat_only):
        return self.nb * (self.nlb if lat_only else self.ntb)

    def src(self, lat_only):
        if lat_only:
            return lambda i: (i // self.nlb) * self.ntb + self.ncb + i % self.nlb
        return lambda i: i

    def sel(self, lat_only):
        if lat_only:
            return lambda i: i // self.nlb
        return lambda i: jnp.where(i % self.ntb < self.ncb, self.nb, i // self.ntb)


def _embed_kernel(ncb, ntb, ctx_ref, x_ref, pos_ref, o_ref):
    j = pl.program_id(0) % ntb

    @pl.when(j < ncb)
    def _():
        o_ref[...] = ctx_ref[...]

    @pl.when(j >= ncb)
    def _():
        o_ref[...] = x_ref[...] + pos_ref[...]


def _embed(ctx2, x2, pos, rows):
    d = ctx2.shape[1]
    ncb, ntb, nlb = rows.ncb, rows.ntb, rows.nlb
    return pl.pallas_call(
        functools.partial(_embed_kernel, ncb, ntb),
        out_shape=jax.ShapeDtypeStruct((rows.nb * ntb * SEG, d), F32),
        grid=(rows.nb * ntb,),
        in_specs=[pl.BlockSpec((SEG, d), lambda i: ((i // ntb) * ncb + jnp.minimum(i % ntb, ncb - 1), 0)),
                  pl.BlockSpec((SEG, d), lambda i: ((i // ntb) * nlb + jnp.maximum(i % ntb - ncb, 0), 0)),
                  pl.BlockSpec((SEG, d), lambda i: (jnp.maximum(i % ntb - ncb, 0), 0))],
        out_specs=pl.BlockSpec((SEG, d), lambda i: (i, 0)),
        compiler_params=_cparams(("arbitrary",)),
        name="embed",
    )(ctx2, x2, pos)


def _mod_kernel(c_ref, w_ref, b_ref, o_ref):
    s = _silu(c_ref[...])
    o_ref[0] = _dot(s, w_ref[0], precision=HIGHEST) + b_ref[0]


def _modulation(cond, w_ada, b_ada):
    depth, d, n6 = w_ada.shape
    tn = n6 // 4
    n_rows = cond.shape[0]
    return pl.pallas_call(
        _mod_kernel,
        out_shape=jax.ShapeDtypeStruct((depth, n_rows, n6), F32),
        grid=(depth, n6 // tn),
        in_specs=[pl.BlockSpec((n_rows, d), lambda l, j: (0, 0)),
                  pl.BlockSpec((1, d, tn), lambda l, j: (l, 0, j)),
                  pl.BlockSpec((1, 1, tn), lambda l, j: (l, 0, j))],
        out_specs=pl.BlockSpec((1, n_rows, tn), lambda l, j: (l, 0, j)),
        compiler_params=_cparams(("arbitrary", "arbitrary")),
        name="modulation",
    )(cond, w_ada, b_ada.reshape(depth, 1, n6))


PROJ_TM = 2 * SEG


def _proj_kernel(ncb, ntb, sel, h_ref, hp_ref, hn_ref, mod_ref, g_ref, w_ref, wgt_ref, cw_ref, cb_ref, post_ref,
                 pg_ref, pu_ref, pm_ref, qk_ref, go_ref, gt_ref):
    tm = h_ref.shape[0]
    n_seg = tm // SEG
    blk = [pl.program_id(0) * n_seg + q for q in range(n_seg)]
    mods = [mod_ref[sel(bq)] for bq in blk]
    act = lambda x, m: _rmsnorm(x, g_ref[...]) * (1.0 + m[1:2]) + m[0:1]
    a_seg = [act(h_ref[q * SEG:(q + 1) * SEG, :], mods[q]) for q in range(n_seg)]
    ab = jnp.concatenate(a_seg, axis=0).astype(BF16)
    c1, c2 = NG + 2 * LANE, NG + 2 * LANE + NM
    c3 = c2 + 2 * HP
    go_ref[...] = _dot(ab, w_ref[:, c3:]).astype(go_ref.dtype)
    pg_ref[...] = _dot(ab, w_ref[:, 0:NG])
    pu_ref[...] = _dot(ab, w_ref[:, NG:c1])
    pm_ref[...] = _dot(ab, w_ref[:, c1:c2])
    gt_ref[...] = _dot_nt(wgt_ref[...], ab)
    ae = jnp.concatenate([act(hp_ref[...], mods[0])] + a_seg + [act(hn_ref[...], mods[-1])], axis=0)
    xe = _dot(ae.astype(BF16), w_ref[:, c2:c3])
    n_e = tm + 16
    row = lax.broadcasted_iota(jnp.int32, (tm, xe.shape[1]), 0)
    keep_prev = jnp.ones((tm, xe.shape[1]), F32)
    keep_next = keep_prev
    for q, bq in enumerate(blk):
        j = bq % ntb
        first = jnp.logical_or(j == 0, j == ncb).astype(F32)
        last = jnp.logical_or(j == ncb - 1, j == ntb - 1).astype(F32)
        keep_prev = jnp.where(row == q * SEG, 1.0 - first, keep_prev)
        keep_next = jnp.where(row == (q + 1) * SEG - 1, 1.0 - last, keep_next)
    xp = pltpu.roll(xe, 1, axis=0)[8:8 + tm] * keep_prev
    xn = pltpu.roll(xe, n_e - 1, axis=0)[8:8 + tm] * keep_next
    y = cw_ref[0:1] * xp + cw_ref[1:2] * xe[8:8 + tm] + cw_ref[2:3] * xn + cb_ref[...]
    qk_ref[...] = _silu(y) * post_ref[...]


def _proj(h, modtab, g, w, wgt, cw, cb, post, *, layer, rows):
    r, d = h.shape
    tm = PROJ_TM
    assert r % tm == 0
    t8 = tm // 8
    full = lambda a: pl.BlockSpec(a.shape, lambda i: (0,) * a.ndim)
    return pl.pallas_call(
        functools.partial(_proj_kernel, rows.ncb, rows.ntb, rows.sel(False)),
        out_shape=(jax.ShapeDtypeStruct((r, NG), F32), jax.ShapeDtypeStruct((r, 2 * LANE), F32),
                   jax.ShapeDtypeStruct((r, NM), F32), jax.ShapeDtypeStruct((r, 2 * HP), F32),
                   jax.ShapeDtypeStruct((r, NGO), BF16),
                   jax.ShapeDtypeStruct((16, r), F32)),
        grid=(r // tm,),
        in_specs=[pl.BlockSpec((tm, d), lambda i: (i, 0)),
                  pl.BlockSpec((8, d), lambda i: (jnp.maximum(i * t8 - 1, 0), 0)),
                  pl.BlockSpec((8, d), lambda i: (jnp.minimum((i + 1) * t8, r // 8 - 1), 0)),
                  pl.BlockSpec((None,) + modtab.shape[1:], lambda i: (layer, 0, 0, 0)),
                  full(g), full(w), full(wgt), full(cw), full(cb), full(post)],
        out_specs=(pl.BlockSpec((tm, NG), lambda i: (i, 0)),
                   pl.BlockSpec((tm, 2 * LANE), lambda i: (i, 0)),
                   pl.BlockSpec((tm, NM), lambda i: (i, 0)),
                   pl.BlockSpec((tm, 2 * HP), lambda i: (i, 0)),
                   pl.BlockSpec((tm, NGO), lambda i: (i, 0)),
                   pl.BlockSpec((16, tm), lambda i: (0, i))),
        compiler_params=_cparams(("arbitrary",)),
        name="proj",
    )(h, h, h, modtab, g, w, wgt, cw, cb, post)


def _scan_pos(d, s, ncb, ntb):
    rev = jnp.where(s < ncb, ncb - 1 - s, ntb - 1 - (s - ncb))
    return jnp.where(d == 0, s, rev)


def _scan_pos_static(rev, s, ncb, ntb):
    if not rev:
        return s
    return jnp.where(s < ncb, ncb - 1 - s, ntb - 1 - (s - ncb))


def _tri(rev):
    r = lax.broadcasted_iota(jnp.int32, (CHUNK, CHUNK), 0)
    c = lax.broadcasted_iota(jnp.int32, (CHUNK, CHUNK), 1)
    return (r <= c) if rev else (r >= c)


def _chunk_rows(rev):
    return [(N_SUB - 1 - j if rev else j) * CHUNK for j in range(N_SUB)]


def _gla_kernel(rev, p_ref, wa_ref, ba_ref, o_ref, st_ref):
    s, g = pl.program_id(0), pl.program_id(1)
    nbb = p_ref.shape[0]
    b0 = g * nbb

    @pl.when(s == 0)
    def _():
        st_ref[pl.ds(b0, nbb)] = jnp.zeros((nbb,) + st_ref.shape[1:], F32)

    valid = _tri(rev)
    tri = valid.astype(F32)
    r0s = _chunk_rows(rev)
    wa, ba = wa_ref[...], ba_ref[...]
    inst = [(bb, j) for bb in range(nbb) for j in range(N_SUB)]
    heads = [slice(h * LANE, (h + 1) * LANE) for h in range(GLA_HEADS)]
    pairs = [slice((h // 2) * LANE, (h // 2 + 1) * LANE) for h in range(GLA_HEADS)]
    lane = lax.broadcasted_iota(jnp.int32, (CHUNK, LANE), 1)
    own = [(lane // GLA_KP) == (h % 2) for h in range(GLA_HEADS)]

    la = {}
    for bb, j in inst:
        lr = p_ref[bb, pl.ds(r0s[j], CHUNK), 2 * GQ + HP:NG]
        la[bb, j] = _log_sigmoid(_dot(lr, wa) + ba) * (1.0 / GLA_GATE_NORM)
    bc, e_last = {}, {}
    for i in inst:
        bc[i] = _dot_exact01(tri, la[i], lhs_is_01=True, pieces=2)
        e_last[i] = jnp.exp(jnp.sum(la[i], axis=0, keepdims=True))
    q_in, k_in, k_out, v = {}, {}, {}, {}
    for bb, j in inst:
        i = (bb, j)
        rs = pl.ds(r0s[j], CHUNK)
        qs = (p_ref[bb, rs, 0:GQ] * (GLA_DK ** -0.5) * jnp.exp(bc[i])).astype(BF16)
        for h in range(GLA_HEADS):
            q_in[i, h] = jnp.where(own[h], qs[:, pairs[h]], jnp.zeros_like(qs[:, pairs[h]]))
        kd = p_ref[bb, rs, GQ:2 * GQ] * jnp.exp(-bc[i])
        k_out[i] = (kd * e_last[i]).astype(BF16)
        k_in[i] = kd.astype(BF16)
        v[i] = p_ref[bb, rs, 2 * GQ:2 * GQ + HP].astype(BF16)
    att = {}
    for i in inst:
        for h in range(GLA_HEADS):
            att[i, h] = jnp.where(valid, _dot_nt(q_in[i, h], k_in[i][:, pairs[h]]), 0.0).astype(BF16)
    o_intra, ds = {}, {}
    for i in inst:
        for h, sl in enumerate(heads):
            o_intra[i, h] = _dot(att[i, h], v[i][:, sl])
            ds[i, h] = _dot_tn(v[i][:, sl], k_out[i][:, pairs[h]])
    s_in = {}
    for bb in range(nbb):
        for h in range(GLA_HEADS):
            st = st_ref[b0 + bb, h]
            for j in range(N_SUB):
                s_in[(bb, j), h] = st.astype(BF16)
                st = st * e_last[bb, j][:, pairs[h]] + ds[(bb, j), h]
            st_ref[b0 + bb, h] = st
    for bb, j in inst:
        for h, sl in enumerate(heads):
            o = o_intra[(bb, j), h] + _dot_nt(q_in[(bb, j), h], s_in[(bb, j), h])
            o_ref[bb, pl.ds(r0s[j], CHUNK), sl] = o.astype(o_ref.dtype)


def _gla(pg3, wa, ba, *, rows, rev):
    nb, l, _ = pg3.shape
    nbb = SCAN_BATCHES
    pos = functools.partial(_scan_pos_static, rev, ncb=rows.ncb, ntb=rows.ntb)
    return pl.pallas_call(
        functools.partial(_gla_kernel, rev),
        out_shape=jax.ShapeDtypeStruct((nb, l, HP), BF16),
        grid=(rows.ntb, nb // nbb),
        in_specs=[pl.BlockSpec((nbb, SEG, NG), lambda s, g: (g, pos(s), 0)),
                  pl.BlockSpec((LANE, GQ), lambda s, g: (0, 0)),
                  pl.BlockSpec((1, GQ), lambda s, g: (0, 0))],
        out_specs=pl.BlockSpec((nbb, SEG, HP), lambda s, g: (g, pos(s), 0)),
        scratch_shapes=[pltpu.VMEM((nb, GLA_HEADS, LANE, LANE), F32)],
        compiler_params=_cparams(("arbitrary", "arbitrary")),
        name="gla_scan_bwd" if rev else "gla_scan_fwd",
    )(pg3, wa, ba)


def _split_bf16(x, n):
    parts, r = [], x
    for _ in range(n):
        p = r.astype(BF16)
        parts.append(p)
        r = r - p.astype(F32)
    return parts


def _dot_exact01(a, b, lhs_is_01, pieces=3):
    if lhs_is_01:
        a = a.astype(BF16)
        terms = [_dot(a, p) for p in _split_bf16(b, pieces)]
    else:
        b = b.astype(BF16)
        terms = [_dot(p, b) for p in _split_bf16(a, pieces)]
    return functools.reduce(lambda x, y: x + y, terms)


def _cummax_rows(a, rev):
    n = a.shape[0]
    row = lax.broadcasted_iota(jnp.int32, a.shape, 0)
    k = 1
    while k < n:
        if rev:
            sh = jnp.where(row < n - k, pltpu.roll(a, n - k, axis=0), NEG)
        else:
            sh = jnp.where(row >= k, pltpu.roll(a, k, axis=0), NEG)
        a = jnp.maximum(a, sh)
        k *= 2
    return a


ML_GL = ML_HEADS


def _mlstm_kernel(rev, qk_ref, v_ref, g_ref, gt_ref, gbr_ref, gbt_ref, o_ref, st_ref, m_ref):
    s, g = pl.program_id(0), pl.program_id(1)
    nbb = qk_ref.shape[0]
    b0 = g * nbb

    @pl.when(s == 0)
    def _():
        st_ref[pl.ds(b0, nbb)] = jnp.zeros((nbb,) + st_ref.shape[1:], F32)
        m_ref[pl.ds(b0, nbb)] = jnp.zeros((nbb,) + m_ref.shape[1:], F32)

    valid = _tri(rev)
    tri = valid.astype(F32)
    r0s = _chunk_rows(rev)
    cs = [r // CHUNK for r in r0s]
    last = 0 if rev else CHUNK - 1
    inst = [(bb, j) for bb in range(nbb) for j in range(N_SUB)]
    heads = [slice(h * LANE, (h + 1) * LANE) for h in range(ML_HEADS)]

    r_sel = lax.broadcasted_iota(jnp.int32, (LANE, HP), 0)
    c_sel = lax.broadcasted_iota(jnp.int32, (LANE, HP), 1)
    sel_h = (r_sel == ML_GL + c_sel // LANE).astype(BF16)
    r_t = lax.broadcasted_iota(jnp.int32, (HP, HP), 0)
    c_t = lax.broadcasted_iota(jnp.int32, (HP, HP), 1)
    same = jnp.logical_and(r_t // LANE == c_t // LANE, jnp.logical_and(r_t % LANE < CHUNK, c_t % LANE < CHUNK))
    before = (r_t % LANE >= c_t % LANE) if rev else (r_t % LANE <= c_t % LANE)
    tri_b = jnp.logical_and(same, before).astype(BF16)
    r_v = lax.broadcasted_iota(jnp.int32, (CHUNK, HP), 0)
    c_v = lax.broadcasted_iota(jnp.int32, (CHUNK, HP), 1) % LANE
    valid4 = jnp.logical_and(c_v < CHUNK, (r_v <= c_v) if rev else (r_v >= c_v))
    lane4 = lax.broadcasted_iota(jnp.int32, (CHUNK, HP), 1) % LANE
    lane_c = lax.broadcasted_iota(jnp.int32, (CHUNK, LANE), 1)
    lane1 = lax.broadcasted_iota(jnp.int32, (1, LANE), 1)
    head_lane = jnp.logical_and(lane1 >= ML_GL, lane1 < ML_GL + ML_HEADS)
    gbr, gbt = gbr_ref[...], gbt_ref[...]

    gcs, fcm, cmx, a_row, grt = {}, {}, {}, {}, {}
    for bb, j in inst:
        gc = g_ref[bb, pl.ds(r0s[j], CHUNK), :] + gbr
        gcs[bb, j] = pltpu.roll(gc, ML_GL, axis=1)
        fcm[bb, j] = _dot_exact01(tri, _log_sigmoid(gc), lhs_is_01=True)
        grt[bb, j] = gt_ref[bb, cs[j]] + gbt
    row_id = lax.broadcasted_iota(jnp.int32, (len(inst), HP), 0)
    lfr = jnp.zeros((len(inst), HP), F32)
    for n, i in enumerate(inst):
        lfr = jnp.where(row_id == n, _log_sigmoid(grt[i][1:2]), lfr)
    fcr = _dot_exact01(lfr, tri_b, lhs_is_01=False)
    for n, i in enumerate(inst):
        a_row[i] = grt[i][0:1] - fcr[n:n + 1]
        cmx[i] = _cummax_rows(gcs[i] - fcm[i], rev)
    bx = {}
    e_neg, gd = {}, {}
    for bb in range(nbb):
        m_prev = m_ref[b0 + bb, 0:1, :]
        for j in range(N_SUB):
            i = (bb, j)
            m_t = fcm[i] + jnp.maximum(m_prev, cmx[i])
            m_new = m_t[last:last + 1]
            f_tot = fcm[i][last:last + 1]
            u = fcm[i] - m_t
            w_prev = jnp.exp(u + m_prev)
            w_s = jnp.exp(f_tot - fcm[i] + gcs[i] - m_new)
            gdec = jnp.broadcast_to(jnp.exp(f_tot + m_prev - m_new), (16, LANE))
            e_neg[i] = jnp.exp(-m_t)
            keep = lambda a: jnp.where(head_lane, a, 0.0)
            bx[i] = jnp.concatenate(_split_bf16(keep(u), 2) + _split_bf16(keep(w_prev), 1)
                                    + _split_bf16(keep(w_s), 1) + _split_bf16(keep(gdec), 2), axis=0)
            m_prev = m_new
        m_ref[b0 + bb] = jnp.broadcast_to(m_prev, (8, LANE))
    ub, wpb, wsb, gdb = {}, {}, {}, {}
    for i in inst:
        y = _dot(bx[i], sel_h)
        c = CHUNK
        ub[i] = y[0:c] + y[c:2 * c]
        wpb[i] = y[2 * c:3 * c]
        wsb[i] = y[3 * c:4 * c]
        gdb[i] = y[4 * c:4 * c + 1] + y[4 * c + 16:4 * c + 17]
    q, v, qkw, ds = {}, {}, {}, {}
    for bb, j in inst:
        i = (bb, j)
        rs = pl.ds(r0s[j], CHUNK)
        w = jnp.where(valid4, jnp.exp(ub[i] + a_row[i]), 0.0)
        q[i] = qk_ref[bb, rs, 0:HP].astype(BF16)
        k = qk_ref[bb, rs, HP:2 * HP]
        kb = k.astype(BF16)
        kw = (k * wsb[i]).astype(BF16)
        v[i] = jnp.where(lane4 == ML_DH, 1.0, v_ref[bb, rs, :]).astype(BF16)
        for h, sl in enumerate(heads):
            sc = _dot_nt(q[i][:, sl], kb[:, sl])
            qkw[i, h] = (sc * w[:, h * LANE:h * LANE + CHUNK]).astype(BF16)
            ds[i, h] = _dot_tn(v[i][:, sl], kw[:, sl])
    s_in = {}
    for bb in range(nbb):
        for h, sl in enumerate(heads):
            st = st_ref[b0 + bb, h]
            for j in range(N_SUB):
                i = (bb, j)
                s_in[i, h] = st.astype(BF16)
                st = gdb[i][:, sl] * st + ds[i, h]
            st_ref[b0 + bb, h] = st
    num = {}
    for i in inst:
        parts = [_dot_nt(q[i][:, sl], s_in[i, h]) for h, sl in enumerate(heads)]
        intra = [_dot(qkw[i, h], v[i][:, sl]) for h, sl in enumerate(heads)]
        num[i] = wpb[i] * jnp.concatenate(parts, axis=1) + jnp.concatenate(intra, axis=1)
    for bb, j in inst:
        i = (bb, j)
        den = jnp.zeros((CHUNK, LANE), F32)
        for h, sl in enumerate(heads):
            dh = jnp.sum(jnp.where(lane_c == ML_DH, num[i][:, sl], 0.0), axis=-1, keepdims=True)
            den = jnp.where(lane_c == ML_GL + h, jnp.broadcast_to(dh, (CHUNK, LANE)), den)
        r1, r2 = _split_bf16(jnp.where(head_lane, 1.0 / jnp.maximum(jnp.abs(den), e_neg[i]), 0.0), 2)
        rb = _dot(jnp.concatenate([r1, r2], axis=0), sel_h)
        rb = rb[0:CHUNK] + rb[CHUNK:2 * CHUNK]
        o_ref[bb, pl.ds(r0s[j], CHUNK), :] = jnp.where(lane4 < ML_DH, num[i] * rb, 0.0).astype(o_ref.dtype)


def _mlstm(qk3, pm3, gtl, gbr, gbt, *, rows, rev):
    nb, l, _ = qk3.shape
    nbb = SCAN_BATCHES
    dr = 1 if rev else 0
    pos = functools.partial(_scan_pos_static, rev, ncb=rows.ncb, ntb=rows.ntb)
    return pl.pallas_call(
        functools.partial(_mlstm_kernel, rev),
        out_shape=jax.ShapeDtypeStruct((nb, l, HP), BF16),
        grid=(rows.ntb, nb // nbb),
        in_specs=[pl.BlockSpec((nbb, SEG, 2 * HP), lambda s, g: (g, pos(s), 0)),
                  pl.BlockSpec((nbb, SEG, HP), lambda s, g: (g, pos(s), 0)),
                  pl.BlockSpec((nbb, SEG, LANE), lambda s, g: (g, pos(s), HP // LANE + dr)),
                  pl.BlockSpec((nbb, N_SUB, None, 2, HP), lambda s, g: (g, pos(s), dr, 0, 0)),
                  pl.BlockSpec((1, LANE), lambda s, g: (0, 0)),
                  pl.BlockSpec((2, HP), lambda s, g: (0, 0))],
        out_specs=pl.BlockSpec((nbb, SEG, HP), lambda s, g: (g, pos(s), 0)),
        scratch_shapes=[pltpu.VMEM((nb, ML_HEADS, LANE, LANE), F32), pltpu.VMEM((nb, 8, LANE), F32)],
        compiler_params=_cparams(("arbitrary", "arbitrary")),
        name="mlstm_scan_bwd" if rev else "mlstm_scan_fwd",
    )(qk3, pm3, pm3, gtl, gbr, gbt)


def _s5_kernel(nb, u_ref, bre_ref, bim_ref, are_ref, aim_ref, cre_ref, cim_ref, o_ref, xr_ref, xi_ref, st_ref):
    d, s = pl.program_id(0), pl.program_id(1)

    @pl.when(s == 0)
    def _():
        st_ref[...] = jnp.zeros(st_ref.shape, F32)

    n_sub = u_ref.shape[0] // (S5_SUB * nb)
    sub_rows = S5_SUB * nb

    def run(rev):
        ar, ai = are_ref[0], aim_ref[0]
        order = list(range(n_sub))[::-1] if rev else list(range(n_sub))

        def project_in(q):
            rs = slice(q * sub_rows, (q + 1) * sub_rows)
            u = u_ref[rs, :].astype(BF16)
            xr_ref[rs, :] = _dot(u, bre_ref[0])
            xi_ref[rs, :] = _dot(u, bim_ref[0])

        sr, si = st_ref[0], st_ref[1]
        project_in(order[0])
        for n, q in enumerate(order):
            if n + 1 < n_sub:
                project_in(order[n + 1])
            for j in range(S5_SUB):
                r0 = (q * S5_SUB + (S5_SUB - 1 - j if rev else j)) * nb
                nr = ar * sr - ai * si + xr_ref[r0:r0 + nb, :]
                ni = ar * si + ai * sr + xi_ref[r0:r0 + nb, :]
                xr_ref[r0:r0 + nb, :] = nr
                xi_ref[r0:r0 + nb, :] = ni
                sr, si = nr, ni
            rs = slice(q * sub_rows, (q + 1) * sub_rows)
            o_ref[0, rs, :] = (_dot(xr_ref[rs, :].astype(BF16), cre_ref[...])
                               - _dot(xi_ref[rs, :].astype(BF16), cim_ref[...])).astype(o_ref.dtype)
        st_ref[0] = sr
        st_ref[1] = si

    @pl.when(d == 0)
    def _():
        run(False)

    @pl.when(d == 1)
    def _():
        run(True)


def _s5(ut, bre, bim, are, aim, cre, cim, *, nb, nc, nt):
    n_rows, ch = ut.shape
    tr = S5_STEPS * nb
    ns = bre.shape[-1]
    pos = functools.partial(_scan_pos, ncb=nc, ntb=nt)
    return pl.pallas_call(
        functools.partial(_s5_kernel, nb),
        out_shape=jax.ShapeDtypeStruct((2, n_rows, ch), BF16),
        grid=(2, nt),
        in_specs=[pl.BlockSpec((tr, ch), lambda d, s: (pos(d, s), 0)),
                  pl.BlockSpec((1, ch, ns), lambda d, s: (d, 0, 0)),
                  pl.BlockSpec((1, ch, ns), lambda d, s: (d, 0, 0)),
                  pl.BlockSpec((1, nb, ns), lambda d, s: (d, 0, 0)),
                  pl.BlockSpec((1, nb, ns), lambda d, s: (d, 0, 0)),
                  pl.BlockSpec((ns, ch), lambda d, s: (0, 0)),
                  pl.BlockSpec((ns, ch), lambda d, s: (0, 0))],
        out_specs=pl.BlockSpec((1, tr, ch), lambda d, s: (d, pos(d, s), 0)),
        scratch_shapes=[pltpu.VMEM((tr, ns), F32), pltpu.VMEM((tr, ns), F32), pltpu.VMEM((2, nb, ns), F32)],
        compiler_params=_cparams(("arbitrary", "arbitrary")),
        name="s5_scan",
    )(ut, bre, bim, are, aim, cre, cim)


def _head_norm(o, gain, dim):
    parts = []
    for h in range(o.shape[1] // LANE):
        seg = o[:, h * LANE:(h + 1) * LANE]
        ms = jnp.sum(seg * seg, axis=-1, keepdims=True) * (1.0 / dim)
        parts.append(seg * lax.rsqrt(ms + EPS))
    return jnp.concatenate(parts, axis=1) * gain


def _mix_kernel(with_router, ogf_ref, ogb_ref, gg_ref, ys_ref, u_ref, omf_ref, omb_ref, mo_ref, h_ref, mod_ref,
                gn_ref, mn_ref, sd_ref, gw_ref, gb_ref, wo_ref, n2_ref, *rest):
    if with_router:
        wr_ref, ho_ref, f_ref, rt_ref = rest
    else:
        ho_ref, f_ref = rest
    gla = _head_norm(ogf_ref[...].astype(F32) + ogb_ref[...].astype(F32), gn_ref[...], GLA_DV) * _silu(gg_ref[...].astype(F32))
    z = _gelu_tanh(ys_ref[0].astype(F32) + ys_ref[1].astype(F32) + sd_ref[...] * u_ref[...])
    s5 = z * jax.nn.sigmoid(_dot(z.astype(BF16), gw_ref[...]) + gb_ref[...])
    ml = _head_norm(omf_ref[...].astype(F32) + omb_ref[...].astype(F32), mn_ref[...], ML_DH) * jax.nn.sigmoid(mo_ref[...].astype(F32))
    mix = (_dot(gla.astype(BF16), wo_ref[0:HP]) + _dot(s5.astype(BF16), wo_ref[HP:HP + 2 * LANE])
           + _dot(ml.astype(BF16), wo_ref[HP + 2 * LANE:]))
    m = mod_ref[0]
    hn = h_ref[...] + m[2:3] * mix
    ho_ref[...] = hn
    f = _rmsnorm(hn, n2_ref[...]) * (1.0 + m[4:5]) + m[3:4]
    if not with_router:
        f_ref[...] = f.astype(f_ref.dtype)
    if with_router:
        f_hi, f_lo = _split_bf16(f, 2)
        logits = (_dot(f_hi, wr_ref[0]) + _dot(f_lo, wr_ref[0]) + _dot(f_hi, wr_ref[1])).T[0:2 * N_EXPERTS]
        row = lax.broadcasted_iota(jnp.int32, logits.shape, 0)
        l0 = jnp.where(row < N_EXPERTS, logits, NEG)
        m1 = jnp.max(l0, axis=0, keepdims=True)
        i1 = jnp.min(jnp.where(l0 == m1, row, 2 * N_EXPERTS), axis=0, keepdims=True)
        l1 = jnp.where(row == i1, NEG, l0)
        m2 = jnp.max(l1, axis=0, keepdims=True)
        i2 = jnp.min(jnp.where(l1 == m2, row, 2 * N_EXPERTS), axis=0, keepdims=True)
        e = jnp.exp(m2 - m1)
        w1 = 1.0 / (1.0 + e)
        w2 = e / (1.0 + e)
        rt = jnp.where(row == 0, i1.astype(F32),
                       jnp.where(row == 1, i2.astype(F32),
                                 jnp.where(row == 2, w1, jnp.where(row == 3, w2, 0.0))))
        rt_ref[...] = rt
        d = f.shape[1]
        f_ref[:, 0:d // 2] = _pack_bf16_pairs(f)
        f_ref[:, d // 2:] = jnp.concatenate([rt, jnp.zeros((LANE - rt.shape[0], rt.shape[1]), F32)], axis=0).T


def _mix(ogf, ogb, pg, ys, pu, omf, omb, pm, h, modtab, gn, mn, sd, gw, gb, wo, n2, wr, *, layer, rows, lat_only):
    d = h.shape[1]
    tm = SEG
    n = rows.n_blocks(lat_only)
    src, sel = rows.src(lat_only), rows.sel(lat_only)
    full = lambda a: pl.BlockSpec(a.shape, lambda i: (0,) * a.ndim)
    with_router = wr is not None
    in_specs = [pl.BlockSpec((tm, HP), lambda i: (src(i), 0)),
                pl.BlockSpec((tm, HP), lambda i: (src(i), 0)),
                pl.BlockSpec((tm, HP), lambda i: (src(i), 0)),
                pl.BlockSpec((2, tm, 2 * LANE), lambda i: (0, src(i), 0)),
                pl.BlockSpec((tm, 2 * LANE), lambda i: (src(i), 0)),
                pl.BlockSpec((tm, HP), lambda i: (src(i), 0)),
                pl.BlockSpec((tm, HP), lambda i: (src(i), 0)),
                pl.BlockSpec((tm, HP), lambda i: (src(i), 1)),
                pl.BlockSpec((tm, d), lambda i: (src(i), 0)),
                pl.BlockSpec((None, 1, 8, d), lambda i: (layer, sel(i), 0, 0)),
                full(gn), full(mn), full(sd), full(gw), full(gb), full(wo), full(n2)]
    args = [ogf, ogb, pg, ys, pu, omf, omb, pm, h, modtab, gn, mn, sd, gw, gb, wo, n2]
    fw, fdt = (d // 2 + LANE, F32) if with_router else (d, BF16)
    out_shape = [jax.ShapeDtypeStruct((n * tm, d), F32), jax.ShapeDtypeStruct((n * tm, fw), fdt)]
    out_specs = [pl.BlockSpec((tm, d), lambda i: (i, 0)), pl.BlockSpec((tm, fw), lambda i: (i, 0))]
    if with_router:
        in_specs.append(full(wr))
        args.append(wr)
        out_shape.append(jax.ShapeDtypeStruct((2 * N_EXPERTS, n * tm), F32))
        out_specs.append(pl.BlockSpec((2 * N_EXPERTS, tm), lambda i: (0, i)))
    return pl.pallas_call(
        functools.partial(_mix_kernel, with_router),
        out_shape=tuple(out_shape),
        grid=(n,),
        in_specs=in_specs,
        out_specs=tuple(out_specs),
        compiler_params=_cparams(("arbitrary",)),
        name="mix_out",
    )(*args)


FF_TILE = 256


def _swiglu(xb, w1_ref, w3_ref, w2_ref, a_ref, lead=()):
    dff = w1_ref.shape[-1]
    for j in range(dff // FF_TILE):
        sl = slice(j * FF_TILE, (j + 1) * FF_TILE)
        h1 = _dot(xb, w1_ref[lead + (slice(None), sl)])
        h3 = _dot(xb, w3_ref[lead + (slice(None), sl)])
        a_ref[:, sl] = (_silu(h1) * h3).astype(BF16)
    return _dot(a_ref[...], w2_ref[lead + (slice(None), slice(None))])


FFN_TM = 1024
MIXFFN_TM = 2 * SEG


def _mix_ffn_kernel(sel, ogf_ref, ogb_ref, gg_ref, ys_ref, u_ref, omf_ref, omb_ref, mo_ref, h_ref, mod_ref,
                    gn_ref, mn_ref, sd_ref, gw_ref, gb_ref, wo_ref, n2_ref, w1_ref, w3_ref, w2_ref, o_ref,
                    a_ref, hn_ref, f_ref):
    n_seg = h_ref.shape[0] // SEG
    mods = []
    for q in range(n_seg):
        rs = slice(q * SEG, (q + 1) * SEG)
        m = mod_ref[sel(pl.program_id(0) * n_seg + q)]
        mods.append(m)
        gla = (_head_norm(ogf_ref[rs, :].astype(F32) + ogb_ref[rs, :].astype(F32), gn_ref[...], GLA_DV)
               * _silu(gg_ref[rs, :].astype(F32)))
        z = _gelu_tanh(ys_ref[0, rs, :].astype(F32) + ys_ref[1, rs, :].astype(F32) + sd_ref[...] * u_ref[rs, :])
        s5 = z * jax.nn.sigmoid(_dot(z.astype(BF16), gw_ref[...]) + gb_ref[...])
        ml = (_head_norm(omf_ref[rs, :].astype(F32) + omb_ref[rs, :].astype(F32), mn_ref[...], ML_DH)
              * jax.nn.sigmoid(mo_ref[rs, :].astype(F32)))
        mix = (_dot(gla.astype(BF16), wo_ref[0:HP]) + _dot(s5.astype(BF16), wo_ref[HP:HP + 2 * LANE])
               + _dot(ml.astype(BF16), wo_ref[HP + 2 * LANE:]))
        hn = h_ref[rs, :] + m[2:3] * mix
        hn_ref[rs, :] = hn
        f_ref[rs, :] = (_rmsnorm(hn, n2_ref[...]) * (1.0 + m[4:5]) + m[3:4]).astype(BF16)
    y = _swiglu(f_ref[...], w1_ref, w3_ref, w2_ref, a_ref)
    for q in range(n_seg):
        rs = slice(q * SEG, (q + 1) * SEG)
        o_ref[rs, :] = hn_ref[rs, :] + mods[q][5:6] * y[rs]


def _mix_ffn(ogf, ogb, pg, ys, pu, omf, omb, pm, h, modtab, gn, mn, sd, gw, gb, wo, n2, w1, w3, w2, *, layer, rows):
    r, d = h.shape
    tm = MIXFFN_TM
    assert r % tm == 0
    dff = w1.shape[-1]
    full = lambda a: pl.BlockSpec(a.shape, lambda i: (0,) * a.ndim)
    resident = lambda a: pl.BlockSpec(a.shape, lambda i: (0,) * a.ndim, pipeline_mode=pl.Buffered(1))
    in_specs = [pl.BlockSpec((tm, HP), lambda i: (i, 0)),
                pl.BlockSpec((tm, HP), lambda i: (i, 0)),
                pl.BlockSpec((tm, HP), lambda i: (i, 0)),
                pl.BlockSpec((2, tm, 2 * LANE), lambda i: (0, i, 0)),
                pl.BlockSpec((tm, 2 * LANE), lambda i: (i, 0)),
                pl.BlockSpec((tm, HP), lambda i: (i, 0)),
                pl.BlockSpec((tm, HP), lambda i: (i, 0)),
                pl.BlockSpec((tm, HP), lambda i: (i, 1)),
                pl.BlockSpec((tm, d), lambda i: (i, 0)),
                pl.BlockSpec((None,) + modtab.shape[1:], lambda i: (layer, 0, 0, 0)),
                full(gn), full(mn), full(sd), full(gw), full(gb), resident(wo), full(n2),
                resident(w1), resident(w3), resident(w2)]
    return pl.pallas_call(
        functools.partial(_mix_ffn_kernel, rows.sel(False)),
        out_shape=jax.ShapeDtypeStruct((r, d), F32),
        grid=(r // tm,),
        in_specs=in_specs,
        out_specs=pl.BlockSpec((tm, d), lambda i: (i, 0)),
        scratch_shapes=[pltpu.VMEM((tm, dff), BF16), pltpu.VMEM((tm, d), F32), pltpu.VMEM((tm, d), BF16)],
        compiler_params=_cparams(("arbitrary",)),
        name="mix_ffn",
    )(ogf, ogb, pg, ys, pu, omf, omb, pm, h, modtab, gn, mn, sd, gw, gb, wo, n2, w1, w3, w2)


def _ffn_kernel(final, sel, f_ref, h_ref, mod_ref, w1_ref, w3_ref, w2_ref, *rest):
    if final:
        nf_ref, o_ref, a_ref = rest
    else:
        o_ref, a_ref = rest
    y = _swiglu(f_ref[...], w1_ref, w3_ref, w2_ref, a_ref)
    n_seg = f_ref.shape[0] // SEG
    for q in range(n_seg):
        rs = slice(q * SEG, (q + 1) * SEG)
        gate = mod_ref[sel(pl.program_id(0) * n_seg + q)][5:6]
        hn = h_ref[rs, :] + gate * y[rs]
        o_ref[rs, :] = _rmsnorm(hn, nf_ref[...]) if final else hn


def _ffn(f, h, modtab, w1, w3, w2, nf, *, layer, rows, lat_only):
    r, d = h.shape
    tm = FFN_TM
    assert r % tm == 0
    dff = w1.shape[-1]
    full = lambda a: pl.BlockSpec(a.shape, lambda i: (0,) * a.ndim)
    resident = lambda a: pl.BlockSpec(a.shape, lambda i: (0,) * a.ndim, pipeline_mode=pl.Buffered(1))
    in_specs = [pl.BlockSpec((tm, d), lambda i: (i, 0)),
                pl.BlockSpec((tm, d), lambda i: (i, 0)),
                pl.BlockSpec((None,) + modtab.shape[1:], lambda i: (layer, 0, 0, 0)),
                resident(w1), resident(w3), resident(w2)]
    args = [f, h, modtab, w1, w3, w2]
    if lat_only:
        in_specs.append(full(nf))
        args.append(nf)
    return pl.pallas_call(
        functools.partial(_ffn_kernel, lat_only, rows.sel(lat_only)),
        out_shape=jax.ShapeDtypeStruct((r, d), F32),
        grid=(r // tm,),
        in_specs=in_specs,
        out_specs=pl.BlockSpec((tm, d), lambda i: (i, 0)),
        scratch_shapes=[pltpu.VMEM((tm, dff), BF16)],
        compiler_params=_cparams(("arbitrary",)),
        name="ffn",
    )(*args)


MOE_TM = 512


def _moe_kernel(be_ref, live_ref, x_ref, w1_ref, w3_ref, w2_ref, o_ref, a_ref):
    i = pl.program_id(0)
    d = w1_ref.shape[1]

    @pl.when(live_ref[i] > 0)
    def _():
        xa, xb = _unpack_bf16_pairs(x_ref[:, 0:d // 2])
        x = jnp.concatenate([xa.astype(BF16), xb.astype(BF16)], axis=1)
        y = _swiglu(x, w1_ref, w3_ref, w2_ref, a_ref, lead=(0,))
        tail = x_ref[:, d // 2:]
        mine = tail[:, 0:1] == be_ref[i].astype(F32)
        o_ref[...] = _pack_bf16_pairs(y * jnp.where(mine, tail[:, 2:3], tail[:, 3:4]))

    @pl.when(live_ref[i] == 0)
    def _():
        o_ref[...] = jnp.zeros(o_ref.shape, F32)


def _moe_experts(block_expert, block_live, xg, w1, w3, w2):
    n_rows, dx = xg.shape
    d = w1.shape[-2]
    dff = w1.shape[-1]
    tm = MOE_TM
    return pl.pallas_call(
        _moe_kernel,
        out_shape=jax.ShapeDtypeStruct((n_rows, d // 2), F32),
        grid_spec=pltpu.PrefetchScalarGridSpec(
            num_scalar_prefetch=2,
            grid=(n_rows // tm,),
            in_specs=[pl.BlockSpec((tm, dx), lambda i, be, lv: (i, 0)),
                      pl.BlockSpec((1, d, dff), lambda i, be, lv: (be[i], 0, 0)),
                      pl.BlockSpec((1, d, dff), lambda i, be, lv: (be[i], 0, 0)),
                      pl.BlockSpec((1, dff, d), lambda i, be, lv: (be[i], 0, 0))],
            out_specs=pl.BlockSpec((tm, d // 2), lambda i, be, lv: (i, 0)),
            scratch_shapes=[pltpu.VMEM((tm, dff), BF16)]),
        compiler_params=_cparams(("arbitrary",)),
        name="moe_experts",
    )(block_expert, block_live, xg, w1, w3, w2)


SC_GATHER_ROWS = 64


def _gather_rows(table, idx):
    n_k, n_idx = idx.shape
    _, d = table.shape
    info = plsc.get_sparse_core_info()
    n_cores, n_workers = info.num_cores, info.num_cores * info.num_subcores
    assert n_idx % (n_workers * SC_GATHER_ROWS) == 0
    per_worker = n_idx // n_workers
    mesh = plsc.VectorSubcoreMesh(core_axis_name="c", subcore_axis_name="s")

    @functools.partial(
        pl.kernel, mesh=mesh,
        out_type=jax.ShapeDtypeStruct((n_k * n_idx, d), table.dtype),
        scratch_types=[pltpu.VMEM((SC_GATHER_ROWS,), jnp.int32),
                       pltpu.VMEM((SC_GATHER_ROWS, d), table.dtype),
                       pltpu.SemaphoreType.DMA])
    def gather(table_hbm, idx_hbm, out_hbm, idx_v, rows_v, sem):
        base = (lax.axis_index("s") * n_cores + lax.axis_index("c")) * per_worker

        @pl.loop(0, per_worker // SC_GATHER_ROWS)
        def _(it):
            for k in range(n_k):
                off = pl.multiple_of(k * n_idx + base + it * SC_GATHER_ROWS, SC_GATHER_ROWS)
                pltpu.sync_copy(idx_hbm.at[pl.ds(off, SC_GATHER_ROWS)], idx_v)
                pltpu.async_copy(table_hbm.at[idx_v], rows_v, sem).wait()
                pltpu.sync_copy(rows_v, out_hbm.at[pl.ds(off, SC_GATHER_ROWS)])

    return gather(table, idx.reshape(-1)).reshape(n_k, n_idx, d)


def _scatter_rows(src, dest, n_rows):
    n_tok, d = src.shape
    info = plsc.get_sparse_core_info()
    n_cores, n_workers = info.num_cores, info.num_cores * info.num_subcores
    assert n_tok % (n_workers * SC_GATHER_ROWS) == 0
    per_worker = n_tok // n_workers
    mesh = plsc.VectorSubcoreMesh(core_axis_name="c", subcore_axis_name="s")

    @functools.partial(
        pl.kernel, mesh=mesh,
        out_type=jax.ShapeDtypeStruct((n_rows, d), src.dtype),
        scratch_types=[pltpu.VMEM((SC_GATHER_ROWS,), jnp.int32),
                       pltpu.VMEM((SC_GATHER_ROWS, d), src.dtype),
                       pltpu.SemaphoreType.DMA])
    def scatter(src_hbm, dest_hbm, out_hbm, idx_v, rows_v, sem):
        base = (lax.axis_index("s") * n_cores + lax.axis_index("c")) * per_worker

        @pl.loop(0, per_worker // SC_GATHER_ROWS)
        def _(it):
            off = pl.multiple_of(base + it * SC_GATHER_ROWS, SC_GATHER_ROWS)
            pltpu.sync_copy(src_hbm.at[pl.ds(off, SC_GATHER_ROWS)], rows_v)
            for k in range(TOP_K):
                pltpu.sync_copy(dest_hbm.at[pl.ds(k * n_tok + off, SC_GATHER_ROWS)], idx_v)
                pltpu.async_copy(rows_v, out_hbm.at[idx_v], sem).wait()

    return scatter(src, dest.reshape(-1))


def _pack_bf16_pairs(y):
    n = y.shape[1] // 2
    hi = pltpu.bitcast(y[:, :n].astype(BF16).astype(F32), jnp.uint32)
    lo = pltpu.bitcast(y[:, n:].astype(BF16).astype(F32), jnp.uint32)
    return pltpu.bitcast(hi | (lo >> 16), F32)


def _unpack_bf16_pairs(w):
    u = pltpu.bitcast(w, jnp.uint32)
    return pltpu.bitcast(u & jnp.uint32(0xFFFF0000), F32), pltpu.bitcast(u << 16, F32)


def _resid_kernel(final, h_ref, y_ref, mod_ref, *rest):
    a0, b0 = _unpack_bf16_pairs(y_ref[0])
    a1, b1 = _unpack_bf16_pairs(y_ref[1])
    hn = h_ref[...] + mod_ref[0][5:6] * jnp.concatenate([a0 + a1, b0 + b1], axis=1)
    if final:
        nf_ref, o_ref = rest
        o_ref[...] = _rmsnorm(hn, nf_ref[...])
    else:
        (o_ref,) = rest
        o_ref[...] = hn


def _moe_resid(h, y01, modtab, nf, *, layer, rows, lat_only):
    r, d = h.shape
    tm = SEG
    sel = rows.sel(lat_only)
    row = lambda i: (i, 0)
    in_specs = [pl.BlockSpec((tm, d), row), pl.BlockSpec((TOP_K, tm, d // 2), lambda i: (0, i, 0)),
                pl.BlockSpec((None, 1, 8, d), lambda i: (layer, sel(i), 0, 0))]
    args = [h, y01, modtab]
    if lat_only:
        in_specs.append(pl.BlockSpec((1, d), lambda i: (0, 0)))
        args.append(nf)
    return pl.pallas_call(
        functools.partial(_resid_kernel, lat_only),
        out_shape=jax.ShapeDtypeStruct((r, d), F32),
        grid=(r // tm,),
        in_specs=in_specs,
        out_specs=pl.BlockSpec((tm, d), row),
        compiler_params=_cparams(("arbitrary",)),
        name="moe_resid",
    )(*args)


CAST_PARTS = 8


def _cast_kernel(*refs):
    o_ref = refs[-1]
    tr = refs[0].shape[1]
    for k, x_ref in enumerate(refs[:-1]):
        o_ref[0, k * tr:(k + 1) * tr, :] = x_ref[0].astype(o_ref.dtype)


def _to_bf16(w, j):
    lead, (r, c) = w.shape[1:-2], w.shape[-2:]
    n = int(np.prod(lead, dtype=np.int64))
    w3 = w.reshape((-1, r, c))
    tr = r // CAST_PARTS
    assert tr * CAST_PARTS == r and tr % 16 == 0
    band = lambda k: pl.BlockSpec((1, tr, c), lambda e: (j * n + e, k, 0))
    out = pl.pallas_call(
        _cast_kernel,
        out_shape=jax.ShapeDtypeStruct((n, r, c), BF16),
        grid=(n,),
        in_specs=[band(k) for k in range(CAST_PARTS)],
        out_specs=pl.BlockSpec((1, r, c), lambda e: (e, 0, 0)),
        compiler_params=_cparams(("arbitrary",)),
        name="to_bf16",
    )(*([w3] * CAST_PARTS))
    return out.reshape(lead + (r, c))


def _pad_heads(w, heads, dim, to=LANE):
    lead = w.shape[:-1]
    w = w.reshape(lead + (heads, dim))
    w = jnp.pad(w, [(0, 0)] * len(lead) + [(0, 0), (0, to - dim)])
    return w.reshape(lead + (heads * to,))


def _pad_last(w, to):
    return jnp.pad(w, [(0, 0)] * (w.ndim - 1) + [(0, to - w.shape[-1])])


def _pos_embed(n_tokens, d):
    n_grid_rows = n_tokens // GRID_W
    row, col = jnp.meshgrid(jnp.arange(n_grid_rows, dtype=F32), jnp.arange(GRID_W, dtype=F32), indexing='ij')
    n_freq = d // 4
    omega = jnp.exp(-math.log(POS_BASE) * jnp.arange(n_freq, dtype=F32) / n_freq)

    def axis_embed(p):
        ang = p.reshape(-1, 1) * omega
        return jnp.concatenate([jnp.sin(ang), jnp.cos(ang)], axis=-1)

    return jnp.concatenate([axis_embed(row), axis_embed(col)], axis=-1)


def _s5_discretise(lam_re, lam_im, log_dt, b_re, b_im):
    dt = jnp.exp(log_dt)[:, None]
    mag = jnp.exp(lam_re * dt)
    abar_re, abar_im = mag * jnp.cos(lam_im * dt), mag * jnp.sin(lam_im * dt)
    den = lam_re * lam_re + lam_im * lam_im
    pr, pi = abar_re - 1.0, abar_im
    coef_re = (pr * lam_re + pi * lam_im) / den
    coef_im = (pi * lam_re - pr * lam_im) / den
    bbar_re = coef_re[..., None] * b_re - coef_im[..., None] * b_im
    bbar_im = coef_re[..., None] * b_im + coef_im[..., None] * b_re
    return abar_re, abar_im, bbar_re, bbar_im


def _block_diag(m):
    g, a, b = m.shape
    eye = jnp.eye(g, dtype=m.dtype)
    return (eye[:, None, :, None] * m[:, :, None, :]).reshape(g * a, g * b)


def _route_plan(route, tm):
    n_tok = route.shape[1]
    n_assign = n_tok * TOP_K
    flat_e = route[0:TOP_K].astype(jnp.int32).reshape(-1)
    onehot = (jnp.arange(N_EXPERTS, dtype=jnp.int32)[:, None] == flat_e[None, :]).astype(jnp.int32)
    csum = jnp.cumsum(onehot, axis=1)
    counts = csum[:, -1]
    padded = (counts + tm - 1) // tm * tm
    pend = jnp.cumsum(padded)
    pstart = pend - padded
    dest = jnp.sum(onehot * (csum - 1 + pstart[:, None]), axis=0)
    n_blocks = -(-n_assign // tm) + N_EXPERTS
    block_start = jnp.arange(n_blocks, dtype=jnp.int32) * tm
    block_expert = jnp.minimum(jnp.searchsorted(pend, block_start, side='right'), N_EXPERTS - 1).astype(jnp.int32)
    block_live = (block_start < (pstart + counts)[block_expert]).astype(jnp.int32)
    return n_blocks * tm, block_expert, block_live, dest.reshape(TOP_K, n_tok)


def kernel(x, c, ctx, c_ctx, w_ada, b_ada, norm1, norm2, w_in, w_out, gla_wa2, gla_ba, gla_norm, s5_lam_re, s5_lam_im, s5_log_dt, s5_b_re, s5_b_im, s5_c_re, s5_c_im, s5_d, s5_glu_w, s5_glu_b, ml_conv_w, ml_conv_b, ml_gate_b, ml_norm, ffn_w1, ffn_w3, ffn_w2, moe_router, moe_w1, moe_w3, moe_w2, norm_f):
    nb, n_lat, d = x.shape
    lc = ctx.shape[1]
    depth = w_ada.shape[0]
    l = lc + n_lat
    assert lc % SEG == 0 and n_lat % SEG == 0 and nb == 8 and nb % SCAN_BATCHES == 0
    rows = _Rows(nb, lc // SEG, l // SEG)

    h = _embed(ctx.reshape(nb * lc, d), x.reshape(nb * n_lat, d), _pos_embed(n_lat, d), rows)

    cond = jnp.zeros((16, d), F32).at[:nb].set(c).at[nb].set(c_ctx)
    mod = _modulation(cond, w_ada, b_ada)
    modtab = jnp.pad(mod.reshape(depth, 16, 6, d), ((0, 0), (0, 0), (0, 2), (0, 0)))

    dk, dv, dh = GLA_HEADS * GLA_DK, GLA_HEADS * GLA_DV, ML_HEADS * ML_DH
    s5c = s5_d.shape[-1]
    cuts = np.cumsum([dk, dk, dv, GLA_RANK, dv, s5c, dh, dh, dh, dh, 4 * ML_HEADS])

    for i in range(depth):
        last = i == depth - 1
        gq, gk, gv, glr, gg, su, mq, mk, mv, mo, mg = jnp.split(w_in[i], cuts[:-1], axis=-1)
        w_all = jnp.concatenate([
            _pad_heads(gq, GLA_HEADS, GLA_DK, GLA_KP), _pad_heads(gk, GLA_HEADS, GLA_DK, GLA_KP),
            _pad_heads(gv, GLA_HEADS, GLA_DV), _pad_last(glr, LANE),
            su,
            _pad_heads(mv, ML_HEADS, ML_DH),
            _pad_last(mg[:, :2 * ML_HEADS], LANE), _pad_last(mg[:, 2 * ML_HEADS:], LANE),
            _pad_heads(mq, ML_HEADS, ML_DH), _pad_heads(mk, ML_HEADS, ML_DH),
            _pad_heads(gg, GLA_HEADS, GLA_DV), _pad_heads(mo, ML_HEADS, ML_DH)],
            axis=-1).astype(BF16)
        wgt = mg.T.astype(BF16)
        cw = jnp.concatenate([_pad_heads(ml_conv_w[i][:, :dh], ML_HEADS, ML_DH),
                              _pad_heads(ml_conv_w[i][:, dh:], ML_HEADS, ML_DH)], axis=-1)
        cw = jnp.pad(cw, ((0, 8 - ML_CONV), (0, 0)))
        cb = jnp.concatenate([_pad_heads(ml_conv_b[i][:dh], ML_HEADS, ML_DH),
                              _pad_heads(ml_conv_b[i][dh:], ML_HEADS, ML_DH)])[None]
        post = jnp.concatenate([jnp.ones((HP,), F32), jnp.full((HP,), ML_DH ** -0.5, F32)])[None]
        pg, pu, pm, qk, go, gt = _proj(h, modtab, norm1[i][None], w_all, wgt, cw, cb, post, layer=i, rows=rows)

        wa = jnp.pad(_pad_heads(gla_wa2[i], GLA_HEADS, GLA_DK, GLA_KP), ((0, 0), (0, LANE - GLA_RANK), (0, 0)))
        ba = _pad_heads(gla_ba[i], GLA_HEADS, GLA_DK, GLA_KP)[:, None, :]
        pg3 = pg.reshape(nb, l, NG)
        ogf = _gla(pg3, wa[0], ba[0], rows=rows, rev=False).reshape(nb * l, HP)
        ogb = _gla(pg3, wa[1], ba[1], rows=rows, rev=True).reshape(nb * l, HP)

        bres, bims, ares, aims = [], [], [], []
        for dr in (0, 1):
            a_re, a_im, b_re, b_im = _s5_discretise(s5_lam_re[i, dr], s5_lam_im[i, dr], s5_log_dt[i, dr],
                                                    s5_b_re[i], s5_b_im[i])
            bres.append(_block_diag(jnp.swapaxes(b_re, 1, 2)))
            bims.append(_block_diag(jnp.swapaxes(b_im, 1, 2)))
            ares.append(jnp.broadcast_to(a_re.reshape(1, -1), (nb, a_re.size)))
            aims.append(jnp.broadcast_to(a_im.reshape(1, -1), (nb, a_im.size)))
        cre = _block_diag(jnp.swapaxes(s5_c_re[i], 1, 2)).astype(BF16)
        cim = _block_diag(jnp.swapaxes(s5_c_im[i], 1, 2)).astype(BF16)
        ut = pu.reshape(nb, l, s5c).swapaxes(0, 1).reshape(l * nb, s5c)
        yt = _s5(ut, jnp.stack(bres).astype(BF16), jnp.stack(bims).astype(BF16), jnp.stack(ares), jnp.stack(aims),
                 cre, cim, nb=nb, nc=lc // S5_STEPS, nt=l // S5_STEPS)
        ys = yt.reshape(2, l, nb, s5c).swapaxes(1, 2).reshape(2, nb * l, s5c)

        gb = ml_gate_b[i].reshape(2, 2, ML_HEADS)
        gbr = _pad_last(gb.reshape(2, 1, 2 * ML_HEADS), LANE)
        gbt = _pad_last(jnp.broadcast_to(gb[..., None], (2, 2, ML_HEADS, CHUNK)), LANE).reshape(2, 2, HP)
        gtl = _pad_last(gt.reshape(2, 2, ML_HEADS, nb, l // CHUNK, CHUNK).transpose(3, 4, 0, 1, 2, 5),
                        LANE).reshape(nb, l // CHUNK, 2, 2, HP)
        qk3, pm3 = qk.reshape(nb, l, 2 * HP), pm.reshape(nb, l, NM)
        omf = _mlstm(qk3, pm3, gtl, gbr[0], gbt[0], rows=rows, rev=False).reshape(nb * l, HP)
        omb = _mlstm(qk3, pm3, gtl, gbr[1], gbt[1], rows=rows, rev=True).reshape(nb * l, HP)

        wo = w_out[i]
        wo_p = jnp.concatenate([
            jnp.pad(wo[:dv].reshape(GLA_HEADS, GLA_DV, d), ((0, 0), (0, LANE - GLA_DV), (0, 0))).reshape(HP, d),
            wo[dv:dv + s5c],
            jnp.pad(wo[dv + s5c:].reshape(ML_HEADS, ML_DH, d), ((0, 0), (0, LANE - ML_DH), (0, 0))).reshape(HP, d)],
            axis=0).astype(BF16)
        gn = jnp.tile(_pad_last(gla_norm[i], LANE), GLA_HEADS)[None]
        mn = jnp.tile(_pad_last(ml_norm[i], LANE), ML_HEADS)[None]
        is_moe = i % 2 == 1
        j = i // 2
        wr = jnp.stack(_split_bf16(_pad_last(moe_router[j], LANE), 2)) if is_moe else None
        mix_args = (ogf, ogb, go, ys, pu, omf, omb, go, h, modtab, gn, mn, s5_d[i][None], s5_glu_w[i].astype(BF16),
                    s5_glu_b[i][None], wo_p, norm2[i][None])
        if not is_moe and not last:
            h = _mix_ffn(*mix_args, _to_bf16(ffn_w1, j), _to_bf16(ffn_w3, j), _to_bf16(ffn_w2, j),
                         layer=i, rows=rows)
            continue
        outs = _mix(*mix_args, wr, layer=i, rows=rows, lat_only=last)
        if not is_moe:
            h, f = outs
            h = _ffn(f, h, modtab, _to_bf16(ffn_w1, j), _to_bf16(ffn_w3, j), _to_bf16(ffn_w2, j),
                     norm_f[None], layer=i, rows=rows, lat_only=last)
        else:
            h, f, route = outs
            n_rows, block_expert, block_live, dest = _route_plan(route, MOE_TM)
            xg = _scatter_rows(f, dest, n_rows)
            yg = _moe_experts(block_expert, block_live, xg, _to_bf16(moe_w1, j), _to_bf16(moe_w3, j),
                              _to_bf16(moe_w2, j))
            y01 = _gather_rows(yg, dest)
            h = _moe_resid(h, y01, modtab, norm_f[None], layer=i, rows=rows, lat_only=last)
    return h.reshape(nb, n_lat, d)
```

```python
import functools
import math

import numpy as np
import jax
import jax.numpy as jnp
from jax import lax
from jax.experimental import pallas as pl
from jax.experimental.pallas import tpu as pltpu
from jax.experimental.pallas import tpu_sc as plsc

F32 = jnp.float32
BF16 = jnp.bfloat16
HIGHEST = lax.Precision.HIGHEST

GRID_W = 64
POS_BASE = 10000.0
EPS = 1e-6
GLA_HEADS, GLA_DK, GLA_DV, GLA_RANK, GLA_GATE_NORM = 4, 48, 96, 16, 16.0
S5_GROUP, S5_STATE = 16, 64
ML_HEADS, ML_DH, ML_CONV = 4, 96, 3
N_EXPERTS, TOP_K = 8, 2

LANE = 128
CHUNK = 64
SEG = 256
N_SUB = SEG // CHUNK
SCAN_BATCHES = 8
S5_STEPS = 128
S5_SUB = 32
NEG = -1e30
V7X_VMEM_BYTES = 64 * 1024 * 1024
VMEM_LIMIT = V7X_VMEM_BYTES - 8 * 1024 * 1024

HP = LANE * GLA_HEADS
GLA_KP = 64
GQ = GLA_HEADS * GLA_KP
NG = 2 * GQ + HP + LANE
NM = HP + 2 * LANE
NGO = 2 * HP


def _cparams(sem):
    return pltpu.CompilerParams(dimension_semantics=sem, vmem_limit_bytes=VMEM_LIMIT)


def _dot(a, b, **kw):
    return jnp.dot(a, b, preferred_element_type=F32, **kw)


def _dot_nt(a, b, **kw):
    return lax.dot_general(a, b, (((1,), (1,)), ((), ())), preferred_element_type=F32, **kw)


def _dot_tn(a, b, **kw):
    return lax.dot_general(a, b, (((0,), (0,)), ((), ())), preferred_element_type=F32, **kw)


def _log_sigmoid(x):
    return jnp.minimum(x, 0.0) - jnp.log1p(jnp.exp(-jnp.abs(x)))


def _silu(x):
    return x * jax.nn.sigmoid(x)


def _gelu_tanh(x):
    return 0.5 * x * (1.0 + jnp.tanh(math.sqrt(2.0 / math.pi) * (x + 0.044715 * (x * x * x))))


def _rmsnorm(x, g):
    return x * lax.rsqrt(jnp.mean(x * x, axis=-1, keepdims=True) + EPS) * g


class _Rows:
    def __init__(self, nb, ncb, ntb):
        self.nb, self.ncb, self.ntb, self.nlb = nb, ncb, ntb, ntb - ncb

    def n_blocks(self, lat_only):
        return self.nb * (self.nlb if lat_only else self.ntb)

    def src(self, lat_only):
        if lat_only:
            return lambda i: (i // self.nlb) * self.ntb + self.ncb + i % self.nlb
        return lambda i: i

    def sel(self, lat_only):
        if lat_only:
            return lambda i: i // self.nlb
        return lambda i: jnp.where(i % self.ntb < self.ncb, self.nb, i // self.ntb)


def _embed_kernel(ncb, ntb, ctx_ref, x_ref, pos_ref, o_ref):
    j = pl.program_id(0) % ntb

    @pl.when(j < ncb)
    def _():
        o_ref[...] = ctx_ref[...]

    @pl.when(j >= ncb)
    def _():
        o_ref[...] = x_ref[...] + pos_ref[...]


def _embed(ctx2, x2, pos, rows):
    d = ctx2.shape[1]
    ncb, ntb, nlb = rows.ncb, rows.ntb, rows.nlb
    return pl.pallas_call(
        functools.partial(_embed_kernel, ncb, ntb),
        out_shape=jax.ShapeDtypeStruct((rows.nb * ntb * SEG, d), F32),
        grid=(rows.nb * ntb,),
        in_specs=[pl.BlockSpec((SEG, d), lambda i: ((i // ntb) * ncb + jnp.minimum(i % ntb, ncb - 1), 0)),
                  pl.BlockSpec((SEG, d), lambda i: ((i // ntb) * nlb + jnp.maximum(i % ntb - ncb, 0), 0)),
                  pl.BlockSpec((SEG, d), lambda i: (jnp.maximum(i % ntb - ncb, 0), 0))],
        out_specs=pl.BlockSpec((SEG, d), lambda i: (i, 0)),
        compiler_params=_cparams(("arbitrary",)),
        name="embed",
    )(ctx2, x2, pos)


def _mod_kernel(c_ref, w_ref, b_ref, o_ref):
    s = _silu(c_ref[...])
    o_ref[0] = _dot(s, w_ref[0], precision=HIGHEST) + b_ref[0]


def _modulation(cond, w_ada, b_ada):
    depth, d, n6 = w_ada.shape
    tn = n6 // 4
    n_rows = cond.shape[0]
    return pl.pallas_call(
        _mod_kernel,
        out_shape=jax.ShapeDtypeStruct((depth, n_rows, n6), F32),
        grid=(depth, n6 // tn),
        in_specs=[pl.BlockSpec((n_rows, d), lambda l, j: (0, 0)),
                  pl.BlockSpec((1, d, tn), lambda l, j: (l, 0, j)),
                  pl.BlockSpec((1, 1, tn), lambda l, j: (l, 0, j))],
        out_specs=pl.BlockSpec((1, n_rows, tn), lambda l, j: (l, 0, j)),
        compiler_params=_cparams(("arbitrary", "arbitrary")),
        name="modulation",
    )(cond, w_ada, b_ada.reshape(depth, 1, n6))


PROJ_TM = 2 * SEG


def _proj_kernel(ncb, ntb, sel, h_ref, hp_ref, hn_ref, mod_ref, g_ref, w_ref, wgt_ref, cw_ref, cb_ref, post_ref,
                 pg_ref, pu_ref, pm_ref, qk_ref, go_ref, gt_ref):
    tm = h_ref.shape[0]
    n_seg = tm // SEG
    blk = [pl.program_id(0) * n_seg + q for q in range(n_seg)]
    mods = [mod_ref[sel(bq)] for bq in blk]
    act = lambda x, m: _rmsnorm(x, g_ref[...]) * (1.0 + m[1:2]) + m[0:1]
    a_seg = [act(h_ref[q * SEG:(q + 1) * SEG, :], mods[q]) for q in range(n_seg)]
    ab = jnp.concatenate(a_seg, axis=0).astype(BF16)
    c1, c2 = NG + 2 * LANE, NG + 2 * LANE + NM
    c3 = c2 + 2 * HP
    go_ref[...] = _dot(ab, w_ref[:, c3:]).astype(go_ref.dtype)
    pg_ref[...] = _dot(ab, w_ref[:, 0:NG])
    pu_ref[...] = _dot(ab, w_ref[:, NG:c1])
    pm_ref[...] = _dot(ab, w_ref[:, c1:c2])
    gt_ref[...] = _dot_nt(wgt_ref[...], ab)
    ae = jnp.concatenate([act(hp_ref[...], mods[0])] + a_seg + [act(hn_ref[...], mods[-1])], axis=0)
    xe = _dot(ae.astype(BF16), w_ref[:, c2:c3])
    n_e = tm + 16
    row = lax.broadcasted_iota(jnp.int32, (tm, xe.shape[1]), 0)
    keep_prev = jnp.ones((tm, xe.shape[1]), F32)
    keep_next = keep_prev
    for q, bq in enumerate(blk):
        j = bq % ntb
        first = jnp.logical_or(j == 0, j == ncb).astype(F32)
        last = jnp.logical_or(j == ncb - 1, j == ntb - 1).astype(F32)
        keep_prev = jnp.where(row == q * SEG, 1.0 - first, keep_prev)
        keep_next = jnp.where(row == (q + 1) * SEG - 1, 1.0 - last, keep_next)
    xp = pltpu.roll(xe, 1, axis=0)[8:8 + tm] * keep_prev
    xn = pltpu.roll(xe, n_e - 1, axis=0)[8:8 + tm] * keep_next
    y = cw_ref[0:1] * xp + cw_ref[1:2] * xe[8:8 + tm] + cw_ref[2:3] * xn + cb_ref[...]
    qk_ref[...] = _silu(y) * post_ref[...]


def _proj(h, modtab, g, w, wgt, cw, cb, post, *, layer, rows):
    r, d = h.shape
    tm = PROJ_TM
    assert r % tm == 0
    t8 = tm // 8
    full = lambda a: pl.BlockSpec(a.shape, lambda i: (0,) * a.ndim)
    return pl.pallas_call(
        functools.partial(_proj_kernel, rows.ncb, rows.ntb, rows.sel(False)),
        out_shape=(jax.ShapeDtypeStruct((r, NG), F32), jax.ShapeDtypeStruct((r, 2 * LANE), F32),
                   jax.ShapeDtypeStruct((r, NM), F32), jax.ShapeDtypeStruct((r, 2 * HP), F32),
                   jax.ShapeDtypeStruct((r, NGO), BF16),
                   jax.ShapeDtypeStruct((16, r), F32)),
        grid=(r // tm,),
        in_specs=[pl.BlockSpec((tm, d), lambda i: (i, 0)),
                  pl.BlockSpec((8, d), lambda i: (jnp.maximum(i * t8 - 1, 0), 0)),
                  pl.BlockSpec((8, d), lambda i: (jnp.minimum((i + 1) * t8, r // 8 - 1), 0)),
                  pl.BlockSpec((None,) + modtab.shape[1:], lambda i: (layer, 0, 0, 0)),
                  full(g), full(w), full(wgt), full(cw), full(cb), full(post)],
        out_specs=(pl.BlockSpec((tm, NG), lambda i: (i, 0)),
                   pl.BlockSpec((tm, 2 * LANE), lambda i: (i, 0)),
                   pl.BlockSpec((tm, NM), lambda i: (i, 0)),
                   pl.BlockSpec((tm, 2 * HP), lambda i: (i, 0)),
                   pl.BlockSpec((tm, NGO), lambda i: (i, 0)),
                   pl.BlockSpec((16, tm), lambda i: (0, i))),
        compiler_params=_cparams(("arbitrary",)),
        name="proj",
    )(h, h, h, modtab, g, w, wgt, cw, cb, post)


def _scan_pos(d, s, ncb, ntb):
    rev = jnp.where(s < ncb, ncb - 1 - s, ntb - 1 - (s - ncb))
    return jnp.where(d == 0, s, rev)


def _scan_pos_static(rev, s, ncb, ntb):
    if not rev:
        return s
    return jnp.where(s < ncb, ncb - 1 - s, ntb - 1 - (s - ncb))


def _tri(rev):
    r = lax.broadcasted_iota(jnp.int32, (CHUNK, CHUNK), 0)
    c = lax.broadcasted_iota(jnp.int32, (CHUNK, CHUNK), 1)
    return (r <= c) if rev else (r >= c)


def _chunk_rows(rev):
    return [(N_SUB - 1 - j if rev else j) * CHUNK for j in range(N_SUB)]


def _gla_kernel(rev, p_ref, wa_ref, ba_ref, o_ref, st_ref):
    s, g = pl.program_id(0), pl.program_id(1)
    nbb = p_ref.shape[0]
    b0 = g * nbb

    @pl.when(s == 0)
    def _():
        st_ref[pl.ds(b0, nbb)] = jnp.zeros((nbb,) + st_ref.shape[1:], F32)

    valid = _tri(rev)
    tri = valid.astype(F32)
    r0s = _chunk_rows(rev)
    wa, ba = wa_ref[...], ba_ref[...]
    inst = [(bb, j) for bb in range(nbb) for j in range(N_SUB)]
    heads = [slice(h * LANE, (h + 1) * LANE) for h in range(GLA_HEADS)]
    pairs = [slice((h // 2) * LANE, (h // 2 + 1) * LANE) for h in range(GLA_HEADS)]
    lane = lax.broadcasted_iota(jnp.int32, (CHUNK, LANE), 1)
    own = [(lane // GLA_KP) == (h % 2) for h in range(GLA_HEADS)]

    la = {}
    for bb, j in inst:
        lr = p_ref[bb, pl.ds(r0s[j], CHUNK), 2 * GQ + HP:NG]
        la[bb, j] = _log_sigmoid(_dot(lr, wa) + ba) * (1.0 / GLA_GATE_NORM)
    bc, e_last = {}, {}
    for i in inst:
        bc[i] = _dot_exact01(tri, la[i], lhs_is_01=True, pieces=2)
        e_last[i] = jnp.exp(jnp.sum(la[i], axis=0, keepdims=True))
    q_in, k_in, k_out, v = {}, {}, {}, {}
    for bb, j in inst:
        i = (bb, j)
        rs = pl.ds(r0s[j], CHUNK)
        qs = (p_ref[bb, rs, 0:GQ] * (GLA_DK ** -0.5) * jnp.exp(bc[i])).astype(BF16)
        for h in range(GLA_HEADS):
            q_in[i, h] = jnp.where(own[h], qs[:, pairs[h]], jnp.zeros_like(qs[:, pairs[h]]))
        kd = p_ref[bb, rs, GQ:2 * GQ] * jnp.exp(-bc[i])
        k_out[i] = (kd * e_last[i]).astype(BF16)
        k_in[i] = kd.astype(BF16)
        v[i] = p_ref[bb, rs, 2 * GQ:2 * GQ + HP].astype(BF16)
    att = {}
    for i in inst:
        for h in range(GLA_HEADS):
            att[i, h] = jnp.where(valid, _dot_nt(q_in[i, h], k_in[i][:, pairs[h]]), 0.0).astype(BF16)
    o_intra, ds = {}, {}
    for i in inst:
        for h, sl in enumerate(heads):
            o_intra[i, h] = _dot(att[i, h], v[i][:, sl])
            ds[i, h] = _dot_tn(v[i][:, sl], k_out[i][:, pairs[h]])
    s_in = {}
    for bb in range(nbb):
        for h in range(GLA_HEADS):
            st = st_ref[b0 + bb, h]
            for j in range(N_SUB):
                s_in[(bb, j), h] = st.astype(BF16)
                st = st * e_last[bb, j][:, pairs[h]] + ds[(bb, j), h]
            st_ref[b0 + bb, h] = st
    for bb, j in inst:
        for h, sl in enumerate(heads):
            o = o_intra[(bb, j), h] + _dot_nt(q_in[(bb, j), h], s_in[(bb, j), h])
            o_ref[bb, pl.ds(r0s[j], CHUNK), sl] = o.astype(o_ref.dtype)


def _gla(pg3, wa, ba, *, rows, rev):
    nb, l, _ = pg3.shape
    nbb = SCAN_BATCHES
    pos = functools.partial(_scan_pos_static, rev, ncb=rows.ncb, ntb=rows.ntb)
    return pl.pallas_call(
        functools.partial(_gla_kernel, rev),
        out_shape=jax.ShapeDtypeStruct((nb, l, HP), BF16),
        grid=(rows.ntb, nb // nbb),
        in_specs=[pl.BlockSpec((nbb, SEG, NG), lambda s, g: (g, pos(s), 0)),
                  pl.BlockSpec((LANE, GQ), lambda s, g: (0, 0)),
                  pl.BlockSpec((1, GQ), lambda s, g: (0, 0))],
        out_specs=pl.BlockSpec((nbb, SEG, HP), lambda s, g: (g, pos(s), 0)),
        scratch_shapes=[pltpu.VMEM((nb, GLA_HEADS, LANE, LANE), F32)],
        compiler_params=_cparams(("arbitrary", "arbitrary")),
        name="gla_scan_bwd" if rev else "gla_scan_fwd",
    )(pg3, wa, ba)


def _split_bf16(x, n):
    parts, r = [], x
    for _ in range(n):
        p = r.astype(BF16)
        parts.append(p)
        r = r - p.astype(F32)
    return parts


def _dot_exact01(a, b, lhs_is_01, pieces=3):
    if lhs_is_01:
        a = a.astype(BF16)
        terms = [_dot(a, p) for p in _split_bf16(b, pieces)]
    else:
        b = b.astype(BF16)
        terms = [_dot(p, b) for p in _split_bf16(a, pieces)]
    return functools.reduce(lambda x, y: x + y, terms)


def _cummax_rows(a, rev):
    n = a.shape[0]
    row = lax.broadcasted_iota(jnp.int32, a.shape, 0)
    k = 1
    while k < n:
        if rev:
            sh = jnp.where(row < n - k, pltpu.roll(a, n - k, axis=0), NEG)
        else:
            sh = jnp.where(row >= k, pltpu.roll(a, k, axis=0), NEG)
        a = jnp.maximum(a, sh)
        k *= 2
    return a


ML_GL = ML_HEADS


def _mlstm_kernel(rev, qk_ref, v_ref, g_ref, gt_ref, gbr_ref, gbt_ref, o_ref, st_ref, m_ref):
    s, g = pl.program_id(0), pl.program_id(1)
    nbb = qk_ref.shape[0]
    b0 = g * nbb

    @pl.when(s == 0)
    def _():
        st_ref[pl.ds(b0, nbb)] = jnp.zeros((nbb,) + st_ref.shape[1:], F32)
        m_ref[pl.ds(b0, nbb)] = jnp.zeros((nbb,) + m_ref.shape[1:], F32)

    valid = _tri(rev)
    tri = valid.astype(F32)
    r0s = _chunk_rows(rev)
    cs = [r // CHUNK for r in r0s]
    last = 0 if rev else CHUNK - 1
    inst = [(bb, j) for bb in range(nbb) for j in range(N_SUB)]
    heads = [slice(h * LANE, (h + 1) * LANE) for h in range(ML_HEADS)]

    r_sel = lax.broadcasted_iota(jnp.int32, (LANE, HP), 0)
    c_sel = lax.broadcasted_iota(jnp.int32, (LANE, HP), 1)
    sel_h = (r_sel == ML_GL + c_sel // LANE).astype(BF16)
    r_t = lax.broadcasted_iota(jnp.int32, (HP, HP), 0)
    c_t = lax.broadcasted_iota(jnp.int32, (HP, HP), 1)
    same = jnp.logical_and(r_t // LANE == c_t // LANE, jnp.logical_and(r_t % LANE < CHUNK, c_t % LANE < CHUNK))
    before = (r_t % LANE >= c_t % LANE) if rev else (r_t % LANE <= c_t % LANE)
    tri_b = jnp.logical_and(same, before).astype(BF16)
    r_v = lax.broadcasted_iota(jnp.int32, (CHUNK, HP), 0)
    c_v = lax.broadcasted_iota(jnp.int32, (CHUNK, HP), 1) % LANE
    valid4 = jnp.logical_and(c_v < CHUNK, (r_v <= c_v) if rev else (r_v >= c_v))
    lane4 = lax.broadcasted_iota(jnp.int32, (CHUNK, HP), 1) % LANE
    lane_c = lax.broadcasted_iota(jnp.int32, (CHUNK, LANE), 1)
    lane1 = lax.broadcasted_iota(jnp.int32, (1, LANE), 1)
    head_lane = jnp.logical_and(lane1 >= ML_GL, lane1 < ML_GL + ML_HEADS)
    gbr, gbt = gbr_ref[...], gbt_ref[...]

    gcs, fcm, cmx, a_row, grt = {}, {}, {}, {}, {}
    for bb, j in inst:
        gc = g_ref[bb, pl.ds(r0s[j], CHUNK), :] + gbr
        gcs[bb, j] = pltpu.roll(gc, ML_GL, axis=1)
        fcm[bb, j] = _dot_exact01(tri, _log_sigmoid(gc), lhs_is_01=True)
        grt[bb, j] = gt_ref[bb, cs[j]] + gbt
    row_id = lax.broadcasted_iota(jnp.int32, (len(inst), HP), 0)
    lfr = jnp.zeros((len(inst), HP), F32)
    for n, i in enumerate(inst):
        lfr = jnp.where(row_id == n, _log_sigmoid(grt[i][1:2]), lfr)
    fcr = _dot_exact01(lfr, tri_b, lhs_is_01=False)
    for n, i in enumerate(inst):
        a_row[i] = grt[i][0:1] - fcr[n:n + 1]
        cmx[i] = _cummax_rows(gcs[i] - fcm[i], rev)
    bx = {}
    e_neg, gd = {}, {}
    for bb in range(nbb):
        m_prev = m_ref[b0 + bb, 0:1, :]
        for j in range(N_SUB):
            i = (bb, j)
            m_t = fcm[i] + jnp.maximum(m_prev, cmx[i])
            m_new = m_t[last:last + 1]
            f_tot = fcm[i][last:last + 1]
            u = fcm[i] - m_t
            w_prev = jnp.exp(u + m_prev)
            w_s = jnp.exp(f_tot - fcm[i] + gcs[i] - m_new)
            gdec = jnp.broadcast_to(jnp.exp(f_tot + m_prev - m_new), (16, LANE))
            e_neg[i] = jnp.exp(-m_t)
            keep = lambda a: jnp.where(head_lane, a, 0.0)
            bx[i] = jnp.concatenate(_split_bf16(keep(u), 2) + _split_bf16(keep(w_prev), 1)
                                    + _split_bf16(keep(w_s), 1) + _split_bf16(keep(gdec), 2), axis=0)
            m_prev = m_new
        m_ref[b0 + bb] = jnp.broadcast_to(m_prev, (8, LANE))
    ub, wpb, wsb, gdb = {}, {}, {}, {}
    for i in inst:
        y = _dot(bx[i], sel_h)
        c = CHUNK
        ub[i] = y[0:c] + y[c:2 * c]
        wpb[i] = y[2 * c:3 * c]
        wsb[i] = y[3 * c:4 * c]
        gdb[i] = y[4 * c:4 * c + 1] + y[4 * c + 16:4 * c + 17]
    q, v, qkw, ds = {}, {}, {}, {}
    for bb, j in inst:
        i = (bb, j)
        rs = pl.ds(r0s[j], CHUNK)
        w = jnp.where(valid4, jnp.exp(ub[i] + a_row[i]), 0.0)
        q[i] = qk_ref[bb, rs, 0:HP].astype(BF16)
        k = qk_ref[bb, rs, HP:2 * HP]
        kb = k.astype(BF16)
        kw = (k * wsb[i]).astype(BF16)
        v[i] = jnp.where(lane4 == ML_DH, 1.0, v_ref[bb, rs, :]).astype(BF16)
        for h, sl in enumerate(heads):
            sc = _dot_nt(q[i][:, sl], kb[:, sl])
            qkw[i, h] = (sc * w[:, h * LANE:h * LANE + CHUNK]).astype(BF16)
            ds[i, h] = _dot_tn(v[i][:, sl], kw[:, sl])
    s_in = {}
    for bb in range(nbb):
        for h, sl in enumerate(heads):
            st = st_ref[b0 + bb, h]
            for j in range(N_SUB):
                i = (bb, j)
                s_in[i, h] = st.astype(BF16)
                st = gdb[i][:, sl] * st + ds[i, h]
            st_ref[b0 + bb, h] = st
    num = {}
    for i in inst:
        parts = [_dot_nt(q[i][:, sl], s_in[i, h]) for h, sl in enumerate(heads)]
        intra = [_dot(qkw[i, h], v[i][:, sl]) for h, sl in enumerate(heads)]
        num[i] = wpb[i] * jnp.concatenate(parts, axis=1) + jnp.concatenate(intra, axis=1)
    for bb, j in inst:
        i = (bb, j)
        den = jnp.zeros((CHUNK, LANE), F32)
        for h, sl in enumerate(heads):
            dh = jnp.sum(jnp.where(lane_c == ML_DH, num[i][:, sl], 0.0), axis=-1, keepdims=True)
            den = jnp.where(lane_c == ML_GL + h, jnp.broadcast_to(dh, (CHUNK, LANE)), den)
        rb = _dot(jnp.where(head_lane, 1.0 / jnp.maximum(jnp.abs(den), e_neg[i]), 0.0).astype(BF16), sel_h)
        o_ref[bb, pl.ds(r0s[j], CHUNK), :] = jnp.where(lane4 < ML_DH, num[i] * rb, 0.0).astype(o_ref.dtype)


def _mlstm(qk3, pm3, gtl, gbr, gbt, *, rows, rev):
    nb, l, _ = qk3.shape
    nbb = SCAN_BATCHES
    dr = 1 if rev else 0
    pos = functools.partial(_scan_pos_static, rev, ncb=rows.ncb, ntb=rows.ntb)
    return pl.pallas_call(
        functools.partial(_mlstm_kernel, rev),
        out_shape=jax.ShapeDtypeStruct((nb, l, HP), BF16),
        grid=(rows.ntb, nb // nbb),
        in_specs=[pl.BlockSpec((nbb, SEG, 2 * HP), lambda s, g: (g, pos(s), 0)),
                  pl.BlockSpec((nbb, SEG, HP), lambda s, g: (g, pos(s), 0)),
                  pl.BlockSpec((nbb, SEG, LANE), lambda s, g: (g, pos(s), HP // LANE + dr)),
                  pl.BlockSpec((nbb, N_SUB, None, 2, HP), lambda s, g: (g, pos(s), dr, 0, 0)),
                  pl.BlockSpec((1, LANE), lambda s, g: (0, 0)),
                  pl.BlockSpec((2, HP), lambda s, g: (0, 0))],
        out_specs=pl.BlockSpec((nbb, SEG, HP), lambda s, g: (g, pos(s), 0)),
        scratch_shapes=[pltpu.VMEM((nb, ML_HEADS, LANE, LANE), F32), pltpu.VMEM((nb, 8, LANE), F32)],
        compiler_params=_cparams(("arbitrary", "arbitrary")),
        name="mlstm_scan_bwd" if rev else "mlstm_scan_fwd",
    )(qk3, pm3, pm3, gtl, gbr, gbt)


def _s5_kernel(nb, u_ref, bre_ref, bim_ref, are_ref, aim_ref, cre_ref, cim_ref, o_ref, xr_ref, xi_ref, st_ref):
    d, s = pl.program_id(0), pl.program_id(1)

    @pl.when(s == 0)
    def _():
        st_ref[...] = jnp.zeros(st_ref.shape, F32)

    n_sub = u_ref.shape[0] // (S5_SUB * nb)
    sub_rows = S5_SUB * nb

    def run(rev):
        ar, ai = are_ref[0], aim_ref[0]
        order = list(range(n_sub))[::-1] if rev else list(range(n_sub))

        def project_in(q):
            rs = slice(q * sub_rows, (q + 1) * sub_rows)
            u = u_ref[rs, :].astype(BF16)
            xr_ref[rs, :] = _dot(u, bre_ref[0])
            xi_ref[rs, :] = _dot(u, bim_ref[0])

        sr, si = st_ref[0], st_ref[1]
        project_in(order[0])
        for n, q in enumerate(order):
            if n + 1 < n_sub:
                project_in(order[n + 1])
            for j in range(S5_SUB):
                r0 = (q * S5_SUB + (S5_SUB - 1 - j if rev else j)) * nb
                nr = ar * sr - ai * si + xr_ref[r0:r0 + nb, :]
                ni = ar * si + ai * sr + xi_ref[r0:r0 + nb, :]
                xr_ref[r0:r0 + nb, :] = nr
                xi_ref[r0:r0 + nb, :] = ni
                sr, si = nr, ni
            rs = slice(q * sub_rows, (q + 1) * sub_rows)
            o_ref[0, rs, :] = (_dot(xr_ref[rs, :].astype(BF16), cre_ref[...])
                               - _dot(xi_ref[rs, :].astype(BF16), cim_ref[...])).astype(o_ref.dtype)
        st_ref[0] = sr
        st_ref[1] = si

    @pl.when(d == 0)
    def _():
        run(False)

    @pl.when(d == 1)
    def _():
        run(True)


def _s5(ut, bre, bim, are, aim, cre, cim, *, nb, nc, nt):
    n_rows, ch = ut.shape
    tr = S5_STEPS * nb
    ns = bre.shape[-1]
    pos = functools.partial(_scan_pos, ncb=nc, ntb=nt)
    return pl.pallas_call(
        functools.partial(_s5_kernel, nb),
        out_shape=jax.ShapeDtypeStruct((2, n_rows, ch), BF16),
        grid=(2, nt),
        in_specs=[pl.BlockSpec((tr, ch), lambda d, s: (pos(d, s), 0)),
                  pl.BlockSpec((1, ch, ns), lambda d, s: (d, 0, 0)),
                  pl.BlockSpec((1, ch, ns), lambda d, s: (d, 0, 0)),
                  pl.BlockSpec((1, nb, ns), lambda d, s: (d, 0, 0)),
                  pl.BlockSpec((1, nb, ns), lambda d, s: (d, 0, 0)),
                  pl.BlockSpec((ns, ch), lambda d, s: (0, 0)),
                  pl.BlockSpec((ns, ch), lambda d, s: (0, 0))],
        out_specs=pl.BlockSpec((1, tr, ch), lambda d, s: (d, pos(d, s), 0)),
        scratch_shapes=[pltpu.VMEM((tr, ns), F32), pltpu.VMEM((tr, ns), F32), pltpu.VMEM((2, nb, ns), F32)],
        compiler_params=_cparams(("arbitrary", "arbitrary")),
        name="s5_scan",
    )(ut, bre, bim, are, aim, cre, cim)


def _head_norm(o, gain, dim):
    parts = []
    for h in range(o.shape[1] // LANE):
        seg = o[:, h * LANE:(h + 1) * LANE]
        ms = jnp.sum(seg * seg, axis=-1, keepdims=True) * (1.0 / dim)
        parts.append(seg * lax.rsqrt(ms + EPS))
    return jnp.concatenate(parts, axis=1) * gain


def _mix_kernel(with_router, ogf_ref, ogb_ref, gg_ref, ys_ref, u_ref, omf_ref, omb_ref, mo_ref, h_ref, mod_ref,
                gn_ref, mn_ref, sd_ref, gw_ref, gb_ref, wo_ref, n2_ref, *rest):
    if with_router:
        wr_ref, ho_ref, f_ref, rt_ref = rest
    else:
        ho_ref, f_ref = rest
    gla = _head_norm(ogf_ref[...].astype(F32) + ogb_ref[...].astype(F32), gn_ref[...], GLA_DV) * _silu(gg_ref[...].astype(F32))
    z = _gelu_tanh(ys_ref[0].astype(F32) + ys_ref[1].astype(F32) + sd_ref[...] * u_ref[...])
    s5 = z * jax.nn.sigmoid(_dot(z.astype(BF16), gw_ref[...]) + gb_ref[...])
    ml = _head_norm(omf_ref[...].astype(F32) + omb_ref[...].astype(F32), mn_ref[...], ML_DH) * jax.nn.sigmoid(mo_ref[...].astype(F32))
    mix = (_dot(gla.astype(BF16), wo_ref[0:HP]) + _dot(s5.astype(BF16), wo_ref[HP:HP + 2 * LANE])
           + _dot(ml.astype(BF16), wo_ref[HP + 2 * LANE:]))
    m = mod_ref[0]
    hn = h_ref[...] + m[2:3] * mix
    ho_ref[...] = hn
    f = _rmsnorm(hn, n2_ref[...]) * (1.0 + m[4:5]) + m[3:4]
    if not with_router:
        f_ref[...] = f.astype(f_ref.dtype)
    if with_router:
        f_hi, f_lo = _split_bf16(f, 2)
        logits = (_dot(f_hi, wr_ref[0]) + _dot(f_lo, wr_ref[0]) + _dot(f_hi, wr_ref[1])).T[0:2 * N_EXPERTS]
        row = lax.broadcasted_iota(jnp.int32, logits.shape, 0)
        l0 = jnp.where(row < N_EXPERTS, logits, NEG)
        m1 = jnp.max(l0, axis=0, keepdims=True)
        i1 = jnp.min(jnp.where(l0 == m1, row, 2 * N_EXPERTS), axis=0, keepdims=True)
        l1 = jnp.where(row == i1, NEG, l0)
        m2 = jnp.max(l1, axis=0, keepdims=True)
        i2 = jnp.min(jnp.where(l1 == m2, row, 2 * N_EXPERTS), axis=0, keepdims=True)
        e = jnp.exp(m2 - m1)
        w1 = 1.0 / (1.0 + e)
        w2 = e / (1.0 + e)
        rt = jnp.where(row == 0, i1.astype(F32),
                       jnp.where(row == 1, i2.astype(F32),
                                 jnp.where(row == 2, w1, jnp.where(row == 3, w2, 0.0))))
        rt_ref[...] = rt
        d = f.shape[1]
        f_ref[:, 0:d // 2] = _pack_bf16_pairs(f)
        f_ref[:, d // 2:] = jnp.concatenate([rt, jnp.zeros((LANE - rt.shape[0], rt.shape[1]), F32)], axis=0).T


def _mix(ogf, ogb, pg, ys, pu, omf, omb, pm, h, modtab, gn, mn, sd, gw, gb, wo, n2, wr, *, layer, rows, lat_only):
    d = h.shape[1]
    tm = SEG
    n = rows.n_blocks(lat_only)
    src, sel = rows.src(lat_only), rows.sel(lat_only)
    full = lambda a: pl.BlockSpec(a.shape, lambda i: (0,) * a.ndim)
    with_router = wr is not None
    in_specs = [pl.BlockSpec((tm, HP), lambda i: (src(i), 0)),
                pl.BlockSpec((tm, HP), lambda i: (src(i), 0)),
                pl.BlockSpec((tm, HP), lambda i: (src(i), 0)),
                pl.BlockSpec((2, tm, 2 * LANE), lambda i: (0, src(i), 0)),
                pl.BlockSpec((tm, 2 * LANE), lambda i: (src(i), 0)),
                pl.BlockSpec((tm, HP), lambda i: (src(i), 0)),
                pl.BlockSpec((tm, HP), lambda i: (src(i), 0)),
                pl.BlockSpec((tm, HP), lambda i: (src(i), 1)),
                pl.BlockSpec((tm, d), lambda i: (src(i), 0)),
                pl.BlockSpec((None, 1, 8, d), lambda i: (layer, sel(i), 0, 0)),
                full(gn), full(mn), full(sd), full(gw), full(gb), full(wo), full(n2)]
    args = [ogf, ogb, pg, ys, pu, omf, omb, pm, h, modtab, gn, mn, sd, gw, gb, wo, n2]
    fw, fdt = (d // 2 + LANE, F32) if with_router else (d, BF16)
    out_shape = [jax.ShapeDtypeStruct((n * tm, d), F32), jax.ShapeDtypeStruct((n * tm, fw), fdt)]
    out_specs = [pl.BlockSpec((tm, d), lambda i: (i, 0)), pl.BlockSpec((tm, fw), lambda i: (i, 0))]
    if with_router:
        in_specs.append(full(wr))
        args.append(wr)
        out_shape.append(jax.ShapeDtypeStruct((2 * N_EXPERTS, n * tm), F32))
        out_specs.append(pl.BlockSpec((2 * N_EXPERTS, tm), lambda i: (0, i)))
    return pl.pallas_call(
        functools.partial(_mix_kernel, with_router),
        out_shape=tuple(out_shape),
        grid=(n,),
        in_specs=in_specs,
        out_specs=tuple(out_specs),
        compiler_params=_cparams(("arbitrary",)),
        name="mix_out",
    )(*args)


FF_TILE = 256


def _swiglu(xb, w1_ref, w3_ref, w2_ref, a_ref, lead=()):
    dff = w1_ref.shape[-1]
    for j in range(dff // FF_TILE):
        sl = slice(j * FF_TILE, (j + 1) * FF_TILE)
        h1 = _dot(xb, w1_ref[lead + (slice(None), sl)])
        h3 = _dot(xb, w3_ref[lead + (slice(None), sl)])
        a_ref[:, sl] = (_silu(h1) * h3).astype(BF16)
    return _dot(a_ref[...], w2_ref[lead + (slice(None), slice(None))])


FFN_TM = 1024
MIXFFN_TM = 2 * SEG


def _mix_ffn_kernel(sel, ogf_ref, ogb_ref, gg_ref, ys_ref, u_ref, omf_ref, omb_ref, mo_ref, h_ref, mod_ref,
                    gn_ref, mn_ref, sd_ref, gw_ref, gb_ref, wo_ref, n2_ref, w1_ref, w3_ref, w2_ref, o_ref,
                    a_ref, hn_ref, f_ref):
    n_seg = h_ref.shape[0] // SEG
    mods = []
    for q in range(n_seg):
        rs = slice(q * SEG, (q + 1) * SEG)
        m = mod_ref[sel(pl.program_id(0) * n_seg + q)]
        mods.append(m)
        gla = (_head_norm(ogf_ref[rs, :].astype(F32) + ogb_ref[rs, :].astype(F32), gn_ref[...], GLA_DV)
               * _silu(gg_ref[rs, :].astype(F32)))
        z = _gelu_tanh(ys_ref[0, rs, :].astype(F32) + ys_ref[1, rs, :].astype(F32) + sd_ref[...] * u_ref[rs, :])
        s5 = z * jax.nn.sigmoid(_dot(z.astype(BF16), gw_ref[...]) + gb_ref[...])
        ml = (_head_norm(omf_ref[rs, :].astype(F32) + omb_ref[rs, :].astype(F32), mn_ref[...], ML_DH)
              * jax.nn.sigmoid(mo_ref[rs, :].astype(F32)))
        mix = (_dot(gla.astype(BF16), wo_ref[0:HP]) + _dot(s5.astype(BF16), wo_ref[HP:HP + 2 * LANE])
               + _dot(ml.astype(BF16), wo_ref[HP + 2 * LANE:]))
        hn = h_ref[rs, :] + m[2:3] * mix
        hn_ref[rs, :] = hn
        f_ref[rs, :] = (_rmsnorm(hn, n2_ref[...]) * (1.0 + m[4:5]) + m[3:4]).astype(BF16)
    y = _swiglu(f_ref[...], w1_ref, w3_ref, w2_ref, a_ref)
    for q in range(n_seg):
        rs = slice(q * SEG, (q + 1) * SEG)
        o_ref[rs, :] = hn_ref[rs, :] + mods[q][5:6] * y[rs]


def _mix_ffn(ogf, ogb, pg, ys, pu, omf, omb, pm, h, modtab, gn, mn, sd, gw, gb, wo, n2, w1, w3, w2, *, layer, rows):
    r, d = h.shape
    tm = MIXFFN_TM
    assert r % tm == 0
    dff = w1.shape[-1]
    full = lambda a: pl.BlockSpec(a.shape, lambda i: (0,) * a.ndim)
    resident = lambda a: pl.BlockSpec(a.shape, lambda i: (0,) * a.ndim, pipeline_mode=pl.Buffered(1))
    in_specs = [pl.BlockSpec((tm, HP), lambda i: (i, 0)),
                pl.BlockSpec((tm, HP), lambda i: (i, 0)),
                pl.BlockSpec((tm, HP), lambda i: (i, 0)),
                pl.BlockSpec((2, tm, 2 * LANE), lambda i: (0, i, 0)),
                pl.BlockSpec((tm, 2 * LANE), lambda i: (i, 0)),
                pl.BlockSpec((tm, HP), lambda i: (i, 0)),
                pl.BlockSpec((tm, HP), lambda i: (i, 0)),
                pl.BlockSpec((tm, HP), lambda i: (i, 1)),
                pl.BlockSpec((tm, d), lambda i: (i, 0)),
                pl.BlockSpec((None,) + modtab.shape[1:], lambda i: (layer, 0, 0, 0)),
                full(gn), full(mn), full(sd), full(gw), full(gb), resident(wo), full(n2),
                resident(w1), resident(w3), resident(w2)]
    return pl.pallas_call(
        functools.partial(_mix_ffn_kernel, rows.sel(False)),
        out_shape=jax.ShapeDtypeStruct((r, d), F32),
        grid=(r // tm,),
        in_specs=in_specs,
        out_specs=pl.BlockSpec((tm, d), lambda i: (i, 0)),
        scratch_shapes=[pltpu.VMEM((tm, dff), BF16), pltpu.VMEM((tm, d), F32), pltpu.VMEM((tm, d), BF16)],
        compiler_params=_cparams(("arbitrary",)),
        name="mix_ffn",
    )(ogf, ogb, pg, ys, pu, omf, omb, pm, h, modtab, gn, mn, sd, gw, gb, wo, n2, w1, w3, w2)


def _ffn_kernel(final, sel, f_ref, h_ref, mod_ref, w1_ref, w3_ref, w2_ref, *rest):
    if final:
        nf_ref, o_ref, a_ref = rest
    else:
        o_ref, a_ref = rest
    y = _swiglu(f_ref[...], w1_ref, w3_ref, w2_ref, a_ref)
    n_seg = f_ref.shape[0] // SEG
    for q in range(n_seg):
        rs = slice(q * SEG, (q + 1) * SEG)
        gate = mod_ref[sel(pl.program_id(0) * n_seg + q)][5:6]
        hn = h_ref[rs, :] + gate * y[rs]
        o_ref[rs, :] = _rmsnorm(hn, nf_ref[...]) if final else hn


def _ffn(f, h, modtab, w1, w3, w2, nf, *, layer, rows, lat_only):
    r, d = h.shape
    tm = FFN_TM
    assert r % tm == 0
    dff = w1.shape[-1]
    full = lambda a: pl.BlockSpec(a.shape, lambda i: (0,) * a.ndim)
    resident = lambda a: pl.BlockSpec(a.shape, lambda i: (0,) * a.ndim, pipeline_mode=pl.Buffered(1))
    in_specs = [pl.BlockSpec((tm, d), lambda i: (i, 0)),
                pl.BlockSpec((tm, d), lambda i: (i, 0)),
                pl.BlockSpec((None,) + modtab.shape[1:], lambda i: (layer, 0, 0, 0)),
                resident(w1), resident(w3), resident(w2)]
    args = [f, h, modtab, w1, w3, w2]
    if lat_only:
        in_specs.append(full(nf))
        args.append(nf)
    return pl.pallas_call(
        functools.partial(_ffn_kernel, lat_only, rows.sel(lat_only)),
        out_shape=jax.ShapeDtypeStruct((r, d), F32),
        grid=(r // tm,),
        in_specs=in_specs,
        out_specs=pl.BlockSpec((tm, d), lambda i: (i, 0)),
        scratch_shapes=[pltpu.VMEM((tm, dff), BF16)],
        compiler_params=_cparams(("arbitrary",)),
        name="ffn",
    )(*args)


MOE_TM = 512


def _moe_kernel(be_ref, live_ref, x_ref, w1_ref, w3_ref, w2_ref, o_ref, a_ref):
    i = pl.program_id(0)
    d = w1_ref.shape[1]

    @pl.when(live_ref[i] > 0)
    def _():
        xa, xb = _unpack_bf16_pairs(x_ref[:, 0:d // 2])
        x = jnp.concatenate([xa.astype(BF16), xb.astype(BF16)], axis=1)
        y = _swiglu(x, w1_ref, w3_ref, w2_ref, a_ref, lead=(0,))
        tail = x_ref[:, d // 2:]
        mine = tail[:, 0:1] == be_ref[i].astype(F32)
        o_ref[...] = _pack_bf16_pairs(y * jnp.where(mine, tail[:, 2:3], tail[:, 3:4]))

    @pl.when(live_ref[i] == 0)
    def _():
        o_ref[...] = jnp.zeros(o_ref.shape, F32)


def _moe_experts(block_expert, block_live, xg, w1, w3, w2):
    n_rows, dx = xg.shape
    d = w1.shape[-2]
    dff = w1.shape[-1]
    tm = MOE_TM
    return pl.pallas_call(
        _moe_kernel,
        out_shape=jax.ShapeDtypeStruct((n_rows, d // 2), F32),
        grid_spec=pltpu.PrefetchScalarGridSpec(
            num_scalar_prefetch=2,
            grid=(n_rows // tm,),
            in_specs=[pl.BlockSpec((tm, dx), lambda i, be, lv: (i, 0)),
                      pl.BlockSpec((1, d, dff), lambda i, be, lv: (be[i], 0, 0)),
                      pl.BlockSpec((1, d, dff), lambda i, be, lv: (be[i], 0, 0)),
                      pl.BlockSpec((1, dff, d), lambda i, be, lv: (be[i], 0, 0))],
            out_specs=pl.BlockSpec((tm, d // 2), lambda i, be, lv: (i, 0)),
            scratch_shapes=[pltpu.VMEM((tm, dff), BF16)]),
        compiler_params=_cparams(("arbitrary",)),
        name="moe_experts",
    )(block_expert, block_live, xg, w1, w3, w2)


SC_GATHER_ROWS = 64


def _gather_rows(table, idx):
    n_k, n_idx = idx.shape
    _, d = table.shape
    info = plsc.get_sparse_core_info()
    n_cores, n_workers = info.num_cores, info.num_cores * info.num_subcores
    assert n_idx % (n_workers * SC_GATHER_ROWS) == 0
    per_worker = n_idx // n_workers
    mesh = plsc.VectorSubcoreMesh(core_axis_name="c", subcore_axis_name="s")

    @functools.partial(
        pl.kernel, mesh=mesh,
        out_type=jax.ShapeDtypeStruct((n_k * n_idx, d), table.dtype),
        scratch_types=[pltpu.VMEM((SC_GATHER_ROWS,), jnp.int32),
                       pltpu.VMEM((SC_GATHER_ROWS, d), table.dtype),
                       pltpu.SemaphoreType.DMA])
    def gather(table_hbm, idx_hbm, out_hbm, idx_v, rows_v, sem):
        base = (lax.axis_index("s") * n_cores + lax.axis_index("c")) * per_worker

        @pl.loop(0, per_worker // SC_GATHER_ROWS)
        def _(it):
            for k in range(n_k):
                off = pl.multiple_of(k * n_idx + base + it * SC_GATHER_ROWS, SC_GATHER_ROWS)
                pltpu.sync_copy(idx_hbm.at[pl.ds(off, SC_GATHER_ROWS)], idx_v)
                pltpu.async_copy(table_hbm.at[idx_v], rows_v, sem).wait()
                pltpu.sync_copy(rows_v, out_hbm.at[pl.ds(off, SC_GATHER_ROWS)])

    return gather(table, idx.reshape(-1)).reshape(n_k, n_idx, d)


def _scatter_rows(src, dest, n_rows):
    n_tok, d = src.shape
    info = plsc.get_sparse_core_info()
    n_cores, n_workers = info.num_cores, info.num_cores * info.num_subcores
    assert n_tok % (n_workers * SC_GATHER_ROWS) == 0
    per_worker = n_tok // n_workers
    mesh = plsc.VectorSubcoreMesh(core_axis_name="c", subcore_axis_name="s")

    @functools.partial(
        pl.kernel, mesh=mesh,
        out_type=jax.ShapeDtypeStruct((n_rows, d), src.dtype),
        scratch_types=[pltpu.VMEM((SC_GATHER_ROWS,), jnp.int32),
                       pltpu.VMEM((SC_GATHER_ROWS, d), src.dtype),
                       pltpu.SemaphoreType.DMA])
    def scatter(src_hbm, dest_hbm, out_hbm, idx_v, rows_v, sem):
        base = (lax.axis_index("s") * n_cores + lax.axis_index("c")) * per_worker

        @pl.loop(0, per_worker // SC_GATHER_ROWS)
        def _(it):
            off = pl.multiple_of(base + it * SC_GATHER_ROWS, SC_GATHER_ROWS)
            pltpu.sync_copy(src_hbm.at[pl.ds(off, SC_GATHER_ROWS)], rows_v)
            for k in range(TOP_K):
                pltpu.sync_copy(dest_hbm.at[pl.ds(k * n_tok + off, SC_GATHER_ROWS)], idx_v)
                pltpu.async_copy(rows_v, out_hbm.at[idx_v], sem).wait()

    return scatter(src, dest.reshape(-1))


def _pack_bf16_pairs(y):
    n = y.shape[1] // 2
    hi = pltpu.bitcast(y[:, :n].astype(BF16).astype(F32), jnp.uint32)
    lo = pltpu.bitcast(y[:, n:].astype(BF16).astype(F32), jnp.uint32)
    return pltpu.bitcast(hi | (lo >> 16), F32)


def _unpack_bf16_pairs(w):
    u = pltpu.bitcast(w, jnp.uint32)
    return pltpu.bitcast(u & jnp.uint32(0xFFFF0000), F32), pltpu.bitcast(u << 16, F32)


def _resid_kernel(final, h_ref, y_ref, mod_ref, *rest):
    a0, b0 = _unpack_bf16_pairs(y_ref[0])
    a1, b1 = _unpack_bf16_pairs(y_ref[1])
    hn = h_ref[...] + mod_ref[0][5:6] * jnp.concatenate([a0 + a1, b0 + b1], axis=1)
    if final:
        nf_ref, o_ref = rest
        o_ref[...] = _rmsnorm(hn, nf_ref[...])
    else:
        (o_ref,) = rest
        o_ref[...] = hn


def _moe_resid(h, y01, modtab, nf, *, layer, rows, lat_only):
    r, d = h.shape
    tm = SEG
    sel = rows.sel(lat_only)
    row = lambda i: (i, 0)
    in_specs = [pl.BlockSpec((tm, d), row), pl.BlockSpec((TOP_K, tm, d // 2), lambda i: (0, i, 0)),
                pl.BlockSpec((None, 1, 8, d), lambda i: (layer, sel(i), 0, 0))]
    args = [h, y01, modtab]
    if lat_only:
        in_specs.append(pl.BlockSpec((1, d), lambda i: (0, 0)))
        args.append(nf)
    return pl.pallas_call(
        functools.partial(_resid_kernel, lat_only),
        out_shape=jax.ShapeDtypeStruct((r, d), F32),
        grid=(r // tm,),
        in_specs=in_specs,
        out_specs=pl.BlockSpec((tm, d), row),
        compiler_params=_cparams(("arbitrary",)),
        name="moe_resid",
    )(*args)


CAST_PARTS = 8


def _cast_kernel(*refs):
    o_ref = refs[-1]
    tr = refs[0].shape[1]
    for k, x_ref in enumerate(refs[:-1]):
        o_ref[0, k * tr:(k + 1) * tr, :] = x_ref[0].astype(o_ref.dtype)


def _to_bf16(w, j):
    lead, (r, c) = w.shape[1:-2], w.shape[-2:]
    n = int(np.prod(lead, dtype=np.int64))
    w3 = w.reshape((-1, r, c))
    tr = r // CAST_PARTS
    assert tr * CAST_PARTS == r and tr % 16 == 0
    band = lambda k: pl.BlockSpec((1, tr, c), lambda e: (j * n + e, k, 0))
    out = pl.pallas_call(
        _cast_kernel,
        out_shape=jax.ShapeDtypeStruct((n, r, c), BF16),
        grid=(n,),
        in_specs=[band(k) for k in range(CAST_PARTS)],
        out_specs=pl.BlockSpec((1, r, c), lambda e: (e, 0, 0)),
        compiler_params=_cparams(("arbitrary",)),
        name="to_bf16",
    )(*([w3] * CAST_PARTS))
    return out.reshape(lead + (r, c))


def _pad_heads(w, heads, dim, to=LANE):
    lead = w.shape[:-1]
    w = w.reshape(lead + (heads, dim))
    w = jnp.pad(w, [(0, 0)] * len(lead) + [(0, 0), (0, to - dim)])
    return w.reshape(lead + (heads * to,))


def _pad_last(w, to):
    return jnp.pad(w, [(0, 0)] * (w.ndim - 1) + [(0, to - w.shape[-1])])


def _pos_embed(n_tokens, d):
    n_grid_rows = n_tokens // GRID_W
    row, col = jnp.meshgrid(jnp.arange(n_grid_rows, dtype=F32), jnp.arange(GRID_W, dtype=F32), indexing='ij')
    n_freq = d // 4
    omega = jnp.exp(-math.log(POS_BASE) * jnp.arange(n_freq, dtype=F32) / n_freq)

    def axis_embed(p):
        ang = p.reshape(-1, 1) * omega
        return jnp.concatenate([jnp.sin(ang), jnp.cos(ang)], axis=-1)

    return jnp.concatenate([axis_embed(row), axis_embed(col)], axis=-1)


def _s5_discretise(lam_re, lam_im, log_dt, b_re, b_im):
    dt = jnp.exp(log_dt)[:, None]
    mag = jnp.exp(lam_re * dt)
    abar_re, abar_im = mag * jnp.cos(lam_im * dt), mag * jnp.sin(lam_im * dt)
    den = lam_re * lam_re + lam_im * lam_im
    pr, pi = abar_re - 1.0, abar_im
    coef_re = (pr * lam_re + pi * lam_im) / den
    coef_im = (pi * lam_re - pr * lam_im) / den
    bbar_re = coef_re[..., None] * b_re - coef_im[..., None] * b_im
    bbar_im = coef_re[..., None] * b_im + coef_im[..., None] * b_re
    return abar_re, abar_im, bbar_re, bbar_im


def _block_diag(m):
    g, a, b = m.shape
    eye = jnp.eye(g, dtype=m.dtype)
    return (eye[:, None, :, None] * m[:, :, None, :]).reshape(g * a, g * b)


def _route_plan(route, tm):
    n_tok = route.shape[1]
    n_assign = n_tok * TOP_K
    flat_e = route[0:TOP_K].astype(jnp.int32).reshape(-1)
    onehot = (jnp.arange(N_EXPERTS, dtype=jnp.int32)[:, None] == flat_e[None, :]).astype(jnp.int32)
    csum = jnp.cumsum(onehot, axis=1)
    counts = csum[:, -1]
    padded = (counts + tm - 1) // tm * tm
    pend = jnp.cumsum(padded)
    pstart = pend - padded
    dest = jnp.sum(onehot * (csum - 1 + pstart[:, None]), axis=0)
    n_blocks = -(-n_assign // tm) + N_EXPERTS
    block_start = jnp.arange(n_blocks, dtype=jnp.int32) * tm
    block_expert = jnp.minimum(jnp.searchsorted(pend, block_start, side='right'), N_EXPERTS - 1).astype(jnp.int32)
    block_live = (block_start < (pstart + counts)[block_expert]).astype(jnp.int32)
    return n_blocks * tm, block_expert, block_live, dest.reshape(TOP_K, n_tok)


def kernel(x, c, ctx, c_ctx, w_ada, b_ada, norm1, norm2, w_in, w_out, gla_wa2, gla_ba, gla_norm, s5_lam_re, s5_lam_im, s5_log_dt, s5_b_re, s5_b_im, s5_c_re, s5_c_im, s5_d, s5_glu_w, s5_glu_b, ml_conv_w, ml_conv_b, ml_gate_b, ml_norm, ffn_w1, ffn_w3, ffn_w2, moe_router, moe_w1, moe_w3, moe_w2, norm_f):
    nb, n_lat, d = x.shape
    lc = ctx.shape[1]
    depth = w_ada.shape[0]
    l = lc + n_lat
    assert lc % SEG == 0 and n_lat % SEG == 0 and nb == 8 and nb % SCAN_BATCHES == 0
    rows = _Rows(nb, lc // SEG, l // SEG)

    h = _embed(ctx.reshape(nb * lc, d), x.reshape(nb * n_lat, d), _pos_embed(n_lat, d), rows)

    cond = jnp.zeros((16, d), F32).at[:nb].set(c).at[nb].set(c_ctx)
    mod = _modulation(cond, w_ada, b_ada)
    modtab = jnp.pad(mod.reshape(depth, 16, 6, d), ((0, 0), (0, 0), (0, 2), (0, 0)))

    dk, dv, dh = GLA_HEADS * GLA_DK, GLA_HEADS * GLA_DV, ML_HEADS * ML_DH
    s5c = s5_d.shape[-1]
    cuts = np.cumsum([dk, dk, dv, GLA_RANK, dv, s5c, dh, dh, dh, dh, 4 * ML_HEADS])

    for i in range(depth):
        last = i == depth - 1
        gq, gk, gv, glr, gg, su, mq, mk, mv, mo, mg = jnp.split(w_in[i], cuts[:-1], axis=-1)
        w_all = jnp.concatenate([
            _pad_heads(gq, GLA_HEADS, GLA_DK, GLA_KP), _pad_heads(gk, GLA_HEADS, GLA_DK, GLA_KP),
            _pad_heads(gv, GLA_HEADS, GLA_DV), _pad_last(glr, LANE),
            su,
            _pad_heads(mv, ML_HEADS, ML_DH),
            _pad_last(mg[:, :2 * ML_HEADS], LANE), _pad_last(mg[:, 2 * ML_HEADS:], LANE),
            _pad_heads(mq, ML_HEADS, ML_DH), _pad_heads(mk, ML_HEADS, ML_DH),
            _pad_heads(gg, GLA_HEADS, GLA_DV), _pad_heads(mo, ML_HEADS, ML_DH)],
            axis=-1).astype(BF16)
        wgt = mg.T.astype(BF16)
        cw = jnp.concatenate([_pad_heads(ml_conv_w[i][:, :dh], ML_HEADS, ML_DH),
                              _pad_heads(ml_conv_w[i][:, dh:], ML_HEADS, ML_DH)], axis=-1)
        cw = jnp.pad(cw, ((0, 8 - ML_CONV), (0, 0)))
        cb = jnp.concatenate([_pad_heads(ml_conv_b[i][:dh], ML_HEADS, ML_DH),
                              _pad_heads(ml_conv_b[i][dh:], ML_HEADS, ML_DH)])[None]
        post = jnp.concatenate([jnp.ones((HP,), F32), jnp.full((HP,), ML_DH ** -0.5, F32)])[None]
        pg, pu, pm, qk, go, gt = _proj(h, modtab, norm1[i][None], w_all, wgt, cw, cb, post, layer=i, rows=rows)

        wa = jnp.pad(_pad_heads(gla_wa2[i], GLA_HEADS, GLA_DK, GLA_KP), ((0, 0), (0, LANE - GLA_RANK), (0, 0)))
        ba = _pad_heads(gla_ba[i], GLA_HEADS, GLA_DK, GLA_KP)[:, None, :]
        pg3 = pg.reshape(nb, l, NG)
        ogf = _gla(pg3, wa[0], ba[0], rows=rows, rev=False).reshape(nb * l, HP)
        ogb = _gla(pg3, wa[1], ba[1], rows=rows, rev=True).reshape(nb * l, HP)

        bres, bims, ares, aims = [], [], [], []
        for dr in (0, 1):
            a_re, a_im, b_re, b_im = _s5_discretise(s5_lam_re[i, dr], s5_lam_im[i, dr], s5_log_dt[i, dr],
                                                    s5_b_re[i], s5_b_im[i])
            bres.append(_block_diag(jnp.swapaxes(b_re, 1, 2)))
            bims.append(_block_diag(jnp.swapaxes(b_im, 1, 2)))
            ares.append(jnp.broadcast_to(a_re.reshape(1, -1), (nb, a_re.size)))
            aims.append(jnp.broadcast_to(a_im.reshape(1, -1), (nb, a_im.size)))
        cre = _block_diag(jnp.swapaxes(s5_c_re[i], 1, 2)).astype(BF16)
        cim = _block_diag(jnp.swapaxes(s5_c_im[i], 1, 2)).astype(BF16)
        ut = pu.reshape(nb, l, s5c).swapaxes(0, 1).reshape(l * nb, s5c)
        yt = _s5(ut, jnp.stack(bres).astype(BF16), jnp.stack(bims).astype(BF16), jnp.stack(ares), jnp.stack(aims),
                 cre, cim, nb=nb, nc=lc // S5_STEPS, nt=l // S5_STEPS)
        ys = yt.reshape(2, l, nb, s5c).swapaxes(1, 2).reshape(2, nb * l, s5c)

        gb = ml_gate_b[i].reshape(2, 2, ML_HEADS)
        gbr = _pad_last(gb.reshape(2, 1, 2 * ML_HEADS), LANE)
        gbt = _pad_last(jnp.broadcast_to(gb[..., None], (2, 2, ML_HEADS, CHUNK)), LANE).reshape(2, 2, HP)
        gtl = _pad_last(gt.reshape(2, 2, ML_HEADS, nb, l // CHUNK, CHUNK).transpose(3, 4, 0, 1, 2, 5),
                        LANE).reshape(nb, l // CHUNK, 2, 2, HP)
        qk3, pm3 = qk.reshape(nb, l, 2 * HP), pm.reshape(nb, l, NM)
        omf = _mlstm(qk3, pm3, gtl, gbr[0], gbt[0], rows=rows, rev=False).reshape(nb * l, HP)
        omb = _mlstm(qk3, pm3, gtl, gbr[1], gbt[1], rows=rows, rev=True).reshape(nb * l, HP)

        wo = w_out[i]
        wo_p = jnp.concatenate([
            jnp.pad(wo[:dv].reshape(GLA_HEADS, GLA_DV, d), ((0, 0), (0, LANE - GLA_DV), (0, 0))).reshape(HP, d),
            wo[dv:dv + s5c],
            jnp.pad(wo[dv + s5c:].reshape(ML_HEADS, ML_DH, d), ((0, 0), (0, LANE - ML_DH), (0, 0))).reshape(HP, d)],
            axis=0).astype(BF16)
        gn = jnp.tile(_pad_last(gla_norm[i], LANE), GLA_HEADS)[None]
        mn = jnp.tile(_pad_last(ml_norm[i], LANE), ML_HEADS)[None]
        is_moe = i % 2 == 1
        j = i // 2
        wr = jnp.stack(_split_bf16(_pad_last(moe_router[j], LANE), 2)) if is_moe else None
        mix_args = (ogf, ogb, go, ys, pu, omf, omb, go, h, modtab, gn, mn, s5_d[i][None], s5_glu_w[i].astype(BF16),
                    s5_glu_b[i][None], wo_p, norm2[i][None])
        if not is_moe and not last:
            h = _mix_ffn(*mix_args, _to_bf16(ffn_w1, j), _to_bf16(ffn_w3, j), _to_bf16(ffn_w2, j),
                         layer=i, rows=rows)
            continue
        outs = _mix(*mix_args, wr, layer=i, rows=rows, lat_only=last)
        if not is_moe:
            h, f = outs
            h = _ffn(f, h, modtab, _to_bf16(ffn_w1, j), _to_bf16(ffn_w3, j), _to_bf16(ffn_w2, j),
                     norm_f[None], layer=i, rows=rows, lat_only=last)
        else:
            h, f, route = outs
            n_rows, block_expert, block_live, dest = _route_plan(route, MOE_TM)
            xg = _scatter_rows(f, dest, n_rows)
            yg = _moe_experts(block_expert, block_live, xg, _to_bf16(moe_w1, j), _to_bf16(moe_w3, j),
                              _to_bf16(moe_w2, j))
            y01 = _gather_rows(yg, dest)
            h = _moe_resid(h, y01, modtab, norm_f[None], layer=i, rows=rows, lat_only=last)
    return h.reshape(nb, n_lat, d)
```

```python
import functools
import math

import numpy as np
import jax
import jax.numpy as jnp
from jax import lax
from jax.experimental import pallas as pl
from jax.experimental.pallas import tpu as pltpu
from jax.experimental.pallas import tpu_sc as plsc

F32 = jnp.float32
BF16 = jnp.bfloat16
HIGHEST = lax.Precision.HIGHEST

GRID_W = 64
POS_BASE = 10000.0
EPS = 1e-6
GLA_HEADS, GLA_DK, GLA_DV, GLA_RANK, GLA_GATE_NORM = 4, 48, 96, 16, 16.0
S5_GROUP, S5_STATE = 16, 64
ML_HEADS, ML_DH, ML_CONV = 4, 96, 3
N_EXPERTS, TOP_K = 8, 2

LANE = 128
CHUNK = 64
SEG = 256
N_SUB = SEG // CHUNK
SCAN_BATCHES = 8
S5_STEPS = 256
S5_SUB = 32
NEG = -1e30
V7X_VMEM_BYTES = 64 * 1024 * 1024
VMEM_LIMIT = V7X_VMEM_BYTES - 8 * 1024 * 1024

HP = LANE * GLA_HEADS
GLA_KP = 64
GQ = GLA_HEADS * GLA_KP
NG = 2 * GQ + HP + LANE
NM = HP + 2 * LANE
NGO = 2 * HP


def _cparams(sem):
    return pltpu.CompilerParams(dimension_semantics=sem, vmem_limit_bytes=VMEM_LIMIT)


def _dot(a, b, **kw):
    return jnp.dot(a, b, preferred_element_type=F32, **kw)


def _dot_nt(a, b, **kw):
    return lax.dot_general(a, b, (((1,), (1,)), ((), ())), preferred_element_type=F32, **kw)


def _dot_tn(a, b, **kw):
    return lax.dot_general(a, b, (((0,), (0,)), ((), ())), preferred_element_type=F32, **kw)


def _log_sigmoid(x):
    return jnp.minimum(x, 0.0) - jnp.log1p(jnp.exp(-jnp.abs(x)))


def _silu(x):
    return x * jax.nn.sigmoid(x)


def _gelu_tanh(x):
    return 0.5 * x * (1.0 + jnp.tanh(math.sqrt(2.0 / math.pi) * (x + 0.044715 * (x * x * x))))


def _rmsnorm(x, g):
    return x * lax.rsqrt(jnp.mean(x * x, axis=-1, keepdims=True) + EPS) * g


class _Rows:
    def __init__(self, nb, ncb, ntb):
        self.nb, self.ncb, self.ntb, self.nlb = nb, ncb, ntb, ntb - ncb

    def n_blocks(self, lat_only):
        return self.nb * (self.nlb if lat_only else self.ntb)

    def src(self, lat_only):
        if lat_only:
            return lambda i: (i // self.nlb) * self.ntb + self.ncb + i % self.nlb
        return lambda i: i

    def sel(self, lat_only):
        if lat_only:
            return lambda i: i // self.nlb
        return lambda i: jnp.where(i % self.ntb < self.ncb, self.nb, i // self.ntb)


def _embed_kernel(ncb, ntb, ctx_ref, x_ref, pos_ref, o_ref):
    j = pl.program_id(0) % ntb

    @pl.when(j < ncb)
    def _():
        o_ref[...] = ctx_ref[...]

    @pl.when(j >= ncb)
    def _():
        o_ref[...] = x_ref[...] + pos_ref[...]


def _embed(ctx2, x2, pos, rows):
    d = ctx2.shape[1]
    ncb, ntb, nlb = rows.ncb, rows.ntb, rows.nlb
    return pl.pallas_call(
        functools.partial(_embed_kernel, ncb, ntb),
        out_shape=jax.ShapeDtypeStruct((rows.nb * ntb * SEG, d), F32),
        grid=(rows.nb * ntb,),
        in_specs=[pl.BlockSpec((SEG, d), lambda i: ((i // ntb) * ncb + jnp.minimum(i % ntb, ncb - 1), 0)),
                  pl.BlockSpec((SEG, d), lambda i: ((i // ntb) * nlb + jnp.maximum(i % ntb - ncb, 0), 0)),
                  pl.BlockSpec((SEG, d), lambda i: (jnp.maximum(i % ntb - ncb, 0), 0))],
        out_specs=pl.BlockSpec((SEG, d), lambda i: (i, 0)),
        compiler_params=_cparams(("arbitrary",)),
        name="embed",
    )(ctx2, x2, pos)


def _mod_kernel(c_ref, w_ref, b_ref, o_ref):
    s = _silu(c_ref[...])
    o_ref[0] = _dot(s, w_ref[0], precision=HIGHEST) + b_ref[0]


def _modulation(cond, w_ada, b_ada):
    depth, d, n6 = w_ada.shape
    tn = n6 // 4
    n_rows = cond.shape[0]
    return pl.pallas_call(
        _mod_kernel,
        out_shape=jax.ShapeDtypeStruct((depth, n_rows, n6), F32),
        grid=(depth, n6 // tn),
        in_specs=[pl.BlockSpec((n_rows, d), lambda l, j: (0, 0)),
                  pl.BlockSpec((1, d, tn), lambda l, j: (l, 0, j)),
                  pl.BlockSpec((1, 1, tn), lambda l, j: (l, 0, j))],
        out_specs=pl.BlockSpec((1, n_rows, tn), lambda l, j: (l, 0, j)),
        compiler_params=_cparams(("arbitrary", "arbitrary")),
        name="modulation",
    )(cond, w_ada, b_ada.reshape(depth, 1, n6))


PROJ_TM = 2 * SEG


def _proj_kernel(ncb, ntb, sel, h_ref, hp_ref, hn_ref, mod_ref, g_ref, w_ref, wgt_ref, cw_ref, cb_ref, post_ref,
                 pg_ref, pu_ref, pm_ref, qk_ref, go_ref, gt_ref):
    tm = h_ref.shape[0]
    n_seg = tm // SEG
    blk = [pl.program_id(0) * n_seg + q for q in range(n_seg)]
    mods = [mod_ref[sel(bq)] for bq in blk]
    act = lambda x, m: _rmsnorm(x, g_ref[...]) * (1.0 + m[1:2]) + m[0:1]
    a_seg = [act(h_ref[q * SEG:(q + 1) * SEG, :], mods[q]) for q in range(n_seg)]
    ab = jnp.concatenate(a_seg, axis=0).astype(BF16)
    c1, c2 = NG + 2 * LANE, NG + 2 * LANE + NM
    c3 = c2 + 2 * HP
    go_ref[...] = _dot(ab, w_ref[:, c3:]).astype(go_ref.dtype)
    pg_ref[...] = _dot(ab, w_ref[:, 0:NG])
    pu_ref[...] = _dot(ab, w_ref[:, NG:c1])
    pm_ref[...] = _dot(ab, w_ref[:, c1:c2])
    gt_ref[...] = _dot_nt(wgt_ref[...], ab)
    ae = jnp.concatenate([act(hp_ref[...], mods[0])] + a_seg + [act(hn_ref[...], mods[-1])], axis=0)
    xe = _dot(ae.astype(BF16), w_ref[:, c2:c3])
    n_e = tm + 16
    row = lax.broadcasted_iota(jnp.int32, (tm, xe.shape[1]), 0)
    keep_prev = jnp.ones((tm, xe.shape[1]), F32)
    keep_next = keep_prev
    for q, bq in enumerate(blk):
        j = bq % ntb
        first = jnp.logical_or(j == 0, j == ncb).astype(F32)
        last = jnp.logical_or(j == ncb - 1, j == ntb - 1).astype(F32)
        keep_prev = jnp.where(row == q * SEG, 1.0 - first, keep_prev)
        keep_next = jnp.where(row == (q + 1) * SEG - 1, 1.0 - last, keep_next)
    xp = pltpu.roll(xe, 1, axis=0)[8:8 + tm] * keep_prev
    xn = pltpu.roll(xe, n_e - 1, axis=0)[8:8 + tm] * keep_next
    y = cw_ref[0:1] * xp + cw_ref[1:2] * xe[8:8 + tm] + cw_ref[2:3] * xn + cb_ref[...]
    qk_ref[...] = _silu(y) * post_ref[...]


def _proj(h, modtab, g, w, wgt, cw, cb, post, *, layer, rows):
    r, d = h.shape
    tm = PROJ_TM
    assert r % tm == 0
    t8 = tm // 8
    full = lambda a: pl.BlockSpec(a.shape, lambda i: (0,) * a.ndim)
    return pl.pallas_call(
        functools.partial(_proj_kernel, rows.ncb, rows.ntb, rows.sel(False)),
        out_shape=(jax.ShapeDtypeStruct((r, NG), F32), jax.ShapeDtypeStruct((r, 2 * LANE), F32),
                   jax.ShapeDtypeStruct((r, NM), F32), jax.ShapeDtypeStruct((r, 2 * HP), F32),
                   jax.ShapeDtypeStruct((r, NGO), BF16),
                   jax.ShapeDtypeStruct((16, r), F32)),
        grid=(r // tm,),
        in_specs=[pl.BlockSpec((tm, d), lambda i: (i, 0)),
                  pl.BlockSpec((8, d), lambda i: (jnp.maximum(i * t8 - 1, 0), 0)),
                  pl.BlockSpec((8, d), lambda i: (jnp.minimum((i + 1) * t8, r // 8 - 1), 0)),
                  pl.BlockSpec((None,) + modtab.shape[1:], lambda i: (layer, 0, 0, 0)),
                  full(g), full(w), full(wgt), full(cw), full(cb), full(post)],
        out_specs=(pl.BlockSpec((tm, NG), lambda i: (i, 0)),
                   pl.BlockSpec((tm, 2 * LANE), lambda i: (i, 0)),
                   pl.BlockSpec((tm, NM), lambda i: (i, 0)),
                   pl.BlockSpec((tm, 2 * HP), lambda i: (i, 0)),
                   pl.BlockSpec((tm, NGO), lambda i: (i, 0)),
                   pl.BlockSpec((16, tm), lambda i: (0, i))),
        compiler_params=_cparams(("arbitrary",)),
        name="proj",
    )(h, h, h, modtab, g, w, wgt, cw, cb, post)


def _scan_pos(d, s, ncb, ntb):
    rev = jnp.where(s < ncb, ncb - 1 - s, ntb - 1 - (s - ncb))
    return jnp.where(d == 0, s, rev)


def _scan_pos_static(rev, s, ncb, ntb):
    if not rev:
        return s
    return jnp.where(s < ncb, ncb - 1 - s, ntb - 1 - (s - ncb))


def _tri(rev):
    r = lax.broadcasted_iota(jnp.int32, (CHUNK, CHUNK), 0)
    c = lax.broadcasted_iota(jnp.int32, (CHUNK, CHUNK), 1)
    return (r <= c) if rev else (r >= c)


def _chunk_rows(rev):
    return [(N_SUB - 1 - j if rev else j) * CHUNK for j in range(N_SUB)]


def _gla_kernel(rev, p_ref, wa_ref, ba_ref, o_ref, st_ref):
    s, g = pl.program_id(0), pl.program_id(1)
    nbb = p_ref.shape[0]
    b0 = g * nbb

    @pl.when(s == 0)
    def _():
        st_ref[pl.ds(b0, nbb)] = jnp.zeros((nbb,) + st_ref.shape[1:], F32)

    valid = _tri(rev)
    tri = valid.astype(F32)
    r0s = _chunk_rows(rev)
    wa, ba = wa_ref[...], ba_ref[...]
    inst = [(bb, j) for bb in range(nbb) for j in range(N_SUB)]
    heads = [slice(h * LANE, (h + 1) * LANE) for h in range(GLA_HEADS)]
    pairs = [slice((h // 2) * LANE, (h // 2 + 1) * LANE) for h in range(GLA_HEADS)]
    lane = lax.broadcasted_iota(jnp.int32, (CHUNK, LANE), 1)
    own = [(lane // GLA_KP) == (h % 2) for h in range(GLA_HEADS)]

    la = {}
    for bb, j in inst:
        lr = p_ref[bb, pl.ds(r0s[j], CHUNK), 2 * GQ + HP:NG]
        la[bb, j] = _log_sigmoid(_dot(lr, wa) + ba) * (1.0 / GLA_GATE_NORM)
    bc, e_last = {}, {}
    for i in inst:
        bc[i] = _dot_exact01(tri, la[i], lhs_is_01=True, pieces=2)
        e_last[i] = jnp.exp(jnp.sum(la[i], axis=0, keepdims=True))
    q_in, k_in, k_out, v = {}, {}, {}, {}
    for bb, j in inst:
        i = (bb, j)
        rs = pl.ds(r0s[j], CHUNK)
        qs = (p_ref[bb, rs, 0:GQ] * (GLA_DK ** -0.5) * jnp.exp(bc[i])).astype(BF16)
        for h in range(GLA_HEADS):
            q_in[i, h] = jnp.where(own[h], qs[:, pairs[h]], jnp.zeros_like(qs[:, pairs[h]]))
        kd = p_ref[bb, rs, GQ:2 * GQ] * jnp.exp(-bc[i])
        k_out[i] = (kd * e_last[i]).astype(BF16)
        k_in[i] = kd.astype(BF16)
        v[i] = p_ref[bb, rs, 2 * GQ:2 * GQ + HP].astype(BF16)
    att = {}
    for i in inst:
        for h in range(GLA_HEADS):
            att[i, h] = jnp.where(valid, _dot_nt(q_in[i, h], k_in[i][:, pairs[h]]), 0.0).astype(BF16)
    o_intra, ds = {}, {}
    for i in inst:
        for h, sl in enumerate(heads):
            o_intra[i, h] = _dot(att[i, h], v[i][:, sl])
            ds[i, h] = _dot_tn(v[i][:, sl], k_out[i][:, pairs[h]])
    s_in = {}
    for bb in range(nbb):
        for h in range(GLA_HEADS):
            st = st_ref[b0 + bb, h]
            for j in range(N_SUB):
                s_in[(bb, j), h] = st.astype(BF16)
                st = st * e_last[bb, j][:, pairs[h]] + ds[(bb, j), h]
            st_ref[b0 + bb, h] = st
    for bb, j in inst:
        for h, sl in enumerate(heads):
            o = o_intra[(bb, j), h] + _dot_nt(q_in[(bb, j), h], s_in[(bb, j), h])
            o_ref[bb, pl.ds(r0s[j], CHUNK), sl] = o.astype(o_ref.dtype)


def _gla(pg3, wa, ba, *, rows, rev):
    nb, l, _ = pg3.shape
    nbb = SCAN_BATCHES
    pos = functools.partial(_scan_pos_static, rev, ncb=rows.ncb, ntb=rows.ntb)
    return pl.pallas_call(
        functools.partial(_gla_kernel, rev),
        out_shape=jax.ShapeDtypeStruct((nb, l, HP), BF16),
        grid=(rows.ntb, nb // nbb),
        in_specs=[pl.BlockSpec((nbb, SEG, NG), lambda s, g: (g, pos(s), 0)),
                  pl.BlockSpec((LANE, GQ), lambda s, g: (0, 0)),
                  pl.BlockSpec((1, GQ), lambda s, g: (0, 0))],
        out_specs=pl.BlockSpec((nbb, SEG, HP), lambda s, g: (g, pos(s), 0)),
        scratch_shapes=[pltpu.VMEM((nb, GLA_HEADS, LANE, LANE), F32)],
        compiler_params=_cparams(("arbitrary", "arbitrary")),
        name="gla_scan_bwd" if rev else "gla_scan_fwd",
    )(pg3, wa, ba)


def _split_bf16(x, n):
    parts, r = [], x
    for _ in range(n):
        p = r.astype(BF16)
        parts.append(p)
        r = r - p.astype(F32)
    return parts


def _dot_exact01(a, b, lhs_is_01, pieces=3):
    if lhs_is_01:
        a = a.astype(BF16)
        terms = [_dot(a, p) for p in _split_bf16(b, pieces)]
    else:
        b = b.astype(BF16)
        terms = [_dot(p, b) for p in _split_bf16(a, pieces)]
    return functools.reduce(lambda x, y: x + y, terms)


def _cummax_rows(a, rev):
    n = a.shape[0]
    row = lax.broadcasted_iota(jnp.int32, a.shape, 0)
    k = 1
    while k < n:
        if rev:
            sh = jnp.where(row < n - k, pltpu.roll(a, n - k, axis=0), NEG)
        else:
            sh = jnp.where(row >= k, pltpu.roll(a, k, axis=0), NEG)
        a = jnp.maximum(a, sh)
        k *= 2
    return a


ML_GL = ML_HEADS


def _mlstm_kernel(rev, qk_ref, v_ref, g_ref, gt_ref, gbr_ref, gbt_ref, o_ref, st_ref, m_ref):
    s, g = pl.program_id(0), pl.program_id(1)
    nbb = qk_ref.shape[0]
    b0 = g * nbb

    @pl.when(s == 0)
    def _():
        st_ref[pl.ds(b0, nbb)] = jnp.zeros((nbb,) + st_ref.shape[1:], F32)
        m_ref[pl.ds(b0, nbb)] = jnp.zeros((nbb,) + m_ref.shape[1:], F32)

    valid = _tri(rev)
    tri = valid.astype(F32)
    r0s = _chunk_rows(rev)
    cs = [r // CHUNK for r in r0s]
    last = 0 if rev else CHUNK - 1
    inst = [(bb, j) for bb in range(nbb) for j in range(N_SUB)]
    heads = [slice(h * LANE, (h + 1) * LANE) for h in range(ML_HEADS)]

    r_sel = lax.broadcasted_iota(jnp.int32, (LANE, HP), 0)
    c_sel = lax.broadcasted_iota(jnp.int32, (LANE, HP), 1)
    sel_h = (r_sel == ML_GL + c_sel // LANE).astype(BF16)
    r_t = lax.broadcasted_iota(jnp.int32, (HP, HP), 0)
    c_t = lax.broadcasted_iota(jnp.int32, (HP, HP), 1)
    same = jnp.logical_and(r_t // LANE == c_t // LANE, jnp.logical_and(r_t % LANE < CHUNK, c_t % LANE < CHUNK))
    before = (r_t % LANE >= c_t % LANE) if rev else (r_t % LANE <= c_t % LANE)
    tri_b = jnp.logical_and(same, before).astype(BF16)
    r_v = lax.broadcasted_iota(jnp.int32, (CHUNK, HP), 0)
    c_v = lax.broadcasted_iota(jnp.int32, (CHUNK, HP), 1) % LANE
    valid4 = jnp.logical_and(c_v < CHUNK, (r_v <= c_v) if rev else (r_v >= c_v))
    lane4 = lax.broadcasted_iota(jnp.int32, (CHUNK, HP), 1) % LANE
    lane_c = lax.broadcasted_iota(jnp.int32, (CHUNK, LANE), 1)
    lane1 = lax.broadcasted_iota(jnp.int32, (1, LANE), 1)
    head_lane = jnp.logical_and(lane1 >= ML_GL, lane1 < ML_GL + ML_HEADS)
    gbr, gbt = gbr_ref[...], gbt_ref[...]

    gcs, fcm, cmx, a_row, grt = {}, {}, {}, {}, {}
    for bb, j in inst:
        gc = g_ref[bb, pl.ds(r0s[j], CHUNK), :] + gbr
        gcs[bb, j] = pltpu.roll(gc, ML_GL, axis=1)
        fcm[bb, j] = _dot_exact01(tri, _log_sigmoid(gc), lhs_is_01=True)
        grt[bb, j] = gt_ref[bb, cs[j]] + gbt
    row_id = lax.broadcasted_iota(jnp.int32, (len(inst), HP), 0)
    lfr = jnp.zeros((len(inst), HP), F32)
    for n, i in enumerate(inst):
        lfr = jnp.where(row_id == n, _log_sigmoid(grt[i][1:2]), lfr)
    fcr = _dot_exact01(lfr, tri_b, lhs_is_01=False)
    for n, i in enumerate(inst):
        a_row[i] = grt[i][0:1] - fcr[n:n + 1]
        cmx[i] = _cummax_rows(gcs[i] - fcm[i], rev)
    bx = {}
    e_neg, gd = {}, {}
    for bb in range(nbb):
        m_prev = m_ref[b0 + bb, 0:1, :]
        for j in range(N_SUB):
            i = (bb, j)
            m_t = fcm[i] + jnp.maximum(m_prev, cmx[i])
            m_new = m_t[last:last + 1]
            f_tot = fcm[i][last:last + 1]
            u = fcm[i] - m_t
            w_prev = jnp.exp(u + m_prev)
            w_s = jnp.exp(f_tot - fcm[i] + gcs[i] - m_new)
            gdec = jnp.broadcast_to(jnp.exp(f_tot + m_prev - m_new), (16, LANE))
            e_neg[i] = jnp.exp(-m_t)
            keep = lambda a: jnp.where(head_lane, a, 0.0)
            bx[i] = jnp.concatenate(_split_bf16(keep(u), 2) + _split_bf16(keep(w_prev), 1)
                                    + _split_bf16(keep(w_s), 1) + _split_bf16(keep(gdec), 2), axis=0)
            m_prev = m_new
        m_ref[b0 + bb] = jnp.broadcast_to(m_prev, (8, LANE))
    ub, wpb, wsb, gdb = {}, {}, {}, {}
    for i in inst:
        y = _dot(bx[i], sel_h)
        c = CHUNK
        ub[i] = y[0:c] + y[c:2 * c]
        wpb[i] = y[2 * c:3 * c]
        wsb[i] = y[3 * c:4 * c]
        gdb[i] = y[4 * c:4 * c + 1] + y[4 * c + 16:4 * c + 17]
    q, v, qkw, ds = {}, {}, {}, {}
    for bb, j in inst:
        i = (bb, j)
        rs = pl.ds(r0s[j], CHUNK)
        w = jnp.where(valid4, jnp.exp(ub[i] + a_row[i]), 0.0)
        q[i] = qk_ref[bb, rs, 0:HP].astype(BF16)
        k = qk_ref[bb, rs, HP:2 * HP]
        kb = k.astype(BF16)
        kw = (k * wsb[i]).astype(BF16)
        v[i] = jnp.where(lane4 == ML_DH, 1.0, v_ref[bb, rs, :]).astype(BF16)
        for h, sl in enumerate(heads):
            sc = _dot_nt(q[i][:, sl], kb[:, sl])
            qkw[i, h] = (sc * w[:, h * LANE:h * LANE + CHUNK]).astype(BF16)
            ds[i, h] = _dot_tn(v[i][:, sl], kw[:, sl])
    s_in = {}
    for bb in range(nbb):
        for h, sl in enumerate(heads):
            st = st_ref[b0 + bb, h]
            for j in range(N_SUB):
                i = (bb, j)
                s_in[i, h] = st.astype(BF16)
                st = gdb[i][:, sl] * st + ds[i, h]
            st_ref[b0 + bb, h] = st
    num = {}
    for i in inst:
        parts = [_dot_nt(q[i][:, sl], s_in[i, h]) for h, sl in enumerate(heads)]
        intra = [_dot(qkw[i, h], v[i][:, sl]) for h, sl in enumerate(heads)]
        num[i] = wpb[i] * jnp.concatenate(parts, axis=1) + jnp.concatenate(intra, axis=1)
    for bb, j in inst:
        i = (bb, j)
        den = jnp.zeros((CHUNK, LANE), F32)
        for h, sl in enumerate(heads):
            dh = jnp.sum(jnp.where(lane_c == ML_DH, num[i][:, sl], 0.0), axis=-1, keepdims=True)
            den = jnp.where(lane_c == ML_GL + h, jnp.broadcast_to(dh, (CHUNK, LANE)), den)
        rb = _dot(jnp.where(head_lane, 1.0 / jnp.maximum(jnp.abs(den), e_neg[i]), 0.0).astype(BF16), sel_h)
        o_ref[bb, pl.ds(r0s[j], CHUNK), :] = jnp.where(lane4 < ML_DH, num[i] * rb, 0.0).astype(o_ref.dtype)


def _mlstm(qk3, pm3, gtl, gbr, gbt, *, rows, rev):
    nb, l, _ = qk3.shape
    nbb = SCAN_BATCHES
    dr = 1 if rev else 0
    pos = functools.partial(_scan_pos_static, rev, ncb=rows.ncb, ntb=rows.ntb)
    return pl.pallas_call(
        functools.partial(_mlstm_kernel, rev),
        out_shape=jax.ShapeDtypeStruct((nb, l, HP), BF16),
        grid=(rows.ntb, nb // nbb),
        in_specs=[pl.BlockSpec((nbb, SEG, 2 * HP), lambda s, g: (g, pos(s), 0)),
                  pl.BlockSpec((nbb, SEG, HP), lambda s, g: (g, pos(s), 0)),
                  pl.BlockSpec((nbb, SEG, LANE), lambda s, g: (g, pos(s), HP // LANE + dr)),
                  pl.BlockSpec((nbb, N_SUB, None, 2, HP), lambda s, g: (g, pos(s), dr, 0, 0)),
                  pl.BlockSpec((1, LANE), lambda s, g: (0, 0)),
                  pl.BlockSpec((2, HP), lambda s, g: (0, 0))],
        out_specs=pl.BlockSpec((nbb, SEG, HP), lambda s, g: (g, pos(s), 0)),
        scratch_shapes=[pltpu.VMEM((nb, ML_HEADS, LANE, LANE), F32), pltpu.VMEM((nb, 8, LANE), F32)],
        compiler_params=_cparams(("arbitrary", "arbitrary")),
        name="mlstm_scan_bwd" if rev else "mlstm_scan_fwd",
    )(qk3, pm3, pm3, gtl, gbr, gbt)


def _s5_kernel(nb, u_ref, bre_ref, bim_ref, are_ref, aim_ref, cre_ref, cim_ref, o_ref, xr_ref, xi_ref, st_ref):
    d, s = pl.program_id(0), pl.program_id(1)

    @pl.when(s == 0)
    def _():
        st_ref[...] = jnp.zeros(st_ref.shape, F32)

    n_sub = u_ref.shape[0] // (S5_SUB * nb)
    sub_rows = S5_SUB * nb

    def run(rev):
        ar, ai = are_ref[0], aim_ref[0]
        order = list(range(n_sub))[::-1] if rev else list(range(n_sub))

        def project_in(q):
            rs = slice(q * sub_rows, (q + 1) * sub_rows)
            u = u_ref[rs, :].astype(BF16)
            xr_ref[rs, :] = _dot(u, bre_ref[0])
            xi_ref[rs, :] = _dot(u, bim_ref[0])

        sr, si = st_ref[0], st_ref[1]
        project_in(order[0])
        for n, q in enumerate(order):
            if n + 1 < n_sub:
                project_in(order[n + 1])
            for j in range(S5_SUB):
                r0 = (q * S5_SUB + (S5_SUB - 1 - j if rev else j)) * nb
                nr = ar * sr - ai * si + xr_ref[r0:r0 + nb, :]
                ni = ar * si + ai * sr + xi_ref[r0:r0 + nb, :]
                xr_ref[r0:r0 + nb, :] = nr
                xi_ref[r0:r0 + nb, :] = ni
                sr, si = nr, ni
            rs = slice(q * sub_rows, (q + 1) * sub_rows)
            o_ref[0, rs, :] = (_dot(xr_ref[rs, :].astype(BF16), cre_ref[...])
                               - _dot(xi_ref[rs, :].astype(BF16), cim_ref[...])).astype(o_ref.dtype)
        st_ref[0] = sr
        st_ref[1] = si

    @pl.when(d == 0)
    def _():
        run(False)

    @pl.when(d == 1)
    def _():
        run(True)


def _s5(ut, bre, bim, are, aim, cre, cim, *, nb, nc, nt):
    n_rows, ch = ut.shape
    tr = S5_STEPS * nb
    ns = bre.shape[-1]
    pos = functools.partial(_scan_pos, ncb=nc, ntb=nt)
    return pl.pallas_call(
        functools.partial(_s5_kernel, nb),
        out_shape=jax.ShapeDtypeStruct((2, n_rows, ch), BF16),
        grid=(2, nt),
        in_specs=[pl.BlockSpec((tr, ch), lambda d, s: (pos(d, s), 0)),
                  pl.BlockSpec((1, ch, ns), lambda d, s: (d, 0, 0)),
                  pl.BlockSpec((1, ch, ns), lambda d, s: (d, 0, 0)),
                  pl.BlockSpec((1, nb, ns), lambda d, s: (d, 0, 0)),
                  pl.BlockSpec((1, nb, ns), lambda d, s: (d, 0, 0)),
                  pl.BlockSpec((ns, ch), lambda d, s: (0, 0)),
                  pl.BlockSpec((ns, ch), lambda d, s: (0, 0))],
        out_specs=pl.BlockSpec((1, tr, ch), lambda d, s: (d, pos(d, s), 0)),
        scratch_shapes=[pltpu.VMEM((tr, ns), F32), pltpu.VMEM((tr, ns), F32), pltpu.VMEM((2, nb, ns), F32)],
        compiler_params=_cparams(("arbitrary", "arbitrary")),
        name="s5_scan",
    )(ut, bre, bim, are, aim, cre, cim)


def _head_norm(o, gain, dim):
    parts = []
    for h in range(o.shape[1] // LANE):
        seg = o[:, h * LANE:(h + 1) * LANE]
        ms = jnp.sum(seg * seg, axis=-1, keepdims=True) * (1.0 / dim)
        parts.append(seg * lax.rsqrt(ms + EPS))
    return jnp.concatenate(parts, axis=1) * gain


def _mix_kernel(with_router, ogf_ref, ogb_ref, gg_ref, ys_ref, u_ref, omf_ref, omb_ref, mo_ref, h_ref, mod_ref,
                gn_ref, mn_ref, sd_ref, gw_ref, gb_ref, wo_ref, n2_ref, *rest):
    if with_router:
        wr_ref, ho_ref, f_ref, rt_ref = rest
    else:
        ho_ref, f_ref = rest
    gla = _head_norm(ogf_ref[...].astype(F32) + ogb_ref[...].astype(F32), gn_ref[...], GLA_DV) * _silu(gg_ref[...].astype(F32))
    z = _gelu_tanh(ys_ref[0].astype(F32) + ys_ref[1].astype(F32) + sd_ref[...] * u_ref[...])
    s5 = z * jax.nn.sigmoid(_dot(z.astype(BF16), gw_ref[...]) + gb_ref[...])
    ml = _head_norm(omf_ref[...].astype(F32) + omb_ref[...].astype(F32), mn_ref[...], ML_DH) * jax.nn.sigmoid(mo_ref[...].astype(F32))
    mix = (_dot(gla.astype(BF16), wo_ref[0:HP]) + _dot(s5.astype(BF16), wo_ref[HP:HP + 2 * LANE])
           + _dot(ml.astype(BF16), wo_ref[HP + 2 * LANE:]))
    m = mod_ref[0]
    hn = h_ref[...] + m[2:3] * mix
    ho_ref[...] = hn
    f = _rmsnorm(hn, n2_ref[...]) * (1.0 + m[4:5]) + m[3:4]
    if not with_router:
        f_ref[...] = f.astype(f_ref.dtype)
    if with_router:
        f_hi, f_lo = _split_bf16(f, 2)
        logits = (_dot(f_hi, wr_ref[0]) + _dot(f_lo, wr_ref[0]) + _dot(f_hi, wr_ref[1])).T[0:2 * N_EXPERTS]
        row = lax.broadcasted_iota(jnp.int32, logits.shape, 0)
        l0 = jnp.where(row < N_EXPERTS, logits, NEG)
        m1 = jnp.max(l0, axis=0, keepdims=True)
        i1 = jnp.min(jnp.where(l0 == m1, row, 2 * N_EXPERTS), axis=0, keepdims=True)
        l1 = jnp.where(row == i1, NEG, l0)
        m2 = jnp.max(l1, axis=0, keepdims=True)
        i2 = jnp.min(jnp.where(l1 == m2, row, 2 * N_EXPERTS), axis=0, keepdims=True)
        e = jnp.exp(m2 - m1)
        w1 = 1.0 / (1.0 + e)
        w2 = e / (1.0 + e)
        rt = jnp.where(row == 0, i1.astype(F32),
                       jnp.where(row == 1, i2.astype(F32),
                                 jnp.where(row == 2, w1, jnp.where(row == 3, w2, 0.0))))
        rt_ref[...] = rt
        d = f.shape[1]
        f_ref[:, 0:d // 2] = _pack_bf16_pairs(f)
        f_ref[:, d // 2:] = jnp.concatenate([rt, jnp.zeros((LANE - rt.shape[0], rt.shape[1]), F32)], axis=0).T


def _mix(ogf, ogb, pg, ys, pu, omf, omb, pm, h, modtab, gn, mn, sd, gw, gb, wo, n2, wr, *, layer, rows, lat_only):
    d = h.shape[1]
    tm = SEG
    n = rows.n_blocks(lat_only)
    src, sel = rows.src(lat_only), rows.sel(lat_only)
    full = lambda a: pl.BlockSpec(a.shape, lambda i: (0,) * a.ndim)
    with_router = wr is not None
    in_specs = [pl.BlockSpec((tm, HP), lambda i: (src(i), 0)),
                pl.BlockSpec((tm, HP), lambda i: (src(i), 0)),
                pl.BlockSpec((tm, HP), lambda i: (src(i), 0)),
                pl.BlockSpec((2, tm, 2 * LANE), lambda i: (0, src(i), 0)),
                pl.BlockSpec((tm, 2 * LANE), lambda i: (src(i), 0)),
                pl.BlockSpec((tm, HP), lambda i: (src(i), 0)),
                pl.BlockSpec((tm, HP), lambda i: (src(i), 0)),
                pl.BlockSpec((tm, HP), lambda i: (src(i), 1)),
                pl.BlockSpec((tm, d), lambda i: (src(i), 0)),
                pl.BlockSpec((None, 1, 8, d), lambda i: (layer, sel(i), 0, 0)),
                full(gn), full(mn), full(sd), full(gw), full(gb), full(wo), full(n2)]
    args = [ogf, ogb, pg, ys, pu, omf, omb, pm, h, modtab, gn, mn, sd, gw, gb, wo, n2]
    fw, fdt = (d // 2 + LANE, F32) if with_router else (d, BF16)
    out_shape = [jax.ShapeDtypeStruct((n * tm, d), F32), jax.ShapeDtypeStruct((n * tm, fw), fdt)]
    out_specs = [pl.BlockSpec((tm, d), lambda i: (i, 0)), pl.BlockSpec((tm, fw), lambda i: (i, 0))]
    if with_router:
        in_specs.append(full(wr))
        args.append(wr)
        out_shape.append(jax.ShapeDtypeStruct((2 * N_EXPERTS, n * tm), F32))
        out_specs.append(pl.BlockSpec((2 * N_EXPERTS, tm), lambda i: (0, i)))
    return pl.pallas_call(
        functools.partial(_mix_kernel, with_router),
        out_shape=tuple(out_shape),
        grid=(n,),
        in_specs=in_specs,
        out_specs=tuple(out_specs),
        compiler_params=_cparams(("arbitrary",)),
        name="mix_out",
    )(*args)


FF_TILE = 256


def _swiglu(xb, w1_ref, w3_ref, w2_ref, a_ref, lead=()):
    dff = w1_ref.shape[-1]
    for j in range(dff // FF_TILE):
        sl = slice(j * FF_TILE, (j + 1) * FF_TILE)
        h1 = _dot(xb, w1_ref[lead + (slice(None), sl)])
        h3 = _dot(xb, w3_ref[lead + (slice(None), sl)])
        a_ref[:, sl] = (_silu(h1) * h3).astype(BF16)
    return _dot(a_ref[...], w2_ref[lead + (slice(None), slice(None))])


FFN_TM = 1024
MIXFFN_TM = 2 * SEG


def _mix_ffn_kernel(sel, ogf_ref, ogb_ref, gg_ref, ys_ref, u_ref, omf_ref, omb_ref, mo_ref, h_ref, mod_ref,
                    gn_ref, mn_ref, sd_ref, gw_ref, gb_ref, wo_ref, n2_ref, w1_ref, w3_ref, w2_ref, o_ref,
                    a_ref, hn_ref, f_ref):
    n_seg = h_ref.shape[0] // SEG
    mods = []
    for q in range(n_seg):
        rs = slice(q * SEG, (q + 1) * SEG)
        m = mod_ref[sel(pl.program_id(0) * n_seg + q)]
        mods.append(m)
        gla = (_head_norm(ogf_ref[rs, :].astype(F32) + ogb_ref[rs, :].astype(F32), gn_ref[...], GLA_DV)
               * _silu(gg_ref[rs, :].astype(F32)))
        z = _gelu_tanh(ys_ref[0, rs, :].astype(F32) + ys_ref[1, rs, :].astype(F32) + sd_ref[...] * u_ref[rs, :])
        s5 = z * jax.nn.sigmoid(_dot(z.astype(BF16), gw_ref[...]) + gb_ref[...])
        ml = (_head_norm(omf_ref[rs, :].astype(F32) + omb_ref[rs, :].astype(F32), mn_ref[...], ML_DH)
              * jax.nn.sigmoid(mo_ref[rs, :].astype(F32)))
        mix = (_dot(gla.astype(BF16), wo_ref[0:HP]) + _dot(s5.astype(BF16), wo_ref[HP:HP + 2 * LANE])
               + _dot(ml.astype(BF16), wo_ref[HP + 2 * LANE:]))
        hn = h_ref[rs, :] + m[2:3] * mix
        hn_ref[rs, :] = hn
        f_ref[rs, :] = (_rmsnorm(hn, n2_ref[...]) * (1.0 + m[4:5]) + m[3:4]).astype(BF16)
    y = _swiglu(f_ref[...], w1_ref, w3_ref, w2_ref, a_ref)
    for q in range(n_seg):
        rs = slice(q * SEG, (q + 1) * SEG)
        o_ref[rs, :] = hn_ref[rs, :] + mods[q][5:6] * y[rs]


def _mix_ffn(ogf, ogb, pg, ys, pu, omf, omb, pm, h, modtab, gn, mn, sd, gw, gb, wo, n2, w1, w3, w2, *, layer, rows):
    r, d = h.shape
    tm = MIXFFN_TM
    assert r % tm == 0
    dff = w1.shape[-1]
    full = lambda a: pl.BlockSpec(a.shape, lambda i: (0,) * a.ndim)
    resident = lambda a: pl.BlockSpec(a.shape, lambda i: (0,) * a.ndim, pipeline_mode=pl.Buffered(1))
    in_specs = [pl.BlockSpec((tm, HP), lambda i: (i, 0)),
                pl.BlockSpec((tm, HP), lambda i: (i, 0)),
                pl.BlockSpec((tm, HP), lambda i: (i, 0)),
                pl.BlockSpec((2, tm, 2 * LANE), lambda i: (0, i, 0)),
                pl.BlockSpec((tm, 2 * LANE), lambda i: (i, 0)),
                pl.BlockSpec((tm, HP), lambda i: (i, 0)),
                pl.BlockSpec((tm, HP), lambda i: (i, 0)),
                pl.BlockSpec((tm, HP), lambda i: (i, 1)),
                pl.BlockSpec((tm, d), lambda i: (i, 0)),
                pl.BlockSpec((None,) + modtab.shape[1:], lambda i: (layer, 0, 0, 0)),
                full(gn), full(mn), full(sd), full(gw), full(gb), resident(wo), full(n2),
                resident(w1), resident(w3), resident(w2)]
    return pl.pallas_call(
        functools.partial(_mix_ffn_kernel, rows.sel(False)),
        out_shape=jax.ShapeDtypeStruct((r, d), F32),
        grid=(r // tm,),
        in_specs=in_specs,
        out_specs=pl.BlockSpec((tm, d), lambda i: (i, 0)),
        scratch_shapes=[pltpu.VMEM((tm, dff), BF16), pltpu.VMEM((tm, d), F32), pltpu.VMEM((tm, d), BF16)],
        compiler_params=_cparams(("arbitrary",)),
        name="mix_ffn",
    )(ogf, ogb, pg, ys, pu, omf, omb, pm, h, modtab, gn, mn, sd, gw, gb, wo, n2, w1, w3, w2)


def _ffn_kernel(final, sel, f_ref, h_ref, mod_ref, w1_ref, w3_ref, w2_ref, *rest):
    if final:
        nf_ref, o_ref, a_ref = rest
    else:
        o_ref, a_ref = rest
    y = _swiglu(f_ref[...], w1_ref, w3_ref, w2_ref, a_ref)
    n_seg = f_ref.shape[0] // SEG
    for q in range(n_seg):
        rs = slice(q * SEG, (q + 1) * SEG)
        gate = mod_ref[sel(pl.program_id(0) * n_seg + q)][5:6]
        hn = h_ref[rs, :] + gate * y[rs]
        o_ref[rs, :] = _rmsnorm(hn, nf_ref[...]) if final else hn


def _ffn(f, h, modtab, w1, w3, w2, nf, *, layer, rows, lat_only):
    r, d = h.shape
    tm = FFN_TM
    assert r % tm == 0
    dff = w1.shape[-1]
    full = lambda a: pl.BlockSpec(a.shape, lambda i: (0,) * a.ndim)
    resident = lambda a: pl.BlockSpec(a.shape, lambda i: (0,) * a.ndim, pipeline_mode=pl.Buffered(1))
    in_specs = [pl.BlockSpec((tm, d), lambda i: (i, 0)),
                pl.BlockSpec((tm, d), lambda i: (i, 0)),
                pl.BlockSpec((None,) + modtab.shape[1:], lambda i: (layer, 0, 0, 0)),
                resident(w1), resident(w3), resident(w2)]
    args = [f, h, modtab, w1, w3, w2]
    if lat_only:
        in_specs.append(full(nf))
        args.append(nf)
    return pl.pallas_call(
        functools.partial(_ffn_kernel, lat_only, rows.sel(lat_only)),
        out_shape=jax.ShapeDtypeStruct((r, d), F32),
        grid=(r // tm,),
        in_specs=in_specs,
        out_specs=pl.BlockSpec((tm, d), lambda i: (i, 0)),
        scratch_shapes=[pltpu.VMEM((tm, dff), BF16)],
        compiler_params=_cparams(("arbitrary",)),
        name="ffn",
    )(*args)


MOE_TM = 512


def _moe_kernel(be_ref, live_ref, x_ref, w1_ref, w3_ref, w2_ref, o_ref, a_ref):
    i = pl.program_id(0)
    d = w1_ref.shape[1]

    @pl.when(live_ref[i] > 0)
    def _():
        xa, xb = _unpack_bf16_pairs(x_ref[:, 0:d // 2])
        x = jnp.concatenate([xa.astype(BF16), xb.astype(BF16)], axis=1)
        y = _swiglu(x, w1_ref, w3_ref, w2_ref, a_ref, lead=(0,))
        tail = x_ref[:, d // 2:]
        mine = tail[:, 0:1] == be_ref[i].astype(F32)
        o_ref[...] = _pack_bf16_pairs(y * jnp.where(mine, tail[:, 2:3], tail[:, 3:4]))

    @pl.when(live_ref[i] == 0)
    def _():
        o_ref[...] = jnp.zeros(o_ref.shape, F32)


def _moe_experts(block_expert, block_live, xg, w1, w3, w2):
    n_rows, dx = xg.shape
    d = w1.shape[-2]
    dff = w1.shape[-1]
    tm = MOE_TM
    return pl.pallas_call(
        _moe_kernel,
        out_shape=jax.ShapeDtypeStruct((n_rows, d // 2), F32),
        grid_spec=pltpu.PrefetchScalarGridSpec(
            num_scalar_prefetch=2,
            grid=(n_rows // tm,),
            in_specs=[pl.BlockSpec((tm, dx), lambda i, be, lv: (i, 0)),
                      pl.BlockSpec((1, d, dff), lambda i, be, lv: (be[i], 0, 0)),
                      pl.BlockSpec((1, d, dff), lambda i, be, lv: (be[i], 0, 0)),
                      pl.BlockSpec((1, dff, d), lambda i, be, lv: (be[i], 0, 0))],
            out_specs=pl.BlockSpec((tm, d // 2), lambda i, be, lv: (i, 0)),
            scratch_shapes=[pltpu.VMEM((tm, dff), BF16)]),
        compiler_params=_cparams(("arbitrary",)),
        name="moe_experts",
    )(block_expert, block_live, xg, w1, w3, w2)


SC_GATHER_ROWS = 64


def _gather_rows(table, idx):
    n_k, n_idx = idx.shape
    _, d = table.shape
    info = plsc.get_sparse_core_info()
    n_cores, n_workers = info.num_cores, info.num_cores * info.num_subcores
    assert n_idx % (n_workers * SC_GATHER_ROWS) == 0
    per_worker = n_idx // n_workers
    mesh = plsc.VectorSubcoreMesh(core_axis_name="c", subcore_axis_name="s")

    @functools.partial(
        pl.kernel, mesh=mesh,
        out_type=jax.ShapeDtypeStruct((n_k * n_idx, d), table.dtype),
        scratch_types=[pltpu.VMEM((SC_GATHER_ROWS,), jnp.int32),
                       pltpu.VMEM((SC_GATHER_ROWS, d), table.dtype),
                       pltpu.SemaphoreType.DMA])
    def gather(table_hbm, idx_hbm, out_hbm, idx_v, rows_v, sem):
        base = (lax.axis_index("s") * n_cores + lax.axis_index("c")) * per_worker

        @pl.loop(0, per_worker // SC_GATHER_ROWS)
        def _(it):
            for k in range(n_k):
                off = pl.multiple_of(k * n_idx + base + it * SC_GATHER_ROWS, SC_GATHER_ROWS)
                pltpu.sync_copy(idx_hbm.at[pl.ds(off, SC_GATHER_ROWS)], idx_v)
                pltpu.async_copy(table_hbm.at[idx_v], rows_v, sem).wait()
                pltpu.sync_copy(rows_v, out_hbm.at[pl.ds(off, SC_GATHER_ROWS)])

    return gather(table, idx.reshape(-1)).reshape(n_k, n_idx, d)


def _scatter_rows(src, dest, n_rows):
    n_tok, d = src.shape
    info = plsc.get_sparse_core_info()
    n_cores, n_workers = info.num_cores, info.num_cores * info.num_subcores
    assert n_tok % (n_workers * SC_GATHER_ROWS) == 0
    per_worker = n_tok // n_workers
    mesh = plsc.VectorSubcoreMesh(core_axis_name="c", subcore_axis_name="s")

    @functools.partial(
        pl.kernel, mesh=mesh,
        out_type=jax.ShapeDtypeStruct((n_rows, d), src.dtype),
        scratch_types=[pltpu.VMEM((SC_GATHER_ROWS,), jnp.int32),
                       pltpu.VMEM((SC_GATHER_ROWS, d), src.dtype),
                       pltpu.SemaphoreType.DMA])
    def scatter(src_hbm, dest_hbm, out_hbm, idx_v, rows_v, sem):
        base = (lax.axis_index("s") * n_cores + lax.axis_index("c")) * per_worker

        @pl.loop(0, per_worker // SC_GATHER_ROWS)
        def _(it):
            off = pl.multiple_of(base + it * SC_GATHER_ROWS, SC_GATHER_ROWS)
            pltpu.sync_copy(src_hbm.at[pl.ds(off, SC_GATHER_ROWS)], rows_v)
            for k in range(TOP_K):
                pltpu.sync_copy(dest_hbm.at[pl.ds(k * n_tok + off, SC_GATHER_ROWS)], idx_v)
                pltpu.async_copy(rows_v, out_hbm.at[idx_v], sem).wait()

    return scatter(src, dest.reshape(-1))


def _pack_bf16_pairs(y):
    n = y.shape[1] // 2
    hi = pltpu.bitcast(y[:, :n].astype(BF16).astype(F32), jnp.uint32)
    lo = pltpu.bitcast(y[:, n:].astype(BF16).astype(F32), jnp.uint32)
    return pltpu.bitcast(hi | (lo >> 16), F32)


def _unpack_bf16_pairs(w):
    u = pltpu.bitcast(w, jnp.uint32)
    return pltpu.bitcast(u & jnp.uint32(0xFFFF0000), F32), pltpu.bitcast(u << 16, F32)


def _resid_kernel(final, h_ref, y_ref, mod_ref, *rest):
    a0, b0 = _unpack_bf16_pairs(y_ref[0])
    a1, b1 = _unpack_bf16_pairs(y_ref[1])
    hn = h_ref[...] + mod_ref[0][5:6] * jnp.concatenate([a0 + a1, b0 + b1], axis=1)
    if final:
        nf_ref, o_ref = rest
        o_ref[...] = _rmsnorm(hn, nf_ref[...])
    else:
        (o_ref,) = rest
        o_ref[...] = hn


def _moe_resid(h, y01, modtab, nf, *, layer, rows, lat_only):
    r, d = h.shape
    tm = SEG
    sel = rows.sel(lat_only)
    row = lambda i: (i, 0)
    in_specs = [pl.BlockSpec((tm, d), row), pl.BlockSpec((TOP_K, tm, d // 2), lambda i: (0, i, 0)),
                pl.BlockSpec((None, 1, 8, d), lambda i: (layer, sel(i), 0, 0))]
    args = [h, y01, modtab]
    if lat_only:
        in_specs.append(pl.BlockSpec((1, d), lambda i: (0, 0)))
        args.append(nf)
    return pl.pallas_call(
        functools.partial(_resid_kernel, lat_only),
        out_shape=jax.ShapeDtypeStruct((r, d), F32),
        grid=(r // tm,),
        in_specs=in_specs,
        out_specs=pl.BlockSpec((tm, d), row),
        compiler_params=_cparams(("arbitrary",)),
        name="moe_resid",
    )(*args)


CAST_PARTS = 8


def _cast_kernel(*refs):
    o_ref = refs[-1]
    tr = refs[0].shape[1]
    for k, x_ref in enumerate(refs[:-1]):
        o_ref[0, k * tr:(k + 1) * tr, :] = x_ref[0].astype(o_ref.dtype)


def _to_bf16(w, j):
    lead, (r, c) = w.shape[1:-2], w.shape[-2:]
    n = int(np.prod(lead, dtype=np.int64))
    w3 = w.reshape((-1, r, c))
    tr = r // CAST_PARTS
    assert tr * CAST_PARTS == r and tr % 16 == 0
    band = lambda k: pl.BlockSpec((1, tr, c), lambda e: (j * n + e, k, 0))
    out = pl.pallas_call(
        _cast_kernel,
        out_shape=jax.ShapeDtypeStruct((n, r, c), BF16),
        grid=(n,),
        in_specs=[band(k) for k in range(CAST_PARTS)],
        out_specs=pl.BlockSpec((1, r, c), lambda e: (e, 0, 0)),
        compiler_params=_cparams(("arbitrary",)),
        name="to_bf16",
    )(*([w3] * CAST_PARTS))
    return out.reshape(lead + (r, c))


def _pad_heads(w, heads, dim, to=LANE):
    lead = w.shape[:-1]
    w = w.reshape(lead + (heads, dim))
    w = jnp.pad(w, [(0, 0)] * len(lead) + [(0, 0), (0, to - dim)])
    return w.reshape(lead + (heads * to,))


def _pad_last(w, to):
    return jnp.pad(w, [(0, 0)] * (w.ndim - 1) + [(0, to - w.shape[-1])])


def _pos_embed(n_tokens, d):
    n_grid_rows = n_tokens // GRID_W
    row, col = jnp.meshgrid(jnp.arange(n_grid_rows, dtype=F32), jnp.arange(GRID_W, dtype=F32), indexing='ij')
    n_freq = d // 4
    omega = jnp.exp(-math.log(POS_BASE) * jnp.arange(n_freq, dtype=F32) / n_freq)

    def axis_embed(p):
        ang = p.reshape(-1, 1) * omega
        return jnp.concatenate([jnp.sin(ang), jnp.cos(ang)], axis=-1)

    return jnp.concatenate([axis_embed(row), axis_embed(col)], axis=-1)


def _s5_discretise(lam_re, lam_im, log_dt, b_re, b_im):
    dt = jnp.exp(log_dt)[:, None]
    mag = jnp.exp(lam_re * dt)
    abar_re, abar_im = mag * jnp.cos(lam_im * dt), mag * jnp.sin(lam_im * dt)
    den = lam_re * lam_re + lam_im * lam_im
    pr, pi = abar_re - 1.0, abar_im
    coef_re = (pr * lam_re + pi * lam_im) / den
    coef_im = (pi * lam_re - pr * lam_im) / den
    bbar_re = coef_re[..., None] * b_re - coef_im[..., None] * b_im
    bbar_im = coef_re[..., None] * b_im + coef_im[..., None] * b_re
    return abar_re, abar_im, bbar_re, bbar_im


def _block_diag(m):
    g, a, b = m.shape
    eye = jnp.eye(g, dtype=m.dtype)
    return (eye[:, None, :, None] * m[:, :, None, :]).reshape(g * a, g * b)


def _route_plan(route, tm):
    n_tok = route.shape[1]
    n_assign = n_tok * TOP_K
    flat_e = route[0:TOP_K].astype(jnp.int32).reshape(-1)
    onehot = (jnp.arange(N_EXPERTS, dtype=jnp.int32)[:, None] == flat_e[None, :]).astype(jnp.int32)
    csum = jnp.cumsum(onehot, axis=1)
    counts = csum[:, -1]
    padded = (counts + tm - 1) // tm * tm
    pend = jnp.cumsum(padded)
    pstart = pend - padded
    dest = jnp.sum(onehot * (csum - 1 + pstart[:, None]), axis=0)
    n_blocks = -(-n_assign // tm) + N_EXPERTS
    block_start = jnp.arange(n_blocks, dtype=jnp.int32) * tm
    block_expert = jnp.minimum(jnp.searchsorted(pend, block_start, side='right'), N_EXPERTS - 1).astype(jnp.int32)
    block_live = (block_start < (pstart + counts)[block_expert]).astype(jnp.int32)
    return n_blocks * tm, block_expert, block_live, dest.reshape(TOP_K, n_tok)


def kernel(x, c, ctx, c_ctx, w_ada, b_ada, norm1, norm2, w_in, w_out, gla_wa2, gla_ba, gla_norm, s5_lam_re, s5_lam_im, s5_log_dt, s5_b_re, s5_b_im, s5_c_re, s5_c_im, s5_d, s5_glu_w, s5_glu_b, ml_conv_w, ml_conv_b, ml_gate_b, ml_norm, ffn_w1, ffn_w3, ffn_w2, moe_router, moe_w1, moe_w3, moe_w2, norm_f):
    nb, n_lat, d = x.shape
    lc = ctx.shape[1]
    depth = w_ada.shape[0]
    l = lc + n_lat
    assert lc % SEG == 0 and n_lat % SEG == 0 and nb == 8 and nb % SCAN_BATCHES == 0
    rows = _Rows(nb, lc // SEG, l // SEG)

    h = _embed(ctx.reshape(nb * lc, d), x.reshape(nb * n_lat, d), _pos_embed(n_lat, d), rows)

    cond = jnp.zeros((16, d), F32).at[:nb].set(c).at[nb].set(c_ctx)
    mod = _modulation(cond, w_ada, b_ada)
    modtab = jnp.pad(mod.reshape(depth, 16, 6, d), ((0, 0), (0, 0), (0, 2), (0, 0)))

    dk, dv, dh = GLA_HEADS * GLA_DK, GLA_HEADS * GLA_DV, ML_HEADS * ML_DH
    s5c = s5_d.shape[-1]
    cuts = np.cumsum([dk, dk, dv, GLA_RANK, dv, s5c, dh, dh, dh, dh, 4 * ML_HEADS])

    for i in range(depth):
        last = i == depth - 1
        gq, gk, gv, glr, gg, su, mq, mk, mv, mo, mg = jnp.split(w_in[i], cuts[:-1], axis=-1)
        w_all = jnp.concatenate([
            _pad_heads(gq, GLA_HEADS, GLA_DK, GLA_KP), _pad_heads(gk, GLA_HEADS, GLA_DK, GLA_KP),
            _pad_heads(gv, GLA_HEADS, GLA_DV), _pad_last(glr, LANE),
            su,
            _pad_heads(mv, ML_HEADS, ML_DH),
            _pad_last(mg[:, :2 * ML_HEADS], LANE), _pad_last(mg[:, 2 * ML_HEADS:], LANE),
            _pad_heads(mq, ML_HEADS, ML_DH), _pad_heads(mk, ML_HEADS, ML_DH),
            _pad_heads(gg, GLA_HEADS, GLA_DV), _pad_heads(mo, ML_HEADS, ML_DH)],
            axis=-1).astype(BF16)
        wgt = mg.T.astype(BF16)
        cw = jnp.concatenate([_pad_heads(ml_conv_w[i][:, :dh], ML_HEADS, ML_DH),
                              _pad_heads(ml_conv_w[i][:, dh:], ML_HEADS, ML_DH)], axis=-1)
        cw = jnp.pad(cw, ((0, 8 - ML_CONV), (0, 0)))
        cb = jnp.concatenate([_pad_heads(ml_conv_b[i][:dh], ML_HEADS, ML_DH),
                              _pad_heads(ml_conv_b[i][dh:], ML_HEADS, ML_DH)])[None]
        post = jnp.concatenate([jnp.ones((HP,), F32), jnp.full((HP,), ML_DH ** -0.5, F32)])[None]
        pg, pu, pm, qk, go, gt = _proj(h, modtab, norm1[i][None], w_all, wgt, cw, cb, post, layer=i, rows=rows)

        wa = jnp.pad(_pad_heads(gla_wa2[i], GLA_HEADS, GLA_DK, GLA_KP), ((0, 0), (0, LANE - GLA_RANK), (0, 0)))
        ba = _pad_heads(gla_ba[i], GLA_HEADS, GLA_DK, GLA_KP)[:, None, :]
        pg3 = pg.reshape(nb, l, NG)
        ogf = _gla(pg3, wa[0], ba[0], rows=rows, rev=False).reshape(nb * l, HP)
        ogb = _gla(pg3, wa[1], ba[1], rows=rows, rev=True).reshape(nb * l, HP)

        bres, bims, ares, aims = [], [], [], []
        for dr in (0, 1):
            a_re, a_im, b_re, b_im = _s5_discretise(s5_lam_re[i, dr], s5_lam_im[i, dr], s5_log_dt[i, dr],
                                                    s5_b_re[i], s5_b_im[i])
            bres.append(_block_diag(jnp.swapaxes(b_re, 1, 2)))
            bims.append(_block_diag(jnp.swapaxes(b_im, 1, 2)))
            ares.append(jnp.broadcast_to(a_re.reshape(1, -1), (nb, a_re.size)))
            aims.append(jnp.broadcast_to(a_im.reshape(1, -1), (nb, a_im.size)))
        cre = _block_diag(jnp.swapaxes(s5_c_re[i], 1, 2)).astype(BF16)
        cim = _block_diag(jnp.swapaxes(s5_c_im[i], 1, 2)).astype(BF16)
        ut = pu.reshape(nb, l, s5c).swapaxes(0, 1).reshape(l * nb, s5c)
        yt = _s5(ut, jnp.stack(bres).astype(BF16), jnp.stack(bims).astype(BF16), jnp.stack(ares), jnp.stack(aims),
                 cre, cim, nb=nb, nc=lc // S5_STEPS, nt=l // S5_STEPS)
        ys = yt.reshape(2, l, nb, s5c).swapaxes(1, 2).reshape(2, nb * l, s5c)

        gb = ml_gate_b[i].reshape(2, 2, ML_HEADS)
        gbr = _pad_last(gb.reshape(2, 1, 2 * ML_HEADS), LANE)
        gbt = _pad_last(jnp.broadcast_to(gb[..., None], (2, 2, ML_HEADS, CHUNK)), LANE).reshape(2, 2, HP)
        gtl = _pad_last(gt.reshape(2, 2, ML_HEADS, nb, l // CHUNK, CHUNK).transpose(3, 4, 0, 1, 2, 5),
                        LANE).reshape(nb, l // CHUNK, 2, 2, HP)
        qk3, pm3 = qk.reshape(nb, l, 2 * HP), pm.reshape(nb, l, NM)
        omf = _mlstm(qk3, pm3, gtl, gbr[0], gbt[0], rows=rows, rev=False).reshape(nb * l, HP)
        omb = _mlstm(qk3, pm3, gtl, gbr[1], gbt[1], rows=rows, rev=True).reshape(nb * l, HP)

        wo = w_out[i]
        wo_p = jnp.concatenate([
            jnp.pad(wo[:dv].reshape(GLA_HEADS, GLA_DV, d), ((0, 0), (0, LANE - GLA_DV), (0, 0))).reshape(HP, d),
            wo[dv:dv + s5c],
            jnp.pad(wo[dv + s5c:].reshape(ML_HEADS, ML_DH, d), ((0, 0), (0, LANE - ML_DH), (0, 0))).reshape(HP, d)],
            axis=0).astype(BF16)
        gn = jnp.tile(_pad_last(gla_norm[i], LANE), GLA_HEADS)[None]
        mn = jnp.tile(_pad_last(ml_norm[i], LANE), ML_HEADS)[None]
        is_moe = i % 2 == 1
        j = i // 2
        wr = jnp.stack(_split_bf16(_pad_last(moe_router[j], LANE), 2)) if is_moe else None
        mix_args = (ogf, ogb, go, ys, pu, omf, omb, go, h, modtab, gn, mn, s5_d[i][None], s5_glu_w[i].astype(BF16),
                    s5_glu_b[i][None], wo_p, norm2[i][None])
        if not is_moe and not last:
            h = _mix_ffn(*mix_args, _to_bf16(ffn_w1, j), _to_bf16(ffn_w3, j), _to_bf16(ffn_w2, j),
                         layer=i, rows=rows)
            continue
        outs = _mix(*mix_args, wr, layer=i, rows=rows, lat_only=last)
        if not is_moe:
            h, f = outs
            h = _ffn(f, h, modtab, _to_bf16(ffn_w1, j), _to_bf16(ffn_w3, j), _to_bf16(ffn_w2, j),
                     norm_f[None], layer=i, rows=rows, lat_only=last)
        else:
            h, f, route = outs
            n_rows, block_expert, block_live, dest = _route_plan(route, MOE_TM)
            xg = _scatter_rows(f, dest, n_rows)
            yg = _moe_experts(block_expert, block_live, xg, _to_bf16(moe_w1, j), _to_bf16(moe_w3, j),
                              _to_bf16(moe_w2, j))
            y01 = _gather_rows(yg, dest)
            h = _moe_resid(h, y01, modtab, norm_f[None], layer=i, rows=rows, lat_only=last)
    return h.reshape(nb, n_lat, d)
```

```python
import functools
import math

import numpy as np
import jax
import jax.numpy as jnp
from jax import lax
from jax.experimental import pallas as pl
from jax.experimental.pallas import tpu as pltpu
from jax.experimental.pallas import tpu_sc as plsc

F32 = jnp.float32
BF16 = jnp.bfloat16
HIGHEST = lax.Precision.HIGHEST

GRID_W = 64
POS_BASE = 10000.0
EPS = 1e-6
GLA_HEADS, GLA_DK, GLA_DV, GLA_RANK, GLA_GATE_NORM = 4, 48, 96, 16, 16.0
S5_GROUP, S5_STATE = 16, 64
ML_HEADS, ML_DH, ML_CONV = 4, 96, 3
N_EXPERTS, TOP_K = 8, 2

LANE = 128
CHUNK = 64
SEG = 256
N_SUB = SEG // CHUNK
SCAN_BATCHES = 8
S5_STEPS = 256
S5_SUB = 32
NEG = -1e30
V7X_VMEM_BYTES = 64 * 1024 * 1024
VMEM_LIMIT = V7X_VMEM_BYTES - 8 * 1024 * 1024

HP = LANE * GLA_HEADS
GLA_KP = 64
GQ = GLA_HEADS * GLA_KP
NG = 2 * GQ + HP + LANE
NM = HP + 2 * LANE
NGO = 2 * HP


def _cparams(sem):
    return pltpu.CompilerParams(dimension_semantics=sem, vmem_limit_bytes=VMEM_LIMIT)


def _dot(a, b, **kw):
    return jnp.dot(a, b, preferred_element_type=F32, **kw)


def _dot_nt(a, b, **kw):
    return lax.dot_general(a, b, (((1,), (1,)), ((), ())), preferred_element_type=F32, **kw)


def _dot_tn(a, b, **kw):
    return lax.dot_general(a, b, (((0,), (0,)), ((), ())), preferred_element_type=F32, **kw)


def _log_sigmoid(x):
    return jnp.minimum(x, 0.0) - jnp.log1p(jnp.exp(-jnp.abs(x)))


def _silu(x):
    return x * jax.nn.sigmoid(x)


def _gelu_tanh(x):
    return 0.5 * x * (1.0 + jnp.tanh(math.sqrt(2.0 / math.pi) * (x + 0.044715 * (x * x * x))))


def _rmsnorm(x, g):
    return x * lax.rsqrt(jnp.mean(x * x, axis=-1, keepdims=True) + EPS) * g


class _Rows:
    def __init__(self, nb, ncb, ntb):
        self.nb, self.ncb, self.ntb, self.nlb = nb, ncb, ntb, ntb - ncb

    def n_blocks(self, lat_only):
        return self.nb * (self.nlb if lat_only else self.ntb)

    def src(self, lat_only):
        if lat_only:
            return lambda i: (i // self.nlb) * self.ntb + self.ncb + i % self.nlb
        return lambda i: i

    def sel(self, lat_only):
        if lat_only:
            return lambda i: i // self.nlb
        return lambda i: jnp.where(i % self.ntb < self.ncb, self.nb, i // self.ntb)


def _embed_kernel(ncb, ntb, ctx_ref, x_ref, pos_ref, o_ref):
    j = pl.program_id(0) % ntb

    @pl.when(j < ncb)
    def _():
        o_ref[...] = ctx_ref[...]

    @pl.when(j >= ncb)
    def _():
        o_ref[...] = x_ref[...] + pos_ref[...]


def _embed(ctx2, x2, pos, rows):
    d = ctx2.shape[1]
    ncb, ntb, nlb = rows.ncb, rows.ntb, rows.nlb
    return pl.pallas_call(
        functools.partial(_embed_kernel, ncb, ntb),
        out_shape=jax.ShapeDtypeStruct((rows.nb * ntb * SEG, d), F32),
        grid=(rows.nb * ntb,),
        in_specs=[pl.BlockSpec((SEG, d), lambda i: ((i // ntb) * ncb + jnp.minimum(i % ntb, ncb - 1), 0)),
                  pl.BlockSpec((SEG, d), lambda i: ((i // ntb) * nlb + jnp.maximum(i % ntb - ncb, 0), 0)),
                  pl.BlockSpec((SEG, d), lambda i: (jnp.maximum(i % ntb - ncb, 0), 0))],
        out_specs=pl.BlockSpec((SEG, d), lambda i: (i, 0)),
        compiler_params=_cparams(("arbitrary",)),
        name="embed",
    )(ctx2, x2, pos)


def _mod_kernel(c_ref, w_ref, b_ref, o_ref):
    s = _silu(c_ref[...])
    o_ref[0] = _dot(s, w_ref[0], precision=HIGHEST) + b_ref[0]


def _modulation(cond, w_ada, b_ada):
    depth, d, n6 = w_ada.shape
    tn = n6 // 4
    n_rows = cond.shape[0]
    return pl.pallas_call(
        _mod_kernel,
        out_shape=jax.ShapeDtypeStruct((depth, n_rows, n6), F32),
        grid=(depth, n6 // tn),
        in_specs=[pl.BlockSpec((n_rows, d), lambda l, j: (0, 0)),
                  pl.BlockSpec((1, d, tn), lambda l, j: (l, 0, j)),
                  pl.BlockSpec((1, 1, tn), lambda l, j: (l, 0, j))],
        out_specs=pl.BlockSpec((1, n_rows, tn), lambda l, j: (l, 0, j)),
        compiler_params=_cparams(("arbitrary", "arbitrary")),
        name="modulation",
    )(cond, w_ada, b_ada.reshape(depth, 1, n6))


PROJ_TM = 2 * SEG


def _proj_kernel(ncb, ntb, sel, h_ref, hp_ref, hn_ref, mod_ref, g_ref, w_ref, wgt_ref, cw_ref, cb_ref, post_ref,
                 pg_ref, pu_ref, pm_ref, qk_ref, go_ref, gt_ref):
    tm = h_ref.shape[0]
    n_seg = tm // SEG
    blk = [pl.program_id(0) * n_seg + q for q in range(n_seg)]
    mods = [mod_ref[sel(bq)] for bq in blk]
    act = lambda x, m: _rmsnorm(x, g_ref[...]) * (1.0 + m[1:2]) + m[0:1]
    a_seg = [act(h_ref[q * SEG:(q + 1) * SEG, :], mods[q]) for q in range(n_seg)]
    ab = jnp.concatenate(a_seg, axis=0).astype(BF16)
    c1, c2 = NG + 2 * LANE, NG + 2 * LANE + NM
    c3 = c2 + 2 * HP
    go_ref[...] = _dot(ab, w_ref[:, c3:]).astype(go_ref.dtype)
    pg_ref[...] = _dot(ab, w_ref[:, 0:NG])
    pu_ref[...] = _dot(ab, w_ref[:, NG:c1])
    pm_ref[...] = _dot(ab, w_ref[:, c1:c2])
    gt_ref[...] = _dot_nt(wgt_ref[...], ab)
    ae = jnp.concatenate([act(hp_ref[...], mods[0])] + a_seg + [act(hn_ref[...], mods[-1])], axis=0)
    xe = _dot(ae.astype(BF16), w_ref[:, c2:c3])
    n_e = tm + 16
    row = lax.broadcasted_iota(jnp.int32, (tm, xe.shape[1]), 0)
    keep_prev = jnp.ones((tm, xe.shape[1]), F32)
    keep_next = keep_prev
    for q, bq in enumerate(blk):
        j = bq % ntb
        first = jnp.logical_or(j == 0, j == ncb).astype(F32)
        last = jnp.logical_or(j == ncb - 1, j == ntb - 1).astype(F32)
        keep_prev = jnp.where(row == q * SEG, 1.0 - first, keep_prev)
        keep_next = jnp.where(row == (q + 1) * SEG - 1, 1.0 - last, keep_next)
    xp = pltpu.roll(xe, 1, axis=0)[8:8 + tm] * keep_prev
    xn = pltpu.roll(xe, n_e - 1, axis=0)[8:8 + tm] * keep_next
    y = cw_ref[0:1] * xp + cw_ref[1:2] * xe[8:8 + tm] + cw_ref[2:3] * xn + cb_ref[...]
    qk_ref[...] = _silu(y) * post_ref[...]


def _proj(h, modtab, g, w, wgt, cw, cb, post, *, layer, rows):
    r, d = h.shape
    tm = PROJ_TM
    assert r % tm == 0
    t8 = tm // 8
    full = lambda a: pl.BlockSpec(a.shape, lambda i: (0,) * a.ndim)
    return pl.pallas_call(
        functools.partial(_proj_kernel, rows.ncb, rows.ntb, rows.sel(False)),
        out_shape=(jax.ShapeDtypeStruct((r, NG), F32), jax.ShapeDtypeStruct((r, 2 * LANE), F32),
                   jax.ShapeDtypeStruct((r, NM), F32), jax.ShapeDtypeStruct((r, 2 * HP), F32),
                   jax.ShapeDtypeStruct((r, NGO), BF16),
                   jax.ShapeDtypeStruct((16, r), F32)),
        grid=(r // tm,),
        in_specs=[pl.BlockSpec((tm, d), lambda i: (i, 0)),
                  pl.BlockSpec((8, d), lambda i: (jnp.maximum(i * t8 - 1, 0), 0)),
                  pl.BlockSpec((8, d), lambda i: (jnp.minimum((i + 1) * t8, r // 8 - 1), 0)),
                  pl.BlockSpec((None,) + modtab.shape[1:], lambda i: (layer, 0, 0, 0)),
                  full(g), full(w), full(wgt), full(cw), full(cb), full(post)],
        out_specs=(pl.BlockSpec((tm, NG), lambda i: (i, 0)),
                   pl.BlockSpec((tm, 2 * LANE), lambda i: (i, 0)),
                   pl.BlockSpec((tm, NM), lambda i: (i, 0)),
                   pl.BlockSpec((tm, 2 * HP), lambda i: (i, 0)),
                   pl.BlockSpec((tm, NGO), lambda i: (i, 0)),
                   pl.BlockSpec((16, tm), lambda i: (0, i))),
        compiler_params=_cparams(("arbitrary",)),
        name="proj",
    )(h, h, h, modtab, g, w, wgt, cw, cb, post)


def _scan_pos(d, s, ncb, ntb):
    rev = jnp.where(s < ncb, ncb - 1 - s, ntb - 1 - (s - ncb))
    return jnp.where(d == 0, s, rev)


def _scan_pos_static(rev, s, ncb, ntb):
    if not rev:
        return s
    return jnp.where(s < ncb, ncb - 1 - s, ntb - 1 - (s - ncb))


def _tri(rev):
    r = lax.broadcasted_iota(jnp.int32, (CHUNK, CHUNK), 0)
    c = lax.broadcasted_iota(jnp.int32, (CHUNK, CHUNK), 1)
    return (r <= c) if rev else (r >= c)


def _chunk_rows(rev):
    return [(N_SUB - 1 - j if rev else j) * CHUNK for j in range(N_SUB)]


def _gla_kernel(rev, p_ref, wa_ref, ba_ref, o_ref, st_ref):
    s, g = pl.program_id(0), pl.program_id(1)
    nbb = p_ref.shape[0]
    b0 = g * nbb

    @pl.when(s == 0)
    def _():
        st_ref[pl.ds(b0, nbb)] = jnp.zeros((nbb,) + st_ref.shape[1:], F32)

    valid = _tri(rev)
    tri = valid.astype(F32)
    r0s = _chunk_rows(rev)
    wa, ba = wa_ref[...], ba_ref[...]
    inst = [(bb, j) for bb in range(nbb) for j in range(N_SUB)]
    heads = [slice(h * LANE, (h + 1) * LANE) for h in range(GLA_HEADS)]
    pairs = [slice((h // 2) * LANE, (h // 2 + 1) * LANE) for h in range(GLA_HEADS)]
    lane = lax.broadcasted_iota(jnp.int32, (CHUNK, LANE), 1)
    own = [(lane // GLA_KP) == (h % 2) for h in range(GLA_HEADS)]

    la = {}
    for bb, j in inst:
        lr = p_ref[bb, pl.ds(r0s[j], CHUNK), 2 * GQ + HP:NG]
        la[bb, j] = _log_sigmoid(_dot(lr, wa) + ba) * (1.0 / GLA_GATE_NORM)
    bc, e_last = {}, {}
    for i in inst:
        bc[i] = _dot_exact01(tri, la[i], lhs_is_01=True, pieces=2)
        e_last[i] = jnp.exp(jnp.sum(la[i], axis=0, keepdims=True))
    q_in, k_in, k_out, v = {}, {}, {}, {}
    for bb, j in inst:
        i = (bb, j)
        rs = pl.ds(r0s[j], CHUNK)
        qs = (p_ref[bb, rs, 0:GQ] * (GLA_DK ** -0.5) * jnp.exp(bc[i])).astype(BF16)
        for h in range(GLA_HEADS):
            q_in[i, h] = jnp.where(own[h], qs[:, pairs[h]], jnp.zeros_like(qs[:, pairs[h]]))
        kd = p_ref[bb, rs, GQ:2 * GQ] * jnp.exp(-bc[i])
        k_out[i] = (kd * e_last[i]).astype(BF16)
        k_in[i] = kd.astype(BF16)
        v[i] = p_ref[bb, rs, 2 * GQ:2 * GQ + HP].astype(BF16)
    att = {}
    for i in inst:
        for h in range(GLA_HEADS):
            att[i, h] = jnp.where(valid, _dot_nt(q_in[i, h], k_in[i][:, pairs[h]]), 0.0).astype(BF16)
    o_intra, ds = {}, {}
    for i in inst:
        for h, sl in enumerate(heads):
            o_intra[i, h] = _dot(att[i, h], v[i][:, sl])
            ds[i, h] = _dot_tn(v[i][:, sl], k_out[i][:, pairs[h]])
    s_in = {}
    for bb in range(nbb):
        for h in range(GLA_HEADS):
            st = st_ref[b0 + bb, h]
            for j in range(N_SUB):
                s_in[(bb, j), h] = st.astype(BF16)
                st = st * e_last[bb, j][:, pairs[h]] + ds[(bb, j), h]
            st_ref[b0 + bb, h] = st
    for bb, j in inst:
        for h, sl in enumerate(heads):
            o = o_intra[(bb, j), h] + _dot_nt(q_in[(bb, j), h], s_in[(bb, j), h])
            o_ref[bb, pl.ds(r0s[j], CHUNK), sl] = o.astype(o_ref.dtype)


def _gla(pg3, wa, ba, *, rows, rev):
    nb, l, _ = pg3.shape
    nbb = SCAN_BATCHES
    pos = functools.partial(_scan_pos_static, rev, ncb=rows.ncb, ntb=rows.ntb)
    return pl.pallas_call(
        functools.partial(_gla_kernel, rev),
        out_shape=jax.ShapeDtypeStruct((nb, l, HP), BF16),
        grid=(rows.ntb, nb // nbb),
        in_specs=[pl.BlockSpec((nbb, SEG, NG), lambda s, g: (g, pos(s), 0)),
                  pl.BlockSpec((LANE, GQ), lambda s, g: (0, 0)),
                  pl.BlockSpec((1, GQ), lambda s, g: (0, 0))],
        out_specs=pl.BlockSpec((nbb, SEG, HP), lambda s, g: (g, pos(s), 0)),
        scratch_shapes=[pltpu.VMEM((nb, GLA_HEADS, LANE, LANE), F32)],
        compiler_params=_cparams(("arbitrary", "arbitrary")),
        name="gla_scan_bwd" if rev else "gla_scan_fwd",
    )(pg3, wa, ba)


def _split_bf16(x, n):
    parts, r = [], x
    for _ in range(n):
        p = r.astype(BF16)
        parts.append(p)
        r = r - p.astype(F32)
    return parts


def _dot_exact01(a, b, lhs_is_01, pieces=3):
    if lhs_is_01:
        a = a.astype(BF16)
        terms = [_dot(a, p) for p in _split_bf16(b, pieces)]
    else:
        b = b.astype(BF16)
        terms = [_dot(p, b) for p in _split_bf16(a, pieces)]
    return functools.reduce(lambda x, y: x + y, terms)


def _cummax_rows(a, rev):
    n = a.shape[0]
    row = lax.broadcasted_iota(jnp.int32, a.shape, 0)
    k = 1
    while k < n:
        if rev:
            sh = jnp.where(row < n - k, pltpu.roll(a, n - k, axis=0), NEG)
        else:
            sh = jnp.where(row >= k, pltpu.roll(a, k, axis=0), NEG)
        a = jnp.maximum(a, sh)
        k *= 2
    return a


ML_GL = ML_HEADS


def _mlstm_kernel(rev, qk_ref, v_ref, g_ref, gt_ref, gbr_ref, gbt_ref, o_ref, st_ref, m_ref):
    s, g = pl.program_id(0), pl.program_id(1)
    nbb = qk_ref.shape[0]
    b0 = g * nbb

    @pl.when(s == 0)
    def _():
        st_ref[pl.ds(b0, nbb)] = jnp.zeros((nbb,) + st_ref.shape[1:], F32)
        m_ref[pl.ds(b0, nbb)] = jnp.zeros((nbb,) + m_ref.shape[1:], F32)

    valid = _tri(rev)
    tri = valid.astype(F32)
    r0s = _chunk_rows(rev)
    cs = [r // CHUNK for r in r0s]
    last = 0 if rev else CHUNK - 1
    inst = [(bb, j) for bb in range(nbb) for j in range(N_SUB)]
    heads = [slice(h * LANE, (h + 1) * LANE) for h in range(ML_HEADS)]

    r_sel = lax.broadcasted_iota(jnp.int32, (LANE, HP), 0)
    c_sel = lax.broadcasted_iota(jnp.int32, (LANE, HP), 1)
    sel_h = (r_sel == ML_GL + c_sel // LANE).astype(BF16)
    r_t = lax.broadcasted_iota(jnp.int32, (HP, HP), 0)
    c_t = lax.broadcasted_iota(jnp.int32, (HP, HP), 1)
    same = jnp.logical_and(r_t // LANE == c_t // LANE, jnp.logical_and(r_t % LANE < CHUNK, c_t % LANE < CHUNK))
    before = (r_t % LANE >= c_t % LANE) if rev else (r_t % LANE <= c_t % LANE)
    tri_b = jnp.logical_and(same, before).astype(BF16)
    r_v = lax.broadcasted_iota(jnp.int32, (CHUNK, HP), 0)
    c_v = lax.broadcasted_iota(jnp.int32, (CHUNK, HP), 1) % LANE
    valid4 = jnp.logical_and(c_v < CHUNK, (r_v <= c_v) if rev else (r_v >= c_v))
    lane4 = lax.broadcasted_iota(jnp.int32, (CHUNK, HP), 1) % LANE
    lane_c = lax.broadcasted_iota(jnp.int32, (CHUNK, LANE), 1)
    lane1 = lax.broadcasted_iota(jnp.int32, (1, LANE), 1)
    head_lane = jnp.logical_and(lane1 >= ML_GL, lane1 < ML_GL + ML_HEADS)
    gbr, gbt = gbr_ref[...], gbt_ref[...]

    gcs, fcm, cmx, a_row, grt = {}, {}, {}, {}, {}
    for bb, j in inst:
        gc = g_ref[bb, pl.ds(r0s[j], CHUNK), :] + gbr
        gcs[bb, j] = pltpu.roll(gc, ML_GL, axis=1)
        fcm[bb, j] = _dot_exact01(tri, _log_sigmoid(gc), lhs_is_01=True)
        grt[bb, j] = gt_ref[bb, cs[j]] + gbt
    row_id = lax.broadcasted_iota(jnp.int32, (len(inst), HP), 0)
    lfr = jnp.zeros((len(inst), HP), F32)
    for n, i in enumerate(inst):
        lfr = jnp.where(row_id == n, _log_sigmoid(grt[i][1:2]), lfr)
    fcr = _dot_exact01(lfr, tri_b, lhs_is_01=False)
    for n, i in enumerate(inst):
        a_row[i] = grt[i][0:1] - fcr[n:n + 1]
        cmx[i] = _cummax_rows(gcs[i] - fcm[i], rev)
    bx = {}
    e_neg, gd = {}, {}
    for bb in range(nbb):
        m_prev = m_ref[b0 + bb, 0:1, :]
        for j in range(N_SUB):
            i = (bb, j)
            m_t = fcm[i] + jnp.maximum(m_prev, cmx[i])
            m_new = m_t[last:last + 1]
            f_tot = fcm[i][last:last + 1]
            u = fcm[i] - m_t
            w_prev = jnp.exp(u + m_prev)
            w_s = jnp.exp(f_tot - fcm[i] + gcs[i] - m_new)
            gdec = jnp.broadcast_to(jnp.exp(f_tot + m_prev - m_new), (16, LANE))
            e_neg[i] = jnp.exp(-m_t)
            keep = lambda a: jnp.where(head_lane, a, 0.0)
            bx[i] = jnp.concatenate(_split_bf16(keep(u), 2) + _split_bf16(keep(w_prev), 1)
                                    + _split_bf16(keep(w_s), 1) + _split_bf16(keep(gdec), 2), axis=0)
            m_prev = m_new
        m_ref[b0 + bb] = jnp.broadcast_to(m_prev, (8, LANE))
    ub, wpb, wsb, gdb = {}, {}, {}, {}
    for i in inst:
        y = _dot(bx[i], sel_h)
        c = CHUNK
        ub[i] = y[0:c] + y[c:2 * c]
        wpb[i] = y[2 * c:3 * c]
        wsb[i] = y[3 * c:4 * c]
        gdb[i] = y[4 * c:4 * c + 1] + y[4 * c + 16:4 * c + 17]
    q, v, qkw, ds = {}, {}, {}, {}
    for bb, j in inst:
        i = (bb, j)
        rs = pl.ds(r0s[j], CHUNK)
        w = jnp.where(valid4, jnp.exp(ub[i] + a_row[i]), 0.0)
        q[i] = qk_ref[bb, rs, 0:HP].astype(BF16)
        k = qk_ref[bb, rs, HP:2 * HP]
        kb = k.astype(BF16)
        kw = (k * wsb[i]).astype(BF16)
        v[i] = jnp.where(lane4 == ML_DH, 1.0, v_ref[bb, rs, :]).astype(BF16)
        for h, sl in enumerate(heads):
            sc = _dot_nt(q[i][:, sl], kb[:, sl])
            qkw[i, h] = (sc * w[:, h * LANE:h * LANE + CHUNK]).astype(BF16)
            ds[i, h] = _dot_tn(v[i][:, sl], kw[:, sl])
    s_in = {}
    for bb in range(nbb):
        for h, sl in enumerate(heads):
            st = st_ref[b0 + bb, h]
            for j in range(N_SUB):
                i = (bb, j)
                s_in[i, h] = st.astype(BF16)
                st = gdb[i][:, sl] * st + ds[i, h]
            st_ref[b0 + bb, h] = st
    num = {}
    for i in inst:
        parts = [_dot_nt(q[i][:, sl], s_in[i, h]) for h, sl in enumerate(heads)]
        intra = [_dot(qkw[i, h], v[i][:, sl]) for h, sl in enumerate(heads)]
        num[i] = wpb[i] * jnp.concatenate(parts, axis=1) + jnp.concatenate(intra, axis=1)
    for bb, j in inst:
        i = (bb, j)
        den = jnp.zeros((CHUNK, LANE), F32)
        for h, sl in enumerate(heads):
            dh = jnp.sum(jnp.where(lane_c == ML_DH, num[i][:, sl], 0.0), axis=-1, keepdims=True)
            den = jnp.where(lane_c == ML_GL + h, jnp.broadcast_to(dh, (CHUNK, LANE)), den)
        rb = _dot(jnp.where(head_lane, 1.0 / jnp.maximum(jnp.abs(den), e_neg[i]), 0.0).astype(BF16), sel_h)
        o_ref[bb, pl.ds(r0s[j], CHUNK), :] = jnp.where(lane4 < ML_DH, num[i] * rb, 0.0).astype(o_ref.dtype)


def _mlstm(qk3, pm3, gtl, gbr, gbt, *, rows, rev):
    nb, l, _ = qk3.shape
    nbb = SCAN_BATCHES
    dr = 1 if rev else 0
    pos = functools.partial(_scan_pos_static, rev, ncb=rows.ncb, ntb=rows.ntb)
    return pl.pallas_call(
        functools.partial(_mlstm_kernel, rev),
        out_shape=jax.ShapeDtypeStruct((nb, l, HP), BF16),
        grid=(rows.ntb, nb // nbb),
        in_specs=[pl.BlockSpec((nbb, SEG, 2 * HP), lambda s, g: (g, pos(s), 0)),
                  pl.BlockSpec((nbb, SEG, HP), lambda s, g: (g, pos(s), 0)),
                  pl.BlockSpec((nbb, SEG, LANE), lambda s, g: (g, pos(s), HP // LANE + dr)),
                  pl.BlockSpec((nbb, N_SUB, None, 2, HP), lambda s, g: (g, pos(s), dr, 0, 0)),
                  pl.BlockSpec((1, LANE), lambda s, g: (0, 0)),
                  pl.BlockSpec((2, HP), lambda s, g: (0, 0))],
        out_specs=pl.BlockSpec((nbb, SEG, HP), lambda s, g: (g, pos(s), 0)),
        scratch_shapes=[pltpu.VMEM((nb, ML_HEADS, LANE, LANE), F32), pltpu.VMEM((nb, 8, LANE), F32)],
        compiler_params=_cparams(("arbitrary", "arbitrary")),
        name="mlstm_scan_bwd" if rev else "mlstm_scan_fwd",
    )(qk3, pm3, pm3, gtl, gbr, gbt)


def _s5_kernel(nb, u_ref, bre_ref, bim_ref, are_ref, aim_ref, cre_ref, cim_ref, o_ref, xr_ref, xi_ref, st_ref):
    d, s = pl.program_id(0), pl.program_id(1)

    @pl.when(s == 0)
    def _():
        st_ref[...] = jnp.zeros(st_ref.shape, F32)

    n_sub = u_ref.shape[0] // (S5_SUB * nb)
    sub_rows = S5_SUB * nb

    def run(rev):
        ar, ai = are_ref[0], aim_ref[0]
        order = list(range(n_sub))[::-1] if rev else list(range(n_sub))

        def project_in(q):
            rs = slice(q * sub_rows, (q + 1) * sub_rows)
            u = u_ref[rs, :].astype(BF16)
            xr_ref[rs, :] = _dot(u, bre_ref[0])
            xi_ref[rs, :] = _dot(u, bim_ref[0])

        sr, si = st_ref[0], st_ref[1]
        project_in(order[0])
        for n, q in enumerate(order):
            if n + 1 < n_sub:
                project_in(order[n + 1])
            for j in range(S5_SUB):
                r0 = (q * S5_SUB + (S5_SUB - 1 - j if rev else j)) * nb
                nr = ar * sr - ai * si + xr_ref[r0:r0 + nb, :]
                ni = ar * si + ai * sr + xi_ref[r0:r0 + nb, :]
                xr_ref[r0:r0 + nb, :] = nr
                xi_ref[r0:r0 + nb, :] = ni
                sr, si = nr, ni
            rs = slice(q * sub_rows, (q + 1) * sub_rows)
            o_ref[0, rs, :] = (_dot(xr_ref[rs, :].astype(BF16), cre_ref[...])
                               - _dot(xi_ref[rs, :].astype(BF16), cim_ref[...])).astype(o_ref.dtype)
        st_ref[0] = sr
        st_ref[1] = si

    @pl.when(d == 0)
    def _():
        run(False)

    @pl.when(d == 1)
    def _():
        run(True)


def _s5(ut, bre, bim, are, aim, cre, cim, *, nb, nc, nt):
    n_rows, ch = ut.shape
    tr = S5_STEPS * nb
    ns = bre.shape[-1]
    pos = functools.partial(_scan_pos, ncb=nc, ntb=nt)
    return pl.pallas_call(
        functools.partial(_s5_kernel, nb),
        out_shape=jax.ShapeDtypeStruct((2, n_rows, ch), BF16),
        grid=(2, nt),
        in_specs=[pl.BlockSpec((tr, ch), lambda d, s: (pos(d, s), 0)),
                  pl.BlockSpec((1, ch, ns), lambda d, s: (d, 0, 0)),
                  pl.BlockSpec((1, ch, ns), lambda d, s: (d, 0, 0)),
                  pl.BlockSpec((1, nb, ns), lambda d, s: (d, 0, 0)),
                  pl.BlockSpec((1, nb, ns), lambda d, s: (d, 0, 0)),
                  pl.BlockSpec((ns, ch), lambda d, s: (0, 0)),
                  pl.BlockSpec((ns, ch), lambda d, s: (0, 0))],
        out_specs=pl.BlockSpec((1, tr, ch), lambda d, s: (d, pos(d, s), 0)),
        scratch_shapes=[pltpu.VMEM((tr, ns), F32), pltpu.VMEM((tr, ns), F32), pltpu.VMEM((2, nb, ns), F32)],
        compiler_params=_cparams(("arbitrary", "arbitrary")),
        name="s5_scan",
    )(ut, bre, bim, are, aim, cre, cim)


def _head_norm(o, gain, dim):
    parts = []
    for h in range(o.shape[1] // LANE):
        seg = o[:, h * LANE:(h + 1) * LANE]
        ms = jnp.sum(seg * seg, axis=-1, keepdims=True) * (1.0 / dim)
        parts.append(seg * lax.rsqrt(ms + EPS))
    return jnp.concatenate(parts, axis=1) * gain


def _mix_kernel(with_router, ogf_ref, ogb_ref, gg_ref, ys_ref, u_ref, omf_ref, omb_ref, mo_ref, h_ref, mod_ref,
                gn_ref, mn_ref, sd_ref, gw_ref, gb_ref, wo_ref, n2_ref, *rest):
    if with_router:
        wr_ref, ho_ref, f_ref, rt_ref = rest
    else:
        ho_ref, f_ref = rest
    gla = _head_norm(ogf_ref[...].astype(F32) + ogb_ref[...].astype(F32), gn_ref[...], GLA_DV) * _silu(gg_ref[...].astype(F32))
    z = _gelu_tanh(ys_ref[0].astype(F32) + ys_ref[1].astype(F32) + sd_ref[...] * u_ref[...])
    s5 = z * jax.nn.sigmoid(_dot(z.astype(BF16), gw_ref[...]) + gb_ref[...])
    ml = _head_norm(omf_ref[...].astype(F32) + omb_ref[...].astype(F32), mn_ref[...], ML_DH) * jax.nn.sigmoid(mo_ref[...].astype(F32))
    mix = (_dot(gla.astype(BF16), wo_ref[0:HP]) + _dot(s5.astype(BF16), wo_ref[HP:HP + 2 * LANE])
           + _dot(ml.astype(BF16), wo_ref[HP + 2 * LANE:]))
    m = mod_ref[0]
    hn = h_ref[...] + m[2:3] * mix
    ho_ref[...] = hn
    f = _rmsnorm(hn, n2_ref[...]) * (1.0 + m[4:5]) + m[3:4]
    if not with_router:
        f_ref[...] = f.astype(f_ref.dtype)
    if with_router:
        f_hi, f_lo = _split_bf16(f, 2)
        logits = (_dot(f_hi, wr_ref[0]) + _dot(f_lo, wr_ref[0]) + _dot(f_hi, wr_ref[1])).T[0:2 * N_EXPERTS]
        row = lax.broadcasted_iota(jnp.int32, logits.shape, 0)
        l0 = jnp.where(row < N_EXPERTS, logits, NEG)
        m1 = jnp.max(l0, axis=0, keepdims=True)
        i1 = jnp.min(jnp.where(l0 == m1, row, 2 * N_EXPERTS), axis=0, keepdims=True)
        l1 = jnp.where(row == i1, NEG, l0)
        m2 = jnp.max(l1, axis=0, keepdims=True)
        i2 = jnp.min(jnp.where(l1 == m2, row, 2 * N_EXPERTS), axis=0, keepdims=True)
        e = jnp.exp(m2 - m1)
        w1 = 1.0 / (1.0 + e)
        w2 = e / (1.0 + e)
        rt = jnp.where(row == 0, i1.astype(F32),
                       jnp.where(row == 1, i2.astype(F32),
                                 jnp.where(row == 2, w1, jnp.where(row == 3, w2, 0.0))))
        rt_ref[...] = rt
        d = f.shape[1]
        f_ref[:, 0:d // 2] = _pack_bf16_pairs(f)
        f_ref[:, d // 2:] = jnp.concatenate([rt, jnp.zeros((LANE - rt.shape[0], rt.shape[1]), F32)], axis=0).T


def _mix(ogf, ogb, pg, ys, pu, omf, omb, pm, h, modtab, gn, mn, sd, gw, gb, wo, n2, wr, *, layer, rows, lat_only):
    d = h.shape[1]
    tm = SEG
    n = rows.n_blocks(lat_only)
    src, sel = rows.src(lat_only), rows.sel(lat_only)
    full = lambda a: pl.BlockSpec(a.shape, lambda i: (0,) * a.ndim)
    with_router = wr is not None
    in_specs = [pl.BlockSpec((tm, HP), lambda i: (src(i), 0)),
                pl.BlockSpec((tm, HP), lambda i: (src(i), 0)),
                pl.BlockSpec((tm, HP), lambda i: (src(i), 0)),
                pl.BlockSpec((2, tm, 2 * LANE), lambda i: (0, src(i), 0)),
                pl.BlockSpec((tm, 2 * LANE), lambda i: (src(i), 0)),
                pl.BlockSpec((tm, HP), lambda i: (src(i), 0)),
                pl.BlockSpec((tm, HP), lambda i: (src(i), 0)),
                pl.BlockSpec((tm, HP), lambda i: (src(i), 1)),
                pl.BlockSpec((tm, d), lambda i: (src(i), 0)),
                pl.BlockSpec((None, 1, 8, d), lambda i: (layer, sel(i), 0, 0)),
                full(gn), full(mn), full(sd), full(gw), full(gb), full(wo), full(n2)]
    args = [ogf, ogb, pg, ys, pu, omf, omb, pm, h, modtab, gn, mn, sd, gw, gb, wo, n2]
    fw, fdt = (d // 2 + LANE, F32) if with_router else (d, BF16)
    out_shape = [jax.ShapeDtypeStruct((n * tm, d), F32), jax.ShapeDtypeStruct((n * tm, fw), fdt)]
    out_specs = [pl.BlockSpec((tm, d), lambda i: (i, 0)), pl.BlockSpec((tm, fw), lambda i: (i, 0))]
    if with_router:
        in_specs.append(full(wr))
        args.append(wr)
        out_shape.append(jax.ShapeDtypeStruct((2 * N_EXPERTS, n * tm), F32))
        out_specs.append(pl.BlockSpec((2 * N_EXPERTS, tm), lambda i: (0, i)))
    return pl.pallas_call(
        functools.partial(_mix_kernel, with_router),
        out_shape=tuple(out_shape),
        grid=(n,),
        in_specs=in_specs,
        out_specs=tuple(out_specs),
        compiler_params=_cparams(("arbitrary",)),
        name="mix_out",
    )(*args)


FF_TILE = 256


def _swiglu(xb, w1_ref, w3_ref, w2_ref, a_ref, lead=()):
    dff = w1_ref.shape[-1]
    for j in range(dff // FF_TILE):
        sl = slice(j * FF_TILE, (j + 1) * FF_TILE)
        h1 = _dot(xb, w1_ref[lead + (slice(None), sl)])
        h3 = _dot(xb, w3_ref[lead + (slice(None), sl)])
        a_ref[:, sl] = (_silu(h1) * h3).astype(BF16)
    return _dot(a_ref[...], w2_ref[lead + (slice(None), slice(None))])


FFN_TM = 1024
MIXFFN_TM = 2 * SEG


def _mix_ffn_kernel(sel, ogf_ref, ogb_ref, gg_ref, ys_ref, u_ref, omf_ref, omb_ref, mo_ref, h_ref, mod_ref,
                    gn_ref, mn_ref, sd_ref, gw_ref, gb_ref, wo_ref, n2_ref, w1_ref, w3_ref, w2_ref, o_ref,
                    a_ref, hn_ref, f_ref):
    n_seg = h_ref.shape[0] // SEG
    mods = []
    for q in range(n_seg):
        rs = slice(q * SEG, (q + 1) * SEG)
        m = mod_ref[sel(pl.program_id(0) * n_seg + q)]
        mods.append(m)
        gla = (_head_norm(ogf_ref[rs, :].astype(F32) + ogb_ref[rs, :].astype(F32), gn_ref[...], GLA_DV)
               * _silu(gg_ref[rs, :].astype(F32)))
        z = _gelu_tanh(ys_ref[0, rs, :].astype(F32) + ys_ref[1, rs, :].astype(F32) + sd_ref[...] * u_ref[rs, :])
        s5 = z * jax.nn.sigmoid(_dot(z.astype(BF16), gw_ref[...]) + gb_ref[...])
        ml = (_head_norm(omf_ref[rs, :].astype(F32) + omb_ref[rs, :].astype(F32), mn_ref[...], ML_DH)
              * jax.nn.sigmoid(mo_ref[rs, :].astype(F32)))
        mix = (_dot(gla.astype(BF16), wo_ref[0:HP]) + _dot(s5.astype(BF16), wo_ref[HP:HP + 2 * LANE])
               + _dot(ml.astype(BF16), wo_ref[HP + 2 * LANE:]))
        hn = h_ref[rs, :] + m[2:3] * mix
        hn_ref[rs, :] = hn
        f_ref[rs, :] = (_rmsnorm(hn, n2_ref[...]) * (1.0 + m[4:5]) + m[3:4]).astype(BF16)
    y = _swiglu(f_ref[...], w1_ref, w3_ref, w2_ref, a_ref)
    for q in range(n_seg):
        rs = slice(q * SEG, (q + 1) * SEG)
        o_ref[rs, :] = hn_ref[rs, :] + mods[q][5:6] * y[rs]


def _mix_ffn(ogf, ogb, pg, ys, pu, omf, omb, pm, h, modtab, gn, mn, sd, gw, gb, wo, n2, w1, w3, w2, *, layer, rows):
    r, d = h.shape
    tm = MIXFFN_TM
    assert r % tm == 0
    dff = w1.shape[-1]
    full = lambda a: pl.BlockSpec(a.shape, lambda i: (0,) * a.ndim)
    resident = lambda a: pl.BlockSpec(a.shape, lambda i: (0,) * a.ndim, pipeline_mode=pl.Buffered(1))
    in_specs = [pl.BlockSpec((tm, HP), lambda i: (i, 0)),
                pl.BlockSpec((tm, HP), lambda i: (i, 0)),
                pl.BlockSpec((tm, HP), lambda i: (i, 0)),
                pl.BlockSpec((2, tm, 2 * LANE), lambda i: (0, i, 0)),
                pl.BlockSpec((tm, 2 * LANE), lambda i: (i, 0)),
                pl.BlockSpec((tm, HP), lambda i: (i, 0)),
                pl.BlockSpec((tm, HP), lambda i: (i, 0)),
                pl.BlockSpec((tm, HP), lambda i: (i, 1)),
                pl.BlockSpec((tm, d), lambda i: (i, 0)),
                pl.BlockSpec((None,) + modtab.shape[1:], lambda i: (layer, 0, 0, 0)),
                full(gn), full(mn), full(sd), full(gw), full(gb), resident(wo), full(n2),
                resident(w1), resident(w3), resident(w2)]
    return pl.pallas_call(
        functools.partial(_mix_ffn_kernel, rows.sel(False)),
        out_shape=jax.ShapeDtypeStruct((r, d), F32),
        grid=(r // tm,),
        in_specs=in_specs,
        out_specs=pl.BlockSpec((tm, d), lambda i: (i, 0)),
        scratch_shapes=[pltpu.VMEM((tm, dff), BF16), pltpu.VMEM((tm, d), F32), pltpu.VMEM((tm, d), BF16)],
        compiler_params=_cparams(("arbitrary",)),
        name="mix_ffn",
    )(ogf, ogb, pg, ys, pu, omf, omb, pm, h, modtab, gn, mn, sd, gw, gb, wo, n2, w1, w3, w2)


def _ffn_kernel(final, sel, f_ref, h_ref, mod_ref, w1_ref, w3_ref, w2_ref, *rest):
    if final:
        nf_ref, o_ref, a_ref = rest
    else:
        o_ref, a_ref = rest
    y = _swiglu(f_ref[...], w1_ref, w3_ref, w2_ref, a_ref)
    n_seg = f_ref.shape[0] // SEG
    for q in range(n_seg):
        rs = slice(q * SEG, (q + 1) * SEG)
        gate = mod_ref[sel(pl.program_id(0) * n_seg + q)][5:6]
        hn = h_ref[rs, :] + gate * y[rs]
        o_ref[rs, :] = _rmsnorm(hn, nf_ref[...]) if final else hn


def _ffn(f, h, modtab, w1, w3, w2, nf, *, layer, rows, lat_only):
    r, d = h.shape
    tm = FFN_TM
    assert r % tm == 0
    dff = w1.shape[-1]
    full = lambda a: pl.BlockSpec(a.shape, lambda i: (0,) * a.ndim)
    resident = lambda a: pl.BlockSpec(a.shape, lambda i: (0,) * a.ndim, pipeline_mode=pl.Buffered(1))
    in_specs = [pl.BlockSpec((tm, d), lambda i: (i, 0)),
                pl.BlockSpec((tm, d), lambda i: (i, 0)),
                pl.BlockSpec((None,) + modtab.shape[1:], lambda i: (layer, 0, 0, 0)),
                resident(w1), resident(w3), resident(w2)]
    args = [f, h, modtab, w1, w3, w2]
    if lat_only:
        in_specs.append(full(nf))
        args.append(nf)
    return pl.pallas_call(
        functools.partial(_ffn_kernel, lat_only, rows.sel(lat_only)),
        out_shape=jax.ShapeDtypeStruct((r, d), F32),
        grid=(r // tm,),
        in_specs=in_specs,
        out_specs=pl.BlockSpec((tm, d), lambda i: (i, 0)),
        scratch_shapes=[pltpu.VMEM((tm, dff), BF16)],
        compiler_params=_cparams(("arbitrary",)),
        name="ffn",
    )(*args)


MOE_TM = 512


def _moe_kernel(be_ref, live_ref, x_ref, w1_ref, w3_ref, w2_ref, o_ref, a_ref):
    i = pl.program_id(0)
    d = w1_ref.shape[1]

    @pl.when(live_ref[i] > 0)
    def _():
        xa, xb = _unpack_bf16_pairs(x_ref[:, 0:d // 2])
        x = jnp.concatenate([xa.astype(BF16), xb.astype(BF16)], axis=1)
        y = _swiglu(x, w1_ref, w3_ref, w2_ref, a_ref, lead=(0,))
        tail = x_ref[:, d // 2:]
        mine = tail[:, 0:1] == be_ref[i].astype(F32)
        o_ref[...] = _pack_bf16_pairs(y * jnp.where(mine, tail[:, 2:3], tail[:, 3:4]))

    @pl.when(live_ref[i] == 0)
    def _():
        o_ref[...] = jnp.zeros(o_ref.shape, F32)


def _moe_experts(block_expert, block_live, xg, w1, w3, w2):
    n_rows, dx = xg.shape
    d = w1.shape[-2]
    dff = w1.shape[-1]
    tm = MOE_TM
    return pl.pallas_call(
        _moe_kernel,
        out_shape=jax.ShapeDtypeStruct((n_rows, d // 2), F32),
        grid_spec=pltpu.PrefetchScalarGridSpec(
            num_scalar_prefetch=2,
            grid=(n_rows // tm,),
            in_specs=[pl.BlockSpec((tm, dx), lambda i, be, lv: (i, 0)),
                      pl.BlockSpec((1, d, dff), lambda i, be, lv: (be[i], 0, 0)),
                      pl.BlockSpec((1, d, dff), lambda i, be, lv: (be[i], 0, 0)),
                      pl.BlockSpec((1, dff, d), lambda i, be, lv: (be[i], 0, 0))],
            out_specs=pl.BlockSpec((tm, d // 2), lambda i, be, lv: (i, 0)),
            scratch_shapes=[pltpu.VMEM((tm, dff), BF16)]),
        compiler_params=_cparams(("arbitrary",)),
        name="moe_experts",
    )(block_expert, block_live, xg, w1, w3, w2)


SC_GATHER_ROWS = 64


def _gather_rows(table, idx):
    n_k, n_idx = idx.shape
    _, d = table.shape
    info = plsc.get_sparse_core_info()
    n_cores, n_workers = info.num_cores, info.num_cores * info.num_subcores
    assert n_idx % (n_workers * SC_GATHER_ROWS) == 0
    per_worker = n_idx // n_workers
    mesh = plsc.VectorSubcoreMesh(core_axis_name="c", subcore_axis_name="s")

    @functools.partial(
        pl.kernel, mesh=mesh,
        out_type=jax.ShapeDtypeStruct((n_k * n_idx, d), table.dtype),
        scratch_types=[pltpu.VMEM((SC_GATHER_ROWS,), jnp.int32),
                       pltpu.VMEM((SC_GATHER_ROWS, d), table.dtype),
                       pltpu.SemaphoreType.DMA])
    def gather(table_hbm, idx_hbm, out_hbm, idx_v, rows_v, sem):
        base = (lax.axis_index("s") * n_cores + lax.axis_index("c")) * per_worker

        @pl.loop(0, per_worker // SC_GATHER_ROWS)
        def _(it):
            for k in range(n_k):
                off = pl.multiple_of(k * n_idx + base + it * SC_GATHER_ROWS, SC_GATHER_ROWS)
                pltpu.sync_copy(idx_hbm.at[pl.ds(off, SC_GATHER_ROWS)], idx_v)
                pltpu.async_copy(table_hbm.at[idx_v], rows_v, sem).wait()
                pltpu.sync_copy(rows_v, out_hbm.at[pl.ds(off, SC_GATHER_ROWS)])

    return gather(table, idx.reshape(-1)).reshape(n_k, n_idx, d)


def _scatter_rows(src, dest, n_rows):
    n_tok, d = src.shape
    info = plsc.get_sparse_core_info()
    n_cores, n_workers = info.num_cores, info.num_cores * info.num_subcores
    assert n_tok % (n_workers * SC_GATHER_ROWS) == 0
    per_worker = n_tok // n_workers
    mesh = plsc.VectorSubcoreMesh(core_axis_name="c", subcore_axis_name="s")

    @functools.partial(
        pl.kernel, mesh=mesh,
        out_type=jax.ShapeDtypeStruct((n_rows, d), src.dtype),
        scratch_types=[pltpu.VMEM((SC_GATHER_ROWS,), jnp.int32),
                       pltpu.VMEM((SC_GATHER_ROWS, d), src.dtype),
                       pltpu.SemaphoreType.DMA])
    def scatter(src_hbm, dest_hbm, out_hbm, idx_v, rows_v, sem):
        base = (lax.axis_index("s") * n_cores + lax.axis_index("c")) * per_worker

        @pl.loop(0, per_worker // SC_GATHER_ROWS)
        def _(it):
            off = pl.multiple_of(base + it * SC_GATHER_ROWS, SC_GATHER_ROWS)
            pltpu.sync_copy(src_hbm.at[pl.ds(off, SC_GATHER_ROWS)], rows_v)
            for k in range(TOP_K):
                pltpu.sync_copy(dest_hbm.at[pl.ds(k * n_tok + off, SC_GATHER_ROWS)], idx_v)
                pltpu.async_copy(rows_v, out_hbm.at[idx_v], sem).wait()

    return scatter(src, dest.reshape(-1))


def _pack_bf16_pairs(y):
    n = y.shape[1] // 2
    hi = pltpu.bitcast(y[:, :n].astype(BF16).astype(F32), jnp.uint32)
    lo = pltpu.bitcast(y[:, n:].astype(BF16).astype(F32), jnp.uint32)
    return pltpu.bitcast(hi | (lo >> 16), F32)


def _unpack_bf16_pairs(w):
    u = pltpu.bitcast(w, jnp.uint32)
    return pltpu.bitcast(u & jnp.uint32(0xFFFF0000), F32), pltpu.bitcast(u << 16, F32)


def _resid_kernel(final, h_ref, y_ref, mod_ref, *rest):
    a0, b0 = _unpack_bf16_pairs(y_ref[0])
    a1, b1 = _unpack_bf16_pairs(y_ref[1])
    hn = h_ref[...] + mod_ref[0][5:6] * jnp.concatenate([a0 + a1, b0 + b1], axis=1)
    if final:
        nf_ref, o_ref = rest
        o_ref[...] = _rmsnorm(hn, nf_ref[...])
    else:
        (o_ref,) = rest
        o_ref[...] = hn


def _moe_resid(h, y01, modtab, nf, *, layer, rows, lat_only):
    r, d = h.shape
    tm = SEG
    sel = rows.sel(lat_only)
    row = lambda i: (i, 0)
    in_specs = [pl.BlockSpec((tm, d), row), pl.BlockSpec((TOP_K, tm, d // 2), lambda i: (0, i, 0)),
                pl.BlockSpec((None, 1, 8, d), lambda i: (layer, sel(i), 0, 0))]
    args = [h, y01, modtab]
    if lat_only:
        in_specs.append(pl.BlockSpec((1, d), lambda i: (0, 0)))
        args.append(nf)
    return pl.pallas_call(
        functools.partial(_resid_kernel, lat_only),
        out_shape=jax.ShapeDtypeStruct((r, d), F32),
        grid=(r // tm,),
        in_specs=in_specs,
        out_specs=pl.BlockSpec((tm, d), row),
        compiler_params=_cparams(("arbitrary",)),
        name="moe_resid",
    )(*args)


CAST_PARTS = 16


def _cast_kernel(*refs):
    o_ref = refs[-1]
    tr = refs[0].shape[1]
    for k, x_ref in enumerate(refs[:-1]):
        o_ref[0, k * tr:(k + 1) * tr, :] = x_ref[0].astype(o_ref.dtype)


def _to_bf16(w, j):
    lead, (r, c) = w.shape[1:-2], w.shape[-2:]
    n = int(np.prod(lead, dtype=np.int64))
    w3 = w.reshape((-1, r, c))
    tr = r // CAST_PARTS
    assert tr * CAST_PARTS == r and tr % 16 == 0
    band = lambda k: pl.BlockSpec((1, tr, c), lambda e: (j * n + e, k, 0))
    out = pl.pallas_call(
        _cast_kernel,
        out_shape=jax.ShapeDtypeStruct((n, r, c), BF16),
        grid=(n,),
        in_specs=[band(k) for k in range(CAST_PARTS)],
        out_specs=pl.BlockSpec((1, r, c), lambda e: (e, 0, 0)),
        compiler_params=_cparams(("arbitrary",)),
        name="to_bf16",
    )(*([w3] * CAST_PARTS))
    return out.reshape(lead + (r, c))


def _pad_heads(w, heads, dim, to=LANE):
    lead = w.shape[:-1]
    w = w.reshape(lead + (heads, dim))
    w = jnp.pad(w, [(0, 0)] * len(lead) + [(0, 0), (0, to - dim)])
    return w.reshape(lead + (heads * to,))


def _pad_last(w, to):
    return jnp.pad(w, [(0, 0)] * (w.ndim - 1) + [(0, to - w.shape[-1])])


def _pos_embed(n_tokens, d):
    n_grid_rows = n_tokens // GRID_W
    row, col = jnp.meshgrid(jnp.arange(n_grid_rows, dtype=F32), jnp.arange(GRID_W, dtype=F32), indexing='ij')
    n_freq = d // 4
    omega = jnp.exp(-math.log(POS_BASE) * jnp.arange(n_freq, dtype=F32) / n_freq)

    def axis_embed(p):
        ang = p.reshape(-1, 1) * omega
        return jnp.concatenate([jnp.sin(ang), jnp.cos(ang)], axis=-1)

    return jnp.concatenate([axis_embed(row), axis_embed(col)], axis=-1)


def _s5_discretise(lam_re, lam_im, log_dt, b_re, b_im):
    dt = jnp.exp(log_dt)[:, None]
    mag = jnp.exp(lam_re * dt)
    abar_re, abar_im = mag * jnp.cos(lam_im * dt), mag * jnp.sin(lam_im * dt)
    den = lam_re * lam_re + lam_im * lam_im
    pr, pi = abar_re - 1.0, abar_im
    coef_re = (pr * lam_re + pi * lam_im) / den
    coef_im = (pi * lam_re - pr * lam_im) / den
    bbar_re = coef_re[..., None] * b_re - coef_im[..., None] * b_im
    bbar_im = coef_re[..., None] * b_im + coef_im[..., None] * b_re
    return abar_re, abar_im, bbar_re, bbar_im


def _block_diag(m):
    g, a, b = m.shape
    eye = jnp.eye(g, dtype=m.dtype)
    return (eye[:, None, :, None] * m[:, :, None, :]).reshape(g * a, g * b)


def _route_plan(route, tm):
    n_tok = route.shape[1]
    n_assign = n_tok * TOP_K
    flat_e = route[0:TOP_K].astype(jnp.int32).reshape(-1)
    onehot = (jnp.arange(N_EXPERTS, dtype=jnp.int32)[:, None] == flat_e[None, :]).astype(jnp.int32)
    csum = jnp.cumsum(onehot, axis=1)
    counts = csum[:, -1]
    padded = (counts + tm - 1) // tm * tm
    pend = jnp.cumsum(padded)
    pstart = pend - padded
    dest = jnp.sum(onehot * (csum - 1 + pstart[:, None]), axis=0)
    n_blocks = -(-n_assign // tm) + N_EXPERTS
    block_start = jnp.arange(n_blocks, dtype=jnp.int32) * tm
    block_expert = jnp.minimum(jnp.searchsorted(pend, block_start, side='right'), N_EXPERTS - 1).astype(jnp.int32)
    block_live = (block_start < (pstart + counts)[block_expert]).astype(jnp.int32)
    return n_blocks * tm, block_expert, block_live, dest.reshape(TOP_K, n_tok)


def kernel(x, c, ctx, c_ctx, w_ada, b_ada, norm1, norm2, w_in, w_out, gla_wa2, gla_ba, gla_norm, s5_lam_re, s5_lam_im, s5_log_dt, s5_b_re, s5_b_im, s5_c_re, s5_c_im, s5_d, s5_glu_w, s5_glu_b, ml_conv_w, ml_conv_b, ml_gate_b, ml_norm, ffn_w1, ffn_w3, ffn_w2, moe_router, moe_w1, moe_w3, moe_w2, norm_f):
    nb, n_lat, d = x.shape
    lc = ctx.shape[1]
    depth = w_ada.shape[0]
    l = lc + n_lat
    assert lc % SEG == 0 and n_lat % SEG == 0 and nb == 8 and nb % SCAN_BATCHES == 0
    rows = _Rows(nb, lc // SEG, l // SEG)

    h = _embed(ctx.reshape(nb * lc, d), x.reshape(nb * n_lat, d), _pos_embed(n_lat, d), rows)

    cond = jnp.zeros((16, d), F32).at[:nb].set(c).at[nb].set(c_ctx)
    mod = _modulation(cond, w_ada, b_ada)
    modtab = jnp.pad(mod.reshape(depth, 16, 6, d), ((0, 0), (0, 0), (0, 2), (0, 0)))

    dk, dv, dh = GLA_HEADS * GLA_DK, GLA_HEADS * GLA_DV, ML_HEADS * ML_DH
    s5c = s5_d.shape[-1]
    cuts = np.cumsum([dk, dk, dv, GLA_RANK, dv, s5c, dh, dh, dh, dh, 4 * ML_HEADS])

    for i in range(depth):
        last = i == depth - 1
        gq, gk, gv, glr, gg, su, mq, mk, mv, mo, mg = jnp.split(w_in[i], cuts[:-1], axis=-1)
        w_all = jnp.concatenate([
            _pad_heads(gq, GLA_HEADS, GLA_DK, GLA_KP), _pad_heads(gk, GLA_HEADS, GLA_DK, GLA_KP),
            _pad_heads(gv, GLA_HEADS, GLA_DV), _pad_last(glr, LANE),
            su,
            _pad_heads(mv, ML_HEADS, ML_DH),
            _pad_last(mg[:, :2 * ML_HEADS], LANE), _pad_last(mg[:, 2 * ML_HEADS:], LANE),
            _pad_heads(mq, ML_HEADS, ML_DH), _pad_heads(mk, ML_HEADS, ML_DH),
            _pad_heads(gg, GLA_HEADS, GLA_DV), _pad_heads(mo, ML_HEADS, ML_DH)],
            axis=-1).astype(BF16)
        wgt = mg.T.astype(BF16)
        cw = jnp.concatenate([_pad_heads(ml_conv_w[i][:, :dh], ML_HEADS, ML_DH),
                              _pad_heads(ml_conv_w[i][:, dh:], ML_HEADS, ML_DH)], axis=-1)
        cw = jnp.pad(cw, ((0, 8 - ML_CONV), (0, 0)))
        cb = jnp.concatenate([_pad_heads(ml_conv_b[i][:dh], ML_HEADS, ML_DH),
                              _pad_heads(ml_conv_b[i][dh:], ML_HEADS, ML_DH)])[None]
        post = jnp.concatenate([jnp.ones((HP,), F32), jnp.full((HP,), ML_DH ** -0.5, F32)])[None]
        pg, pu, pm, qk, go, gt = _proj(h, modtab, norm1[i][None], w_all, wgt, cw, cb, post, layer=i, rows=rows)

        wa = jnp.pad(_pad_heads(gla_wa2[i], GLA_HEADS, GLA_DK, GLA_KP), ((0, 0), (0, LANE - GLA_RANK), (0, 0)))
        ba = _pad_heads(gla_ba[i], GLA_HEADS, GLA_DK, GLA_KP)[:, None, :]
        pg3 = pg.reshape(nb, l, NG)
        ogf = _gla(pg3, wa[0], ba[0], rows=rows, rev=False).reshape(nb * l, HP)
        ogb = _gla(pg3, wa[1], ba[1], rows=rows, rev=True).reshape(nb * l, HP)

        bres, bims, ares, aims = [], [], [], []
        for dr in (0, 1):
            a_re, a_im, b_re, b_im = _s5_discretise(s5_lam_re[i, dr], s5_lam_im[i, dr], s5_log_dt[i, dr],
                                                    s5_b_re[i], s5_b_im[i])
            bres.append(_block_diag(jnp.swapaxes(b_re, 1, 2)))
            bims.append(_block_diag(jnp.swapaxes(b_im, 1, 2)))
            ares.append(jnp.broadcast_to(a_re.reshape(1, -1), (nb, a_re.size)))
            aims.append(jnp.broadcast_to(a_im.reshape(1, -1), (nb, a_im.size)))
        cre = _block_diag(jnp.swapaxes(s5_c_re[i], 1, 2)).astype(BF16)
        cim = _block_diag(jnp.swapaxes(s5_c_im[i], 1, 2)).astype(BF16)
        ut = pu.reshape(nb, l, s5c).swapaxes(0, 1).reshape(l * nb, s5c)
        yt = _s5(ut, jnp.stack(bres).astype(BF16), jnp.stack(bims).astype(BF16), jnp.stack(ares), jnp.stack(aims),
                 cre, cim, nb=nb, nc=lc // S5_STEPS, nt=l // S5_STEPS)
        ys = yt.reshape(2, l, nb, s5c).swapaxes(1, 2).reshape(2, nb * l, s5c)

        gb = ml_gate_b[i].reshape(2, 2, ML_HEADS)
        gbr = _pad_last(gb.reshape(2, 1, 2 * ML_HEADS), LANE)
        gbt = _pad_last(jnp.broadcast_to(gb[..., None], (2, 2, ML_HEADS, CHUNK)), LANE).reshape(2, 2, HP)
        gtl = _pad_last(gt.reshape(2, 2, ML_HEADS, nb, l // CHUNK, CHUNK).transpose(3, 4, 0, 1, 2, 5),
                        LANE).reshape(nb, l // CHUNK, 2, 2, HP)
        qk3, pm3 = qk.reshape(nb, l, 2 * HP), pm.reshape(nb, l, NM)
        omf = _mlstm(qk3, pm3, gtl, gbr[0], gbt[0], rows=rows, rev=False).reshape(nb * l, HP)
        omb = _mlstm(qk3, pm3, gtl, gbr[1], gbt[1], rows=rows, rev=True).reshape(nb * l, HP)

        wo = w_out[i]
        wo_p = jnp.concatenate([
            jnp.pad(wo[:dv].reshape(GLA_HEADS, GLA_DV, d), ((0, 0), (0, LANE - GLA_DV), (0, 0))).reshape(HP, d),
            wo[dv:dv + s5c],
            jnp.pad(wo[dv + s5c:].reshape(ML_HEADS, ML_DH, d), ((0, 0), (0, LANE - ML_DH), (0, 0))).reshape(HP, d)],
            axis=0).astype(BF16)
        gn = jnp.tile(_pad_last(gla_norm[i], LANE), GLA_HEADS)[None]
        mn = jnp.tile(_pad_last(ml_norm[i], LANE), ML_HEADS)[None]
        is_moe = i % 2 == 1
        j = i // 2
        wr = jnp.stack(_split_bf16(_pad_last(moe_router[j], LANE), 2)) if is_moe else None
        mix_args = (ogf, ogb, go, ys, pu, omf, omb, go, h, modtab, gn, mn, s5_d[i][None], s5_glu_w[i].astype(BF16),
                    s5_glu_b[i][None], wo_p, norm2[i][None])
        if not is_moe and not last:
            h = _mix_ffn(*mix_args, _to_bf16(ffn_w1, j), _to_bf16(ffn_w3, j), _to_bf16(ffn_w2, j),
                         layer=i, rows=rows)
            continue
        outs = _mix(*mix_args, wr, layer=i, rows=rows, lat_only=last)
        if not is_moe:
            h, f = outs
            h = _ffn(f, h, modtab, _to_bf16(ffn_w1, j), _to_bf16(ffn_w3, j), _to_bf16(ffn_w2, j),
                     norm_f[None], layer=i, rows=rows, lat_only=last)
        else:
            h, f, route = outs
            n_rows, block_expert, block_live, dest = _route_plan(route, MOE_TM)
            xg = _scatter_rows(f, dest, n_rows)
            yg = _moe_experts(block_expert, block_live, xg, _to_bf16(moe_w1, j), _to_bf16(moe_w3, j),
                              _to_bf16(moe_w2, j))
            y01 = _gather_rows(yg, dest)
            h = _moe_resid(h, y01, modtab, norm_f[None], layer=i, rows=rows, lat_only=last)
    return h.reshape(nb, n_lat, d)
```

```python
import functools
import math

import numpy as np
import jax
import jax.numpy as jnp
from jax import lax
from jax.experimental import pallas as pl
from jax.experimental.pallas import tpu as pltpu
from jax.experimental.pallas import tpu_sc as plsc

F32 = jnp.float32
BF16 = jnp.bfloat16
HIGHEST = lax.Precision.HIGHEST

GRID_W = 64
POS_BASE = 10000.0
EPS = 1e-6
GLA_HEADS, GLA_DK, GLA_DV, GLA_RANK, GLA_GATE_NORM = 4, 48, 96, 16, 16.0
S5_GROUP, S5_STATE = 16, 64
ML_HEADS, ML_DH, ML_CONV = 4, 96, 3
N_EXPERTS, TOP_K = 8, 2

LANE = 128
CHUNK = 64
SEG = 256
N_SUB = SEG // CHUNK
SCAN_BATCHES = 8
S5_STEPS = 256
S5_SUB = 32
NEG = -1e30
V7X_VMEM_BYTES = 64 * 1024 * 1024
VMEM_LIMIT = V7X_VMEM_BYTES - 8 * 1024 * 1024

HP = LANE * GLA_HEADS
GLA_KP = 64
GQ = GLA_HEADS * GLA_KP
NG = 2 * GQ + HP + LANE
NM = HP + 2 * LANE
NGO = 2 * HP


def _cparams(sem):
    return pltpu.CompilerParams(dimension_semantics=sem, vmem_limit_bytes=VMEM_LIMIT)


def _dot(a, b, **kw):
    return jnp.dot(a, b, preferred_element_type=F32, **kw)


def _dot_nt(a, b, **kw):
    return lax.dot_general(a, b, (((1,), (1,)), ((), ())), preferred_element_type=F32, **kw)


def _dot_tn(a, b, **kw):
    return lax.dot_general(a, b, (((0,), (0,)), ((), ())), preferred_element_type=F32, **kw)


def _log_sigmoid(x):
    return jnp.minimum(x, 0.0) - jnp.log1p(jnp.exp(-jnp.abs(x)))


def _silu(x):
    return x * jax.nn.sigmoid(x)


def _gelu_tanh(x):
    return 0.5 * x * (1.0 + jnp.tanh(math.sqrt(2.0 / math.pi) * (x + 0.044715 * (x * x * x))))


def _rmsnorm(x, g):
    return x * lax.rsqrt(jnp.mean(x * x, axis=-1, keepdims=True) + EPS) * g


class _Rows:
    def __init__(self, nb, ncb, ntb):
        self.nb, self.ncb, self.ntb, self.nlb = nb, ncb, ntb, ntb - ncb

    def n_blocks(self, lat_only):
        return self.nb * (self.nlb if lat_only else self.ntb)

    def src(self, lat_only):
        if lat_only:
            return lambda i: (i // self.nlb) * self.ntb + self.ncb + i % self.nlb
        return lambda i: i

    def sel(self, lat_only):
        if lat_only:
            return lambda i: i // self.nlb
        return lambda i: jnp.where(i % self.ntb < self.ncb, self.nb, i // self.ntb)


def _embed_kernel(ncb, ntb, ctx_ref, x_ref, pos_ref, o_ref):
    j = pl.program_id(0) % ntb

    @pl.when(j < ncb)
    def _():
        o_ref[...] = ctx_ref[...]

    @pl.when(j >= ncb)
    def _():
        o_ref[...] = x_ref[...] + pos_ref[...]


def _embed(ctx2, x2, pos, rows):
    d = ctx2.shape[1]
    ncb, ntb, nlb = rows.ncb, rows.ntb, rows.nlb
    return pl.pallas_call(
        functools.partial(_embed_kernel, ncb, ntb),
        out_shape=jax.ShapeDtypeStruct((rows.nb * ntb * SEG, d), F32),
        grid=(rows.nb * ntb,),
        in_specs=[pl.BlockSpec((SEG, d), lambda i: ((i // ntb) * ncb + jnp.minimum(i % ntb, ncb - 1), 0)),
                  pl.BlockSpec((SEG, d), lambda i: ((i // ntb) * nlb + jnp.maximum(i % ntb - ncb, 0), 0)),
                  pl.BlockSpec((SEG, d), lambda i: (jnp.maximum(i % ntb - ncb, 0), 0))],
        out_specs=pl.BlockSpec((SEG, d), lambda i: (i, 0)),
        compiler_params=_cparams(("arbitrary",)),
        name="embed",
    )(ctx2, x2, pos)


def _mod_kernel(c_ref, w_ref, b_ref, o_ref):
    s = _silu(c_ref[...])
    o_ref[0] = _dot(s, w_ref[0], precision=HIGHEST) + b_ref[0]


def _modulation(cond, w_ada, b_ada):
    depth, d, n6 = w_ada.shape
    tn = n6 // 4
    n_rows = cond.shape[0]
    return pl.pallas_call(
        _mod_kernel,
        out_shape=jax.ShapeDtypeStruct((depth, n_rows, n6), F32),
        grid=(depth, n6 // tn),
        in_specs=[pl.BlockSpec((n_rows, d), lambda l, j: (0, 0)),
                  pl.BlockSpec((1, d, tn), lambda l, j: (l, 0, j)),
                  pl.BlockSpec((1, 1, tn), lambda l, j: (l, 0, j))],
        out_specs=pl.BlockSpec((1, n_rows, tn), lambda l, j: (l, 0, j)),
        compiler_params=_cparams(("arbitrary", "arbitrary")),
        name="modulation",
    )(cond, w_ada, b_ada.reshape(depth, 1, n6))


PROJ_TM = 2 * SEG


def _proj_kernel(ncb, ntb, sel, h_ref, hp_ref, hn_ref, mod_ref, g_ref, w_ref, wgt_ref, cw_ref, cb_ref, post_ref,
                 pg_ref, pu_ref, pm_ref, qk_ref, go_ref, gt_ref):
    tm = h_ref.shape[0]
    n_seg = tm // SEG
    blk = [pl.program_id(0) * n_seg + q for q in range(n_seg)]
    mods = [mod_ref[sel(bq)] for bq in blk]
    act = lambda x, m: _rmsnorm(x, g_ref[...]) * (1.0 + m[1:2]) + m[0:1]
    a_seg = [act(h_ref[q * SEG:(q + 1) * SEG, :], mods[q]) for q in range(n_seg)]
    ab = jnp.concatenate(a_seg, axis=0).astype(BF16)
    c1, c2 = NG + 2 * LANE, NG + 2 * LANE + NM
    c3 = c2 + 2 * HP
    go_ref[...] = _dot(ab, w_ref[:, c3:]).astype(go_ref.dtype)
    pg_ref[...] = _dot(ab, w_ref[:, 0:NG])
    pu_ref[...] = _dot(ab, w_ref[:, NG:c1])
    pm = _dot(ab, w_ref[:, c1:c2])
    pm_ref[...] = pm
    gates_t = pm[:, HP:HP + 2 * LANE].T
    gt_ref[...] = jnp.concatenate([gates_t[0:8], gates_t[LANE:LANE + 8]], axis=0)
    ae = jnp.concatenate([act(hp_ref[...], mods[0])] + a_seg + [act(hn_ref[...], mods[-1])], axis=0)
    xe = _dot(ae.astype(BF16), w_ref[:, c2:c3])
    n_e = tm + 16
    row = lax.broadcasted_iota(jnp.int32, (tm, xe.shape[1]), 0)
    keep_prev = jnp.ones((tm, xe.shape[1]), F32)
    keep_next = keep_prev
    for q, bq in enumerate(blk):
        j = bq % ntb
        first = jnp.logical_or(j == 0, j == ncb).astype(F32)
        last = jnp.logical_or(j == ncb - 1, j == ntb - 1).astype(F32)
        keep_prev = jnp.where(row == q * SEG, 1.0 - first, keep_prev)
        keep_next = jnp.where(row == (q + 1) * SEG - 1, 1.0 - last, keep_next)
    xp = pltpu.roll(xe, 1, axis=0)[8:8 + tm] * keep_prev
    xn = pltpu.roll(xe, n_e - 1, axis=0)[8:8 + tm] * keep_next
    y = cw_ref[0:1] * xp + cw_ref[1:2] * xe[8:8 + tm] + cw_ref[2:3] * xn + cb_ref[...]
    qk_ref[...] = _silu(y) * post_ref[...]


def _proj(h, modtab, g, w, wgt, cw, cb, post, *, layer, rows):
    r, d = h.shape
    tm = PROJ_TM
    assert r % tm == 0
    t8 = tm // 8
    full = lambda a: pl.BlockSpec(a.shape, lambda i: (0,) * a.ndim)
    return pl.pallas_call(
        functools.partial(_proj_kernel, rows.ncb, rows.ntb, rows.sel(False)),
        out_shape=(jax.ShapeDtypeStruct((r, NG), F32), jax.ShapeDtypeStruct((r, 2 * LANE), F32),
                   jax.ShapeDtypeStruct((r, NM), F32), jax.ShapeDtypeStruct((r, 2 * HP), F32),
                   jax.ShapeDtypeStruct((r, NGO), BF16),
                   jax.ShapeDtypeStruct((16, r), F32)),
        grid=(r // tm,),
        in_specs=[pl.BlockSpec((tm, d), lambda i: (i, 0)),
                  pl.BlockSpec((8, d), lambda i: (jnp.maximum(i * t8 - 1, 0), 0)),
                  pl.BlockSpec((8, d), lambda i: (jnp.minimum((i + 1) * t8, r // 8 - 1), 0)),
                  pl.BlockSpec((None,) + modtab.shape[1:], lambda i: (layer, 0, 0, 0)),
                  full(g), full(w), full(wgt), full(cw), full(cb), full(post)],
        out_specs=(pl.BlockSpec((tm, NG), lambda i: (i, 0)),
                   pl.BlockSpec((tm, 2 * LANE), lambda i: (i, 0)),
                   pl.BlockSpec((tm, NM), lambda i: (i, 0)),
                   pl.BlockSpec((tm, 2 * HP), lambda i: (i, 0)),
                   pl.BlockSpec((tm, NGO), lambda i: (i, 0)),
                   pl.BlockSpec((16, tm), lambda i: (0, i))),
        compiler_params=_cparams(("arbitrary",)),
        name="proj",
    )(h, h, h, modtab, g, w, wgt, cw, cb, post)


def _scan_pos(d, s, ncb, ntb):
    rev = jnp.where(s < ncb, ncb - 1 - s, ntb - 1 - (s - ncb))
    return jnp.where(d == 0, s, rev)


def _scan_pos_static(rev, s, ncb, ntb):
    if not rev:
        return s
    return jnp.where(s < ncb, ncb - 1 - s, ntb - 1 - (s - ncb))


def _tri(rev):
    r = lax.broadcasted_iota(jnp.int32, (CHUNK, CHUNK), 0)
    c = lax.broadcasted_iota(jnp.int32, (CHUNK, CHUNK), 1)
    return (r <= c) if rev else (r >= c)


def _chunk_rows(rev):
    return [(N_SUB - 1 - j if rev else j) * CHUNK for j in range(N_SUB)]


def _gla_kernel(rev, p_ref, wa_ref, ba_ref, o_ref, st_ref):
    s, g = pl.program_id(0), pl.program_id(1)
    nbb = p_ref.shape[0]
    b0 = g * nbb

    @pl.when(s == 0)
    def _():
        st_ref[pl.ds(b0, nbb)] = jnp.zeros((nbb,) + st_ref.shape[1:], F32)

    valid = _tri(rev)
    tri = valid.astype(F32)
    r0s = _chunk_rows(rev)
    wa, ba = wa_ref[...], ba_ref[...]
    inst = [(bb, j) for bb in range(nbb) for j in range(N_SUB)]
    heads = [slice(h * LANE, (h + 1) * LANE) for h in range(GLA_HEADS)]
    pairs = [slice((h // 2) * LANE, (h // 2 + 1) * LANE) for h in range(GLA_HEADS)]
    lane = lax.broadcasted_iota(jnp.int32, (CHUNK, LANE), 1)
    own = [(lane // GLA_KP) == (h % 2) for h in range(GLA_HEADS)]

    la = {}
    for bb, j in inst:
        lr = p_ref[bb, pl.ds(r0s[j], CHUNK), 2 * GQ + HP:NG]
        la[bb, j] = _log_sigmoid(_dot(lr, wa) + ba) * (1.0 / GLA_GATE_NORM)
    bc, e_last = {}, {}
    for i in inst:
        bc[i] = _dot_exact01(tri, la[i], lhs_is_01=True, pieces=2)
        e_last[i] = jnp.exp(jnp.sum(la[i], axis=0, keepdims=True))
    q_in, k_in, k_out, v = {}, {}, {}, {}
    for bb, j in inst:
        i = (bb, j)
        rs = pl.ds(r0s[j], CHUNK)
        qs = (p_ref[bb, rs, 0:GQ] * (GLA_DK ** -0.5) * jnp.exp(bc[i])).astype(BF16)
        for h in range(GLA_HEADS):
            q_in[i, h] = jnp.where(own[h], qs[:, pairs[h]], jnp.zeros_like(qs[:, pairs[h]]))
        kd = p_ref[bb, rs, GQ:2 * GQ] * jnp.exp(-bc[i])
        k_out[i] = (kd * e_last[i]).astype(BF16)
        k_in[i] = kd.astype(BF16)
        v[i] = p_ref[bb, rs, 2 * GQ:2 * GQ + HP].astype(BF16)
    att = {}
    for i in inst:
        for h in range(GLA_HEADS):
            att[i, h] = jnp.where(valid, _dot_nt(q_in[i, h], k_in[i][:, pairs[h]]), 0.0).astype(BF16)
    o_intra, ds = {}, {}
    for i in inst:
        for h, sl in enumerate(heads):
            o_intra[i, h] = _dot(att[i, h], v[i][:, sl])
            ds[i, h] = _dot_tn(v[i][:, sl], k_out[i][:, pairs[h]])
    s_in = {}
    for bb in range(nbb):
        for h in range(GLA_HEADS):
            st = st_ref[b0 + bb, h]
            for j in range(N_SUB):
                s_in[(bb, j), h] = st.astype(BF16)
                st = st * e_last[bb, j][:, pairs[h]] + ds[(bb, j), h]
            st_ref[b0 + bb, h] = st
    for bb, j in inst:
        for h, sl in enumerate(heads):
            o = o_intra[(bb, j), h] + _dot_nt(q_in[(bb, j), h], s_in[(bb, j), h])
            o_ref[bb, pl.ds(r0s[j], CHUNK), sl] = o.astype(o_ref.dtype)


def _gla(pg3, wa, ba, *, rows, rev):
    nb, l, _ = pg3.shape
    nbb = SCAN_BATCHES
    pos = functools.partial(_scan_pos_static, rev, ncb=rows.ncb, ntb=rows.ntb)
    return pl.pallas_call(
        functools.partial(_gla_kernel, rev),
        out_shape=jax.ShapeDtypeStruct((nb, l, HP), BF16),
        grid=(rows.ntb, nb // nbb),
        in_specs=[pl.BlockSpec((nbb, SEG, NG), lambda s, g: (g, pos(s), 0)),
                  pl.BlockSpec((LANE, GQ), lambda s, g: (0, 0)),
                  pl.BlockSpec((1, GQ), lambda s, g: (0, 0))],
        out_specs=pl.BlockSpec((nbb, SEG, HP), lambda s, g: (g, pos(s), 0)),
        scratch_shapes=[pltpu.VMEM((nb, GLA_HEADS, LANE, LANE), F32)],
        compiler_params=_cparams(("arbitrary", "arbitrary")),
        name="gla_scan_bwd" if rev else "gla_scan_fwd",
    )(pg3, wa, ba)


def _split_bf16(x, n):
    parts, r = [], x
    for _ in range(n):
        p = r.astype(BF16)
        parts.append(p)
        r = r - p.astype(F32)
    return parts


def _dot_exact01(a, b, lhs_is_01, pieces=3):
    if lhs_is_01:
        a = a.astype(BF16)
        terms = [_dot(a, p) for p in _split_bf16(b, pieces)]
    else:
        b = b.astype(BF16)
        terms = [_dot(p, b) for p in _split_bf16(a, pieces)]
    return functools.reduce(lambda x, y: x + y, terms)


def _cummax_rows(a, rev):
    n = a.shape[0]
    row = lax.broadcasted_iota(jnp.int32, a.shape, 0)
    k = 1
    while k < n:
        if rev:
            sh = jnp.where(row < n - k, pltpu.roll(a, n - k, axis=0), NEG)
        else:
            sh = jnp.where(row >= k, pltpu.roll(a, k, axis=0), NEG)
        a = jnp.maximum(a, sh)
        k *= 2
    return a


ML_GL = ML_HEADS


def _mlstm_kernel(rev, qk_ref, v_ref, g_ref, gt_ref, gbr_ref, gbt_ref, o_ref, st_ref, m_ref):
    s, g = pl.program_id(0), pl.program_id(1)
    nbb = qk_ref.shape[0]
    b0 = g * nbb

    @pl.when(s == 0)
    def _():
        st_ref[pl.ds(b0, nbb)] = jnp.zeros((nbb,) + st_ref.shape[1:], F32)
        m_ref[pl.ds(b0, nbb)] = jnp.zeros((nbb,) + m_ref.shape[1:], F32)

    valid = _tri(rev)
    tri = valid.astype(F32)
    r0s = _chunk_rows(rev)
    cs = [r // CHUNK for r in r0s]
    last = 0 if rev else CHUNK - 1
    inst = [(bb, j) for bb in range(nbb) for j in range(N_SUB)]
    heads = [slice(h * LANE, (h + 1) * LANE) for h in range(ML_HEADS)]

    r_sel = lax.broadcasted_iota(jnp.int32, (LANE, HP), 0)
    c_sel = lax.broadcasted_iota(jnp.int32, (LANE, HP), 1)
    sel_h = (r_sel == ML_GL + c_sel // LANE).astype(BF16)
    r_t = lax.broadcasted_iota(jnp.int32, (HP, HP), 0)
    c_t = lax.broadcasted_iota(jnp.int32, (HP, HP), 1)
    same = jnp.logical_and(r_t // LANE == c_t // LANE, jnp.logical_and(r_t % LANE < CHUNK, c_t % LANE < CHUNK))
    before = (r_t % LANE >= c_t % LANE) if rev else (r_t % LANE <= c_t % LANE)
    tri_b = jnp.logical_and(same, before).astype(BF16)
    r_v = lax.broadcasted_iota(jnp.int32, (CHUNK, HP), 0)
    c_v = lax.broadcasted_iota(jnp.int32, (CHUNK, HP), 1) % LANE
    valid4 = jnp.logical_and(c_v < CHUNK, (r_v <= c_v) if rev else (r_v >= c_v))
    lane4 = lax.broadcasted_iota(jnp.int32, (CHUNK, HP), 1) % LANE
    lane_c = lax.broadcasted_iota(jnp.int32, (CHUNK, LANE), 1)
    lane1 = lax.broadcasted_iota(jnp.int32, (1, LANE), 1)
    head_lane = jnp.logical_and(lane1 >= ML_GL, lane1 < ML_GL + ML_HEADS)
    gbr, gbt = gbr_ref[...], gbt_ref[...]

    gcs, fcm, cmx, a_row, grt = {}, {}, {}, {}, {}
    for bb, j in inst:
        gc = g_ref[bb, pl.ds(r0s[j], CHUNK), :] + gbr
        gcs[bb, j] = pltpu.roll(gc, ML_GL, axis=1)
        fcm[bb, j] = _dot_exact01(tri, _log_sigmoid(gc), lhs_is_01=True)
        grt[bb, j] = gt_ref[bb, cs[j]] + gbt
    row_id = lax.broadcasted_iota(jnp.int32, (len(inst), HP), 0)
    lfr = jnp.zeros((len(inst), HP), F32)
    for n, i in enumerate(inst):
        lfr = jnp.where(row_id == n, _log_sigmoid(grt[i][1:2]), lfr)
    fcr = _dot_exact01(lfr, tri_b, lhs_is_01=False)
    for n, i in enumerate(inst):
        a_row[i] = grt[i][0:1] - fcr[n:n + 1]
        cmx[i] = _cummax_rows(gcs[i] - fcm[i], rev)
    bx = {}
    e_neg, gd = {}, {}
    for bb in range(nbb):
        m_prev = m_ref[b0 + bb, 0:1, :]
        for j in range(N_SUB):
            i = (bb, j)
            m_t = fcm[i] + jnp.maximum(m_prev, cmx[i])
            m_new = m_t[last:last + 1]
            f_tot = fcm[i][last:last + 1]
            u = fcm[i] - m_t
            w_prev = jnp.exp(u + m_prev)
            w_s = jnp.exp(f_tot - fcm[i] + gcs[i] - m_new)
            gdec = jnp.broadcast_to(jnp.exp(f_tot + m_prev - m_new), (16, LANE))
            e_neg[i] = jnp.exp(-m_t)
            keep = lambda a: jnp.where(head_lane, a, 0.0)
            bx[i] = jnp.concatenate(_split_bf16(keep(u), 2) + _split_bf16(keep(w_prev), 1)
                                    + _split_bf16(keep(w_s), 1) + _split_bf16(keep(gdec), 2), axis=0)
            m_prev = m_new
        m_ref[b0 + bb] = jnp.broadcast_to(m_prev, (8, LANE))
    ub, wpb, wsb, gdb = {}, {}, {}, {}
    for i in inst:
        y = _dot(bx[i], sel_h)
        c = CHUNK
        ub[i] = y[0:c] + y[c:2 * c]
        wpb[i] = y[2 * c:3 * c]
        wsb[i] = y[3 * c:4 * c]
        gdb[i] = y[4 * c:4 * c + 1] + y[4 * c + 16:4 * c + 17]
    q, v, qkw, ds = {}, {}, {}, {}
    for bb, j in inst:
        i = (bb, j)
        rs = pl.ds(r0s[j], CHUNK)
        w = jnp.where(valid4, jnp.exp(ub[i] + a_row[i]), 0.0)
        q[i] = qk_ref[bb, rs, 0:HP].astype(BF16)
        k = qk_ref[bb, rs, HP:2 * HP]
        kb = k.astype(BF16)
        kw = (k * wsb[i]).astype(BF16)
        v[i] = jnp.where(lane4 == ML_DH, 1.0, v_ref[bb, rs, :]).astype(BF16)
        for h, sl in enumerate(heads):
            sc = _dot_nt(q[i][:, sl], kb[:, sl])
            qkw[i, h] = (sc * w[:, h * LANE:h * LANE + CHUNK]).astype(BF16)
            ds[i, h] = _dot_tn(v[i][:, sl], kw[:, sl])
    s_in = {}
    for bb in range(nbb):
        for h, sl in enumerate(heads):
            st = st_ref[b0 + bb, h]
            for j in range(N_SUB):
                i = (bb, j)
                s_in[i, h] = st.astype(BF16)
                st = gdb[i][:, sl] * st + ds[i, h]
            st_ref[b0 + bb, h] = st
    num = {}
    for i in inst:
        parts = [_dot_nt(q[i][:, sl], s_in[i, h]) for h, sl in enumerate(heads)]
        intra = [_dot(qkw[i, h], v[i][:, sl]) for h, sl in enumerate(heads)]
        num[i] = wpb[i] * jnp.concatenate(parts, axis=1) + jnp.concatenate(intra, axis=1)
    for bb, j in inst:
        i = (bb, j)
        den = jnp.zeros((CHUNK, LANE), F32)
        for h, sl in enumerate(heads):
            dh = jnp.sum(jnp.where(lane_c == ML_DH, num[i][:, sl], 0.0), axis=-1, keepdims=True)
            den = jnp.where(lane_c == ML_GL + h, jnp.broadcast_to(dh, (CHUNK, LANE)), den)
        rb = _dot(jnp.where(head_lane, 1.0 / jnp.maximum(jnp.abs(den), e_neg[i]), 0.0).astype(BF16), sel_h)
        o_ref[bb, pl.ds(r0s[j], CHUNK), :] = jnp.where(lane4 < ML_DH, num[i] * rb, 0.0).astype(o_ref.dtype)


def _mlstm(qk3, pm3, gtl, gbr, gbt, *, rows, rev):
    nb, l, _ = qk3.shape
    nbb = SCAN_BATCHES
    dr = 1 if rev else 0
    pos = functools.partial(_scan_pos_static, rev, ncb=rows.ncb, ntb=rows.ntb)
    return pl.pallas_call(
        functools.partial(_mlstm_kernel, rev),
        out_shape=jax.ShapeDtypeStruct((nb, l, HP), BF16),
        grid=(rows.ntb, nb // nbb),
        in_specs=[pl.BlockSpec((nbb, SEG, 2 * HP), lambda s, g: (g, pos(s), 0)),
                  pl.BlockSpec((nbb, SEG, HP), lambda s, g: (g, pos(s), 0)),
                  pl.BlockSpec((nbb, SEG, LANE), lambda s, g: (g, pos(s), HP // LANE + dr)),
                  pl.BlockSpec((nbb, N_SUB, None, 2, HP), lambda s, g: (g, pos(s), dr, 0, 0)),
                  pl.BlockSpec((1, LANE), lambda s, g: (0, 0)),
                  pl.BlockSpec((2, HP), lambda s, g: (0, 0))],
        out_specs=pl.BlockSpec((nbb, SEG, HP), lambda s, g: (g, pos(s), 0)),
        scratch_shapes=[pltpu.VMEM((nb, ML_HEADS, LANE, LANE), F32), pltpu.VMEM((nb, 8, LANE), F32)],
        compiler_params=_cparams(("arbitrary", "arbitrary")),
        name="mlstm_scan_bwd" if rev else "mlstm_scan_fwd",
    )(qk3, pm3, pm3, gtl, gbr, gbt)


def _s5_kernel(nb, u_ref, bre_ref, bim_ref, are_ref, aim_ref, cre_ref, cim_ref, o_ref, xr_ref, xi_ref, st_ref):
    d, s = pl.program_id(0), pl.program_id(1)

    @pl.when(s == 0)
    def _():
        st_ref[...] = jnp.zeros(st_ref.shape, F32)

    n_sub = u_ref.shape[0] // (S5_SUB * nb)
    sub_rows = S5_SUB * nb

    def run(rev):
        ar, ai = are_ref[0], aim_ref[0]
        order = list(range(n_sub))[::-1] if rev else list(range(n_sub))

        def project_in(q):
            rs = slice(q * sub_rows, (q + 1) * sub_rows)
            u = u_ref[rs, :].astype(BF16)
            xr_ref[rs, :] = _dot(u, bre_ref[0])
            xi_ref[rs, :] = _dot(u, bim_ref[0])

        sr, si = st_ref[0], st_ref[1]
        project_in(order[0])
        for n, q in enumerate(order):
            if n + 1 < n_sub:
                project_in(order[n + 1])
            for j in range(S5_SUB):
                r0 = (q * S5_SUB + (S5_SUB - 1 - j if rev else j)) * nb
                nr = ar * sr - ai * si + xr_ref[r0:r0 + nb, :]
                ni = ar * si + ai * sr + xi_ref[r0:r0 + nb, :]
                xr_ref[r0:r0 + nb, :] = nr
                xi_ref[r0:r0 + nb, :] = ni
                sr, si = nr, ni
            rs = slice(q * sub_rows, (q + 1) * sub_rows)
            o_ref[0, rs, :] = (_dot(xr_ref[rs, :].astype(BF16), cre_ref[...])
                               - _dot(xi_ref[rs, :].astype(BF16), cim_ref[...])).astype(o_ref.dtype)
        st_ref[0] = sr
        st_ref[1] = si

    @pl.when(d == 0)
    def _():
        run(False)

    @pl.when(d == 1)
    def _():
        run(True)


def _s5(ut, bre, bim, are, aim, cre, cim, *, nb, nc, nt):
    n_rows, ch = ut.shape
    tr = S5_STEPS * nb
    ns = bre.shape[-1]
    pos = functools.partial(_scan_pos, ncb=nc, ntb=nt)
    return pl.pallas_call(
        functools.partial(_s5_kernel, nb),
        out_shape=jax.ShapeDtypeStruct((2, n_rows, ch), BF16),
        grid=(2, nt),
        in_specs=[pl.BlockSpec((tr, ch), lambda d, s: (pos(d, s), 0)),
                  pl.BlockSpec((1, ch, ns), lambda d, s: (d, 0, 0)),
                  pl.BlockSpec((1, ch, ns), lambda d, s: (d, 0, 0)),
                  pl.BlockSpec((1, nb, ns), lambda d, s: (d, 0, 0)),
                  pl.BlockSpec((1, nb, ns), lambda d, s: (d, 0, 0)),
                  pl.BlockSpec((ns, ch), lambda d, s: (0, 0)),
                  pl.BlockSpec((ns, ch), lambda d, s: (0, 0))],
        out_specs=pl.BlockSpec((1, tr, ch), lambda d, s: (d, pos(d, s), 0)),
        scratch_shapes=[pltpu.VMEM((tr, ns), F32), pltpu.VMEM((tr, ns), F32), pltpu.VMEM((2, nb, ns), F32)],
        compiler_params=_cparams(("arbitrary", "arbitrary")),
        name="s5_scan",
    )(ut, bre, bim, are, aim, cre, cim)


def _head_norm(o, gain, dim):
    parts = []
    for h in range(o.shape[1] // LANE):
        seg = o[:, h * LANE:(h + 1) * LANE]
        ms = jnp.sum(seg * seg, axis=-1, keepdims=True) * (1.0 / dim)
        parts.append(seg * lax.rsqrt(ms + EPS))
    return jnp.concatenate(parts, axis=1) * gain


def _mix_kernel(with_router, ogf_ref, ogb_ref, gg_ref, ys_ref, u_ref, omf_ref, omb_ref, mo_ref, h_ref, mod_ref,
                gn_ref, mn_ref, sd_ref, gw_ref, gb_ref, wo_ref, n2_ref, *rest):
    if with_router:
        wr_ref, ho_ref, f_ref, rt_ref = rest
    else:
        ho_ref, f_ref = rest
    gla = _head_norm(ogf_ref[...].astype(F32) + ogb_ref[...].astype(F32), gn_ref[...], GLA_DV) * _silu(gg_ref[...].astype(F32))
    z = _gelu_tanh(ys_ref[0].astype(F32) + ys_ref[1].astype(F32) + sd_ref[...] * u_ref[...])
    s5 = z * jax.nn.sigmoid(_dot(z.astype(BF16), gw_ref[...]) + gb_ref[...])
    ml = _head_norm(omf_ref[...].astype(F32) + omb_ref[...].astype(F32), mn_ref[...], ML_DH) * jax.nn.sigmoid(mo_ref[...].astype(F32))
    mix = (_dot(gla.astype(BF16), wo_ref[0:HP]) + _dot(s5.astype(BF16), wo_ref[HP:HP + 2 * LANE])
           + _dot(ml.astype(BF16), wo_ref[HP + 2 * LANE:]))
    m = mod_ref[0]
    hn = h_ref[...] + m[2:3] * mix
    ho_ref[...] = hn
    f = _rmsnorm(hn, n2_ref[...]) * (1.0 + m[4:5]) + m[3:4]
    if not with_router:
        f_ref[...] = f.astype(f_ref.dtype)
    if with_router:
        f_hi, f_lo = _split_bf16(f, 2)
        logits = (_dot(f_hi, wr_ref[0]) + _dot(f_lo, wr_ref[0]) + _dot(f_hi, wr_ref[1])).T[0:2 * N_EXPERTS]
        row = lax.broadcasted_iota(jnp.int32, logits.shape, 0)
        l0 = jnp.where(row < N_EXPERTS, logits, NEG)
        m1 = jnp.max(l0, axis=0, keepdims=True)
        i1 = jnp.min(jnp.where(l0 == m1, row, 2 * N_EXPERTS), axis=0, keepdims=True)
        l1 = jnp.where(row == i1, NEG, l0)
        m2 = jnp.max(l1, axis=0, keepdims=True)
        i2 = jnp.min(jnp.where(l1 == m2, row, 2 * N_EXPERTS), axis=0, keepdims=True)
        e = jnp.exp(m2 - m1)
        w1 = 1.0 / (1.0 + e)
        w2 = e / (1.0 + e)
        rt = jnp.where(row == 0, i1.astype(F32),
                       jnp.where(row == 1, i2.astype(F32),
                                 jnp.where(row == 2, w1, jnp.where(row == 3, w2, 0.0))))
        rt_ref[...] = rt
        d = f.shape[1]
        f_ref[:, 0:d // 2] = _pack_bf16_pairs(f)
        f_ref[:, d // 2:] = jnp.concatenate([rt, jnp.zeros((LANE - rt.shape[0], rt.shape[1]), F32)], axis=0).T


def _mix(ogf, ogb, pg, ys, pu, omf, omb, pm, h, modtab, gn, mn, sd, gw, gb, wo, n2, wr, *, layer, rows, lat_only):
    d = h.shape[1]
    tm = SEG
    n = rows.n_blocks(lat_only)
    src, sel = rows.src(lat_only), rows.sel(lat_only)
    full = lambda a: pl.BlockSpec(a.shape, lambda i: (0,) * a.ndim)
    with_router = wr is not None
    in_specs = [pl.BlockSpec((tm, HP), lambda i: (src(i), 0)),
                pl.BlockSpec((tm, HP), lambda i: (src(i), 0)),
                pl.BlockSpec((tm, HP), lambda i: (src(i), 0)),
                pl.BlockSpec((2, tm, 2 * LANE), lambda i: (0, src(i), 0)),
                pl.BlockSpec((tm, 2 * LANE), lambda i: (src(i), 0)),
                pl.BlockSpec((tm, HP), lambda i: (src(i), 0)),
                pl.BlockSpec((tm, HP), lambda i: (src(i), 0)),
                pl.BlockSpec((tm, HP), lambda i: (src(i), 1)),
                pl.BlockSpec((tm, d), lambda i: (src(i), 0)),
                pl.BlockSpec((None, 1, 8, d), lambda i: (layer, sel(i), 0, 0)),
                full(gn), full(mn), full(sd), full(gw), full(gb), full(wo), full(n2)]
    args = [ogf, ogb, pg, ys, pu, omf, omb, pm, h, modtab, gn, mn, sd, gw, gb, wo, n2]
    fw, fdt = (d // 2 + LANE, F32) if with_router else (d, BF16)
    out_shape = [jax.ShapeDtypeStruct((n * tm, d), F32), jax.ShapeDtypeStruct((n * tm, fw), fdt)]
    out_specs = [pl.BlockSpec((tm, d), lambda i: (i, 0)), pl.BlockSpec((tm, fw), lambda i: (i, 0))]
    if with_router:
        in_specs.append(full(wr))
        args.append(wr)
        out_shape.append(jax.ShapeDtypeStruct((2 * N_EXPERTS, n * tm), F32))
        out_specs.append(pl.BlockSpec((2 * N_EXPERTS, tm), lambda i: (0, i)))
    return pl.pallas_call(
        functools.partial(_mix_kernel, with_router),
        out_shape=tuple(out_shape),
        grid=(n,),
        in_specs=in_specs,
        out_specs=tuple(out_specs),
        compiler_params=_cparams(("arbitrary",)),
        name="mix_out",
    )(*args)


FF_TILE = 256


def _swiglu(xb, w1_ref, w3_ref, w2_ref, a_ref, lead=()):
    dff = w1_ref.shape[-1]
    for j in range(dff // FF_TILE):
        sl = slice(j * FF_TILE, (j + 1) * FF_TILE)
        h1 = _dot(xb, w1_ref[lead + (slice(None), sl)])
        h3 = _dot(xb, w3_ref[lead + (slice(None), sl)])
        a_ref[:, sl] = (_silu(h1) * h3).astype(BF16)
    return _dot(a_ref[...], w2_ref[lead + (slice(None), slice(None))])


FFN_TM = 1024
MIXFFN_TM = 2 * SEG


def _mix_ffn_kernel(sel, ogf_ref, ogb_ref, gg_ref, ys_ref, u_ref, omf_ref, omb_ref, mo_ref, h_ref, mod_ref,
                    gn_ref, mn_ref, sd_ref, gw_ref, gb_ref, wo_ref, n2_ref, w1_ref, w3_ref, w2_ref, o_ref,
                    a_ref, hn_ref, f_ref):
    n_seg = h_ref.shape[0] // SEG
    mods = []
    for q in range(n_seg):
        rs = slice(q * SEG, (q + 1) * SEG)
        m = mod_ref[sel(pl.program_id(0) * n_seg + q)]
        mods.append(m)
        gla = (_head_norm(ogf_ref[rs, :].astype(F32) + ogb_ref[rs, :].astype(F32), gn_ref[...], GLA_DV)
               * _silu(gg_ref[rs, :].astype(F32)))
        z = _gelu_tanh(ys_ref[0, rs, :].astype(F32) + ys_ref[1, rs, :].astype(F32) + sd_ref[...] * u_ref[rs, :])
        s5 = z * jax.nn.sigmoid(_dot(z.astype(BF16), gw_ref[...]) + gb_ref[...])
        ml = (_head_norm(omf_ref[rs, :].astype(F32) + omb_ref[rs, :].astype(F32), mn_ref[...], ML_DH)
              * jax.nn.sigmoid(mo_ref[rs, :].astype(F32)))
        mix = (_dot(gla.astype(BF16), wo_ref[0:HP]) + _dot(s5.astype(BF16), wo_ref[HP:HP + 2 * LANE])
               + _dot(ml.astype(BF16), wo_ref[HP + 2 * LANE:]))
        hn = h_ref[rs, :] + m[2:3] * mix
        hn_ref[rs, :] = hn
        f_ref[rs, :] = (_rmsnorm(hn, n2_ref[...]) * (1.0 + m[4:5]) + m[3:4]).astype(BF16)
    y = _swiglu(f_ref[...], w1_ref, w3_ref, w2_ref, a_ref)
    for q in range(n_seg):
        rs = slice(q * SEG, (q + 1) * SEG)
        o_ref[rs, :] = hn_ref[rs, :] + mods[q][5:6] * y[rs]


def _mix_ffn(ogf, ogb, pg, ys, pu, omf, omb, pm, h, modtab, gn, mn, sd, gw, gb, wo, n2, w1, w3, w2, *, layer, rows):
    r, d = h.shape
    tm = MIXFFN_TM
    assert r % tm == 0
    dff = w1.shape[-1]
    full = lambda a: pl.BlockSpec(a.shape, lambda i: (0,) * a.ndim)
    resident = lambda a: pl.BlockSpec(a.shape, lambda i: (0,) * a.ndim, pipeline_mode=pl.Buffered(1))
    in_specs = [pl.BlockSpec((tm, HP), lambda i: (i, 0)),
                pl.BlockSpec((tm, HP), lambda i: (i, 0)),
                pl.BlockSpec((tm, HP), lambda i: (i, 0)),
                pl.BlockSpec((2, tm, 2 * LANE), lambda i: (0, i, 0)),
                pl.BlockSpec((tm, 2 * LANE), lambda i: (i, 0)),
                pl.BlockSpec((tm, HP), lambda i: (i, 0)),
                pl.BlockSpec((tm, HP), lambda i: (i, 0)),
                pl.BlockSpec((tm, HP), lambda i: (i, 1)),
                pl.BlockSpec((tm, d), lambda i: (i, 0)),
                pl.BlockSpec((None,) + modtab.shape[1:], lambda i: (layer, 0, 0, 0)),
                full(gn), full(mn), full(sd), full(gw), full(gb), resident(wo), full(n2),
                resident(w1), resident(w3), resident(w2)]
    return pl.pallas_call(
        functools.partial(_mix_ffn_kernel, rows.sel(False)),
        out_shape=jax.ShapeDtypeStruct((r, d), F32),
        grid=(r // tm,),
        in_specs=in_specs,
        out_specs=pl.BlockSpec((tm, d), lambda i: (i, 0)),
        scratch_shapes=[pltpu.VMEM((tm, dff), BF16), pltpu.VMEM((tm, d), F32), pltpu.VMEM((tm, d), BF16)],
        compiler_params=_cparams(("arbitrary",)),
        name="mix_ffn",
    )(ogf, ogb, pg, ys, pu, omf, omb, pm, h, modtab, gn, mn, sd, gw, gb, wo, n2, w1, w3, w2)


def _ffn_kernel(final, sel, f_ref, h_ref, mod_ref, w1_ref, w3_ref, w2_ref, *rest):
    if final:
        nf_ref, o_ref, a_ref = rest
    else:
        o_ref, a_ref = rest
    y = _swiglu(f_ref[...], w1_ref, w3_ref, w2_ref, a_ref)
    n_seg = f_ref.shape[0] // SEG
    for q in range(n_seg):
        rs = slice(q * SEG, (q + 1) * SEG)
        gate = mod_ref[sel(pl.program_id(0) * n_seg + q)][5:6]
        hn = h_ref[rs, :] + gate * y[rs]
        o_ref[rs, :] = _rmsnorm(hn, nf_ref[...]) if final else hn


def _ffn(f, h, modtab, w1, w3, w2, nf, *, layer, rows, lat_only):
    r, d = h.shape
    tm = FFN_TM
    assert r % tm == 0
    dff = w1.shape[-1]
    full = lambda a: pl.BlockSpec(a.shape, lambda i: (0,) * a.ndim)
    resident = lambda a: pl.BlockSpec(a.shape, lambda i: (0,) * a.ndim, pipeline_mode=pl.Buffered(1))
    in_specs = [pl.BlockSpec((tm, d), lambda i: (i, 0)),
                pl.BlockSpec((tm, d), lambda i: (i, 0)),
                pl.BlockSpec((None,) + modtab.shape[1:], lambda i: (layer, 0, 0, 0)),
                resident(w1), resident(w3), resident(w2)]
    args = [f, h, modtab, w1, w3, w2]
    if lat_only:
        in_specs.append(full(nf))
        args.append(nf)
    return pl.pallas_call(
        functools.partial(_ffn_kernel, lat_only, rows.sel(lat_only)),
        out_shape=jax.ShapeDtypeStruct((r, d), F32),
        grid=(r // tm,),
        in_specs=in_specs,
        out_specs=pl.BlockSpec((tm, d), lambda i: (i, 0)),
        scratch_shapes=[pltpu.VMEM((tm, dff), BF16)],
        compiler_params=_cparams(("arbitrary",)),
        name="ffn",
    )(*args)


MOE_TM = 512


def _moe_kernel(be_ref, live_ref, x_ref, w1_ref, w3_ref, w2_ref, o_ref, a_ref):
    i = pl.program_id(0)
    d = w1_ref.shape[1]

    @pl.when(live_ref[i] > 0)
    def _():
        xa, xb = _unpack_bf16_pairs(x_ref[:, 0:d // 2])
        x = jnp.concatenate([xa.astype(BF16), xb.astype(BF16)], axis=1)
        y = _swiglu(x, w1_ref, w3_ref, w2_ref, a_ref, lead=(0,))
        tail = x_ref[:, d // 2:]
        mine = tail[:, 0:1] == be_ref[i].astype(F32)
        o_ref[...] = _pack_bf16_pairs(y * jnp.where(mine, tail[:, 2:3], tail[:, 3:4]))

    @pl.when(live_ref[i] == 0)
    def _():
        o_ref[...] = jnp.zeros(o_ref.shape, F32)


def _moe_experts(block_expert, block_live, xg, w1, w3, w2):
    n_rows, dx = xg.shape
    d = w1.shape[-2]
    dff = w1.shape[-1]
    tm = MOE_TM
    return pl.pallas_call(
        _moe_kernel,
        out_shape=jax.ShapeDtypeStruct((n_rows, d // 2), F32),
        grid_spec=pltpu.PrefetchScalarGridSpec(
            num_scalar_prefetch=2,
            grid=(n_rows // tm,),
            in_specs=[pl.BlockSpec((tm, dx), lambda i, be, lv: (i, 0)),
                      pl.BlockSpec((1, d, dff), lambda i, be, lv: (be[i], 0, 0)),
                      pl.BlockSpec((1, d, dff), lambda i, be, lv: (be[i], 0, 0)),
                      pl.BlockSpec((1, dff, d), lambda i, be, lv: (be[i], 0, 0))],
            out_specs=pl.BlockSpec((tm, d // 2), lambda i, be, lv: (i, 0)),
            scratch_shapes=[pltpu.VMEM((tm, dff), BF16)]),
        compiler_params=_cparams(("arbitrary",)),
        name="moe_experts",
    )(block_expert, block_live, xg, w1, w3, w2)


SC_GATHER_ROWS = 64


def _gather_rows(table, idx):
    n_k, n_idx = idx.shape
    _, d = table.shape
    info = plsc.get_sparse_core_info()
    n_cores, n_workers = info.num_cores, info.num_cores * info.num_subcores
    assert n_idx % (n_workers * SC_GATHER_ROWS) == 0
    per_worker = n_idx // n_workers
    mesh = plsc.VectorSubcoreMesh(core_axis_name="c", subcore_axis_name="s")

    @functools.partial(
        pl.kernel, mesh=mesh,
        out_type=jax.ShapeDtypeStruct((n_k * n_idx, d), table.dtype),
        scratch_types=[pltpu.VMEM((SC_GATHER_ROWS,), jnp.int32),
                       pltpu.VMEM((SC_GATHER_ROWS, d), table.dtype),
                       pltpu.SemaphoreType.DMA])
    def gather(table_hbm, idx_hbm, out_hbm, idx_v, rows_v, sem):
        base = (lax.axis_index("s") * n_cores + lax.axis_index("c")) * per_worker

        @pl.loop(0, per_worker // SC_GATHER_ROWS)
        def _(it):
            for k in range(n_k):
                off = pl.multiple_of(k * n_idx + base + it * SC_GATHER_ROWS, SC_GATHER_ROWS)
                pltpu.sync_copy(idx_hbm.at[pl.ds(off, SC_GATHER_ROWS)], idx_v)
                pltpu.async_copy(table_hbm.at[idx_v], rows_v, sem).wait()
                pltpu.sync_copy(rows_v, out_hbm.at[pl.ds(off, SC_GATHER_ROWS)])

    return gather(table, idx.reshape(-1)).reshape(n_k, n_idx, d)


def _scatter_rows(src, dest, n_rows):
    n_tok, d = src.shape
    info = plsc.get_sparse_core_info()
    n_cores, n_workers = info.num_cores, info.num_cores * info.num_subcores
    assert n_tok % (n_workers * SC_GATHER_ROWS) == 0
    per_worker = n_tok // n_workers
    mesh = plsc.VectorSubcoreMesh(core_axis_name="c", subcore_axis_name="s")

    @functools.partial(
        pl.kernel, mesh=mesh,
        out_type=jax.ShapeDtypeStruct((n_rows, d), src.dtype),
        scratch_types=[pltpu.VMEM((SC_GATHER_ROWS,), jnp.int32),
                       pltpu.VMEM((SC_GATHER_ROWS, d), src.dtype),
                       pltpu.SemaphoreType.DMA])
    def scatter(src_hbm, dest_hbm, out_hbm, idx_v, rows_v, sem):
        base = (lax.axis_index("s") * n_cores + lax.axis_index("c")) * per_worker

        @pl.loop(0, per_worker // SC_GATHER_ROWS)
        def _(it):
            off = pl.multiple_of(base + it * SC_GATHER_ROWS, SC_GATHER_ROWS)
            pltpu.sync_copy(src_hbm.at[pl.ds(off, SC_GATHER_ROWS)], rows_v)
            for k in range(TOP_K):
                pltpu.sync_copy(dest_hbm.at[pl.ds(k * n_tok + off, SC_GATHER_ROWS)], idx_v)
                pltpu.async_copy(rows_v, out_hbm.at[idx_v], sem).wait()

    return scatter(src, dest.reshape(-1))


def _pack_bf16_pairs(y):
    n = y.shape[1] // 2
    hi = pltpu.bitcast(y[:, :n].astype(BF16).astype(F32), jnp.uint32)
    lo = pltpu.bitcast(y[:, n:].astype(BF16).astype(F32), jnp.uint32)
    return pltpu.bitcast(hi | (lo >> 16), F32)


def _unpack_bf16_pairs(w):
    u = pltpu.bitcast(w, jnp.uint32)
    return pltpu.bitcast(u & jnp.uint32(0xFFFF0000), F32), pltpu.bitcast(u << 16, F32)


def _resid_kernel(final, h_ref, y_ref, mod_ref, *rest):
    a0, b0 = _unpack_bf16_pairs(y_ref[0])
    a1, b1 = _unpack_bf16_pairs(y_ref[1])
    hn = h_ref[...] + mod_ref[0][5:6] * jnp.concatenate([a0 + a1, b0 + b1], axis=1)
    if final:
        nf_ref, o_ref = rest
        o_ref[...] = _rmsnorm(hn, nf_ref[...])
    else:
        (o_ref,) = rest
        o_ref[...] = hn


def _moe_resid(h, y01, modtab, nf, *, layer, rows, lat_only):
    r, d = h.shape
    tm = SEG
    sel = rows.sel(lat_only)
    row = lambda i: (i, 0)
    in_specs = [pl.BlockSpec((tm, d), row), pl.BlockSpec((TOP_K, tm, d // 2), lambda i: (0, i, 0)),
                pl.BlockSpec((None, 1, 8, d), lambda i: (layer, sel(i), 0, 0))]
    args = [h, y01, modtab]
    if lat_only:
        in_specs.append(pl.BlockSpec((1, d), lambda i: (0, 0)))
        args.append(nf)
    return pl.pallas_call(
        functools.partial(_resid_kernel, lat_only),
        out_shape=jax.ShapeDtypeStruct((r, d), F32),
        grid=(r // tm,),
        in_specs=in_specs,
        out_specs=pl.BlockSpec((tm, d), row),
        compiler_params=_cparams(("arbitrary",)),
        name="moe_resid",
    )(*args)


CAST_PARTS = 16


def _cast_kernel(*refs):
    o_ref = refs[-1]
    tr = refs[0].shape[1]
    for k, x_ref in enumerate(refs[:-1]):
        o_ref[0, k * tr:(k + 1) * tr, :] = x_ref[0].astype(o_ref.dtype)


def _to_bf16(w, j):
    lead, (r, c) = w.shape[1:-2], w.shape[-2:]
    n = int(np.prod(lead, dtype=np.int64))
    w3 = w.reshape((-1, r, c))
    tr = r // CAST_PARTS
    assert tr * CAST_PARTS == r and tr % 16 == 0
    band = lambda k: pl.BlockSpec((1, tr, c), lambda e: (j * n + e, k, 0))
    out = pl.pallas_call(
        _cast_kernel,
        out_shape=jax.ShapeDtypeStruct((n, r, c), BF16),
        grid=(n,),
        in_specs=[band(k) for k in range(CAST_PARTS)],
        out_specs=pl.BlockSpec((1, r, c), lambda e: (e, 0, 0)),
        compiler_params=_cparams(("arbitrary",)),
        name="to_bf16",
    )(*([w3] * CAST_PARTS))
    return out.reshape(lead + (r, c))


def _pad_heads(w, heads, dim, to=LANE):
    lead = w.shape[:-1]
    w = w.reshape(lead + (heads, dim))
    w = jnp.pad(w, [(0, 0)] * len(lead) + [(0, 0), (0, to - dim)])
    return w.reshape(lead + (heads * to,))


def _pad_last(w, to):
    return jnp.pad(w, [(0, 0)] * (w.ndim - 1) + [(0, to - w.shape[-1])])


def _pos_embed(n_tokens, d):
    n_grid_rows = n_tokens // GRID_W
    row, col = jnp.meshgrid(jnp.arange(n_grid_rows, dtype=F32), jnp.arange(GRID_W, dtype=F32), indexing='ij')
    n_freq = d // 4
    omega = jnp.exp(-math.log(POS_BASE) * jnp.arange(n_freq, dtype=F32) / n_freq)

    def axis_embed(p):
        ang = p.reshape(-1, 1) * omega
        return jnp.concatenate([jnp.sin(ang), jnp.cos(ang)], axis=-1)

    return jnp.concatenate([axis_embed(row), axis_embed(col)], axis=-1)


def _s5_discretise(lam_re, lam_im, log_dt, b_re, b_im):
    dt = jnp.exp(log_dt)[:, None]
    mag = jnp.exp(lam_re * dt)
    abar_re, abar_im = mag * jnp.cos(lam_im * dt), mag * jnp.sin(lam_im * dt)
    den = lam_re * lam_re + lam_im * lam_im
    pr, pi = abar_re - 1.0, abar_im
    coef_re = (pr * lam_re + pi * lam_im) / den
    coef_im = (pi * lam_re - pr * lam_im) / den
    bbar_re = coef_re[..., None] * b_re - coef_im[..., None] * b_im
    bbar_im = coef_re[..., None] * b_im + coef_im[..., None] * b_re
    return abar_re, abar_im, bbar_re, bbar_im


def _block_diag(m):
    g, a, b = m.shape
    eye = jnp.eye(g, dtype=m.dtype)
    return (eye[:, None, :, None] * m[:, :, None, :]).reshape(g * a, g * b)


def _route_plan(route, tm):
    n_tok = route.shape[1]
    n_assign = n_tok * TOP_K
    flat_e = route[0:TOP_K].astype(jnp.int32).reshape(-1)
    onehot = (jnp.arange(N_EXPERTS, dtype=jnp.int32)[:, None] == flat_e[None, :]).astype(jnp.int32)
    csum = jnp.cumsum(onehot, axis=1)
    counts = csum[:, -1]
    padded = (counts + tm - 1) // tm * tm
    pend = jnp.cumsum(padded)
    pstart = pend - padded
    dest = jnp.sum(onehot * (csum - 1 + pstart[:, None]), axis=0)
    n_blocks = -(-n_assign // tm) + N_EXPERTS
    block_start = jnp.arange(n_blocks, dtype=jnp.int32) * tm
    block_expert = jnp.minimum(jnp.searchsorted(pend, block_start, side='right'), N_EXPERTS - 1).astype(jnp.int32)
    block_live = (block_start < (pstart + counts)[block_expert]).astype(jnp.int32)
    return n_blocks * tm, block_expert, block_live, dest.reshape(TOP_K, n_tok)


def kernel(x, c, ctx, c_ctx, w_ada, b_ada, norm1, norm2, w_in, w_out, gla_wa2, gla_ba, gla_norm, s5_lam_re, s5_lam_im, s5_log_dt, s5_b_re, s5_b_im, s5_c_re, s5_c_im, s5_d, s5_glu_w, s5_glu_b, ml_conv_w, ml_conv_b, ml_gate_b, ml_norm, ffn_w1, ffn_w3, ffn_w2, moe_router, moe_w1, moe_w3, moe_w2, norm_f):
    nb, n_lat, d = x.shape
    lc = ctx.shape[1]
    depth = w_ada.shape[0]
    l = lc + n_lat
    assert lc % SEG == 0 and n_lat % SEG == 0 and nb == 8 and nb % SCAN_BATCHES == 0
    rows = _Rows(nb, lc // SEG, l // SEG)

    h = _embed(ctx.reshape(nb * lc, d), x.reshape(nb * n_lat, d), _pos_embed(n_lat, d), rows)

    cond = jnp.zeros((16, d), F32).at[:nb].set(c).at[nb].set(c_ctx)
    mod = _modulation(cond, w_ada, b_ada)
    modtab = jnp.pad(mod.reshape(depth, 16, 6, d), ((0, 0), (0, 0), (0, 2), (0, 0)))

    dk, dv, dh = GLA_HEADS * GLA_DK, GLA_HEADS * GLA_DV, ML_HEADS * ML_DH
    s5c = s5_d.shape[-1]
    cuts = np.cumsum([dk, dk, dv, GLA_RANK, dv, s5c, dh, dh, dh, dh, 4 * ML_HEADS])

    for i in range(depth):
        last = i == depth - 1
        gq, gk, gv, glr, gg, su, mq, mk, mv, mo, mg = jnp.split(w_in[i], cuts[:-1], axis=-1)
        w_all = jnp.concatenate([
            _pad_heads(gq, GLA_HEADS, GLA_DK, GLA_KP), _pad_heads(gk, GLA_HEADS, GLA_DK, GLA_KP),
            _pad_heads(gv, GLA_HEADS, GLA_DV), _pad_last(glr, LANE),
            su,
            _pad_heads(mv, ML_HEADS, ML_DH),
            _pad_last(mg[:, :2 * ML_HEADS], LANE), _pad_last(mg[:, 2 * ML_HEADS:], LANE),
            _pad_heads(mq, ML_HEADS, ML_DH), _pad_heads(mk, ML_HEADS, ML_DH),
            _pad_heads(gg, GLA_HEADS, GLA_DV), _pad_heads(mo, ML_HEADS, ML_DH)],
            axis=-1).astype(BF16)
        wgt = mg.T.astype(BF16)
        cw = jnp.concatenate([_pad_heads(ml_conv_w[i][:, :dh], ML_HEADS, ML_DH),
                              _pad_heads(ml_conv_w[i][:, dh:], ML_HEADS, ML_DH)], axis=-1)
        cw = jnp.pad(cw, ((0, 8 - ML_CONV), (0, 0)))
        cb = jnp.concatenate([_pad_heads(ml_conv_b[i][:dh], ML_HEADS, ML_DH),
                              _pad_heads(ml_conv_b[i][dh:], ML_HEADS, ML_DH)])[None]
        post = jnp.concatenate([jnp.ones((HP,), F32), jnp.full((HP,), ML_DH ** -0.5, F32)])[None]
        pg, pu, pm, qk, go, gt = _proj(h, modtab, norm1[i][None], w_all, wgt, cw, cb, post, layer=i, rows=rows)

        wa = jnp.pad(_pad_heads(gla_wa2[i], GLA_HEADS, GLA_DK, GLA_KP), ((0, 0), (0, LANE - GLA_RANK), (0, 0)))
        ba = _pad_heads(gla_ba[i], GLA_HEADS, GLA_DK, GLA_KP)[:, None, :]
        pg3 = pg.reshape(nb, l, NG)
        ogf = _gla(pg3, wa[0], ba[0], rows=rows, rev=False).reshape(nb * l, HP)
        ogb = _gla(pg3, wa[1], ba[1], rows=rows, rev=True).reshape(nb * l, HP)

        bres, bims, ares, aims = [], [], [], []
        for dr in (0, 1):
            a_re, a_im, b_re, b_im = _s5_discretise(s5_lam_re[i, dr], s5_lam_im[i, dr], s5_log_dt[i, dr],
                                                    s5_b_re[i], s5_b_im[i])
            bres.append(_block_diag(jnp.swapaxes(b_re, 1, 2)))
            bims.append(_block_diag(jnp.swapaxes(b_im, 1, 2)))
            ares.append(jnp.broadcast_to(a_re.reshape(1, -1), (nb, a_re.size)))
            aims.append(jnp.broadcast_to(a_im.reshape(1, -1), (nb, a_im.size)))
        cre = _block_diag(jnp.swapaxes(s5_c_re[i], 1, 2)).astype(BF16)
        cim = _block_diag(jnp.swapaxes(s5_c_im[i], 1, 2)).astype(BF16)
        ut = pu.reshape(nb, l, s5c).swapaxes(0, 1).reshape(l * nb, s5c)
        yt = _s5(ut, jnp.stack(bres).astype(BF16), jnp.stack(bims).astype(BF16), jnp.stack(ares), jnp.stack(aims),
                 cre, cim, nb=nb, nc=lc // S5_STEPS, nt=l // S5_STEPS)
        ys = yt.reshape(2, l, nb, s5c).swapaxes(1, 2).reshape(2, nb * l, s5c)

        gb = ml_gate_b[i].reshape(2, 2, ML_HEADS)
        gbr = _pad_last(gb.reshape(2, 1, 2 * ML_HEADS), LANE)
        gbt = _pad_last(jnp.broadcast_to(gb[..., None], (2, 2, ML_HEADS, CHUNK)), LANE).reshape(2, 2, HP)
        gtl = _pad_last(gt.reshape(2, 2, ML_HEADS, nb, l // CHUNK, CHUNK).transpose(3, 4, 0, 1, 2, 5),
                        LANE).reshape(nb, l // CHUNK, 2, 2, HP)
        qk3, pm3 = qk.reshape(nb, l, 2 * HP), pm.reshape(nb, l, NM)
        omf = _mlstm(qk3, pm3, gtl, gbr[0], gbt[0], rows=rows, rev=False).reshape(nb * l, HP)
        omb = _mlstm(qk3, pm3, gtl, gbr[1], gbt[1], rows=rows, rev=True).reshape(nb * l, HP)

        wo = w_out[i]
        wo_p = jnp.concatenate([
            jnp.pad(wo[:dv].reshape(GLA_HEADS, GLA_DV, d), ((0, 0), (0, LANE - GLA_DV), (0, 0))).reshape(HP, d),
            wo[dv:dv + s5c],
            jnp.pad(wo[dv + s5c:].reshape(ML_HEADS, ML_DH, d), ((0, 0), (0, LANE - ML_DH), (0, 0))).reshape(HP, d)],
            axis=0).astype(BF16)
        gn = jnp.tile(_pad_last(gla_norm[i], LANE), GLA_HEADS)[None]
        mn = jnp.tile(_pad_last(ml_norm[i], LANE), ML_HEADS)[None]
        is_moe = i % 2 == 1
        j = i // 2
        wr = jnp.stack(_split_bf16(_pad_last(moe_router[j], LANE), 2)) if is_moe else None
        mix_args = (ogf, ogb, go, ys, pu, omf, omb, go, h, modtab, gn, mn, s5_d[i][None], s5_glu_w[i].astype(BF16),
                    s5_glu_b[i][None], wo_p, norm2[i][None])
        if not is_moe and not last:
            h = _mix_ffn(*mix_args, _to_bf16(ffn_w1, j), _to_bf16(ffn_w3, j), _to_bf16(ffn_w2, j),
                         layer=i, rows=rows)
            continue
        outs = _mix(*mix_args, wr, layer=i, rows=rows, lat_only=last)
        if not is_moe:
            h, f = outs
            h = _ffn(f, h, modtab, _to_bf16(ffn_w1, j), _to_bf16(ffn_w3, j), _to_bf16(ffn_w2, j),
                     norm_f[None], layer=i, rows=rows, lat_only=last)
        else:
            h, f, route = outs
            n_rows, block_expert, block_live, dest = _route_plan(route, MOE_TM)
            xg = _scatter_rows(f, dest, n_rows)
            yg = _moe_experts(block_expert, block_live, xg, _to_bf16(moe_w1, j), _to_bf16(moe_w3, j),
                              _to_bf16(moe_w2, j))
            y01 = _gather_rows(yg, dest)
            h = _moe_resid(h, y01, modtab, norm_f[None], layer=i, rows=rows, lat_only=last)
    return h.reshape(nb, n_lat, d)
```
